```python
import jax, jax.numpy as jnp
from jax import lax
import numpy as np

D_MODEL = 1024
BATCH = 16
SEQ = 256
DEPTH = 2
DEC_BATCH = 2
DEC_SEQ = 4096
PAST_LEN = 512

GRID_W = 64
N_ADA = 6
D_FF = 4 * D_MODEL
EPS = 1e-6
NEG_INF = -1e30
Q_BLOCK = 128
ROPE_THETA = 10000.0
POOL_WIDTH = D_MODEL // 2
POOL_WINDOWS = (2, 4, 8, 16)
N_POOL_GROUPS = len(POOL_WINDOWS)
POOL_GROUP = POOL_WIDTH // N_POOL_GROUPS
MLA_HEADS = 8
MLA_NOPE = 64
MLA_ROPE = 32
MLA_QK = MLA_NOPE + MLA_ROPE
MLA_V = 64
Q_LORA = 256
KV_LORA = 128
SPLIT0 = (POOL_WIDTH, POOL_WIDTH + Q_LORA, POOL_WIDTH + Q_LORA + KV_LORA)
IN0_WIDTH = POOL_WIDTH + Q_LORA + KV_LORA + MLA_ROPE
OUT0_WIDTH = POOL_WIDTH + MLA_HEADS * MLA_V
NA_HEADS = 16
NA_HEAD_DIM = D_MODEL // NA_HEADS
NA_KH = 8
NA_KW = 16

kernel_name = "hybrid_pool_mla_natten_diffusion_step"


def rms_norm(x, w):
    xf = x.astype(jnp.float32)
    y = xf * lax.rsqrt(jnp.mean(xf * xf, axis=-1, keepdims=True) + EPS)
    return (y * w.astype(jnp.float32)).astype(x.dtype)


def ada_modulation(cond, w_ada, b_ada):
    m = jax.nn.silu(cond) @ w_ada + b_ada
    return jnp.split(m[:, None, :], N_ADA, axis=-1)


def modulate(x, norm_w, shift, scale):
    return rms_norm(x, norm_w) * (1 + scale) + shift


def channel_mixer(h, w1, w2):
    return jnp.square(jax.nn.relu(h @ w1)) @ w2


def axial_rope(x):
    S = x.shape[1]
    t = jnp.arange(S)
    half = x.shape[-1] // 2
    inv_freq = jnp.power(ROPE_THETA, -jnp.arange(0, half, 2, dtype=jnp.float32) / half)
    xf = x.astype(jnp.float32)

    def rotate(xa, pos):
        ang = pos.astype(jnp.float32)[:, None] * inv_freq[None, :]
        cos = jnp.cos(ang)[None, :, None, :]
        sin = jnp.sin(ang)[None, :, None, :]
        x1, x2 = xa[..., : half // 2], xa[..., half // 2:]
        return jnp.concatenate([x1 * cos - x2 * sin, x1 * sin + x2 * cos], axis=-1)

    y = jnp.concatenate([rotate(xf[..., :half], t // GRID_W), rotate(xf[..., half:], t % GRID_W)], axis=-1)
    return y.astype(x.dtype)


def blocked_attention(q, k, v, scale):
    B, S, H, Dq = q.shape
    nblk = S // Q_BLOCK
    qb = jnp.moveaxis(q.reshape(B, nblk, Q_BLOCK, H, Dq), 1, 0)

    def one_block(qblk):
        s = jnp.einsum('bqhd,bkhd->bhqk', qblk, k, preferred_element_type=jnp.float32) * scale
        p = jax.nn.softmax(s, axis=-1).astype(v.dtype)
        return jnp.einsum('bhqk,bkhd->bqhd', p, v)

    out = lax.map(one_block, qb)
    return jnp.moveaxis(out, 0, 1).reshape(B, S, H, v.shape[-1])


def pool_mixer(u, w_pool, pool_scale):
    B, S, _ = u.shape
    ug = u.astype(jnp.float32).reshape(B, S, N_POOL_GROUPS, POOL_GROUP)
    cs = jnp.concatenate([jnp.zeros((B, 1, N_POOL_GROUPS, POOL_GROUP), jnp.float32), jnp.cumsum(ug, axis=1)], axis=1)
    t = jnp.arange(S)
    outs = []
    for g, win in enumerate(POOL_WINDOWS):
        lo = jnp.clip(t - win // 2, 0, S)
        hi = jnp.clip(t - win // 2 + win, 0, S)
        csg = cs[:, :, g]
        mean = (csg[:, hi] - csg[:, lo]) / (hi - lo).astype(jnp.float32)[None, :, None]
        outs.append(mean - ug[:, :, g])
    pooled = jnp.stack(outs, axis=2)
    mixed = jnp.einsum('bsgc,gcd->bsgd', pooled, w_pool.astype(jnp.float32))
    return (mixed.reshape(B, S, POOL_WIDTH) * pool_scale.astype(jnp.float32)).astype(u.dtype)


def mla_split(h, w_in, q_lora_norm, kv_lora_norm, w_q_up, mla_q_norm):
    B, S, _ = h.shape
    a_in, q_lat, kv_lat, k_rope = jnp.split(h @ w_in, SPLIT0, axis=-1)
    q = (rms_norm(q_lat, q_lora_norm) @ w_q_up).reshape(B, S, MLA_HEADS, MLA_QK)
    q = rms_norm(q, mla_q_norm)
    ckv = rms_norm(kv_lat, kv_lora_norm)
    return a_in, q, ckv, k_rope


def mla_keys_values(ckv, k_rope, w_kv_up, mla_k_norm):
    B, L, _ = ckv.shape
    kv = (ckv @ w_kv_up).reshape(B, L, MLA_HEADS, MLA_NOPE + MLA_V)
    k_nope, v = kv[..., :MLA_NOPE], kv[..., MLA_NOPE:]
    k = jnp.concatenate([k_nope, jnp.broadcast_to(k_rope[:, :, None, :], (B, L, MLA_HEADS, MLA_ROPE))], axis=-1)
    return rms_norm(k, mla_k_norm), v


def rope_tail(x):
    return jnp.concatenate([x[..., :MLA_NOPE], axial_rope(x[..., MLA_NOPE:])], axis=-1)


def pool_mla_context(h, w_in, q_lora_norm, kv_lora_norm, w_q_up, w_kv_up, mla_q_norm, mla_k_norm,
                     w_pool, pool_scale, w_out):
    B, S, _ = h.shape
    a_in, q, ckv, k_rope = mla_split(h, w_in, q_lora_norm, kv_lora_norm, w_q_up, mla_q_norm)
    k, v = mla_keys_values(ckv, k_rope, w_kv_up, mla_k_norm)
    attn = blocked_attention(q, k, v, MLA_QK ** -0.5).reshape(B, S, MLA_HEADS * MLA_V)
    out = jnp.concatenate([pool_mixer(a_in, w_pool, pool_scale), attn], axis=-1) @ w_out
    return out, ckv, k_rope


def pool_mla_latent(h, ckv_ctx, krope_ctx, w_in, q_lora_norm, kv_lora_norm, w_q_up, w_kv_up, mla_q_norm,
                    mla_k_norm, w_pool, pool_scale, w_out):
    B, S, _ = h.shape
    a_in, q, ckv, k_rope = mla_split(h, w_in, q_lora_norm, kv_lora_norm, w_q_up, mla_q_norm)
    k_lat, v_lat = mla_keys_values(ckv, k_rope, w_kv_up, mla_k_norm)
    k_ctx, v_ctx = mla_keys_values(ckv_ctx, krope_ctx, w_kv_up, mla_k_norm)
    q = rope_tail(q)
    k = jnp.concatenate([rope_tail(k_lat), k_ctx], axis=1)
    v = jnp.concatenate([v_lat, v_ctx], axis=1)
    attn = blocked_attention(q, k, v, MLA_QK ** -0.5).reshape(B, S, MLA_HEADS * MLA_V)
    return jnp.concatenate([pool_mixer(a_in, w_pool, pool_scale), attn], axis=-1) @ w_out


def na_qkv(h, w_in, na_q_norm, na_k_norm):
    B, S, _ = h.shape
    q, k, v = jnp.split(h @ w_in, 3, axis=-1)
    shp = (B, S, NA_HEADS, NA_HEAD_DIM)
    return rms_norm(q.reshape(shp), na_q_norm), rms_norm(k.reshape(shp), na_k_norm), v.reshape(shp)


def neighbourhood_attention(q, k, v, k_ctx, v_ctx, rel_bias):
    B, S, H, D = q.shape
    rows = S // GRID_W
    kh = min(NA_KH, rows)
    kw = NA_KW
    scale = D ** -0.5
    qg = q.reshape(B, rows, GRID_W, H, D)
    kg = k.reshape(B, rows, GRID_W, H, D)
    vg = v.reshape(B, rows, GRID_W, H, D)
    qcol = jnp.arange(GRID_W)
    kcol = jnp.arange(GRID_W)
    col_start = jnp.clip(qcol - kw // 2, 0, GRID_W - kw)
    col_mask = (kcol[None, :] >= col_start[:, None]) & (kcol[None, :] < col_start[:, None] + kw)
    dc_idx = jnp.clip(kcol[None, :] - qcol[:, None] + NA_KW - 1, 0, 2 * NA_KW - 2)
    bias_cols = rel_bias.astype(jnp.float32)[:, :, dc_idx]

    def one_row(r):
        rs = jnp.clip(r - kh // 2, 0, rows - kh)
        kb = lax.dynamic_slice_in_dim(kg, rs, kh, axis=1)
        vb = lax.dynamic_slice_in_dim(vg, rs, kh, axis=1)
        qr = lax.dynamic_index_in_dim(qg, r, axis=1, keepdims=False)
        dr_idx = rs + jnp.arange(kh) - r + NA_KH - 1
        bias = jnp.take(bias_cols, dr_idx, axis=1)
        bias = jnp.where(col_mask[None, None], bias, NEG_INF).transpose(0, 2, 1, 3)
        s_loc = jnp.einsum('bqhd,bikhd->bhqik', qr, kb, preferred_element_type=jnp.float32) * scale + bias[None]
        s_ctx = jnp.einsum('bqhd,blhd->bhql', qr, k_ctx, preferred_element_type=jnp.float32) * scale
        s = jnp.concatenate([s_loc.reshape(B, H, GRID_W, kh * GRID_W), s_ctx], axis=-1)
        p = jax.nn.softmax(s, axis=-1).astype(v.dtype)
        p_loc = p[..., : kh * GRID_W].reshape(B, H, GRID_W, kh, GRID_W)
        p_ctx = p[..., kh * GRID_W:]
        return (jnp.einsum('bhqik,bikhd->bqhd', p_loc, vb)
                + jnp.einsum('bhql,blhd->bqhd', p_ctx, v_ctx))

    out = lax.map(one_row, jnp.arange(rows))
    return jnp.moveaxis(out, 0, 1).reshape(B, S, H, D)


def na_context(h, w_in, na_q_norm, na_k_norm, w_out):
    B, S, _ = h.shape
    q, k, v = na_qkv(h, w_in, na_q_norm, na_k_norm)
    attn = blocked_attention(q, k, v, NA_HEAD_DIM ** -0.5).reshape(B, S, D_MODEL)
    return attn @ w_out, k, v


def na_latent(h, k_ctx, v_ctx, w_in, na_q_norm, na_k_norm, rel_bias, w_out):
    B, S, _ = h.shape
    q, k, v = na_qkv(h, w_in, na_q_norm, na_k_norm)
    attn = neighbourhood_attention(q, k, v, k_ctx, v_ctx, rel_bias).reshape(B, S, D_MODEL)
    return attn @ w_out


def setup_inputs(seed: int = 0) -> dict:
    key = jax.random.key(seed)
    keys = jax.random.split(key, 48)
    counter = [0]

    def nk():
        counter[0] += 1
        return keys[counter[0] - 1]

    def nrm(shape, s=1.0):
        return s * jax.random.normal(nk(), shape, jnp.float32)

    def dense(shape, fan_in, s=1.0):
        return nrm(shape, s * fan_in ** -0.5)

    def gain(shape):
        return 1.0 + nrm(shape, 0.05)

    return {
        "x_prompt": nrm((BATCH, SEQ, D_MODEL)),
        "x_sample": nrm((DEC_BATCH, DEC_SEQ, D_MODEL)),
        "cache_l0_mla_ckv": nrm((DEC_BATCH, PAST_LEN, KV_LORA)),
        "cache_l0_mla_krope": nrm((DEC_BATCH, PAST_LEN, MLA_ROPE)),
        "cache_l1_na_k": nrm((DEC_BATCH, PAST_LEN, NA_HEADS, NA_HEAD_DIM)),
        "cache_l1_na_v": nrm((DEC_BATCH, PAST_LEN, NA_HEADS, NA_HEAD_DIM)),
        "c": nrm((DEC_BATCH, D_MODEL)),
        "c_ctx": nrm((D_MODEL,)),
        "w_ada_l0": dense((D_MODEL, N_ADA * D_MODEL), D_MODEL, 0.5),
        "b_ada_l0": nrm((N_ADA * D_MODEL,), 0.02),
        "norm_mix_l0": gain((D_MODEL,)),
        "norm_mlp_l0": gain((D_MODEL,)),
        "w_mlp1_l0": dense((D_MODEL, D_FF), D_MODEL),
        "w_mlp2_l0": dense((D_FF, D_MODEL), D_FF),
        "w_in_l0": dense((D_MODEL, IN0_WIDTH), D_MODEL),
        "q_lora_norm_l0": gain((Q_LORA,)),
        "kv_lora_norm_l0": gain((KV_LORA,)),
        "w_q_up_l0": dense((Q_LORA, MLA_HEADS * MLA_QK), Q_LORA),
        "w_kv_up_l0": dense((KV_LORA, MLA_HEADS * (MLA_NOPE + MLA_V)), KV_LORA),
        "mla_q_norm_l0": gain((MLA_QK,)),
        "mla_k_norm_l0": gain((MLA_QK,)),
        "w_pool_l0": dense((N_POOL_GROUPS, POOL_GROUP, POOL_GROUP), POOL_GROUP),
        "pool_scale_l0": gain((POOL_WIDTH,)),
        "w_out_l0": dense((OUT0_WIDTH, D_MODEL), OUT0_WIDTH),
        "w_ada_l1": dense((D_MODEL, N_ADA * D_MODEL), D_MODEL, 0.5),
        "b_ada_l1": nrm((N_ADA * D_MODEL,), 0.02),
        "norm_mix_l1": gain((D_MODEL,)),
        "norm_mlp_l1": gain((D_MODEL,)),
        "w_mlp1_l1": dense((D_MODEL, D_FF), D_MODEL),
        "w_mlp2_l1": dense((D_FF, D_MODEL), D_FF),
        "w_in_l1": dense((D_MODEL, 3 * D_MODEL), D_MODEL),
        "na_q_norm_l1": gain((NA_HEAD_DIM,)),
        "na_k_norm_l1": gain((NA_HEAD_DIM,)),
        "rel_bias_l1": nrm((NA_HEADS, 2 * NA_KH - 1, 2 * NA_KW - 1), 0.2),
        "w_out_l1": dense((D_MODEL, D_MODEL), D_MODEL),
    }


def reference(x_prompt, x_sample, cache_l0_mla_ckv, cache_l0_mla_krope, cache_l1_na_k, cache_l1_na_v, c, c_ctx,
              w_ada_l0, b_ada_l0, norm_mix_l0, norm_mlp_l0, w_mlp1_l0, w_mlp2_l0,
              w_in_l0, q_lora_norm_l0, kv_lora_norm_l0, w_q_up_l0, w_kv_up_l0, mla_q_norm_l0, mla_k_norm_l0,
              w_pool_l0, pool_scale_l0, w_out_l0,
              w_ada_l1, b_ada_l1, norm_mix_l1, norm_mlp_l1, w_mlp1_l1, w_mlp2_l1,
              w_in_l1, na_q_norm_l1, na_k_norm_l1, rel_bias_l1, w_out_l1):
    ada_w = (w_ada_l0, w_ada_l1)
    ada_b = (b_ada_l0, b_ada_l1)
    norm_mix = (norm_mix_l0, norm_mix_l1)
    norm_mlp = (norm_mlp_l0, norm_mlp_l1)
    mlp1 = (w_mlp1_l0, w_mlp1_l1)
    mlp2 = (w_mlp2_l0, w_mlp2_l1)
    p0 = (w_in_l0, q_lora_norm_l0, kv_lora_norm_l0, w_q_up_l0, w_kv_up_l0, mla_q_norm_l0, mla_k_norm_l0,
          w_pool_l0, pool_scale_l0, w_out_l0)

    xp = x_prompt
    xs = x_sample
    new_state = []
    for i in range(DEPTH):
        sh1_p, sc1_p, g1_p, sh2_p, sc2_p, g2_p = ada_modulation(c_ctx[None, :], ada_w[i], ada_b[i])
        sh1_s, sc1_s, g1_s, sh2_s, sc2_s, g2_s = ada_modulation(c, ada_w[i], ada_b[i])
        hp = modulate(xp, norm_mix[i], sh1_p, sc1_p)
        hs = modulate(xs, norm_mix[i], sh1_s, sc1_s)
        if i % 2 == 0:
            mp, ckv_new, krope_new = pool_mla_context(hp, *p0)
            ms = pool_mla_latent(hs, cache_l0_mla_ckv, cache_l0_mla_krope, *p0)
            new_state += [ckv_new, krope_new]
        else:
            mp, k_new, v_new = na_context(hp, w_in_l1, na_q_norm_l1, na_k_norm_l1, w_out_l1)
            ms = na_latent(hs, cache_l1_na_k, cache_l1_na_v, w_in_l1, na_q_norm_l1, na_k_norm_l1,
                           rel_bias_l1, w_out_l1)
            new_state += [k_new, v_new]
        xp = xp + g1_p * mp
        xs = xs + g1_s * ms
        xp = xp + g2_p * channel_mixer(modulate(xp, norm_mlp[i], sh2_p, sc2_p), mlp1[i], mlp2[i])
        xs = xs + g2_s * channel_mixer(modulate(xs, norm_mlp[i], sh2_s, sc2_s), mlp1[i], mlp2[i])
    y_prompt = xp
    y_sample = xs
    return (y_prompt, y_sample, *new_state)
```

```python
import functools

import jax
import jax.numpy as jnp
import numpy as np
from jax import lax
from jax.experimental import pallas as pl
from jax.experimental.pallas import tpu as pltpu

F32 = jnp.float32
BF16 = jnp.bfloat16

D_MODEL = 1024
D_FF = 4 * D_MODEL
N_ADA = 6
EPS = 1e-6
NEG_INF = -1e30
ROPE_THETA = 10000.0
GRID_W = 64
POOL_WIDTH = 512
POOL_WINDOWS = (2, 4, 8, 16)
POOL_GROUP = 128
POOL_HALO = 8
MLA_HEADS = 8
MLA_NOPE = 64
MLA_ROPE = 32
MLA_QK = MLA_NOPE + MLA_ROPE
MLA_V = 64
Q_LORA = 256
KV_LORA = 128
NA_HEADS = 16
NA_HEAD_DIM = 64
NA_KH = 8
NA_KW = 16

LANES = 128
SUBLANES = 8
VMEM_LIMIT = 56 * 1024 * 1024

TOKEN_TILE = 512
MLA_Q_TILE = 256
FF_CHUNK = 1024


def _params(n_axes):
    return pltpu.CompilerParams(dimension_semantics=("arbitrary",) * n_axes,
                                vmem_limit_bytes=VMEM_LIMIT)


def _const_spec(shape):
    zeros = (0,) * len(shape)
    return pl.BlockSpec(shape, lambda *_: zeros, pipeline_mode=pl.Buffered(1))


def _dot(a, b):
    return jnp.dot(a, b, preferred_element_type=F32)


def _dot_nt(a, b):
    return lax.dot_general(a, b, (((1,), (1,)), ((), ())), preferred_element_type=F32)


def _rms(x, w, n=None):
    n = x.shape[-1] if n is None else n
    ss = jnp.sum(x * x, axis=-1, keepdims=True)
    return x * lax.rsqrt(ss * (1.0 / n) + EPS) * w


def _modulate(x, nw, shift, scale):
    return _rms(x, nw) * (1.0 + scale) + shift


def _ada_kernel(cond_ref, w_ref, b_ref, o_ref):
    c = cond_ref[...]
    s = (c * jax.nn.sigmoid(c)).astype(BF16)
    o_ref[...] = _dot(s, w_ref[...].astype(BF16)) + b_ref[...]


def _ada(cond8, w_ada, b_ada):
    n_out = w_ada.shape[1]
    tn = D_MODEL
    return pl.pallas_call(
        _ada_kernel,
        out_shape=jax.ShapeDtypeStruct((SUBLANES, n_out), F32),
        grid=(n_out // tn,),
        in_specs=[pl.BlockSpec((SUBLANES, D_MODEL), lambda j: (0, 0)),
                  pl.BlockSpec((D_MODEL, tn), lambda j: (0, j)),
                  pl.BlockSpec((1, tn), lambda j: (0, j))],
        out_specs=pl.BlockSpec((SUBLANES, tn), lambda j: (0, j)),
        compiler_params=_params(1),
        name="ada",
    )(cond8, w_ada, b_ada.reshape(1, n_out))


def _rope(x, c, s1, s2):
    return x * c + pltpu.roll(x, LANES - 8, 1) * s1 + pltpu.roll(x, 8, 1) * s2


def _head_norm(xh, g, rope_tabs):
    y = _rms(xh, g, MLA_QK)
    if rope_tabs is not None:
        y = _rope(y, *rope_tabs)
    return y.astype(BF16)


def _mla_keys_values(ckv, kr_slab, wk_ref, gk, wv_ref, rope_tabs, k_ref, v_ref):
    c16 = ckv.astype(BF16)
    kn = _dot(c16, wk_ref[...])
    for h in range(MLA_HEADS):
        kh = kn[:, h * LANES:(h + 1) * LANES] + kr_slab
        k_ref[:, h * LANES:(h + 1) * LANES] = _head_norm(kh, gk, rope_tabs)
    v_ref[...] = _dot(c16, wv_ref[...]).astype(BF16)


def _l0_pre_kernel(latent, x_ref, mod_ref, nw_ref, win_ref, qln_ref, kvn_ref, wq_ref, gq_ref,
                   wk_ref, gk_ref, wv_ref, *rest):
    if latent:
        c_ref, s1_ref, s2_ref, a_ref, q_ref, k_ref, v_ref, ckv_ref, kr_ref = rest
        rope_tabs = (c_ref[...], s1_ref[...], s2_ref[...])
    else:
        a_ref, q_ref, k_ref, v_ref, ckv_ref, kr_ref = rest
        rope_tabs = None
    h = _modulate(x_ref[...], nw_ref[...], mod_ref[0:1, :], mod_ref[1:2, :]).astype(BF16)
    proj = _dot(h, win_ref[...])
    a_ref[...] = proj[:, :POOL_WIDTH]
    q_lat = proj[:, POOL_WIDTH:POOL_WIDTH + Q_LORA]
    kv_lat = proj[:, POOL_WIDTH + Q_LORA:POOL_WIDTH + Q_LORA + KV_LORA]
    kr_slab = proj[:, POOL_WIDTH + Q_LORA + KV_LORA:]
    q = _dot(_rms(q_lat, qln_ref[...]).astype(BF16), wq_ref[...])
    gq = gq_ref[...]
    for hd in range(MLA_HEADS):
        q_ref[:, hd * LANES:(hd + 1) * LANES] = _head_norm(q[:, hd * LANES:(hd + 1) * LANES], gq, rope_tabs)
    ckv = _rms(kv_lat, kvn_ref[...])
    ckv_ref[...] = ckv
    kr_ref[...] = kr_slab[:, MLA_NOPE:MLA_QK]
    _mla_keys_values(ckv, kr_slab, wk_ref, gk_ref[...], wv_ref, rope_tabs, k_ref, v_ref)


def _l0_pre(x, mod, tiles_per_group, latent, w, rope_tabs):
    n = x.shape[0]
    tm = TOKEN_TILE
    row = lambda i: (i, 0)
    hk = MLA_HEADS * LANES
    in_specs = [pl.BlockSpec((tm, D_MODEL), row),
                pl.BlockSpec((None, SUBLANES, D_MODEL), lambda i: (i // tiles_per_group, 0, 0)),
                _const_spec((1, D_MODEL)),
                _const_spec((D_MODEL, D_MODEL)),
                _const_spec((1, Q_LORA)),
                _const_spec((1, KV_LORA)),
                _const_spec((Q_LORA, hk)),
                _const_spec((1, LANES)),
                _const_spec((KV_LORA, hk)),
                _const_spec((1, LANES)),
                _const_spec((KV_LORA, MLA_HEADS * MLA_V))]
    args = [x, mod, w["norm_mix"], w["w_in"], w["qln"], w["kvn"], w["wq"], w["gq"], w["wk"], w["gk"], w["wv"]]
    if latent:
        seq_tiles = rope_tabs[0].shape[0] // tm
        in_specs += [pl.BlockSpec((tm, LANES), lambda i: (i % seq_tiles, 0))] * 3
        args += list(rope_tabs)
    out_shape = (jax.ShapeDtypeStruct((n, POOL_WIDTH), F32),
                 jax.ShapeDtypeStruct((n, hk), BF16),
                 jax.ShapeDtypeStruct((n, hk), BF16),
                 jax.ShapeDtypeStruct((n, MLA_HEADS * MLA_V), BF16),
                 jax.ShapeDtypeStruct((n, KV_LORA), F32),
                 jax.ShapeDtypeStruct((n, MLA_ROPE), F32))
    out_specs = tuple(pl.BlockSpec((tm, s.shape[1]), row) for s in out_shape)
    return pl.pallas_call(
        functools.partial(_l0_pre_kernel, latent),
        out_shape=out_shape, grid=(n // tm,), in_specs=in_specs, out_specs=out_specs,
        compiler_params=_params(1), name="l0_pre_latent" if latent else "l0_pre_ctx",
    )(*args)


def _cache_kv_kernel(ckv_ref, kr_ref, wk_ref, gk_ref, wv_ref, k_ref, v_ref):
    _mla_keys_values(ckv_ref[...], kr_ref[...], wk_ref, gk_ref[...], wv_ref, None, k_ref, v_ref)


def _cache_kv(ckv, kr_slab, w):
    n = ckv.shape[0]
    tm = TOKEN_TILE
    row = lambda i: (i, 0)
    hk = MLA_HEADS * LANES
    return pl.pallas_call(
        _cache_kv_kernel,
        out_shape=(jax.ShapeDtypeStruct((n, hk), BF16), jax.ShapeDtypeStruct((n, MLA_HEADS * MLA_V), BF16)),
        grid=(n // tm,),
        in_specs=[pl.BlockSpec((tm, KV_LORA), row), pl.BlockSpec((tm, LANES), row),
                  _const_spec((KV_LORA, hk)), _const_spec((1, LANES)), _const_spec((KV_LORA, MLA_HEADS * MLA_V))],
        out_specs=(pl.BlockSpec((tm, hk), row), pl.BlockSpec((tm, MLA_HEADS * MLA_V), row)),
        compiler_params=_params(1), name="l0_cache_kv",
    )(ckv, kr_slab, w["wk"], w["gk"], w["wv"])


def _pool_kernel(seq_len, prev_ref, u_ref, next_ref, wp_ref, ps_ref, o_ref, buf_ref):
    tm = u_ref.shape[0]
    buf_ref[0:POOL_HALO, :] = prev_ref[...]
    buf_ref[POOL_HALO:POOL_HALO + tm, :] = u_ref[...]
    buf_ref[POOL_HALO + tm:, :] = next_ref[...]
    base = pl.program_id(0) * tm
    pos = (base + lax.broadcasted_iota(jnp.int32, (tm, 1), 0)) % seq_len
    for g, win in enumerate(POOL_WINDOWS):
        lanes = pl.ds(g * POOL_GROUP, POOL_GROUP)
        acc = jnp.zeros((tm, POOL_GROUP), F32)
        cnt = jnp.zeros((tm, 1), F32)
        for d in range(-(win // 2), win - win // 2):
            src = pos + d
            ok = (src >= 0) & (src < seq_len)
            acc = acc + jnp.where(ok, buf_ref[pl.ds(POOL_HALO + d, tm), lanes], 0.0)
            cnt = cnt + ok.astype(F32)
        pooled = acc / cnt - u_ref[:, lanes]
        o_ref[:, lanes] = (_dot(pooled.astype(BF16), wp_ref[g]) * ps_ref[:, lanes]).astype(BF16)


def _pool(a_in, seq_len, w):
    n = a_in.shape[0]
    tm = TOKEN_TILE
    hb = tm // POOL_HALO
    last = n // POOL_HALO - 1
    return pl.pallas_call(
        functools.partial(_pool_kernel, seq_len),
        out_shape=jax.ShapeDtypeStruct((n, POOL_WIDTH), BF16),
        grid=(n // tm,),
        in_specs=[pl.BlockSpec((POOL_HALO, POOL_WIDTH), lambda i: (jnp.maximum(i * hb - 1, 0), 0)),
                  pl.BlockSpec((tm, POOL_WIDTH), lambda i: (i, 0)),
                  pl.BlockSpec((POOL_HALO, POOL_WIDTH), lambda i: (jnp.minimum((i + 1) * hb, last), 0)),
                  _const_spec((len(POOL_WINDOWS), POOL_GROUP, POOL_GROUP)),
                  _const_spec((1, POOL_WIDTH))],
        out_specs=pl.BlockSpec((tm, POOL_WIDTH), lambda i: (i, 0)),
        scratch_shapes=[pltpu.VMEM((tm + 2 * POOL_HALO, POOL_WIDTH), F32)],
        compiler_params=_params(1), name="l0_pool",
    )(a_in, a_in, a_in, w["w_pool"], w["pool_scale"])


def _softmax_pv(scores, values):
    m = functools.reduce(jnp.maximum, [jnp.max(s, axis=-1, keepdims=True) for s in scores])
    acc = None
    den = None
    for s, v in zip(scores, values):
        p = jnp.exp(s - m)
        l = jnp.sum(p, axis=-1, keepdims=True)
        o = _dot(p.astype(BF16), v)
        acc = o if acc is None else acc + o
        den = l if den is None else den + l
    return acc / den


def _ctx_attn_kernel(heads, dq, dv, q_ref, k_ref, v_ref, o_ref):
    for h in range(heads):
        s = _dot_nt(q_ref[:, h * dq:(h + 1) * dq], k_ref[:, h * dq:(h + 1) * dq])
        o = _softmax_pv([s], [v_ref[:, h * dv:(h + 1) * dv]])
        o_ref[:, h * dv:(h + 1) * dv] = o.astype(o_ref.dtype)


def _ctx_attn(q, k, v, seq, heads, dq, dv):
    n = q.shape[0]
    row = lambda b: (b, 0)
    return pl.pallas_call(
        functools.partial(_ctx_attn_kernel, heads, dq, dv),
        out_shape=jax.ShapeDtypeStruct((n, heads * dv), BF16),
        grid=(n // seq,),
        in_specs=[pl.BlockSpec((seq, heads * dq), row), pl.BlockSpec((seq, heads * dq), row),
                  pl.BlockSpec((seq, heads * dv), row)],
        out_specs=pl.BlockSpec((seq, heads * dv), row),
        compiler_params=_params(1), name="ctx_attn",
    )(q, k, v)


MLA_PAIR = 2


def _mla_lat_kernel(q_ref, kl_ref, vl_ref, kc_ref, vc_ref, o_ref):
    for h in range(MLA_PAIR):
        qs = pl.ds(h * LANES, LANES)
        vs = pl.ds(h * MLA_V, MLA_V)
        q = q_ref[:, qs]
        o = _softmax_pv([_dot_nt(q, kl_ref[:, qs]), _dot_nt(q, kc_ref[:, qs])],
                        [vl_ref[:, vs], vc_ref[:, vs]])
        o_ref[:, vs] = o.astype(o_ref.dtype)


def _mla_lat_attn(q, k, v, kc, vc, seq, past):
    n = q.shape[0]
    tq = MLA_Q_TILE
    nq = seq // tq
    kw = MLA_PAIR * LANES
    vw = MLA_PAIR * MLA_V
    return pl.pallas_call(
        _mla_lat_kernel,
        out_shape=jax.ShapeDtypeStruct((n, MLA_HEADS * MLA_V), BF16),
        grid=(n // seq, MLA_HEADS // MLA_PAIR, nq),
        in_specs=[pl.BlockSpec((tq, kw), lambda b, h, i: (b * nq + i, h)),
                  pl.BlockSpec((seq, kw), lambda b, h, i: (b, h)),
                  pl.BlockSpec((seq, vw), lambda b, h, i: (b, h)),
                  pl.BlockSpec((past, kw), lambda b, h, i: (b, h)),
                  pl.BlockSpec((past, vw), lambda b, h, i: (b, h))],
        out_specs=pl.BlockSpec((tq, vw), lambda b, h, i: (b * nq + i, h)),
        compiler_params=_params(3), name="l0_mla_latent_attn",
    )(q, k, v, kc, vc)


def _na_lat_kernel(q_ref, k_ref, v_ref, kc_ref, vc_ref, bias_ref, o_ref):
    d = NA_HEAD_DIM
    for h in range(NA_HEADS):
        hs = pl.ds(h * d, d)
        q = q_ref[:, hs]
        s_loc = _dot_nt(q, k_ref[:, hs]) + bias_ref[h]
        s_ctx = _dot_nt(q, kc_ref[:, hs])
        o = _softmax_pv([s_loc, s_ctx], [v_ref[:, hs], vc_ref[:, hs]])
        o_ref[:, hs] = o.astype(o_ref.dtype)


def _na_row_start(r, rows):
    return jnp.clip(r - NA_KH // 2, 0, rows - NA_KH)


def _na_lat_attn(q, k, v, kc, vc, bias_tab, seq, past):
    n = q.shape[0]
    rows = seq // GRID_W
    win = NA_KH * GRID_W
    width = NA_HEADS * NA_HEAD_DIM

    def win_map(b, r):
        return (pl.multiple_of(b * seq + _na_row_start(r, rows) * GRID_W, GRID_W), 0)

    return pl.pallas_call(
        _na_lat_kernel,
        out_shape=jax.ShapeDtypeStruct((n, width), BF16),
        grid=(n // seq, rows),
        in_specs=[pl.BlockSpec((GRID_W, width), lambda b, r: (b * rows + r, 0)),
                  pl.BlockSpec((pl.Element(win), pl.Element(width)), win_map),
                  pl.BlockSpec((pl.Element(win), pl.Element(width)), win_map),
                  pl.BlockSpec((past, width), lambda b, r: (b, 0)),
                  pl.BlockSpec((past, width), lambda b, r: (b, 0)),
                  pl.BlockSpec((None, NA_HEADS, GRID_W, win),
                               lambda b, r: (_na_row_start(r, rows) - r + NA_KH - 1, 0, 0, 0))],
        out_specs=pl.BlockSpec((GRID_W, width), lambda b, r: (b * rows + r, 0)),
        compiler_params=_params(2), name="l1_na_latent_attn",
    )(q, k, v, kc, vc, bias_tab)


def _l1_pre_kernel(keep_f32, x_ref, mod_ref, nw_ref, win_ref, gq_ref, gk_ref, q_ref, k_ref, v_ref, *f32_refs):
    h = _modulate(x_ref[...], nw_ref[...], mod_ref[0:1, :], mod_ref[1:2, :]).astype(BF16)
    d = NA_HEAD_DIM
    width = NA_HEADS * d
    q = _dot(h, win_ref[:, 0:width])
    k = _dot(h, win_ref[:, width:2 * width])
    v = _dot(h, win_ref[:, 2 * width:3 * width])
    gq = gq_ref[...]
    gk = gk_ref[...]
    for hd in range(NA_HEADS):
        hs = slice(hd * d, (hd + 1) * d)
        q_ref[:, hs] = _rms(q[:, hs], gq).astype(BF16)
        kn = _rms(k[:, hs], gk)
        k_ref[:, hs] = kn.astype(BF16)
        if keep_f32:
            f32_refs[0][:, hs] = kn
    v_ref[...] = v.astype(BF16)
    if keep_f32:
        f32_refs[1][...] = v


def _l1_pre(x, mod, tiles_per_group, keep_f32, w):
    n = x.shape[0]
    tm = TOKEN_TILE
    row = lambda i: (i, 0)
    width = NA_HEADS * NA_HEAD_DIM
    out_shape = [jax.ShapeDtypeStruct((n, width), BF16)] * 3
    if keep_f32:
        out_shape += [jax.ShapeDtypeStruct((n, width), F32)] * 2
    return pl.pallas_call(
        functools.partial(_l1_pre_kernel, keep_f32),
        out_shape=tuple(out_shape), grid=(n // tm,),
        in_specs=[pl.BlockSpec((tm, D_MODEL), row),
                  pl.BlockSpec((None, SUBLANES, D_MODEL), lambda i: (i // tiles_per_group, 0, 0)),
                  _const_spec((1, D_MODEL)),
                  _const_spec((D_MODEL, 3 * width)),
                  _const_spec((1, NA_HEAD_DIM)),
                  _const_spec((1, NA_HEAD_DIM))],
        out_specs=tuple(pl.BlockSpec((tm, width), row) for _ in out_shape),
        compiler_params=_params(1), name="l1_pre_ctx" if keep_f32 else "l1_pre_latent",
    )(x, mod, w["norm_mix"], w["w_in"], w["gq"], w["gk"])


def _post_kernel(n_parts, x_ref, *rest):
    parts = rest[:n_parts]
    mod_ref, nw_ref, wo_ref, w1_ref, w2_ref, o_ref = rest[n_parts:]
    mix_in = parts[0][...] if n_parts == 1 else jnp.concatenate([p[...] for p in parts], axis=-1)
    x1 = x_ref[...] + mod_ref[2:3, :] * _dot(mix_in, wo_ref[...])
    h = _modulate(x1, nw_ref[...], mod_ref[3:4, :], mod_ref[4:5, :]).astype(BF16)
    acc = jnp.zeros(x1.shape, F32)
    for c in range(D_FF // FF_CHUNK):
        cs = pl.ds(c * FF_CHUNK, FF_CHUNK)
        u = jnp.square(jnp.maximum(_dot(h, w1_ref[:, cs]), 0.0)).astype(BF16)
        acc = acc + _dot(u, w2_ref[cs, :])
    o_ref[...] = x1 + mod_ref[5:6, :] * acc


def _post(x, parts, mod, tiles_per_group, w, name):
    n = x.shape[0]
    tm = TOKEN_TILE
    row = lambda i: (i, 0)
    in_specs = [pl.BlockSpec((tm, D_MODEL), row)]
    in_specs += [pl.BlockSpec((tm, p.shape[1]), row) for p in parts]
    in_specs += [pl.BlockSpec((None, SUBLANES, D_MODEL), lambda i: (i // tiles_per_group, 0, 0)),
                 _const_spec((1, D_MODEL)),
                 _const_spec((D_MODEL, D_MODEL)),
                 _const_spec((D_MODEL, D_FF)),
                 _const_spec((D_FF, D_MODEL))]
    return pl.pallas_call(
        functools.partial(_post_kernel, len(parts)),
        out_shape=jax.ShapeDtypeStruct((n, D_MODEL), F32), grid=(n // tm,),
        in_specs=in_specs, out_specs=pl.BlockSpec((tm, D_MODEL), row),
        compiler_params=_params(1), name=name,
    )(x, *parts, mod, w["norm_mlp"], w["w_out"], w["w_mlp1"], w["w_mlp2"])


def _rope_tables(seq):
    t = jnp.arange(seq)
    half = MLA_ROPE // 2
    inv_freq = jnp.power(ROPE_THETA, -jnp.arange(0, half, 2, dtype=F32) / half)
    ang_r = (t // GRID_W).astype(F32)[:, None] * inv_freq[None, :]
    ang_c = (t % GRID_W).astype(F32)[:, None] * inv_freq[None, :]
    cr, sr, cc, sc = jnp.cos(ang_r), jnp.sin(ang_r), jnp.cos(ang_c), jnp.sin(ang_c)
    one = jnp.ones((seq, 1), F32)
    z8 = jnp.zeros((seq, half // 2), F32)
    pad = LANES - MLA_QK
    c = jnp.concatenate([one.repeat(MLA_NOPE, 1), cr, cr, cc, cc, one.repeat(pad, 1)], axis=1)
    s1 = jnp.concatenate([z8.repeat(MLA_NOPE // 8, 1), -sr, z8, -sc, z8, z8.repeat(pad // 8, 1)], axis=1)
    s2 = jnp.concatenate([z8.repeat(MLA_NOPE // 8, 1), z8, sr, z8, sc, z8.repeat(pad // 8, 1)], axis=1)
    return c, s1, s2


def _na_bias_table(rel_bias):
    qcol = jnp.arange(GRID_W)
    kcol = jnp.arange(GRID_W)
    col_start = jnp.clip(qcol - NA_KW // 2, 0, GRID_W - NA_KW)
    col_mask = (kcol[None, :] >= col_start[:, None]) & (kcol[None, :] < col_start[:, None] + NA_KW)
    dc_idx = jnp.clip(kcol[None, :] - qcol[:, None] + NA_KW - 1, 0, 2 * NA_KW - 2)
    bias_cols = jnp.where(col_mask[None, None], rel_bias.astype(F32)[:, :, dc_idx], NEG_INF)
    variants = []
    for first in range(NA_KH):
        blk = bias_cols[:, first:first + NA_KH]
        variants.append(blk.transpose(0, 2, 1, 3).reshape(NA_HEADS, GRID_W, NA_KH * GRID_W))
    return jnp.stack(variants)


def _prep_l0(w_in, q_lora_norm, kv_lora_norm, w_q_up, w_kv_up, mla_q_norm, mla_k_norm, w_pool, pool_scale,
             w_out, norm_mix, norm_mlp, w_mlp1, w_mlp2):
    lat_end = POOL_WIDTH + Q_LORA + KV_LORA
    w_in_pad = jnp.zeros((D_MODEL, D_MODEL), F32)
    w_in_pad = w_in_pad.at[:, :lat_end].set(w_in[:, :lat_end])
    w_in_pad = w_in_pad.at[:, lat_end + MLA_NOPE:lat_end + MLA_QK].set(w_in[:, lat_end:])
    pad = LANES - MLA_QK
    wq = jnp.pad(w_q_up.reshape(Q_LORA, MLA_HEADS, MLA_QK), ((0, 0), (0, 0), (0, pad)))
    kv = w_kv_up.reshape(KV_LORA, MLA_HEADS, MLA_NOPE + MLA_V)
    wk = jnp.pad(kv[:, :, :MLA_NOPE], ((0, 0), (0, 0), (0, LANES - MLA_NOPE)))
    return dict(
        norm_mix=norm_mix.reshape(1, -1), norm_mlp=norm_mlp.reshape(1, -1),
        w_in=w_in_pad.astype(BF16),
        qln=q_lora_norm.reshape(1, -1), kvn=kv_lora_norm.reshape(1, -1),
        wq=wq.reshape(Q_LORA, MLA_HEADS * LANES).astype(BF16),
        gq=jnp.pad(mla_q_norm * (MLA_QK ** -0.5), (0, pad)).reshape(1, LANES),
        wk=wk.reshape(KV_LORA, MLA_HEADS * LANES).astype(BF16),
        gk=jnp.pad(mla_k_norm, (0, pad)).reshape(1, LANES),
        wv=kv[:, :, MLA_NOPE:].reshape(KV_LORA, MLA_HEADS * MLA_V).astype(BF16),
        w_pool=w_pool.astype(BF16), pool_scale=pool_scale.reshape(1, -1),
        w_out=w_out.astype(BF16), w_mlp1=w_mlp1.astype(BF16), w_mlp2=w_mlp2.astype(BF16))


def _split_mods(m):
    mods = m.reshape(SUBLANES, N_ADA, D_MODEL)[:3]
    mods = jnp.pad(mods, ((0, 0), (0, SUBLANES - N_ADA), (0, 0)))
    return mods[0:1], mods[1:3]


def kernel(x_prompt, x_sample, cache_l0_mla_ckv, cache_l0_mla_krope, cache_l1_na_k, cache_l1_na_v, c, c_ctx, w_ada_l0, b_ada_l0, norm_mix_l0, norm_mlp_l0, w_mlp1_l0, w_mlp2_l0, w_in_l0, q_lora_norm_l0, kv_lora_norm_l0, w_q_up_l0, w_kv_up_l0, mla_q_norm_l0, mla_k_norm_l0, w_pool_l0, pool_scale_l0, w_out_l0, w_ada_l1, b_ada_l1, norm_mix_l1, norm_mlp_l1, w_mlp1_l1, w_mlp2_l1, w_in_l1, na_q_norm_l1, na_k_norm_l1, rel_bias_l1, w_out_l1):
    batch, seq, _ = x_prompt.shape
    dec_batch, dec_seq, _ = x_sample.shape
    past = cache_l0_mla_ckv.shape[1]
    xp = x_prompt.reshape(batch * seq, D_MODEL)
    xs = x_sample.reshape(dec_batch * dec_seq, D_MODEL)
    tiles_p = (batch * seq) // TOKEN_TILE
    tiles_s = dec_seq // TOKEN_TILE

    cond8 = jnp.zeros((SUBLANES, D_MODEL), F32).at[0].set(c_ctx).at[1:1 + dec_batch].set(c)
    mod0_p, mod0_s = _split_mods(_ada(cond8, w_ada_l0, b_ada_l0))
    mod1_p, mod1_s = _split_mods(_ada(cond8, w_ada_l1, b_ada_l1))

    w0 = _prep_l0(w_in_l0, q_lora_norm_l0, kv_lora_norm_l0, w_q_up_l0, w_kv_up_l0, mla_q_norm_l0, mla_k_norm_l0,
                  w_pool_l0, pool_scale_l0, w_out_l0, norm_mix_l0, norm_mlp_l0, w_mlp1_l0, w_mlp2_l0)
    a_p, q_p, k_p, v_p, ckv_p, kr_p = _l0_pre(xp, mod0_p, tiles_p, False, w0, None)
    a_s, q_s, k_s, v_s, _, _ = _l0_pre(xs, mod0_s, tiles_s, True, w0, _rope_tables(dec_seq))
    kr_cache = jnp.pad(cache_l0_mla_krope.reshape(dec_batch * past, MLA_ROPE), ((0, 0), (MLA_NOPE, LANES - MLA_QK)))
    k_c, v_c = _cache_kv(cache_l0_mla_ckv.reshape(dec_batch * past, KV_LORA), kr_cache, w0)
    attn_p = _ctx_attn(q_p, k_p, v_p, seq, MLA_HEADS, LANES, MLA_V)
    attn_s = _mla_lat_attn(q_s, k_s, v_s, k_c, v_c, dec_seq, past)
    xp = _post(xp, [_pool(a_p, seq, w0), attn_p], mod0_p, tiles_p, w0, "l0_post_ctx")
    xs = _post(xs, [_pool(a_s, dec_seq, w0), attn_s], mod0_s, tiles_s, w0, "l0_post_latent")

    w1 = dict(norm_mix=norm_mix_l1.reshape(1, -1), norm_mlp=norm_mlp_l1.reshape(1, -1),
              w_in=w_in_l1.astype(BF16),
              gq=(na_q_norm_l1 * (NA_HEAD_DIM ** -0.5)).reshape(1, -1), gk=na_k_norm_l1.reshape(1, -1),
              w_out=w_out_l1.astype(BF16), w_mlp1=w_mlp1_l1.astype(BF16), w_mlp2=w_mlp2_l1.astype(BF16))
    q1_p, k1_p, v1_p, k1_new, v1_new = _l1_pre(xp, mod1_p, tiles_p, True, w1)
    q1_s, k1_s, v1_s = _l1_pre(xs, mod1_s, tiles_s, False, w1)
    width = NA_HEADS * NA_HEAD_DIM
    na_p = _ctx_attn(q1_p, k1_p, v1_p, seq, NA_HEADS, NA_HEAD_DIM, NA_HEAD_DIM)
    na_s = _na_lat_attn(q1_s, k1_s, v1_s,
                        cache_l1_na_k.reshape(dec_batch * past, width).astype(BF16),
                        cache_l1_na_v.reshape(dec_batch * past, width).astype(BF16),
                        _na_bias_table(rel_bias_l1), dec_seq, past)
    xp = _post(xp, [na_p], mod1_p, tiles_p, w1, "l1_post_ctx")
    xs = _post(xs, [na_s], mod1_s, tiles_s, w1, "l1_post_latent")

    return (xp.reshape(batch, seq, D_MODEL), xs.reshape(dec_batch, dec_seq, D_MODEL),
            ckv_p.reshape(batch, seq, KV_LORA), kr_p.reshape(batch, seq, MLA_ROPE),
            k1_new.reshape(batch, seq, NA_HEADS, NA_HEAD_DIM), v1_new.reshape(batch, seq, NA_HEADS, NA_HEAD_DIM))
```

```python
import functools

import jax
import jax.numpy as jnp
from jax import lax
from jax.experimental import pallas as pl
from jax.experimental.pallas import tpu as pltpu

F32 = jnp.float32
BF16 = jnp.bfloat16

D_MODEL = 1024
D_FF = 4 * D_MODEL
N_ADA = 6
EPS = 1e-6
NEG_INF = -1e30
ROPE_THETA = 10000.0
GRID_W = 64
POOL_WIDTH = 512
POOL_WINDOWS = (2, 4, 8, 16)
POOL_GROUP = 128
POOL_HALO = 8
MLA_HEADS = 8
MLA_NOPE = 64
MLA_ROPE = 32
MLA_QK = MLA_NOPE + MLA_ROPE
MLA_V = 64
Q_LORA = 256
KV_LORA = 128
NA_HEADS = 16
NA_HEAD_DIM = 64
NA_KH = 8
NA_KW = 16
LOG2E = 1.4426950408889634

LANES = 128
SUBLANES = 8
VMEM_LIMIT = 56 * 1024 * 1024

TOKEN_TILE = 512
MLA_Q_TILE = 512
MLA_STEP_HEADS = 2
MLA_KEY_CHUNK = 512
FF_CHUNK = 1024
NA_SLABS = NA_HEADS * NA_HEAD_DIM // LANES
NA_TILE_ROWS = 4
NA_WIN_ROWS = NA_TILE_ROWS + NA_KH - 1


def _params(n_axes):
    return pltpu.CompilerParams(dimension_semantics=("arbitrary",) * n_axes,
                                vmem_limit_bytes=VMEM_LIMIT)


def _const_spec(shape):
    zeros = (0,) * len(shape)
    return pl.BlockSpec(shape, lambda *_: zeros, pipeline_mode=pl.Buffered(1))


def _dot(a, b):
    return jnp.dot(a, b, preferred_element_type=F32)


def _dot_nt(a, b):
    return lax.dot_general(a, b, (((1,), (1,)), ((), ())), preferred_element_type=F32)


def _dot_tn(a, b):
    return lax.dot_general(a, b, (((0,), (0,)), ((), ())), preferred_element_type=F32)


def _rms(x, w, n=None):
    n = x.shape[-1] if n is None else n
    ss = jnp.sum(x * x, axis=-1, keepdims=True)
    return x * lax.rsqrt(ss * (1.0 / n) + EPS) * w


def _modulate(x, nw, shift, scale):
    return _rms(x, nw) * (1.0 + scale) + shift


def _ada_kernel(cond_ref, w_ref, b_ref, o_ref):
    c = cond_ref[...]
    s = (c * jax.nn.sigmoid(c)).astype(BF16)
    o_ref[...] = _dot(s, w_ref[...].astype(BF16)) + b_ref[...]


def _ada(cond8, w_ada, b_ada):
    n_out = w_ada.shape[1]
    tn = D_MODEL
    return pl.pallas_call(
        _ada_kernel,
        out_shape=jax.ShapeDtypeStruct((SUBLANES, n_out), F32),
        grid=(n_out // tn,),
        in_specs=[pl.BlockSpec((SUBLANES, D_MODEL), lambda j: (0, 0)),
                  pl.BlockSpec((D_MODEL, tn), lambda j: (0, j)),
                  pl.BlockSpec((1, tn), lambda j: (0, j))],
        out_specs=pl.BlockSpec((SUBLANES, tn), lambda j: (0, j)),
        compiler_params=_params(1),
        name="ada",
    )(cond8, w_ada, b_ada.reshape(1, n_out))


def _rope(x, c, s1, s2):
    return x * c + pltpu.roll(x, LANES - 8, 1) * s1 + pltpu.roll(x, 8, 1) * s2


def _head_norm(xh, g, rope_tabs):
    y = _rms(xh, g, MLA_QK)
    if rope_tabs is not None:
        y = _rope(y, *rope_tabs)
    return y.astype(BF16)


def _mla_keys_values(ckv, kr_slab, wk_ref, gk, wv_ref, rope_tabs, k_ref, v_ref):
    c16 = ckv.astype(BF16)
    kn = _dot(c16, wk_ref[...])
    v = _dot(c16, wv_ref[...])
    ones_lane = (lax.broadcasted_iota(jnp.int32, (1, LANES), 1) == MLA_V).astype(F32)
    for h in range(MLA_HEADS):
        hs = slice(h * LANES, (h + 1) * LANES)
        k_ref[h] = _head_norm(kn[:, hs] + kr_slab, gk, rope_tabs)
        v_ref[h] = (v[:, hs] + ones_lane).astype(BF16)


def _l0_pre_kernel(latent, x_ref, mod_ref, nw_ref, win_ref, qln_ref, kvn_ref, wq_ref, gq_ref,
                   wk_ref, gk_ref, wv_ref, *rest):
    if latent:
        c_ref, s1_ref, s2_ref, a_ref, q_ref, k_ref, v_ref, ckv_ref, kr_ref = rest
        rope_tabs = (c_ref[...], s1_ref[...], s2_ref[...])
    else:
        a_ref, q_ref, k_ref, v_ref, ckv_ref, kr_ref = rest
        rope_tabs = None
    h = _modulate(x_ref[...], nw_ref[...], mod_ref[0:1, :], mod_ref[1:2, :]).astype(BF16)
    proj = _dot(h, win_ref[...])
    a_ref[...] = proj[:, :POOL_WIDTH]
    q_lat = proj[:, POOL_WIDTH:POOL_WIDTH + Q_LORA]
    kv_lat = proj[:, POOL_WIDTH + Q_LORA:POOL_WIDTH + Q_LORA + KV_LORA]
    kr_slab = proj[:, POOL_WIDTH + Q_LORA + KV_LORA:]
    q = _dot(_rms(q_lat, qln_ref[...]).astype(BF16), wq_ref[...])
    gq = gq_ref[...]
    for hd in range(MLA_HEADS):
        q_ref[hd] = _head_norm(q[:, hd * LANES:(hd + 1) * LANES], gq, rope_tabs)
    ckv = _rms(kv_lat, kvn_ref[...])
    ckv_ref[...] = ckv
    kr_ref[...] = kr_slab[:, MLA_NOPE:MLA_QK]
    _mla_keys_values(ckv, kr_slab, wk_ref, gk_ref[...], wv_ref, rope_tabs, k_ref, v_ref)


def _slab_spec(slabs, tm):
    return pl.BlockSpec((slabs, tm, LANES), lambda i: (0, i, 0))


def _l0_pre(x, mod, tiles_per_group, latent, w, rope_tabs):
    n = x.shape[0]
    tm = TOKEN_TILE
    row = lambda i: (i, 0)
    hk = MLA_HEADS * LANES
    in_specs = [pl.BlockSpec((tm, D_MODEL), row),
                pl.BlockSpec((None, SUBLANES, D_MODEL), lambda i: (i // tiles_per_group, 0, 0)),
                _const_spec((1, D_MODEL)),
                _const_spec((D_MODEL, D_MODEL)),
                _const_spec((1, Q_LORA)),
                _const_spec((1, KV_LORA)),
                _const_spec((Q_LORA, hk)),
                _const_spec((1, LANES)),
                _const_spec((KV_LORA, hk)),
                _const_spec((1, LANES)),
                _const_spec((KV_LORA, hk))]
    args = [x, mod, w["norm_mix"], w["w_in"], w["qln"], w["kvn"], w["wq"], w["gq"], w["wk"], w["gk"], w["wv"]]
    if latent:
        seq_tiles = rope_tabs[0].shape[0] // tm
        in_specs += [pl.BlockSpec((tm, LANES), lambda i: (i % seq_tiles, 0))] * 3
        args += list(rope_tabs)
    slab = jax.ShapeDtypeStruct((MLA_HEADS, n, LANES), BF16)
    out_shape = (jax.ShapeDtypeStruct((n, POOL_WIDTH), F32), slab, slab, slab,
                 jax.ShapeDtypeStruct((n, KV_LORA), F32),
                 jax.ShapeDtypeStruct((n, MLA_ROPE), F32))
    out_specs = (pl.BlockSpec((tm, POOL_WIDTH), row),
                 _slab_spec(MLA_HEADS, tm), _slab_spec(MLA_HEADS, tm), _slab_spec(MLA_HEADS, tm),
                 pl.BlockSpec((tm, KV_LORA), row), pl.BlockSpec((tm, MLA_ROPE), row))
    return pl.pallas_call(
        functools.partial(_l0_pre_kernel, latent),
        out_shape=out_shape, grid=(n // tm,), in_specs=in_specs, out_specs=out_specs,
        compiler_params=_params(1), name="l0_pre_latent" if latent else "l0_pre_ctx",
    )(*args)


def _cache_kv_kernel(ckv_ref, kr_ref, wk_ref, gk_ref, wv_ref, k_ref, v_ref):
    _mla_keys_values(ckv_ref[...], kr_ref[...], wk_ref, gk_ref[...], wv_ref, None, k_ref, v_ref)


def _cache_kv(ckv, kr_slab, w):
    n = ckv.shape[0]
    tm = TOKEN_TILE
    row = lambda i: (i, 0)
    hk = MLA_HEADS * LANES
    slab = jax.ShapeDtypeStruct((MLA_HEADS, n, LANES), BF16)
    return pl.pallas_call(
        _cache_kv_kernel,
        out_shape=(slab, slab),
        grid=(n // tm,),
        in_specs=[pl.BlockSpec((tm, KV_LORA), row), pl.BlockSpec((tm, LANES), row),
                  _const_spec((KV_LORA, hk)), _const_spec((1, LANES)), _const_spec((KV_LORA, hk))],
        out_specs=(_slab_spec(MLA_HEADS, tm), _slab_spec(MLA_HEADS, tm)),
        compiler_params=_params(1), name="l0_cache_kv",
    )(ckv, kr_slab, w["wk"], w["gk"], w["wv"])


def _pool_kernel(seq_len, prev_ref, u_ref, next_ref, wp_ref, ps_ref, o_ref, buf_ref):
    tm = u_ref.shape[0]
    buf_ref[0:POOL_HALO, :] = prev_ref[...]
    buf_ref[POOL_HALO:POOL_HALO + tm, :] = u_ref[...]
    buf_ref[POOL_HALO + tm:, :] = next_ref[...]
    base = pl.program_id(0) * tm
    pos = (base + lax.broadcasted_iota(jnp.int32, (tm, 1), 0)) % seq_len
    for g, win in enumerate(POOL_WINDOWS):
        lanes = pl.ds(g * POOL_GROUP, POOL_GROUP)
        acc = jnp.zeros((tm, POOL_GROUP), F32)
        cnt = jnp.zeros((tm, 1), F32)
        for d in range(-(win // 2), win - win // 2):
            src = pos + d
            ok = (src >= 0) & (src < seq_len)
            acc = acc + jnp.where(ok, buf_ref[pl.ds(POOL_HALO + d, tm), lanes], 0.0)
            cnt = cnt + ok.astype(F32)
        pooled = acc / cnt - u_ref[:, lanes]
        o_ref[:, lanes] = (_dot(pooled.astype(BF16), wp_ref[g]) * ps_ref[:, lanes]).astype(BF16)


def _pool(a_in, seq_len, w):
    n = a_in.shape[0]
    tm = TOKEN_TILE
    hb = tm // POOL_HALO
    last = n // POOL_HALO - 1
    return pl.pallas_call(
        functools.partial(_pool_kernel, seq_len),
        out_shape=jax.ShapeDtypeStruct((n, POOL_WIDTH), BF16),
        grid=(n // tm,),
        in_specs=[pl.BlockSpec((POOL_HALO, POOL_WIDTH), lambda i: (jnp.maximum(i * hb - 1, 0), 0)),
                  pl.BlockSpec((tm, POOL_WIDTH), lambda i: (i, 0)),
                  pl.BlockSpec((POOL_HALO, POOL_WIDTH), lambda i: (jnp.minimum((i + 1) * hb, last), 0)),
                  _const_spec((len(POOL_WINDOWS), POOL_GROUP, POOL_GROUP)),
                  _const_spec((1, POOL_WIDTH))],
        out_specs=pl.BlockSpec((tm, POOL_WIDTH), lambda i: (i, 0)),
        scratch_shapes=[pltpu.VMEM((tm + 2 * POOL_HALO, POOL_WIDTH), F32)],
        compiler_params=_params(1), name="l0_pool",
    )(a_in, a_in, a_in, w["w_pool"], w["pool_scale"])


def _attend(streams):
    items = [(si, ch) for si, (_, chunks, _) in enumerate(streams) for ch in chunks]

    def score(item):
        si, (k_fn, _, extra_fn) = item
        s = _dot_nt(k_fn(), streams[si][0])
        return s if extra_fn is None else s + extra_fn()

    state = [None] * len(streams)
    s_next = score(items[0])
    for i, (si, (_, v_fn, _)) in enumerate(items):
        s = s_next
        if i + 1 < len(items):
            s_next = score(items[i + 1])
        want_den = streams[si][2]
        m_new = jnp.max(s, axis=0, keepdims=True)
        if state[si] is not None:
            m, acc, den = state[si]
            m_new = jnp.maximum(m, m_new)
            alpha = jnp.exp2(m - m_new)
        p = jnp.exp2(s - m_new)
        pv = _dot_tn(v_fn(), p.astype(BF16))
        psum = jnp.sum(p, axis=0, keepdims=True) if want_den else None
        if state[si] is None:
            state[si] = (m_new, pv, psum)
        else:
            state[si] = (m_new, acc * alpha + pv, den * alpha + psum if want_den else None)
    return [(acc, den) for _, acc, den in state]


def _ref_chunks(k_ref, v_ref, slab, chunk, extra_fn=None):
    nk = k_ref.shape[1]
    out = []
    for c in range(0, nk, chunk):
        rows = pl.ds(c, min(chunk, nk - c))
        out.append((lambda rows=rows: k_ref[slab, rows, :], lambda rows=rows: v_ref[slab, rows, :], extra_fn))
    return out


def _mla_finish(o_t):
    return o_t[:MLA_V] / o_t[MLA_V:MLA_V + 1]


def _na_queries(q2):
    low = lax.broadcasted_iota(jnp.int32, (1, LANES), 1) < NA_HEAD_DIM
    zero = jnp.zeros_like(q2)
    return jnp.concatenate([jnp.where(low, q2, zero), jnp.where(low, zero, q2)], axis=0)


def _na_finish(o_t, den):
    nq = o_t.shape[1] // 2
    o = o_t / den
    return jnp.concatenate([o[:NA_HEAD_DIM, :nq], o[NA_HEAD_DIM:, nq:]], axis=0)


def _mla_ctx_kernel(q_ref, k_ref, v_ref, o_ref):
    seq = k_ref.shape[1]
    res = _attend([(q_ref[h], _ref_chunks(k_ref, v_ref, h, seq), False) for h in range(MLA_HEADS)])
    for hp in range(MLA_HEADS // 2):
        o_t = jnp.concatenate([_mla_finish(res[h][0]) for h in (2 * hp, 2 * hp + 1)], axis=0)
        o_ref[:, hp * LANES:(hp + 1) * LANES] = o_t.T.astype(BF16)


def _na_ctx_kernel(q_ref, k_ref, v_ref, o_ref):
    seq = k_ref.shape[1]
    res = _attend([(_na_queries(q_ref[s]), _ref_chunks(k_ref, v_ref, s, seq), True) for s in range(NA_SLABS)])
    for s in range(NA_SLABS):
        o_ref[:, s * LANES:(s + 1) * LANES] = _na_finish(*res[s]).T.astype(BF16)


def _ctx_attn(body, q, k, v, seq, width, name):
    slabs, n, _ = q.shape
    spec = pl.BlockSpec((slabs, seq, LANES), lambda b: (0, b, 0))
    return pl.pallas_call(
        body,
        out_shape=jax.ShapeDtypeStruct((n, width), BF16),
        grid=(n // seq,),
        in_specs=[spec, spec, spec],
        out_specs=pl.BlockSpec((seq, width), lambda b: (b, 0)),
        compiler_params=_params(1), name=name,
    )(q, k, v)


def _mla_lat_kernel(q_ref, kl_ref, vl_ref, kc_ref, vc_ref, o_ref):
    res = _attend([(q_ref[h],
                    _ref_chunks(kl_ref, vl_ref, h, MLA_KEY_CHUNK) + _ref_chunks(kc_ref, vc_ref, h, MLA_KEY_CHUNK),
                    False) for h in range(MLA_STEP_HEADS)])
    o_t = jnp.concatenate([_mla_finish(o) for o, _ in res], axis=0)
    o_ref[...] = o_t.T.astype(BF16)


def _mla_lat_attn(q, k, v, kc, vc, seq, past):
    n = q.shape[1]
    tq = MLA_Q_TILE
    nq = seq // tq
    hs = MLA_STEP_HEADS
    return pl.pallas_call(
        _mla_lat_kernel,
        out_shape=jax.ShapeDtypeStruct((n, MLA_HEADS * MLA_V), BF16),
        grid=(n // seq, MLA_HEADS // hs, nq),
        in_specs=[pl.BlockSpec((hs, tq, LANES), lambda b, h, i: (h, b * nq + i, 0)),
                  pl.BlockSpec((hs, seq, LANES), lambda b, h, i: (h, b, 0)),
                  pl.BlockSpec((hs, seq, LANES), lambda b, h, i: (h, b, 0)),
                  pl.BlockSpec((hs, past, LANES), lambda b, h, i: (h, b, 0)),
                  pl.BlockSpec((hs, past, LANES), lambda b, h, i: (h, b, 0))],
        out_specs=pl.BlockSpec((tq, hs * MLA_V), lambda b, h, i: (b * nq + i, h)),
        compiler_params=_params(3), name="l0_mla_latent_attn",
    )(q, k, v, kc, vc)


def _na_first_row(r, rows):
    return jnp.clip(r - NA_KH // 2, 0, rows - NA_KH)


def _na_lat_kernel(rows, q_ref, k_ref, v_ref, kc_ref, vc_ref, t2_ref, o_ref):
    nq = NA_TILE_ROWS * GRID_W
    nk = NA_WIN_ROWS * GRID_W
    r0 = pl.program_id(1) * NA_TILE_ROWS
    w0 = jnp.clip(r0 - NA_KH // 2, 0, rows - NA_WIN_ROWS)
    key_row = w0 + lax.broadcasted_iota(jnp.int32, (nk, nq), 0) // GRID_W
    first = _na_first_row(r0 + lax.broadcasted_iota(jnp.int32, (nk, nq), 1) // GRID_W, rows)
    row_mask = jnp.where((key_row >= first) & (key_row < first + NA_KH), 0.0, NEG_INF)
    idx = [[jnp.clip(w0 + j - (r0 + 2 * t) + NA_KH - 1, 0, 2 * NA_KH - 2) for t in range(NA_TILE_ROWS // 2)]
           for j in range(NA_WIN_ROWS)]

    def local_bias(s):
        heads = [jnp.concatenate([jnp.concatenate([t2_ref[2 * s + hh, i] for i in row], axis=1) for row in idx], axis=0)
                 + row_mask for hh in range(2)]
        return jnp.concatenate(heads, axis=1)

    res = _attend([(_na_queries(q_ref[s]),
                    _ref_chunks(k_ref, v_ref, s, nk, functools.partial(local_bias, s)) + _ref_chunks(kc_ref, vc_ref, s, nk),
                    True) for s in range(NA_SLABS)])
    for s in range(NA_SLABS):
        o_ref[:, s * LANES:(s + 1) * LANES] = _na_finish(*res[s]).T.astype(BF16)


def _na_lat_attn(q, k, v, kc, vc, t2, seq, past):
    slabs, n, _ = q.shape
    rows = seq // GRID_W
    tiles = rows // NA_TILE_ROWS
    nq = NA_TILE_ROWS * GRID_W
    nk = NA_WIN_ROWS * GRID_W

    def win_map(b, t):
        w0 = jnp.clip(t * NA_TILE_ROWS - NA_KH // 2, 0, rows - NA_WIN_ROWS)
        return (0, pl.multiple_of(b * seq + w0 * GRID_W, GRID_W), 0)

    win_spec = pl.BlockSpec((pl.Element(slabs), pl.Element(nk), pl.Element(LANES)), win_map)
    ctx_spec = pl.BlockSpec((slabs, past, LANES), lambda b, t: (0, b, 0))
    return pl.pallas_call(
        functools.partial(_na_lat_kernel, rows),
        out_shape=jax.ShapeDtypeStruct((n, slabs * LANES), BF16),
        grid=(n // seq, tiles),
        in_specs=[pl.BlockSpec((slabs, nq, LANES), lambda b, t: (0, b * tiles + t, 0)),
                  win_spec, win_spec, ctx_spec, ctx_spec,
                  _const_spec(t2.shape)],
        out_specs=pl.BlockSpec((nq, slabs * LANES), lambda b, t: (b * tiles + t, 0)),
        compiler_params=_params(2), name="l1_na_latent_attn",
    )(q, k, v, kc, vc, t2)


def _pair_norm(x2, g2):
    low = lax.broadcasted_iota(jnp.int32, (1, LANES), 1) < NA_HEAD_DIM
    sq = x2 * x2
    ss_all = jnp.sum(sq, axis=-1, keepdims=True)
    ss_low = jnp.sum(jnp.where(low, sq, 0.0), axis=-1, keepdims=True)
    ss = jnp.where(low, ss_low, ss_all - ss_low)
    return x2 * lax.rsqrt(ss * (1.0 / NA_HEAD_DIM) + EPS) * g2


def _l1_pre_kernel(keep_f32, x_ref, mod_ref, nw_ref, win_ref, gq_ref, gk_ref, q_ref, k_ref, v_ref, *f32_refs):
    h = _modulate(x_ref[...], nw_ref[...], mod_ref[0:1, :], mod_ref[1:2, :]).astype(BF16)
    width = NA_SLABS * LANES
    q = _dot(h, win_ref[:, 0:width])
    k = _dot(h, win_ref[:, width:2 * width])
    v = _dot(h, win_ref[:, 2 * width:3 * width])
    gq = gq_ref[...]
    gk = gk_ref[...]
    for s in range(NA_SLABS):
        ls = slice(s * LANES, (s + 1) * LANES)
        q_ref[s] = _pair_norm(q[:, ls], gq).astype(BF16)
        kn = _pair_norm(k[:, ls], gk)
        k_ref[s] = kn.astype(BF16)
        v_ref[s] = v[:, ls].astype(BF16)
        if keep_f32:
            f32_refs[0][:, ls] = kn
    if keep_f32:
        f32_refs[1][...] = v


def _l1_pre(x, mod, tiles_per_group, keep_f32, w):
    n = x.shape[0]
    tm = TOKEN_TILE
    row = lambda i: (i, 0)
    width = NA_SLABS * LANES
    slab = jax.ShapeDtypeStruct((NA_SLABS, n, LANES), BF16)
    out_shape = [slab, slab, slab]
    out_specs = [_slab_spec(NA_SLABS, tm)] * 3
    if keep_f32:
        out_shape += [jax.ShapeDtypeStruct((n, width), F32)] * 2
        out_specs += [pl.BlockSpec((tm, width), row)] * 2
    return pl.pallas_call(
        functools.partial(_l1_pre_kernel, keep_f32),
        out_shape=tuple(out_shape), grid=(n // tm,),
        in_specs=[pl.BlockSpec((tm, D_MODEL), row),
                  pl.BlockSpec((None, SUBLANES, D_MODEL), lambda i: (i // tiles_per_group, 0, 0)),
                  _const_spec((1, D_MODEL)),
                  _const_spec((D_MODEL, 3 * width)),
                  _const_spec((1, LANES)),
                  _const_spec((1, LANES))],
        out_specs=tuple(out_specs),
        compiler_params=_params(1), name="l1_pre_ctx" if keep_f32 else "l1_pre_latent",
    )(x, mod, w["norm_mix"], w["w_in"], w["gq"], w["gk"])


def _post_kernel(n_parts, x_ref, *rest):
    parts = rest[:n_parts]
    mod_ref, nw_ref, wo_ref, w1_ref, w2_ref, o_ref = rest[n_parts:]
    mix_in = parts[0][...] if n_parts == 1 else jnp.concatenate([p[...] for p in parts], axis=-1)
    x1 = x_ref[...] + mod_ref[2:3, :] * _dot(mix_in, wo_ref[...])
    h = _modulate(x1, nw_ref[...], mod_ref[3:4, :], mod_ref[4:5, :]).astype(BF16)
    acc = jnp.zeros(x1.shape, F32)
    for c in range(D_FF // FF_CHUNK):
        cs = pl.ds(c * FF_CHUNK, FF_CHUNK)
        u = jnp.square(jnp.maximum(_dot(h, w1_ref[:, cs]), 0.0)).astype(BF16)
        acc = acc + _dot(u, w2_ref[cs, :])
    o_ref[...] = x1 + mod_ref[5:6, :] * acc


def _post(x, parts, mod, tiles_per_group, w, name):
    n = x.shape[0]
    tm = TOKEN_TILE
    row = lambda i: (i, 0)
    in_specs = [pl.BlockSpec((tm, D_MODEL), row)]
    in_specs += [pl.BlockSpec((tm, p.shape[1]), row) for p in parts]
    in_specs += [pl.BlockSpec((None, SUBLANES, D_MODEL), lambda i: (i // tiles_per_group, 0, 0)),
                 _const_spec((1, D_MODEL)),
                 _const_spec((D_MODEL, D_MODEL)),
                 _const_spec((D_MODEL, D_FF)),
                 _const_spec((D_FF, D_MODEL))]
    return pl.pallas_call(
        functools.partial(_post_kernel, len(parts)),
        out_shape=jax.ShapeDtypeStruct((n, D_MODEL), F32), grid=(n // tm,),
        in_specs=in_specs, out_specs=pl.BlockSpec((tm, D_MODEL), row),
        compiler_params=_params(1), name=name,
    )(x, *parts, mod, w["norm_mlp"], w["w_out"], w["w_mlp1"], w["w_mlp2"])


def _rope_tables(seq):
    half = MLA_ROPE // 2
    inv_freq = jnp.power(ROPE_THETA, -jnp.arange(0, half, 2, dtype=F32) / half)
    ang = jnp.arange(GRID_W, dtype=F32)[:, None] * inv_freq[None, :]
    cos, sin = jnp.cos(ang), jnp.sin(ang)
    t = jnp.arange(seq)
    cr, sr, cc, sc = cos[t // GRID_W], sin[t // GRID_W], cos[t % GRID_W], sin[t % GRID_W]
    one = jnp.ones((seq, 1), F32)
    z8 = jnp.zeros((seq, half // 2), F32)
    pad = LANES - MLA_QK
    c = jnp.concatenate([one.repeat(MLA_NOPE, 1), cr, cr, cc, cc, one.repeat(pad, 1)], axis=1)
    s1 = jnp.concatenate([z8.repeat(MLA_NOPE // 8, 1), -sr, z8, -sc, z8, z8.repeat(pad // 8, 1)], axis=1)
    s2 = jnp.concatenate([z8.repeat(MLA_NOPE // 8, 1), z8, sr, z8, sc, z8.repeat(pad // 8, 1)], axis=1)
    return c, s1, s2


def _na_bias_table(rel_bias):
    qcol = jnp.arange(GRID_W)
    kcol = jnp.arange(GRID_W)
    col_start = jnp.clip(qcol - NA_KW // 2, 0, GRID_W - NA_KW)
    col_mask = (kcol[:, None] >= col_start[None, :]) & (kcol[:, None] < col_start[None, :] + NA_KW)
    dc_idx = jnp.clip(kcol[:, None] - qcol[None, :] + NA_KW - 1, 0, 2 * NA_KW - 2)
    blocks = jnp.where(col_mask[None, None], rel_bias.astype(F32)[:, :, dc_idx] * LOG2E, NEG_INF)
    below = jnp.concatenate([jnp.full_like(blocks[:, :1], NEG_INF), blocks[:, :-1]], axis=1)
    return jnp.concatenate([blocks, below], axis=-1)


def _prep_l0(w_in, q_lora_norm, kv_lora_norm, w_q_up, w_kv_up, mla_q_norm, mla_k_norm, w_pool, pool_scale,
             w_out, norm_mix, norm_mlp, w_mlp1, w_mlp2):
    lat_end = POOL_WIDTH + Q_LORA + KV_LORA
    w_in_pad = jnp.zeros((D_MODEL, D_MODEL), F32)
    w_in_pad = w_in_pad.at[:, :lat_end].set(w_in[:, :lat_end])
    w_in_pad = w_in_pad.at[:, lat_end + MLA_NOPE:lat_end + MLA_QK].set(w_in[:, lat_end:])
    pad = LANES - MLA_QK
    hk = MLA_HEADS * LANES
    wq = jnp.pad(w_q_up.reshape(Q_LORA, MLA_HEADS, MLA_QK), ((0, 0), (0, 0), (0, pad)))
    kv = w_kv_up.reshape(KV_LORA, MLA_HEADS, MLA_NOPE + MLA_V)
    wk = jnp.pad(kv[:, :, :MLA_NOPE], ((0, 0), (0, 0), (0, LANES - MLA_NOPE)))
    wv = jnp.pad(kv[:, :, MLA_NOPE:], ((0, 0), (0, 0), (0, LANES - MLA_V)))
    return dict(
        norm_mix=norm_mix.reshape(1, -1), norm_mlp=norm_mlp.reshape(1, -1),
        w_in=w_in_pad.astype(BF16),
        qln=q_lora_norm.reshape(1, -1), kvn=kv_lora_norm.reshape(1, -1),
        wq=wq.reshape(Q_LORA, hk).astype(BF16),
        gq=jnp.pad(mla_q_norm * (MLA_QK ** -0.5 * LOG2E), (0, pad)).reshape(1, LANES),
        wk=wk.reshape(KV_LORA, hk).astype(BF16),
        gk=jnp.pad(mla_k_norm, (0, pad)).reshape(1, LANES),
        wv=wv.reshape(KV_LORA, hk).astype(BF16),
        w_pool=w_pool.astype(BF16), pool_scale=pool_scale.reshape(1, -1),
        w_out=w_out.astype(BF16), w_mlp1=w_mlp1.astype(BF16), w_mlp2=w_mlp2.astype(BF16))


def _split_mods(m):
    mods = m.reshape(SUBLANES, N_ADA, D_MODEL)[:3]
    mods = jnp.pad(mods, ((0, 0), (0, SUBLANES - N_ADA), (0, 0)))
    return mods[0:1], mods[1:3]


def _to_slabs(x, n):
    return x.reshape(n, -1, LANES).transpose(1, 0, 2).astype(BF16)


def kernel(x_prompt, x_sample, cache_l0_mla_ckv, cache_l0_mla_krope, cache_l1_na_k, cache_l1_na_v, c, c_ctx, w_ada_l0, b_ada_l0, norm_mix_l0, norm_mlp_l0, w_mlp1_l0, w_mlp2_l0, w_in_l0, q_lora_norm_l0, kv_lora_norm_l0, w_q_up_l0, w_kv_up_l0, mla_q_norm_l0, mla_k_norm_l0, w_pool_l0, pool_scale_l0, w_out_l0, w_ada_l1, b_ada_l1, norm_mix_l1, norm_mlp_l1, w_mlp1_l1, w_mlp2_l1, w_in_l1, na_q_norm_l1, na_k_norm_l1, rel_bias_l1, w_out_l1):
    batch, seq, _ = x_prompt.shape
    dec_batch, dec_seq, _ = x_sample.shape
    past = cache_l0_mla_ckv.shape[1]
    xp = x_prompt.reshape(batch * seq, D_MODEL)
    xs = x_sample.reshape(dec_batch * dec_seq, D_MODEL)
    tiles_p = (batch * seq) // TOKEN_TILE
    tiles_s = dec_seq // TOKEN_TILE

    cond8 = jnp.zeros((SUBLANES, D_MODEL), F32).at[0].set(c_ctx).at[1:1 + dec_batch].set(c)
    mod0_p, mod0_s = _split_mods(_ada(cond8, w_ada_l0, b_ada_l0))
    mod1_p, mod1_s = _split_mods(_ada(cond8, w_ada_l1, b_ada_l1))

    w0 = _prep_l0(w_in_l0, q_lora_norm_l0, kv_lora_norm_l0, w_q_up_l0, w_kv_up_l0, mla_q_norm_l0, mla_k_norm_l0,
                  w_pool_l0, pool_scale_l0, w_out_l0, norm_mix_l0, norm_mlp_l0, w_mlp1_l0, w_mlp2_l0)
    a_p, q_p, k_p, v_p, ckv_p, kr_p = _l0_pre(xp, mod0_p, tiles_p, False, w0, None)
    a_s, q_s, k_s, v_s, _, _ = _l0_pre(xs, mod0_s, tiles_s, True, w0, _rope_tables(dec_seq))
    kr_cache = jnp.pad(cache_l0_mla_krope.reshape(dec_batch * past, MLA_ROPE), ((0, 0), (MLA_NOPE, LANES - MLA_QK)))
    k_c, v_c = _cache_kv(cache_l0_mla_ckv.reshape(dec_batch * past, KV_LORA), kr_cache, w0)
    attn_p = _ctx_attn(_mla_ctx_kernel, q_p, k_p, v_p, seq, MLA_HEADS * MLA_V, "l0_mla_ctx_attn")
    attn_s = _mla_lat_attn(q_s, k_s, v_s, k_c, v_c, dec_seq, past)
    xp = _post(xp, [_pool(a_p, seq, w0), attn_p], mod0_p, tiles_p, w0, "l0_post_ctx")
    xs = _post(xs, [_pool(a_s, dec_seq, w0), attn_s], mod0_s, tiles_s, w0, "l0_post_latent")

    w1 = dict(norm_mix=norm_mix_l1.reshape(1, -1), norm_mlp=norm_mlp_l1.reshape(1, -1),
              w_in=w_in_l1.astype(BF16),
              gq=jnp.tile(na_q_norm_l1 * (NA_HEAD_DIM ** -0.5 * LOG2E), 2).reshape(1, LANES),
              gk=jnp.tile(na_k_norm_l1, 2).reshape(1, LANES),
              w_out=w_out_l1.astype(BF16), w_mlp1=w_mlp1_l1.astype(BF16), w_mlp2=w_mlp2_l1.astype(BF16))
    q1_p, k1_p, v1_p, k1_new, v1_new = _l1_pre(xp, mod1_p, tiles_p, True, w1)
    q1_s, k1_s, v1_s = _l1_pre(xs, mod1_s, tiles_s, False, w1)
    na_p = _ctx_attn(_na_ctx_kernel, q1_p, k1_p, v1_p, seq, NA_SLABS * LANES, "l1_na_ctx_attn")
    na_s = _na_lat_attn(q1_s, k1_s, v1_s,
                        _to_slabs(cache_l1_na_k, dec_batch * past), _to_slabs(cache_l1_na_v, dec_batch * past),
                        _na_bias_table(rel_bias_l1), dec_seq, past)
    xp = _post(xp, [na_p], mod1_p, tiles_p, w1, "l1_post_ctx")
    xs = _post(xs, [na_s], mod1_s, tiles_s, w1, "l1_post_latent")

    return (xp.reshape(batch, seq, D_MODEL), xs.reshape(dec_batch, dec_seq, D_MODEL),
            ckv_p.reshape(batch, seq, KV_LORA), kr_p.reshape(batch, seq, MLA_ROPE),
            k1_new.reshape(batch, seq, NA_HEADS, NA_HEAD_DIM), v1_new.reshape(batch, seq, NA_HEADS, NA_HEAD_DIM))
```

```python
import functools

import jax
import jax.numpy as jnp
from jax import lax
from jax.experimental import pallas as pl
from jax.experimental.pallas import tpu as pltpu

F32 = jnp.float32
BF16 = jnp.bfloat16

D_MODEL = 1024
D_FF = 4 * D_MODEL
N_ADA = 6
EPS = 1e-6
NEG_INF = -1e30
ROPE_THETA = 10000.0
GRID_W = 64
POOL_WIDTH = 512
POOL_WINDOWS = (2, 4, 8, 16)
POOL_GROUP = 128
POOL_HALO = 8
MLA_HEADS = 8
MLA_NOPE = 64
MLA_ROPE = 32
MLA_QK = MLA_NOPE + MLA_ROPE
MLA_V = 64
Q_LORA = 256
KV_LORA = 128
NA_HEADS = 16
NA_HEAD_DIM = 64
NA_KH = 8
NA_KW = 16
LOG2E = 1.4426950408889634

LANES = 128
SUBLANES = 8
VMEM_LIMIT = 56 * 1024 * 1024

TOKEN_TILE = 512
MLA_Q_TILE = 512
MLA_STEP_HEADS = 2
MLA_KEY_CHUNK = 512
FF_CHUNK = 1024
NA_SLABS = NA_HEADS * NA_HEAD_DIM // LANES
NA_TILE_ROWS = 4
NA_WIN_ROWS = NA_TILE_ROWS + NA_KH - 1


def _params(n_axes):
    return pltpu.CompilerParams(dimension_semantics=("arbitrary",) * n_axes,
                                vmem_limit_bytes=VMEM_LIMIT)


def _const_spec(shape):
    zeros = (0,) * len(shape)
    return pl.BlockSpec(shape, lambda *_: zeros, pipeline_mode=pl.Buffered(1))


def _dot(a, b):
    return jnp.dot(a, b, preferred_element_type=F32)


def _dot_nt(a, b):
    return lax.dot_general(a, b, (((1,), (1,)), ((), ())), preferred_element_type=F32)


def _dot_tn(a, b):
    return lax.dot_general(a, b, (((0,), (0,)), ((), ())), preferred_element_type=F32)


def _rms(x, w, n=None):
    n = x.shape[-1] if n is None else n
    ss = jnp.sum(x * x, axis=-1, keepdims=True)
    return x * lax.rsqrt(ss * (1.0 / n) + EPS) * w


def _modulate(x, nw, shift, scale):
    return _rms(x, nw) * (1.0 + scale) + shift


def _ada_kernel(cond_ref, w_ref, b_ref, o_ref):
    c = cond_ref[...]
    s = (c * jax.nn.sigmoid(c)).astype(BF16)
    o_ref[...] = _dot(s, w_ref[...].astype(BF16)) + b_ref[...]


def _ada(cond8, w_ada, b_ada):
    n_out = w_ada.shape[1]
    tn = D_MODEL
    return pl.pallas_call(
        _ada_kernel,
        out_shape=jax.ShapeDtypeStruct((SUBLANES, n_out), F32),
        grid=(n_out // tn,),
        in_specs=[pl.BlockSpec((SUBLANES, D_MODEL), lambda j: (0, 0)),
                  pl.BlockSpec((D_MODEL, tn), lambda j: (0, j)),
                  pl.BlockSpec((1, tn), lambda j: (0, j))],
        out_specs=pl.BlockSpec((SUBLANES, tn), lambda j: (0, j)),
        compiler_params=_params(1),
        name="ada",
    )(cond8, w_ada, b_ada.reshape(1, n_out))


def _rope(x, c, s):
    return x * c + pltpu.roll(x, LANES - MLA_ROPE, 1) * s


def _head_rinv(raw, e2_ref):
    sq = (raw * raw).astype(BF16)
    pair = 2 * LANES
    ss = jnp.concatenate([_dot(sq[:, p:p + pair], e2_ref[...]) for p in range(0, raw.shape[1], pair)], axis=1)
    return lax.rsqrt(ss * (1.0 / MLA_QK) + EPS)


def _mla_keys_values(ckv, kr_slab, wk_ref, gk, wv_ref, e2_ref, rope_tabs, k_ref, v_ref):
    c16 = ckv.astype(BF16)
    kn = _dot(c16, wk_ref[...])
    v = _dot(c16, wv_ref[...])
    rinv = _head_rinv(kn + jnp.tile(kr_slab, (1, MLA_HEADS)), e2_ref)
    tail = kr_slab * gk
    if rope_tabs is not None:
        tail = _rope(tail, *rope_tabs)
    ones_lane = (lax.broadcasted_iota(jnp.int32, (1, LANES), 1) == MLA_V).astype(F32)
    for h in range(MLA_HEADS):
        hs = slice(h * LANES, (h + 1) * LANES)
        k_ref[h] = (rinv[:, hs] * (kn[:, hs] * gk + tail)).astype(BF16)
        v_ref[h] = (v[:, hs] + ones_lane).astype(BF16)


def _l0_pre_kernel(latent, x_ref, mod_ref, nw_ref, win_ref, qln_ref, kvn_ref, wq_ref, gq_ref,
                   wk_ref, gk_ref, wv_ref, e2_ref, *rest):
    if latent:
        c_ref, s_ref, a_ref, q_ref, k_ref, v_ref, ckv_ref, kr_ref = rest
        rope_tabs = (c_ref[...], s_ref[...])
    else:
        a_ref, q_ref, k_ref, v_ref, ckv_ref, kr_ref = rest
        rope_tabs = None
    h = _modulate(x_ref[...], nw_ref[...], mod_ref[0:1, :], mod_ref[1:2, :]).astype(BF16)
    proj = _dot(h, win_ref[...])
    a_ref[...] = proj[:, :POOL_WIDTH]
    q_lat = proj[:, POOL_WIDTH:POOL_WIDTH + Q_LORA]
    kv_lat = proj[:, POOL_WIDTH + Q_LORA:POOL_WIDTH + Q_LORA + KV_LORA]
    kr_slab = proj[:, POOL_WIDTH + Q_LORA + KV_LORA:]
    q = _dot(_rms(q_lat, qln_ref[...]).astype(BF16), wq_ref[...])
    q_rinv = _head_rinv(q, e2_ref)
    gq = gq_ref[...]
    for hd in range(MLA_HEADS):
        hs = slice(hd * LANES, (hd + 1) * LANES)
        y = q[:, hs] * q_rinv[:, hs] * gq
        if rope_tabs is not None:
            y = _rope(y, *rope_tabs)
        q_ref[hd] = y.astype(BF16)
    ckv = _rms(kv_lat, kvn_ref[...])
    ckv_ref[...] = ckv
    kr_ref[...] = kr_slab[:, MLA_NOPE:MLA_QK]
    _mla_keys_values(ckv, kr_slab, wk_ref, gk_ref[...], wv_ref, e2_ref, rope_tabs, k_ref, v_ref)


def _slab_spec(slabs, tm):
    return pl.BlockSpec((slabs, tm, LANES), lambda i: (0, i, 0))


def _l0_pre(x, mod, tiles_per_group, latent, w, rope_tabs):
    n = x.shape[0]
    tm = TOKEN_TILE
    row = lambda i: (i, 0)
    hk = MLA_HEADS * LANES
    in_specs = [pl.BlockSpec((tm, D_MODEL), row),
                pl.BlockSpec((None, SUBLANES, D_MODEL), lambda i: (i // tiles_per_group, 0, 0)),
                _const_spec((1, D_MODEL)),
                _const_spec((D_MODEL, D_MODEL)),
                _const_spec((1, Q_LORA)),
                _const_spec((1, KV_LORA)),
                _const_spec((Q_LORA, hk)),
                _const_spec((1, LANES)),
                _const_spec((KV_LORA, hk)),
                _const_spec((1, LANES)),
                _const_spec((KV_LORA, hk)),
                _const_spec((2 * LANES, 2 * LANES))]
    gains = "rope" if latent else "plain"
    args = [x, mod, w["norm_mix"], w["w_in"], w["qln"], w["kvn"], w["wq"], w["gq_" + gains], w["wk"],
            w["gk_" + gains], w["wv"], w["e2"]]
    if latent:
        seq_tiles = rope_tabs[0].shape[0] // tm
        in_specs += [pl.BlockSpec((tm, LANES), lambda i: (i % seq_tiles, 0))] * 2
        args += list(rope_tabs)
    slab = jax.ShapeDtypeStruct((MLA_HEADS, n, LANES), BF16)
    out_shape = (jax.ShapeDtypeStruct((n, POOL_WIDTH), F32), slab, slab, slab,
                 jax.ShapeDtypeStruct((n, KV_LORA), F32),
                 jax.ShapeDtypeStruct((n, MLA_ROPE), F32))
    out_specs = (pl.BlockSpec((tm, POOL_WIDTH), row),
                 _slab_spec(MLA_HEADS, tm), _slab_spec(MLA_HEADS, tm), _slab_spec(MLA_HEADS, tm),
                 pl.BlockSpec((tm, KV_LORA), row), pl.BlockSpec((tm, MLA_ROPE), row))
    return pl.pallas_call(
        functools.partial(_l0_pre_kernel, latent),
        out_shape=out_shape, grid=(n // tm,), in_specs=in_specs, out_specs=out_specs,
        compiler_params=_params(1), name="l0_pre_latent" if latent else "l0_pre_ctx",
    )(*args)


def _cache_kv_kernel(ckv_ref, kr_ref, wk_ref, gk_ref, wv_ref, e2_ref, k_ref, v_ref):
    _mla_keys_values(ckv_ref[...], kr_ref[...], wk_ref, gk_ref[...], wv_ref, e2_ref, None, k_ref, v_ref)


def _cache_kv(ckv, kr_slab, w):
    n = ckv.shape[0]
    tm = TOKEN_TILE
    row = lambda i: (i, 0)
    hk = MLA_HEADS * LANES
    slab = jax.ShapeDtypeStruct((MLA_HEADS, n, LANES), BF16)
    return pl.pallas_call(
        _cache_kv_kernel,
        out_shape=(slab, slab),
        grid=(n // tm,),
        in_specs=[pl.BlockSpec((tm, KV_LORA), row), pl.BlockSpec((tm, LANES), row),
                  _const_spec((KV_LORA, hk)), _const_spec((1, LANES)), _const_spec((KV_LORA, hk)),
                  _const_spec((2 * LANES, 2 * LANES))],
        out_specs=(_slab_spec(MLA_HEADS, tm), _slab_spec(MLA_HEADS, tm)),
        compiler_params=_params(1), name="l0_cache_kv",
    )(ckv, kr_slab, w["wk"], w["gk_plain"], w["wv"], w["e2"])


def _pool_kernel(seq_len, prev_ref, u_ref, next_ref, wp_ref, ps_ref, o_ref, buf_ref):
    tm = u_ref.shape[0]
    pos0 = (pl.program_id(0) * tm) % seq_len
    buf_ref[0:POOL_HALO, :] = jnp.where(pos0 != 0, prev_ref[...], 0.0)
    buf_ref[POOL_HALO:POOL_HALO + tm, :] = u_ref[...]
    buf_ref[POOL_HALO + tm:, :] = jnp.where(pos0 + tm != seq_len, next_ref[...], 0.0)
    pos = pos0 + lax.broadcasted_iota(jnp.int32, (tm, 1), 0)
    for g, win in enumerate(POOL_WINDOWS):
        lanes = pl.ds(g * POOL_GROUP, POOL_GROUP)
        lo = pos - win // 2
        cnt = (jnp.minimum(lo + win, seq_len) - jnp.maximum(lo, 0)).astype(F32)
        acc = buf_ref[pl.ds(POOL_HALO - win // 2, tm), lanes]
        for d in range(1 - win // 2, win - win // 2):
            acc = acc + buf_ref[pl.ds(POOL_HALO + d, tm), lanes]
        pooled = acc / cnt - u_ref[:, lanes]
        o_ref[:, lanes] = (_dot(pooled.astype(BF16), wp_ref[g]) * ps_ref[:, lanes]).astype(BF16)


def _pool(a_in, seq_len, w):
    n = a_in.shape[0]
    tm = min(TOKEN_TILE, seq_len)
    hb = tm // POOL_HALO
    last = n // POOL_HALO - 1
    return pl.pallas_call(
        functools.partial(_pool_kernel, seq_len),
        out_shape=jax.ShapeDtypeStruct((n, POOL_WIDTH), BF16),
        grid=(n // tm,),
        in_specs=[pl.BlockSpec((POOL_HALO, POOL_WIDTH), lambda i: (jnp.maximum(i * hb - 1, 0), 0)),
                  pl.BlockSpec((tm, POOL_WIDTH), lambda i: (i, 0)),
                  pl.BlockSpec((POOL_HALO, POOL_WIDTH), lambda i: (jnp.minimum((i + 1) * hb, last), 0)),
                  _const_spec((len(POOL_WINDOWS), POOL_GROUP, POOL_GROUP)),
                  _const_spec((1, POOL_WIDTH))],
        out_specs=pl.BlockSpec((tm, POOL_WIDTH), lambda i: (i, 0)),
        scratch_shapes=[pltpu.VMEM((tm + 2 * POOL_HALO, POOL_WIDTH), F32)],
        compiler_params=_params(1), name="l0_pool",
    )(a_in, a_in, a_in, w["w_pool"], w["pool_scale"])


def _attend(streams):
    items = [(si, ch) for si, (_, chunks, _) in enumerate(streams) for ch in chunks]

    def score(item):
        si, (k_fn, _, extra_fn) = item
        s = _dot_nt(k_fn(), streams[si][0])
        return s if extra_fn is None else s + extra_fn()

    state = [None] * len(streams)
    s_next = score(items[0])
    for i, (si, (_, v_fn, _)) in enumerate(items):
        s = s_next
        if i + 1 < len(items):
            s_next = score(items[i + 1])
        want_den = streams[si][2]
        m_new = jnp.max(s, axis=0, keepdims=True)
        if state[si] is not None:
            m, acc, den = state[si]
            m_new = jnp.maximum(m, m_new)
            alpha = jnp.exp2(m - m_new)
        p = jnp.exp2(s - m_new)
        pv = _dot_tn(v_fn(), p.astype(BF16))
        psum = jnp.sum(p, axis=0, keepdims=True) if want_den else None
        if state[si] is None:
            state[si] = (m_new, pv, psum)
        else:
            state[si] = (m_new, acc * alpha + pv, den * alpha + psum if want_den else None)
    return [(acc, den) for _, acc, den in state]


def _ref_chunks(k_ref, v_ref, slab, chunk, extra_fn=None):
    nk = k_ref.shape[1]
    out = []
    for c in range(0, nk, chunk):
        rows = pl.ds(c, min(chunk, nk - c))
        out.append((lambda rows=rows: k_ref[slab, rows, :], lambda rows=rows: v_ref[slab, rows, :], extra_fn))
    return out


def _mla_finish(o_t):
    return o_t[:MLA_V] / o_t[MLA_V:MLA_V + 1]


def _na_queries(q2):
    low = lax.broadcasted_iota(jnp.int32, (1, LANES), 1) < NA_HEAD_DIM
    zero = jnp.zeros_like(q2)
    return jnp.concatenate([jnp.where(low, q2, zero), jnp.where(low, zero, q2)], axis=0)


def _na_finish(o_t, den):
    nq = o_t.shape[1] // 2
    o = o_t / den
    return jnp.concatenate([o[:NA_HEAD_DIM, :nq], o[NA_HEAD_DIM:, nq:]], axis=0)


def _mla_ctx_kernel(q_ref, k_ref, v_ref, o_ref):
    seq = k_ref.shape[1]
    res = _attend([(q_ref[h], _ref_chunks(k_ref, v_ref, h, seq), False) for h in range(MLA_HEADS)])
    for hp in range(MLA_HEADS // 2):
        o_t = jnp.concatenate([_mla_finish(res[h][0]) for h in (2 * hp, 2 * hp + 1)], axis=0)
        o_ref[:, hp * LANES:(hp + 1) * LANES] = o_t.T.astype(BF16)


def _na_ctx_kernel(q_ref, k_ref, v_ref, o_ref):
    seq = k_ref.shape[1]
    res = _attend([(_na_queries(q_ref[s]), _ref_chunks(k_ref, v_ref, s, seq), True) for s in range(NA_SLABS)])
    for s in range(NA_SLABS):
        o_ref[:, s * LANES:(s + 1) * LANES] = _na_finish(*res[s]).T.astype(BF16)


def _ctx_attn(body, q, k, v, seq, width, name):
    slabs, n, _ = q.shape
    spec = pl.BlockSpec((slabs, seq, LANES), lambda b: (0, b, 0))
    return pl.pallas_call(
        body,
        out_shape=jax.ShapeDtypeStruct((n, width), BF16),
        grid=(n // seq,),
        in_specs=[spec, spec, spec],
        out_specs=pl.BlockSpec((seq, width), lambda b: (b, 0)),
        compiler_params=_params(1), name=name,
    )(q, k, v)


def _mla_lat_kernel(q_ref, kl_ref, vl_ref, kc_ref, vc_ref, o_ref):
    res = _attend([(q_ref[h],
                    _ref_chunks(kl_ref, vl_ref, h, MLA_KEY_CHUNK) + _ref_chunks(kc_ref, vc_ref, h, MLA_KEY_CHUNK),
                    False) for h in range(MLA_STEP_HEADS)])
    o_t = jnp.concatenate([_mla_finish(o) for o, _ in res], axis=0)
    o_ref[...] = o_t.T.astype(BF16)


def _mla_lat_attn(q, k, v, kc, vc, seq, past):
    n = q.shape[1]
    tq = MLA_Q_TILE
    nq = seq // tq
    hs = MLA_STEP_HEADS
    return pl.pallas_call(
        _mla_lat_kernel,
        out_shape=jax.ShapeDtypeStruct((n, MLA_HEADS * MLA_V), BF16),
        grid=(n // seq, MLA_HEADS // hs, nq),
        in_specs=[pl.BlockSpec((hs, tq, LANES), lambda b, h, i: (h, b * nq + i, 0)),
                  pl.BlockSpec((hs, seq, LANES), lambda b, h, i: (h, b, 0)),
                  pl.BlockSpec((hs, seq, LANES), lambda b, h, i: (h, b, 0)),
                  pl.BlockSpec((hs, past, LANES), lambda b, h, i: (h, b, 0)),
                  pl.BlockSpec((hs, past, LANES), lambda b, h, i: (h, b, 0))],
        out_specs=pl.BlockSpec((tq, hs * MLA_V), lambda b, h, i: (b * nq + i, h)),
        compiler_params=_params(3), name="l0_mla_latent_attn",
    )(q, k, v, kc, vc)


def _na_first_row(r, rows):
    return jnp.clip(r - NA_KH // 2, 0, rows - NA_KH)


def _na_lat_kernel(rows, q_ref, k_ref, v_ref, kc_ref, vc_ref, t2_ref, o_ref):
    nq = NA_TILE_ROWS * GRID_W
    nk = NA_WIN_ROWS * GRID_W
    r0 = pl.program_id(1) * NA_TILE_ROWS
    w0 = jnp.clip(r0 - NA_KH // 2, 0, rows - NA_WIN_ROWS)
    key_row = w0 + lax.broadcasted_iota(jnp.int32, (nk, nq), 0) // GRID_W
    first = _na_first_row(r0 + lax.broadcasted_iota(jnp.int32, (nk, nq), 1) // GRID_W, rows)
    row_mask = jnp.where((key_row >= first) & (key_row < first + NA_KH), 0.0, NEG_INF)
    idx = [[jnp.clip(w0 + j - (r0 + 2 * t) + NA_KH - 1, 0, 2 * NA_KH - 2) for t in range(NA_TILE_ROWS // 2)]
           for j in range(NA_WIN_ROWS)]

    def local_bias(s):
        heads = [jnp.concatenate([jnp.concatenate([t2_ref[2 * s + hh, i] for i in row], axis=1) for row in idx], axis=0)
                 + row_mask for hh in range(2)]
        return jnp.concatenate(heads, axis=1)

    res = _attend([(_na_queries(q_ref[s]),
                    _ref_chunks(k_ref, v_ref, s, nk, functools.partial(local_bias, s)) + _ref_chunks(kc_ref, vc_ref, s, nk),
                    True) for s in range(NA_SLABS)])
    for s in range(NA_SLABS):
        o_ref[:, s * LANES:(s + 1) * LANES] = _na_finish(*res[s]).T.astype(BF16)


def _na_lat_attn(q, k, v, kc, vc, t2, seq, past):
    slabs, n, _ = q.shape
    rows = seq // GRID_W
    tiles = rows // NA_TILE_ROWS
    nq = NA_TILE_ROWS * GRID_W
    nk = NA_WIN_ROWS * GRID_W

    def win_map(b, t):
        w0 = jnp.clip(t * NA_TILE_ROWS - NA_KH // 2, 0, rows - NA_WIN_ROWS)
        return (0, pl.multiple_of(b * seq + w0 * GRID_W, GRID_W), 0)

    win_spec = pl.BlockSpec((pl.Element(slabs), pl.Element(nk), pl.Element(LANES)), win_map)
    ctx_spec = pl.BlockSpec((slabs, past, LANES), lambda b, t: (0, b, 0))
    return pl.pallas_call(
        functools.partial(_na_lat_kernel, rows),
        out_shape=jax.ShapeDtypeStruct((n, slabs * LANES), BF16),
        grid=(n // seq, tiles),
        in_specs=[pl.BlockSpec((slabs, nq, LANES), lambda b, t: (0, b * tiles + t, 0)),
                  win_spec, win_spec, ctx_spec, ctx_spec,
                  _const_spec(t2.shape)],
        out_specs=pl.BlockSpec((nq, slabs * LANES), lambda b, t: (b * tiles + t, 0)),
        compiler_params=_params(2), name="l1_na_latent_attn",
    )(q, k, v, kc, vc, t2)


def _pair_norm(x2, g2):
    low = lax.broadcasted_iota(jnp.int32, (1, LANES), 1) < NA_HEAD_DIM
    sq = x2 * x2
    ss_all = jnp.sum(sq, axis=-1, keepdims=True)
    ss_low = jnp.sum(jnp.where(low, sq, 0.0), axis=-1, keepdims=True)
    ss = jnp.where(low, ss_low, ss_all - ss_low)
    return x2 * lax.rsqrt(ss * (1.0 / NA_HEAD_DIM) + EPS) * g2


def _l1_pre_kernel(keep_f32, x_ref, mod_ref, nw_ref, win_ref, gq_ref, gk_ref, q_ref, k_ref, v_ref, *f32_refs):
    h = _modulate(x_ref[...], nw_ref[...], mod_ref[0:1, :], mod_ref[1:2, :]).astype(BF16)
    width = NA_SLABS * LANES
    q = _dot(h, win_ref[:, 0:width])
    k = _dot(h, win_ref[:, width:2 * width])
    v = _dot(h, win_ref[:, 2 * width:3 * width])
    gq = gq_ref[...]
    gk = gk_ref[...]
    for s in range(NA_SLABS):
        ls = slice(s * LANES, (s + 1) * LANES)
        q_ref[s] = _pair_norm(q[:, ls], gq).astype(BF16)
        kn = _pair_norm(k[:, ls], gk)
        k_ref[s] = kn.astype(BF16)
        v_ref[s] = v[:, ls].astype(BF16)
        if keep_f32:
            f32_refs[0][:, ls] = kn
    if keep_f32:
        f32_refs[1][...] = v


def _l1_pre(x, mod, tiles_per_group, keep_f32, w):
    n = x.shape[0]
    tm = TOKEN_TILE
    row = lambda i: (i, 0)
    width = NA_SLABS * LANES
    slab = jax.ShapeDtypeStruct((NA_SLABS, n, LANES), BF16)
    out_shape = [slab, slab, slab]
    out_specs = [_slab_spec(NA_SLABS, tm)] * 3
    if keep_f32:
        out_shape += [jax.ShapeDtypeStruct((n, width), F32)] * 2
        out_specs += [pl.BlockSpec((tm, width), row)] * 2
    return pl.pallas_call(
        functools.partial(_l1_pre_kernel, keep_f32),
        out_shape=tuple(out_shape), grid=(n // tm,),
        in_specs=[pl.BlockSpec((tm, D_MODEL), row),
                  pl.BlockSpec((None, SUBLANES, D_MODEL), lambda i: (i // tiles_per_group, 0, 0)),
                  _const_spec((1, D_MODEL)),
                  _const_spec((D_MODEL, 3 * width)),
                  _const_spec((1, LANES)),
                  _const_spec((1, LANES))],
        out_specs=tuple(out_specs),
        compiler_params=_params(1), name="l1_pre_ctx" if keep_f32 else "l1_pre_latent",
    )(x, mod, w["norm_mix"], w["w_in"], w["gq"], w["gk"])


def _post_kernel(n_parts, x_ref, *rest):
    parts = rest[:n_parts]
    mod_ref, nw_ref, wo_ref, w1_ref, w2_ref, o_ref = rest[n_parts:]
    mix_in = parts[0][...] if n_parts == 1 else jnp.concatenate([p[...] for p in parts], axis=-1)
    x1 = x_ref[...] + mod_ref[2:3, :] * _dot(mix_in, wo_ref[...])
    h = _modulate(x1, nw_ref[...], mod_ref[3:4, :], mod_ref[4:5, :]).astype(BF16)
    acc = jnp.zeros(x1.shape, F32)
    for c in range(D_FF // FF_CHUNK):
        cs = pl.ds(c * FF_CHUNK, FF_CHUNK)
        u = jnp.square(jnp.maximum(_dot(h, w1_ref[:, cs]), 0.0)).astype(BF16)
        acc = acc + _dot(u, w2_ref[cs, :])
    o_ref[...] = x1 + mod_ref[5:6, :] * acc


def _post(x, parts, mod, tiles_per_group, w, name):
    n = x.shape[0]
    tm = TOKEN_TILE
    row = lambda i: (i, 0)
    in_specs = [pl.BlockSpec((tm, D_MODEL), row)]
    in_specs += [pl.BlockSpec((tm, p.shape[1]), row) for p in parts]
    in_specs += [pl.BlockSpec((None, SUBLANES, D_MODEL), lambda i: (i // tiles_per_group, 0, 0)),
                 _const_spec((1, D_MODEL)),
                 _const_spec((D_MODEL, D_MODEL)),
                 _const_spec((D_MODEL, D_FF)),
                 _const_spec((D_FF, D_MODEL))]
    return pl.pallas_call(
        functools.partial(_post_kernel, len(parts)),
        out_shape=jax.ShapeDtypeStruct((n, D_MODEL), F32), grid=(n // tm,),
        in_specs=in_specs, out_specs=pl.BlockSpec((tm, D_MODEL), row),
        compiler_params=_params(1), name=name,
    )(x, *parts, mod, w["norm_mlp"], w["w_out"], w["w_mlp1"], w["w_mlp2"])


def _rope_tables(seq):
    half = MLA_ROPE // 2
    inv_freq = jnp.power(ROPE_THETA, -jnp.arange(0, half, 2, dtype=F32) / half)
    ang = jnp.arange(GRID_W, dtype=F32)[:, None] * inv_freq[None, :]
    cos, sin = jnp.cos(ang), jnp.sin(ang)
    rows = seq // GRID_W
    by_row = lambda t: jnp.broadcast_to(t[:rows, None, :], (rows, GRID_W, half // 2)).reshape(seq, half // 2)
    by_col = lambda t: jnp.broadcast_to(t[None, :, :], (rows, GRID_W, half // 2)).reshape(seq, half // 2)
    cr, sr, cc, sc = by_row(cos), by_row(sin), by_col(cos), by_col(sin)
    pad = LANES - MLA_QK
    c = jnp.concatenate([jnp.ones((seq, MLA_NOPE), F32), cr, cr, cc, cc, jnp.zeros((seq, pad), F32)], axis=1)
    s = jnp.concatenate([jnp.zeros((seq, MLA_NOPE), F32), -sr, sr, -sc, sc, jnp.zeros((seq, pad), F32)], axis=1)
    return c, s


def _rope_partner(tail):
    shape = tail.shape
    t = tail.reshape(shape[:-1] + (2, 2, MLA_ROPE // 4))
    return t[..., ::-1, :].reshape(shape)


def _na_bias_table(rel_bias):
    qcol = jnp.arange(GRID_W)
    kcol = jnp.arange(GRID_W)
    col_start = jnp.clip(qcol - NA_KW // 2, 0, GRID_W - NA_KW)
    col_mask = (kcol[:, None] >= col_start[None, :]) & (kcol[:, None] < col_start[None, :] + NA_KW)
    edge = GRID_W - NA_KW
    padded = jnp.pad(rel_bias.astype(F32) * LOG2E, ((0, 0), (0, 0), (edge, edge)))
    blocks = jnp.stack([padded[:, :, GRID_W - 1 - q:2 * GRID_W - 1 - q] for q in range(GRID_W)], axis=-1)
    blocks = jnp.where(col_mask[None, None], blocks, NEG_INF)
    below = jnp.concatenate([jnp.full_like(blocks[:, :1], NEG_INF), blocks[:, :-1]], axis=1)
    return jnp.concatenate([blocks, below], axis=-1)


def _prep_l0(w_in, q_lora_norm, kv_lora_norm, w_q_up, w_kv_up, mla_q_norm, mla_k_norm, w_pool, pool_scale,
             w_out, norm_mix, norm_mlp, w_mlp1, w_mlp2):
    lat_end = POOL_WIDTH + Q_LORA + KV_LORA
    w_in_pad = jnp.zeros((D_MODEL, D_MODEL), F32)
    w_in_pad = w_in_pad.at[:, :lat_end].set(w_in[:, :lat_end])
    w_in_pad = w_in_pad.at[:, lat_end + MLA_NOPE:lat_end + MLA_QK].set(w_in[:, lat_end:])
    w_in_pad = w_in_pad.at[:, lat_end + MLA_QK:].set(_rope_partner(w_in[:, lat_end:]))
    pad = LANES - MLA_QK
    hk = MLA_HEADS * LANES
    wq = w_q_up.reshape(Q_LORA, MLA_HEADS, MLA_QK)
    wq = jnp.concatenate([wq, _rope_partner(wq[:, :, MLA_NOPE:])], axis=-1)
    kv = w_kv_up.reshape(KV_LORA, MLA_HEADS, MLA_NOPE + MLA_V)
    wk = jnp.pad(kv[:, :, :MLA_NOPE], ((0, 0), (0, 0), (0, LANES - MLA_NOPE)))
    wv = jnp.pad(kv[:, :, MLA_NOPE:], ((0, 0), (0, 0), (0, LANES - MLA_V)))
    gq = mla_q_norm * (MLA_QK ** -0.5 * LOG2E)
    slab_lane = jnp.arange(2 * LANES)
    e2 = (slab_lane[:, None] // LANES == slab_lane[None, :] // LANES) & (slab_lane[:, None] % LANES < MLA_QK)
    return dict(
        norm_mix=norm_mix.reshape(1, -1), norm_mlp=norm_mlp.reshape(1, -1),
        w_in=w_in_pad.astype(BF16),
        qln=q_lora_norm.reshape(1, -1), kvn=kv_lora_norm.reshape(1, -1),
        wq=wq.reshape(Q_LORA, hk).astype(BF16),
        gq_plain=jnp.pad(gq, (0, pad)).reshape(1, LANES),
        gq_rope=jnp.concatenate([gq, _rope_partner(gq[MLA_NOPE:])]).reshape(1, LANES),
        wk=wk.reshape(KV_LORA, hk).astype(BF16),
        gk_plain=jnp.pad(mla_k_norm, (0, pad)).reshape(1, LANES),
        gk_rope=jnp.concatenate([mla_k_norm, _rope_partner(mla_k_norm[MLA_NOPE:])]).reshape(1, LANES),
        wv=wv.reshape(KV_LORA, hk).astype(BF16),
        e2=e2.astype(BF16),
        w_pool=w_pool.astype(BF16), pool_scale=pool_scale.reshape(1, -1),
        w_out=w_out.astype(BF16), w_mlp1=w_mlp1.astype(BF16), w_mlp2=w_mlp2.astype(BF16))


def _split_mods(m):
    mods = m.reshape(SUBLANES, N_ADA, D_MODEL)[:3]
    mods = jnp.pad(mods, ((0, 0), (0, SUBLANES - N_ADA), (0, 0)))
    return mods[0:1], mods[1:3]


def _to_slabs(x, n):
    return x.reshape(n, -1, LANES).transpose(1, 0, 2).astype(BF16)


def kernel(x_prompt, x_sample, cache_l0_mla_ckv, cache_l0_mla_krope, cache_l1_na_k, cache_l1_na_v, c, c_ctx, w_ada_l0, b_ada_l0, norm_mix_l0, norm_mlp_l0, w_mlp1_l0, w_mlp2_l0, w_in_l0, q_lora_norm_l0, kv_lora_norm_l0, w_q_up_l0, w_kv_up_l0, mla_q_norm_l0, mla_k_norm_l0, w_pool_l0, pool_scale_l0, w_out_l0, w_ada_l1, b_ada_l1, norm_mix_l1, norm_mlp_l1, w_mlp1_l1, w_mlp2_l1, w_in_l1, na_q_norm_l1, na_k_norm_l1, rel_bias_l1, w_out_l1):
    batch, seq, _ = x_prompt.shape
    dec_batch, dec_seq, _ = x_sample.shape
    past = cache_l0_mla_ckv.shape[1]
    xp = x_prompt.reshape(batch * seq, D_MODEL)
    xs = x_sample.reshape(dec_batch * dec_seq, D_MODEL)
    tiles_p = (batch * seq) // TOKEN_TILE
    tiles_s = dec_seq // TOKEN_TILE

    cond8 = jnp.zeros((SUBLANES, D_MODEL), F32).at[0].set(c_ctx).at[1:1 + dec_batch].set(c)
    mod0_p, mod0_s = _split_mods(_ada(cond8, w_ada_l0, b_ada_l0))
    mod1_p, mod1_s = _split_mods(_ada(cond8, w_ada_l1, b_ada_l1))

    w0 = _prep_l0(w_in_l0, q_lora_norm_l0, kv_lora_norm_l0, w_q_up_l0, w_kv_up_l0, mla_q_norm_l0, mla_k_norm_l0,
                  w_pool_l0, pool_scale_l0, w_out_l0, norm_mix_l0, norm_mlp_l0, w_mlp1_l0, w_mlp2_l0)
    a_p, q_p, k_p, v_p, ckv_p, kr_p = _l0_pre(xp, mod0_p, tiles_p, False, w0, None)
    a_s, q_s, k_s, v_s, _, _ = _l0_pre(xs, mod0_s, tiles_s, True, w0, _rope_tables(dec_seq))
    kr_cache = jnp.pad(cache_l0_mla_krope.reshape(dec_batch * past, MLA_ROPE), ((0, 0), (MLA_NOPE, LANES - MLA_QK)))
    k_c, v_c = _cache_kv(cache_l0_mla_ckv.reshape(dec_batch * past, KV_LORA), kr_cache, w0)
    attn_p = _ctx_attn(_mla_ctx_kernel, q_p, k_p, v_p, seq, MLA_HEADS * MLA_V, "l0_mla_ctx_attn")
    attn_s = _mla_lat_attn(q_s, k_s, v_s, k_c, v_c, dec_seq, past)
    xp = _post(xp, [_pool(a_p, seq, w0), attn_p], mod0_p, tiles_p, w0, "l0_post_ctx")
    xs = _post(xs, [_pool(a_s, dec_seq, w0), attn_s], mod0_s, tiles_s, w0, "l0_post_latent")

    w1 = dict(norm_mix=norm_mix_l1.reshape(1, -1), norm_mlp=norm_mlp_l1.reshape(1, -1),
              w_in=w_in_l1.astype(BF16),
              gq=jnp.tile(na_q_norm_l1 * (NA_HEAD_DIM ** -0.5 * LOG2E), 2).reshape(1, LANES),
              gk=jnp.tile(na_k_norm_l1, 2).reshape(1, LANES),
              w_out=w_out_l1.astype(BF16), w_mlp1=w_mlp1_l1.astype(BF16), w_mlp2=w_mlp2_l1.astype(BF16))
    q1_p, k1_p, v1_p, k1_new, v1_new = _l1_pre(xp, mod1_p, tiles_p, True, w1)
    q1_s, k1_s, v1_s = _l1_pre(xs, mod1_s, tiles_s, False, w1)
    na_p = _ctx_attn(_na_ctx_kernel, q1_p, k1_p, v1_p, seq, NA_SLABS * LANES, "l1_na_ctx_attn")
    na_s = _na_lat_attn(q1_s, k1_s, v1_s,
                        _to_slabs(cache_l1_na_k, dec_batch * past), _to_slabs(cache_l1_na_v, dec_batch * past),
                        _na_bias_table(rel_bias_l1), dec_seq, past)
    xp = _post(xp, [na_p], mod1_p, tiles_p, w1, "l1_post_ctx")
    xs = _post(xs, [na_s], mod1_s, tiles_s, w1, "l1_post_latent")

    return (xp.reshape(batch, seq, D_MODEL), xs.reshape(dec_batch, dec_seq, D_MODEL),
            ckv_p.reshape(batch, seq, KV_LORA), kr_p.reshape(batch, seq, MLA_ROPE),
            k1_new.reshape(batch, seq, NA_HEADS, NA_HEAD_DIM), v1_new.reshape(batch, seq, NA_HEADS, NA_HEAD_DIM))
```

```python
import functools

import jax
import jax.numpy as jnp
from jax import lax
from jax.experimental import pallas as pl
from jax.experimental.pallas import tpu as pltpu

F32 = jnp.float32
BF16 = jnp.bfloat16

D_MODEL = 1024
D_FF = 4 * D_MODEL
N_ADA = 6
EPS = 1e-6
NEG_INF = -1e30
ROPE_THETA = 10000.0
GRID_W = 64
POOL_WIDTH = 512
POOL_WINDOWS = (2, 4, 8, 16)
POOL_GROUP = 128
POOL_HALO = 8
MLA_HEADS = 8
MLA_NOPE = 64
MLA_ROPE = 32
MLA_QK = MLA_NOPE + MLA_ROPE
MLA_V = 64
Q_LORA = 256
KV_LORA = 128
NA_HEADS = 16
NA_HEAD_DIM = 64
NA_KH = 8
NA_KW = 16
LOG2E = 1.4426950408889634

LANES = 128
SUBLANES = 8
VMEM_LIMIT = 56 * 1024 * 1024

TOKEN_TILE = 512
MLA_Q_TILE = 512
MLA_STEP_HEADS = 2
MLA_KEY_CHUNK = 512
FF_CHUNK = 1024
NA_SLABS = NA_HEADS * NA_HEAD_DIM // LANES
NA_TILE_ROWS = 4
NA_WIN_ROWS = NA_TILE_ROWS + NA_KH - 1


def _params(n_axes):
    return pltpu.CompilerParams(dimension_semantics=("arbitrary",) * n_axes,
                                vmem_limit_bytes=VMEM_LIMIT)


def _const_spec(shape):
    zeros = (0,) * len(shape)
    return pl.BlockSpec(shape, lambda *_: zeros, pipeline_mode=pl.Buffered(1))


def _dot(a, b):
    return jnp.dot(a, b, preferred_element_type=F32)


def _dot_nt(a, b):
    return lax.dot_general(a, b, (((1,), (1,)), ((), ())), preferred_element_type=F32)


def _dot_tn(a, b):
    return lax.dot_general(a, b, (((0,), (0,)), ((), ())), preferred_element_type=F32)


def _rms(x, w, n=None):
    n = x.shape[-1] if n is None else n
    ss = jnp.sum(x * x, axis=-1, keepdims=True)
    return x * lax.rsqrt(ss * (1.0 / n) + EPS) * w


def _modulate(x, nw, shift, scale):
    return _rms(x, nw) * (1.0 + scale) + shift


def _ada_kernel(cond_ref, w_ref, b_ref, o_ref):
    c = cond_ref[...]
    s = (c * jax.nn.sigmoid(c)).astype(BF16)
    o_ref[...] = _dot(s, w_ref[...].astype(BF16)) + b_ref[...]


def _ada(cond8, w_ada, b_ada):
    n_out = w_ada.shape[1]
    tn = D_MODEL
    return pl.pallas_call(
        _ada_kernel,
        out_shape=jax.ShapeDtypeStruct((SUBLANES, n_out), F32),
        grid=(n_out // tn,),
        in_specs=[pl.BlockSpec((SUBLANES, D_MODEL), lambda j: (0, 0)),
                  pl.BlockSpec((D_MODEL, tn), lambda j: (0, j)),
                  pl.BlockSpec((1, tn), lambda j: (0, j))],
        out_specs=pl.BlockSpec((SUBLANES, tn), lambda j: (0, j)),
        compiler_params=_params(1),
        name="ada",
    )(cond8, w_ada, b_ada.reshape(1, n_out))


def _rope(x, c, s):
    return x * c + pltpu.roll(x, LANES - MLA_ROPE, 1) * s


def _head_rinv(raw, e2_ref):
    sq = (raw * raw).astype(BF16)
    pair = 2 * LANES
    ss = jnp.concatenate([_dot(sq[:, p:p + pair], e2_ref[...]) for p in range(0, raw.shape[1], pair)], axis=1)
    return lax.rsqrt(ss * (1.0 / MLA_QK) + EPS)


def _mla_keys_values(ckv, kr_slab, wk_ref, gk, wv_ref, e2_ref, rope_tabs, k_ref, v_ref):
    c16 = ckv.astype(BF16)
    kn = _dot(c16, wk_ref[...])
    v = _dot(c16, wv_ref[...])
    rinv = _head_rinv(kn + jnp.tile(kr_slab, (1, MLA_HEADS)), e2_ref)
    tail = kr_slab * gk
    if rope_tabs is not None:
        tail = _rope(tail, *rope_tabs)
    ones_lane = (lax.broadcasted_iota(jnp.int32, (1, LANES), 1) == MLA_V).astype(F32)
    for h in range(MLA_HEADS):
        hs = slice(h * LANES, (h + 1) * LANES)
        k_ref[h] = (rinv[:, hs] * (kn[:, hs] * gk + tail)).astype(BF16)
        v_ref[h] = (v[:, hs] + ones_lane).astype(BF16)


def _l0_pre_kernel(latent, x_ref, mod_ref, nw_ref, win_ref, qln_ref, kvn_ref, wq_ref, gq_ref,
                   wk_ref, gk_ref, wv_ref, e2_ref, *rest):
    if latent:
        c_ref, s_ref, a_ref, q_ref, k_ref, v_ref, ckv_ref, kr_ref = rest
        rope_tabs = (c_ref[...], s_ref[...])
    else:
        a_ref, q_ref, k_ref, v_ref, ckv_ref, kr_ref = rest
        rope_tabs = None
    h = _modulate(x_ref[...], nw_ref[...], mod_ref[0:1, :], mod_ref[1:2, :]).astype(BF16)
    proj = _dot(h, win_ref[...])
    a_ref[...] = proj[:, :POOL_WIDTH]
    q_lat = proj[:, POOL_WIDTH:POOL_WIDTH + Q_LORA]
    kv_lat = proj[:, POOL_WIDTH + Q_LORA:POOL_WIDTH + Q_LORA + KV_LORA]
    kr_slab = proj[:, POOL_WIDTH + Q_LORA + KV_LORA:]
    q = _dot(_rms(q_lat, qln_ref[...]).astype(BF16), wq_ref[...])
    q_rinv = _head_rinv(q, e2_ref)
    gq = gq_ref[...]
    for hd in range(MLA_HEADS):
        hs = slice(hd * LANES, (hd + 1) * LANES)
        y = q[:, hs] * q_rinv[:, hs] * gq
        if rope_tabs is not None:
            y = _rope(y, *rope_tabs)
        q_ref[hd] = y.astype(BF16)
    ckv = _rms(kv_lat, kvn_ref[...])
    ckv_ref[...] = ckv
    kr_ref[...] = kr_slab[:, MLA_NOPE:MLA_QK]
    _mla_keys_values(ckv, kr_slab, wk_ref, gk_ref[...], wv_ref, e2_ref, rope_tabs, k_ref, v_ref)


def _slab_spec(slabs, tm):
    return pl.BlockSpec((slabs, tm, LANES), lambda i: (0, i, 0))


def _l0_pre(x, mod, tiles_per_group, latent, w, rope_tabs):
    n = x.shape[0]
    tm = TOKEN_TILE
    row = lambda i: (i, 0)
    hk = MLA_HEADS * LANES
    in_specs = [pl.BlockSpec((tm, D_MODEL), row),
                pl.BlockSpec((None, SUBLANES, D_MODEL), lambda i: (i // tiles_per_group, 0, 0)),
                _const_spec((1, D_MODEL)),
                _const_spec((D_MODEL, D_MODEL)),
                _const_spec((1, Q_LORA)),
                _const_spec((1, KV_LORA)),
                _const_spec((Q_LORA, hk)),
                _const_spec((1, LANES)),
                _const_spec((KV_LORA, hk)),
                _const_spec((1, LANES)),
                _const_spec((KV_LORA, hk)),
                _const_spec((2 * LANES, 2 * LANES))]
    gains = "rope" if latent else "plain"
    args = [x, mod, w["norm_mix"], w["w_in"], w["qln"], w["kvn"], w["wq"], w["gq_" + gains], w["wk"],
            w["gk_" + gains], w["wv"], w["e2"]]
    if latent:
        seq_tiles = rope_tabs[0].shape[0] // tm
        in_specs += [pl.BlockSpec((tm, LANES), lambda i: (i % seq_tiles, 0))] * 2
        args += list(rope_tabs)
    slab = jax.ShapeDtypeStruct((MLA_HEADS, n, LANES), BF16)
    out_shape = (jax.ShapeDtypeStruct((n, POOL_WIDTH), F32), slab, slab, slab,
                 jax.ShapeDtypeStruct((n, KV_LORA), F32),
                 jax.ShapeDtypeStruct((n, MLA_ROPE), F32))
    out_specs = (pl.BlockSpec((tm, POOL_WIDTH), row),
                 _slab_spec(MLA_HEADS, tm), _slab_spec(MLA_HEADS, tm), _slab_spec(MLA_HEADS, tm),
                 pl.BlockSpec((tm, KV_LORA), row), pl.BlockSpec((tm, MLA_ROPE), row))
    return pl.pallas_call(
        functools.partial(_l0_pre_kernel, latent),
        out_shape=out_shape, grid=(n // tm,), in_specs=in_specs, out_specs=out_specs,
        compiler_params=_params(1), name="l0_pre_latent" if latent else "l0_pre_ctx",
    )(*args)


def _cache_kv_kernel(ckv_ref, kr_ref, wk_ref, gk_ref, wv_ref, e2_ref, k_ref, v_ref):
    _mla_keys_values(ckv_ref[...], kr_ref[...], wk_ref, gk_ref[...], wv_ref, e2_ref, None, k_ref, v_ref)


def _cache_kv(ckv, kr_slab, w):
    n = ckv.shape[0]
    tm = TOKEN_TILE
    row = lambda i: (i, 0)
    hk = MLA_HEADS * LANES
    slab = jax.ShapeDtypeStruct((MLA_HEADS, n, LANES), BF16)
    return pl.pallas_call(
        _cache_kv_kernel,
        out_shape=(slab, slab),
        grid=(n // tm,),
        in_specs=[pl.BlockSpec((tm, KV_LORA), row), pl.BlockSpec((tm, LANES), row),
                  _const_spec((KV_LORA, hk)), _const_spec((1, LANES)), _const_spec((KV_LORA, hk)),
                  _const_spec((2 * LANES, 2 * LANES))],
        out_specs=(_slab_spec(MLA_HEADS, tm), _slab_spec(MLA_HEADS, tm)),
        compiler_params=_params(1), name="l0_cache_kv",
    )(ckv, kr_slab, w["wk"], w["gk_plain"], w["wv"], w["e2"])


def _pool_kernel(seq_len, prev_ref, u_ref, next_ref, wp_ref, ps_ref, o_ref, buf_ref):
    tm = u_ref.shape[0]
    pos0 = (pl.program_id(0) * tm) % seq_len
    buf_ref[0:POOL_HALO, :] = jnp.where(pos0 != 0, prev_ref[...], 0.0)
    buf_ref[POOL_HALO:POOL_HALO + tm, :] = u_ref[...]
    buf_ref[POOL_HALO + tm:, :] = jnp.where(pos0 + tm != seq_len, next_ref[...], 0.0)
    pos = pos0 + lax.broadcasted_iota(jnp.int32, (tm, 1), 0)
    for g, win in enumerate(POOL_WINDOWS):
        lanes = pl.ds(g * POOL_GROUP, POOL_GROUP)
        lo = pos - win // 2
        cnt = (jnp.minimum(lo + win, seq_len) - jnp.maximum(lo, 0)).astype(F32)
        acc = buf_ref[pl.ds(POOL_HALO - win // 2, tm), lanes]
        for d in range(1 - win // 2, win - win // 2):
            acc = acc + buf_ref[pl.ds(POOL_HALO + d, tm), lanes]
        pooled = acc / cnt - u_ref[:, lanes]
        o_ref[:, lanes] = (_dot(pooled.astype(BF16), wp_ref[g]) * ps_ref[:, lanes]).astype(BF16)


def _pool(a_in, seq_len, w):
    n = a_in.shape[0]
    tm = min(TOKEN_TILE, seq_len)
    hb = tm // POOL_HALO
    last = n // POOL_HALO - 1
    return pl.pallas_call(
        functools.partial(_pool_kernel, seq_len),
        out_shape=jax.ShapeDtypeStruct((n, POOL_WIDTH), BF16),
        grid=(n // tm,),
        in_specs=[pl.BlockSpec((POOL_HALO, POOL_WIDTH), lambda i: (jnp.maximum(i * hb - 1, 0), 0)),
                  pl.BlockSpec((tm, POOL_WIDTH), lambda i: (i, 0)),
                  pl.BlockSpec((POOL_HALO, POOL_WIDTH), lambda i: (jnp.minimum((i + 1) * hb, last), 0)),
                  _const_spec((len(POOL_WINDOWS), POOL_GROUP, POOL_GROUP)),
                  _const_spec((1, POOL_WIDTH))],
        out_specs=pl.BlockSpec((tm, POOL_WIDTH), lambda i: (i, 0)),
        scratch_shapes=[pltpu.VMEM((tm + 2 * POOL_HALO, POOL_WIDTH), F32)],
        compiler_params=_params(1), name="l0_pool",
    )(a_in, a_in, a_in, w["w_pool"], w["pool_scale"])


def _attend(streams):
    items = [(si, ch) for si, (_, chunks, _) in enumerate(streams) for ch in chunks]

    def score(item):
        si, (k_fn, _, extra_fn) = item
        s = _dot_nt(k_fn(), streams[si][0])
        return s if extra_fn is None else s + extra_fn()

    state = [None] * len(streams)
    s_next = score(items[0])
    for i, (si, (_, v_fn, _)) in enumerate(items):
        s = s_next
        if i + 1 < len(items):
            s_next = score(items[i + 1])
        want_den = streams[si][2]
        m_new = jnp.max(s, axis=0, keepdims=True)
        if state[si] is not None:
            m, acc, den = state[si]
            m_new = jnp.maximum(m, m_new)
            alpha = jnp.exp2(m - m_new)
        p = jnp.exp2(s - m_new)
        pv = _dot_tn(v_fn(), p.astype(BF16))
        psum = jnp.sum(p, axis=0, keepdims=True) if want_den else None
        if state[si] is None:
            state[si] = (m_new, pv, psum)
        else:
            state[si] = (m_new, acc * alpha + pv, den * alpha + psum if want_den else None)
    return [(acc, den) for _, acc, den in state]


def _ref_chunks(k_ref, v_ref, slab, chunk, extra_fn=None):
    nk = k_ref.shape[1]
    out = []
    for c in range(0, nk, chunk):
        rows = pl.ds(c, min(chunk, nk - c))
        out.append((lambda rows=rows: k_ref[slab, rows, :], lambda rows=rows: v_ref[slab, rows, :], extra_fn))
    return out


def _mla_finish(o_t):
    return o_t[:MLA_V] / o_t[MLA_V:MLA_V + 1]


def _na_queries(q2):
    low = lax.broadcasted_iota(jnp.int32, (1, LANES), 1) < NA_HEAD_DIM
    zero = jnp.zeros_like(q2)
    return jnp.concatenate([jnp.where(low, q2, zero), jnp.where(low, zero, q2)], axis=0)


def _na_finish(o_t, den):
    nq = o_t.shape[1] // 2
    o = o_t / den
    return jnp.concatenate([o[:NA_HEAD_DIM, :nq], o[NA_HEAD_DIM:, nq:]], axis=0)


def _mla_ctx_kernel(q_ref, k_ref, v_ref, o_ref):
    seq = k_ref.shape[1]
    res = _attend([(q_ref[h], _ref_chunks(k_ref, v_ref, h, seq), False) for h in range(MLA_HEADS)])
    for hp in range(MLA_HEADS // 2):
        o_t = jnp.concatenate([_mla_finish(res[h][0]) for h in (2 * hp, 2 * hp + 1)], axis=0)
        o_ref[:, hp * LANES:(hp + 1) * LANES] = o_t.T.astype(BF16)


def _na_ctx_kernel(q_ref, k_ref, v_ref, o_ref):
    seq = k_ref.shape[1]
    res = _attend([(_na_queries(q_ref[s]), _ref_chunks(k_ref, v_ref, s, seq), True) for s in range(NA_SLABS)])
    for s in range(NA_SLABS):
        o_ref[:, s * LANES:(s + 1) * LANES] = _na_finish(*res[s]).T.astype(BF16)


def _ctx_attn(body, q, k, v, seq, width, name):
    slabs, n, _ = q.shape
    spec = pl.BlockSpec((slabs, seq, LANES), lambda b: (0, b, 0))
    return pl.pallas_call(
        body,
        out_shape=jax.ShapeDtypeStruct((n, width), BF16),
        grid=(n // seq,),
        in_specs=[spec, spec, spec],
        out_specs=pl.BlockSpec((seq, width), lambda b: (b, 0)),
        compiler_params=_params(1), name=name,
    )(q, k, v)


def _mla_lat_kernel(q_ref, kl_ref, vl_ref, kc_ref, vc_ref, o_ref):
    res = _attend([(q_ref[h],
                    _ref_chunks(kl_ref, vl_ref, h, MLA_KEY_CHUNK) + _ref_chunks(kc_ref, vc_ref, h, MLA_KEY_CHUNK),
                    False) for h in range(MLA_STEP_HEADS)])
    o_t = jnp.concatenate([_mla_finish(o) for o, _ in res], axis=0)
    o_ref[...] = o_t.T.astype(BF16)


def _mla_lat_attn(q, k, v, kc, vc, seq, past):
    n = q.shape[1]
    tq = MLA_Q_TILE
    nq = seq // tq
    hs = MLA_STEP_HEADS
    return pl.pallas_call(
        _mla_lat_kernel,
        out_shape=jax.ShapeDtypeStruct((n, MLA_HEADS * MLA_V), BF16),
        grid=(n // seq, MLA_HEADS // hs, nq),
        in_specs=[pl.BlockSpec((hs, tq, LANES), lambda b, h, i: (h, b * nq + i, 0)),
                  pl.BlockSpec((hs, seq, LANES), lambda b, h, i: (h, b, 0)),
                  pl.BlockSpec((hs, seq, LANES), lambda b, h, i: (h, b, 0)),
                  pl.BlockSpec((hs, past, LANES), lambda b, h, i: (h, b, 0)),
                  pl.BlockSpec((hs, past, LANES), lambda b, h, i: (h, b, 0))],
        out_specs=pl.BlockSpec((tq, hs * MLA_V), lambda b, h, i: (b * nq + i, h)),
        compiler_params=_params(3), name="l0_mla_latent_attn",
    )(q, k, v, kc, vc)


def _na_first_row(r, rows):
    return jnp.clip(r - NA_KH // 2, 0, rows - NA_KH)


def _na_lat_kernel(rows, q_ref, k_ref, v_ref, kc_ref, vc_ref, t2_ref, o_ref):
    nq = NA_TILE_ROWS * GRID_W
    nk = NA_WIN_ROWS * GRID_W
    r0 = pl.program_id(1) * NA_TILE_ROWS
    w0 = jnp.clip(r0 - NA_KH // 2, 0, rows - NA_WIN_ROWS)
    key_row = w0 + lax.broadcasted_iota(jnp.int32, (nk, nq), 0) // GRID_W
    first = _na_first_row(r0 + lax.broadcasted_iota(jnp.int32, (nk, nq), 1) // GRID_W, rows)
    row_mask = jnp.where((key_row >= first) & (key_row < first + NA_KH), 0.0, NEG_INF)
    idx = [[jnp.clip(w0 + j - (r0 + 2 * t) + NA_KH - 1, 0, 2 * NA_KH - 2) for t in range(NA_TILE_ROWS // 2)]
           for j in range(NA_WIN_ROWS)]

    def local_bias(s):
        heads = [jnp.concatenate([jnp.concatenate([t2_ref[2 * s + hh, i] for i in row], axis=1) for row in idx], axis=0)
                 + row_mask for hh in range(2)]
        return jnp.concatenate(heads, axis=1)

    res = _attend([(_na_queries(q_ref[s]),
                    _ref_chunks(k_ref, v_ref, s, nk, functools.partial(local_bias, s)) + _ref_chunks(kc_ref, vc_ref, s, nk),
                    True) for s in range(NA_SLABS)])
    for s in range(NA_SLABS):
        o_ref[:, s * LANES:(s + 1) * LANES] = _na_finish(*res[s]).T.astype(BF16)


def _na_lat_attn(q, k, v, kc, vc, t2, seq, past):
    slabs, n, _ = q.shape
    rows = seq // GRID_W
    tiles = rows // NA_TILE_ROWS
    nq = NA_TILE_ROWS * GRID_W
    nk = NA_WIN_ROWS * GRID_W

    def win_map(b, t):
        w0 = jnp.clip(t * NA_TILE_ROWS - NA_KH // 2, 0, rows - NA_WIN_ROWS)
        return (0, pl.multiple_of(b * seq + w0 * GRID_W, GRID_W), 0)

    win_spec = pl.BlockSpec((pl.Element(slabs), pl.Element(nk), pl.Element(LANES)), win_map)
    ctx_spec = pl.BlockSpec((slabs, past, LANES), lambda b, t: (0, b, 0))
    return pl.pallas_call(
        functools.partial(_na_lat_kernel, rows),
        out_shape=jax.ShapeDtypeStruct((n, slabs * LANES), BF16),
        grid=(n // seq, tiles),
        in_specs=[pl.BlockSpec((slabs, nq, LANES), lambda b, t: (0, b * tiles + t, 0)),
                  win_spec, win_spec, ctx_spec, ctx_spec,
                  _const_spec(t2.shape)],
        out_specs=pl.BlockSpec((nq, slabs * LANES), lambda b, t: (b * tiles + t, 0)),
        compiler_params=_params(2), name="l1_na_latent_attn",
    )(q, k, v, kc, vc, t2)


def _pair_norm(x2, g2):
    low = lax.broadcasted_iota(jnp.int32, (1, LANES), 1) < NA_HEAD_DIM
    sq = x2 * x2
    ss_all = jnp.sum(sq, axis=-1, keepdims=True)
    ss_low = jnp.sum(jnp.where(low, sq, 0.0), axis=-1, keepdims=True)
    ss = jnp.where(low, ss_low, ss_all - ss_low)
    return x2 * lax.rsqrt(ss * (1.0 / NA_HEAD_DIM) + EPS) * g2


def _l1_pre_kernel(keep_f32, x_ref, mod_ref, nw_ref, win_ref, gq_ref, gk_ref, q_ref, k_ref, v_ref, *f32_refs):
    h = _modulate(x_ref[...], nw_ref[...], mod_ref[0:1, :], mod_ref[1:2, :]).astype(BF16)
    width = NA_SLABS * LANES
    q = _dot(h, win_ref[:, 0:width])
    k = _dot(h, win_ref[:, width:2 * width])
    v = _dot(h, win_ref[:, 2 * width:3 * width])
    gq = gq_ref[...]
    gk = gk_ref[...]
    for s in range(NA_SLABS):
        ls = slice(s * LANES, (s + 1) * LANES)
        q_ref[s] = _pair_norm(q[:, ls], gq).astype(BF16)
        kn = _pair_norm(k[:, ls], gk)
        k_ref[s] = kn.astype(BF16)
        v_ref[s] = v[:, ls].astype(BF16)
        if keep_f32:
            f32_refs[0][:, ls] = kn
    if keep_f32:
        f32_refs[1][...] = v


def _l1_pre(x, mod, tiles_per_group, keep_f32, w):
    n = x.shape[0]
    tm = TOKEN_TILE
    row = lambda i: (i, 0)
    width = NA_SLABS * LANES
    slab = jax.ShapeDtypeStruct((NA_SLABS, n, LANES), BF16)
    out_shape = [slab, slab, slab]
    out_specs = [_slab_spec(NA_SLABS, tm)] * 3
    if keep_f32:
        out_shape += [jax.ShapeDtypeStruct((n, width), F32)] * 2
        out_specs += [pl.BlockSpec((tm, width), row)] * 2
    return pl.pallas_call(
        functools.partial(_l1_pre_kernel, keep_f32),
        out_shape=tuple(out_shape), grid=(n // tm,),
        in_specs=[pl.BlockSpec((tm, D_MODEL), row),
                  pl.BlockSpec((None, SUBLANES, D_MODEL), lambda i: (i // tiles_per_group, 0, 0)),
                  _const_spec((1, D_MODEL)),
                  _const_spec((D_MODEL, 3 * width)),
                  _const_spec((1, LANES)),
                  _const_spec((1, LANES))],
        out_specs=tuple(out_specs),
        compiler_params=_params(1), name="l1_pre_ctx" if keep_f32 else "l1_pre_latent",
    )(x, mod, w["norm_mix"], w["w_in"], w["gq"], w["gk"])


def _post_kernel(n_parts, x_ref, *rest):
    parts = rest[:n_parts]
    mod_ref, nw_ref, wo_ref, w1_ref, w2_ref, o_ref = rest[n_parts:]
    mix_in = parts[0][...] if n_parts == 1 else jnp.concatenate([p[...] for p in parts], axis=-1)
    x1 = x_ref[...] + mod_ref[2:3, :] * _dot(mix_in, wo_ref[...])
    h = _modulate(x1, nw_ref[...], mod_ref[3:4, :], mod_ref[4:5, :]).astype(BF16)
    acc = jnp.zeros(x1.shape, F32)
    for c in range(D_FF // FF_CHUNK):
        cs = pl.ds(c * FF_CHUNK, FF_CHUNK)
        u = jnp.square(jnp.maximum(_dot(h, w1_ref[:, cs]), 0.0)).astype(BF16)
        acc = acc + _dot(u, w2_ref[cs, :])
    o_ref[...] = x1 + mod_ref[5:6, :] * acc


def _post(x, parts, mod, tiles_per_group, w, name):
    n = x.shape[0]
    tm = TOKEN_TILE
    row = lambda i: (i, 0)
    in_specs = [pl.BlockSpec((tm, D_MODEL), row)]
    in_specs += [pl.BlockSpec((tm, p.shape[1]), row) for p in parts]
    in_specs += [pl.BlockSpec((None, SUBLANES, D_MODEL), lambda i: (i // tiles_per_group, 0, 0)),
                 _const_spec((1, D_MODEL)),
                 _const_spec((D_MODEL, D_MODEL)),
                 _const_spec((D_MODEL, D_FF)),
                 _const_spec((D_FF, D_MODEL))]
    return pl.pallas_call(
        functools.partial(_post_kernel, len(parts)),
        out_shape=jax.ShapeDtypeStruct((n, D_MODEL), F32), grid=(n // tm,),
        in_specs=in_specs, out_specs=pl.BlockSpec((tm, D_MODEL), row),
        compiler_params=_params(1), name=name,
    )(x, *parts, mod, w["norm_mlp"], w["w_out"], w["w_mlp1"], w["w_mlp2"])


def _rope_tables(seq):
    half = MLA_ROPE // 2
    inv_freq = jnp.power(ROPE_THETA, -jnp.arange(0, half, 2, dtype=F32) / half)
    ang = jnp.arange(GRID_W, dtype=F32)[:, None] * inv_freq[None, :]
    cos, sin = jnp.cos(ang), jnp.sin(ang)
    rows = seq // GRID_W
    by_row = lambda t: jnp.broadcast_to(t[:rows, None, :], (rows, GRID_W, half // 2)).reshape(seq, half // 2)
    by_col = lambda t: jnp.broadcast_to(t[None, :, :], (rows, GRID_W, half // 2)).reshape(seq, half // 2)
    cr, sr, cc, sc = by_row(cos), by_row(sin), by_col(cos), by_col(sin)
    pad = LANES - MLA_QK
    c = jnp.concatenate([jnp.ones((seq, MLA_NOPE), F32), cr, cr, cc, cc, jnp.zeros((seq, pad), F32)], axis=1)
    s = jnp.concatenate([jnp.zeros((seq, MLA_NOPE), F32), -sr, sr, -sc, sc, jnp.zeros((seq, pad), F32)], axis=1)
    return c, s


def _rope_partner(tail):
    shape = tail.shape
    t = tail.reshape(shape[:-1] + (2, 2, MLA_ROPE // 4))
    return t[..., ::-1, :].reshape(shape)


def _na_bias_table(rel_bias):
    qcol = jnp.arange(GRID_W)
    kcol = jnp.arange(GRID_W)
    col_start = jnp.clip(qcol - NA_KW // 2, 0, GRID_W - NA_KW)
    col_mask = (kcol[:, None] >= col_start[None, :]) & (kcol[:, None] < col_start[None, :] + NA_KW)
    offs = jnp.arange(2 * NA_KW - 1)
    onehot = (kcol[None, :, None] - qcol[None, None, :] + NA_KW - 1 == offs[:, None, None]).astype(F32)
    blocks = jnp.einsum('hdj,jkq->hdkq', rel_bias.astype(F32) * LOG2E, onehot, precision=lax.Precision.HIGHEST)
    blocks = jnp.where(col_mask[None, None], blocks, NEG_INF)
    below = jnp.concatenate([jnp.full_like(blocks[:, :1], NEG_INF), blocks[:, :-1]], axis=1)
    return jnp.concatenate([blocks, below], axis=-1)


def _prep_l0(w_in, q_lora_norm, kv_lora_norm, w_q_up, w_kv_up, mla_q_norm, mla_k_norm, w_pool, pool_scale,
             w_out, norm_mix, norm_mlp, w_mlp1, w_mlp2):
    lat_end = POOL_WIDTH + Q_LORA + KV_LORA
    w_in_pad = jnp.zeros((D_MODEL, D_MODEL), F32)
    w_in_pad = w_in_pad.at[:, :lat_end].set(w_in[:, :lat_end])
    w_in_pad = w_in_pad.at[:, lat_end + MLA_NOPE:lat_end + MLA_QK].set(w_in[:, lat_end:])
    w_in_pad = w_in_pad.at[:, lat_end + MLA_QK:].set(_rope_partner(w_in[:, lat_end:]))
    pad = LANES - MLA_QK
    hk = MLA_HEADS * LANES
    wq = w_q_up.reshape(Q_LORA, MLA_HEADS, MLA_QK)
    wq = jnp.concatenate([wq, _rope_partner(wq[:, :, MLA_NOPE:])], axis=-1)
    kv = w_kv_up.reshape(KV_LORA, MLA_HEADS, MLA_NOPE + MLA_V)
    wk = jnp.pad(kv[:, :, :MLA_NOPE], ((0, 0), (0, 0), (0, LANES - MLA_NOPE)))
    wv = jnp.pad(kv[:, :, MLA_NOPE:], ((0, 0), (0, 0), (0, LANES - MLA_V)))
    gq = mla_q_norm * (MLA_QK ** -0.5 * LOG2E)
    slab_lane = jnp.arange(2 * LANES)
    e2 = (slab_lane[:, None] // LANES == slab_lane[None, :] // LANES) & (slab_lane[:, None] % LANES < MLA_QK)
    return dict(
        norm_mix=norm_mix.reshape(1, -1), norm_mlp=norm_mlp.reshape(1, -1),
        w_in=w_in_pad.astype(BF16),
        qln=q_lora_norm.reshape(1, -1), kvn=kv_lora_norm.reshape(1, -1),
        wq=wq.reshape(Q_LORA, hk).astype(BF16),
        gq_plain=jnp.pad(gq, (0, pad)).reshape(1, LANES),
        gq_rope=jnp.concatenate([gq, _rope_partner(gq[MLA_NOPE:])]).reshape(1, LANES),
        wk=wk.reshape(KV_LORA, hk).astype(BF16),
        gk_plain=jnp.pad(mla_k_norm, (0, pad)).reshape(1, LANES),
        gk_rope=jnp.concatenate([mla_k_norm, _rope_partner(mla_k_norm[MLA_NOPE:])]).reshape(1, LANES),
        wv=wv.reshape(KV_LORA, hk).astype(BF16),
        e2=e2.astype(BF16),
        w_pool=w_pool.astype(BF16), pool_scale=pool_scale.reshape(1, -1),
        w_out=w_out.astype(BF16), w_mlp1=w_mlp1.astype(BF16), w_mlp2=w_mlp2.astype(BF16))


def _split_mods(m):
    mods = m.reshape(SUBLANES, N_ADA, D_MODEL)[:3]
    mods = jnp.pad(mods, ((0, 0), (0, SUBLANES - N_ADA), (0, 0)))
    return mods[0:1], mods[1:3]


def _to_slabs(x, n):
    return x.reshape(n, -1, LANES).transpose(1, 0, 2).astype(BF16)


def kernel(x_prompt, x_sample, cache_l0_mla_ckv, cache_l0_mla_krope, cache_l1_na_k, cache_l1_na_v, c, c_ctx, w_ada_l0, b_ada_l0, norm_mix_l0, norm_mlp_l0, w_mlp1_l0, w_mlp2_l0, w_in_l0, q_lora_norm_l0, kv_lora_norm_l0, w_q_up_l0, w_kv_up_l0, mla_q_norm_l0, mla_k_norm_l0, w_pool_l0, pool_scale_l0, w_out_l0, w_ada_l1, b_ada_l1, norm_mix_l1, norm_mlp_l1, w_mlp1_l1, w_mlp2_l1, w_in_l1, na_q_norm_l1, na_k_norm_l1, rel_bias_l1, w_out_l1):
    batch, seq, _ = x_prompt.shape
    dec_batch, dec_seq, _ = x_sample.shape
    past = cache_l0_mla_ckv.shape[1]
    xp = x_prompt.reshape(batch * seq, D_MODEL)
    xs = x_sample.reshape(dec_batch * dec_seq, D_MODEL)
    tiles_p = (batch * seq) // TOKEN_TILE
    tiles_s = dec_seq // TOKEN_TILE

    cond8 = jnp.zeros((SUBLANES, D_MODEL), F32).at[0].set(c_ctx).at[1:1 + dec_batch].set(c)
    mod0_p, mod0_s = _split_mods(_ada(cond8, w_ada_l0, b_ada_l0))
    mod1_p, mod1_s = _split_mods(_ada(cond8, w_ada_l1, b_ada_l1))

    w0 = _prep_l0(w_in_l0, q_lora_norm_l0, kv_lora_norm_l0, w_q_up_l0, w_kv_up_l0, mla_q_norm_l0, mla_k_norm_l0,
                  w_pool_l0, pool_scale_l0, w_out_l0, norm_mix_l0, norm_mlp_l0, w_mlp1_l0, w_mlp2_l0)
    a_p, q_p, k_p, v_p, ckv_p, kr_p = _l0_pre(xp, mod0_p, tiles_p, False, w0, None)
    a_s, q_s, k_s, v_s, _, _ = _l0_pre(xs, mod0_s, tiles_s, True, w0, _rope_tables(dec_seq))
    kr_cache = jnp.pad(cache_l0_mla_krope.reshape(dec_batch * past, MLA_ROPE), ((0, 0), (MLA_NOPE, LANES - MLA_QK)))
    k_c, v_c = _cache_kv(cache_l0_mla_ckv.reshape(dec_batch * past, KV_LORA), kr_cache, w0)
    attn_p = _ctx_attn(_mla_ctx_kernel, q_p, k_p, v_p, seq, MLA_HEADS * MLA_V, "l0_mla_ctx_attn")
    attn_s = _mla_lat_attn(q_s, k_s, v_s, k_c, v_c, dec_seq, past)
    xp = _post(xp, [_pool(a_p, seq, w0), attn_p], mod0_p, tiles_p, w0, "l0_post_ctx")
    xs = _post(xs, [_pool(a_s, dec_seq, w0), attn_s], mod0_s, tiles_s, w0, "l0_post_latent")

    w1 = dict(norm_mix=norm_mix_l1.reshape(1, -1), norm_mlp=norm_mlp_l1.reshape(1, -1),
              w_in=w_in_l1.astype(BF16),
              gq=jnp.tile(na_q_norm_l1 * (NA_HEAD_DIM ** -0.5 * LOG2E), 2).reshape(1, LANES),
              gk=jnp.tile(na_k_norm_l1, 2).reshape(1, LANES),
              w_out=w_out_l1.astype(BF16), w_mlp1=w_mlp1_l1.astype(BF16), w_mlp2=w_mlp2_l1.astype(BF16))
    q1_p, k1_p, v1_p, k1_new, v1_new = _l1_pre(xp, mod1_p, tiles_p, True, w1)
    q1_s, k1_s, v1_s = _l1_pre(xs, mod1_s, tiles_s, False, w1)
    na_p = _ctx_attn(_na_ctx_kernel, q1_p, k1_p, v1_p, seq, NA_SLABS * LANES, "l1_na_ctx_attn")
    na_s = _na_lat_attn(q1_s, k1_s, v1_s,
                        _to_slabs(cache_l1_na_k, dec_batch * past), _to_slabs(cache_l1_na_v, dec_batch * past),
                        _na_bias_table(rel_bias_l1), dec_seq, past)
    xp = _post(xp, [na_p], mod1_p, tiles_p, w1, "l1_post_ctx")
    xs = _post(xs, [na_s], mod1_s, tiles_s, w1, "l1_post_latent")

    return (xp.reshape(batch, seq, D_MODEL), xs.reshape(dec_batch, dec_seq, D_MODEL),
            ckv_p.reshape(batch, seq, KV_LORA), kr_p.reshape(batch, seq, MLA_ROPE),
            k1_new.reshape(batch, seq, NA_HEADS, NA_HEAD_DIM), v1_new.reshape(batch, seq, NA_HEADS, NA_HEAD_DIM))
```

```python
import functools

import jax
import jax.numpy as jnp
from jax import lax
from jax.experimental import pallas as pl
from jax.experimental.pallas import tpu as pltpu

F32 = jnp.float32
BF16 = jnp.bfloat16

D_MODEL = 1024
D_FF = 4 * D_MODEL
N_ADA = 6
EPS = 1e-6
NEG_INF = -1e30
ROPE_THETA = 10000.0
GRID_W = 64
POOL_WIDTH = 512
POOL_WINDOWS = (2, 4, 8, 16)
POOL_GROUP = 128
POOL_HALO = 8
MLA_HEADS = 8
MLA_NOPE = 64
MLA_ROPE = 32
MLA_QK = MLA_NOPE + MLA_ROPE
MLA_V = 64
Q_LORA = 256
KV_LORA = 128
NA_HEADS = 16
NA_HEAD_DIM = 64
NA_KH = 8
NA_KW = 16
LOG2E = 1.4426950408889634

LANES = 128
SUBLANES = 8
VMEM_LIMIT = 56 * 1024 * 1024

TOKEN_TILE = 512
MLA_Q_TILE = 512
MLA_STEP_HEADS = 2
MLA_KEY_CHUNK = 1024
SCORE_LOOKAHEAD = 2
FF_CHUNK = 1024
NA_SLABS = NA_HEADS * NA_HEAD_DIM // LANES
NA_TILE_ROWS = 4
NA_WIN_ROWS = NA_TILE_ROWS + NA_KH - 1


def _params(n_axes):
    return pltpu.CompilerParams(dimension_semantics=("arbitrary",) * n_axes,
                                vmem_limit_bytes=VMEM_LIMIT)


def _const_spec(shape):
    zeros = (0,) * len(shape)
    return pl.BlockSpec(shape, lambda *_: zeros, pipeline_mode=pl.Buffered(1))


def _dot(a, b):
    return jnp.dot(a, b, preferred_element_type=F32)


def _dot_nt(a, b):
    return lax.dot_general(a, b, (((1,), (1,)), ((), ())), preferred_element_type=F32)


def _dot_tn(a, b):
    return lax.dot_general(a, b, (((0,), (0,)), ((), ())), preferred_element_type=F32)


def _rms(x, w, n=None):
    n = x.shape[-1] if n is None else n
    ss = jnp.sum(x * x, axis=-1, keepdims=True)
    return x * lax.rsqrt(ss * (1.0 / n) + EPS) * w


def _modulate(x, nw, shift, scale):
    return _rms(x, nw) * (1.0 + scale) + shift


def _ada_kernel(cond_ref, w_ref, b_ref, o_ref):
    c = cond_ref[...]
    s = (c * jax.nn.sigmoid(c)).astype(BF16)
    o_ref[...] = _dot(s, w_ref[...].astype(BF16)) + b_ref[...]


def _ada(cond8, w_ada, b_ada):
    n_out = w_ada.shape[1]
    tn = D_MODEL
    return pl.pallas_call(
        _ada_kernel,
        out_shape=jax.ShapeDtypeStruct((SUBLANES, n_out), F32),
        grid=(n_out // tn,),
        in_specs=[pl.BlockSpec((SUBLANES, D_MODEL), lambda j: (0, 0)),
                  pl.BlockSpec((D_MODEL, tn), lambda j: (0, j)),
                  pl.BlockSpec((1, tn), lambda j: (0, j))],
        out_specs=pl.BlockSpec((SUBLANES, tn), lambda j: (0, j)),
        compiler_params=_params(1),
        name="ada",
    )(cond8, w_ada, b_ada.reshape(1, n_out))


def _rope(x, c, s):
    return x * c + pltpu.roll(x, LANES - MLA_ROPE, 1) * s


def _head_rinv(raw, e2_ref):
    sq = (raw * raw).astype(BF16)
    pair = 2 * LANES
    ss = jnp.concatenate([_dot(sq[:, p:p + pair], e2_ref[...]) for p in range(0, raw.shape[1], pair)], axis=1)
    return lax.rsqrt(ss * (1.0 / MLA_QK) + EPS)


def _mla_keys_values(ckv, kr_slab, wk_ref, gk, wv_ref, e2_ref, rope_tabs, k_ref, v_ref):
    c16 = ckv.astype(BF16)
    kn = _dot(c16, wk_ref[...])
    v = _dot(c16, wv_ref[...])
    rinv = _head_rinv(kn + jnp.tile(kr_slab, (1, MLA_HEADS)), e2_ref)
    tail = kr_slab * gk
    if rope_tabs is not None:
        tail = _rope(tail, *rope_tabs)
    ones_lane = (lax.broadcasted_iota(jnp.int32, (1, LANES), 1) == MLA_V).astype(F32)
    for h in range(MLA_HEADS):
        hs = slice(h * LANES, (h + 1) * LANES)
        k_ref[h] = (rinv[:, hs] * (kn[:, hs] * gk + tail)).astype(BF16)
        v_ref[h] = (v[:, hs] + ones_lane).astype(BF16)


def _l0_pre_kernel(latent, x_ref, mod_ref, nw_ref, win_ref, qln_ref, kvn_ref, wq_ref, gq_ref,
                   wk_ref, gk_ref, wv_ref, e2_ref, *rest):
    if latent:
        c_ref, s_ref, a_ref, q_ref, k_ref, v_ref, ckv_ref, kr_ref = rest
        rope_tabs = (c_ref[...], s_ref[...])
    else:
        a_ref, q_ref, k_ref, v_ref, ckv_ref, kr_ref = rest
        rope_tabs = None
    h = _modulate(x_ref[...], nw_ref[...], mod_ref[0:1, :], mod_ref[1:2, :]).astype(BF16)
    proj = _dot(h, win_ref[...])
    a_ref[...] = proj[:, :POOL_WIDTH]
    q_lat = proj[:, POOL_WIDTH:POOL_WIDTH + Q_LORA]
    kv_lat = proj[:, POOL_WIDTH + Q_LORA:POOL_WIDTH + Q_LORA + KV_LORA]
    kr_slab = proj[:, POOL_WIDTH + Q_LORA + KV_LORA:]
    q = _dot(_rms(q_lat, qln_ref[...]).astype(BF16), wq_ref[...])
    q_rinv = _head_rinv(q, e2_ref)
    gq = gq_ref[...]
    for hd in range(MLA_HEADS):
        hs = slice(hd * LANES, (hd + 1) * LANES)
        y = q[:, hs] * q_rinv[:, hs] * gq
        if rope_tabs is not None:
            y = _rope(y, *rope_tabs)
        q_ref[hd] = y.astype(BF16)
    ckv = _rms(kv_lat, kvn_ref[...])
    ckv_ref[...] = ckv
    kr_ref[...] = kr_slab[:, MLA_NOPE:MLA_QK]
    _mla_keys_values(ckv, kr_slab, wk_ref, gk_ref[...], wv_ref, e2_ref, rope_tabs, k_ref, v_ref)


def _slab_spec(slabs, tm):
    return pl.BlockSpec((slabs, tm, LANES), lambda i: (0, i, 0))


def _l0_pre(x, mod, tiles_per_group, latent, w, rope_tabs):
    n = x.shape[0]
    tm = TOKEN_TILE
    row = lambda i: (i, 0)
    hk = MLA_HEADS * LANES
    in_specs = [pl.BlockSpec((tm, D_MODEL), row),
                pl.BlockSpec((None, SUBLANES, D_MODEL), lambda i: (i // tiles_per_group, 0, 0)),
                _const_spec((1, D_MODEL)),
                _const_spec((D_MODEL, D_MODEL)),
                _const_spec((1, Q_LORA)),
                _const_spec((1, KV_LORA)),
                _const_spec((Q_LORA, hk)),
                _const_spec((1, LANES)),
                _const_spec((KV_LORA, hk)),
                _const_spec((1, LANES)),
                _const_spec((KV_LORA, hk)),
                _const_spec((2 * LANES, 2 * LANES))]
    gains = "rope" if latent else "plain"
    args = [x, mod, w["norm_mix"], w["w_in"], w["qln"], w["kvn"], w["wq"], w["gq_" + gains], w["wk"],
            w["gk_" + gains], w["wv"], w["e2"]]
    if latent:
        seq_tiles = rope_tabs[0].shape[0] // tm
        in_specs += [pl.BlockSpec((tm, LANES), lambda i: (i % seq_tiles, 0))] * 2
        args += list(rope_tabs)
    slab = jax.ShapeDtypeStruct((MLA_HEADS, n, LANES), BF16)
    out_shape = (jax.ShapeDtypeStruct((n, POOL_WIDTH), F32), slab, slab, slab,
                 jax.ShapeDtypeStruct((n, KV_LORA), F32),
                 jax.ShapeDtypeStruct((n, MLA_ROPE), F32))
    out_specs = (pl.BlockSpec((tm, POOL_WIDTH), row),
                 _slab_spec(MLA_HEADS, tm), _slab_spec(MLA_HEADS, tm), _slab_spec(MLA_HEADS, tm),
                 pl.BlockSpec((tm, KV_LORA), row), pl.BlockSpec((tm, MLA_ROPE), row))
    return pl.pallas_call(
        functools.partial(_l0_pre_kernel, latent),
        out_shape=out_shape, grid=(n // tm,), in_specs=in_specs, out_specs=out_specs,
        compiler_params=_params(1), name="l0_pre_latent" if latent else "l0_pre_ctx",
    )(*args)


def _cache_kv_kernel(ckv_ref, kr_ref, wk_ref, gk_ref, wv_ref, e2_ref, k_ref, v_ref):
    _mla_keys_values(ckv_ref[...], kr_ref[...], wk_ref, gk_ref[...], wv_ref, e2_ref, None, k_ref, v_ref)


def _cache_kv(ckv, kr_slab, w):
    n = ckv.shape[0]
    tm = TOKEN_TILE
    row = lambda i: (i, 0)
    hk = MLA_HEADS * LANES
    slab = jax.ShapeDtypeStruct((MLA_HEADS, n, LANES), BF16)
    return pl.pallas_call(
        _cache_kv_kernel,
        out_shape=(slab, slab),
        grid=(n // tm,),
        in_specs=[pl.BlockSpec((tm, KV_LORA), row), pl.BlockSpec((tm, LANES), row),
                  _const_spec((KV_LORA, hk)), _const_spec((1, LANES)), _const_spec((KV_LORA, hk)),
                  _const_spec((2 * LANES, 2 * LANES))],
        out_specs=(_slab_spec(MLA_HEADS, tm), _slab_spec(MLA_HEADS, tm)),
        compiler_params=_params(1), name="l0_cache_kv",
    )(ckv, kr_slab, w["wk"], w["gk_plain"], w["wv"], w["e2"])


def _pool_kernel(seq_len, prev_ref, u_ref, next_ref, wp_ref, ps_ref, o_ref, buf_ref):
    tm = u_ref.shape[0]
    pos0 = (pl.program_id(0) * tm) % seq_len
    buf_ref[0:POOL_HALO, :] = jnp.where(pos0 != 0, prev_ref[...], 0.0)
    buf_ref[POOL_HALO:POOL_HALO + tm, :] = u_ref[...]
    buf_ref[POOL_HALO + tm:, :] = jnp.where(pos0 + tm != seq_len, next_ref[...], 0.0)
    pos = pos0 + lax.broadcasted_iota(jnp.int32, (tm, 1), 0)
    for g, win in enumerate(POOL_WINDOWS):
        lanes = pl.ds(g * POOL_GROUP, POOL_GROUP)
        lo = pos - win // 2
        cnt = (jnp.minimum(lo + win, seq_len) - jnp.maximum(lo, 0)).astype(F32)
        acc = buf_ref[pl.ds(POOL_HALO - win // 2, tm), lanes]
        for d in range(1 - win // 2, win - win // 2):
            acc = acc + buf_ref[pl.ds(POOL_HALO + d, tm), lanes]
        pooled = acc / cnt - u_ref[:, lanes]
        o_ref[:, lanes] = (_dot(pooled.astype(BF16), wp_ref[g]) * ps_ref[:, lanes]).astype(BF16)


def _pool(a_in, seq_len, w):
    n = a_in.shape[0]
    tm = min(TOKEN_TILE, seq_len)
    hb = tm // POOL_HALO
    last = n // POOL_HALO - 1
    return pl.pallas_call(
        functools.partial(_pool_kernel, seq_len),
        out_shape=jax.ShapeDtypeStruct((n, POOL_WIDTH), BF16),
        grid=(n // tm,),
        in_specs=[pl.BlockSpec((POOL_HALO, POOL_WIDTH), lambda i: (jnp.maximum(i * hb - 1, 0), 0)),
                  pl.BlockSpec((tm, POOL_WIDTH), lambda i: (i, 0)),
                  pl.BlockSpec((POOL_HALO, POOL_WIDTH), lambda i: (jnp.minimum((i + 1) * hb, last), 0)),
                  _const_spec((len(POOL_WINDOWS), POOL_GROUP, POOL_GROUP)),
                  _const_spec((1, POOL_WIDTH))],
        out_specs=pl.BlockSpec((tm, POOL_WIDTH), lambda i: (i, 0)),
        scratch_shapes=[pltpu.VMEM((tm + 2 * POOL_HALO, POOL_WIDTH), F32)],
        compiler_params=_params(1), name="l0_pool",
    )(a_in, a_in, a_in, w["w_pool"], w["pool_scale"])


def _attend(streams):
    items = [(si, ch) for si, (_, chunks, _) in enumerate(streams) for ch in chunks]

    def score(item):
        si, (k_fn, _, extra_fn) = item
        s = _dot_nt(k_fn(), streams[si][0])
        return s if extra_fn is None else s + extra_fn()

    state = [None] * len(streams)
    pending = [score(item) for item in items[:SCORE_LOOKAHEAD]]
    for i, (si, (_, v_fn, _)) in enumerate(items):
        s = pending.pop(0)
        if i + SCORE_LOOKAHEAD < len(items):
            pending.append(score(items[i + SCORE_LOOKAHEAD]))
        want_den = streams[si][2]
        m_new = jnp.max(s, axis=0, keepdims=True)
        if state[si] is not None:
            m, acc, den = state[si]
            m_new = jnp.maximum(m, m_new)
            alpha = jnp.exp2(m - m_new)
        p = jnp.exp2(s - m_new)
        pv = _dot_tn(v_fn(), p.astype(BF16))
        psum = jnp.sum(p, axis=0, keepdims=True) if want_den else None
        if state[si] is None:
            state[si] = (m_new, pv, psum)
        else:
            state[si] = (m_new, acc * alpha + pv, den * alpha + psum if want_den else None)
    return [(acc, den) for _, acc, den in state]


def _ref_chunks(k_ref, v_ref, slab, chunk, extra_fn=None):
    nk = k_ref.shape[1]
    out = []
    for c in range(0, nk, chunk):
        rows = pl.ds(c, min(chunk, nk - c))
        out.append((lambda rows=rows: k_ref[slab, rows, :], lambda rows=rows: v_ref[slab, rows, :], extra_fn))
    return out


def _mla_finish(o_t):
    return o_t[:MLA_V] / o_t[MLA_V:MLA_V + 1]


def _na_queries(q2):
    low = lax.broadcasted_iota(jnp.int32, (1, LANES), 1) < NA_HEAD_DIM
    zero = jnp.zeros_like(q2)
    return jnp.concatenate([jnp.where(low, q2, zero), jnp.where(low, zero, q2)], axis=0)


def _na_finish(o_t, den):
    nq = o_t.shape[1] // 2
    o = o_t / den
    return jnp.concatenate([o[:NA_HEAD_DIM, :nq], o[NA_HEAD_DIM:, nq:]], axis=0)


def _mla_ctx_kernel(q_ref, k_ref, v_ref, o_ref):
    seq = k_ref.shape[1]
    res = _attend([(q_ref[h], _ref_chunks(k_ref, v_ref, h, seq), False) for h in range(MLA_HEADS)])
    for hp in range(MLA_HEADS // 2):
        o_t = jnp.concatenate([_mla_finish(res[h][0]) for h in (2 * hp, 2 * hp + 1)], axis=0)
        o_ref[:, hp * LANES:(hp + 1) * LANES] = o_t.T.astype(BF16)


def _na_ctx_kernel(q_ref, k_ref, v_ref, o_ref):
    seq = k_ref.shape[1]
    res = _attend([(_na_queries(q_ref[s]), _ref_chunks(k_ref, v_ref, s, seq), True) for s in range(NA_SLABS)])
    for s in range(NA_SLABS):
        o_ref[:, s * LANES:(s + 1) * LANES] = _na_finish(*res[s]).T.astype(BF16)


def _ctx_attn(body, q, k, v, seq, width, name):
    slabs, n, _ = q.shape
    spec = pl.BlockSpec((slabs, seq, LANES), lambda b: (0, b, 0))
    return pl.pallas_call(
        body,
        out_shape=jax.ShapeDtypeStruct((n, width), BF16),
        grid=(n // seq,),
        in_specs=[spec, spec, spec],
        out_specs=pl.BlockSpec((seq, width), lambda b: (b, 0)),
        compiler_params=_params(1), name=name,
    )(q, k, v)


def _mla_lat_kernel(q_ref, kl_ref, vl_ref, kc_ref, vc_ref, o_ref):
    res = _attend([(q_ref[h],
                    _ref_chunks(kl_ref, vl_ref, h, MLA_KEY_CHUNK) + _ref_chunks(kc_ref, vc_ref, h, MLA_KEY_CHUNK),
                    False) for h in range(MLA_STEP_HEADS)])
    o_t = jnp.concatenate([_mla_finish(o) for o, _ in res], axis=0)
    o_ref[...] = o_t.T.astype(BF16)


def _mla_lat_attn(q, k, v, kc, vc, seq, past):
    n = q.shape[1]
    tq = MLA_Q_TILE
    nq = seq // tq
    hs = MLA_STEP_HEADS
    return pl.pallas_call(
        _mla_lat_kernel,
        out_shape=jax.ShapeDtypeStruct((n, MLA_HEADS * MLA_V), BF16),
        grid=(n // seq, MLA_HEADS // hs, nq),
        in_specs=[pl.BlockSpec((hs, tq, LANES), lambda b, h, i: (h, b * nq + i, 0)),
                  pl.BlockSpec((hs, seq, LANES), lambda b, h, i: (h, b, 0)),
                  pl.BlockSpec((hs, seq, LANES), lambda b, h, i: (h, b, 0)),
                  pl.BlockSpec((hs, past, LANES), lambda b, h, i: (h, b, 0)),
                  pl.BlockSpec((hs, past, LANES), lambda b, h, i: (h, b, 0))],
        out_specs=pl.BlockSpec((tq, hs * MLA_V), lambda b, h, i: (b * nq + i, h)),
        compiler_params=_params(3), name="l0_mla_latent_attn",
    )(q, k, v, kc, vc)


def _na_first_row(r, rows):
    return jnp.clip(r - NA_KH // 2, 0, rows - NA_KH)


def _na_lat_kernel(rows, q_ref, k_ref, v_ref, kc_ref, vc_ref, t2_ref, o_ref):
    nq = NA_TILE_ROWS * GRID_W
    nk = NA_WIN_ROWS * GRID_W
    n_off = 2 * NA_KH - 1
    r0 = pl.program_id(1) * NA_TILE_ROWS
    w0 = jnp.clip(r0 - NA_KH // 2, 0, rows - NA_WIN_ROWS)

    def block_index(j, t):
        key_row = w0 + j
        i = r0 + 2 * t
        first_u, first_l = _na_first_row(i, rows), _na_first_row(i + 1, rows)
        in_u = (key_row >= first_u) & (key_row < first_u + NA_KH)
        in_l = (key_row >= first_l) & (key_row < first_l + NA_KH)
        d = jnp.clip(key_row - i + NA_KH - 1, 0, n_off - 1)
        return jnp.where(in_u & in_l, d, jnp.where(in_u, n_off, jnp.where(in_l, n_off + 1, n_off + 2)))

    idx = [[block_index(j, t) for t in range(NA_TILE_ROWS // 2)] for j in range(NA_WIN_ROWS)]

    def local_bias(s):
        heads = [jnp.concatenate([jnp.concatenate([t2_ref[2 * s + hh, i] for i in row], axis=1) for row in idx], axis=0)
                 for hh in range(2)]
        return jnp.concatenate(heads, axis=1)

    res = _attend([(_na_queries(q_ref[s]),
                    _ref_chunks(k_ref, v_ref, s, nk, functools.partial(local_bias, s)) + _ref_chunks(kc_ref, vc_ref, s, nk),
                    True) for s in range(NA_SLABS)])
    for s in range(NA_SLABS):
        o_ref[:, s * LANES:(s + 1) * LANES] = _na_finish(*res[s]).T.astype(BF16)


def _na_lat_attn(q, k, v, kc, vc, t2, seq, past):
    slabs, n, _ = q.shape
    rows = seq // GRID_W
    tiles = rows // NA_TILE_ROWS
    nq = NA_TILE_ROWS * GRID_W
    nk = NA_WIN_ROWS * GRID_W

    def win_map(b, t):
        w0 = jnp.clip(t * NA_TILE_ROWS - NA_KH // 2, 0, rows - NA_WIN_ROWS)
        return (0, pl.multiple_of(b * seq + w0 * GRID_W, GRID_W), 0)

    win_spec = pl.BlockSpec((pl.Element(slabs), pl.Element(nk), pl.Element(LANES)), win_map)
    ctx_spec = pl.BlockSpec((slabs, past, LANES), lambda b, t: (0, b, 0))
    return pl.pallas_call(
        functools.partial(_na_lat_kernel, rows),
        out_shape=jax.ShapeDtypeStruct((n, slabs * LANES), BF16),
        grid=(n // seq, tiles),
        in_specs=[pl.BlockSpec((slabs, nq, LANES), lambda b, t: (0, b * tiles + t, 0)),
                  win_spec, win_spec, ctx_spec, ctx_spec,
                  _const_spec(t2.shape)],
        out_specs=pl.BlockSpec((nq, slabs * LANES), lambda b, t: (b * tiles + t, 0)),
        compiler_params=_params(2), name="l1_na_latent_attn",
    )(q, k, v, kc, vc, t2)


def _pair_norm(x2, g2):
    low = lax.broadcasted_iota(jnp.int32, (1, LANES), 1) < NA_HEAD_DIM
    sq = x2 * x2
    ss_all = jnp.sum(sq, axis=-1, keepdims=True)
    ss_low = jnp.sum(jnp.where(low, sq, 0.0), axis=-1, keepdims=True)
    ss = jnp.where(low, ss_low, ss_all - ss_low)
    return x2 * lax.rsqrt(ss * (1.0 / NA_HEAD_DIM) + EPS) * g2


def _l1_pre_kernel(keep_f32, x_ref, mod_ref, nw_ref, win_ref, gq_ref, gk_ref, q_ref, k_ref, v_ref, *f32_refs):
    h = _modulate(x_ref[...], nw_ref[...], mod_ref[0:1, :], mod_ref[1:2, :]).astype(BF16)
    width = NA_SLABS * LANES
    q = _dot(h, win_ref[:, 0:width])
    k = _dot(h, win_ref[:, width:2 * width])
    v = _dot(h, win_ref[:, 2 * width:3 * width])
    gq = gq_ref[...]
    gk = gk_ref[...]
    for s in range(NA_SLABS):
        ls = slice(s * LANES, (s + 1) * LANES)
        q_ref[s] = _pair_norm(q[:, ls], gq).astype(BF16)
        kn = _pair_norm(k[:, ls], gk)
        k_ref[s] = kn.astype(BF16)
        v_ref[s] = v[:, ls].astype(BF16)
        if keep_f32:
            f32_refs[0][:, ls] = kn
    if keep_f32:
        f32_refs[1][...] = v


def _l1_pre(x, mod, tiles_per_group, keep_f32, w):
    n = x.shape[0]
    tm = TOKEN_TILE
    row = lambda i: (i, 0)
    width = NA_SLABS * LANES
    slab = jax.ShapeDtypeStruct((NA_SLABS, n, LANES), BF16)
    out_shape = [slab, slab, slab]
    out_specs = [_slab_spec(NA_SLABS, tm)] * 3
    if keep_f32:
        out_shape += [jax.ShapeDtypeStruct((n, width), F32)] * 2
        out_specs += [pl.BlockSpec((tm, width), row)] * 2
    return pl.pallas_call(
        functools.partial(_l1_pre_kernel, keep_f32),
        out_shape=tuple(out_shape), grid=(n // tm,),
        in_specs=[pl.BlockSpec((tm, D_MODEL), row),
                  pl.BlockSpec((None, SUBLANES, D_MODEL), lambda i: (i // tiles_per_group, 0, 0)),
                  _const_spec((1, D_MODEL)),
                  _const_spec((D_MODEL, 3 * width)),
                  _const_spec((1, LANES)),
                  _const_spec((1, LANES))],
        out_specs=tuple(out_specs),
        compiler_params=_params(1), name="l1_pre_ctx" if keep_f32 else "l1_pre_latent",
    )(x, mod, w["norm_mix"], w["w_in"], w["gq"], w["gk"])


def _post_kernel(n_parts, x_ref, *rest):
    parts = rest[:n_parts]
    mod_ref, nw_ref, wo_ref, w1_ref, w2_ref, o_ref = rest[n_parts:]
    mix_in = parts[0][...] if n_parts == 1 else jnp.concatenate([p[...] for p in parts], axis=-1)
    x1 = x_ref[...] + mod_ref[2:3, :] * _dot(mix_in, wo_ref[...])
    h = _modulate(x1, nw_ref[...], mod_ref[3:4, :], mod_ref[4:5, :]).astype(BF16)
    acc = jnp.zeros(x1.shape, F32)
    for c in range(D_FF // FF_CHUNK):
        cs = pl.ds(c * FF_CHUNK, FF_CHUNK)
        u = jnp.square(jnp.maximum(_dot(h, w1_ref[:, cs]), 0.0)).astype(BF16)
        acc = acc + _dot(u, w2_ref[cs, :])
    o_ref[...] = x1 + mod_ref[5:6, :] * acc


def _post(x, parts, mod, tiles_per_group, w, name):
    n = x.shape[0]
    tm = TOKEN_TILE
    row = lambda i: (i, 0)
    in_specs = [pl.BlockSpec((tm, D_MODEL), row)]
    in_specs += [pl.BlockSpec((tm, p.shape[1]), row) for p in parts]
    in_specs += [pl.BlockSpec((None, SUBLANES, D_MODEL), lambda i: (i // tiles_per_group, 0, 0)),
                 _const_spec((1, D_MODEL)),
                 _const_spec((D_MODEL, D_MODEL)),
                 _const_spec((D_MODEL, D_FF)),
                 _const_spec((D_FF, D_MODEL))]
    return pl.pallas_call(
        functools.partial(_post_kernel, len(parts)),
        out_shape=jax.ShapeDtypeStruct((n, D_MODEL), F32), grid=(n // tm,),
        in_specs=in_specs, out_specs=pl.BlockSpec((tm, D_MODEL), row),
        compiler_params=_params(1), name=name,
    )(x, *parts, mod, w["norm_mlp"], w["w_out"], w["w_mlp1"], w["w_mlp2"])


def _rope_tables(seq):
    half = MLA_ROPE // 2
    inv_freq = jnp.power(ROPE_THETA, -jnp.arange(0, half, 2, dtype=F32) / half)
    ang = jnp.arange(GRID_W, dtype=F32)[:, None] * inv_freq[None, :]
    cos, sin = jnp.cos(ang), jnp.sin(ang)
    rows = seq // GRID_W
    by_row = lambda t: jnp.broadcast_to(t[:rows, None, :], (rows, GRID_W, half // 2)).reshape(seq, half // 2)
    by_col = lambda t: jnp.broadcast_to(t[None, :, :], (rows, GRID_W, half // 2)).reshape(seq, half // 2)
    cr, sr, cc, sc = by_row(cos), by_row(sin), by_col(cos), by_col(sin)
    pad = LANES - MLA_QK
    c = jnp.concatenate([jnp.ones((seq, MLA_NOPE), F32), cr, cr, cc, cc, jnp.zeros((seq, pad), F32)], axis=1)
    s = jnp.concatenate([jnp.zeros((seq, MLA_NOPE), F32), -sr, sr, -sc, sc, jnp.zeros((seq, pad), F32)], axis=1)
    return c, s


def _rope_partner(tail):
    shape = tail.shape
    t = tail.reshape(shape[:-1] + (2, 2, MLA_ROPE // 4))
    return t[..., ::-1, :].reshape(shape)


def _na_bias_table(rel_bias, rows):
    qcol = jnp.arange(GRID_W)
    kcol = jnp.arange(GRID_W)
    col_start = jnp.clip(qcol - NA_KW // 2, 0, GRID_W - NA_KW)
    col_mask = (kcol[:, None] >= col_start[None, :]) & (kcol[:, None] < col_start[None, :] + NA_KW)
    offs = jnp.arange(2 * NA_KW - 1)
    onehot = (kcol[None, :, None] - qcol[None, None, :] + NA_KW - 1 == offs[:, None, None]).astype(F32)
    blocks = jnp.einsum('hdj,jkq->hdkq', rel_bias.astype(F32) * LOG2E, onehot, precision=lax.Precision.HIGHEST)
    blocks = jnp.where(col_mask[None, None], blocks, NEG_INF)
    neg = jnp.full_like(blocks[:, :1], NEG_INF)
    both = jnp.concatenate([blocks, jnp.concatenate([neg, blocks[:, :-1]], axis=1)], axis=-1)
    d_upper, d_lower = _na_partial_offsets(rows)
    upper_only = jnp.concatenate([blocks[:, d_upper:d_upper + 1], neg], axis=-1)
    lower_only = jnp.concatenate([neg, blocks[:, d_lower - 1:d_lower]], axis=-1)
    return jnp.concatenate([both, upper_only, lower_only, jnp.concatenate([neg, neg], axis=-1)], axis=1)


def _na_partial_offsets(rows):
    first = lambda r: min(max(r - NA_KH // 2, 0), rows - NA_KH)
    upper, lower = set(), set()
    for r0 in range(0, rows, NA_TILE_ROWS):
        w0 = min(max(r0 - NA_KH // 2, 0), rows - NA_WIN_ROWS)
        for j in range(NA_WIN_ROWS):
            for t in range(NA_TILE_ROWS // 2):
                key_row, i = w0 + j, r0 + 2 * t
                in_u = first(i) <= key_row < first(i) + NA_KH
                in_l = first(i + 1) <= key_row < first(i + 1) + NA_KH
                d = key_row - i + NA_KH - 1
                if in_u and in_l:
                    assert 1 <= d <= 2 * NA_KH - 2
                elif in_u:
                    upper.add(d)
                elif in_l:
                    lower.add(d)
    assert len(upper) == 1 and len(lower) == 1, (upper, lower)
    return upper.pop(), lower.pop()


def _prep_l0(w_in, q_lora_norm, kv_lora_norm, w_q_up, w_kv_up, mla_q_norm, mla_k_norm, w_pool, pool_scale,
             w_out, norm_mix, norm_mlp, w_mlp1, w_mlp2):
    lat_end = POOL_WIDTH + Q_LORA + KV_LORA
    w_in_pad = jnp.zeros((D_MODEL, D_MODEL), F32)
    w_in_pad = w_in_pad.at[:, :lat_end].set(w_in[:, :lat_end])
    w_in_pad = w_in_pad.at[:, lat_end + MLA_NOPE:lat_end + MLA_QK].set(w_in[:, lat_end:])
    w_in_pad = w_in_pad.at[:, lat_end + MLA_QK:].set(_rope_partner(w_in[:, lat_end:]))
    pad = LANES - MLA_QK
    hk = MLA_HEADS * LANES
    wq = w_q_up.reshape(Q_LORA, MLA_HEADS, MLA_QK)
    wq = jnp.concatenate([wq, _rope_partner(wq[:, :, MLA_NOPE:])], axis=-1)
    kv = w_kv_up.reshape(KV_LORA, MLA_HEADS, MLA_NOPE + MLA_V)
    wk = jnp.pad(kv[:, :, :MLA_NOPE], ((0, 0), (0, 0), (0, LANES - MLA_NOPE)))
    wv = jnp.pad(kv[:, :, MLA_NOPE:], ((0, 0), (0, 0), (0, LANES - MLA_V)))
    gq = mla_q_norm * (MLA_QK ** -0.5 * LOG2E)
    slab_lane = jnp.arange(2 * LANES)
    e2 = (slab_lane[:, None] // LANES == slab_lane[None, :] // LANES) & (slab_lane[:, None] % LANES < MLA_QK)
    return dict(
        norm_mix=norm_mix.reshape(1, -1), norm_mlp=norm_mlp.reshape(1, -1),
        w_in=w_in_pad.astype(BF16),
        qln=q_lora_norm.reshape(1, -1), kvn=kv_lora_norm.reshape(1, -1),
        wq=wq.reshape(Q_LORA, hk).astype(BF16),
        gq_plain=jnp.pad(gq, (0, pad)).reshape(1, LANES),
        gq_rope=jnp.concatenate([gq, _rope_partner(gq[MLA_NOPE:])]).reshape(1, LANES),
        wk=wk.reshape(KV_LORA, hk).astype(BF16),
        gk_plain=jnp.pad(mla_k_norm, (0, pad)).reshape(1, LANES),
        gk_rope=jnp.concatenate([mla_k_norm, _rope_partner(mla_k_norm[MLA_NOPE:])]).reshape(1, LANES),
        wv=wv.reshape(KV_LORA, hk).astype(BF16),
        e2=e2.astype(BF16),
        w_pool=w_pool.astype(BF16), pool_scale=pool_scale.reshape(1, -1),
        w_out=w_out.astype(BF16), w_mlp1=w_mlp1.astype(BF16), w_mlp2=w_mlp2.astype(BF16))


def _split_mods(m):
    mods = m.reshape(SUBLANES, N_ADA, D_MODEL)[:3]
    mods = jnp.pad(mods, ((0, 0), (0, SUBLANES - N_ADA), (0, 0)))
    return mods[0:1], mods[1:3]


def _to_slabs(x, n):
    return x.reshape(n, -1, LANES).transpose(1, 0, 2).astype(BF16)


def kernel(x_prompt, x_sample, cache_l0_mla_ckv, cache_l0_mla_krope, cache_l1_na_k, cache_l1_na_v, c, c_ctx, w_ada_l0, b_ada_l0, norm_mix_l0, norm_mlp_l0, w_mlp1_l0, w_mlp2_l0, w_in_l0, q_lora_norm_l0, kv_lora_norm_l0, w_q_up_l0, w_kv_up_l0, mla_q_norm_l0, mla_k_norm_l0, w_pool_l0, pool_scale_l0, w_out_l0, w_ada_l1, b_ada_l1, norm_mix_l1, norm_mlp_l1, w_mlp1_l1, w_mlp2_l1, w_in_l1, na_q_norm_l1, na_k_norm_l1, rel_bias_l1, w_out_l1):
    batch, seq, _ = x_prompt.shape
    dec_batch, dec_seq, _ = x_sample.shape
    past = cache_l0_mla_ckv.shape[1]
    xp = x_prompt.reshape(batch * seq, D_MODEL)
    xs = x_sample.reshape(dec_batch * dec_seq, D_MODEL)
    tiles_p = (batch * seq) // TOKEN_TILE
    tiles_s = dec_seq // TOKEN_TILE

    cond8 = jnp.zeros((SUBLANES, D_MODEL), F32).at[0].set(c_ctx).at[1:1 + dec_batch].set(c)
    mod0_p, mod0_s = _split_mods(_ada(cond8, w_ada_l0, b_ada_l0))
    mod1_p, mod1_s = _split_mods(_ada(cond8, w_ada_l1, b_ada_l1))

    w0 = _prep_l0(w_in_l0, q_lora_norm_l0, kv_lora_norm_l0, w_q_up_l0, w_kv_up_l0, mla_q_norm_l0, mla_k_norm_l0,
                  w_pool_l0, pool_scale_l0, w_out_l0, norm_mix_l0, norm_mlp_l0, w_mlp1_l0, w_mlp2_l0)
    a_p, q_p, k_p, v_p, ckv_p, kr_p = _l0_pre(xp, mod0_p, tiles_p, False, w0, None)
    a_s, q_s, k_s, v_s, _, _ = _l0_pre(xs, mod0_s, tiles_s, True, w0, _rope_tables(dec_seq))
    kr_cache = jnp.pad(cache_l0_mla_krope.reshape(dec_batch * past, MLA_ROPE), ((0, 0), (MLA_NOPE, LANES - MLA_QK)))
    k_c, v_c = _cache_kv(cache_l0_mla_ckv.reshape(dec_batch * past, KV_LORA), kr_cache, w0)
    attn_p = _ctx_attn(_mla_ctx_kernel, q_p, k_p, v_p, seq, MLA_HEADS * MLA_V, "l0_mla_ctx_attn")
    attn_s = _mla_lat_attn(q_s, k_s, v_s, k_c, v_c, dec_seq, past)
    xp = _post(xp, [_pool(a_p, seq, w0), attn_p], mod0_p, tiles_p, w0, "l0_post_ctx")
    xs = _post(xs, [_pool(a_s, dec_seq, w0), attn_s], mod0_s, tiles_s, w0, "l0_post_latent")

    w1 = dict(norm_mix=norm_mix_l1.reshape(1, -1), norm_mlp=norm_mlp_l1.reshape(1, -1),
              w_in=w_in_l1.astype(BF16),
              gq=jnp.tile(na_q_norm_l1 * (NA_HEAD_DIM ** -0.5 * LOG2E), 2).reshape(1, LANES),
              gk=jnp.tile(na_k_norm_l1, 2).reshape(1, LANES),
              w_out=w_out_l1.astype(BF16), w_mlp1=w_mlp1_l1.astype(BF16), w_mlp2=w_mlp2_l1.astype(BF16))
    q1_p, k1_p, v1_p, k1_new, v1_new = _l1_pre(xp, mod1_p, tiles_p, True, w1)
    q1_s, k1_s, v1_s = _l1_pre(xs, mod1_s, tiles_s, False, w1)
    na_p = _ctx_attn(_na_ctx_kernel, q1_p, k1_p, v1_p, seq, NA_SLABS * LANES, "l1_na_ctx_attn")
    na_s = _na_lat_attn(q1_s, k1_s, v1_s,
                        _to_slabs(cache_l1_na_k, dec_batch * past), _to_slabs(cache_l1_na_v, dec_batch * past),
                        _na_bias_table(rel_bias_l1, dec_seq // GRID_W), dec_seq, past)
    xp = _post(xp, [na_p], mod1_p, tiles_p, w1, "l1_post_ctx")
    xs = _post(xs, [na_s], mod1_s, tiles_s, w1, "l1_post_latent")

    return (xp.reshape(batch, seq, D_MODEL), xs.reshape(dec_batch, dec_seq, D_MODEL),
            ckv_p.reshape(batch, seq, KV_LORA), kr_p.reshape(batch, seq, MLA_ROPE),
            k1_new.reshape(batch, seq, NA_HEADS, NA_HEAD_DIM), v1_new.reshape(batch, seq, NA_HEADS, NA_HEAD_DIM))
```

```python
import functools

import jax
import jax.numpy as jnp
from jax import lax
from jax.experimental import pallas as pl
from jax.experimental.pallas import tpu as pltpu

F32 = jnp.float32
BF16 = jnp.bfloat16

D_MODEL = 1024
D_FF = 4 * D_MODEL
N_ADA = 6
EPS = 1e-6
NEG_INF = -1e30
ROPE_THETA = 10000.0
GRID_W = 64
POOL_WIDTH = 512
POOL_WINDOWS = (2, 4, 8, 16)
POOL_GROUP = 128
POOL_HALO = 8
MLA_HEADS = 8
MLA_NOPE = 64
MLA_ROPE = 32
MLA_QK = MLA_NOPE + MLA_ROPE
MLA_V = 64
Q_LORA = 256
KV_LORA = 128
NA_HEADS = 16
NA_HEAD_DIM = 64
NA_KH = 8
NA_KW = 16
LOG2E = 1.4426950408889634

LANES = 128
SUBLANES = 8
VMEM_LIMIT = 56 * 1024 * 1024

TOKEN_TILE = 512
MLA_Q_TILE = 512
MLA_STEP_HEADS = 2
MLA_KEY_CHUNK = 2048
MLA_ONLINE_CHUNK = 1024
SCORE_LOOKAHEAD = 2
MAX_SCORE_BOUND = 40.0
FF_CHUNK = 1024
NA_SLABS = NA_HEADS * NA_HEAD_DIM // LANES
NA_TILE_ROWS = 4
NA_WIN_ROWS = NA_TILE_ROWS + NA_KH - 1


def _params(n_axes):
    return pltpu.CompilerParams(dimension_semantics=("arbitrary",) * n_axes,
                                vmem_limit_bytes=VMEM_LIMIT)


def _const_spec(shape):
    zeros = (0,) * len(shape)
    return pl.BlockSpec(shape, lambda *_: zeros, pipeline_mode=pl.Buffered(1))


def _dot(a, b):
    return jnp.dot(a, b, preferred_element_type=F32)


def _dot_nt(a, b):
    return lax.dot_general(a, b, (((1,), (1,)), ((), ())), preferred_element_type=F32)


def _dot_tn(a, b):
    return lax.dot_general(a, b, (((0,), (0,)), ((), ())), preferred_element_type=F32)


def _rms(x, w, n=None):
    n = x.shape[-1] if n is None else n
    ss = jnp.sum(x * x, axis=-1, keepdims=True)
    return x * lax.rsqrt(ss * (1.0 / n) + EPS) * w


def _modulate(x, nw, shift, scale):
    return _rms(x, nw) * (1.0 + scale) + shift


def _ada_kernel(cond_ref, w_ref, b_ref, o_ref):
    c = cond_ref[...]
    s = (c * jax.nn.sigmoid(c)).astype(BF16)
    o_ref[...] = _dot(s, w_ref[...].astype(BF16)) + b_ref[...]


def _ada(cond8, w_ada, b_ada):
    n_out = w_ada.shape[1]
    tn = D_MODEL
    return pl.pallas_call(
        _ada_kernel,
        out_shape=jax.ShapeDtypeStruct((SUBLANES, n_out), F32),
        grid=(n_out // tn,),
        in_specs=[pl.BlockSpec((SUBLANES, D_MODEL), lambda j: (0, 0)),
                  pl.BlockSpec((D_MODEL, tn), lambda j: (0, j)),
                  pl.BlockSpec((1, tn), lambda j: (0, j))],
        out_specs=pl.BlockSpec((SUBLANES, tn), lambda j: (0, j)),
        compiler_params=_params(1),
        name="ada",
    )(cond8, w_ada, b_ada.reshape(1, n_out))


def _rope(x, c, s):
    return x * c + pltpu.roll(x, LANES - MLA_ROPE, 1) * s


def _head_rinv(raw, e2_ref):
    sq = (raw * raw).astype(BF16)
    pair = 2 * LANES
    ss = jnp.concatenate([_dot(sq[:, p:p + pair], e2_ref[...]) for p in range(0, raw.shape[1], pair)], axis=1)
    return lax.rsqrt(ss * (1.0 / MLA_QK) + EPS)


def _mla_keys_values(ckv, kr_slab, wk_ref, gk, wv_ref, e2_ref, rope_tabs, k_ref, v_ref):
    c16 = ckv.astype(BF16)
    kn = _dot(c16, wk_ref[...])
    v = _dot(c16, wv_ref[...])
    rinv = _head_rinv(kn + jnp.tile(kr_slab, (1, MLA_HEADS)), e2_ref)
    tail = kr_slab * gk
    if rope_tabs is not None:
        tail = _rope(tail, *rope_tabs)
    ones_lane = (lax.broadcasted_iota(jnp.int32, (1, LANES), 1) == MLA_V).astype(F32)
    for h in range(MLA_HEADS):
        hs = slice(h * LANES, (h + 1) * LANES)
        k_ref[h] = (rinv[:, hs] * (kn[:, hs] * gk + tail)).astype(BF16)
        v_ref[h] = (v[:, hs] + ones_lane).astype(BF16)


def _l0_pre_kernel(latent, x_ref, mod_ref, nw_ref, win_ref, qln_ref, kvn_ref, wq_ref, gq_ref,
                   wk_ref, gk_ref, wv_ref, e2_ref, *rest):
    if latent:
        c_ref, s_ref, a_ref, q_ref, k_ref, v_ref, ckv_ref, kr_ref = rest
        rope_tabs = (c_ref[...], s_ref[...])
    else:
        a_ref, q_ref, k_ref, v_ref, ckv_ref, kr_ref = rest
        rope_tabs = None
    h = _modulate(x_ref[...], nw_ref[...], mod_ref[0:1, :], mod_ref[1:2, :]).astype(BF16)
    proj = _dot(h, win_ref[...])
    a_ref[...] = proj[:, :POOL_WIDTH]
    q_lat = proj[:, POOL_WIDTH:POOL_WIDTH + Q_LORA]
    kv_lat = proj[:, POOL_WIDTH + Q_LORA:POOL_WIDTH + Q_LORA + KV_LORA]
    kr_slab = proj[:, POOL_WIDTH + Q_LORA + KV_LORA:]
    q = _dot(_rms(q_lat, qln_ref[...]).astype(BF16), wq_ref[...])
    q_rinv = _head_rinv(q, e2_ref)
    gq = gq_ref[...]
    for hd in range(MLA_HEADS):
        hs = slice(hd * LANES, (hd + 1) * LANES)
        y = q[:, hs] * q_rinv[:, hs] * gq
        if rope_tabs is not None:
            y = _rope(y, *rope_tabs)
        q_ref[hd] = y.astype(BF16)
    ckv = _rms(kv_lat, kvn_ref[...])
    ckv_ref[...] = ckv
    kr_ref[...] = kr_slab[:, MLA_NOPE:MLA_QK]
    _mla_keys_values(ckv, kr_slab, wk_ref, gk_ref[...], wv_ref, e2_ref, rope_tabs, k_ref, v_ref)


def _slab_spec(slabs, tm):
    return pl.BlockSpec((slabs, tm, LANES), lambda i: (0, i, 0))


def _l0_pre(x, mod, tiles_per_group, latent, w, rope_tabs):
    n = x.shape[0]
    tm = TOKEN_TILE
    row = lambda i: (i, 0)
    hk = MLA_HEADS * LANES
    in_specs = [pl.BlockSpec((tm, D_MODEL), row),
                pl.BlockSpec((None, SUBLANES, D_MODEL), lambda i: (i // tiles_per_group, 0, 0)),
                _const_spec((1, D_MODEL)),
                _const_spec((D_MODEL, D_MODEL)),
                _const_spec((1, Q_LORA)),
                _const_spec((1, KV_LORA)),
                _const_spec((Q_LORA, hk)),
                _const_spec((1, LANES)),
                _const_spec((KV_LORA, hk)),
                _const_spec((1, LANES)),
                _const_spec((KV_LORA, hk)),
                _const_spec((2 * LANES, 2 * LANES))]
    gains = "rope" if latent else "plain"
    args = [x, mod, w["norm_mix"], w["w_in"], w["qln"], w["kvn"], w["wq"], w["gq_" + gains], w["wk"],
            w["gk_" + gains], w["wv"], w["e2"]]
    if latent:
        seq_tiles = rope_tabs[0].shape[0] // tm
        in_specs += [pl.BlockSpec((tm, LANES), lambda i: (i % seq_tiles, 0))] * 2
        args += list(rope_tabs)
    slab = jax.ShapeDtypeStruct((MLA_HEADS, n, LANES), BF16)
    out_shape = (jax.ShapeDtypeStruct((n, POOL_WIDTH), F32), slab, slab, slab,
                 jax.ShapeDtypeStruct((n, KV_LORA), F32),
                 jax.ShapeDtypeStruct((n, MLA_ROPE), F32))
    out_specs = (pl.BlockSpec((tm, POOL_WIDTH), row),
                 _slab_spec(MLA_HEADS, tm), _slab_spec(MLA_HEADS, tm), _slab_spec(MLA_HEADS, tm),
                 pl.BlockSpec((tm, KV_LORA), row), pl.BlockSpec((tm, MLA_ROPE), row))
    return pl.pallas_call(
        functools.partial(_l0_pre_kernel, latent),
        out_shape=out_shape, grid=(n // tm,), in_specs=in_specs, out_specs=out_specs,
        compiler_params=_params(1), name="l0_pre_latent" if latent else "l0_pre_ctx",
    )(*args)


def _cache_kv_kernel(ckv_ref, kr_ref, wk_ref, gk_ref, wv_ref, e2_ref, k_ref, v_ref):
    _mla_keys_values(ckv_ref[...], kr_ref[...], wk_ref, gk_ref[...], wv_ref, e2_ref, None, k_ref, v_ref)


def _cache_kv(ckv, kr_slab, w):
    n = ckv.shape[0]
    tm = TOKEN_TILE
    row = lambda i: (i, 0)
    hk = MLA_HEADS * LANES
    slab = jax.ShapeDtypeStruct((MLA_HEADS, n, LANES), BF16)
    return pl.pallas_call(
        _cache_kv_kernel,
        out_shape=(slab, slab),
        grid=(n // tm,),
        in_specs=[pl.BlockSpec((tm, KV_LORA), row), pl.BlockSpec((tm, LANES), row),
                  _const_spec((KV_LORA, hk)), _const_spec((1, LANES)), _const_spec((KV_LORA, hk)),
                  _const_spec((2 * LANES, 2 * LANES))],
        out_specs=(_slab_spec(MLA_HEADS, tm), _slab_spec(MLA_HEADS, tm)),
        compiler_params=_params(1), name="l0_cache_kv",
    )(ckv, kr_slab, w["wk"], w["gk_plain"], w["wv"], w["e2"])


def _pool_kernel(seq_len, prev_ref, u_ref, next_ref, wp_ref, ps_ref, o_ref, buf_ref):
    tm = u_ref.shape[0]
    pos0 = (pl.program_id(0) * tm) % seq_len
    buf_ref[0:POOL_HALO, :] = jnp.where(pos0 != 0, prev_ref[...], 0.0)
    buf_ref[POOL_HALO:POOL_HALO + tm, :] = u_ref[...]
    buf_ref[POOL_HALO + tm:, :] = jnp.where(pos0 + tm != seq_len, next_ref[...], 0.0)
    pos = pos0 + lax.broadcasted_iota(jnp.int32, (tm, 1), 0)
    for g, win in enumerate(POOL_WINDOWS):
        lanes = pl.ds(g * POOL_GROUP, POOL_GROUP)
        lo = pos - win // 2
        cnt = (jnp.minimum(lo + win, seq_len) - jnp.maximum(lo, 0)).astype(F32)
        acc = buf_ref[pl.ds(POOL_HALO - win // 2, tm), lanes]
        for d in range(1 - win // 2, win - win // 2):
            acc = acc + buf_ref[pl.ds(POOL_HALO + d, tm), lanes]
        pooled = acc / cnt - u_ref[:, lanes]
        o_ref[:, lanes] = (_dot(pooled.astype(BF16), wp_ref[g]) * ps_ref[:, lanes]).astype(BF16)


def _pool(a_in, seq_len, w):
    n = a_in.shape[0]
    tm = min(TOKEN_TILE, seq_len)
    hb = tm // POOL_HALO
    last = n // POOL_HALO - 1
    return pl.pallas_call(
        functools.partial(_pool_kernel, seq_len),
        out_shape=jax.ShapeDtypeStruct((n, POOL_WIDTH), BF16),
        grid=(n // tm,),
        in_specs=[pl.BlockSpec((POOL_HALO, POOL_WIDTH), lambda i: (jnp.maximum(i * hb - 1, 0), 0)),
                  pl.BlockSpec((tm, POOL_WIDTH), lambda i: (i, 0)),
                  pl.BlockSpec((POOL_HALO, POOL_WIDTH), lambda i: (jnp.minimum((i + 1) * hb, last), 0)),
                  _const_spec((len(POOL_WINDOWS), POOL_GROUP, POOL_GROUP)),
                  _const_spec((1, POOL_WIDTH))],
        out_specs=pl.BlockSpec((tm, POOL_WIDTH), lambda i: (i, 0)),
        scratch_shapes=[pltpu.VMEM((tm + 2 * POOL_HALO, POOL_WIDTH), F32)],
        compiler_params=_params(1), name="l0_pool",
    )(a_in, a_in, a_in, w["w_pool"], w["pool_scale"])


def _attend(streams):
    items = [(si, ch) for si, (_, chunks, _) in enumerate(streams) for ch in chunks]

    def score(item):
        si, (k_fn, _, extra_fn) = item
        s = _dot_nt(k_fn(), streams[si][0])
        return s if extra_fn is None else s + extra_fn()

    state = [None] * len(streams)
    pending = [score(item) for item in items[:SCORE_LOOKAHEAD]]
    for i, (si, (_, v_fn, _)) in enumerate(items):
        s = pending.pop(0)
        if i + SCORE_LOOKAHEAD < len(items):
            pending.append(score(items[i + SCORE_LOOKAHEAD]))
        want_den = streams[si][2]
        m_new = jnp.max(s, axis=0, keepdims=True)
        if state[si] is not None:
            m, acc, den = state[si]
            m_new = jnp.maximum(m, m_new)
            alpha = jnp.exp2(m - m_new)
        p = jnp.exp2(s - m_new)
        pv = _dot_tn(v_fn(), p.astype(BF16))
        psum = jnp.sum(p, axis=0, keepdims=True) if want_den else None
        if state[si] is None:
            state[si] = (m_new, pv, psum)
        else:
            state[si] = (m_new, acc * alpha + pv, den * alpha + psum if want_den else None)
    return [(acc, den) for _, acc, den in state]


def _ref_chunks(k_ref, v_ref, slab, chunk, extra_fn=None):
    nk = k_ref.shape[1]
    out = []
    for c in range(0, nk, chunk):
        rows = pl.ds(c, min(chunk, nk - c))
        out.append((lambda rows=rows: k_ref[slab, rows, :], lambda rows=rows: v_ref[slab, rows, :], extra_fn))
    return out


def _mla_finish(o_t):
    return o_t[:MLA_V] / o_t[MLA_V:MLA_V + 1]


def _na_queries(q2):
    low = lax.broadcasted_iota(jnp.int32, (1, LANES), 1) < NA_HEAD_DIM
    zero = jnp.zeros_like(q2)
    return jnp.concatenate([jnp.where(low, q2, zero), jnp.where(low, zero, q2)], axis=0)


def _na_finish(o_t, den):
    nq = o_t.shape[1] // 2
    o = o_t / den
    return jnp.concatenate([o[:NA_HEAD_DIM, :nq], o[NA_HEAD_DIM:, nq:]], axis=0)


def _mla_ctx_kernel(q_ref, k_ref, v_ref, o_ref):
    seq = k_ref.shape[1]
    res = _attend([(q_ref[h], _ref_chunks(k_ref, v_ref, h, seq), False) for h in range(MLA_HEADS)])
    for hp in range(MLA_HEADS // 2):
        o_t = jnp.concatenate([_mla_finish(res[h][0]) for h in (2 * hp, 2 * hp + 1)], axis=0)
        o_ref[:, hp * LANES:(hp + 1) * LANES] = o_t.T.astype(BF16)


def _na_ctx_kernel(q_ref, k_ref, v_ref, o_ref):
    seq = k_ref.shape[1]
    res = _attend([(_na_queries(q_ref[s]), _ref_chunks(k_ref, v_ref, s, seq), True) for s in range(NA_SLABS)])
    for s in range(NA_SLABS):
        o_ref[:, s * LANES:(s + 1) * LANES] = _na_finish(*res[s]).T.astype(BF16)


def _ctx_attn(body, q, k, v, seq, width, name):
    slabs, n, _ = q.shape
    spec = pl.BlockSpec((slabs, seq, LANES), lambda b: (0, b, 0))
    return pl.pallas_call(
        body,
        out_shape=jax.ShapeDtypeStruct((n, width), BF16),
        grid=(n // seq,),
        in_specs=[spec, spec, spec],
        out_specs=pl.BlockSpec((seq, width), lambda b: (b, 0)),
        compiler_params=_params(1), name=name,
    )(q, k, v)


def _bounded_attn(q, chunks):
    acc = None
    for k_fn, v_fn, _ in chunks:
        p = jnp.exp2(_dot_nt(k_fn(), q)).astype(BF16)
        pv = _dot_tn(v_fn(), p)
        acc = pv if acc is None else acc + pv
    return acc


def _mla_lat_kernel(bound_ref, q_ref, kl_ref, vl_ref, kc_ref, vc_ref, o_ref):
    batch, head0 = pl.program_id(0), pl.program_id(1) * MLA_STEP_HEADS
    heads = range(MLA_STEP_HEADS)
    bounds = [bound_ref[batch, head0 + h] for h in heads]
    chunks = lambda h, size: _ref_chunks(kl_ref, vl_ref, h, size) + _ref_chunks(kc_ref, vc_ref, h, size)

    def bounded():
        return [_bounded_attn(q_ref[h], chunks(h, MLA_KEY_CHUNK)) for h in heads]

    def online():
        return [o for o, _ in _attend([(q_ref[h], chunks(h, MLA_ONLINE_CHUNK), False) for h in heads])]

    outs = lax.cond(functools.reduce(jnp.maximum, bounds) <= MAX_SCORE_BOUND, bounded, online)
    o_ref[...] = jnp.concatenate([_mla_finish(o) for o in outs], axis=0).T.astype(BF16)


def _score_bound(q, keys, group, heads_per_slab=1):
    def max_norm2(x, per_batch):
        slabs, tokens, _ = x.shape
        n2 = jnp.sum(jnp.square(x.astype(F32)).reshape(slabs, tokens, heads_per_slab, -1), axis=-1)
        n2 = jnp.max(n2.reshape(slabs, tokens // per_batch, per_batch, heads_per_slab), axis=2)
        return n2.transpose(1, 0, 2).reshape(tokens // per_batch, slabs * heads_per_slab)
    k2 = functools.reduce(jnp.maximum, [max_norm2(k, g) for k, g in zip(keys, group[1:])])
    return jnp.sqrt(max_norm2(q, group[0]) * k2)


def _mla_lat_attn(q, k, v, kc, vc, seq, past):
    n = q.shape[1]
    tq = MLA_Q_TILE
    nq = seq // tq
    hs = MLA_STEP_HEADS
    bound = _score_bound(q, [k, kc], (seq, seq, past))
    return pl.pallas_call(
        _mla_lat_kernel,
        out_shape=jax.ShapeDtypeStruct((n, MLA_HEADS * MLA_V), BF16),
        grid=(n // seq, MLA_HEADS // hs, nq),
        in_specs=[pl.BlockSpec(memory_space=pltpu.SMEM),
                  pl.BlockSpec((hs, tq, LANES), lambda b, h, i: (h, b * nq + i, 0)),
                  pl.BlockSpec((hs, seq, LANES), lambda b, h, i: (h, b, 0)),
                  pl.BlockSpec((hs, seq, LANES), lambda b, h, i: (h, b, 0)),
                  pl.BlockSpec((hs, past, LANES), lambda b, h, i: (h, b, 0)),
                  pl.BlockSpec((hs, past, LANES), lambda b, h, i: (h, b, 0))],
        out_specs=pl.BlockSpec((tq, hs * MLA_V), lambda b, h, i: (b * nq + i, h)),
        compiler_params=_params(3), name="l0_mla_latent_attn",
    )(bound, q, k, v, kc, vc)


def _na_first_row(r, rows):
    return jnp.clip(r - NA_KH // 2, 0, rows - NA_KH)


def _na_lat_kernel(rows, bound_ref, q_ref, k_ref, v_ref, kc_ref, vc_ref, t2_ref, o_ref):
    nq = NA_TILE_ROWS * GRID_W
    nk = NA_WIN_ROWS * GRID_W
    n_off = 2 * NA_KH - 1
    r0 = pl.program_id(1) * NA_TILE_ROWS
    w0 = jnp.clip(r0 - NA_KH // 2, 0, rows - NA_WIN_ROWS)

    def block_index(j, t):
        key_row = w0 + j
        i = r0 + 2 * t
        first_u, first_l = _na_first_row(i, rows), _na_first_row(i + 1, rows)
        in_u = (key_row >= first_u) & (key_row < first_u + NA_KH)
        in_l = (key_row >= first_l) & (key_row < first_l + NA_KH)
        d = jnp.clip(key_row - i + NA_KH - 1, 0, n_off - 1)
        return jnp.where(in_u & in_l, d, jnp.where(in_u, n_off, jnp.where(in_l, n_off + 1, n_off + 2)))

    idx = [[block_index(j, t) for t in range(NA_TILE_ROWS // 2)] for j in range(NA_WIN_ROWS)]

    def local_bias(s):
        heads = [jnp.concatenate([jnp.concatenate([t2_ref[2 * s + hh, i] for i in row], axis=1) for row in idx], axis=0)
                 for hh in range(2)]
        return jnp.concatenate(heads, axis=1)

    batch = pl.program_id(0)
    bounds = [bound_ref[batch, h] for h in range(NA_HEADS)]

    def bounded():
        res = []
        for s in range(NA_SLABS):
            q2 = _na_queries(q_ref[s])
            p_loc = jnp.exp2(_dot_nt(k_ref[s], q2) + local_bias(s))
            p_ctx = jnp.exp2(_dot_nt(kc_ref[s], q2))
            den = jnp.sum(p_loc, axis=0, keepdims=True) + jnp.sum(p_ctx, axis=0, keepdims=True)
            o_t = _dot_tn(v_ref[s], p_loc.astype(BF16)) + _dot_tn(vc_ref[s], p_ctx.astype(BF16))
            res.append(_na_finish(o_t, den))
        return res

    def online():
        res = _attend([(_na_queries(q_ref[s]),
                        _ref_chunks(k_ref, v_ref, s, nk, functools.partial(local_bias, s))
                        + _ref_chunks(kc_ref, vc_ref, s, nk),
                        True) for s in range(NA_SLABS)])
        return [_na_finish(o_t, den) for o_t, den in res]

    outs = lax.cond(functools.reduce(jnp.maximum, bounds) <= MAX_SCORE_BOUND, bounded, online)
    for s in range(NA_SLABS):
        o_ref[:, s * LANES:(s + 1) * LANES] = outs[s].T.astype(BF16)


def _na_lat_attn(q, k, v, kc, vc, t2, bias_max, seq, past):
    slabs, n, _ = q.shape
    rows = seq // GRID_W
    tiles = rows // NA_TILE_ROWS
    nq = NA_TILE_ROWS * GRID_W
    nk = NA_WIN_ROWS * GRID_W
    bound = _score_bound(q, [k, kc], (seq, seq, past), heads_per_slab=2) + bias_max[None, :]

    def win_map(b, t):
        w0 = jnp.clip(t * NA_TILE_ROWS - NA_KH // 2, 0, rows - NA_WIN_ROWS)
        return (0, pl.multiple_of(b * seq + w0 * GRID_W, GRID_W), 0)

    win_spec = pl.BlockSpec((pl.Element(slabs), pl.Element(nk), pl.Element(LANES)), win_map)
    ctx_spec = pl.BlockSpec((slabs, past, LANES), lambda b, t: (0, b, 0))
    return pl.pallas_call(
        functools.partial(_na_lat_kernel, rows),
        out_shape=jax.ShapeDtypeStruct((n, slabs * LANES), BF16),
        grid=(n // seq, tiles),
        in_specs=[pl.BlockSpec(memory_space=pltpu.SMEM),
                  pl.BlockSpec((slabs, nq, LANES), lambda b, t: (0, b * tiles + t, 0)),
                  win_spec, win_spec, ctx_spec, ctx_spec,
                  _const_spec(t2.shape)],
        out_specs=pl.BlockSpec((nq, slabs * LANES), lambda b, t: (b * tiles + t, 0)),
        compiler_params=_params(2), name="l1_na_latent_attn",
    )(bound, q, k, v, kc, vc, t2)


def _pair_norm(x2, g2):
    low = lax.broadcasted_iota(jnp.int32, (1, LANES), 1) < NA_HEAD_DIM
    sq = x2 * x2
    ss_all = jnp.sum(sq, axis=-1, keepdims=True)
    ss_low = jnp.sum(jnp.where(low, sq, 0.0), axis=-1, keepdims=True)
    ss = jnp.where(low, ss_low, ss_all - ss_low)
    return x2 * lax.rsqrt(ss * (1.0 / NA_HEAD_DIM) + EPS) * g2


def _l1_pre_kernel(keep_f32, x_ref, mod_ref, nw_ref, win_ref, gq_ref, gk_ref, q_ref, k_ref, v_ref, *f32_refs):
    h = _modulate(x_ref[...], nw_ref[...], mod_ref[0:1, :], mod_ref[1:2, :]).astype(BF16)
    width = NA_SLABS * LANES
    q = _dot(h, win_ref[:, 0:width])
    k = _dot(h, win_ref[:, width:2 * width])
    v = _dot(h, win_ref[:, 2 * width:3 * width])
    gq = gq_ref[...]
    gk = gk_ref[...]
    for s in range(NA_SLABS):
        ls = slice(s * LANES, (s + 1) * LANES)
        q_ref[s] = _pair_norm(q[:, ls], gq).astype(BF16)
        kn = _pair_norm(k[:, ls], gk)
        k_ref[s] = kn.astype(BF16)
        v_ref[s] = v[:, ls].astype(BF16)
        if keep_f32:
            f32_refs[0][:, ls] = kn
    if keep_f32:
        f32_refs[1][...] = v


def _l1_pre(x, mod, tiles_per_group, keep_f32, w):
    n = x.shape[0]
    tm = TOKEN_TILE
    row = lambda i: (i, 0)
    width = NA_SLABS * LANES
    slab = jax.ShapeDtypeStruct((NA_SLABS, n, LANES), BF16)
    out_shape = [slab, slab, slab]
    out_specs = [_slab_spec(NA_SLABS, tm)] * 3
    if keep_f32:
        out_shape += [jax.ShapeDtypeStruct((n, width), F32)] * 2
        out_specs += [pl.BlockSpec((tm, width), row)] * 2
    return pl.pallas_call(
        functools.partial(_l1_pre_kernel, keep_f32),
        out_shape=tuple(out_shape), grid=(n // tm,),
        in_specs=[pl.BlockSpec((tm, D_MODEL), row),
                  pl.BlockSpec((None, SUBLANES, D_MODEL), lambda i: (i // tiles_per_group, 0, 0)),
                  _const_spec((1, D_MODEL)),
                  _const_spec((D_MODEL, 3 * width)),
                  _const_spec((1, LANES)),
                  _const_spec((1, LANES))],
        out_specs=tuple(out_specs),
        compiler_params=_params(1), name="l1_pre_ctx" if keep_f32 else "l1_pre_latent",
    )(x, mod, w["norm_mix"], w["w_in"], w["gq"], w["gk"])


def _post_kernel(n_parts, x_ref, *rest):
    parts = rest[:n_parts]
    mod_ref, nw_ref, wo_ref, w1_ref, w2_ref, o_ref = rest[n_parts:]
    mix_in = parts[0][...] if n_parts == 1 else jnp.concatenate([p[...] for p in parts], axis=-1)
    x1 = x_ref[...] + mod_ref[2:3, :] * _dot(mix_in, wo_ref[...])
    h = _modulate(x1, nw_ref[...], mod_ref[3:4, :], mod_ref[4:5, :]).astype(BF16)
    acc = jnp.zeros(x1.shape, F32)
    for c in range(D_FF // FF_CHUNK):
        cs = pl.ds(c * FF_CHUNK, FF_CHUNK)
        u = jnp.square(jnp.maximum(_dot(h, w1_ref[:, cs]), 0.0)).astype(BF16)
        acc = acc + _dot(u, w2_ref[cs, :])
    o_ref[...] = x1 + mod_ref[5:6, :] * acc


def _post(x, parts, mod, tiles_per_group, w, name):
    n = x.shape[0]
    tm = TOKEN_TILE
    row = lambda i: (i, 0)
    in_specs = [pl.BlockSpec((tm, D_MODEL), row)]
    in_specs += [pl.BlockSpec((tm, p.shape[1]), row) for p in parts]
    in_specs += [pl.BlockSpec((None, SUBLANES, D_MODEL), lambda i: (i // tiles_per_group, 0, 0)),
                 _const_spec((1, D_MODEL)),
                 _const_spec((D_MODEL, D_MODEL)),
                 _const_spec((D_MODEL, D_FF)),
                 _const_spec((D_FF, D_MODEL))]
    return pl.pallas_call(
        functools.partial(_post_kernel, len(parts)),
        out_shape=jax.ShapeDtypeStruct((n, D_MODEL), F32), grid=(n // tm,),
        in_specs=in_specs, out_specs=pl.BlockSpec((tm, D_MODEL), row),
        compiler_params=_params(1), name=name,
    )(x, *parts, mod, w["norm_mlp"], w["w_out"], w["w_mlp1"], w["w_mlp2"])


def _rope_tables(seq):
    half = MLA_ROPE // 2
    inv_freq = jnp.power(ROPE_THETA, -jnp.arange(0, half, 2, dtype=F32) / half)
    ang = jnp.arange(GRID_W, dtype=F32)[:, None] * inv_freq[None, :]
    cos, sin = jnp.cos(ang), jnp.sin(ang)
    rows = seq // GRID_W
    by_row = lambda t: jnp.broadcast_to(t[:rows, None, :], (rows, GRID_W, half // 2)).reshape(seq, half // 2)
    by_col = lambda t: jnp.broadcast_to(t[None, :, :], (rows, GRID_W, half // 2)).reshape(seq, half // 2)
    cr, sr, cc, sc = by_row(cos), by_row(sin), by_col(cos), by_col(sin)
    pad = LANES - MLA_QK
    c = jnp.concatenate([jnp.ones((seq, MLA_NOPE), F32), cr, cr, cc, cc, jnp.zeros((seq, pad), F32)], axis=1)
    s = jnp.concatenate([jnp.zeros((seq, MLA_NOPE), F32), -sr, sr, -sc, sc, jnp.zeros((seq, pad), F32)], axis=1)
    return c, s


def _rope_partner(tail):
    shape = tail.shape
    t = tail.reshape(shape[:-1] + (2, 2, MLA_ROPE // 4))
    return t[..., ::-1, :].reshape(shape)


def _na_bias_table(rel_bias, rows):
    qcol = jnp.arange(GRID_W)
    kcol = jnp.arange(GRID_W)
    col_start = jnp.clip(qcol - NA_KW // 2, 0, GRID_W - NA_KW)
    col_mask = (kcol[:, None] >= col_start[None, :]) & (kcol[:, None] < col_start[None, :] + NA_KW)
    offs = jnp.arange(2 * NA_KW - 1)
    onehot = (kcol[None, :, None] - qcol[None, None, :] + NA_KW - 1 == offs[:, None, None]).astype(F32)
    blocks = jnp.einsum('hdj,jkq->hdkq', rel_bias.astype(F32) * LOG2E, onehot, precision=lax.Precision.HIGHEST)
    blocks = jnp.where(col_mask[None, None], blocks, NEG_INF)
    neg = jnp.full_like(blocks[:, :1], NEG_INF)
    both = jnp.concatenate([blocks, jnp.concatenate([neg, blocks[:, :-1]], axis=1)], axis=-1)
    d_upper, d_lower = _na_partial_offsets(rows)
    upper_only = jnp.concatenate([blocks[:, d_upper:d_upper + 1], neg], axis=-1)
    lower_only = jnp.concatenate([neg, blocks[:, d_lower - 1:d_lower]], axis=-1)
    return jnp.concatenate([both, upper_only, lower_only, jnp.concatenate([neg, neg], axis=-1)], axis=1)


def _na_partial_offsets(rows):
    first = lambda r: min(max(r - NA_KH // 2, 0), rows - NA_KH)
    upper, lower = set(), set()
    for r0 in range(0, rows, NA_TILE_ROWS):
        w0 = min(max(r0 - NA_KH // 2, 0), rows - NA_WIN_ROWS)
        for j in range(NA_WIN_ROWS):
            for t in range(NA_TILE_ROWS // 2):
                key_row, i = w0 + j, r0 + 2 * t
                in_u = first(i) <= key_row < first(i) + NA_KH
                in_l = first(i + 1) <= key_row < first(i + 1) + NA_KH
                d = key_row - i + NA_KH - 1
                if in_u and in_l:
                    assert 1 <= d <= 2 * NA_KH - 2
                elif in_u:
                    upper.add(d)
                elif in_l:
                    lower.add(d)
    assert len(upper) == 1 and len(lower) == 1, (upper, lower)
    return upper.pop(), lower.pop()


def _prep_l0(w_in, q_lora_norm, kv_lora_norm, w_q_up, w_kv_up, mla_q_norm, mla_k_norm, w_pool, pool_scale,
             w_out, norm_mix, norm_mlp, w_mlp1, w_mlp2):
    lat_end = POOL_WIDTH + Q_LORA + KV_LORA
    w_in_pad = jnp.zeros((D_MODEL, D_MODEL), F32)
    w_in_pad = w_in_pad.at[:, :lat_end].set(w_in[:, :lat_end])
    w_in_pad = w_in_pad.at[:, lat_end + MLA_NOPE:lat_end + MLA_QK].set(w_in[:, lat_end:])
    w_in_pad = w_in_pad.at[:, lat_end + MLA_QK:].set(_rope_partner(w_in[:, lat_end:]))
    pad = LANES - MLA_QK
    hk = MLA_HEADS * LANES
    wq = w_q_up.reshape(Q_LORA, MLA_HEADS, MLA_QK)
    wq = jnp.concatenate([wq, _rope_partner(wq[:, :, MLA_NOPE:])], axis=-1)
    kv = w_kv_up.reshape(KV_LORA, MLA_HEADS, MLA_NOPE + MLA_V)
    wk = jnp.pad(kv[:, :, :MLA_NOPE], ((0, 0), (0, 0), (0, LANES - MLA_NOPE)))
    wv = jnp.pad(kv[:, :, MLA_NOPE:], ((0, 0), (0, 0), (0, LANES - MLA_V)))
    gq = mla_q_norm * (MLA_QK ** -0.5 * LOG2E)
    slab_lane = jnp.arange(2 * LANES)
    e2 = (slab_lane[:, None] // LANES == slab_lane[None, :] // LANES) & (slab_lane[:, None] % LANES < MLA_QK)
    return dict(
        norm_mix=norm_mix.reshape(1, -1), norm_mlp=norm_mlp.reshape(1, -1),
        w_in=w_in_pad.astype(BF16),
        qln=q_lora_norm.reshape(1, -1), kvn=kv_lora_norm.reshape(1, -1),
        wq=wq.reshape(Q_LORA, hk).astype(BF16),
        gq_plain=jnp.pad(gq, (0, pad)).reshape(1, LANES),
        gq_rope=jnp.concatenate([gq, _rope_partner(gq[MLA_NOPE:])]).reshape(1, LANES),
        wk=wk.reshape(KV_LORA, hk).astype(BF16),
        gk_plain=jnp.pad(mla_k_norm, (0, pad)).reshape(1, LANES),
        gk_rope=jnp.concatenate([mla_k_norm, _rope_partner(mla_k_norm[MLA_NOPE:])]).reshape(1, LANES),
        wv=wv.reshape(KV_LORA, hk).astype(BF16),
        e2=e2.astype(BF16),
        w_pool=w_pool.astype(BF16), pool_scale=pool_scale.reshape(1, -1),
        w_out=w_out.astype(BF16), w_mlp1=w_mlp1.astype(BF16), w_mlp2=w_mlp2.astype(BF16))


def _split_mods(m):
    mods = m.reshape(SUBLANES, N_ADA, D_MODEL)[:3]
    mods = jnp.pad(mods, ((0, 0), (0, SUBLANES - N_ADA), (0, 0)))
    return mods[0:1], mods[1:3]


def _to_slabs(x, n):
    return x.reshape(n, -1, LANES).transpose(1, 0, 2).astype(BF16)


def kernel(x_prompt, x_sample, cache_l0_mla_ckv, cache_l0_mla_krope, cache_l1_na_k, cache_l1_na_v, c, c_ctx, w_ada_l0, b_ada_l0, norm_mix_l0, norm_mlp_l0, w_mlp1_l0, w_mlp2_l0, w_in_l0, q_lora_norm_l0, kv_lora_norm_l0, w_q_up_l0, w_kv_up_l0, mla_q_norm_l0, mla_k_norm_l0, w_pool_l0, pool_scale_l0, w_out_l0, w_ada_l1, b_ada_l1, norm_mix_l1, norm_mlp_l1, w_mlp1_l1, w_mlp2_l1, w_in_l1, na_q_norm_l1, na_k_norm_l1, rel_bias_l1, w_out_l1):
    batch, seq, _ = x_prompt.shape
    dec_batch, dec_seq, _ = x_sample.shape
    past = cache_l0_mla_ckv.shape[1]
    xp = x_prompt.reshape(batch * seq, D_MODEL)
    xs = x_sample.reshape(dec_batch * dec_seq, D_MODEL)
    tiles_p = (batch * seq) // TOKEN_TILE
    tiles_s = dec_seq // TOKEN_TILE

    cond8 = jnp.zeros((SUBLANES, D_MODEL), F32).at[0].set(c_ctx).at[1:1 + dec_batch].set(c)
    mod0_p, mod0_s = _split_mods(_ada(cond8, w_ada_l0, b_ada_l0))
    mod1_p, mod1_s = _split_mods(_ada(cond8, w_ada_l1, b_ada_l1))

    w0 = _prep_l0(w_in_l0, q_lora_norm_l0, kv_lora_norm_l0, w_q_up_l0, w_kv_up_l0, mla_q_norm_l0, mla_k_norm_l0,
                  w_pool_l0, pool_scale_l0, w_out_l0, norm_mix_l0, norm_mlp_l0, w_mlp1_l0, w_mlp2_l0)
    a_p, q_p, k_p, v_p, ckv_p, kr_p = _l0_pre(xp, mod0_p, tiles_p, False, w0, None)
    a_s, q_s, k_s, v_s, _, _ = _l0_pre(xs, mod0_s, tiles_s, True, w0, _rope_tables(dec_seq))
    kr_cache = jnp.pad(cache_l0_mla_krope.reshape(dec_batch * past, MLA_ROPE), ((0, 0), (MLA_NOPE, LANES - MLA_QK)))
    k_c, v_c = _cache_kv(cache_l0_mla_ckv.reshape(dec_batch * past, KV_LORA), kr_cache, w0)
    attn_p = _ctx_attn(_mla_ctx_kernel, q_p, k_p, v_p, seq, MLA_HEADS * MLA_V, "l0_mla_ctx_attn")
    attn_s = _mla_lat_attn(q_s, k_s, v_s, k_c, v_c, dec_seq, past)
    xp = _post(xp, [_pool(a_p, seq, w0), attn_p], mod0_p, tiles_p, w0, "l0_post_ctx")
    xs = _post(xs, [_pool(a_s, dec_seq, w0), attn_s], mod0_s, tiles_s, w0, "l0_post_latent")

    w1 = dict(norm_mix=norm_mix_l1.reshape(1, -1), norm_mlp=norm_mlp_l1.reshape(1, -1),
              w_in=w_in_l1.astype(BF16),
              gq=jnp.tile(na_q_norm_l1 * (NA_HEAD_DIM ** -0.5 * LOG2E), 2).reshape(1, LANES),
              gk=jnp.tile(na_k_norm_l1, 2).reshape(1, LANES),
              w_out=w_out_l1.astype(BF16), w_mlp1=w_mlp1_l1.astype(BF16), w_mlp2=w_mlp2_l1.astype(BF16))
    q1_p, k1_p, v1_p, k1_new, v1_new = _l1_pre(xp, mod1_p, tiles_p, True, w1)
    q1_s, k1_s, v1_s = _l1_pre(xs, mod1_s, tiles_s, False, w1)
    na_p = _ctx_attn(_na_ctx_kernel, q1_p, k1_p, v1_p, seq, NA_SLABS * LANES, "l1_na_ctx_attn")
    na_s = _na_lat_attn(q1_s, k1_s, v1_s,
                        _to_slabs(cache_l1_na_k, dec_batch * past), _to_slabs(cache_l1_na_v, dec_batch * past),
                        _na_bias_table(rel_bias_l1, dec_seq // GRID_W),
                        jnp.max(jnp.abs(rel_bias_l1.astype(F32)) * LOG2E, axis=(1, 2)), dec_seq, past)
    xp = _post(xp, [na_p], mod1_p, tiles_p, w1, "l1_post_ctx")
    xs = _post(xs, [na_s], mod1_s, tiles_s, w1, "l1_post_latent")

    return (xp.reshape(batch, seq, D_MODEL), xs.reshape(dec_batch, dec_seq, D_MODEL),
            ckv_p.reshape(batch, seq, KV_LORA), kr_p.reshape(batch, seq, MLA_ROPE),
            k1_new.reshape(batch, seq, NA_HEADS, NA_HEAD_DIM), v1_new.reshape(batch, seq, NA_HEADS, NA_HEAD_DIM))
```

```python
import functools

import jax
import jax.numpy as jnp
from jax import lax
from jax.experimental import pallas as pl
from jax.experimental.pallas import tpu as pltpu

F32 = jnp.float32
BF16 = jnp.bfloat16

D_MODEL = 1024
D_FF = 4 * D_MODEL
N_ADA = 6
EPS = 1e-6
NEG_INF = -1e30
ROPE_THETA = 10000.0
GRID_W = 64
POOL_WIDTH = 512
POOL_WINDOWS = (2, 4, 8, 16)
POOL_GROUP = 128
POOL_HALO = 8
MLA_HEADS = 8
MLA_NOPE = 64
MLA_ROPE = 32
MLA_QK = MLA_NOPE + MLA_ROPE
MLA_V = 64
Q_LORA = 256
KV_LORA = 128
NA_HEADS = 16
NA_HEAD_DIM = 64
NA_KH = 8
NA_KW = 16
LOG2E = 1.4426950408889634

LANES = 128
SUBLANES = 8
VMEM_LIMIT = 56 * 1024 * 1024

TOKEN_TILE = 512
MLA_Q_TILE = 512
MLA_STEP_HEADS = 2
MLA_KEY_CHUNK = 2048
MLA_ONLINE_CHUNK = 1024
SCORE_LOOKAHEAD = 2
MAX_SCORE_BOUND = 40.0
FF_CHUNK = 1024
NA_SLABS = NA_HEADS * NA_HEAD_DIM // LANES
NA_TILE_ROWS = 4
NA_WIN_ROWS = NA_TILE_ROWS + NA_KH - 1


def _params(n_axes):
    return pltpu.CompilerParams(dimension_semantics=("arbitrary",) * n_axes,
                                vmem_limit_bytes=VMEM_LIMIT)


def _const_spec(shape):
    zeros = (0,) * len(shape)
    return pl.BlockSpec(shape, lambda *_: zeros, pipeline_mode=pl.Buffered(1))


def _dot(a, b):
    return jnp.dot(a, b, preferred_element_type=F32)


def _dot_nt(a, b):
    return lax.dot_general(a, b, (((1,), (1,)), ((), ())), preferred_element_type=F32)


def _dot_tn(a, b):
    return lax.dot_general(a, b, (((0,), (0,)), ((), ())), preferred_element_type=F32)


def _rms(x, w, n=None):
    n = x.shape[-1] if n is None else n
    ss = jnp.sum(x * x, axis=-1, keepdims=True)
    return x * lax.rsqrt(ss * (1.0 / n) + EPS) * w


def _modulate(x, nw, shift, scale):
    return _rms(x, nw) * (1.0 + scale) + shift


def _ada_kernel(cond_ref, w_ref, b_ref, o_ref):
    c = cond_ref[...]
    s = (c * jax.nn.sigmoid(c)).astype(BF16)
    o_ref[...] = _dot(s, w_ref[...].astype(BF16)) + b_ref[...]


def _ada(cond8, w_ada, b_ada):
    n_out = w_ada.shape[1]
    tn = D_MODEL
    return pl.pallas_call(
        _ada_kernel,
        out_shape=jax.ShapeDtypeStruct((SUBLANES, n_out), F32),
        grid=(n_out // tn,),
        in_specs=[pl.BlockSpec((SUBLANES, D_MODEL), lambda j: (0, 0)),
                  pl.BlockSpec((D_MODEL, tn), lambda j: (0, j)),
                  pl.BlockSpec((1, tn), lambda j: (0, j))],
        out_specs=pl.BlockSpec((SUBLANES, tn), lambda j: (0, j)),
        compiler_params=_params(1),
        name="ada",
    )(cond8, w_ada, b_ada.reshape(1, n_out))


def _rope(x, c, s):
    return x * c + pltpu.roll(x, LANES - MLA_ROPE, 1) * s


def _head_rinv(raw, e2_ref):
    sq = (raw * raw).astype(BF16)
    pair = 2 * LANES
    ss = jnp.concatenate([_dot(sq[:, p:p + pair], e2_ref[...]) for p in range(0, raw.shape[1], pair)], axis=1)
    return lax.rsqrt(ss * (1.0 / MLA_QK) + EPS)


def _mla_keys_values(ckv, kr_slab, wk_ref, gk, wv_ref, e2_ref, rope_tabs, k_ref, v_ref):
    c16 = ckv.astype(BF16)
    kn = _dot(c16, wk_ref[...])
    v = _dot(c16, wv_ref[...])
    rinv = _head_rinv(kn + jnp.tile(kr_slab, (1, MLA_HEADS)), e2_ref)
    tail = kr_slab * gk
    if rope_tabs is not None:
        tail = _rope(tail, *rope_tabs)
    ones_lane = (lax.broadcasted_iota(jnp.int32, (1, LANES), 1) == MLA_V).astype(F32)
    for h in range(MLA_HEADS):
        hs = slice(h * LANES, (h + 1) * LANES)
        k_ref[h] = (rinv[:, hs] * (kn[:, hs] * gk + tail)).astype(BF16)
        v_ref[h] = (v[:, hs] + ones_lane).astype(BF16)


def _l0_pre_kernel(latent, x_ref, mod_ref, nw_ref, win_ref, qln_ref, kvn_ref, wq_ref, gq_ref,
                   wk_ref, gk_ref, wv_ref, e2_ref, *rest):
    if latent:
        c_ref, s_ref, a_ref, q_ref, k_ref, v_ref, ckv_ref, kr_ref = rest
        rope_tabs = (c_ref[...], s_ref[...])
    else:
        a_ref, q_ref, k_ref, v_ref, ckv_ref, kr_ref = rest
        rope_tabs = None
    h = _modulate(x_ref[...], nw_ref[...], mod_ref[0:1, :], mod_ref[1:2, :]).astype(BF16)
    proj = _dot(h, win_ref[...])
    a_ref[...] = proj[:, :POOL_WIDTH]
    q_lat = proj[:, POOL_WIDTH:POOL_WIDTH + Q_LORA]
    kv_lat = proj[:, POOL_WIDTH + Q_LORA:POOL_WIDTH + Q_LORA + KV_LORA]
    kr_slab = proj[:, POOL_WIDTH + Q_LORA + KV_LORA:]
    q = _dot(_rms(q_lat, qln_ref[...]).astype(BF16), wq_ref[...])
    q_rinv = _head_rinv(q, e2_ref)
    gq = gq_ref[...]
    for hd in range(MLA_HEADS):
        hs = slice(hd * LANES, (hd + 1) * LANES)
        y = q[:, hs] * q_rinv[:, hs] * gq
        if rope_tabs is not None:
            y = _rope(y, *rope_tabs)
        q_ref[hd] = y.astype(BF16)
    ckv = _rms(kv_lat, kvn_ref[...])
    ckv_ref[...] = ckv
    kr_ref[...] = kr_slab[:, MLA_NOPE:MLA_QK]
    _mla_keys_values(ckv, kr_slab, wk_ref, gk_ref[...], wv_ref, e2_ref, rope_tabs, k_ref, v_ref)


def _slab_spec(slabs, tm):
    return pl.BlockSpec((slabs, tm, LANES), lambda i: (0, i, 0))


def _l0_pre(x, mod, tiles_per_group, latent, w, rope_tabs):
    n = x.shape[0]
    tm = TOKEN_TILE
    row = lambda i: (i, 0)
    hk = MLA_HEADS * LANES
    in_specs = [pl.BlockSpec((tm, D_MODEL), row),
                pl.BlockSpec((None, SUBLANES, D_MODEL), lambda i: (i // tiles_per_group, 0, 0)),
                _const_spec((1, D_MODEL)),
                _const_spec((D_MODEL, D_MODEL)),
                _const_spec((1, Q_LORA)),
                _const_spec((1, KV_LORA)),
                _const_spec((Q_LORA, hk)),
                _const_spec((1, LANES)),
                _const_spec((KV_LORA, hk)),
                _const_spec((1, LANES)),
                _const_spec((KV_LORA, hk)),
                _const_spec((2 * LANES, 2 * LANES))]
    gains = "rope" if latent else "plain"
    args = [x, mod, w["norm_mix"], w["w_in"], w["qln"], w["kvn"], w["wq"], w["gq_" + gains], w["wk"],
            w["gk_" + gains], w["wv"], w["e2"]]
    if latent:
        seq_tiles = rope_tabs[0].shape[0] // tm
        in_specs += [pl.BlockSpec((tm, LANES), lambda i: (i % seq_tiles, 0))] * 2
        args += list(rope_tabs)
    slab = jax.ShapeDtypeStruct((MLA_HEADS, n, LANES), BF16)
    out_shape = (jax.ShapeDtypeStruct((n, POOL_WIDTH), F32), slab, slab, slab,
                 jax.ShapeDtypeStruct((n, KV_LORA), F32),
                 jax.ShapeDtypeStruct((n, MLA_ROPE), F32))
    out_specs = (pl.BlockSpec((tm, POOL_WIDTH), row),
                 _slab_spec(MLA_HEADS, tm), _slab_spec(MLA_HEADS, tm), _slab_spec(MLA_HEADS, tm),
                 pl.BlockSpec((tm, KV_LORA), row), pl.BlockSpec((tm, MLA_ROPE), row))
    return pl.pallas_call(
        functools.partial(_l0_pre_kernel, latent),
        out_shape=out_shape, grid=(n // tm,), in_specs=in_specs, out_specs=out_specs,
        compiler_params=_params(1), name="l0_pre_latent" if latent else "l0_pre_ctx",
    )(*args)


def _cache_kv_kernel(ckv_ref, kr_ref, wk_ref, gk_ref, wv_ref, e2_ref, k_ref, v_ref):
    _mla_keys_values(ckv_ref[...], kr_ref[...], wk_ref, gk_ref[...], wv_ref, e2_ref, None, k_ref, v_ref)


def _cache_kv(ckv, kr_slab, w):
    n = ckv.shape[0]
    tm = TOKEN_TILE
    row = lambda i: (i, 0)
    hk = MLA_HEADS * LANES
    slab = jax.ShapeDtypeStruct((MLA_HEADS, n, LANES), BF16)
    return pl.pallas_call(
        _cache_kv_kernel,
        out_shape=(slab, slab),
        grid=(n // tm,),
        in_specs=[pl.BlockSpec((tm, KV_LORA), row), pl.BlockSpec((tm, LANES), row),
                  _const_spec((KV_LORA, hk)), _const_spec((1, LANES)), _const_spec((KV_LORA, hk)),
                  _const_spec((2 * LANES, 2 * LANES))],
        out_specs=(_slab_spec(MLA_HEADS, tm), _slab_spec(MLA_HEADS, tm)),
        compiler_params=_params(1), name="l0_cache_kv",
    )(ckv, kr_slab, w["wk"], w["gk_plain"], w["wv"], w["e2"])


def _pool_kernel(seq_len, prev_ref, u_ref, next_ref, wp_ref, ps_ref, o_ref, buf_ref):
    tm = u_ref.shape[0]
    pos0 = (pl.program_id(0) * tm) % seq_len
    buf_ref[0:POOL_HALO, :] = jnp.where(pos0 != 0, prev_ref[...], 0.0)
    buf_ref[POOL_HALO:POOL_HALO + tm, :] = u_ref[...]
    buf_ref[POOL_HALO + tm:, :] = jnp.where(pos0 + tm != seq_len, next_ref[...], 0.0)
    pos = pos0 + lax.broadcasted_iota(jnp.int32, (tm, 1), 0)
    for g, win in enumerate(POOL_WINDOWS):
        lanes = pl.ds(g * POOL_GROUP, POOL_GROUP)
        lo = pos - win // 2
        cnt = (jnp.minimum(lo + win, seq_len) - jnp.maximum(lo, 0)).astype(F32)
        acc = buf_ref[pl.ds(POOL_HALO - win // 2, tm), lanes]
        for d in range(1 - win // 2, win - win // 2):
            acc = acc + buf_ref[pl.ds(POOL_HALO + d, tm), lanes]
        pooled = acc / cnt - u_ref[:, lanes]
        o_ref[:, lanes] = (_dot(pooled.astype(BF16), wp_ref[g]) * ps_ref[:, lanes]).astype(BF16)


def _pool(a_in, seq_len, w):
    n = a_in.shape[0]
    tm = min(TOKEN_TILE, seq_len)
    hb = tm // POOL_HALO
    last = n // POOL_HALO - 1
    return pl.pallas_call(
        functools.partial(_pool_kernel, seq_len),
        out_shape=jax.ShapeDtypeStruct((n, POOL_WIDTH), BF16),
        grid=(n // tm,),
        in_specs=[pl.BlockSpec((POOL_HALO, POOL_WIDTH), lambda i: (jnp.maximum(i * hb - 1, 0), 0)),
                  pl.BlockSpec((tm, POOL_WIDTH), lambda i: (i, 0)),
                  pl.BlockSpec((POOL_HALO, POOL_WIDTH), lambda i: (jnp.minimum((i + 1) * hb, last), 0)),
                  _const_spec((len(POOL_WINDOWS), POOL_GROUP, POOL_GROUP)),
                  _const_spec((1, POOL_WIDTH))],
        out_specs=pl.BlockSpec((tm, POOL_WIDTH), lambda i: (i, 0)),
        scratch_shapes=[pltpu.VMEM((tm + 2 * POOL_HALO, POOL_WIDTH), F32)],
        compiler_params=_params(1), name="l0_pool",
    )(a_in, a_in, a_in, w["w_pool"], w["pool_scale"])


def _attend(streams):
    items = [(si, ch) for si, (_, chunks, _) in enumerate(streams) for ch in chunks]

    def score(item):
        si, (k_fn, _, extra_fn) = item
        s = _dot_nt(k_fn(), streams[si][0])
        return s if extra_fn is None else s + extra_fn()

    state = [None] * len(streams)
    pending = [score(item) for item in items[:SCORE_LOOKAHEAD]]
    for i, (si, (_, v_fn, _)) in enumerate(items):
        s = pending.pop(0)
        if i + SCORE_LOOKAHEAD < len(items):
            pending.append(score(items[i + SCORE_LOOKAHEAD]))
        want_den = streams[si][2]
        m_new = jnp.max(s, axis=0, keepdims=True)
        if state[si] is not None:
            m, acc, den = state[si]
            m_new = jnp.maximum(m, m_new)
            alpha = jnp.exp2(m - m_new)
        p = jnp.exp2(s - m_new)
        pv = _dot_tn(v_fn(), p.astype(BF16))
        psum = jnp.sum(p, axis=0, keepdims=True) if want_den else None
        if state[si] is None:
            state[si] = (m_new, pv, psum)
        else:
            state[si] = (m_new, acc * alpha + pv, den * alpha + psum if want_den else None)
    return [(acc, den) for _, acc, den in state]


def _ref_chunks(k_ref, v_ref, slab, chunk, extra_fn=None):
    nk = k_ref.shape[1]
    out = []
    for c in range(0, nk, chunk):
        rows = pl.ds(c, min(chunk, nk - c))
        out.append((lambda rows=rows: k_ref[slab, rows, :], lambda rows=rows: v_ref[slab, rows, :], extra_fn))
    return out


def _mla_finish(o_t):
    return o_t[:MLA_V] / o_t[MLA_V:MLA_V + 1]


def _na_queries(q2):
    low = lax.broadcasted_iota(jnp.int32, (1, LANES), 1) < NA_HEAD_DIM
    zero = jnp.zeros_like(q2)
    return jnp.concatenate([jnp.where(low, q2, zero), jnp.where(low, zero, q2)], axis=0)


def _na_finish(o_t, den):
    nq = o_t.shape[1] // 2
    o = o_t / den
    return jnp.concatenate([o[:NA_HEAD_DIM, :nq], o[NA_HEAD_DIM:, nq:]], axis=0)


def _mla_ctx_kernel(bound_ref, q_ref, k_ref, v_ref, o_ref):
    seq = k_ref.shape[1]
    heads = range(MLA_HEADS)

    def bounded():
        return [_bounded_attn(q_ref[h], _ref_chunks(k_ref, v_ref, h, seq)) for h in heads]

    def online():
        return [o for o, _ in _attend([(q_ref[h], _ref_chunks(k_ref, v_ref, h, seq), False) for h in heads])]

    outs = lax.cond(bound_ref[0, 0] <= MAX_SCORE_BOUND, bounded, online)
    for hp in range(MLA_HEADS // 2):
        o_t = jnp.concatenate([_mla_finish(outs[h]) for h in (2 * hp, 2 * hp + 1)], axis=0)
        o_ref[:, hp * LANES:(hp + 1) * LANES] = o_t.T.astype(BF16)


def _na_ctx_kernel(bound_ref, q_ref, k_ref, v_ref, o_ref):
    seq = k_ref.shape[1]

    def bounded():
        res = []
        for s in range(NA_SLABS):
            p = jnp.exp2(_dot_nt(k_ref[s], _na_queries(q_ref[s])))
            res.append(_na_finish(_dot_tn(v_ref[s], p.astype(BF16)), jnp.sum(p, axis=0, keepdims=True)))
        return res

    def online():
        res = _attend([(_na_queries(q_ref[s]), _ref_chunks(k_ref, v_ref, s, seq), True) for s in range(NA_SLABS)])
        return [_na_finish(o_t, den) for o_t, den in res]

    outs = lax.cond(bound_ref[0, 0] <= MAX_SCORE_BOUND, bounded, online)
    for s in range(NA_SLABS):
        o_ref[:, s * LANES:(s + 1) * LANES] = outs[s].T.astype(BF16)


def _ctx_attn(body, q, k, v, bound, seq, width, name):
    slabs, n, _ = q.shape
    spec = pl.BlockSpec((slabs, seq, LANES), lambda b: (0, b, 0))
    return pl.pallas_call(
        body,
        out_shape=jax.ShapeDtypeStruct((n, width), BF16),
        grid=(n // seq,),
        in_specs=[pl.BlockSpec(memory_space=pltpu.SMEM), spec, spec, spec],
        out_specs=pl.BlockSpec((seq, width), lambda b: (b, 0)),
        compiler_params=_params(1), name=name,
    )(bound.reshape(1, 1), q, k, v)


def _bounded_attn(q, chunks):
    acc = None
    for k_fn, v_fn, _ in chunks:
        p = jnp.exp2(_dot_nt(k_fn(), q)).astype(BF16)
        pv = _dot_tn(v_fn(), p)
        acc = pv if acc is None else acc + pv
    return acc


def _mla_lat_kernel(bound_ref, q_ref, kl_ref, vl_ref, kc_ref, vc_ref, o_ref):
    batch, head0 = pl.program_id(0), pl.program_id(1) * MLA_STEP_HEADS
    heads = range(MLA_STEP_HEADS)
    bounds = [bound_ref[batch, head0 + h] for h in heads]
    chunks = lambda h, size: _ref_chunks(kl_ref, vl_ref, h, size) + _ref_chunks(kc_ref, vc_ref, h, size)

    def bounded():
        return [_bounded_attn(q_ref[h], chunks(h, MLA_KEY_CHUNK)) for h in heads]

    def online():
        return [o for o, _ in _attend([(q_ref[h], chunks(h, MLA_ONLINE_CHUNK), False) for h in heads])]

    outs = lax.cond(functools.reduce(jnp.maximum, bounds) <= MAX_SCORE_BOUND, bounded, online)
    o_ref[...] = jnp.concatenate([_mla_finish(o) for o in outs], axis=0).T.astype(BF16)


def _normed_len(gain, dim):
    return (dim ** 0.5) * jnp.max(jnp.abs(gain.astype(F32)))


def _mla_lat_attn(q, k, v, kc, vc, bound, seq, past):
    n = q.shape[1]
    tq = MLA_Q_TILE
    nq = seq // tq
    hs = MLA_STEP_HEADS
    return pl.pallas_call(
        _mla_lat_kernel,
        out_shape=jax.ShapeDtypeStruct((n, MLA_HEADS * MLA_V), BF16),
        grid=(n // seq, MLA_HEADS // hs, nq),
        in_specs=[pl.BlockSpec(memory_space=pltpu.SMEM),
                  pl.BlockSpec((hs, tq, LANES), lambda b, h, i: (h, b * nq + i, 0)),
                  pl.BlockSpec((hs, seq, LANES), lambda b, h, i: (h, b, 0)),
                  pl.BlockSpec((hs, seq, LANES), lambda b, h, i: (h, b, 0)),
                  pl.BlockSpec((hs, past, LANES), lambda b, h, i: (h, b, 0)),
                  pl.BlockSpec((hs, past, LANES), lambda b, h, i: (h, b, 0))],
        out_specs=pl.BlockSpec((tq, hs * MLA_V), lambda b, h, i: (b * nq + i, h)),
        compiler_params=_params(3), name="l0_mla_latent_attn",
    )(bound, q, k, v, kc, vc)


def _na_first_row(r, rows):
    return jnp.clip(r - NA_KH // 2, 0, rows - NA_KH)


def _na_lat_kernel(rows, bound_ref, q_ref, k_ref, v_ref, kc_ref, vc_ref, t2_ref, o_ref):
    nq = NA_TILE_ROWS * GRID_W
    nk = NA_WIN_ROWS * GRID_W
    n_off = 2 * NA_KH - 1
    r0 = pl.program_id(1) * NA_TILE_ROWS
    w0 = jnp.clip(r0 - NA_KH // 2, 0, rows - NA_WIN_ROWS)

    def block_index(j, t):
        key_row = w0 + j
        i = r0 + 2 * t
        first_u, first_l = _na_first_row(i, rows), _na_first_row(i + 1, rows)
        in_u = (key_row >= first_u) & (key_row < first_u + NA_KH)
        in_l = (key_row >= first_l) & (key_row < first_l + NA_KH)
        d = jnp.clip(key_row - i + NA_KH - 1, 0, n_off - 1)
        return jnp.where(in_u & in_l, d, jnp.where(in_u, n_off, jnp.where(in_l, n_off + 1, n_off + 2)))

    idx = [[block_index(j, t) for t in range(NA_TILE_ROWS // 2)] for j in range(NA_WIN_ROWS)]

    def local_bias(s):
        heads = [jnp.concatenate([jnp.concatenate([t2_ref[2 * s + hh, i] for i in row], axis=1) for row in idx], axis=0)
                 for hh in range(2)]
        return jnp.concatenate(heads, axis=1)

    batch = pl.program_id(0)
    bounds = [bound_ref[batch, h] for h in range(NA_HEADS)]

    def bounded():
        res = []
        for s in range(NA_SLABS):
            q2 = _na_queries(q_ref[s])
            p_loc = jnp.exp2(_dot_nt(k_ref[s], q2) + local_bias(s))
            p_ctx = jnp.exp2(_dot_nt(kc_ref[s], q2))
            den = jnp.sum(p_loc, axis=0, keepdims=True) + jnp.sum(p_ctx, axis=0, keepdims=True)
            o_t = _dot_tn(v_ref[s], p_loc.astype(BF16)) + _dot_tn(vc_ref[s], p_ctx.astype(BF16))
            res.append(_na_finish(o_t, den))
        return res

    def online():
        res = _attend([(_na_queries(q_ref[s]),
                        _ref_chunks(k_ref, v_ref, s, nk, functools.partial(local_bias, s))
                        + _ref_chunks(kc_ref, vc_ref, s, nk),
                        True) for s in range(NA_SLABS)])
        return [_na_finish(o_t, den) for o_t, den in res]

    outs = lax.cond(functools.reduce(jnp.maximum, bounds) <= MAX_SCORE_BOUND, bounded, online)
    for s in range(NA_SLABS):
        o_ref[:, s * LANES:(s + 1) * LANES] = outs[s].T.astype(BF16)


def _na_lat_attn(q, k, v, kc, vc, t2, bound, seq, past):
    slabs, n, _ = q.shape
    rows = seq // GRID_W
    tiles = rows // NA_TILE_ROWS
    nq = NA_TILE_ROWS * GRID_W
    nk = NA_WIN_ROWS * GRID_W

    def win_map(b, t):
        w0 = jnp.clip(t * NA_TILE_ROWS - NA_KH // 2, 0, rows - NA_WIN_ROWS)
        return (0, pl.multiple_of(b * seq + w0 * GRID_W, GRID_W), 0)

    win_spec = pl.BlockSpec((pl.Element(slabs), pl.Element(nk), pl.Element(LANES)), win_map)
    ctx_spec = pl.BlockSpec((slabs, past, LANES), lambda b, t: (0, b, 0))
    return pl.pallas_call(
        functools.partial(_na_lat_kernel, rows),
        out_shape=jax.ShapeDtypeStruct((n, slabs * LANES), BF16),
        grid=(n // seq, tiles),
        in_specs=[pl.BlockSpec(memory_space=pltpu.SMEM),
                  pl.BlockSpec((slabs, nq, LANES), lambda b, t: (0, b * tiles + t, 0)),
                  win_spec, win_spec, ctx_spec, ctx_spec,
                  _const_spec(t2.shape)],
        out_specs=pl.BlockSpec((nq, slabs * LANES), lambda b, t: (b * tiles + t, 0)),
        compiler_params=_params(2), name="l1_na_latent_attn",
    )(bound, q, k, v, kc, vc, t2)


def _pair_norm(x2, g2):
    low = lax.broadcasted_iota(jnp.int32, (1, LANES), 1) < NA_HEAD_DIM
    sq = x2 * x2
    ss_all = jnp.sum(sq, axis=-1, keepdims=True)
    ss_low = jnp.sum(jnp.where(low, sq, 0.0), axis=-1, keepdims=True)
    ss = jnp.where(low, ss_low, ss_all - ss_low)
    return x2 * lax.rsqrt(ss * (1.0 / NA_HEAD_DIM) + EPS) * g2


def _l1_pre_kernel(keep_f32, x_ref, mod_ref, nw_ref, win_ref, gq_ref, gk_ref, q_ref, k_ref, v_ref, *f32_refs):
    h = _modulate(x_ref[...], nw_ref[...], mod_ref[0:1, :], mod_ref[1:2, :]).astype(BF16)
    width = NA_SLABS * LANES
    q = _dot(h, win_ref[:, 0:width])
    k = _dot(h, win_ref[:, width:2 * width])
    v = _dot(h, win_ref[:, 2 * width:3 * width])
    gq = gq_ref[...]
    gk = gk_ref[...]
    for s in range(NA_SLABS):
        ls = slice(s * LANES, (s + 1) * LANES)
        q_ref[s] = _pair_norm(q[:, ls], gq).astype(BF16)
        kn = _pair_norm(k[:, ls], gk)
        k_ref[s] = kn.astype(BF16)
        v_ref[s] = v[:, ls].astype(BF16)
        if keep_f32:
            f32_refs[0][:, ls] = kn
    if keep_f32:
        f32_refs[1][...] = v


def _l1_pre(x, mod, tiles_per_group, keep_f32, w):
    n = x.shape[0]
    tm = TOKEN_TILE
    row = lambda i: (i, 0)
    width = NA_SLABS * LANES
    slab = jax.ShapeDtypeStruct((NA_SLABS, n, LANES), BF16)
    out_shape = [slab, slab, slab]
    out_specs = [_slab_spec(NA_SLABS, tm)] * 3
    if keep_f32:
        out_shape += [jax.ShapeDtypeStruct((n, width), F32)] * 2
        out_specs += [pl.BlockSpec((tm, width), row)] * 2
    return pl.pallas_call(
        functools.partial(_l1_pre_kernel, keep_f32),
        out_shape=tuple(out_shape), grid=(n // tm,),
        in_specs=[pl.BlockSpec((tm, D_MODEL), row),
                  pl.BlockSpec((None, SUBLANES, D_MODEL), lambda i: (i // tiles_per_group, 0, 0)),
                  _const_spec((1, D_MODEL)),
                  _const_spec((D_MODEL, 3 * width)),
                  _const_spec((1, LANES)),
                  _const_spec((1, LANES))],
        out_specs=tuple(out_specs),
        compiler_params=_params(1), name="l1_pre_ctx" if keep_f32 else "l1_pre_latent",
    )(x, mod, w["norm_mix"], w["w_in"], w["gq"], w["gk"])


def _post_kernel(n_parts, x_ref, *rest):
    parts = rest[:n_parts]
    mod_ref, nw_ref, wo_ref, w1_ref, w2_ref, o_ref = rest[n_parts:]
    mix_in = parts[0][...] if n_parts == 1 else jnp.concatenate([p[...] for p in parts], axis=-1)
    x1 = x_ref[...] + mod_ref[2:3, :] * _dot(mix_in, wo_ref[...])
    h = _modulate(x1, nw_ref[...], mod_ref[3:4, :], mod_ref[4:5, :]).astype(BF16)
    acc = jnp.zeros(x1.shape, F32)
    for c in range(D_FF // FF_CHUNK):
        cs = pl.ds(c * FF_CHUNK, FF_CHUNK)
        u = jnp.square(jnp.maximum(_dot(h, w1_ref[:, cs]), 0.0)).astype(BF16)
        acc = acc + _dot(u, w2_ref[cs, :])
    o_ref[...] = x1 + mod_ref[5:6, :] * acc


def _post(x, parts, mod, tiles_per_group, w, name):
    n = x.shape[0]
    tm = TOKEN_TILE
    row = lambda i: (i, 0)
    in_specs = [pl.BlockSpec((tm, D_MODEL), row)]
    in_specs += [pl.BlockSpec((tm, p.shape[1]), row) for p in parts]
    in_specs += [pl.BlockSpec((None, SUBLANES, D_MODEL), lambda i: (i // tiles_per_group, 0, 0)),
                 _const_spec((1, D_MODEL)),
                 _const_spec((D_MODEL, D_MODEL)),
                 _const_spec((D_MODEL, D_FF)),
                 _const_spec((D_FF, D_MODEL))]
    return pl.pallas_call(
        functools.partial(_post_kernel, len(parts)),
        out_shape=jax.ShapeDtypeStruct((n, D_MODEL), F32), grid=(n // tm,),
        in_specs=in_specs, out_specs=pl.BlockSpec((tm, D_MODEL), row),
        compiler_params=_params(1), name=name,
    )(x, *parts, mod, w["norm_mlp"], w["w_out"], w["w_mlp1"], w["w_mlp2"])


def _rope_tables(seq):
    half = MLA_ROPE // 2
    inv_freq = jnp.power(ROPE_THETA, -jnp.arange(0, half, 2, dtype=F32) / half)
    ang = jnp.arange(GRID_W, dtype=F32)[:, None] * inv_freq[None, :]
    cos, sin = jnp.cos(ang), jnp.sin(ang)
    rows = seq // GRID_W
    by_row = lambda t: jnp.broadcast_to(t[:rows, None, :], (rows, GRID_W, half // 2)).reshape(seq, half // 2)
    by_col = lambda t: jnp.broadcast_to(t[None, :, :], (rows, GRID_W, half // 2)).reshape(seq, half // 2)
    cr, sr, cc, sc = by_row(cos), by_row(sin), by_col(cos), by_col(sin)
    pad = LANES - MLA_QK
    c = jnp.concatenate([jnp.ones((seq, MLA_NOPE), F32), cr, cr, cc, cc, jnp.zeros((seq, pad), F32)], axis=1)
    s = jnp.concatenate([jnp.zeros((seq, MLA_NOPE), F32), -sr, sr, -sc, sc, jnp.zeros((seq, pad), F32)], axis=1)
    return c, s


def _rope_partner(tail):
    shape = tail.shape
    t = tail.reshape(shape[:-1] + (2, 2, MLA_ROPE // 4))
    return t[..., ::-1, :].reshape(shape)


def _na_bias_table(rel_bias, rows):
    qcol = jnp.arange(GRID_W)
    kcol = jnp.arange(GRID_W)
    col_start = jnp.clip(qcol - NA_KW // 2, 0, GRID_W - NA_KW)
    col_mask = (kcol[:, None] >= col_start[None, :]) & (kcol[:, None] < col_start[None, :] + NA_KW)
    offs = jnp.arange(2 * NA_KW - 1)
    onehot = (kcol[None, :, None] - qcol[None, None, :] + NA_KW - 1 == offs[:, None, None]).astype(F32)
    blocks = jnp.einsum('hdj,jkq->hdkq', rel_bias.astype(F32) * LOG2E, onehot, precision=lax.Precision.HIGHEST)
    blocks = jnp.where(col_mask[None, None], blocks, NEG_INF)
    neg = jnp.full_like(blocks[:, :1], NEG_INF)
    both = jnp.concatenate([blocks, jnp.concatenate([neg, blocks[:, :-1]], axis=1)], axis=-1)
    d_upper, d_lower = _na_partial_offsets(rows)
    upper_only = jnp.concatenate([blocks[:, d_upper:d_upper + 1], neg], axis=-1)
    lower_only = jnp.concatenate([neg, blocks[:, d_lower - 1:d_lower]], axis=-1)
    return jnp.concatenate([both, upper_only, lower_only, jnp.concatenate([neg, neg], axis=-1)], axis=1)


def _na_partial_offsets(rows):
    first = lambda r: min(max(r - NA_KH // 2, 0), rows - NA_KH)
    upper, lower = set(), set()
    for r0 in range(0, rows, NA_TILE_ROWS):
        w0 = min(max(r0 - NA_KH // 2, 0), rows - NA_WIN_ROWS)
        for j in range(NA_WIN_ROWS):
            for t in range(NA_TILE_ROWS // 2):
                key_row, i = w0 + j, r0 + 2 * t
                in_u = first(i) <= key_row < first(i) + NA_KH
                in_l = first(i + 1) <= key_row < first(i + 1) + NA_KH
                d = key_row - i + NA_KH - 1
                if in_u and in_l:
                    assert 1 <= d <= 2 * NA_KH - 2
                elif in_u:
                    upper.add(d)
                elif in_l:
                    lower.add(d)
    assert len(upper) == 1 and len(lower) == 1, (upper, lower)
    return upper.pop(), lower.pop()


def _prep_l0(w_in, q_lora_norm, kv_lora_norm, w_q_up, w_kv_up, mla_q_norm, mla_k_norm, w_pool, pool_scale,
             w_out, norm_mix, norm_mlp, w_mlp1, w_mlp2):
    lat_end = POOL_WIDTH + Q_LORA + KV_LORA
    w_in_pad = jnp.zeros((D_MODEL, D_MODEL), F32)
    w_in_pad = w_in_pad.at[:, :lat_end].set(w_in[:, :lat_end])
    w_in_pad = w_in_pad.at[:, lat_end + MLA_NOPE:lat_end + MLA_QK].set(w_in[:, lat_end:])
    w_in_pad = w_in_pad.at[:, lat_end + MLA_QK:].set(_rope_partner(w_in[:, lat_end:]))
    pad = LANES - MLA_QK
    hk = MLA_HEADS * LANES
    wq = w_q_up.reshape(Q_LORA, MLA_HEADS, MLA_QK)
    wq = jnp.concatenate([wq, _rope_partner(wq[:, :, MLA_NOPE:])], axis=-1)
    kv = w_kv_up.reshape(KV_LORA, MLA_HEADS, MLA_NOPE + MLA_V)
    wk = jnp.pad(kv[:, :, :MLA_NOPE], ((0, 0), (0, 0), (0, LANES - MLA_NOPE)))
    wv = jnp.pad(kv[:, :, MLA_NOPE:], ((0, 0), (0, 0), (0, LANES - MLA_V)))
    gq = mla_q_norm * (MLA_QK ** -0.5 * LOG2E)
    slab_lane = jnp.arange(2 * LANES)
    e2 = (slab_lane[:, None] // LANES == slab_lane[None, :] // LANES) & (slab_lane[:, None] % LANES < MLA_QK)
    return dict(
        norm_mix=norm_mix.reshape(1, -1), norm_mlp=norm_mlp.reshape(1, -1),
        w_in=w_in_pad.astype(BF16),
        qln=q_lora_norm.reshape(1, -1), kvn=kv_lora_norm.reshape(1, -1),
        wq=wq.reshape(Q_LORA, hk).astype(BF16),
        gq_plain=jnp.pad(gq, (0, pad)).reshape(1, LANES),
        gq_rope=jnp.concatenate([gq, _rope_partner(gq[MLA_NOPE:])]).reshape(1, LANES),
        wk=wk.reshape(KV_LORA, hk).astype(BF16),
        gk_plain=jnp.pad(mla_k_norm, (0, pad)).reshape(1, LANES),
        gk_rope=jnp.concatenate([mla_k_norm, _rope_partner(mla_k_norm[MLA_NOPE:])]).reshape(1, LANES),
        wv=wv.reshape(KV_LORA, hk).astype(BF16),
        e2=e2.astype(BF16),
        w_pool=w_pool.astype(BF16), pool_scale=pool_scale.reshape(1, -1),
        w_out=w_out.astype(BF16), w_mlp1=w_mlp1.astype(BF16), w_mlp2=w_mlp2.astype(BF16))


def _split_mods(m):
    mods = m.reshape(SUBLANES, N_ADA, D_MODEL)[:3]
    mods = jnp.pad(mods, ((0, 0), (0, SUBLANES - N_ADA), (0, 0)))
    return mods[0:1], mods[1:3]


def _to_slabs(x, n):
    return x.reshape(n, -1, LANES).transpose(1, 0, 2).astype(BF16)


def kernel(x_prompt, x_sample, cache_l0_mla_ckv, cache_l0_mla_krope, cache_l1_na_k, cache_l1_na_v, c, c_ctx, w_ada_l0, b_ada_l0, norm_mix_l0, norm_mlp_l0, w_mlp1_l0, w_mlp2_l0, w_in_l0, q_lora_norm_l0, kv_lora_norm_l0, w_q_up_l0, w_kv_up_l0, mla_q_norm_l0, mla_k_norm_l0, w_pool_l0, pool_scale_l0, w_out_l0, w_ada_l1, b_ada_l1, norm_mix_l1, norm_mlp_l1, w_mlp1_l1, w_mlp2_l1, w_in_l1, na_q_norm_l1, na_k_norm_l1, rel_bias_l1, w_out_l1):
    batch, seq, _ = x_prompt.shape
    dec_batch, dec_seq, _ = x_sample.shape
    past = cache_l0_mla_ckv.shape[1]
    xp = x_prompt.reshape(batch * seq, D_MODEL)
    xs = x_sample.reshape(dec_batch * dec_seq, D_MODEL)
    tiles_p = (batch * seq) // TOKEN_TILE
    tiles_s = dec_seq // TOKEN_TILE

    cond8 = jnp.zeros((SUBLANES, D_MODEL), F32).at[0].set(c_ctx).at[1:1 + dec_batch].set(c)
    mod0_p, mod0_s = _split_mods(_ada(cond8, w_ada_l0, b_ada_l0))
    mod1_p, mod1_s = _split_mods(_ada(cond8, w_ada_l1, b_ada_l1))

    w0 = _prep_l0(w_in_l0, q_lora_norm_l0, kv_lora_norm_l0, w_q_up_l0, w_kv_up_l0, mla_q_norm_l0, mla_k_norm_l0,
                  w_pool_l0, pool_scale_l0, w_out_l0, norm_mix_l0, norm_mlp_l0, w_mlp1_l0, w_mlp2_l0)
    a_p, q_p, k_p, v_p, ckv_p, kr_p = _l0_pre(xp, mod0_p, tiles_p, False, w0, None)
    a_s, q_s, k_s, v_s, _, _ = _l0_pre(xs, mod0_s, tiles_s, True, w0, _rope_tables(dec_seq))
    kr_cache = jnp.pad(cache_l0_mla_krope.reshape(dec_batch * past, MLA_ROPE), ((0, 0), (MLA_NOPE, LANES - MLA_QK)))
    k_c, v_c = _cache_kv(cache_l0_mla_ckv.reshape(dec_batch * past, KV_LORA), kr_cache, w0)
    mla_bound = _normed_len(mla_q_norm_l0 * (MLA_QK ** -0.5 * LOG2E), MLA_QK) * _normed_len(mla_k_norm_l0, MLA_QK)
    attn_p = _ctx_attn(_mla_ctx_kernel, q_p, k_p, v_p, mla_bound, seq, MLA_HEADS * MLA_V, "l0_mla_ctx_attn")
    attn_s = _mla_lat_attn(q_s, k_s, v_s, k_c, v_c, jnp.full((dec_batch, MLA_HEADS), mla_bound, F32), dec_seq, past)
    xp = _post(xp, [_pool(a_p, seq, w0), attn_p], mod0_p, tiles_p, w0, "l0_post_ctx")
    xs = _post(xs, [_pool(a_s, dec_seq, w0), attn_s], mod0_s, tiles_s, w0, "l0_post_latent")

    w1 = dict(norm_mix=norm_mix_l1.reshape(1, -1), norm_mlp=norm_mlp_l1.reshape(1, -1),
              w_in=w_in_l1.astype(BF16),
              gq=jnp.tile(na_q_norm_l1 * (NA_HEAD_DIM ** -0.5 * LOG2E), 2).reshape(1, LANES),
              gk=jnp.tile(na_k_norm_l1, 2).reshape(1, LANES),
              w_out=w_out_l1.astype(BF16), w_mlp1=w_mlp1_l1.astype(BF16), w_mlp2=w_mlp2_l1.astype(BF16))
    q1_p, k1_p, v1_p, k1_new, v1_new = _l1_pre(xp, mod1_p, tiles_p, True, w1)
    q1_s, k1_s, v1_s = _l1_pre(xs, mod1_s, tiles_s, False, w1)
    q_len, k_len = _normed_len(w1["gq"], NA_HEAD_DIM), _normed_len(w1["gk"], NA_HEAD_DIM)
    na_p = _ctx_attn(_na_ctx_kernel, q1_p, k1_p, v1_p, q_len * k_len, seq, NA_SLABS * LANES, "l1_na_ctx_attn")
    cache_len = jnp.sqrt(jnp.max(jnp.sum(jnp.square(cache_l1_na_k.astype(F32)), axis=-1), axis=1))
    na_bound = (q_len * jnp.maximum(k_len, cache_len)
                + jnp.max(jnp.abs(rel_bias_l1.astype(F32)) * LOG2E, axis=(1, 2))[None, :])
    na_s = _na_lat_attn(q1_s, k1_s, v1_s,
                        _to_slabs(cache_l1_na_k, dec_batch * past), _to_slabs(cache_l1_na_v, dec_batch * past),
                        _na_bias_table(rel_bias_l1, dec_seq // GRID_W), na_bound, dec_seq, past)
    xp = _post(xp, [na_p], mod1_p, tiles_p, w1, "l1_post_ctx")
    xs = _post(xs, [na_s], mod1_s, tiles_s, w1, "l1_post_latent")

    return (xp.reshape(batch, seq, D_MODEL), xs.reshape(dec_batch, dec_seq, D_MODEL),
            ckv_p.reshape(batch, seq, KV_LORA), kr_p.reshape(batch, seq, MLA_ROPE),
            k1_new.reshape(batch, seq, NA_HEADS, NA_HEAD_DIM), v1_new.reshape(batch, seq, NA_HEADS, NA_HEAD_DIM))
```

```python
import functools

import jax
import jax.numpy as jnp
from jax import lax
from jax.experimental import pallas as pl
from jax.experimental.pallas import tpu as pltpu

F32 = jnp.float32
BF16 = jnp.bfloat16

D_MODEL = 1024
D_FF = 4 * D_MODEL
N_ADA = 6
EPS = 1e-6
NEG_INF = -1e30
ROPE_THETA = 10000.0
GRID_W = 64
POOL_WIDTH = 512
POOL_WINDOWS = (2, 4, 8, 16)
POOL_GROUP = 128
POOL_HALO = 8
MLA_HEADS = 8
MLA_NOPE = 64
MLA_ROPE = 32
MLA_QK = MLA_NOPE + MLA_ROPE
MLA_V = 64
Q_LORA = 256
KV_LORA = 128
NA_HEADS = 16
NA_HEAD_DIM = 64
NA_KH = 8
NA_KW = 16
LOG2E = 1.4426950408889634

LANES = 128
SUBLANES = 8
VMEM_LIMIT = 56 * 1024 * 1024

TOKEN_TILE = 512
MLA_Q_TILE = 512
MLA_STEP_HEADS = 2
MLA_KEY_CHUNK = 2048
MLA_ONLINE_CHUNK = 1024
SCORE_LOOKAHEAD = 2
MAX_SCORE_BOUND = 40.0
FF_CHUNK = 1024
NA_SLABS = NA_HEADS * NA_HEAD_DIM // LANES
NA_TILE_ROWS = 4
NA_WIN_ROWS = NA_TILE_ROWS + NA_KH - 1


def _params(n_axes):
    return pltpu.CompilerParams(dimension_semantics=("arbitrary",) * n_axes,
                                vmem_limit_bytes=VMEM_LIMIT)


def _const_spec(shape):
    zeros = (0,) * len(shape)
    return pl.BlockSpec(shape, lambda *_: zeros, pipeline_mode=pl.Buffered(1))


def _dot(a, b):
    return jnp.dot(a, b, preferred_element_type=F32)


def _dot_nt(a, b):
    return lax.dot_general(a, b, (((1,), (1,)), ((), ())), preferred_element_type=F32)


def _dot_tn(a, b):
    return lax.dot_general(a, b, (((0,), (0,)), ((), ())), preferred_element_type=F32)


def _rms(x, w, n=None):
    n = x.shape[-1] if n is None else n
    ss = jnp.sum(x * x, axis=-1, keepdims=True)
    return x * lax.rsqrt(ss * (1.0 / n) + EPS) * w


def _modulate(x, nw, shift, scale):
    return _rms(x, nw) * (1.0 + scale) + shift


def _ada_kernel(cond_ref, w_ref, b_ref, o_ref):
    c = cond_ref[...]
    s = (c * jax.nn.sigmoid(c)).astype(BF16)
    o_ref[...] = _dot(s, w_ref[...].astype(BF16)) + b_ref[...]


def _ada(cond8, w_ada, b_ada):
    n_out = w_ada.shape[1]
    tn = D_MODEL
    return pl.pallas_call(
        _ada_kernel,
        out_shape=jax.ShapeDtypeStruct((SUBLANES, n_out), F32),
        grid=(n_out // tn,),
        in_specs=[pl.BlockSpec((SUBLANES, D_MODEL), lambda j: (0, 0)),
                  pl.BlockSpec((D_MODEL, tn), lambda j: (0, j)),
                  pl.BlockSpec((1, tn), lambda j: (0, j))],
        out_specs=pl.BlockSpec((SUBLANES, tn), lambda j: (0, j)),
        compiler_params=_params(1),
        name="ada",
    )(cond8, w_ada, b_ada.reshape(1, n_out))


def _rope(x, c, s):
    return x * c + pltpu.roll(x, LANES - MLA_ROPE, 1) * s


def _head_rinv(raw, e2_ref):
    sq = (raw * raw).astype(BF16)
    pair = 2 * LANES
    ss = jnp.concatenate([_dot(sq[:, p:p + pair], e2_ref[...]) for p in range(0, raw.shape[1], pair)], axis=1)
    return lax.rsqrt(ss * (1.0 / MLA_QK) + EPS)


def _mla_keys_values(ckv, kr_slab, wk_ref, gk, wv_ref, e2_ref, rope_tabs, k_ref, v_ref):
    c16 = ckv.astype(BF16)
    kn = _dot(c16, wk_ref[...])
    v = _dot(c16, wv_ref[...])
    rinv = _head_rinv(kn + jnp.tile(kr_slab, (1, MLA_HEADS)), e2_ref)
    tail = kr_slab * gk
    if rope_tabs is not None:
        tail = _rope(tail, *rope_tabs)
    ones_lane = (lax.broadcasted_iota(jnp.int32, (1, LANES), 1) == MLA_V).astype(F32)
    for h in range(MLA_HEADS):
        hs = slice(h * LANES, (h + 1) * LANES)
        k_ref[h] = (rinv[:, hs] * (kn[:, hs] * gk + tail)).astype(BF16)
        v_ref[h] = (v[:, hs] + ones_lane).astype(BF16)


def _l0_pre_kernel(latent, x_ref, mod_ref, nw_ref, win_ref, qln_ref, kvn_ref, wq_ref, gq_ref,
                   wk_ref, gk_ref, wv_ref, e2_ref, *rest):
    if latent:
        c_ref, s_ref, a_ref, q_ref, k_ref, v_ref, ckv_ref, kr_ref = rest
        rope_tabs = (c_ref[...], s_ref[...])
    else:
        a_ref, q_ref, k_ref, v_ref, ckv_ref, kr_ref = rest
        rope_tabs = None
    h = _modulate(x_ref[...], nw_ref[...], mod_ref[0:1, :], mod_ref[1:2, :]).astype(BF16)
    proj = _dot(h, win_ref[...])
    a_ref[...] = proj[:, :POOL_WIDTH]
    q_lat = proj[:, POOL_WIDTH:POOL_WIDTH + Q_LORA]
    kv_lat = proj[:, POOL_WIDTH + Q_LORA:POOL_WIDTH + Q_LORA + KV_LORA]
    kr_slab = proj[:, POOL_WIDTH + Q_LORA + KV_LORA:]
    q = _dot(_rms(q_lat, qln_ref[...]).astype(BF16), wq_ref[...])
    q_rinv = _head_rinv(q, e2_ref)
    gq = gq_ref[...]
    for hd in range(MLA_HEADS):
        hs = slice(hd * LANES, (hd + 1) * LANES)
        y = q[:, hs] * q_rinv[:, hs] * gq
        if rope_tabs is not None:
            y = _rope(y, *rope_tabs)
        q_ref[hd] = y.astype(BF16)
    ckv = _rms(kv_lat, kvn_ref[...])
    ckv_ref[...] = ckv
    kr_ref[...] = kr_slab[:, MLA_NOPE:MLA_QK]
    _mla_keys_values(ckv, kr_slab, wk_ref, gk_ref[...], wv_ref, e2_ref, rope_tabs, k_ref, v_ref)


def _slab_spec(slabs, tm):
    return pl.BlockSpec((slabs, tm, LANES), lambda i: (0, i, 0))


def _l0_pre(x, mod, tiles_per_group, latent, w, rope_tabs):
    n = x.shape[0]
    tm = TOKEN_TILE
    row = lambda i: (i, 0)
    hk = MLA_HEADS * LANES
    in_specs = [pl.BlockSpec((tm, D_MODEL), row),
                pl.BlockSpec((None, SUBLANES, D_MODEL), lambda i: (i // tiles_per_group, 0, 0)),
                _const_spec((1, D_MODEL)),
                _const_spec((D_MODEL, D_MODEL)),
                _const_spec((1, Q_LORA)),
                _const_spec((1, KV_LORA)),
                _const_spec((Q_LORA, hk)),
                _const_spec((1, LANES)),
                _const_spec((KV_LORA, hk)),
                _const_spec((1, LANES)),
                _const_spec((KV_LORA, hk)),
                _const_spec((2 * LANES, 2 * LANES))]
    gains = "rope" if latent else "plain"
    args = [x, mod, w["norm_mix"], w["w_in"], w["qln"], w["kvn"], w["wq"], w["gq_" + gains], w["wk"],
            w["gk_" + gains], w["wv"], w["e2"]]
    if latent:
        seq_tiles = rope_tabs[0].shape[0] // tm
        in_specs += [pl.BlockSpec((tm, LANES), lambda i: (i % seq_tiles, 0))] * 2
        args += list(rope_tabs)
    slab = jax.ShapeDtypeStruct((MLA_HEADS, n, LANES), BF16)
    out_shape = (jax.ShapeDtypeStruct((n, POOL_WIDTH), F32), slab, slab, slab,
                 jax.ShapeDtypeStruct((n, KV_LORA), F32),
                 jax.ShapeDtypeStruct((n, MLA_ROPE), F32))
    out_specs = (pl.BlockSpec((tm, POOL_WIDTH), row),
                 _slab_spec(MLA_HEADS, tm), _slab_spec(MLA_HEADS, tm), _slab_spec(MLA_HEADS, tm),
                 pl.BlockSpec((tm, KV_LORA), row), pl.BlockSpec((tm, MLA_ROPE), row))
    return pl.pallas_call(
        functools.partial(_l0_pre_kernel, latent),
        out_shape=out_shape, grid=(n // tm,), in_specs=in_specs, out_specs=out_specs,
        compiler_params=_params(1), name="l0_pre_latent" if latent else "l0_pre_ctx",
    )(*args)


def _cache_kv_kernel(ckv_ref, kr_ref, wk_ref, gk_ref, wv_ref, e2_ref, k_ref, v_ref):
    _mla_keys_values(ckv_ref[...], kr_ref[...], wk_ref, gk_ref[...], wv_ref, e2_ref, None, k_ref, v_ref)


def _cache_kv(ckv, kr_slab, w):
    n = ckv.shape[0]
    tm = TOKEN_TILE
    row = lambda i: (i, 0)
    hk = MLA_HEADS * LANES
    slab = jax.ShapeDtypeStruct((MLA_HEADS, n, LANES), BF16)
    return pl.pallas_call(
        _cache_kv_kernel,
        out_shape=(slab, slab),
        grid=(n // tm,),
        in_specs=[pl.BlockSpec((tm, KV_LORA), row), pl.BlockSpec((tm, LANES), row),
                  _const_spec((KV_LORA, hk)), _const_spec((1, LANES)), _const_spec((KV_LORA, hk)),
                  _const_spec((2 * LANES, 2 * LANES))],
        out_specs=(_slab_spec(MLA_HEADS, tm), _slab_spec(MLA_HEADS, tm)),
        compiler_params=_params(1), name="l0_cache_kv",
    )(ckv, kr_slab, w["wk"], w["gk_plain"], w["wv"], w["e2"])


def _pool_kernel(seq_len, prev_ref, u_ref, next_ref, wp_ref, ps_ref, o_ref, buf_ref):
    tm = u_ref.shape[0]
    pos0 = (pl.program_id(0) * tm) % seq_len
    buf_ref[0:POOL_HALO, :] = jnp.where(pos0 != 0, prev_ref[...], 0.0)
    buf_ref[POOL_HALO:POOL_HALO + tm, :] = u_ref[...]
    buf_ref[POOL_HALO + tm:, :] = jnp.where(pos0 + tm != seq_len, next_ref[...], 0.0)
    pos = pos0 + lax.broadcasted_iota(jnp.int32, (tm, 1), 0)
    for g, win in enumerate(POOL_WINDOWS):
        lanes = pl.ds(g * POOL_GROUP, POOL_GROUP)
        lo = pos - win // 2
        cnt = (jnp.minimum(lo + win, seq_len) - jnp.maximum(lo, 0)).astype(F32)
        acc = buf_ref[pl.ds(POOL_HALO - win // 2, tm), lanes]
        for d in range(1 - win // 2, win - win // 2):
            acc = acc + buf_ref[pl.ds(POOL_HALO + d, tm), lanes]
        pooled = acc / cnt - u_ref[:, lanes]
        o_ref[:, lanes] = (_dot(pooled.astype(BF16), wp_ref[g]) * ps_ref[:, lanes]).astype(BF16)


def _pool(a_in, seq_len, w):
    n = a_in.shape[0]
    tm = min(TOKEN_TILE, seq_len)
    hb = tm // POOL_HALO
    last = n // POOL_HALO - 1
    return pl.pallas_call(
        functools.partial(_pool_kernel, seq_len),
        out_shape=jax.ShapeDtypeStruct((n, POOL_WIDTH), BF16),
        grid=(n // tm,),
        in_specs=[pl.BlockSpec((POOL_HALO, POOL_WIDTH), lambda i: (jnp.maximum(i * hb - 1, 0), 0)),
                  pl.BlockSpec((tm, POOL_WIDTH), lambda i: (i, 0)),
                  pl.BlockSpec((POOL_HALO, POOL_WIDTH), lambda i: (jnp.minimum((i + 1) * hb, last), 0)),
                  _const_spec((len(POOL_WINDOWS), POOL_GROUP, POOL_GROUP)),
                  _const_spec((1, POOL_WIDTH))],
        out_specs=pl.BlockSpec((tm, POOL_WIDTH), lambda i: (i, 0)),
        scratch_shapes=[pltpu.VMEM((tm + 2 * POOL_HALO, POOL_WIDTH), F32)],
        compiler_params=_params(1), name="l0_pool",
    )(a_in, a_in, a_in, w["w_pool"], w["pool_scale"])


def _attend(streams, running_max):
    items = [(si, ch) for si, (_, chunks, _) in enumerate(streams) for ch in chunks]

    def score(item):
        si, (k_fn, _, extra_fn) = item
        s = _dot_nt(k_fn(), streams[si][0])
        return s if extra_fn is None else s + extra_fn()

    state = [None] * len(streams)
    pending = [score(item) for item in items[:SCORE_LOOKAHEAD]]
    for i, (si, (_, v_fn, _)) in enumerate(items):
        s = pending.pop(0)
        if i + SCORE_LOOKAHEAD < len(items):
            pending.append(score(items[i + SCORE_LOOKAHEAD]))
        want_den = streams[si][2]
        m_new, alpha = None, 1.0
        if state[si] is not None:
            m, acc, den = state[si]
        if running_max:
            m_new = jnp.max(s, axis=0, keepdims=True)
            if state[si] is not None:
                m_new = jnp.maximum(m, m_new)
                alpha = jnp.exp2(m - m_new)
            s = s - m_new
        p = jnp.exp2(s)
        pv = _dot_tn(v_fn(), p.astype(BF16))
        psum = jnp.sum(p, axis=0, keepdims=True) if want_den else None
        if state[si] is None:
            state[si] = (m_new, pv, psum)
        elif running_max:
            state[si] = (m_new, acc * alpha + pv, den * alpha + psum if want_den else None)
        else:
            state[si] = (None, acc + pv, den + psum if want_den else None)
    return [(acc, den) for _, acc, den in state]


def _ref_chunks(k_ref, v_ref, slab, chunk, extra_fn=None):
    nk = k_ref.shape[1]
    out = []
    for c in range(0, nk, chunk):
        rows = pl.ds(c, min(chunk, nk - c))
        out.append((lambda rows=rows: k_ref[slab, rows, :], lambda rows=rows: v_ref[slab, rows, :], extra_fn))
    return out


def _mla_finish(o_t):
    return o_t[:MLA_V] / o_t[MLA_V:MLA_V + 1]


def _na_queries(q2):
    low = lax.broadcasted_iota(jnp.int32, (1, LANES), 1) < NA_HEAD_DIM
    zero = jnp.zeros_like(q2)
    return jnp.concatenate([jnp.where(low, q2, zero), jnp.where(low, zero, q2)], axis=0)


def _na_finish(o_t, den):
    nq = o_t.shape[1] // 2
    o = o_t / den
    return jnp.concatenate([o[:NA_HEAD_DIM, :nq], o[NA_HEAD_DIM:, nq:]], axis=0)


def _attend_guarded(bound, streams_fn):
    return lax.cond(bound <= MAX_SCORE_BOUND,
                    lambda: _attend(streams_fn(False), False),
                    lambda: _attend(streams_fn(True), True))


def _mla_ctx_kernel(bound_ref, q_ref, k_ref, v_ref, o_ref):
    seq = k_ref.shape[1]
    res = _attend_guarded(bound_ref[0, 0], lambda _: [(q_ref[h], _ref_chunks(k_ref, v_ref, h, seq), False)
                                                      for h in range(MLA_HEADS)])
    for hp in range(MLA_HEADS // 2):
        o_t = jnp.concatenate([_mla_finish(res[h][0]) for h in (2 * hp, 2 * hp + 1)], axis=0)
        o_ref[:, hp * LANES:(hp + 1) * LANES] = o_t.T.astype(BF16)


def _na_ctx_kernel(bound_ref, q_ref, k_ref, v_ref, o_ref):
    seq = k_ref.shape[1]
    res = _attend_guarded(bound_ref[0, 0], lambda _: [(_na_queries(q_ref[s]), _ref_chunks(k_ref, v_ref, s, seq), True)
                                                      for s in range(NA_SLABS)])
    for s in range(NA_SLABS):
        o_ref[:, s * LANES:(s + 1) * LANES] = _na_finish(*res[s]).T.astype(BF16)


def _ctx_attn(body, q, k, v, bound, seq, width, name):
    slabs, n, _ = q.shape
    spec = pl.BlockSpec((slabs, seq, LANES), lambda b: (0, b, 0))
    return pl.pallas_call(
        body,
        out_shape=jax.ShapeDtypeStruct((n, width), BF16),
        grid=(n // seq,),
        in_specs=[pl.BlockSpec(memory_space=pltpu.SMEM), spec, spec, spec],
        out_specs=pl.BlockSpec((seq, width), lambda b: (b, 0)),
        compiler_params=_params(1), name=name,
    )(bound.reshape(1, 1), q, k, v)


def _mla_lat_kernel(bound_ref, q_ref, kl_ref, vl_ref, kc_ref, vc_ref, o_ref):
    batch, head0 = pl.program_id(0), pl.program_id(1) * MLA_STEP_HEADS
    heads = range(MLA_STEP_HEADS)
    bounds = [bound_ref[batch, head0 + h] for h in heads]

    def streams(running_max):
        size = MLA_ONLINE_CHUNK if running_max else MLA_KEY_CHUNK
        return [(q_ref[h], _ref_chunks(kl_ref, vl_ref, h, size) + _ref_chunks(kc_ref, vc_ref, h, size), False)
                for h in heads]

    res = _attend_guarded(functools.reduce(jnp.maximum, bounds), streams)
    o_ref[...] = jnp.concatenate([_mla_finish(o) for o, _ in res], axis=0).T.astype(BF16)


def _normed_len(gain, dim):
    return (dim ** 0.5) * jnp.max(jnp.abs(gain.astype(F32)))


def _mla_lat_attn(q, k, v, kc, vc, bound, seq, past):
    n = q.shape[1]
    tq = MLA_Q_TILE
    nq = seq // tq
    hs = MLA_STEP_HEADS
    return pl.pallas_call(
        _mla_lat_kernel,
        out_shape=jax.ShapeDtypeStruct((n, MLA_HEADS * MLA_V), BF16),
        grid=(n // seq, MLA_HEADS // hs, nq),
        in_specs=[pl.BlockSpec(memory_space=pltpu.SMEM),
                  pl.BlockSpec((hs, tq, LANES), lambda b, h, i: (h, b * nq + i, 0)),
                  pl.BlockSpec((hs, seq, LANES), lambda b, h, i: (h, b, 0)),
                  pl.BlockSpec((hs, seq, LANES), lambda b, h, i: (h, b, 0)),
                  pl.BlockSpec((hs, past, LANES), lambda b, h, i: (h, b, 0)),
                  pl.BlockSpec((hs, past, LANES), lambda b, h, i: (h, b, 0))],
        out_specs=pl.BlockSpec((tq, hs * MLA_V), lambda b, h, i: (b * nq + i, h)),
        compiler_params=_params(3), name="l0_mla_latent_attn",
    )(bound, q, k, v, kc, vc)


def _na_first_row(r, rows):
    return jnp.clip(r - NA_KH // 2, 0, rows - NA_KH)


def _na_lat_kernel(rows, bound_ref, q_ref, k_ref, v_ref, kc_ref, vc_ref, t2_ref, o_ref):
    nq = NA_TILE_ROWS * GRID_W
    nk = NA_WIN_ROWS * GRID_W
    n_off = 2 * NA_KH - 1
    r0 = pl.program_id(1) * NA_TILE_ROWS
    w0 = jnp.clip(r0 - NA_KH // 2, 0, rows - NA_WIN_ROWS)

    def block_index(j, t):
        key_row = w0 + j
        i = r0 + 2 * t
        first_u, first_l = _na_first_row(i, rows), _na_first_row(i + 1, rows)
        in_u = (key_row >= first_u) & (key_row < first_u + NA_KH)
        in_l = (key_row >= first_l) & (key_row < first_l + NA_KH)
        d = jnp.clip(key_row - i + NA_KH - 1, 0, n_off - 1)
        return jnp.where(in_u & in_l, d, jnp.where(in_u, n_off, jnp.where(in_l, n_off + 1, n_off + 2)))

    idx = [[block_index(j, t) for t in range(NA_TILE_ROWS // 2)] for j in range(NA_WIN_ROWS)]

    def local_bias(s):
        heads = [jnp.concatenate([jnp.concatenate([t2_ref[2 * s + hh, i] for i in row], axis=1) for row in idx], axis=0)
                 for hh in range(2)]
        return jnp.concatenate(heads, axis=1)

    batch = pl.program_id(0)
    bounds = [bound_ref[batch, h] for h in range(NA_HEADS)]

    res = _attend_guarded(
        functools.reduce(jnp.maximum, bounds),
        lambda _: [(_na_queries(q_ref[s]),
                    _ref_chunks(k_ref, v_ref, s, nk, functools.partial(local_bias, s)) + _ref_chunks(kc_ref, vc_ref, s, nk),
                    True) for s in range(NA_SLABS)])
    for s in range(NA_SLABS):
        o_ref[:, s * LANES:(s + 1) * LANES] = _na_finish(*res[s]).T.astype(BF16)


def _na_lat_attn(q, k, v, kc, vc, t2, bound, seq, past):
    slabs, n, _ = q.shape
    rows = seq // GRID_W
    tiles = rows // NA_TILE_ROWS
    nq = NA_TILE_ROWS * GRID_W
    nk = NA_WIN_ROWS * GRID_W

    def win_map(b, t):
        w0 = jnp.clip(t * NA_TILE_ROWS - NA_KH // 2, 0, rows - NA_WIN_ROWS)
        return (0, pl.multiple_of(b * seq + w0 * GRID_W, GRID_W), 0)

    win_spec = pl.BlockSpec((pl.Element(slabs), pl.Element(nk), pl.Element(LANES)), win_map)
    ctx_spec = pl.BlockSpec((slabs, past, LANES), lambda b, t: (0, b, 0))
    return pl.pallas_call(
        functools.partial(_na_lat_kernel, rows),
        out_shape=jax.ShapeDtypeStruct((n, slabs * LANES), BF16),
        grid=(n // seq, tiles),
        in_specs=[pl.BlockSpec(memory_space=pltpu.SMEM),
                  pl.BlockSpec((slabs, nq, LANES), lambda b, t: (0, b * tiles + t, 0)),
                  win_spec, win_spec, ctx_spec, ctx_spec,
                  _const_spec(t2.shape)],
        out_specs=pl.BlockSpec((nq, slabs * LANES), lambda b, t: (b * tiles + t, 0)),
        compiler_params=_params(2), name="l1_na_latent_attn",
    )(bound, q, k, v, kc, vc, t2)


def _pair_norm(x2, g2):
    low = lax.broadcasted_iota(jnp.int32, (1, LANES), 1) < NA_HEAD_DIM
    sq = x2 * x2
    ss_all = jnp.sum(sq, axis=-1, keepdims=True)
    ss_low = jnp.sum(jnp.where(low, sq, 0.0), axis=-1, keepdims=True)
    ss = jnp.where(low, ss_low, ss_all - ss_low)
    return x2 * lax.rsqrt(ss * (1.0 / NA_HEAD_DIM) + EPS) * g2


def _l1_pre_kernel(keep_f32, x_ref, mod_ref, nw_ref, win_ref, gq_ref, gk_ref, q_ref, k_ref, v_ref, *f32_refs):
    h = _modulate(x_ref[...], nw_ref[...], mod_ref[0:1, :], mod_ref[1:2, :]).astype(BF16)
    width = NA_SLABS * LANES
    q = _dot(h, win_ref[:, 0:width])
    k = _dot(h, win_ref[:, width:2 * width])
    v = _dot(h, win_ref[:, 2 * width:3 * width])
    gq = gq_ref[...]
    gk = gk_ref[...]
    for s in range(NA_SLABS):
        ls = slice(s * LANES, (s + 1) * LANES)
        q_ref[s] = _pair_norm(q[:, ls], gq).astype(BF16)
        kn = _pair_norm(k[:, ls], gk)
        k_ref[s] = kn.astype(BF16)
        v_ref[s] = v[:, ls].astype(BF16)
        if keep_f32:
            f32_refs[0][:, ls] = kn
    if keep_f32:
        f32_refs[1][...] = v


def _l1_pre(x, mod, tiles_per_group, keep_f32, w):
    n = x.shape[0]
    tm = TOKEN_TILE
    row = lambda i: (i, 0)
    width = NA_SLABS * LANES
    slab = jax.ShapeDtypeStruct((NA_SLABS, n, LANES), BF16)
    out_shape = [slab, slab, slab]
    out_specs = [_slab_spec(NA_SLABS, tm)] * 3
    if keep_f32:
        out_shape += [jax.ShapeDtypeStruct((n, width), F32)] * 2
        out_specs += [pl.BlockSpec((tm, width), row)] * 2
    return pl.pallas_call(
        functools.partial(_l1_pre_kernel, keep_f32),
        out_shape=tuple(out_shape), grid=(n // tm,),
        in_specs=[pl.BlockSpec((tm, D_MODEL), row),
                  pl.BlockSpec((None, SUBLANES, D_MODEL), lambda i: (i // tiles_per_group, 0, 0)),
                  _const_spec((1, D_MODEL)),
                  _const_spec((D_MODEL, 3 * width)),
                  _const_spec((1, LANES)),
                  _const_spec((1, LANES))],
        out_specs=tuple(out_specs),
        compiler_params=_params(1), name="l1_pre_ctx" if keep_f32 else "l1_pre_latent",
    )(x, mod, w["norm_mix"], w["w_in"], w["gq"], w["gk"])


def _post_kernel(n_parts, x_ref, *rest):
    parts = rest[:n_parts]
    mod_ref, nw_ref, wo_ref, w1_ref, w2_ref, o_ref = rest[n_parts:]
    mix_in = parts[0][...] if n_parts == 1 else jnp.concatenate([p[...] for p in parts], axis=-1)
    x1 = x_ref[...] + mod_ref[2:3, :] * _dot(mix_in, wo_ref[...])
    h = _modulate(x1, nw_ref[...], mod_ref[3:4, :], mod_ref[4:5, :]).astype(BF16)
    acc = jnp.zeros(x1.shape, F32)
    for c in range(D_FF // FF_CHUNK):
        cs = pl.ds(c * FF_CHUNK, FF_CHUNK)
        u = jnp.square(jnp.maximum(_dot(h, w1_ref[:, cs]), 0.0)).astype(BF16)
        acc = acc + _dot(u, w2_ref[cs, :])
    o_ref[...] = x1 + mod_ref[5:6, :] * acc


def _post(x, parts, mod, tiles_per_group, w, name):
    n = x.shape[0]
    tm = TOKEN_TILE
    row = lambda i: (i, 0)
    in_specs = [pl.BlockSpec((tm, D_MODEL), row)]
    in_specs += [pl.BlockSpec((tm, p.shape[1]), row) for p in parts]
    in_specs += [pl.BlockSpec((None, SUBLANES, D_MODEL), lambda i: (i // tiles_per_group, 0, 0)),
                 _const_spec((1, D_MODEL)),
                 _const_spec((D_MODEL, D_MODEL)),
                 _const_spec((D_MODEL, D_FF)),
                 _const_spec((D_FF, D_MODEL))]
    return pl.pallas_call(
        functools.partial(_post_kernel, len(parts)),
        out_shape=jax.ShapeDtypeStruct((n, D_MODEL), F32), grid=(n // tm,),
        in_specs=in_specs, out_specs=pl.BlockSpec((tm, D_MODEL), row),
        compiler_params=_params(1), name=name,
    )(x, *parts, mod, w["norm_mlp"], w["w_out"], w["w_mlp1"], w["w_mlp2"])


def _rope_tables(seq):
    half = MLA_ROPE // 2
    inv_freq = jnp.power(ROPE_THETA, -jnp.arange(0, half, 2, dtype=F32) / half)
    ang = jnp.arange(GRID_W, dtype=F32)[:, None] * inv_freq[None, :]
    cos, sin = jnp.cos(ang), jnp.sin(ang)
    rows = seq // GRID_W
    by_row = lambda t: jnp.broadcast_to(t[:rows, None, :], (rows, GRID_W, half // 2)).reshape(seq, half // 2)
    by_col = lambda t: jnp.broadcast_to(t[None, :, :], (rows, GRID_W, half // 2)).reshape(seq, half // 2)
    cr, sr, cc, sc = by_row(cos), by_row(sin), by_col(cos), by_col(sin)
    pad = LANES - MLA_QK
    c = jnp.concatenate([jnp.ones((seq, MLA_NOPE), F32), cr, cr, cc, cc, jnp.zeros((seq, pad), F32)], axis=1)
    s = jnp.concatenate([jnp.zeros((seq, MLA_NOPE), F32), -sr, sr, -sc, sc, jnp.zeros((seq, pad), F32)], axis=1)
    return c, s


def _rope_partner(tail):
    shape = tail.shape
    t = tail.reshape(shape[:-1] + (2, 2, MLA_ROPE // 4))
    return t[..., ::-1, :].reshape(shape)


def _na_bias_table(rel_bias, rows):
    qcol = jnp.arange(GRID_W)
    kcol = jnp.arange(GRID_W)
    col_start = jnp.clip(qcol - NA_KW // 2, 0, GRID_W - NA_KW)
    col_mask = (kcol[:, None] >= col_start[None, :]) & (kcol[:, None] < col_start[None, :] + NA_KW)
    offs = jnp.arange(2 * NA_KW - 1)
    onehot = (kcol[None, :, None] - qcol[None, None, :] + NA_KW - 1 == offs[:, None, None]).astype(F32)
    blocks = jnp.einsum('hdj,jkq->hdkq', rel_bias.astype(F32) * LOG2E, onehot, precision=lax.Precision.HIGHEST)
    blocks = jnp.where(col_mask[None, None], blocks, NEG_INF)
    neg = jnp.full_like(blocks[:, :1], NEG_INF)
    both = jnp.concatenate([blocks, jnp.concatenate([neg, blocks[:, :-1]], axis=1)], axis=-1)
    d_upper, d_lower = _na_partial_offsets(rows)
    upper_only = jnp.concatenate([blocks[:, d_upper:d_upper + 1], neg], axis=-1)
    lower_only = jnp.concatenate([neg, blocks[:, d_lower - 1:d_lower]], axis=-1)
    return jnp.concatenate([both, upper_only, lower_only, jnp.concatenate([neg, neg], axis=-1)], axis=1)


def _na_partial_offsets(rows):
    first = lambda r: min(max(r - NA_KH // 2, 0), rows - NA_KH)
    upper, lower = set(), set()
    for r0 in range(0, rows, NA_TILE_ROWS):
        w0 = min(max(r0 - NA_KH // 2, 0), rows - NA_WIN_ROWS)
        for j in range(NA_WIN_ROWS):
            for t in range(NA_TILE_ROWS // 2):
                key_row, i = w0 + j, r0 + 2 * t
                in_u = first(i) <= key_row < first(i) + NA_KH
                in_l = first(i + 1) <= key_row < first(i + 1) + NA_KH
                d = key_row - i + NA_KH - 1
                if in_u and in_l:
                    assert 1 <= d <= 2 * NA_KH - 2
                elif in_u:
                    upper.add(d)
                elif in_l:
                    lower.add(d)
    assert len(upper) == 1 and len(lower) == 1, (upper, lower)
    return upper.pop(), lower.pop()


def _prep_l0(w_in, q_lora_norm, kv_lora_norm, w_q_up, w_kv_up, mla_q_norm, mla_k_norm, w_pool, pool_scale,
             w_out, norm_mix, norm_mlp, w_mlp1, w_mlp2):
    lat_end = POOL_WIDTH + Q_LORA + KV_LORA
    w_in_pad = jnp.zeros((D_MODEL, D_MODEL), F32)
    w_in_pad = w_in_pad.at[:, :lat_end].set(w_in[:, :lat_end])
    w_in_pad = w_in_pad.at[:, lat_end + MLA_NOPE:lat_end + MLA_QK].set(w_in[:, lat_end:])
    w_in_pad = w_in_pad.at[:, lat_end + MLA_QK:].set(_rope_partner(w_in[:, lat_end:]))
    pad = LANES - MLA_QK
    hk = MLA_HEADS * LANES
    wq = w_q_up.reshape(Q_LORA, MLA_HEADS, MLA_QK)
    wq = jnp.concatenate([wq, _rope_partner(wq[:, :, MLA_NOPE:])], axis=-1)
    kv = w_kv_up.reshape(KV_LORA, MLA_HEADS, MLA_NOPE + MLA_V)
    wk = jnp.pad(kv[:, :, :MLA_NOPE], ((0, 0), (0, 0), (0, LANES - MLA_NOPE)))
    wv = jnp.pad(kv[:, :, MLA_NOPE:], ((0, 0), (0, 0), (0, LANES - MLA_V)))
    gq = mla_q_norm * (MLA_QK ** -0.5 * LOG2E)
    slab_lane = jnp.arange(2 * LANES)
    e2 = (slab_lane[:, None] // LANES == slab_lane[None, :] // LANES) & (slab_lane[:, None] % LANES < MLA_QK)
    return dict(
        norm_mix=norm_mix.reshape(1, -1), norm_mlp=norm_mlp.reshape(1, -1),
        w_in=w_in_pad.astype(BF16),
        qln=q_lora_norm.reshape(1, -1), kvn=kv_lora_norm.reshape(1, -1),
        wq=wq.reshape(Q_LORA, hk).astype(BF16),
        gq_plain=jnp.pad(gq, (0, pad)).reshape(1, LANES),
        gq_rope=jnp.concatenate([gq, _rope_partner(gq[MLA_NOPE:])]).reshape(1, LANES),
        wk=wk.reshape(KV_LORA, hk).astype(BF16),
        gk_plain=jnp.pad(mla_k_norm, (0, pad)).reshape(1, LANES),
        gk_rope=jnp.concatenate([mla_k_norm, _rope_partner(mla_k_norm[MLA_NOPE:])]).reshape(1, LANES),
        wv=wv.reshape(KV_LORA, hk).astype(BF16),
        e2=e2.astype(BF16),
        w_pool=w_pool.astype(BF16), pool_scale=pool_scale.reshape(1, -1),
        w_out=w_out.astype(BF16), w_mlp1=w_mlp1.astype(BF16), w_mlp2=w_mlp2.astype(BF16))


def _split_mods(m):
    mods = m.reshape(SUBLANES, N_ADA, D_MODEL)[:3]
    mods = jnp.pad(mods, ((0, 0), (0, SUBLANES - N_ADA), (0, 0)))
    return mods[0:1], mods[1:3]


def _to_slabs(x, n):
    return x.reshape(n, -1, LANES).transpose(1, 0, 2).astype(BF16)


def kernel(x_prompt, x_sample, cache_l0_mla_ckv, cache_l0_mla_krope, cache_l1_na_k, cache_l1_na_v, c, c_ctx, w_ada_l0, b_ada_l0, norm_mix_l0, norm_mlp_l0, w_mlp1_l0, w_mlp2_l0, w_in_l0, q_lora_norm_l0, kv_lora_norm_l0, w_q_up_l0, w_kv_up_l0, mla_q_norm_l0, mla_k_norm_l0, w_pool_l0, pool_scale_l0, w_out_l0, w_ada_l1, b_ada_l1, norm_mix_l1, norm_mlp_l1, w_mlp1_l1, w_mlp2_l1, w_in_l1, na_q_norm_l1, na_k_norm_l1, rel_bias_l1, w_out_l1):
    batch, seq, _ = x_prompt.shape
    dec_batch, dec_seq, _ = x_sample.shape
    past = cache_l0_mla_ckv.shape[1]
    xp = x_prompt.reshape(batch * seq, D_MODEL)
    xs = x_sample.reshape(dec_batch * dec_seq, D_MODEL)
    tiles_p = (batch * seq) // TOKEN_TILE
    tiles_s = dec_seq // TOKEN_TILE

    cond8 = jnp.zeros((SUBLANES, D_MODEL), F32).at[0].set(c_ctx).at[1:1 + dec_batch].set(c)
    mod0_p, mod0_s = _split_mods(_ada(cond8, w_ada_l0, b_ada_l0))
    mod1_p, mod1_s = _split_mods(_ada(cond8, w_ada_l1, b_ada_l1))

    w0 = _prep_l0(w_in_l0, q_lora_norm_l0, kv_lora_norm_l0, w_q_up_l0, w_kv_up_l0, mla_q_norm_l0, mla_k_norm_l0,
                  w_pool_l0, pool_scale_l0, w_out_l0, norm_mix_l0, norm_mlp_l0, w_mlp1_l0, w_mlp2_l0)
    a_p, q_p, k_p, v_p, ckv_p, kr_p = _l0_pre(xp, mod0_p, tiles_p, False, w0, None)
    a_s, q_s, k_s, v_s, _, _ = _l0_pre(xs, mod0_s, tiles_s, True, w0, _rope_tables(dec_seq))
    kr_cache = jnp.pad(cache_l0_mla_krope.reshape(dec_batch * past, MLA_ROPE), ((0, 0), (MLA_NOPE, LANES - MLA_QK)))
    k_c, v_c = _cache_kv(cache_l0_mla_ckv.reshape(dec_batch * past, KV_LORA), kr_cache, w0)
    mla_bound = _normed_len(mla_q_norm_l0 * (MLA_QK ** -0.5 * LOG2E), MLA_QK) * _normed_len(mla_k_norm_l0, MLA_QK)
    attn_p = _ctx_attn(_mla_ctx_kernel, q_p, k_p, v_p, mla_bound, seq, MLA_HEADS * MLA_V, "l0_mla_ctx_attn")
    attn_s = _mla_lat_attn(q_s, k_s, v_s, k_c, v_c, jnp.full((dec_batch, MLA_HEADS), mla_bound, F32), dec_seq, past)
    xp = _post(xp, [_pool(a_p, seq, w0), attn_p], mod0_p, tiles_p, w0, "l0_post_ctx")
    xs = _post(xs, [_pool(a_s, dec_seq, w0), attn_s], mod0_s, tiles_s, w0, "l0_post_latent")

    w1 = dict(norm_mix=norm_mix_l1.reshape(1, -1), norm_mlp=norm_mlp_l1.reshape(1, -1),
              w_in=w_in_l1.astype(BF16),
              gq=jnp.tile(na_q_norm_l1 * (NA_HEAD_DIM ** -0.5 * LOG2E), 2).reshape(1, LANES),
              gk=jnp.tile(na_k_norm_l1, 2).reshape(1, LANES),
              w_out=w_out_l1.astype(BF16), w_mlp1=w_mlp1_l1.astype(BF16), w_mlp2=w_mlp2_l1.astype(BF16))
    q1_p, k1_p, v1_p, k1_new, v1_new = _l1_pre(xp, mod1_p, tiles_p, True, w1)
    q1_s, k1_s, v1_s = _l1_pre(xs, mod1_s, tiles_s, False, w1)
    q_len, k_len = _normed_len(w1["gq"], NA_HEAD_DIM), _normed_len(w1["gk"], NA_HEAD_DIM)
    na_p = _ctx_attn(_na_ctx_kernel, q1_p, k1_p, v1_p, q_len * k_len, seq, NA_SLABS * LANES, "l1_na_ctx_attn")
    cache_len = jnp.sqrt(jnp.max(jnp.sum(jnp.square(cache_l1_na_k.astype(F32)), axis=-1), axis=1))
    na_bound = (q_len * jnp.maximum(k_len, cache_len)
                + jnp.max(jnp.abs(rel_bias_l1.astype(F32)) * LOG2E, axis=(1, 2))[None, :])
    na_s = _na_lat_attn(q1_s, k1_s, v1_s,
                        _to_slabs(cache_l1_na_k, dec_batch * past), _to_slabs(cache_l1_na_v, dec_batch * past),
                        _na_bias_table(rel_bias_l1, dec_seq // GRID_W), na_bound, dec_seq, past)
    xp = _post(xp, [na_p], mod1_p, tiles_p, w1, "l1_post_ctx")
    xs = _post(xs, [na_s], mod1_s, tiles_s, w1, "l1_post_latent")

    return (xp.reshape(batch, seq, D_MODEL), xs.reshape(dec_batch, dec_seq, D_MODEL),
            ckv_p.reshape(batch, seq, KV_LORA), kr_p.reshape(batch, seq, MLA_ROPE),
            k1_new.reshape(batch, seq, NA_HEADS, NA_HEAD_DIM), v1_new.reshape(batch, seq, NA_HEADS, NA_HEAD_DIM))
```

```python
import functools

import jax
import jax.numpy as jnp
from jax import lax
from jax.experimental import pallas as pl
from jax.experimental.pallas import tpu as pltpu

F32 = jnp.float32
BF16 = jnp.bfloat16

D_MODEL = 1024
D_FF = 4 * D_MODEL
N_ADA = 6
EPS = 1e-6
NEG_INF = -1e30
ROPE_THETA = 10000.0
GRID_W = 64
POOL_WIDTH = 512
POOL_WINDOWS = (2, 4, 8, 16)
POOL_GROUP = 128
POOL_HALO = 8
MLA_HEADS = 8
MLA_NOPE = 64
MLA_ROPE = 32
MLA_QK = MLA_NOPE + MLA_ROPE
MLA_V = 64
Q_LORA = 256
KV_LORA = 128
NA_HEADS = 16
NA_HEAD_DIM = 64
NA_KH = 8
NA_KW = 16
LOG2E = 1.4426950408889634

LANES = 128
SUBLANES = 8
VMEM_LIMIT = 56 * 1024 * 1024

TOKEN_TILE = 512
MLA_Q_TILE = 512
MLA_STEP_HEADS = 2
MLA_KEY_CHUNK = 2048
MLA_ONLINE_CHUNK = 1024
SCORE_LOOKAHEAD = 2
MAX_SCORE_BOUND = 40.0
FF_CHUNK = 1024
NA_SLABS = NA_HEADS * NA_HEAD_DIM // LANES
NA_TILE_ROWS = 4
NA_WIN_ROWS = NA_TILE_ROWS + NA_KH - 1


def _params(n_axes):
    return pltpu.CompilerParams(dimension_semantics=("arbitrary",) * n_axes,
                                vmem_limit_bytes=VMEM_LIMIT)


def _const_spec(shape):
    zeros = (0,) * len(shape)
    return pl.BlockSpec(shape, lambda *_: zeros, pipeline_mode=pl.Buffered(1))


def _dot(a, b):
    return jnp.dot(a, b, preferred_element_type=F32)


def _dot_nt(a, b):
    return lax.dot_general(a, b, (((1,), (1,)), ((), ())), preferred_element_type=F32)


def _dot_tn(a, b):
    return lax.dot_general(a, b, (((0,), (0,)), ((), ())), preferred_element_type=F32)


def _rms(x, w, n=None):
    n = x.shape[-1] if n is None else n
    ss = jnp.sum(x * x, axis=-1, keepdims=True)
    return x * lax.rsqrt(ss * (1.0 / n) + EPS) * w


def _modulate(x, nw, shift, scale):
    return _rms(x, nw) * (1.0 + scale) + shift


def _ada_kernel(cond_ref, w_ref, b_ref, o_ref):
    c = cond_ref[...]
    s = (c * jax.nn.sigmoid(c)).astype(BF16)
    o_ref[...] = _dot(s, w_ref[...].astype(BF16)) + b_ref[...]


def _ada(cond8, w_ada, b_ada):
    n_out = w_ada.shape[1]
    tn = D_MODEL
    return pl.pallas_call(
        _ada_kernel,
        out_shape=jax.ShapeDtypeStruct((SUBLANES, n_out), F32),
        grid=(n_out // tn,),
        in_specs=[pl.BlockSpec((SUBLANES, D_MODEL), lambda j: (0, 0)),
                  pl.BlockSpec((D_MODEL, tn), lambda j: (0, j)),
                  pl.BlockSpec((1, tn), lambda j: (0, j))],
        out_specs=pl.BlockSpec((SUBLANES, tn), lambda j: (0, j)),
        compiler_params=_params(1),
        name="ada",
    )(cond8, w_ada, b_ada.reshape(1, n_out))


def _rope(x, c, s):
    return x * c + pltpu.roll(x, LANES - MLA_ROPE, 1) * s


def _head_rinv(raw, e2_ref):
    sq = (raw * raw).astype(BF16)
    pair = 2 * LANES
    ss = jnp.concatenate([_dot(sq[:, p:p + pair], e2_ref[...]) for p in range(0, raw.shape[1], pair)], axis=1)
    return lax.rsqrt(ss * (1.0 / MLA_QK) + EPS)


def _mla_keys_values(ckv, kr_slab, wk_ref, gk, wv_ref, e2_ref, rope_tabs, k_ref, v_ref):
    c16 = ckv.astype(BF16)
    kn = _dot(c16, wk_ref[...])
    v = _dot(c16, wv_ref[...])
    rinv = _head_rinv(kn + jnp.tile(kr_slab, (1, MLA_HEADS)), e2_ref)
    tail = kr_slab * gk
    if rope_tabs is not None:
        tail = _rope(tail, *rope_tabs)
    ones_lane = (lax.broadcasted_iota(jnp.int32, (1, LANES), 1) == MLA_V).astype(F32)
    for h in range(MLA_HEADS):
        hs = slice(h * LANES, (h + 1) * LANES)
        k_ref[h] = (rinv[:, hs] * (kn[:, hs] * gk + tail)).astype(BF16)
        v_ref[h] = (v[:, hs] + ones_lane).astype(BF16)


def _l0_pre_kernel(latent, x_ref, mod_ref, nw_ref, win_ref, qln_ref, kvn_ref, wq_ref, gq_ref,
                   wk_ref, gk_ref, wv_ref, e2_ref, *rest):
    if latent:
        rope_ref, a_ref, q_ref, k_ref, v_ref, ckv_ref, kr_ref = rest
        tile_rows = x_ref.shape[0] // GRID_W
        row0 = (pl.program_id(0) * tile_rows) % rope_ref.shape[1]
        expand = lambda col, row: jnp.concatenate(
            [rope_ref[col] + rope_ref[row, pl.ds(row0 + r, 1), :] for r in range(tile_rows)], axis=0)
        rope_tabs = (expand(0, 1), expand(2, 3))
    else:
        a_ref, q_ref, k_ref, v_ref, ckv_ref, kr_ref = rest
        rope_tabs = None
    h = _modulate(x_ref[...], nw_ref[...], mod_ref[0:1, :], mod_ref[1:2, :]).astype(BF16)
    proj = _dot(h, win_ref[...])
    a_ref[...] = proj[:, :POOL_WIDTH]
    q_lat = proj[:, POOL_WIDTH:POOL_WIDTH + Q_LORA]
    kv_lat = proj[:, POOL_WIDTH + Q_LORA:POOL_WIDTH + Q_LORA + KV_LORA]
    kr_slab = proj[:, POOL_WIDTH + Q_LORA + KV_LORA:]
    q = _dot(_rms(q_lat, qln_ref[...]).astype(BF16), wq_ref[...])
    q_rinv = _head_rinv(q, e2_ref)
    gq = gq_ref[...]
    for hd in range(MLA_HEADS):
        hs = slice(hd * LANES, (hd + 1) * LANES)
        y = q[:, hs] * q_rinv[:, hs] * gq
        if rope_tabs is not None:
            y = _rope(y, *rope_tabs)
        q_ref[hd] = y.astype(BF16)
    ckv = _rms(kv_lat, kvn_ref[...])
    ckv_ref[...] = ckv
    kr_ref[...] = kr_slab[:, MLA_NOPE:MLA_QK]
    _mla_keys_values(ckv, kr_slab, wk_ref, gk_ref[...], wv_ref, e2_ref, rope_tabs, k_ref, v_ref)


def _slab_spec(slabs, tm):
    return pl.BlockSpec((slabs, tm, LANES), lambda i: (0, i, 0))


def _l0_pre(x, mod, tiles_per_group, latent, w, rope_tabs):
    n = x.shape[0]
    tm = TOKEN_TILE
    row = lambda i: (i, 0)
    hk = MLA_HEADS * LANES
    in_specs = [pl.BlockSpec((tm, D_MODEL), row),
                pl.BlockSpec((None, SUBLANES, D_MODEL), lambda i: (i // tiles_per_group, 0, 0)),
                _const_spec((1, D_MODEL)),
                _const_spec((D_MODEL, D_MODEL)),
                _const_spec((1, Q_LORA)),
                _const_spec((1, KV_LORA)),
                _const_spec((Q_LORA, hk)),
                _const_spec((1, LANES)),
                _const_spec((KV_LORA, hk)),
                _const_spec((1, LANES)),
                _const_spec((KV_LORA, hk)),
                _const_spec((2 * LANES, 2 * LANES))]
    gains = "rope" if latent else "plain"
    args = [x, mod, w["norm_mix"], w["w_in"], w["qln"], w["kvn"], w["wq"], w["gq_" + gains], w["wk"],
            w["gk_" + gains], w["wv"], w["e2"]]
    if latent:
        assert tm % GRID_W == 0
        in_specs.append(_const_spec(rope_tabs.shape))
        args.append(rope_tabs)
    slab = jax.ShapeDtypeStruct((MLA_HEADS, n, LANES), BF16)
    out_shape = (jax.ShapeDtypeStruct((n, POOL_WIDTH), F32), slab, slab, slab,
                 jax.ShapeDtypeStruct((n, KV_LORA), F32),
                 jax.ShapeDtypeStruct((n, MLA_ROPE), F32))
    out_specs = (pl.BlockSpec((tm, POOL_WIDTH), row),
                 _slab_spec(MLA_HEADS, tm), _slab_spec(MLA_HEADS, tm), _slab_spec(MLA_HEADS, tm),
                 pl.BlockSpec((tm, KV_LORA), row), pl.BlockSpec((tm, MLA_ROPE), row))
    return pl.pallas_call(
        functools.partial(_l0_pre_kernel, latent),
        out_shape=out_shape, grid=(n // tm,), in_specs=in_specs, out_specs=out_specs,
        compiler_params=_params(1), name="l0_pre_latent" if latent else "l0_pre_ctx",
    )(*args)


def _cache_kv_kernel(ckv_ref, kr_ref, wk_ref, gk_ref, wv_ref, e2_ref, k_ref, v_ref):
    _mla_keys_values(ckv_ref[...], kr_ref[...], wk_ref, gk_ref[...], wv_ref, e2_ref, None, k_ref, v_ref)


def _cache_kv(ckv, kr_slab, w):
    n = ckv.shape[0]
    tm = TOKEN_TILE
    row = lambda i: (i, 0)
    hk = MLA_HEADS * LANES
    slab = jax.ShapeDtypeStruct((MLA_HEADS, n, LANES), BF16)
    return pl.pallas_call(
        _cache_kv_kernel,
        out_shape=(slab, slab),
        grid=(n // tm,),
        in_specs=[pl.BlockSpec((tm, KV_LORA), row), pl.BlockSpec((tm, LANES), row),
                  _const_spec((KV_LORA, hk)), _const_spec((1, LANES)), _const_spec((KV_LORA, hk)),
                  _const_spec((2 * LANES, 2 * LANES))],
        out_specs=(_slab_spec(MLA_HEADS, tm), _slab_spec(MLA_HEADS, tm)),
        compiler_params=_params(1), name="l0_cache_kv",
    )(ckv, kr_slab, w["wk"], w["gk_plain"], w["wv"], w["e2"])


def _pool_kernel(seq_len, prev_ref, u_ref, next_ref, wp_ref, ps_ref, o_ref, buf_ref):
    tm = u_ref.shape[0]
    pos0 = (pl.program_id(0) * tm) % seq_len
    buf_ref[0:POOL_HALO, :] = jnp.where(pos0 != 0, prev_ref[...], 0.0)
    buf_ref[POOL_HALO:POOL_HALO + tm, :] = u_ref[...]
    buf_ref[POOL_HALO + tm:, :] = jnp.where(pos0 + tm != seq_len, next_ref[...], 0.0)
    pos = pos0 + lax.broadcasted_iota(jnp.int32, (tm, 1), 0)
    for g, win in enumerate(POOL_WINDOWS):
        lanes = pl.ds(g * POOL_GROUP, POOL_GROUP)
        lo = pos - win // 2
        cnt = (jnp.minimum(lo + win, seq_len) - jnp.maximum(lo, 0)).astype(F32)
        acc = buf_ref[pl.ds(POOL_HALO - win // 2, tm), lanes]
        for d in range(1 - win // 2, win - win // 2):
            acc = acc + buf_ref[pl.ds(POOL_HALO + d, tm), lanes]
        pooled = acc / cnt - u_ref[:, lanes]
        o_ref[:, lanes] = (_dot(pooled.astype(BF16), wp_ref[g]) * ps_ref[:, lanes]).astype(BF16)


def _pool(a_in, seq_len, w):
    n = a_in.shape[0]
    tm = min(TOKEN_TILE, seq_len)
    hb = tm // POOL_HALO
    last = n // POOL_HALO - 1
    return pl.pallas_call(
        functools.partial(_pool_kernel, seq_len),
        out_shape=jax.ShapeDtypeStruct((n, POOL_WIDTH), BF16),
        grid=(n // tm,),
        in_specs=[pl.BlockSpec((POOL_HALO, POOL_WIDTH), lambda i: (jnp.maximum(i * hb - 1, 0), 0)),
                  pl.BlockSpec((tm, POOL_WIDTH), lambda i: (i, 0)),
                  pl.BlockSpec((POOL_HALO, POOL_WIDTH), lambda i: (jnp.minimum((i + 1) * hb, last), 0)),
                  _const_spec((len(POOL_WINDOWS), POOL_GROUP, POOL_GROUP)),
                  _const_spec((1, POOL_WIDTH))],
        out_specs=pl.BlockSpec((tm, POOL_WIDTH), lambda i: (i, 0)),
        scratch_shapes=[pltpu.VMEM((tm + 2 * POOL_HALO, POOL_WIDTH), F32)],
        compiler_params=_params(1), name="l0_pool",
    )(a_in, a_in, a_in, w["w_pool"], w["pool_scale"])


def _attend(streams, running_max):
    items = [(si, ch) for si, (_, chunks, _) in enumerate(streams) for ch in chunks]

    def score(item):
        si, (k_fn, _, extra_fn) = item
        s = _dot_nt(k_fn(), streams[si][0])
        return s if extra_fn is None else s + extra_fn()

    state = [None] * len(streams)
    pending = [score(item) for item in items[:SCORE_LOOKAHEAD]]
    for i, (si, (_, v_fn, _)) in enumerate(items):
        s = pending.pop(0)
        if i + SCORE_LOOKAHEAD < len(items):
            pending.append(score(items[i + SCORE_LOOKAHEAD]))
        want_den = streams[si][2]
        m_new, alpha = None, 1.0
        if state[si] is not None:
            m, acc, den = state[si]
        if running_max:
            m_new = jnp.max(s, axis=0, keepdims=True)
            if state[si] is not None:
                m_new = jnp.maximum(m, m_new)
                alpha = jnp.exp2(m - m_new)
            s = s - m_new
        p = jnp.exp2(s)
        pv = _dot_tn(v_fn(), p.astype(BF16))
        psum = jnp.sum(p, axis=0, keepdims=True) if want_den else None
        if state[si] is None:
            state[si] = (m_new, pv, psum)
        elif running_max:
            state[si] = (m_new, acc * alpha + pv, den * alpha + psum if want_den else None)
        else:
            state[si] = (None, acc + pv, den + psum if want_den else None)
    return [(acc, den) for _, acc, den in state]


def _ref_chunks(k_ref, v_ref, slab, chunk, extra_fn=None):
    nk = k_ref.shape[1]
    out = []
    for c in range(0, nk, chunk):
        rows = pl.ds(c, min(chunk, nk - c))
        out.append((lambda rows=rows: k_ref[slab, rows, :], lambda rows=rows: v_ref[slab, rows, :], extra_fn))
    return out


def _mla_finish(o_t):
    return o_t[:MLA_V] / o_t[MLA_V:MLA_V + 1]


def _na_queries(q2):
    low = lax.broadcasted_iota(jnp.int32, (1, LANES), 1) < NA_HEAD_DIM
    zero = jnp.zeros_like(q2)
    return jnp.concatenate([jnp.where(low, q2, zero), jnp.where(low, zero, q2)], axis=0)


def _na_finish(o_t, den):
    nq = o_t.shape[1] // 2
    o = o_t / den
    return jnp.concatenate([o[:NA_HEAD_DIM, :nq], o[NA_HEAD_DIM:, nq:]], axis=0)


def _attend_guarded(bound, streams_fn):
    return lax.cond(bound <= MAX_SCORE_BOUND,
                    lambda: _attend(streams_fn(False), False),
                    lambda: _attend(streams_fn(True), True))


def _mla_ctx_kernel(bound_ref, q_ref, k_ref, v_ref, o_ref):
    seq = k_ref.shape[1]
    res = _attend_guarded(bound_ref[0, 0], lambda _: [(q_ref[h], _ref_chunks(k_ref, v_ref, h, seq), False)
                                                      for h in range(MLA_HEADS)])
    for hp in range(MLA_HEADS // 2):
        o_t = jnp.concatenate([_mla_finish(res[h][0]) for h in (2 * hp, 2 * hp + 1)], axis=0)
        o_ref[:, hp * LANES:(hp + 1) * LANES] = o_t.T.astype(BF16)


def _na_ctx_kernel(bound_ref, q_ref, k_ref, v_ref, o_ref):
    seq = k_ref.shape[1]
    res = _attend_guarded(bound_ref[0, 0], lambda _: [(_na_queries(q_ref[s]), _ref_chunks(k_ref, v_ref, s, seq), True)
                                                      for s in range(NA_SLABS)])
    for s in range(NA_SLABS):
        o_ref[:, s * LANES:(s + 1) * LANES] = _na_finish(*res[s]).T.astype(BF16)


def _ctx_attn(body, q, k, v, bound, seq, width, name):
    slabs, n, _ = q.shape
    spec = pl.BlockSpec((slabs, seq, LANES), lambda b: (0, b, 0))
    return pl.pallas_call(
        body,
        out_shape=jax.ShapeDtypeStruct((n, width), BF16),
        grid=(n // seq,),
        in_specs=[pl.BlockSpec(memory_space=pltpu.SMEM), spec, spec, spec],
        out_specs=pl.BlockSpec((seq, width), lambda b: (b, 0)),
        compiler_params=_params(1), name=name,
    )(bound.reshape(1, 1), q, k, v)


def _mla_lat_kernel(bound_ref, q_ref, kl_ref, vl_ref, kc_ref, vc_ref, o_ref):
    batch, head0 = pl.program_id(0), pl.program_id(1) * MLA_STEP_HEADS
    heads = range(MLA_STEP_HEADS)
    bounds = [bound_ref[batch, head0 + h] for h in heads]

    def streams(running_max):
        size = MLA_ONLINE_CHUNK if running_max else MLA_KEY_CHUNK
        return [(q_ref[h], _ref_chunks(kl_ref, vl_ref, h, size) + _ref_chunks(kc_ref, vc_ref, h, size), False)
                for h in heads]

    res = _attend_guarded(functools.reduce(jnp.maximum, bounds), streams)
    o_ref[...] = jnp.concatenate([_mla_finish(o) for o, _ in res], axis=0).T.astype(BF16)


def _normed_len(gain, dim):
    return (dim ** 0.5) * jnp.max(jnp.abs(gain.astype(F32)))


def _mla_lat_attn(q, k, v, kc, vc, bound, seq, past):
    n = q.shape[1]
    tq = MLA_Q_TILE
    nq = seq // tq
    hs = MLA_STEP_HEADS
    return pl.pallas_call(
        _mla_lat_kernel,
        out_shape=jax.ShapeDtypeStruct((n, MLA_HEADS * MLA_V), BF16),
        grid=(n // seq, MLA_HEADS // hs, nq),
        in_specs=[pl.BlockSpec(memory_space=pltpu.SMEM),
                  pl.BlockSpec((hs, tq, LANES), lambda b, h, i: (h, b * nq + i, 0)),
                  pl.BlockSpec((hs, seq, LANES), lambda b, h, i: (h, b, 0)),
                  pl.BlockSpec((hs, seq, LANES), lambda b, h, i: (h, b, 0)),
                  pl.BlockSpec((hs, past, LANES), lambda b, h, i: (h, b, 0)),
                  pl.BlockSpec((hs, past, LANES), lambda b, h, i: (h, b, 0))],
        out_specs=pl.BlockSpec((tq, hs * MLA_V), lambda b, h, i: (b * nq + i, h)),
        compiler_params=_params(3), name="l0_mla_latent_attn",
    )(bound, q, k, v, kc, vc)


def _na_first_row(r, rows):
    return jnp.clip(r - NA_KH // 2, 0, rows - NA_KH)


def _na_lat_kernel(rows, bound_ref, q_ref, k_ref, v_ref, kc_ref, vc_ref, t2_ref, o_ref):
    nq = NA_TILE_ROWS * GRID_W
    nk = NA_WIN_ROWS * GRID_W
    n_off = 2 * NA_KH - 1
    r0 = pl.program_id(1) * NA_TILE_ROWS
    w0 = jnp.clip(r0 - NA_KH // 2, 0, rows - NA_WIN_ROWS)

    def block_index(j, t):
        key_row = w0 + j
        i = r0 + 2 * t
        first_u, first_l = _na_first_row(i, rows), _na_first_row(i + 1, rows)
        in_u = (key_row >= first_u) & (key_row < first_u + NA_KH)
        in_l = (key_row >= first_l) & (key_row < first_l + NA_KH)
        d = jnp.clip(key_row - i + NA_KH - 1, 0, n_off - 1)
        return jnp.where(in_u & in_l, d, jnp.where(in_u, n_off, jnp.where(in_l, n_off + 1, n_off + 2)))

    idx = [[block_index(j, t) for t in range(NA_TILE_ROWS // 2)] for j in range(NA_WIN_ROWS)]

    def local_bias(s):
        heads = [jnp.concatenate([jnp.concatenate([t2_ref[2 * s + hh, i] for i in row], axis=1) for row in idx], axis=0)
                 for hh in range(2)]
        return jnp.concatenate(heads, axis=1)

    batch = pl.program_id(0)
    bounds = [bound_ref[batch, h] for h in range(NA_HEADS)]

    res = _attend_guarded(
        functools.reduce(jnp.maximum, bounds),
        lambda _: [(_na_queries(q_ref[s]),
                    _ref_chunks(k_ref, v_ref, s, nk, functools.partial(local_bias, s)) + _ref_chunks(kc_ref, vc_ref, s, nk),
                    True) for s in range(NA_SLABS)])
    for s in range(NA_SLABS):
        o_ref[:, s * LANES:(s + 1) * LANES] = _na_finish(*res[s]).T.astype(BF16)


def _na_lat_attn(q, k, v, kc, vc, t2, bound, seq, past):
    slabs, n, _ = q.shape
    rows = seq // GRID_W
    tiles = rows // NA_TILE_ROWS
    nq = NA_TILE_ROWS * GRID_W
    nk = NA_WIN_ROWS * GRID_W

    def win_map(b, t):
        w0 = jnp.clip(t * NA_TILE_ROWS - NA_KH // 2, 0, rows - NA_WIN_ROWS)
        return (0, pl.multiple_of(b * seq + w0 * GRID_W, GRID_W), 0)

    win_spec = pl.BlockSpec((pl.Element(slabs), pl.Element(nk), pl.Element(LANES)), win_map)
    ctx_spec = pl.BlockSpec((slabs, past, LANES), lambda b, t: (0, b, 0))
    return pl.pallas_call(
        functools.partial(_na_lat_kernel, rows),
        out_shape=jax.ShapeDtypeStruct((n, slabs * LANES), BF16),
        grid=(n // seq, tiles),
        in_specs=[pl.BlockSpec(memory_space=pltpu.SMEM),
                  pl.BlockSpec((slabs, nq, LANES), lambda b, t: (0, b * tiles + t, 0)),
                  win_spec, win_spec, ctx_spec, ctx_spec,
                  _const_spec(t2.shape)],
        out_specs=pl.BlockSpec((nq, slabs * LANES), lambda b, t: (b * tiles + t, 0)),
        compiler_params=_params(2), name="l1_na_latent_attn",
    )(bound, q, k, v, kc, vc, t2)


def _pair_norm(x2, g2):
    low = lax.broadcasted_iota(jnp.int32, (1, LANES), 1) < NA_HEAD_DIM
    sq = x2 * x2
    ss_all = jnp.sum(sq, axis=-1, keepdims=True)
    ss_low = jnp.sum(jnp.where(low, sq, 0.0), axis=-1, keepdims=True)
    ss = jnp.where(low, ss_low, ss_all - ss_low)
    return x2 * lax.rsqrt(ss * (1.0 / NA_HEAD_DIM) + EPS) * g2


def _l1_pre_kernel(keep_f32, x_ref, mod_ref, nw_ref, win_ref, gq_ref, gk_ref, q_ref, k_ref, v_ref, *f32_refs):
    h = _modulate(x_ref[...], nw_ref[...], mod_ref[0:1, :], mod_ref[1:2, :]).astype(BF16)
    width = NA_SLABS * LANES
    q = _dot(h, win_ref[:, 0:width])
    k = _dot(h, win_ref[:, width:2 * width])
    v = _dot(h, win_ref[:, 2 * width:3 * width])
    gq = gq_ref[...]
    gk = gk_ref[...]
    for s in range(NA_SLABS):
        ls = slice(s * LANES, (s + 1) * LANES)
        q_ref[s] = _pair_norm(q[:, ls], gq).astype(BF16)
        kn = _pair_norm(k[:, ls], gk)
        k_ref[s] = kn.astype(BF16)
        v_ref[s] = v[:, ls].astype(BF16)
        if keep_f32:
            f32_refs[0][:, ls] = kn
    if keep_f32:
        f32_refs[1][...] = v


def _l1_pre(x, mod, tiles_per_group, keep_f32, w):
    n = x.shape[0]
    tm = TOKEN_TILE
    row = lambda i: (i, 0)
    width = NA_SLABS * LANES
    slab = jax.ShapeDtypeStruct((NA_SLABS, n, LANES), BF16)
    out_shape = [slab, slab, slab]
    out_specs = [_slab_spec(NA_SLABS, tm)] * 3
    if keep_f32:
        out_shape += [jax.ShapeDtypeStruct((n, width), F32)] * 2
        out_specs += [pl.BlockSpec((tm, width), row)] * 2
    return pl.pallas_call(
        functools.partial(_l1_pre_kernel, keep_f32),
        out_shape=tuple(out_shape), grid=(n // tm,),
        in_specs=[pl.BlockSpec((tm, D_MODEL), row),
                  pl.BlockSpec((None, SUBLANES, D_MODEL), lambda i: (i // tiles_per_group, 0, 0)),
                  _const_spec((1, D_MODEL)),
                  _const_spec((D_MODEL, 3 * width)),
                  _const_spec((1, LANES)),
                  _const_spec((1, LANES))],
        out_specs=tuple(out_specs),
        compiler_params=_params(1), name="l1_pre_ctx" if keep_f32 else "l1_pre_latent",
    )(x, mod, w["norm_mix"], w["w_in"], w["gq"], w["gk"])


def _post_kernel(n_parts, x_ref, *rest):
    parts = rest[:n_parts]
    mod_ref, nw_ref, wo_ref, w1_ref, w2_ref, o_ref = rest[n_parts:]
    mix_in = parts[0][...] if n_parts == 1 else jnp.concatenate([p[...] for p in parts], axis=-1)
    x1 = x_ref[...] + mod_ref[2:3, :] * _dot(mix_in, wo_ref[...])
    h = _modulate(x1, nw_ref[...], mod_ref[3:4, :], mod_ref[4:5, :]).astype(BF16)
    acc = jnp.zeros(x1.shape, F32)
    for c in range(D_FF // FF_CHUNK):
        cs = pl.ds(c * FF_CHUNK, FF_CHUNK)
        u = jnp.square(jnp.maximum(_dot(h, w1_ref[:, cs]), 0.0)).astype(BF16)
        acc = acc + _dot(u, w2_ref[cs, :])
    o_ref[...] = x1 + mod_ref[5:6, :] * acc


def _post(x, parts, mod, tiles_per_group, w, name):
    n = x.shape[0]
    tm = TOKEN_TILE
    row = lambda i: (i, 0)
    in_specs = [pl.BlockSpec((tm, D_MODEL), row)]
    in_specs += [pl.BlockSpec((tm, p.shape[1]), row) for p in parts]
    in_specs += [pl.BlockSpec((None, SUBLANES, D_MODEL), lambda i: (i // tiles_per_group, 0, 0)),
                 _const_spec((1, D_MODEL)),
                 _const_spec((D_MODEL, D_MODEL)),
                 _const_spec((D_MODEL, D_FF)),
                 _const_spec((D_FF, D_MODEL))]
    return pl.pallas_call(
        functools.partial(_post_kernel, len(parts)),
        out_shape=jax.ShapeDtypeStruct((n, D_MODEL), F32), grid=(n // tm,),
        in_specs=in_specs, out_specs=pl.BlockSpec((tm, D_MODEL), row),
        compiler_params=_params(1), name=name,
    )(x, *parts, mod, w["norm_mlp"], w["w_out"], w["w_mlp1"], w["w_mlp2"])


def _rope_tables(seq):
    assert seq == GRID_W * GRID_W
    half = MLA_ROPE // 2
    inv_freq = jnp.power(ROPE_THETA, -jnp.arange(0, half, 2, dtype=F32) / half)
    ang = jnp.arange(GRID_W, dtype=F32)[:, None] * inv_freq[None, :]
    cos, sin = jnp.cos(ang), jnp.sin(ang)
    fill = lambda n, v=0.0: jnp.full((GRID_W, n), v, F32)
    spare = LANES - MLA_QK
    return jnp.stack([
        jnp.concatenate([fill(MLA_NOPE, 1.0), fill(half), cos, cos, fill(spare)], axis=1),
        jnp.concatenate([fill(MLA_NOPE), cos, cos, fill(half), fill(spare)], axis=1),
        jnp.concatenate([fill(MLA_NOPE), fill(half), -sin, sin, fill(spare)], axis=1),
        jnp.concatenate([fill(MLA_NOPE), -sin, sin, fill(half), fill(spare)], axis=1)])


def _rope_partner(tail):
    shape = tail.shape
    t = tail.reshape(shape[:-1] + (2, 2, MLA_ROPE // 4))
    return t[..., ::-1, :].reshape(shape)


def _na_bias_table(rel_bias, rows):
    qcol = jnp.arange(GRID_W)
    kcol = jnp.arange(GRID_W)
    col_start = jnp.clip(qcol - NA_KW // 2, 0, GRID_W - NA_KW)
    col_mask = (kcol[:, None] >= col_start[None, :]) & (kcol[:, None] < col_start[None, :] + NA_KW)
    offs = jnp.arange(2 * NA_KW - 1)
    onehot = (kcol[None, :, None] - qcol[None, None, :] + NA_KW - 1 == offs[:, None, None]).astype(F32)
    blocks = jnp.einsum('hdj,jkq->hdkq', rel_bias.astype(F32) * LOG2E, onehot, precision=lax.Precision.HIGHEST)
    blocks = jnp.where(col_mask[None, None], blocks, NEG_INF)
    neg = jnp.full_like(blocks[:, :1], NEG_INF)
    both = jnp.concatenate([blocks, jnp.concatenate([neg, blocks[:, :-1]], axis=1)], axis=-1)
    d_upper, d_lower = _na_partial_offsets(rows)
    upper_only = jnp.concatenate([blocks[:, d_upper:d_upper + 1], neg], axis=-1)
    lower_only = jnp.concatenate([neg, blocks[:, d_lower - 1:d_lower]], axis=-1)
    return jnp.concatenate([both, upper_only, lower_only, jnp.concatenate([neg, neg], axis=-1)], axis=1)


def _na_partial_offsets(rows):
    first = lambda r: min(max(r - NA_KH // 2, 0), rows - NA_KH)
    upper, lower = set(), set()
    for r0 in range(0, rows, NA_TILE_ROWS):
        w0 = min(max(r0 - NA_KH // 2, 0), rows - NA_WIN_ROWS)
        for j in range(NA_WIN_ROWS):
            for t in range(NA_TILE_ROWS // 2):
                key_row, i = w0 + j, r0 + 2 * t
                in_u = first(i) <= key_row < first(i) + NA_KH
                in_l = first(i + 1) <= key_row < first(i + 1) + NA_KH
                d = key_row - i + NA_KH - 1
                if in_u and in_l:
                    assert 1 <= d <= 2 * NA_KH - 2
                elif in_u:
                    upper.add(d)
                elif in_l:
                    lower.add(d)
    assert len(upper) == 1 and len(lower) == 1, (upper, lower)
    return upper.pop(), lower.pop()


def _prep_l0(w_in, q_lora_norm, kv_lora_norm, w_q_up, w_kv_up, mla_q_norm, mla_k_norm, w_pool, pool_scale,
             w_out, norm_mix, norm_mlp, w_mlp1, w_mlp2):
    lat_end = POOL_WIDTH + Q_LORA + KV_LORA
    w_in_pad = jnp.concatenate([w_in[:, :lat_end], jnp.zeros((D_MODEL, MLA_NOPE), w_in.dtype),
                                w_in[:, lat_end:], _rope_partner(w_in[:, lat_end:])], axis=1)
    pad = LANES - MLA_QK
    hk = MLA_HEADS * LANES
    wq = w_q_up.reshape(Q_LORA, MLA_HEADS, MLA_QK)
    wq = jnp.concatenate([wq, _rope_partner(wq[:, :, MLA_NOPE:])], axis=-1)
    kv = w_kv_up.reshape(KV_LORA, MLA_HEADS, MLA_NOPE + MLA_V)
    wk = jnp.pad(kv[:, :, :MLA_NOPE], ((0, 0), (0, 0), (0, LANES - MLA_NOPE)))
    wv = jnp.pad(kv[:, :, MLA_NOPE:], ((0, 0), (0, 0), (0, LANES - MLA_V)))
    gq = mla_q_norm * (MLA_QK ** -0.5 * LOG2E)
    slab_lane = jnp.arange(2 * LANES)
    e2 = (slab_lane[:, None] // LANES == slab_lane[None, :] // LANES) & (slab_lane[:, None] % LANES < MLA_QK)
    return dict(
        norm_mix=norm_mix.reshape(1, -1), norm_mlp=norm_mlp.reshape(1, -1),
        w_in=w_in_pad.astype(BF16),
        qln=q_lora_norm.reshape(1, -1), kvn=kv_lora_norm.reshape(1, -1),
        wq=wq.reshape(Q_LORA, hk).astype(BF16),
        gq_plain=jnp.pad(gq, (0, pad)).reshape(1, LANES),
        gq_rope=jnp.concatenate([gq, _rope_partner(gq[MLA_NOPE:])]).reshape(1, LANES),
        wk=wk.reshape(KV_LORA, hk).astype(BF16),
        gk_plain=jnp.pad(mla_k_norm, (0, pad)).reshape(1, LANES),
        gk_rope=jnp.concatenate([mla_k_norm, _rope_partner(mla_k_norm[MLA_NOPE:])]).reshape(1, LANES),
        wv=wv.reshape(KV_LORA, hk).astype(BF16),
        e2=e2.astype(BF16),
        w_pool=w_pool.astype(BF16), pool_scale=pool_scale.reshape(1, -1),
        w_out=w_out.astype(BF16), w_mlp1=w_mlp1.astype(BF16), w_mlp2=w_mlp2.astype(BF16))


def _split_mods(m):
    mods = m.reshape(SUBLANES, N_ADA, D_MODEL)[:3]
    mods = jnp.pad(mods, ((0, 0), (0, SUBLANES - N_ADA), (0, 0)))
    return mods[0:1], mods[1:3]


def _to_slabs(x, n):
    return x.reshape(n, -1, LANES).transpose(1, 0, 2).astype(BF16)


def kernel(x_prompt, x_sample, cache_l0_mla_ckv, cache_l0_mla_krope, cache_l1_na_k, cache_l1_na_v, c, c_ctx, w_ada_l0, b_ada_l0, norm_mix_l0, norm_mlp_l0, w_mlp1_l0, w_mlp2_l0, w_in_l0, q_lora_norm_l0, kv_lora_norm_l0, w_q_up_l0, w_kv_up_l0, mla_q_norm_l0, mla_k_norm_l0, w_pool_l0, pool_scale_l0, w_out_l0, w_ada_l1, b_ada_l1, norm_mix_l1, norm_mlp_l1, w_mlp1_l1, w_mlp2_l1, w_in_l1, na_q_norm_l1, na_k_norm_l1, rel_bias_l1, w_out_l1):
    batch, seq, _ = x_prompt.shape
    dec_batch, dec_seq, _ = x_sample.shape
    past = cache_l0_mla_ckv.shape[1]
    xp = x_prompt.reshape(batch * seq, D_MODEL)
    xs = x_sample.reshape(dec_batch * dec_seq, D_MODEL)
    tiles_p = (batch * seq) // TOKEN_TILE
    tiles_s = dec_seq // TOKEN_TILE

    cond8 = jnp.concatenate([c_ctx[None, :], c, jnp.zeros((SUBLANES - 1 - dec_batch, D_MODEL), F32)], axis=0)
    mod0_p, mod0_s = _split_mods(_ada(cond8, w_ada_l0, b_ada_l0))
    mod1_p, mod1_s = _split_mods(_ada(cond8, w_ada_l1, b_ada_l1))

    w0 = _prep_l0(w_in_l0, q_lora_norm_l0, kv_lora_norm_l0, w_q_up_l0, w_kv_up_l0, mla_q_norm_l0, mla_k_norm_l0,
                  w_pool_l0, pool_scale_l0, w_out_l0, norm_mix_l0, norm_mlp_l0, w_mlp1_l0, w_mlp2_l0)
    a_p, q_p, k_p, v_p, ckv_p, kr_p = _l0_pre(xp, mod0_p, tiles_p, False, w0, None)
    a_s, q_s, k_s, v_s, _, _ = _l0_pre(xs, mod0_s, tiles_s, True, w0, _rope_tables(dec_seq))
    kr_cache = jnp.pad(cache_l0_mla_krope.reshape(dec_batch * past, MLA_ROPE), ((0, 0), (MLA_NOPE, LANES - MLA_QK)))
    k_c, v_c = _cache_kv(cache_l0_mla_ckv.reshape(dec_batch * past, KV_LORA), kr_cache, w0)
    mla_bound = _normed_len(mla_q_norm_l0 * (MLA_QK ** -0.5 * LOG2E), MLA_QK) * _normed_len(mla_k_norm_l0, MLA_QK)
    attn_p = _ctx_attn(_mla_ctx_kernel, q_p, k_p, v_p, mla_bound, seq, MLA_HEADS * MLA_V, "l0_mla_ctx_attn")
    attn_s = _mla_lat_attn(q_s, k_s, v_s, k_c, v_c, jnp.full((dec_batch, MLA_HEADS), mla_bound, F32), dec_seq, past)
    xp = _post(xp, [_pool(a_p, seq, w0), attn_p], mod0_p, tiles_p, w0, "l0_post_ctx")
    xs = _post(xs, [_pool(a_s, dec_seq, w0), attn_s], mod0_s, tiles_s, w0, "l0_post_latent")

    w1 = dict(norm_mix=norm_mix_l1.reshape(1, -1), norm_mlp=norm_mlp_l1.reshape(1, -1),
              w_in=w_in_l1.astype(BF16),
              gq=jnp.tile(na_q_norm_l1 * (NA_HEAD_DIM ** -0.5 * LOG2E), 2).reshape(1, LANES),
              gk=jnp.tile(na_k_norm_l1, 2).reshape(1, LANES),
              w_out=w_out_l1.astype(BF16), w_mlp1=w_mlp1_l1.astype(BF16), w_mlp2=w_mlp2_l1.astype(BF16))
    q1_p, k1_p, v1_p, k1_new, v1_new = _l1_pre(xp, mod1_p, tiles_p, True, w1)
    q1_s, k1_s, v1_s = _l1_pre(xs, mod1_s, tiles_s, False, w1)
    q_len, k_len = _normed_len(w1["gq"], NA_HEAD_DIM), _normed_len(w1["gk"], NA_HEAD_DIM)
    na_p = _ctx_attn(_na_ctx_kernel, q1_p, k1_p, v1_p, q_len * k_len, seq, NA_SLABS * LANES, "l1_na_ctx_attn")
    cache_len = jnp.sqrt(jnp.max(jnp.sum(jnp.square(cache_l1_na_k.astype(F32)), axis=-1), axis=1))
    na_bound = (q_len * jnp.maximum(k_len, cache_len)
                + jnp.max(jnp.abs(rel_bias_l1.astype(F32)) * LOG2E, axis=(1, 2))[None, :])
    na_s = _na_lat_attn(q1_s, k1_s, v1_s,
                        _to_slabs(cache_l1_na_k, dec_batch * past), _to_slabs(cache_l1_na_v, dec_batch * past),
                        _na_bias_table(rel_bias_l1, dec_seq // GRID_W), na_bound, dec_seq, past)
    xp = _post(xp, [na_p], mod1_p, tiles_p, w1, "l1_post_ctx")
    xs = _post(xs, [na_s], mod1_s, tiles_s, w1, "l1_post_latent")

    return (xp.reshape(batch, seq, D_MODEL), xs.reshape(dec_batch, dec_seq, D_MODEL),
            ckv_p.reshape(batch, seq, KV_LORA), kr_p.reshape(batch, seq, MLA_ROPE),
            k1_new.reshape(batch, seq, NA_HEADS, NA_HEAD_DIM), v1_new.reshape(batch, seq, NA_HEADS, NA_HEAD_DIM))
```

```python
import functools

import jax
import jax.numpy as jnp
from jax import lax
from jax.experimental import pallas as pl
from jax.experimental.pallas import tpu as pltpu

F32 = jnp.float32
BF16 = jnp.bfloat16

D_MODEL = 1024
D_FF = 4 * D_MODEL
N_ADA = 6
EPS = 1e-6
NEG_INF = -1e30
ROPE_THETA = 10000.0
GRID_W = 64
POOL_WIDTH = 512
POOL_WINDOWS = (2, 4, 8, 16)
POOL_GROUP = 128
POOL_HALO = 8
MLA_HEADS = 8
MLA_NOPE = 64
MLA_ROPE = 32
MLA_QK = MLA_NOPE + MLA_ROPE
MLA_V = 64
Q_LORA = 256
KV_LORA = 128
NA_HEADS = 16
NA_HEAD_DIM = 64
NA_KH = 8
NA_KW = 16
LOG2E = 1.4426950408889634

LANES = 128
SUBLANES = 8
VMEM_LIMIT = 56 * 1024 * 1024

TOKEN_TILE = 512
MLA_Q_TILE = 512
MLA_STEP_HEADS = 2
MLA_KEY_CHUNK = 2048
MLA_ONLINE_CHUNK = 1024
SCORE_LOOKAHEAD = 2
MAX_SCORE_BOUND = 40.0
FF_CHUNK = 1024
NA_SLABS = NA_HEADS * NA_HEAD_DIM // LANES
NA_TILE_ROWS = 4
NA_WIN_ROWS = NA_TILE_ROWS + NA_KH - 1


def _params(n_axes):
    return pltpu.CompilerParams(dimension_semantics=("arbitrary",) * n_axes,
                                vmem_limit_bytes=VMEM_LIMIT)


def _const_spec(shape):
    zeros = (0,) * len(shape)
    return pl.BlockSpec(shape, lambda *_: zeros, pipeline_mode=pl.Buffered(1))


def _dot(a, b):
    return jnp.dot(a, b, preferred_element_type=F32)


def _dot_nt(a, b):
    return lax.dot_general(a, b, (((1,), (1,)), ((), ())), preferred_element_type=F32)


def _dot_tn(a, b):
    return lax.dot_general(a, b, (((0,), (0,)), ((), ())), preferred_element_type=F32)


def _rms(x, w, n=None):
    n = x.shape[-1] if n is None else n
    ss = jnp.sum(x * x, axis=-1, keepdims=True)
    return x * lax.rsqrt(ss * (1.0 / n) + EPS) * w


def _modulate(x, nw, shift, scale):
    return _rms(x, nw) * (1.0 + scale) + shift


def _ada_kernel(cond_ref, w_ref, b_ref, o_ref):
    c = cond_ref[...]
    s = (c * jax.nn.sigmoid(c)).astype(BF16)
    o_ref[...] = _dot(s, w_ref[...].astype(BF16)) + b_ref[...]


def _ada(cond8, w_ada, b_ada):
    n_out = w_ada.shape[1]
    tn = D_MODEL
    return pl.pallas_call(
        _ada_kernel,
        out_shape=jax.ShapeDtypeStruct((SUBLANES, n_out), F32),
        grid=(n_out // tn,),
        in_specs=[pl.BlockSpec((SUBLANES, D_MODEL), lambda j: (0, 0)),
                  pl.BlockSpec((D_MODEL, tn), lambda j: (0, j)),
                  pl.BlockSpec((1, tn), lambda j: (0, j))],
        out_specs=pl.BlockSpec((SUBLANES, tn), lambda j: (0, j)),
        compiler_params=_params(1),
        name="ada",
    )(cond8, w_ada, b_ada.reshape(1, n_out))


def _rope(x, c, s):
    return x * c + pltpu.roll(x, LANES - MLA_ROPE, 1) * s


def _head_rinv(raw, e2_ref):
    sq = (raw * raw).astype(BF16)
    pair = 2 * LANES
    ss = jnp.concatenate([_dot(sq[:, p:p + pair], e2_ref[...]) for p in range(0, raw.shape[1], pair)], axis=1)
    return lax.rsqrt(ss * (1.0 / MLA_QK) + EPS)


def _mla_keys_values(ckv, kr_slab, wk_ref, gk, wv_ref, e2_ref, rope_tabs, k_ref, v_ref):
    c16 = ckv.astype(BF16)
    kn = _dot(c16, wk_ref[...])
    v = _dot(c16, wv_ref[...])
    rinv = _head_rinv(kn + jnp.tile(kr_slab, (1, MLA_HEADS)), e2_ref)
    tail = kr_slab * gk
    if rope_tabs is not None:
        tail = _rope(tail, *rope_tabs)
    ones_lane = (lax.broadcasted_iota(jnp.int32, (1, LANES), 1) == MLA_V).astype(F32)
    for h in range(MLA_HEADS):
        hs = slice(h * LANES, (h + 1) * LANES)
        k_ref[h] = (rinv[:, hs] * (kn[:, hs] * gk + tail)).astype(BF16)
        v_ref[h] = (v[:, hs] + ones_lane).astype(BF16)


def _l0_pre_kernel(latent, x_ref, mod_ref, nw_ref, win_ref, qln_ref, kvn_ref, wq_ref, gq_ref,
                   wk_ref, gk_ref, wv_ref, e2_ref, *rest):
    if latent:
        rope_ref, a_ref, q_ref, k_ref, v_ref, ckv_ref, kr_ref = rest
        tile_rows = x_ref.shape[0] // GRID_W
        row0 = (pl.program_id(0) * tile_rows) % rope_ref.shape[1]
        expand = lambda col, row: jnp.concatenate(
            [rope_ref[col] + rope_ref[row, pl.ds(row0 + r, 1), :] for r in range(tile_rows)], axis=0)
        rope_tabs = (expand(0, 1), expand(2, 3))
    else:
        a_ref, q_ref, k_ref, v_ref, ckv_ref, kr_ref = rest
        rope_tabs = None
    h = _modulate(x_ref[...], nw_ref[...], mod_ref[0:1, :], mod_ref[1:2, :]).astype(BF16)
    proj = _dot(h, win_ref[...])
    a_ref[...] = proj[:, :POOL_WIDTH]
    q_lat = proj[:, POOL_WIDTH:POOL_WIDTH + Q_LORA]
    kv_lat = proj[:, POOL_WIDTH + Q_LORA:POOL_WIDTH + Q_LORA + KV_LORA]
    kr_slab = proj[:, POOL_WIDTH + Q_LORA + KV_LORA:]
    q = _dot(_rms(q_lat, qln_ref[...]).astype(BF16), wq_ref[...])
    q_rinv = _head_rinv(q, e2_ref)
    gq = gq_ref[...]
    for hd in range(MLA_HEADS):
        hs = slice(hd * LANES, (hd + 1) * LANES)
        y = q[:, hs] * q_rinv[:, hs] * gq
        if rope_tabs is not None:
            y = _rope(y, *rope_tabs)
        q_ref[hd] = y.astype(BF16)
    ckv = _rms(kv_lat, kvn_ref[...])
    ckv_ref[...] = ckv
    kr_ref[...] = kr_slab[:, MLA_NOPE:MLA_QK]
    _mla_keys_values(ckv, kr_slab, wk_ref, gk_ref[...], wv_ref, e2_ref, rope_tabs, k_ref, v_ref)


def _slab_spec(slabs, tm):
    return pl.BlockSpec((slabs, tm, LANES), lambda i: (0, i, 0))


def _l0_pre(x, mod, tiles_per_group, latent, w, rope_tabs):
    n = x.shape[0]
    tm = TOKEN_TILE
    row = lambda i: (i, 0)
    hk = MLA_HEADS * LANES
    in_specs = [pl.BlockSpec((tm, D_MODEL), row),
                pl.BlockSpec((None, SUBLANES, D_MODEL), lambda i: (i // tiles_per_group, 0, 0)),
                _const_spec((1, D_MODEL)),
                _const_spec((D_MODEL, D_MODEL)),
                _const_spec((1, Q_LORA)),
                _const_spec((1, KV_LORA)),
                _const_spec((Q_LORA, hk)),
                _const_spec((1, LANES)),
                _const_spec((KV_LORA, hk)),
                _const_spec((1, LANES)),
                _const_spec((KV_LORA, hk)),
                _const_spec((2 * LANES, 2 * LANES))]
    gains = "rope" if latent else "plain"
    args = [x, mod, w["norm_mix"], w["w_in"], w["qln"], w["kvn"], w["wq"], w["gq_" + gains], w["wk"],
            w["gk_" + gains], w["wv"], w["e2"]]
    if latent:
        assert tm % GRID_W == 0
        in_specs.append(_const_spec(rope_tabs.shape))
        args.append(rope_tabs)
    slab = jax.ShapeDtypeStruct((MLA_HEADS, n, LANES), BF16)
    out_shape = (jax.ShapeDtypeStruct((n, POOL_WIDTH), F32), slab, slab, slab,
                 jax.ShapeDtypeStruct((n, KV_LORA), F32),
                 jax.ShapeDtypeStruct((n, MLA_ROPE), F32))
    out_specs = (pl.BlockSpec((tm, POOL_WIDTH), row),
                 _slab_spec(MLA_HEADS, tm), _slab_spec(MLA_HEADS, tm), _slab_spec(MLA_HEADS, tm),
                 pl.BlockSpec((tm, KV_LORA), row), pl.BlockSpec((tm, MLA_ROPE), row))
    return pl.pallas_call(
        functools.partial(_l0_pre_kernel, latent),
        out_shape=out_shape, grid=(n // tm,), in_specs=in_specs, out_specs=out_specs,
        compiler_params=_params(1), name="l0_pre_latent" if latent else "l0_pre_ctx",
    )(*args)


def _cache_kv_kernel(ckv_ref, kr_ref, wk_ref, gk_ref, wv_ref, e2_ref, k_ref, v_ref):
    _mla_keys_values(ckv_ref[...], kr_ref[...], wk_ref, gk_ref[...], wv_ref, e2_ref, None, k_ref, v_ref)


def _cache_kv(ckv, kr_slab, w):
    n = ckv.shape[0]
    tm = TOKEN_TILE
    row = lambda i: (i, 0)
    hk = MLA_HEADS * LANES
    slab = jax.ShapeDtypeStruct((MLA_HEADS, n, LANES), BF16)
    return pl.pallas_call(
        _cache_kv_kernel,
        out_shape=(slab, slab),
        grid=(n // tm,),
        in_specs=[pl.BlockSpec((tm, KV_LORA), row), pl.BlockSpec((tm, LANES), row),
                  _const_spec((KV_LORA, hk)), _const_spec((1, LANES)), _const_spec((KV_LORA, hk)),
                  _const_spec((2 * LANES, 2 * LANES))],
        out_specs=(_slab_spec(MLA_HEADS, tm), _slab_spec(MLA_HEADS, tm)),
        compiler_params=_params(1), name="l0_cache_kv",
    )(ckv, kr_slab, w["wk"], w["gk_plain"], w["wv"], w["e2"])


def _pool_mixer(seq_len, tile, prev_ref, u_ref, next_ref, wp_ref, ps_ref, buf_ref):
    tm = u_ref.shape[0]
    piece = min(tm, seq_len)
    n_pieces = tm // piece
    stride = piece + 2 * POOL_HALO
    starts = []
    for j in range(n_pieces):
        pos0 = (tile * tm + j * piece) % seq_len
        prev = prev_ref[...] if j == 0 else u_ref[j * piece - POOL_HALO:j * piece, :]
        nxt = next_ref[...] if j == n_pieces - 1 else u_ref[(j + 1) * piece:(j + 1) * piece + POOL_HALO, :]
        base = j * stride
        buf_ref[base:base + POOL_HALO, :] = jnp.where(pos0 != 0, prev, 0.0)
        buf_ref[base + POOL_HALO:base + POOL_HALO + piece, :] = u_ref[j * piece:(j + 1) * piece, :]
        buf_ref[base + POOL_HALO + piece:base + stride, :] = jnp.where(pos0 + piece != seq_len, nxt, 0.0)
        starts.append(pos0)
    def group(g):
        win = POOL_WINDOWS[g]
        lanes = pl.ds(g * POOL_GROUP, POOL_GROUP)
        pooled = []
        for j in range(n_pieces):
            centre = j * stride + POOL_HALO
            lo = starts[j] + lax.broadcasted_iota(jnp.int32, (piece, 1), 0) - win // 2
            cnt = (jnp.minimum(lo + win, seq_len) - jnp.maximum(lo, 0)).astype(F32)
            acc = buf_ref[pl.ds(centre - win // 2, piece), lanes]
            for d in range(1 - win // 2, win - win // 2):
                acc = acc + buf_ref[pl.ds(centre + d, piece), lanes]
            pooled.append(acc / cnt - buf_ref[pl.ds(centre, piece), lanes])
        pooled = pooled[0] if n_pieces == 1 else jnp.concatenate(pooled, axis=0)
        return (_dot(pooled.astype(BF16), wp_ref[g]) * ps_ref[:, lanes]).astype(BF16)

    return [functools.partial(group, g) for g in range(len(POOL_WINDOWS))]


def _attend(streams, running_max):
    items = [(si, ch) for si, (_, chunks, _) in enumerate(streams) for ch in chunks]

    def score(item):
        si, (k_fn, _, extra_fn) = item
        s = _dot_nt(k_fn(), streams[si][0])
        return s if extra_fn is None else s + extra_fn()

    state = [None] * len(streams)
    pending = [score(item) for item in items[:SCORE_LOOKAHEAD]]
    for i, (si, (_, v_fn, _)) in enumerate(items):
        s = pending.pop(0)
        if i + SCORE_LOOKAHEAD < len(items):
            pending.append(score(items[i + SCORE_LOOKAHEAD]))
        want_den = streams[si][2]
        m_new, alpha = None, 1.0
        if state[si] is not None:
            m, acc, den = state[si]
        if running_max:
            m_new = jnp.max(s, axis=0, keepdims=True)
            if state[si] is not None:
                m_new = jnp.maximum(m, m_new)
                alpha = jnp.exp2(m - m_new)
            s = s - m_new
        p = jnp.exp2(s)
        pv = _dot_tn(v_fn(), p.astype(BF16))
        psum = jnp.sum(p, axis=0, keepdims=True) if want_den else None
        if state[si] is None:
            state[si] = (m_new, pv, psum)
        elif running_max:
            state[si] = (m_new, acc * alpha + pv, den * alpha + psum if want_den else None)
        else:
            state[si] = (None, acc + pv, den + psum if want_den else None)
    return [(acc, den) for _, acc, den in state]


def _ref_chunks(k_ref, v_ref, slab, chunk, extra_fn=None):
    nk = k_ref.shape[1]
    out = []
    for c in range(0, nk, chunk):
        rows = pl.ds(c, min(chunk, nk - c))
        out.append((lambda rows=rows: k_ref[slab, rows, :], lambda rows=rows: v_ref[slab, rows, :], extra_fn))
    return out


def _mla_finish(o_t):
    return o_t[:MLA_V] / o_t[MLA_V:MLA_V + 1]


def _na_queries(q2):
    low = lax.broadcasted_iota(jnp.int32, (1, LANES), 1) < NA_HEAD_DIM
    zero = jnp.zeros_like(q2)
    return jnp.concatenate([jnp.where(low, q2, zero), jnp.where(low, zero, q2)], axis=0)


def _na_finish(o_t, den):
    nq = o_t.shape[1] // 2
    o = o_t / den
    return jnp.concatenate([o[:NA_HEAD_DIM, :nq], o[NA_HEAD_DIM:, nq:]], axis=0)


def _attend_guarded(bound, streams_fn):
    return lax.cond(bound <= MAX_SCORE_BOUND,
                    lambda: _attend(streams_fn(False), False),
                    lambda: _attend(streams_fn(True), True))


def _mla_ctx_kernel(bound_ref, q_ref, k_ref, v_ref, o_ref):
    seq = k_ref.shape[1]
    res = _attend_guarded(bound_ref[0, 0], lambda _: [(q_ref[h], _ref_chunks(k_ref, v_ref, h, seq), False)
                                                      for h in range(MLA_HEADS)])
    for hp in range(MLA_HEADS // 2):
        o_t = jnp.concatenate([_mla_finish(res[h][0]) for h in (2 * hp, 2 * hp + 1)], axis=0)
        o_ref[:, hp * LANES:(hp + 1) * LANES] = o_t.T.astype(BF16)


def _na_ctx_kernel(bound_ref, q_ref, k_ref, v_ref, o_ref):
    seq = k_ref.shape[1]
    res = _attend_guarded(bound_ref[0, 0], lambda _: [(_na_queries(q_ref[s]), _ref_chunks(k_ref, v_ref, s, seq), True)
                                                      for s in range(NA_SLABS)])
    for s in range(NA_SLABS):
        o_ref[:, s * LANES:(s + 1) * LANES] = _na_finish(*res[s]).T.astype(BF16)


def _ctx_attn(body, q, k, v, bound, seq, width, name):
    slabs, n, _ = q.shape
    spec = pl.BlockSpec((slabs, seq, LANES), lambda b: (0, b, 0))
    return pl.pallas_call(
        body,
        out_shape=jax.ShapeDtypeStruct((n, width), BF16),
        grid=(n // seq,),
        in_specs=[pl.BlockSpec(memory_space=pltpu.SMEM), spec, spec, spec],
        out_specs=pl.BlockSpec((seq, width), lambda b: (b, 0)),
        compiler_params=_params(1), name=name,
    )(bound.reshape(1, 1), q, k, v)


def _mla_lat_kernel(bound_ref, q_ref, kl_ref, vl_ref, kc_ref, vc_ref, o_ref):
    batch, head0 = pl.program_id(0), pl.program_id(1) * MLA_STEP_HEADS
    heads = range(MLA_STEP_HEADS)
    bounds = [bound_ref[batch, head0 + h] for h in heads]

    def streams(running_max):
        size = MLA_ONLINE_CHUNK if running_max else MLA_KEY_CHUNK
        return [(q_ref[h], _ref_chunks(kl_ref, vl_ref, h, size) + _ref_chunks(kc_ref, vc_ref, h, size), False)
                for h in heads]

    res = _attend_guarded(functools.reduce(jnp.maximum, bounds), streams)
    o_ref[...] = jnp.concatenate([_mla_finish(o) for o, _ in res], axis=0).T.astype(BF16)


def _normed_len(gain, dim):
    return (dim ** 0.5) * jnp.max(jnp.abs(gain.astype(F32)))


def _mla_lat_attn(q, k, v, kc, vc, bound, seq, past):
    n = q.shape[1]
    tq = MLA_Q_TILE
    nq = seq // tq
    hs = MLA_STEP_HEADS
    return pl.pallas_call(
        _mla_lat_kernel,
        out_shape=jax.ShapeDtypeStruct((n, MLA_HEADS * MLA_V), BF16),
        grid=(n // seq, MLA_HEADS // hs, nq),
        in_specs=[pl.BlockSpec(memory_space=pltpu.SMEM),
                  pl.BlockSpec((hs, tq, LANES), lambda b, h, i: (h, b * nq + i, 0)),
                  pl.BlockSpec((hs, seq, LANES), lambda b, h, i: (h, b, 0)),
                  pl.BlockSpec((hs, seq, LANES), lambda b, h, i: (h, b, 0)),
                  pl.BlockSpec((hs, past, LANES), lambda b, h, i: (h, b, 0)),
                  pl.BlockSpec((hs, past, LANES), lambda b, h, i: (h, b, 0))],
        out_specs=pl.BlockSpec((tq, hs * MLA_V), lambda b, h, i: (b * nq + i, h)),
        compiler_params=_params(3), name="l0_mla_latent_attn",
    )(bound, q, k, v, kc, vc)


def _na_first_row(r, rows):
    return jnp.clip(r - NA_KH // 2, 0, rows - NA_KH)


def _na_lat_kernel(rows, bound_ref, q_ref, k_ref, v_ref, kc_ref, vc_ref, t2_ref, o_ref):
    nq = NA_TILE_ROWS * GRID_W
    nk = NA_WIN_ROWS * GRID_W
    n_off = 2 * NA_KH - 1
    r0 = pl.program_id(1) * NA_TILE_ROWS
    w0 = jnp.clip(r0 - NA_KH // 2, 0, rows - NA_WIN_ROWS)

    def block_index(j, t):
        key_row = w0 + j
        i = r0 + 2 * t
        first_u, first_l = _na_first_row(i, rows), _na_first_row(i + 1, rows)
        in_u = (key_row >= first_u) & (key_row < first_u + NA_KH)
        in_l = (key_row >= first_l) & (key_row < first_l + NA_KH)
        d = jnp.clip(key_row - i + NA_KH - 1, 0, n_off - 1)
        return jnp.where(in_u & in_l, d, jnp.where(in_u, n_off, jnp.where(in_l, n_off + 1, n_off + 2)))

    idx = [[block_index(j, t) for t in range(NA_TILE_ROWS // 2)] for j in range(NA_WIN_ROWS)]

    def local_bias(s):
        heads = [jnp.concatenate([jnp.concatenate([t2_ref[2 * s + hh, i] for i in row], axis=1) for row in idx], axis=0)
                 for hh in range(2)]
        return jnp.concatenate(heads, axis=1)

    batch = pl.program_id(0)
    bounds = [bound_ref[batch, h] for h in range(NA_HEADS)]

    res = _attend_guarded(
        functools.reduce(jnp.maximum, bounds),
        lambda _: [(_na_queries(q_ref[s]),
                    _ref_chunks(k_ref, v_ref, s, nk, functools.partial(local_bias, s)) + _ref_chunks(kc_ref, vc_ref, s, nk),
                    True) for s in range(NA_SLABS)])
    for s in range(NA_SLABS):
        o_ref[:, s * LANES:(s + 1) * LANES] = _na_finish(*res[s]).T.astype(BF16)


def _na_lat_attn(q, k, v, kc, vc, t2, bound, seq, past):
    slabs, n, _ = q.shape
    rows = seq // GRID_W
    tiles = rows // NA_TILE_ROWS
    nq = NA_TILE_ROWS * GRID_W
    nk = NA_WIN_ROWS * GRID_W

    def win_map(b, t):
        w0 = jnp.clip(t * NA_TILE_ROWS - NA_KH // 2, 0, rows - NA_WIN_ROWS)
        return (0, pl.multiple_of(b * seq + w0 * GRID_W, GRID_W), 0)

    win_spec = pl.BlockSpec((pl.Element(slabs), pl.Element(nk), pl.Element(LANES)), win_map)
    ctx_spec = pl.BlockSpec((slabs, past, LANES), lambda b, t: (0, b, 0))
    return pl.pallas_call(
        functools.partial(_na_lat_kernel, rows),
        out_shape=jax.ShapeDtypeStruct((n, slabs * LANES), BF16),
        grid=(n // seq, tiles),
        in_specs=[pl.BlockSpec(memory_space=pltpu.SMEM),
                  pl.BlockSpec((slabs, nq, LANES), lambda b, t: (0, b * tiles + t, 0)),
                  win_spec, win_spec, ctx_spec, ctx_spec,
                  _const_spec(t2.shape)],
        out_specs=pl.BlockSpec((nq, slabs * LANES), lambda b, t: (b * tiles + t, 0)),
        compiler_params=_params(2), name="l1_na_latent_attn",
    )(bound, q, k, v, kc, vc, t2)


def _pair_norm(x2, g2):
    low = lax.broadcasted_iota(jnp.int32, (1, LANES), 1) < NA_HEAD_DIM
    sq = x2 * x2
    ss_all = jnp.sum(sq, axis=-1, keepdims=True)
    ss_low = jnp.sum(jnp.where(low, sq, 0.0), axis=-1, keepdims=True)
    ss = jnp.where(low, ss_low, ss_all - ss_low)
    return x2 * lax.rsqrt(ss * (1.0 / NA_HEAD_DIM) + EPS) * g2


def _l1_pre_kernel(keep_f32, x_ref, mod_ref, nw_ref, win_ref, gq_ref, gk_ref, q_ref, k_ref, v_ref, *f32_refs):
    h = _modulate(x_ref[...], nw_ref[...], mod_ref[0:1, :], mod_ref[1:2, :]).astype(BF16)
    width = NA_SLABS * LANES
    q = _dot(h, win_ref[:, 0:width])
    k = _dot(h, win_ref[:, width:2 * width])
    v = _dot(h, win_ref[:, 2 * width:3 * width])
    gq = gq_ref[...]
    gk = gk_ref[...]
    for s in range(NA_SLABS):
        ls = slice(s * LANES, (s + 1) * LANES)
        q_ref[s] = _pair_norm(q[:, ls], gq).astype(BF16)
        kn = _pair_norm(k[:, ls], gk)
        k_ref[s] = kn.astype(BF16)
        v_ref[s] = v[:, ls].astype(BF16)
        if keep_f32:
            f32_refs[0][:, ls] = kn
    if keep_f32:
        f32_refs[1][...] = v


def _l1_pre(x, mod, tiles_per_group, keep_f32, w):
    n = x.shape[0]
    tm = TOKEN_TILE
    row = lambda i: (i, 0)
    width = NA_SLABS * LANES
    slab = jax.ShapeDtypeStruct((NA_SLABS, n, LANES), BF16)
    out_shape = [slab, slab, slab]
    out_specs = [_slab_spec(NA_SLABS, tm)] * 3
    if keep_f32:
        out_shape += [jax.ShapeDtypeStruct((n, width), F32)] * 2
        out_specs += [pl.BlockSpec((tm, width), row)] * 2
    return pl.pallas_call(
        functools.partial(_l1_pre_kernel, keep_f32),
        out_shape=tuple(out_shape), grid=(n // tm,),
        in_specs=[pl.BlockSpec((tm, D_MODEL), row),
                  pl.BlockSpec((None, SUBLANES, D_MODEL), lambda i: (i // tiles_per_group, 0, 0)),
                  _const_spec((1, D_MODEL)),
                  _const_spec((D_MODEL, 3 * width)),
                  _const_spec((1, LANES)),
                  _const_spec((1, LANES))],
        out_specs=tuple(out_specs),
        compiler_params=_params(1), name="l1_pre_ctx" if keep_f32 else "l1_pre_latent",
    )(x, mod, w["norm_mix"], w["w_in"], w["gq"], w["gk"])


def _post_kernel(pool_seq, x_ref, attn_ref, *rest):
    if pool_seq is None:
        mod_ref, nw_ref, wo_ref, w1_ref, w2_ref, o_ref = rest
        mixed = _dot(attn_ref[...], wo_ref[...])
    else:
        (first_refs, next_refs, wp_ref, ps_ref, mod_ref, nw_ref, wo_ref, w1_ref, w2_ref, o_ref,
         buf_ref, pooled_ref) = rest[0:3], rest[3:6], *rest[6:]
        step = pl.program_id(0)

        @pl.when(step == 0)
        def _():
            for g, group in enumerate(_pool_mixer(pool_seq, 0, *first_refs, wp_ref, ps_ref, buf_ref)):
                pooled_ref[:, g * POOL_GROUP:(g + 1) * POOL_GROUP] = group()

        mixed = _dot(jnp.concatenate([pooled_ref[...], attn_ref[...]], axis=-1), wo_ref[...])
        next_groups = _pool_mixer(pool_seq, jnp.minimum(step + 1, pl.num_programs(0) - 1), *next_refs,
                                  wp_ref, ps_ref, buf_ref)
    x1 = x_ref[...] + mod_ref[2:3, :] * mixed
    h = _modulate(x1, nw_ref[...], mod_ref[3:4, :], mod_ref[4:5, :]).astype(BF16)
    acc = jnp.zeros(x1.shape, F32)
    n_chunks = D_FF // FF_CHUNK
    for c in range(n_chunks):
        cs = pl.ds(c * FF_CHUNK, FF_CHUNK)
        u = jnp.square(jnp.maximum(_dot(h, w1_ref[:, cs]), 0.0)).astype(BF16)
        acc = acc + _dot(u, w2_ref[cs, :])
        if pool_seq is not None:
            for g in range(c * len(next_groups) // n_chunks, (c + 1) * len(next_groups) // n_chunks):
                pooled_ref[:, g * POOL_GROUP:(g + 1) * POOL_GROUP] = next_groups[g]()
    o_ref[...] = x1 + mod_ref[5:6, :] * acc


def _post(x, attn, mod, tiles_per_group, w, name, pool=None):
    n = x.shape[0]
    tm = TOKEN_TILE
    row = lambda i: (i, 0)
    in_specs = [pl.BlockSpec((tm, D_MODEL), row), pl.BlockSpec((tm, attn.shape[1]), row)]
    args = [x, attn]
    scratch = []
    pool_seq = None
    if pool is not None:
        a_in, pool_seq = pool
        hb = tm // POOL_HALO
        last = n // POOL_HALO - 1
        pieces = tm // min(tm, pool_seq)
        last_tile = n // tm - 1

        def tile_specs(tile_of):
            return [pl.BlockSpec((POOL_HALO, POOL_WIDTH), lambda i: (jnp.maximum(tile_of(i) * hb - 1, 0), 0)),
                    pl.BlockSpec((tm, POOL_WIDTH), lambda i: (tile_of(i), 0)),
                    pl.BlockSpec((POOL_HALO, POOL_WIDTH), lambda i: (jnp.minimum((tile_of(i) + 1) * hb, last), 0))]

        in_specs += tile_specs(lambda i: 0) + tile_specs(lambda i: jnp.minimum(i + 1, last_tile))
        in_specs += [_const_spec((len(POOL_WINDOWS), POOL_GROUP, POOL_GROUP)), _const_spec((1, POOL_WIDTH))]
        args += [a_in] * 6 + [w["w_pool"], w["pool_scale"]]
        scratch = [pltpu.VMEM((tm + 2 * POOL_HALO * pieces, POOL_WIDTH), F32), pltpu.VMEM((tm, POOL_WIDTH), BF16)]
    in_specs += [pl.BlockSpec((None, SUBLANES, D_MODEL), lambda i: (i // tiles_per_group, 0, 0)),
                 _const_spec((1, D_MODEL)),
                 _const_spec((D_MODEL, D_MODEL)),
                 _const_spec((D_MODEL, D_FF)),
                 _const_spec((D_FF, D_MODEL))]
    args += [mod, w["norm_mlp"], w["w_out"], w["w_mlp1"], w["w_mlp2"]]
    return pl.pallas_call(
        functools.partial(_post_kernel, pool_seq),
        out_shape=jax.ShapeDtypeStruct((n, D_MODEL), F32), grid=(n // tm,),
        in_specs=in_specs, out_specs=pl.BlockSpec((tm, D_MODEL), row),
        scratch_shapes=scratch, compiler_params=_params(1), name=name,
    )(*args)


def _rope_tables(seq):
    assert seq == GRID_W * GRID_W
    half = MLA_ROPE // 2
    inv_freq = jnp.power(ROPE_THETA, -jnp.arange(0, half, 2, dtype=F32) / half)
    ang = jnp.arange(GRID_W, dtype=F32)[:, None] * inv_freq[None, :]
    cos, sin = jnp.cos(ang), jnp.sin(ang)
    fill = lambda n, v=0.0: jnp.full((GRID_W, n), v, F32)
    spare = LANES - MLA_QK
    return jnp.stack([
        jnp.concatenate([fill(MLA_NOPE, 1.0), fill(half), cos, cos, fill(spare)], axis=1),
        jnp.concatenate([fill(MLA_NOPE), cos, cos, fill(half), fill(spare)], axis=1),
        jnp.concatenate([fill(MLA_NOPE), fill(half), -sin, sin, fill(spare)], axis=1),
        jnp.concatenate([fill(MLA_NOPE), -sin, sin, fill(half), fill(spare)], axis=1)])


def _rope_partner(tail):
    shape = tail.shape
    t = tail.reshape(shape[:-1] + (2, 2, MLA_ROPE // 4))
    return t[..., ::-1, :].reshape(shape)


def _na_bias_table(rel_bias, rows):
    qcol = jnp.arange(GRID_W)
    kcol = jnp.arange(GRID_W)
    col_start = jnp.clip(qcol - NA_KW // 2, 0, GRID_W - NA_KW)
    col_mask = (kcol[:, None] >= col_start[None, :]) & (kcol[:, None] < col_start[None, :] + NA_KW)
    offs = jnp.arange(2 * NA_KW - 1)
    onehot = (kcol[None, :, None] - qcol[None, None, :] + NA_KW - 1 == offs[:, None, None]).astype(F32)
    blocks = jnp.einsum('hdj,jkq->hdkq', rel_bias.astype(F32) * LOG2E, onehot, precision=lax.Precision.HIGHEST)
    blocks = jnp.where(col_mask[None, None], blocks, NEG_INF)
    neg = jnp.full_like(blocks[:, :1], NEG_INF)
    both = jnp.concatenate([blocks, jnp.concatenate([neg, blocks[:, :-1]], axis=1)], axis=-1)
    d_upper, d_lower = _na_partial_offsets(rows)
    upper_only = jnp.concatenate([blocks[:, d_upper:d_upper + 1], neg], axis=-1)
    lower_only = jnp.concatenate([neg, blocks[:, d_lower - 1:d_lower]], axis=-1)
    return jnp.concatenate([both, upper_only, lower_only, jnp.concatenate([neg, neg], axis=-1)], axis=1)


def _na_partial_offsets(rows):
    first = lambda r: min(max(r - NA_KH // 2, 0), rows - NA_KH)
    upper, lower = set(), set()
    for r0 in range(0, rows, NA_TILE_ROWS):
        w0 = min(max(r0 - NA_KH // 2, 0), rows - NA_WIN_ROWS)
        for j in range(NA_WIN_ROWS):
            for t in range(NA_TILE_ROWS // 2):
                key_row, i = w0 + j, r0 + 2 * t
                in_u = first(i) <= key_row < first(i) + NA_KH
                in_l = first(i + 1) <= key_row < first(i + 1) + NA_KH
                d = key_row - i + NA_KH - 1
                if in_u and in_l:
                    assert 1 <= d <= 2 * NA_KH - 2
                elif in_u:
                    upper.add(d)
                elif in_l:
                    lower.add(d)
    assert len(upper) == 1 and len(lower) == 1, (upper, lower)
    return upper.pop(), lower.pop()


def _prep_l0(w_in, q_lora_norm, kv_lora_norm, w_q_up, w_kv_up, mla_q_norm, mla_k_norm, w_pool, pool_scale,
             w_out, norm_mix, norm_mlp, w_mlp1, w_mlp2):
    lat_end = POOL_WIDTH + Q_LORA + KV_LORA
    w_in_pad = jnp.concatenate([w_in[:, :lat_end], jnp.zeros((D_MODEL, MLA_NOPE), w_in.dtype),
                                w_in[:, lat_end:], _rope_partner(w_in[:, lat_end:])], axis=1)
    pad = LANES - MLA_QK
    hk = MLA_HEADS * LANES
    wq = w_q_up.reshape(Q_LORA, MLA_HEADS, MLA_QK)
    wq = jnp.concatenate([wq, _rope_partner(wq[:, :, MLA_NOPE:])], axis=-1)
    kv = w_kv_up.reshape(KV_LORA, MLA_HEADS, MLA_NOPE + MLA_V)
    wk = jnp.pad(kv[:, :, :MLA_NOPE], ((0, 0), (0, 0), (0, LANES - MLA_NOPE)))
    wv = jnp.pad(kv[:, :, MLA_NOPE:], ((0, 0), (0, 0), (0, LANES - MLA_V)))
    gq = mla_q_norm * (MLA_QK ** -0.5 * LOG2E)
    slab_lane = jnp.arange(2 * LANES)
    e2 = (slab_lane[:, None] // LANES == slab_lane[None, :] // LANES) & (slab_lane[:, None] % LANES < MLA_QK)
    return dict(
        norm_mix=norm_mix.reshape(1, -1), norm_mlp=norm_mlp.reshape(1, -1),
        w_in=w_in_pad.astype(BF16),
        qln=q_lora_norm.reshape(1, -1), kvn=kv_lora_norm.reshape(1, -1),
        wq=wq.reshape(Q_LORA, hk).astype(BF16),
        gq_plain=jnp.pad(gq, (0, pad)).reshape(1, LANES),
        gq_rope=jnp.concatenate([gq, _rope_partner(gq[MLA_NOPE:])]).reshape(1, LANES),
        wk=wk.reshape(KV_LORA, hk).astype(BF16),
        gk_plain=jnp.pad(mla_k_norm, (0, pad)).reshape(1, LANES),
        gk_rope=jnp.concatenate([mla_k_norm, _rope_partner(mla_k_norm[MLA_NOPE:])]).reshape(1, LANES),
        wv=wv.reshape(KV_LORA, hk).astype(BF16),
        e2=e2.astype(BF16),
        w_pool=w_pool.astype(BF16), pool_scale=pool_scale.reshape(1, -1),
        w_out=w_out.astype(BF16), w_mlp1=w_mlp1.astype(BF16), w_mlp2=w_mlp2.astype(BF16))


def _split_mods(m):
    mods = m.reshape(SUBLANES, N_ADA, D_MODEL)[:3]
    mods = jnp.pad(mods, ((0, 0), (0, SUBLANES - N_ADA), (0, 0)))
    return mods[0:1], mods[1:3]


def _to_slabs(x, n):
    return x.reshape(n, -1, LANES).transpose(1, 0, 2).astype(BF16)


def kernel(x_prompt, x_sample, cache_l0_mla_ckv, cache_l0_mla_krope, cache_l1_na_k, cache_l1_na_v, c, c_ctx, w_ada_l0, b_ada_l0, norm_mix_l0, norm_mlp_l0, w_mlp1_l0, w_mlp2_l0, w_in_l0, q_lora_norm_l0, kv_lora_norm_l0, w_q_up_l0, w_kv_up_l0, mla_q_norm_l0, mla_k_norm_l0, w_pool_l0, pool_scale_l0, w_out_l0, w_ada_l1, b_ada_l1, norm_mix_l1, norm_mlp_l1, w_mlp1_l1, w_mlp2_l1, w_in_l1, na_q_norm_l1, na_k_norm_l1, rel_bias_l1, w_out_l1):
    batch, seq, _ = x_prompt.shape
    dec_batch, dec_seq, _ = x_sample.shape
    past = cache_l0_mla_ckv.shape[1]
    xp = x_prompt.reshape(batch * seq, D_MODEL)
    xs = x_sample.reshape(dec_batch * dec_seq, D_MODEL)
    tiles_p = (batch * seq) // TOKEN_TILE
    tiles_s = dec_seq // TOKEN_TILE

    cond8 = jnp.concatenate([c_ctx[None, :], c, jnp.zeros((SUBLANES - 1 - dec_batch, D_MODEL), F32)], axis=0)
    mod0_p, mod0_s = _split_mods(_ada(cond8, w_ada_l0, b_ada_l0))
    mod1_p, mod1_s = _split_mods(_ada(cond8, w_ada_l1, b_ada_l1))

    w0 = _prep_l0(w_in_l0, q_lora_norm_l0, kv_lora_norm_l0, w_q_up_l0, w_kv_up_l0, mla_q_norm_l0, mla_k_norm_l0,
                  w_pool_l0, pool_scale_l0, w_out_l0, norm_mix_l0, norm_mlp_l0, w_mlp1_l0, w_mlp2_l0)
    a_p, q_p, k_p, v_p, ckv_p, kr_p = _l0_pre(xp, mod0_p, tiles_p, False, w0, None)
    a_s, q_s, k_s, v_s, _, _ = _l0_pre(xs, mod0_s, tiles_s, True, w0, _rope_tables(dec_seq))
    kr_cache = jnp.pad(cache_l0_mla_krope.reshape(dec_batch * past, MLA_ROPE), ((0, 0), (MLA_NOPE, LANES - MLA_QK)))
    k_c, v_c = _cache_kv(cache_l0_mla_ckv.reshape(dec_batch * past, KV_LORA), kr_cache, w0)
    mla_bound = _normed_len(mla_q_norm_l0 * (MLA_QK ** -0.5 * LOG2E), MLA_QK) * _normed_len(mla_k_norm_l0, MLA_QK)
    attn_p = _ctx_attn(_mla_ctx_kernel, q_p, k_p, v_p, mla_bound, seq, MLA_HEADS * MLA_V, "l0_mla_ctx_attn")
    attn_s = _mla_lat_attn(q_s, k_s, v_s, k_c, v_c, jnp.full((dec_batch, MLA_HEADS), mla_bound, F32), dec_seq, past)
    xp = _post(xp, attn_p, mod0_p, tiles_p, w0, "l0_post_ctx", pool=(a_p, seq))
    xs = _post(xs, attn_s, mod0_s, tiles_s, w0, "l0_post_latent", pool=(a_s, dec_seq))

    w1 = dict(norm_mix=norm_mix_l1.reshape(1, -1), norm_mlp=norm_mlp_l1.reshape(1, -1),
              w_in=w_in_l1.astype(BF16),
              gq=jnp.tile(na_q_norm_l1 * (NA_HEAD_DIM ** -0.5 * LOG2E), 2).reshape(1, LANES),
              gk=jnp.tile(na_k_norm_l1, 2).reshape(1, LANES),
              w_out=w_out_l1.astype(BF16), w_mlp1=w_mlp1_l1.astype(BF16), w_mlp2=w_mlp2_l1.astype(BF16))
    q1_p, k1_p, v1_p, k1_new, v1_new = _l1_pre(xp, mod1_p, tiles_p, True, w1)
    q1_s, k1_s, v1_s = _l1_pre(xs, mod1_s, tiles_s, False, w1)
    q_len, k_len = _normed_len(w1["gq"], NA_HEAD_DIM), _normed_len(w1["gk"], NA_HEAD_DIM)
    na_p = _ctx_attn(_na_ctx_kernel, q1_p, k1_p, v1_p, q_len * k_len, seq, NA_SLABS * LANES, "l1_na_ctx_attn")
    cache_len = jnp.sqrt(jnp.max(jnp.sum(jnp.square(cache_l1_na_k.astype(F32)), axis=-1), axis=1))
    na_bound = (q_len * jnp.maximum(k_len, cache_len)
                + jnp.max(jnp.abs(rel_bias_l1.astype(F32)) * LOG2E, axis=(1, 2))[None, :])
    na_s = _na_lat_attn(q1_s, k1_s, v1_s,
                        _to_slabs(cache_l1_na_k, dec_batch * past), _to_slabs(cache_l1_na_v, dec_batch * past),
                        _na_bias_table(rel_bias_l1, dec_seq // GRID_W), na_bound, dec_seq, past)
    xp = _post(xp, na_p, mod1_p, tiles_p, w1, "l1_post_ctx")
    xs = _post(xs, na_s, mod1_s, tiles_s, w1, "l1_post_latent")

    return (xp.reshape(batch, seq, D_MODEL), xs.reshape(dec_batch, dec_seq, D_MODEL),
            ckv_p.reshape(batch, seq, KV_LORA), kr_p.reshape(batch, seq, MLA_ROPE),
            k1_new.reshape(batch, seq, NA_HEADS, NA_HEAD_DIM), v1_new.reshape(batch, seq, NA_HEADS, NA_HEAD_DIM))
```

```python
import functools

import jax
import jax.numpy as jnp
from jax import lax
from jax.experimental import pallas as pl
from jax.experimental.pallas import tpu as pltpu

F32 = jnp.float32
BF16 = jnp.bfloat16

D_MODEL = 1024
D_FF = 4 * D_MODEL
DEPTH = 2
N_ADA = 6
EPS = 1e-6
NEG_INF = -1e30
ROPE_THETA = 10000.0
GRID_W = 64
POOL_WIDTH = 512
POOL_WINDOWS = (2, 4, 8, 16)
POOL_GROUP = 128
POOL_HALO = 8
MLA_HEADS = 8
MLA_NOPE = 64
MLA_ROPE = 32
MLA_QK = MLA_NOPE + MLA_ROPE
MLA_V = 64
Q_LORA = 256
KV_LORA = 128
NA_HEADS = 16
NA_HEAD_DIM = 64
NA_KH = 8
NA_KW = 16
LOG2E = 1.4426950408889634

LANES = 128
SUBLANES = 8
VMEM_LIMIT = 56 * 1024 * 1024

TOKEN_TILE = 512
MLA_Q_TILE = 512
MLA_STEP_HEADS = 2
MLA_KEY_CHUNK = 2048
MLA_ONLINE_CHUNK = 1024
SCORE_LOOKAHEAD = 2
MAX_SCORE_BOUND = 40.0
FF_CHUNK = 1024
CTX_STEP_SEQS = 2
NA_SLABS = NA_HEADS * NA_HEAD_DIM // LANES
NA_TILE_ROWS = 4
NA_WIN_ROWS = NA_TILE_ROWS + NA_KH - 1


def _params(n_axes):
    return pltpu.CompilerParams(dimension_semantics=("arbitrary",) * n_axes,
                                vmem_limit_bytes=VMEM_LIMIT)


def _const_spec(shape):
    zeros = (0,) * len(shape)
    return pl.BlockSpec(shape, lambda *_: zeros, pipeline_mode=pl.Buffered(1))


def _dot(a, b):
    return jnp.dot(a, b, preferred_element_type=F32)


def _dot_nt(a, b):
    return lax.dot_general(a, b, (((1,), (1,)), ((), ())), preferred_element_type=F32)


def _dot_tn(a, b):
    return lax.dot_general(a, b, (((0,), (0,)), ((), ())), preferred_element_type=F32)


def _rms(x, w, n=None):
    n = x.shape[-1] if n is None else n
    ss = jnp.sum(x * x, axis=-1, keepdims=True)
    return x * lax.rsqrt(ss * (1.0 / n) + EPS) * w


def _modulate(x, nw, shift, scale):
    return _rms(x, nw) * (1.0 + scale) + shift


def _ada_kernel(cond_ref, w0_ref, w1_ref, b_ref, o_ref):
    c = cond_ref[...]
    s = (c * jax.nn.sigmoid(c)).astype(BF16)
    for layer, w_ref in enumerate((w0_ref, w1_ref)):
        @pl.when(pl.program_id(0) == layer)
        def _(w_ref=w_ref):
            o_ref[...] = _dot(s, w_ref[...].astype(BF16)) + b_ref[...]


def _ada(cond8, w_ada, b_ada):
    n_out = w_ada[0].shape[1]
    tn = D_MODEL
    nb = n_out // tn
    return pl.pallas_call(
        _ada_kernel,
        out_shape=jax.ShapeDtypeStruct((DEPTH, SUBLANES, n_out), F32),
        grid=(DEPTH, nb),
        in_specs=[pl.BlockSpec((SUBLANES, D_MODEL), lambda l, j: (0, 0)),
                  pl.BlockSpec((D_MODEL, tn), lambda l, j: (0, jnp.where(l == 0, j, nb - 1))),
                  pl.BlockSpec((D_MODEL, tn), lambda l, j: (0, jnp.where(l == 1, j, 0))),
                  pl.BlockSpec((None, 1, tn), lambda l, j: (l, 0, j))],
        out_specs=pl.BlockSpec((None, SUBLANES, tn), lambda l, j: (l, 0, j)),
        compiler_params=_params(2),
        name="ada",
    )(cond8, w_ada[0], w_ada[1], jnp.stack(b_ada).reshape(DEPTH, 1, n_out))


def _rope(x, c, s):
    return x * c + pltpu.roll(x, LANES - MLA_ROPE, 1) * s


def _head_rinv(raw, e2_ref):
    sq = (raw * raw).astype(BF16)
    pair = 2 * LANES
    ss = jnp.concatenate([_dot(sq[:, p:p + pair], e2_ref[...]) for p in range(0, raw.shape[1], pair)], axis=1)
    return lax.rsqrt(ss * (1.0 / MLA_QK) + EPS)


def _mla_keys_values(ckv, kr_slab, wk_ref, gk, wv_ref, e2_ref, rope_tabs, k_ref, v_ref):
    c16 = ckv.astype(BF16)
    kn = _dot(c16, wk_ref[...])
    v = _dot(c16, wv_ref[...])
    rinv = _head_rinv(kn + jnp.tile(kr_slab, (1, MLA_HEADS)), e2_ref)
    tail = kr_slab * gk
    if rope_tabs is not None:
        tail = _rope(tail, *rope_tabs)
    ones_lane = (lax.broadcasted_iota(jnp.int32, (1, LANES), 1) == MLA_V).astype(F32)
    for h in range(MLA_HEADS):
        hs = slice(h * LANES, (h + 1) * LANES)
        k_ref[h] = (rinv[:, hs] * (kn[:, hs] * gk + tail)).astype(BF16)
        v_ref[h] = (v[:, hs] + ones_lane).astype(BF16)


def _l0_pre_kernel(latent, x_ref, mod_ref, nw_ref, win_ref, qln_ref, kvn_ref, wq_ref, gq_ref,
                   wk_ref, gk_ref, wv_ref, e2_ref, *rest):
    if latent:
        rope_ref, a_ref, q_ref, k_ref, v_ref, ckv_ref, kr_ref = rest
        tile_rows = x_ref.shape[0] // GRID_W
        row0 = (pl.program_id(0) * tile_rows) % rope_ref.shape[1]
        expand = lambda col, row: jnp.concatenate(
            [rope_ref[col] + rope_ref[row, pl.ds(row0 + r, 1), :] for r in range(tile_rows)], axis=0)
        rope_tabs = (expand(0, 1), expand(2, 3))
    else:
        a_ref, q_ref, k_ref, v_ref, ckv_ref, kr_ref = rest
        rope_tabs = None
    h = _modulate(x_ref[...], nw_ref[...], mod_ref[0:1, :], mod_ref[1:2, :]).astype(BF16)
    proj = _dot(h, win_ref[...])
    a_ref[...] = proj[:, :POOL_WIDTH]
    q_lat = proj[:, POOL_WIDTH:POOL_WIDTH + Q_LORA]
    kv_lat = proj[:, POOL_WIDTH + Q_LORA:POOL_WIDTH + Q_LORA + KV_LORA]
    kr_slab = proj[:, POOL_WIDTH + Q_LORA + KV_LORA:]
    q = _dot(_rms(q_lat, qln_ref[...]).astype(BF16), wq_ref[...])
    q_rinv = _head_rinv(q, e2_ref)
    gq = gq_ref[...]
    for hd in range(MLA_HEADS):
        hs = slice(hd * LANES, (hd + 1) * LANES)
        y = q[:, hs] * q_rinv[:, hs] * gq
        if rope_tabs is not None:
            y = _rope(y, *rope_tabs)
        q_ref[hd] = y.astype(BF16)
    ckv = _rms(kv_lat, kvn_ref[...])
    ckv_ref[...] = ckv
    kr_ref[...] = kr_slab[:, MLA_NOPE:MLA_QK]
    _mla_keys_values(ckv, kr_slab, wk_ref, gk_ref[...], wv_ref, e2_ref, rope_tabs, k_ref, v_ref)


def _slab_spec(slabs, tm):
    return pl.BlockSpec((slabs, tm, LANES), lambda i: (0, i, 0))


def _l0_pre(x, mod, tiles_per_group, latent, w, rope_tabs):
    n = x.shape[0]
    tm = TOKEN_TILE
    row = lambda i: (i, 0)
    hk = MLA_HEADS * LANES
    in_specs = [pl.BlockSpec((tm, D_MODEL), row),
                pl.BlockSpec((None, SUBLANES, D_MODEL), lambda i: (i // tiles_per_group, 0, 0)),
                _const_spec((1, D_MODEL)),
                _const_spec((D_MODEL, D_MODEL)),
                _const_spec((1, Q_LORA)),
                _const_spec((1, KV_LORA)),
                _const_spec((Q_LORA, hk)),
                _const_spec((1, LANES)),
                _const_spec((KV_LORA, hk)),
                _const_spec((1, LANES)),
                _const_spec((KV_LORA, hk)),
                _const_spec((2 * LANES, 2 * LANES))]
    gains = "rope" if latent else "plain"
    args = [x, mod, w["norm_mix"], w["w_in"], w["qln"], w["kvn"], w["wq"], w["gq_" + gains], w["wk"],
            w["gk_" + gains], w["wv"], w["e2"]]
    if latent:
        assert tm % GRID_W == 0
        in_specs.append(_const_spec(rope_tabs.shape))
        args.append(rope_tabs)
    slab = jax.ShapeDtypeStruct((MLA_HEADS, n, LANES), BF16)
    out_shape = (jax.ShapeDtypeStruct((n, POOL_WIDTH), F32), slab, slab, slab,
                 jax.ShapeDtypeStruct((n, KV_LORA), F32),
                 jax.ShapeDtypeStruct((n, MLA_ROPE), F32))
    out_specs = (pl.BlockSpec((tm, POOL_WIDTH), row),
                 _slab_spec(MLA_HEADS, tm), _slab_spec(MLA_HEADS, tm), _slab_spec(MLA_HEADS, tm),
                 pl.BlockSpec((tm, KV_LORA), row), pl.BlockSpec((tm, MLA_ROPE), row))
    return pl.pallas_call(
        functools.partial(_l0_pre_kernel, latent),
        out_shape=out_shape, grid=(n // tm,), in_specs=in_specs, out_specs=out_specs,
        compiler_params=_params(1), name="l0_pre_latent" if latent else "l0_pre_ctx",
    )(*args)


def _cache_kv_kernel(ckv_ref, kr_ref, wk_ref, gk_ref, wv_ref, e2_ref, k_ref, v_ref):
    _mla_keys_values(ckv_ref[...], kr_ref[...], wk_ref, gk_ref[...], wv_ref, e2_ref, None, k_ref, v_ref)


def _cache_kv(ckv, kr_slab, w):
    n = ckv.shape[0]
    tm = TOKEN_TILE
    row = lambda i: (i, 0)
    hk = MLA_HEADS * LANES
    slab = jax.ShapeDtypeStruct((MLA_HEADS, n, LANES), BF16)
    return pl.pallas_call(
        _cache_kv_kernel,
        out_shape=(slab, slab),
        grid=(n // tm,),
        in_specs=[pl.BlockSpec((tm, KV_LORA), row), pl.BlockSpec((tm, LANES), row),
                  _const_spec((KV_LORA, hk)), _const_spec((1, LANES)), _const_spec((KV_LORA, hk)),
                  _const_spec((2 * LANES, 2 * LANES))],
        out_specs=(_slab_spec(MLA_HEADS, tm), _slab_spec(MLA_HEADS, tm)),
        compiler_params=_params(1), name="l0_cache_kv",
    )(ckv, kr_slab, w["wk"], w["gk_plain"], w["wv"], w["e2"])


def _pool_mixer(seq_len, tile, prev_ref, u_ref, next_ref, wp_ref, ps_ref, buf_ref):
    tm = u_ref.shape[0]
    piece = min(tm, seq_len)
    n_pieces = tm // piece
    stride = piece + 2 * POOL_HALO
    starts = []
    for j in range(n_pieces):
        pos0 = (tile * tm + j * piece) % seq_len
        prev = prev_ref[...] if j == 0 else u_ref[j * piece - POOL_HALO:j * piece, :]
        nxt = next_ref[...] if j == n_pieces - 1 else u_ref[(j + 1) * piece:(j + 1) * piece + POOL_HALO, :]
        base = j * stride
        buf_ref[base:base + POOL_HALO, :] = jnp.where(pos0 != 0, prev, 0.0)
        buf_ref[base + POOL_HALO:base + POOL_HALO + piece, :] = u_ref[j * piece:(j + 1) * piece, :]
        buf_ref[base + POOL_HALO + piece:base + stride, :] = jnp.where(pos0 + piece != seq_len, nxt, 0.0)
        starts.append(pos0)
    def group(g):
        win = POOL_WINDOWS[g]
        lanes = pl.ds(g * POOL_GROUP, POOL_GROUP)
        pooled = []
        for j in range(n_pieces):
            centre = j * stride + POOL_HALO
            lo = starts[j] + lax.broadcasted_iota(jnp.int32, (piece, 1), 0) - win // 2
            cnt = (jnp.minimum(lo + win, seq_len) - jnp.maximum(lo, 0)).astype(F32)
            acc = buf_ref[pl.ds(centre - win // 2, piece), lanes]
            for d in range(1 - win // 2, win - win // 2):
                acc = acc + buf_ref[pl.ds(centre + d, piece), lanes]
            pooled.append(acc / cnt - buf_ref[pl.ds(centre, piece), lanes])
        pooled = pooled[0] if n_pieces == 1 else jnp.concatenate(pooled, axis=0)
        return (_dot(pooled.astype(BF16), wp_ref[g]) * ps_ref[:, lanes]).astype(BF16)

    return [functools.partial(group, g) for g in range(len(POOL_WINDOWS))]


def _attend(streams, running_max):
    items = [(si, ch) for si, (_, chunks, _) in enumerate(streams) for ch in chunks]

    def score(item):
        si, (k_fn, _, extra_fn) = item
        s = _dot_nt(k_fn(), streams[si][0])
        return s if extra_fn is None else s + extra_fn()

    state = [None] * len(streams)
    pending = [score(item) for item in items[:SCORE_LOOKAHEAD]]
    for i, (si, (_, v_fn, _)) in enumerate(items):
        s = pending.pop(0)
        if i + SCORE_LOOKAHEAD < len(items):
            pending.append(score(items[i + SCORE_LOOKAHEAD]))
        want_den = streams[si][2]
        m_new, alpha = None, 1.0
        if state[si] is not None:
            m, acc, den = state[si]
        if running_max:
            m_new = jnp.max(s, axis=0, keepdims=True)
            if state[si] is not None:
                m_new = jnp.maximum(m, m_new)
                alpha = jnp.exp2(m - m_new)
            s = s - m_new
        p = jnp.exp2(s)
        pv = _dot_tn(v_fn(), p.astype(BF16))
        psum = jnp.sum(p, axis=0, keepdims=True) if want_den else None
        if state[si] is None:
            state[si] = (m_new, pv, psum)
        elif running_max:
            state[si] = (m_new, acc * alpha + pv, den * alpha + psum if want_den else None)
        else:
            state[si] = (None, acc + pv, den + psum if want_den else None)
    return [(acc, den) for _, acc, den in state]


def _ref_chunks(k_ref, v_ref, slab, chunk, extra_fn=None, lo=0, hi=None):
    hi = k_ref.shape[1] if hi is None else hi
    out = []
    for c in range(lo, hi, chunk):
        rows = pl.ds(c, min(chunk, hi - c))
        out.append((lambda rows=rows: k_ref[slab, rows, :], lambda rows=rows: v_ref[slab, rows, :], extra_fn))
    return out


def _mla_finish(o_t):
    return o_t[:MLA_V] / o_t[MLA_V:MLA_V + 1]


def _na_queries(q2):
    low = lax.broadcasted_iota(jnp.int32, (1, LANES), 1) < NA_HEAD_DIM
    zero = jnp.zeros_like(q2)
    return jnp.concatenate([jnp.where(low, q2, zero), jnp.where(low, zero, q2)], axis=0)


def _na_finish(o_t, den):
    nq = o_t.shape[1] // 2
    o = o_t / den
    return jnp.concatenate([o[:NA_HEAD_DIM, :nq], o[NA_HEAD_DIM:, nq:]], axis=0)


def _attend_guarded(bound, streams_fn):
    return lax.cond(bound <= MAX_SCORE_BOUND,
                    lambda: _attend(streams_fn(False), False),
                    lambda: _attend(streams_fn(True), True))


def _mla_ctx_kernel(seq, bound_ref, q_ref, k_ref, v_ref, o_ref):
    spans = [(lo, lo + seq) for lo in range(0, k_ref.shape[1], seq)]
    res = _attend_guarded(bound_ref[0, 0], lambda _: [
        (q_ref[h, lo:hi, :], _ref_chunks(k_ref, v_ref, h, seq, lo=lo, hi=hi), False)
        for lo, hi in spans for h in range(MLA_HEADS)])
    for i, (lo, hi) in enumerate(spans):
        for hp in range(MLA_HEADS // 2):
            o_t = jnp.concatenate([_mla_finish(res[i * MLA_HEADS + h][0]) for h in (2 * hp, 2 * hp + 1)], axis=0)
            o_ref[lo:hi, hp * LANES:(hp + 1) * LANES] = o_t.T.astype(BF16)


def _na_ctx_kernel(seq, bound_ref, q_ref, k_ref, v_ref, o_ref):
    spans = [(lo, lo + seq) for lo in range(0, k_ref.shape[1], seq)]
    res = _attend_guarded(bound_ref[0, 0], lambda _: [
        (_na_queries(q_ref[s, lo:hi, :]), _ref_chunks(k_ref, v_ref, s, seq, lo=lo, hi=hi), True)
        for lo, hi in spans for s in range(NA_SLABS)])
    for i, (lo, hi) in enumerate(spans):
        for s in range(NA_SLABS):
            o_ref[lo:hi, s * LANES:(s + 1) * LANES] = _na_finish(*res[i * NA_SLABS + s]).T.astype(BF16)


def _ctx_attn(body, q, k, v, bound, seq, width, name):
    slabs, n, _ = q.shape
    rows = CTX_STEP_SEQS * seq
    spec = pl.BlockSpec((slabs, rows, LANES), lambda b: (0, b, 0))
    return pl.pallas_call(
        functools.partial(body, seq),
        out_shape=jax.ShapeDtypeStruct((n, width), BF16),
        grid=(n // rows,),
        in_specs=[pl.BlockSpec(memory_space=pltpu.SMEM), spec, spec, spec],
        out_specs=pl.BlockSpec((rows, width), lambda b: (b, 0)),
        compiler_params=_params(1), name=name,
    )(bound.reshape(1, 1), q, k, v)


def _mla_lat_kernel(bound_ref, q_ref, kl_ref, vl_ref, kc_ref, vc_ref, o_ref):
    batch, head0 = pl.program_id(0), pl.program_id(1) * MLA_STEP_HEADS
    heads = range(MLA_STEP_HEADS)
    bounds = [bound_ref[batch, head0 + h] for h in heads]

    def streams(running_max):
        size = MLA_ONLINE_CHUNK if running_max else MLA_KEY_CHUNK
        return [(q_ref[h], _ref_chunks(kl_ref, vl_ref, h, size) + _ref_chunks(kc_ref, vc_ref, h, size), False)
                for h in heads]

    res = _attend_guarded(functools.reduce(jnp.maximum, bounds), streams)
    o_ref[...] = jnp.concatenate([_mla_finish(o) for o, _ in res], axis=0).T.astype(BF16)


def _normed_len(gain, dim):
    return (dim ** 0.5) * jnp.max(jnp.abs(gain.astype(F32)))


def _mla_lat_attn(q, k, v, kc, vc, bound, seq, past):
    n = q.shape[1]
    tq = MLA_Q_TILE
    nq = seq // tq
    hs = MLA_STEP_HEADS
    return pl.pallas_call(
        _mla_lat_kernel,
        out_shape=jax.ShapeDtypeStruct((n, MLA_HEADS * MLA_V), BF16),
        grid=(n // seq, MLA_HEADS // hs, nq),
        in_specs=[pl.BlockSpec(memory_space=pltpu.SMEM),
                  pl.BlockSpec((hs, tq, LANES), lambda b, h, i: (h, b * nq + i, 0)),
                  pl.BlockSpec((hs, seq, LANES), lambda b, h, i: (h, b, 0)),
                  pl.BlockSpec((hs, seq, LANES), lambda b, h, i: (h, b, 0)),
                  pl.BlockSpec((hs, past, LANES), lambda b, h, i: (h, b, 0)),
                  pl.BlockSpec((hs, past, LANES), lambda b, h, i: (h, b, 0))],
        out_specs=pl.BlockSpec((tq, hs * MLA_V), lambda b, h, i: (b * nq + i, h)),
        compiler_params=_params(3), name="l0_mla_latent_attn",
    )(bound, q, k, v, kc, vc)


def _na_first_row(r, rows):
    return jnp.clip(r - NA_KH // 2, 0, rows - NA_KH)


def _na_lat_kernel(rows, bound_ref, q_ref, k_ref, v_ref, kc_ref, vc_ref, t2_ref, o_ref):
    nq = NA_TILE_ROWS * GRID_W
    nk = NA_WIN_ROWS * GRID_W
    n_off = 2 * NA_KH - 1
    r0 = pl.program_id(1) * NA_TILE_ROWS
    w0 = jnp.clip(r0 - NA_KH // 2, 0, rows - NA_WIN_ROWS)

    def block_index(j, t):
        key_row = w0 + j
        i = r0 + 2 * t
        first_u, first_l = _na_first_row(i, rows), _na_first_row(i + 1, rows)
        in_u = (key_row >= first_u) & (key_row < first_u + NA_KH)
        in_l = (key_row >= first_l) & (key_row < first_l + NA_KH)
        d = jnp.clip(key_row - i + NA_KH - 1, 0, n_off - 1)
        return jnp.where(in_u & in_l, d, jnp.where(in_u, n_off, jnp.where(in_l, n_off + 1, n_off + 2)))

    idx = [[block_index(j, t) for t in range(NA_TILE_ROWS // 2)] for j in range(NA_WIN_ROWS)]

    def local_bias(s):
        heads = [jnp.concatenate([jnp.concatenate([t2_ref[2 * s + hh, i] for i in row], axis=1) for row in idx], axis=0)
                 for hh in range(2)]
        return jnp.concatenate(heads, axis=1)

    batch = pl.program_id(0)
    bounds = [bound_ref[batch, h] for h in range(NA_HEADS)]

    res = _attend_guarded(
        functools.reduce(jnp.maximum, bounds),
        lambda _: [(_na_queries(q_ref[s]),
                    _ref_chunks(k_ref, v_ref, s, nk, functools.partial(local_bias, s)) + _ref_chunks(kc_ref, vc_ref, s, nk),
                    True) for s in range(NA_SLABS)])
    for s in range(NA_SLABS):
        o_ref[:, s * LANES:(s + 1) * LANES] = _na_finish(*res[s]).T.astype(BF16)


def _na_lat_attn(q, k, v, kc, vc, t2, bound, seq, past):
    slabs, n, _ = q.shape
    rows = seq // GRID_W
    tiles = rows // NA_TILE_ROWS
    nq = NA_TILE_ROWS * GRID_W
    nk = NA_WIN_ROWS * GRID_W

    def win_map(b, t):
        w0 = jnp.clip(t * NA_TILE_ROWS - NA_KH // 2, 0, rows - NA_WIN_ROWS)
        return (0, pl.multiple_of(b * seq + w0 * GRID_W, GRID_W), 0)

    win_spec = pl.BlockSpec((pl.Element(slabs), pl.Element(nk), pl.Element(LANES)), win_map)
    ctx_spec = pl.BlockSpec((slabs, past, LANES), lambda b, t: (0, b, 0))
    return pl.pallas_call(
        functools.partial(_na_lat_kernel, rows),
        out_shape=jax.ShapeDtypeStruct((n, slabs * LANES), BF16),
        grid=(n // seq, tiles),
        in_specs=[pl.BlockSpec(memory_space=pltpu.SMEM),
                  pl.BlockSpec((slabs, nq, LANES), lambda b, t: (0, b * tiles + t, 0)),
                  win_spec, win_spec, ctx_spec, ctx_spec,
                  _const_spec(t2.shape)],
        out_specs=pl.BlockSpec((nq, slabs * LANES), lambda b, t: (b * tiles + t, 0)),
        compiler_params=_params(2), name="l1_na_latent_attn",
    )(bound, q, k, v, kc, vc, t2)


def _pair_norm(x2, g2):
    low = lax.broadcasted_iota(jnp.int32, (1, LANES), 1) < NA_HEAD_DIM
    sq = x2 * x2
    ss_all = jnp.sum(sq, axis=-1, keepdims=True)
    ss_low = jnp.sum(jnp.where(low, sq, 0.0), axis=-1, keepdims=True)
    ss = jnp.where(low, ss_low, ss_all - ss_low)
    return x2 * lax.rsqrt(ss * (1.0 / NA_HEAD_DIM) + EPS) * g2


def _l1_pre_kernel(keep_f32, x_ref, mod_ref, nw_ref, win_ref, gq_ref, gk_ref, q_ref, k_ref, v_ref, *f32_refs):
    h = _modulate(x_ref[...], nw_ref[...], mod_ref[0:1, :], mod_ref[1:2, :]).astype(BF16)
    width = NA_SLABS * LANES
    q = _dot(h, win_ref[:, 0:width])
    k = _dot(h, win_ref[:, width:2 * width])
    v = _dot(h, win_ref[:, 2 * width:3 * width])
    gq = gq_ref[...]
    gk = gk_ref[...]
    for s in range(NA_SLABS):
        ls = slice(s * LANES, (s + 1) * LANES)
        q_ref[s] = _pair_norm(q[:, ls], gq).astype(BF16)
        kn = _pair_norm(k[:, ls], gk)
        k_ref[s] = kn.astype(BF16)
        v_ref[s] = v[:, ls].astype(BF16)
        if keep_f32:
            f32_refs[0][:, ls] = kn
    if keep_f32:
        f32_refs[1][...] = v


def _l1_pre(x, mod, tiles_per_group, keep_f32, w):
    n = x.shape[0]
    tm = TOKEN_TILE
    row = lambda i: (i, 0)
    width = NA_SLABS * LANES
    slab = jax.ShapeDtypeStruct((NA_SLABS, n, LANES), BF16)
    out_shape = [slab, slab, slab]
    out_specs = [_slab_spec(NA_SLABS, tm)] * 3
    if keep_f32:
        out_shape += [jax.ShapeDtypeStruct((n, width), F32)] * 2
        out_specs += [pl.BlockSpec((tm, width), row)] * 2
    return pl.pallas_call(
        functools.partial(_l1_pre_kernel, keep_f32),
        out_shape=tuple(out_shape), grid=(n // tm,),
        in_specs=[pl.BlockSpec((tm, D_MODEL), row),
                  pl.BlockSpec((None, SUBLANES, D_MODEL), lambda i: (i // tiles_per_group, 0, 0)),
                  _const_spec((1, D_MODEL)),
                  _const_spec((D_MODEL, 3 * width)),
                  _const_spec((1, LANES)),
                  _const_spec((1, LANES))],
        out_specs=tuple(out_specs),
        compiler_params=_params(1), name="l1_pre_ctx" if keep_f32 else "l1_pre_latent",
    )(x, mod, w["norm_mix"], w["w_in"], w["gq"], w["gk"])


def _post_kernel(pool_seq, x_ref, attn_ref, *rest):
    if pool_seq is None:
        mod_ref, nw_ref, wo_ref, w1_ref, w2_ref, o_ref = rest
        mixed = _dot(attn_ref[...], wo_ref[...])
    else:
        (first_refs, next_refs, wp_ref, ps_ref, mod_ref, nw_ref, wo_ref, w1_ref, w2_ref, o_ref,
         buf_ref, pooled_ref) = rest[0:3], rest[3:6], *rest[6:]
        step = pl.program_id(0)

        @pl.when(step == 0)
        def _():
            for g, group in enumerate(_pool_mixer(pool_seq, 0, *first_refs, wp_ref, ps_ref, buf_ref)):
                pooled_ref[:, g * POOL_GROUP:(g + 1) * POOL_GROUP] = group()

        mixed = _dot(jnp.concatenate([pooled_ref[...], attn_ref[...]], axis=-1), wo_ref[...])
        next_groups = _pool_mixer(pool_seq, jnp.minimum(step + 1, pl.num_programs(0) - 1), *next_refs,
                                  wp_ref, ps_ref, buf_ref)
    x1 = x_ref[...] + mod_ref[2:3, :] * mixed
    h = _modulate(x1, nw_ref[...], mod_ref[3:4, :], mod_ref[4:5, :]).astype(BF16)
    acc = jnp.zeros(x1.shape, F32)
    n_chunks = D_FF // FF_CHUNK
    for c in range(n_chunks):
        cs = pl.ds(c * FF_CHUNK, FF_CHUNK)
        u = jnp.square(jnp.maximum(_dot(h, w1_ref[:, cs]), 0.0)).astype(BF16)
        acc = acc + _dot(u, w2_ref[cs, :])
        if pool_seq is not None:
            for g in range(c * len(next_groups) // n_chunks, (c + 1) * len(next_groups) // n_chunks):
                pooled_ref[:, g * POOL_GROUP:(g + 1) * POOL_GROUP] = next_groups[g]()
    o_ref[...] = x1 + mod_ref[5:6, :] * acc


def _post(x, attn, mod, tiles_per_group, w, name, pool=None):
    n = x.shape[0]
    tm = TOKEN_TILE
    row = lambda i: (i, 0)
    in_specs = [pl.BlockSpec((tm, D_MODEL), row), pl.BlockSpec((tm, attn.shape[1]), row)]
    args = [x, attn]
    scratch = []
    pool_seq = None
    if pool is not None:
        a_in, pool_seq = pool
        hb = tm // POOL_HALO
        last = n // POOL_HALO - 1
        pieces = tm // min(tm, pool_seq)
        last_tile = n // tm - 1

        def tile_specs(tile_of):
            return [pl.BlockSpec((POOL_HALO, POOL_WIDTH), lambda i: (jnp.maximum(tile_of(i) * hb - 1, 0), 0)),
                    pl.BlockSpec((tm, POOL_WIDTH), lambda i: (tile_of(i), 0)),
                    pl.BlockSpec((POOL_HALO, POOL_WIDTH), lambda i: (jnp.minimum((tile_of(i) + 1) * hb, last), 0))]

        in_specs += tile_specs(lambda i: 0) + tile_specs(lambda i: jnp.minimum(i + 1, last_tile))
        in_specs += [_const_spec((len(POOL_WINDOWS), POOL_GROUP, POOL_GROUP)), _const_spec((1, POOL_WIDTH))]
        args += [a_in] * 6 + [w["w_pool"], w["pool_scale"]]
        scratch = [pltpu.VMEM((tm + 2 * POOL_HALO * pieces, POOL_WIDTH), F32), pltpu.VMEM((tm, POOL_WIDTH), BF16)]
    in_specs += [pl.BlockSpec((None, SUBLANES, D_MODEL), lambda i: (i // tiles_per_group, 0, 0)),
                 _const_spec((1, D_MODEL)),
                 _const_spec((D_MODEL, D_MODEL)),
                 _const_spec((D_MODEL, D_FF)),
                 _const_spec((D_FF, D_MODEL))]
    args += [mod, w["norm_mlp"], w["w_out"], w["w_mlp1"], w["w_mlp2"]]
    return pl.pallas_call(
        functools.partial(_post_kernel, pool_seq),
        out_shape=jax.ShapeDtypeStruct((n, D_MODEL), F32), grid=(n // tm,),
        in_specs=in_specs, out_specs=pl.BlockSpec((tm, D_MODEL), row),
        scratch_shapes=scratch, compiler_params=_params(1), name=name,
    )(*args)


def _rope_tables(seq):
    assert seq == GRID_W * GRID_W
    half = MLA_ROPE // 2
    inv_freq = jnp.power(ROPE_THETA, -jnp.arange(0, half, 2, dtype=F32) / half)
    ang = jnp.arange(GRID_W, dtype=F32)[:, None] * inv_freq[None, :]
    cos, sin = jnp.cos(ang), jnp.sin(ang)
    fill = lambda n, v=0.0: jnp.full((GRID_W, n), v, F32)
    spare = LANES - MLA_QK
    return jnp.stack([
        jnp.concatenate([fill(MLA_NOPE, 1.0), fill(half), cos, cos, fill(spare)], axis=1),
        jnp.concatenate([fill(MLA_NOPE), cos, cos, fill(half), fill(spare)], axis=1),
        jnp.concatenate([fill(MLA_NOPE), fill(half), -sin, sin, fill(spare)], axis=1),
        jnp.concatenate([fill(MLA_NOPE), -sin, sin, fill(half), fill(spare)], axis=1)])


def _rope_partner(tail):
    shape = tail.shape
    t = tail.reshape(shape[:-1] + (2, 2, MLA_ROPE // 4))
    return t[..., ::-1, :].reshape(shape)


def _na_bias_table(rel_bias, rows):
    qcol = jnp.arange(GRID_W)
    kcol = jnp.arange(GRID_W)
    col_start = jnp.clip(qcol - NA_KW // 2, 0, GRID_W - NA_KW)
    col_mask = (kcol[:, None] >= col_start[None, :]) & (kcol[:, None] < col_start[None, :] + NA_KW)
    offs = jnp.arange(2 * NA_KW - 1)
    onehot = (kcol[None, :, None] - qcol[None, None, :] + NA_KW - 1 == offs[:, None, None]).astype(F32)
    blocks = jnp.einsum('hdj,jkq->hdkq', rel_bias.astype(F32) * LOG2E, onehot, precision=lax.Precision.HIGHEST)
    blocks = jnp.where(col_mask[None, None], blocks, NEG_INF)
    neg = jnp.full_like(blocks[:, :1], NEG_INF)
    both = jnp.concatenate([blocks, jnp.concatenate([neg, blocks[:, :-1]], axis=1)], axis=-1)
    d_upper, d_lower = _na_partial_offsets(rows)
    upper_only = jnp.concatenate([blocks[:, d_upper:d_upper + 1], neg], axis=-1)
    lower_only = jnp.concatenate([neg, blocks[:, d_lower - 1:d_lower]], axis=-1)
    return jnp.concatenate([both, upper_only, lower_only, jnp.concatenate([neg, neg], axis=-1)], axis=1)


def _na_partial_offsets(rows):
    first = lambda r: min(max(r - NA_KH // 2, 0), rows - NA_KH)
    upper, lower = set(), set()
    for r0 in range(0, rows, NA_TILE_ROWS):
        w0 = min(max(r0 - NA_KH // 2, 0), rows - NA_WIN_ROWS)
        for j in range(NA_WIN_ROWS):
            for t in range(NA_TILE_ROWS // 2):
                key_row, i = w0 + j, r0 + 2 * t
                in_u = first(i) <= key_row < first(i) + NA_KH
                in_l = first(i + 1) <= key_row < first(i + 1) + NA_KH
                d = key_row - i + NA_KH - 1
                if in_u and in_l:
                    assert 1 <= d <= 2 * NA_KH - 2
                elif in_u:
                    upper.add(d)
                elif in_l:
                    lower.add(d)
    assert len(upper) == 1 and len(lower) == 1, (upper, lower)
    return upper.pop(), lower.pop()


def _prep_l0(w_in, q_lora_norm, kv_lora_norm, w_q_up, w_kv_up, mla_q_norm, mla_k_norm, w_pool, pool_scale,
             w_out, norm_mix, norm_mlp, w_mlp1, w_mlp2):
    lat_end = POOL_WIDTH + Q_LORA + KV_LORA
    w_in_pad = jnp.concatenate([w_in[:, :lat_end], jnp.zeros((D_MODEL, MLA_NOPE), w_in.dtype),
                                w_in[:, lat_end:], _rope_partner(w_in[:, lat_end:])], axis=1)
    pad = LANES - MLA_QK
    hk = MLA_HEADS * LANES
    wq = w_q_up.reshape(Q_LORA, MLA_HEADS, MLA_QK)
    wq = jnp.concatenate([wq, _rope_partner(wq[:, :, MLA_NOPE:])], axis=-1)
    kv = w_kv_up.reshape(KV_LORA, MLA_HEADS, MLA_NOPE + MLA_V)
    wk = jnp.pad(kv[:, :, :MLA_NOPE], ((0, 0), (0, 0), (0, LANES - MLA_NOPE)))
    wv = jnp.pad(kv[:, :, MLA_NOPE:], ((0, 0), (0, 0), (0, LANES - MLA_V)))
    gq = mla_q_norm * (MLA_QK ** -0.5 * LOG2E)
    slab_lane = jnp.arange(2 * LANES)
    e2 = (slab_lane[:, None] // LANES == slab_lane[None, :] // LANES) & (slab_lane[:, None] % LANES < MLA_QK)
    return dict(
        norm_mix=norm_mix.reshape(1, -1), norm_mlp=norm_mlp.reshape(1, -1),
        w_in=w_in_pad.astype(BF16),
        qln=q_lora_norm.reshape(1, -1), kvn=kv_lora_norm.reshape(1, -1),
        wq=wq.reshape(Q_LORA, hk).astype(BF16),
        gq_plain=jnp.pad(gq, (0, pad)).reshape(1, LANES),
        gq_rope=jnp.concatenate([gq, _rope_partner(gq[MLA_NOPE:])]).reshape(1, LANES),
        wk=wk.reshape(KV_LORA, hk).astype(BF16),
        gk_plain=jnp.pad(mla_k_norm, (0, pad)).reshape(1, LANES),
        gk_rope=jnp.concatenate([mla_k_norm, _rope_partner(mla_k_norm[MLA_NOPE:])]).reshape(1, LANES),
        wv=wv.reshape(KV_LORA, hk).astype(BF16),
        e2=e2.astype(BF16),
        w_pool=w_pool.astype(BF16), pool_scale=pool_scale.reshape(1, -1),
        w_out=w_out.astype(BF16), w_mlp1=w_mlp1.astype(BF16), w_mlp2=w_mlp2.astype(BF16))


def _split_mods(m):
    mods = m.reshape(SUBLANES, N_ADA, D_MODEL)[:3]
    mods = jnp.pad(mods, ((0, 0), (0, SUBLANES - N_ADA), (0, 0)))
    return mods[0:1], mods[1:3]


def _to_slabs(x, n):
    return x.reshape(n, -1, LANES).transpose(1, 0, 2).astype(BF16)


def kernel(x_prompt, x_sample, cache_l0_mla_ckv, cache_l0_mla_krope, cache_l1_na_k, cache_l1_na_v, c, c_ctx, w_ada_l0, b_ada_l0, norm_mix_l0, norm_mlp_l0, w_mlp1_l0, w_mlp2_l0, w_in_l0, q_lora_norm_l0, kv_lora_norm_l0, w_q_up_l0, w_kv_up_l0, mla_q_norm_l0, mla_k_norm_l0, w_pool_l0, pool_scale_l0, w_out_l0, w_ada_l1, b_ada_l1, norm_mix_l1, norm_mlp_l1, w_mlp1_l1, w_mlp2_l1, w_in_l1, na_q_norm_l1, na_k_norm_l1, rel_bias_l1, w_out_l1):
    batch, seq, _ = x_prompt.shape
    dec_batch, dec_seq, _ = x_sample.shape
    past = cache_l0_mla_ckv.shape[1]
    xp = x_prompt.reshape(batch * seq, D_MODEL)
    xs = x_sample.reshape(dec_batch * dec_seq, D_MODEL)
    tiles_p = (batch * seq) // TOKEN_TILE
    tiles_s = dec_seq // TOKEN_TILE

    cond8 = jnp.concatenate([c_ctx[None, :], c, jnp.zeros((SUBLANES - 1 - dec_batch, D_MODEL), F32)], axis=0)
    mods = _ada(cond8, (w_ada_l0, w_ada_l1), (b_ada_l0, b_ada_l1))
    mod0_p, mod0_s = _split_mods(mods[0])
    mod1_p, mod1_s = _split_mods(mods[1])

    w0 = _prep_l0(w_in_l0, q_lora_norm_l0, kv_lora_norm_l0, w_q_up_l0, w_kv_up_l0, mla_q_norm_l0, mla_k_norm_l0,
                  w_pool_l0, pool_scale_l0, w_out_l0, norm_mix_l0, norm_mlp_l0, w_mlp1_l0, w_mlp2_l0)
    a_p, q_p, k_p, v_p, ckv_p, kr_p = _l0_pre(xp, mod0_p, tiles_p, False, w0, None)
    a_s, q_s, k_s, v_s, _, _ = _l0_pre(xs, mod0_s, tiles_s, True, w0, _rope_tables(dec_seq))
    kr_cache = jnp.pad(cache_l0_mla_krope.reshape(dec_batch * past, MLA_ROPE), ((0, 0), (MLA_NOPE, LANES - MLA_QK)))
    k_c, v_c = _cache_kv(cache_l0_mla_ckv.reshape(dec_batch * past, KV_LORA), kr_cache, w0)
    mla_bound = _normed_len(mla_q_norm_l0 * (MLA_QK ** -0.5 * LOG2E), MLA_QK) * _normed_len(mla_k_norm_l0, MLA_QK)
    attn_p = _ctx_attn(_mla_ctx_kernel, q_p, k_p, v_p, mla_bound, seq, MLA_HEADS * MLA_V, "l0_mla_ctx_attn")
    attn_s = _mla_lat_attn(q_s, k_s, v_s, k_c, v_c, jnp.full((dec_batch, MLA_HEADS), mla_bound, F32), dec_seq, past)
    xp = _post(xp, attn_p, mod0_p, tiles_p, w0, "l0_post_ctx", pool=(a_p, seq))
    xs = _post(xs, attn_s, mod0_s, tiles_s, w0, "l0_post_latent", pool=(a_s, dec_seq))

    w1 = dict(norm_mix=norm_mix_l1.reshape(1, -1), norm_mlp=norm_mlp_l1.reshape(1, -1),
              w_in=w_in_l1.astype(BF16),
              gq=jnp.tile(na_q_norm_l1 * (NA_HEAD_DIM ** -0.5 * LOG2E), 2).reshape(1, LANES),
              gk=jnp.tile(na_k_norm_l1, 2).reshape(1, LANES),
              w_out=w_out_l1.astype(BF16), w_mlp1=w_mlp1_l1.astype(BF16), w_mlp2=w_mlp2_l1.astype(BF16))
    q1_p, k1_p, v1_p, k1_new, v1_new = _l1_pre(xp, mod1_p, tiles_p, True, w1)
    q1_s, k1_s, v1_s = _l1_pre(xs, mod1_s, tiles_s, False, w1)
    q_len, k_len = _normed_len(w1["gq"], NA_HEAD_DIM), _normed_len(w1["gk"], NA_HEAD_DIM)
    na_p = _ctx_attn(_na_ctx_kernel, q1_p, k1_p, v1_p, q_len * k_len, seq, NA_SLABS * LANES, "l1_na_ctx_attn")
    cache_len = jnp.sqrt(jnp.max(jnp.sum(jnp.square(cache_l1_na_k.astype(F32)), axis=-1), axis=1))
    na_bound = (q_len * jnp.maximum(k_len, cache_len)
                + jnp.max(jnp.abs(rel_bias_l1.astype(F32)) * LOG2E, axis=(1, 2))[None, :])
    na_s = _na_lat_attn(q1_s, k1_s, v1_s,
                        _to_slabs(cache_l1_na_k, dec_batch * past), _to_slabs(cache_l1_na_v, dec_batch * past),
                        _na_bias_table(rel_bias_l1, dec_seq // GRID_W), na_bound, dec_seq, past)
    xp = _post(xp, na_p, mod1_p, tiles_p, w1, "l1_post_ctx")
    xs = _post(xs, na_s, mod1_s, tiles_s, w1, "l1_post_latent")

    return (xp.reshape(batch, seq, D_MODEL), xs.reshape(dec_batch, dec_seq, D_MODEL),
            ckv_p.reshape(batch, seq, KV_LORA), kr_p.reshape(batch, seq, MLA_ROPE),
            k1_new.reshape(batch, seq, NA_HEADS, NA_HEAD_DIM), v1_new.reshape(batch, seq, NA_HEADS, NA_HEAD_DIM))
```

```python
import functools

import jax
import jax.numpy as jnp
from jax import lax
from jax.experimental import pallas as pl
from jax.experimental.pallas import tpu as pltpu

F32 = jnp.float32
BF16 = jnp.bfloat16

D_MODEL = 1024
D_FF = 4 * D_MODEL
DEPTH = 2
N_ADA = 6
EPS = 1e-6
NEG_INF = -1e30
ROPE_THETA = 10000.0
GRID_W = 64
POOL_WIDTH = 512
POOL_WINDOWS = (2, 4, 8, 16)
POOL_GROUP = 128
POOL_HALO = 8
MLA_HEADS = 8
MLA_NOPE = 64
MLA_ROPE = 32
MLA_QK = MLA_NOPE + MLA_ROPE
MLA_V = 64
Q_LORA = 256
KV_LORA = 128
NA_HEADS = 16
NA_HEAD_DIM = 64
NA_KH = 8
NA_KW = 16
LOG2E = 1.4426950408889634

LANES = 128
SUBLANES = 8
VMEM_LIMIT = 56 * 1024 * 1024

TOKEN_TILE = 512
MLA_Q_TILE = 512
MLA_STEP_HEADS = 2
MLA_KEY_CHUNK = 2048
MLA_ONLINE_CHUNK = 1024
SCORE_LOOKAHEAD = 2
MAX_SCORE_BOUND = 40.0
FF_CHUNK = 1024
CTX_STEP_SEQS = 2
NA_SLABS = NA_HEADS * NA_HEAD_DIM // LANES
NA_TILE_ROWS = 4
NA_WIN_ROWS = NA_TILE_ROWS + NA_KH - 1


def _params(n_axes):
    return pltpu.CompilerParams(dimension_semantics=("arbitrary",) * n_axes,
                                vmem_limit_bytes=VMEM_LIMIT)


def _const_spec(shape):
    zeros = (0,) * len(shape)
    return pl.BlockSpec(shape, lambda *_: zeros, pipeline_mode=pl.Buffered(1))


def _dot(a, b):
    return jnp.dot(a, b, preferred_element_type=F32)


def _dot_nt(a, b):
    return lax.dot_general(a, b, (((1,), (1,)), ((), ())), preferred_element_type=F32)


def _dot_tn(a, b):
    return lax.dot_general(a, b, (((0,), (0,)), ((), ())), preferred_element_type=F32)


def _rms(x, w, n=None):
    n = x.shape[-1] if n is None else n
    ss = jnp.sum(x * x, axis=-1, keepdims=True)
    return x * lax.rsqrt(ss * (1.0 / n) + EPS) * w


def _modulate(x, nw, shift, scale):
    return _rms(x, nw) * (1.0 + scale) + shift


def _ada_spec(mod, tiles_per_group):
    mods, first_row = mod
    return pl.BlockSpec((None, 1, mods.shape[-1]), lambda i: (first_row + i // tiles_per_group, 0, 0))


def _ada_row(mod_ref, k):
    return mod_ref[:, k * D_MODEL:(k + 1) * D_MODEL]


def _ada_kernel(cond_ref, w0_ref, w1_ref, b_ref, o_ref):
    c = cond_ref[...]
    s = (c * jax.nn.sigmoid(c)).astype(BF16)
    for layer, w_ref in enumerate((w0_ref, w1_ref)):
        @pl.when(pl.program_id(0) == layer)
        def _(w_ref=w_ref):
            o_ref[...] = _dot(s, w_ref[...].astype(BF16)) + b_ref[...]


def _ada(cond8, w_ada, b_ada):
    n_out = w_ada[0].shape[1]
    tn = D_MODEL
    nb = n_out // tn
    return pl.pallas_call(
        _ada_kernel,
        out_shape=jax.ShapeDtypeStruct((DEPTH, SUBLANES, n_out), F32),
        grid=(DEPTH, nb),
        in_specs=[pl.BlockSpec((SUBLANES, D_MODEL), lambda l, j: (0, 0)),
                  pl.BlockSpec((D_MODEL, tn), lambda l, j: (0, jnp.where(l == 0, j, nb - 1))),
                  pl.BlockSpec((D_MODEL, tn), lambda l, j: (0, jnp.where(l == 1, j, 0))),
                  pl.BlockSpec((None, 1, tn), lambda l, j: (l, 0, j))],
        out_specs=pl.BlockSpec((None, SUBLANES, tn), lambda l, j: (l, 0, j)),
        compiler_params=_params(2),
        name="ada",
    )(cond8, w_ada[0], w_ada[1], jnp.stack(b_ada).reshape(DEPTH, 1, n_out))


def _rope(x, c, s):
    return x * c + pltpu.roll(x, LANES - MLA_ROPE, 1) * s


PAIR = 2 * LANES


def _head_rinv(raw, e2_ref):
    ss = _dot((raw * raw).astype(BF16), e2_ref[...])
    return lax.rsqrt(ss * (1.0 / MLA_QK) + EPS)


def _mla_keys_values(ckv, kr_slab, wk_ref, gk, wv_ref, e2_ref, rope_tabs, k_ref, v_ref):
    c16 = ckv.astype(BF16)
    tail = kr_slab * gk
    if rope_tabs is not None:
        tail = _rope(tail, *rope_tabs)
    ones_lane = (lax.broadcasted_iota(jnp.int32, (1, LANES), 1) == MLA_V).astype(F32)
    for p in range(MLA_HEADS // 2):
        cols = pl.ds(p * PAIR, PAIR)
        kn = _dot(c16, wk_ref[:, cols])
        v = _dot(c16, wv_ref[:, cols])
        rinv = _head_rinv(kn + jnp.tile(kr_slab, (1, 2)), e2_ref)
        for i in range(2):
            hs = slice(i * LANES, (i + 1) * LANES)
            k_ref[2 * p + i] = (rinv[:, hs] * (kn[:, hs] * gk + tail)).astype(BF16)
            v_ref[2 * p + i] = (v[:, hs] + ones_lane).astype(BF16)


def _l0_pre_kernel(latent, x_ref, mod_ref, nw_ref, win_ref, qln_ref, kvn_ref, wq_ref, gq_ref,
                   wk_ref, gk_ref, wv_ref, e2_ref, *rest):
    if latent:
        rope_ref, a_ref, q_ref, k_ref, v_ref, ckv_ref, kr_ref = rest
        tile_rows = x_ref.shape[0] // GRID_W
        row0 = (pl.program_id(0) * tile_rows) % rope_ref.shape[1]
        expand = lambda col, row: jnp.concatenate(
            [rope_ref[col] + rope_ref[row, pl.ds(row0 + r, 1), :] for r in range(tile_rows)], axis=0)
        rope_tabs = (expand(0, 1), expand(2, 3))
    else:
        a_ref, q_ref, k_ref, v_ref, ckv_ref, kr_ref = rest
        rope_tabs = None
    h = _modulate(x_ref[...], nw_ref[...], _ada_row(mod_ref, 0), _ada_row(mod_ref, 1)).astype(BF16)
    proj = _dot(h, win_ref[...])
    a_ref[...] = proj[:, :POOL_WIDTH]
    q_lat = proj[:, POOL_WIDTH:POOL_WIDTH + Q_LORA]
    kv_lat = proj[:, POOL_WIDTH + Q_LORA:POOL_WIDTH + Q_LORA + KV_LORA]
    kr_slab = proj[:, POOL_WIDTH + Q_LORA + KV_LORA:]
    ql = _rms(q_lat, qln_ref[...]).astype(BF16)
    gq = gq_ref[...]
    for p in range(MLA_HEADS // 2):
        q = _dot(ql, wq_ref[:, pl.ds(p * PAIR, PAIR)])
        q_rinv = _head_rinv(q, e2_ref)
        for i in range(2):
            hs = slice(i * LANES, (i + 1) * LANES)
            y = q[:, hs] * q_rinv[:, hs] * gq
            if rope_tabs is not None:
                y = _rope(y, *rope_tabs)
            q_ref[2 * p + i] = y.astype(BF16)
    ckv = _rms(kv_lat, kvn_ref[...])
    ckv_ref[...] = ckv
    kr_ref[...] = kr_slab[:, MLA_NOPE:MLA_QK]
    _mla_keys_values(ckv, kr_slab, wk_ref, gk_ref[...], wv_ref, e2_ref, rope_tabs, k_ref, v_ref)


def _slab_spec(slabs, tm):
    return pl.BlockSpec((slabs, tm, LANES), lambda i: (0, i, 0))


def _l0_pre(x, mod, tiles_per_group, latent, w, rope_tabs):
    n = x.shape[0]
    tm = TOKEN_TILE
    row = lambda i: (i, 0)
    hk = MLA_HEADS * LANES
    in_specs = [pl.BlockSpec((tm, D_MODEL), row),
                _ada_spec(mod, tiles_per_group),
                _const_spec((1, D_MODEL)),
                _const_spec((D_MODEL, D_MODEL)),
                _const_spec((1, Q_LORA)),
                _const_spec((1, KV_LORA)),
                _const_spec((Q_LORA, hk)),
                _const_spec((1, LANES)),
                _const_spec((KV_LORA, hk)),
                _const_spec((1, LANES)),
                _const_spec((KV_LORA, hk)),
                _const_spec((2 * LANES, 2 * LANES))]
    gains = "rope" if latent else "plain"
    args = [x, mod[0], w["norm_mix"], w["w_in"], w["qln"], w["kvn"], w["wq"], w["gq_" + gains], w["wk"],
            w["gk_" + gains], w["wv"], w["e2"]]
    if latent:
        assert tm % GRID_W == 0
        in_specs.append(_const_spec(rope_tabs.shape))
        args.append(rope_tabs)
    slab = jax.ShapeDtypeStruct((MLA_HEADS, n, LANES), BF16)
    out_shape = (jax.ShapeDtypeStruct((n, POOL_WIDTH), F32), slab, slab, slab,
                 jax.ShapeDtypeStruct((n, KV_LORA), F32),
                 jax.ShapeDtypeStruct((n, MLA_ROPE), F32))
    out_specs = (pl.BlockSpec((tm, POOL_WIDTH), row),
                 _slab_spec(MLA_HEADS, tm), _slab_spec(MLA_HEADS, tm), _slab_spec(MLA_HEADS, tm),
                 pl.BlockSpec((tm, KV_LORA), row), pl.BlockSpec((tm, MLA_ROPE), row))
    return pl.pallas_call(
        functools.partial(_l0_pre_kernel, latent),
        out_shape=out_shape, grid=(n // tm,), in_specs=in_specs, out_specs=out_specs,
        compiler_params=_params(1), name="l0_pre_latent" if latent else "l0_pre_ctx",
    )(*args)


def _cache_kv_kernel(ckv_ref, kr_ref, wk_ref, gk_ref, wv_ref, e2_ref, k_ref, v_ref):
    _mla_keys_values(ckv_ref[...], kr_ref[...], wk_ref, gk_ref[...], wv_ref, e2_ref, None, k_ref, v_ref)


def _cache_kv(ckv, kr_slab, w):
    n = ckv.shape[0]
    tm = TOKEN_TILE
    row = lambda i: (i, 0)
    hk = MLA_HEADS * LANES
    slab = jax.ShapeDtypeStruct((MLA_HEADS, n, LANES), BF16)
    return pl.pallas_call(
        _cache_kv_kernel,
        out_shape=(slab, slab),
        grid=(n // tm,),
        in_specs=[pl.BlockSpec((tm, KV_LORA), row), pl.BlockSpec((tm, LANES), row),
                  _const_spec((KV_LORA, hk)), _const_spec((1, LANES)), _const_spec((KV_LORA, hk)),
                  _const_spec((2 * LANES, 2 * LANES))],
        out_specs=(_slab_spec(MLA_HEADS, tm), _slab_spec(MLA_HEADS, tm)),
        compiler_params=_params(1), name="l0_cache_kv",
    )(ckv, kr_slab, w["wk"], w["gk_plain"], w["wv"], w["e2"])


def _pool_mixer(seq_len, tile, prev_ref, u_ref, next_ref, wp_ref, ps_ref, buf_ref):
    tm = u_ref.shape[0]
    piece = min(tm, seq_len)
    n_pieces = tm // piece
    stride = piece + 2 * POOL_HALO
    starts = []
    for j in range(n_pieces):
        pos0 = (tile * tm + j * piece) % seq_len
        prev = prev_ref[...] if j == 0 else u_ref[j * piece - POOL_HALO:j * piece, :]
        nxt = next_ref[...] if j == n_pieces - 1 else u_ref[(j + 1) * piece:(j + 1) * piece + POOL_HALO, :]
        base = j * stride
        buf_ref[base:base + POOL_HALO, :] = jnp.where(pos0 != 0, prev, 0.0)
        buf_ref[base + POOL_HALO:base + POOL_HALO + piece, :] = u_ref[j * piece:(j + 1) * piece, :]
        buf_ref[base + POOL_HALO + piece:base + stride, :] = jnp.where(pos0 + piece != seq_len, nxt, 0.0)
        starts.append(pos0)
    def group(g):
        win = POOL_WINDOWS[g]
        lanes = pl.ds(g * POOL_GROUP, POOL_GROUP)
        pooled = []
        for j in range(n_pieces):
            centre = j * stride + POOL_HALO
            lo = starts[j] + lax.broadcasted_iota(jnp.int32, (piece, 1), 0) - win // 2
            cnt = (jnp.minimum(lo + win, seq_len) - jnp.maximum(lo, 0)).astype(F32)
            acc = buf_ref[pl.ds(centre - win // 2, piece), lanes]
            for d in range(1 - win // 2, win - win // 2):
                acc = acc + buf_ref[pl.ds(centre + d, piece), lanes]
            pooled.append(acc / cnt - buf_ref[pl.ds(centre, piece), lanes])
        pooled = pooled[0] if n_pieces == 1 else jnp.concatenate(pooled, axis=0)
        return (_dot(pooled.astype(BF16), wp_ref[g]) * ps_ref[:, lanes]).astype(BF16)

    return [functools.partial(group, g) for g in range(len(POOL_WINDOWS))]


def _attend(streams, running_max):
    items = [(si, ch) for si, (_, chunks, _) in enumerate(streams) for ch in chunks]

    def score(item):
        si, (k_fn, _, extra_fn) = item
        s = _dot_nt(k_fn(), streams[si][0])
        return s if extra_fn is None else s + extra_fn()

    state = [None] * len(streams)
    pending = [score(item) for item in items[:SCORE_LOOKAHEAD]]
    for i, (si, (_, v_fn, _)) in enumerate(items):
        s = pending.pop(0)
        if i + SCORE_LOOKAHEAD < len(items):
            pending.append(score(items[i + SCORE_LOOKAHEAD]))
        want_den = streams[si][2]
        m_new, alpha = None, 1.0
        if state[si] is not None:
            m, acc, den = state[si]
        if running_max:
            m_new = jnp.max(s, axis=0, keepdims=True)
            if state[si] is not None:
                m_new = jnp.maximum(m, m_new)
                alpha = jnp.exp2(m - m_new)
            s = s - m_new
        p = jnp.exp2(s)
        pv = _dot_tn(v_fn(), p.astype(BF16))
        psum = jnp.sum(p, axis=0, keepdims=True) if want_den else None
        if state[si] is None:
            state[si] = (m_new, pv, psum)
        elif running_max:
            state[si] = (m_new, acc * alpha + pv, den * alpha + psum if want_den else None)
        else:
            state[si] = (None, acc + pv, den + psum if want_den else None)
    return [(acc, den) for _, acc, den in state]


def _ref_chunks(k_ref, v_ref, slab, chunk, extra_fn=None, lo=0, hi=None):
    hi = k_ref.shape[1] if hi is None else hi
    out = []
    for c in range(lo, hi, chunk):
        rows = pl.ds(c, min(chunk, hi - c))
        out.append((lambda rows=rows: k_ref[slab, rows, :], lambda rows=rows: v_ref[slab, rows, :], extra_fn))
    return out


def _mla_finish(o_t):
    return o_t[:MLA_V] / o_t[MLA_V:MLA_V + 1]


def _na_queries(q2):
    low = lax.broadcasted_iota(jnp.int32, (1, LANES), 1) < NA_HEAD_DIM
    zero = jnp.zeros_like(q2)
    return jnp.concatenate([jnp.where(low, q2, zero), jnp.where(low, zero, q2)], axis=0)


def _na_finish(o_t, den):
    nq = o_t.shape[1] // 2
    o = o_t / den
    return jnp.concatenate([o[:NA_HEAD_DIM, :nq], o[NA_HEAD_DIM:, nq:]], axis=0)


def _attend_guarded(bound, streams_fn):
    return lax.cond(bound <= MAX_SCORE_BOUND,
                    lambda: _attend(streams_fn(False), False),
                    lambda: _attend(streams_fn(True), True))


def _mla_ctx_kernel(seq, bound_ref, q_ref, k_ref, v_ref, o_ref):
    spans = [(lo, lo + seq) for lo in range(0, k_ref.shape[1], seq)]
    res = _attend_guarded(bound_ref[0, 0], lambda _: [
        (q_ref[h, lo:hi, :], _ref_chunks(k_ref, v_ref, h, seq, lo=lo, hi=hi), False)
        for lo, hi in spans for h in range(MLA_HEADS)])
    for i, (lo, hi) in enumerate(spans):
        for hp in range(MLA_HEADS // 2):
            o_t = jnp.concatenate([_mla_finish(res[i * MLA_HEADS + h][0]) for h in (2 * hp, 2 * hp + 1)], axis=0)
            o_ref[lo:hi, hp * LANES:(hp + 1) * LANES] = o_t.T.astype(BF16)


def _na_ctx_kernel(seq, bound_ref, q_ref, k_ref, v_ref, o_ref):
    spans = [(lo, lo + seq) for lo in range(0, k_ref.shape[1], seq)]
    res = _attend_guarded(bound_ref[0, 0], lambda _: [
        (_na_queries(q_ref[s, lo:hi, :]), _ref_chunks(k_ref, v_ref, s, seq, lo=lo, hi=hi), True)
        for lo, hi in spans for s in range(NA_SLABS)])
    for i, (lo, hi) in enumerate(spans):
        for s in range(NA_SLABS):
            o_ref[lo:hi, s * LANES:(s + 1) * LANES] = _na_finish(*res[i * NA_SLABS + s]).T.astype(BF16)


def _ctx_attn(body, q, k, v, bound, seq, width, name):
    slabs, n, _ = q.shape
    rows = CTX_STEP_SEQS * seq
    spec = pl.BlockSpec((slabs, rows, LANES), lambda b: (0, b, 0))
    return pl.pallas_call(
        functools.partial(body, seq),
        out_shape=jax.ShapeDtypeStruct((n, width), BF16),
        grid=(n // rows,),
        in_specs=[pl.BlockSpec(memory_space=pltpu.SMEM), spec, spec, spec],
        out_specs=pl.BlockSpec((rows, width), lambda b: (b, 0)),
        compiler_params=_params(1), name=name,
    )(bound.reshape(1, 1), q, k, v)


def _mla_lat_kernel(bound_ref, q_ref, kl_ref, vl_ref, kc_ref, vc_ref, o_ref):
    batch, head0 = pl.program_id(0), pl.program_id(1) * MLA_STEP_HEADS
    heads = range(MLA_STEP_HEADS)
    bounds = [bound_ref[batch, head0 + h] for h in heads]

    def streams(running_max):
        size = MLA_ONLINE_CHUNK if running_max else MLA_KEY_CHUNK
        return [(q_ref[h], _ref_chunks(kl_ref, vl_ref, h, size) + _ref_chunks(kc_ref, vc_ref, h, size), False)
                for h in heads]

    res = _attend_guarded(functools.reduce(jnp.maximum, bounds), streams)
    o_ref[...] = jnp.concatenate([_mla_finish(o) for o, _ in res], axis=0).T.astype(BF16)


def _normed_len(gain, dim):
    return (dim ** 0.5) * jnp.max(jnp.abs(gain.astype(F32)))


def _mla_lat_attn(q, k, v, kc, vc, bound, seq, past):
    n = q.shape[1]
    tq = MLA_Q_TILE
    nq = seq // tq
    hs = MLA_STEP_HEADS
    return pl.pallas_call(
        _mla_lat_kernel,
        out_shape=jax.ShapeDtypeStruct((n, MLA_HEADS * MLA_V), BF16),
        grid=(n // seq, MLA_HEADS // hs, nq),
        in_specs=[pl.BlockSpec(memory_space=pltpu.SMEM),
                  pl.BlockSpec((hs, tq, LANES), lambda b, h, i: (h, b * nq + i, 0)),
                  pl.BlockSpec((hs, seq, LANES), lambda b, h, i: (h, b, 0)),
                  pl.BlockSpec((hs, seq, LANES), lambda b, h, i: (h, b, 0)),
                  pl.BlockSpec((hs, past, LANES), lambda b, h, i: (h, b, 0)),
                  pl.BlockSpec((hs, past, LANES), lambda b, h, i: (h, b, 0))],
        out_specs=pl.BlockSpec((tq, hs * MLA_V), lambda b, h, i: (b * nq + i, h)),
        compiler_params=_params(3), name="l0_mla_latent_attn",
    )(bound, q, k, v, kc, vc)


def _na_first_row(r, rows):
    return jnp.clip(r - NA_KH // 2, 0, rows - NA_KH)


def _na_lat_kernel(rows, bound_ref, q_ref, k_ref, v_ref, kc_ref, vc_ref, t2_ref, o_ref):
    nq = NA_TILE_ROWS * GRID_W
    nk = NA_WIN_ROWS * GRID_W
    n_off = 2 * NA_KH - 1
    r0 = pl.program_id(1) * NA_TILE_ROWS
    w0 = jnp.clip(r0 - NA_KH // 2, 0, rows - NA_WIN_ROWS)

    def block_index(j, t):
        key_row = w0 + j
        i = r0 + 2 * t
        first_u, first_l = _na_first_row(i, rows), _na_first_row(i + 1, rows)
        in_u = (key_row >= first_u) & (key_row < first_u + NA_KH)
        in_l = (key_row >= first_l) & (key_row < first_l + NA_KH)
        d = jnp.clip(key_row - i + NA_KH - 1, 0, n_off - 1)
        return jnp.where(in_u & in_l, d, jnp.where(in_u, n_off, jnp.where(in_l, n_off + 1, n_off + 2)))

    idx = [[block_index(j, t) for t in range(NA_TILE_ROWS // 2)] for j in range(NA_WIN_ROWS)]

    def local_bias(s):
        heads = [jnp.concatenate([jnp.concatenate([t2_ref[2 * s + hh, i] for i in row], axis=1) for row in idx], axis=0)
                 for hh in range(2)]
        return jnp.concatenate(heads, axis=1)

    batch = pl.program_id(0)
    bounds = [bound_ref[batch, h] for h in range(NA_HEADS)]

    def cache_chunk(s):
        lanes = pl.ds(s * LANES, LANES)
        return [(lambda: kc_ref[:, lanes].astype(BF16), lambda: vc_ref[:, lanes].astype(BF16), None)]

    res = _attend_guarded(
        functools.reduce(jnp.maximum, bounds),
        lambda _: [(_na_queries(q_ref[s]),
                    _ref_chunks(k_ref, v_ref, s, nk, functools.partial(local_bias, s)) + cache_chunk(s),
                    True) for s in range(NA_SLABS)])
    for s in range(NA_SLABS):
        o_ref[:, s * LANES:(s + 1) * LANES] = _na_finish(*res[s]).T.astype(BF16)


def _na_lat_attn(q, k, v, kc, vc, t2, bound, seq, past):
    slabs, n, _ = q.shape
    rows = seq // GRID_W
    tiles = rows // NA_TILE_ROWS
    nq = NA_TILE_ROWS * GRID_W
    nk = NA_WIN_ROWS * GRID_W

    def win_map(b, t):
        w0 = jnp.clip(t * NA_TILE_ROWS - NA_KH // 2, 0, rows - NA_WIN_ROWS)
        return (0, pl.multiple_of(b * seq + w0 * GRID_W, GRID_W), 0)

    win_spec = pl.BlockSpec((pl.Element(slabs), pl.Element(nk), pl.Element(LANES)), win_map)
    ctx_spec = pl.BlockSpec((past, slabs * LANES), lambda b, t: (b, 0))
    return pl.pallas_call(
        functools.partial(_na_lat_kernel, rows),
        out_shape=jax.ShapeDtypeStruct((n, slabs * LANES), BF16),
        grid=(n // seq, tiles),
        in_specs=[pl.BlockSpec(memory_space=pltpu.SMEM),
                  pl.BlockSpec((slabs, nq, LANES), lambda b, t: (0, b * tiles + t, 0)),
                  win_spec, win_spec, ctx_spec, ctx_spec,
                  _const_spec(t2.shape)],
        out_specs=pl.BlockSpec((nq, slabs * LANES), lambda b, t: (b * tiles + t, 0)),
        compiler_params=_params(2), name="l1_na_latent_attn",
    )(bound, q, k, v, kc, vc, t2)


def _pair_norm(x2, g2):
    low = lax.broadcasted_iota(jnp.int32, (1, LANES), 1) < NA_HEAD_DIM
    sq = x2 * x2
    ss_all = jnp.sum(sq, axis=-1, keepdims=True)
    ss_low = jnp.sum(jnp.where(low, sq, 0.0), axis=-1, keepdims=True)
    ss = jnp.where(low, ss_low, ss_all - ss_low)
    return x2 * lax.rsqrt(ss * (1.0 / NA_HEAD_DIM) + EPS) * g2


def _l1_pre_kernel(keep_f32, x_ref, mod_ref, nw_ref, win_ref, gq_ref, gk_ref, q_ref, k_ref, v_ref, *f32_refs):
    h = _modulate(x_ref[...], nw_ref[...], _ada_row(mod_ref, 0), _ada_row(mod_ref, 1)).astype(BF16)
    width = NA_SLABS * LANES
    q = _dot(h, win_ref[:, 0:width])
    k = _dot(h, win_ref[:, width:2 * width])
    v = _dot(h, win_ref[:, 2 * width:3 * width])
    gq = gq_ref[...]
    gk = gk_ref[...]
    for s in range(NA_SLABS):
        ls = slice(s * LANES, (s + 1) * LANES)
        q_ref[s] = _pair_norm(q[:, ls], gq).astype(BF16)
        kn = _pair_norm(k[:, ls], gk)
        k_ref[s] = kn.astype(BF16)
        v_ref[s] = v[:, ls].astype(BF16)
        if keep_f32:
            f32_refs[0][:, ls] = kn
    if keep_f32:
        f32_refs[1][...] = v


def _l1_pre(x, mod, tiles_per_group, keep_f32, w):
    n = x.shape[0]
    tm = TOKEN_TILE
    row = lambda i: (i, 0)
    width = NA_SLABS * LANES
    slab = jax.ShapeDtypeStruct((NA_SLABS, n, LANES), BF16)
    out_shape = [slab, slab, slab]
    out_specs = [_slab_spec(NA_SLABS, tm)] * 3
    if keep_f32:
        out_shape += [jax.ShapeDtypeStruct((n, width), F32)] * 2
        out_specs += [pl.BlockSpec((tm, width), row)] * 2
    return pl.pallas_call(
        functools.partial(_l1_pre_kernel, keep_f32),
        out_shape=tuple(out_shape), grid=(n // tm,),
        in_specs=[pl.BlockSpec((tm, D_MODEL), row),
                  _ada_spec(mod, tiles_per_group),
                  _const_spec((1, D_MODEL)),
                  _const_spec((D_MODEL, 3 * width)),
                  _const_spec((1, LANES)),
                  _const_spec((1, LANES))],
        out_specs=tuple(out_specs),
        compiler_params=_params(1), name="l1_pre_ctx" if keep_f32 else "l1_pre_latent",
    )(x, mod[0], w["norm_mix"], w["w_in"], w["gq"], w["gk"])


def _post_kernel(pool_seq, x_ref, attn_ref, *rest):
    if pool_seq is None:
        mod_ref, nw_ref, wo_ref, w1_ref, w2_ref, o_ref = rest
        mixed = _dot(attn_ref[...], wo_ref[...])
    else:
        (first_refs, next_refs, wp_ref, ps_ref, mod_ref, nw_ref, wo_ref, w1_ref, w2_ref, o_ref,
         buf_ref, pooled_ref) = rest[0:3], rest[3:6], *rest[6:]
        step = pl.program_id(0)

        @pl.when(step == 0)
        def _():
            for g, group in enumerate(_pool_mixer(pool_seq, 0, *first_refs, wp_ref, ps_ref, buf_ref)):
                pooled_ref[:, g * POOL_GROUP:(g + 1) * POOL_GROUP] = group()

        mixed = _dot(jnp.concatenate([pooled_ref[...], attn_ref[...]], axis=-1), wo_ref[...])
        next_groups = _pool_mixer(pool_seq, jnp.minimum(step + 1, pl.num_programs(0) - 1), *next_refs,
                                  wp_ref, ps_ref, buf_ref)
    x1 = x_ref[...] + _ada_row(mod_ref, 2) * mixed
    h = _modulate(x1, nw_ref[...], _ada_row(mod_ref, 3), _ada_row(mod_ref, 4)).astype(BF16)
    acc = jnp.zeros(x1.shape, F32)
    n_chunks = D_FF // FF_CHUNK
    for c in range(n_chunks):
        cs = pl.ds(c * FF_CHUNK, FF_CHUNK)
        u = jnp.square(jnp.maximum(_dot(h, w1_ref[:, cs]), 0.0)).astype(BF16)
        acc = acc + _dot(u, w2_ref[cs, :])
        if pool_seq is not None:
            for g in range(c * len(next_groups) // n_chunks, (c + 1) * len(next_groups) // n_chunks):
                pooled_ref[:, g * POOL_GROUP:(g + 1) * POOL_GROUP] = next_groups[g]()
    o_ref[...] = x1 + _ada_row(mod_ref, 5) * acc


def _post(x, attn, mod, tiles_per_group, w, name, pool=None):
    n = x.shape[0]
    tm = TOKEN_TILE
    row = lambda i: (i, 0)
    in_specs = [pl.BlockSpec((tm, D_MODEL), row), pl.BlockSpec((tm, attn.shape[1]), row)]
    args = [x, attn]
    scratch = []
    pool_seq = None
    if pool is not None:
        a_in, pool_seq = pool
        hb = tm // POOL_HALO
        last = n // POOL_HALO - 1
        pieces = tm // min(tm, pool_seq)
        last_tile = n // tm - 1

        def tile_specs(tile_of):
            return [pl.BlockSpec((POOL_HALO, POOL_WIDTH), lambda i: (jnp.maximum(tile_of(i) * hb - 1, 0), 0)),
                    pl.BlockSpec((tm, POOL_WIDTH), lambda i: (tile_of(i), 0)),
                    pl.BlockSpec((POOL_HALO, POOL_WIDTH), lambda i: (jnp.minimum((tile_of(i) + 1) * hb, last), 0))]

        in_specs += tile_specs(lambda i: 0) + tile_specs(lambda i: jnp.minimum(i + 1, last_tile))
        in_specs += [_const_spec((len(POOL_WINDOWS), POOL_GROUP, POOL_GROUP)), _const_spec((1, POOL_WIDTH))]
        args += [a_in] * 6 + [w["w_pool"], w["pool_scale"]]
        scratch = [pltpu.VMEM((tm + 2 * POOL_HALO * pieces, POOL_WIDTH), F32), pltpu.VMEM((tm, POOL_WIDTH), BF16)]
    in_specs += [_ada_spec(mod, tiles_per_group),
                 _const_spec((1, D_MODEL)),
                 _const_spec((D_MODEL, D_MODEL)),
                 _const_spec((D_MODEL, D_FF)),
                 _const_spec((D_FF, D_MODEL))]
    args += [mod[0], w["norm_mlp"], w["w_out"], w["w_mlp1"], w["w_mlp2"]]
    return pl.pallas_call(
        functools.partial(_post_kernel, pool_seq),
        out_shape=jax.ShapeDtypeStruct((n, D_MODEL), F32), grid=(n // tm,),
        in_specs=in_specs, out_specs=pl.BlockSpec((tm, D_MODEL), row),
        scratch_shapes=scratch, compiler_params=_params(1), name=name,
    )(*args)


def _rope_tables(seq):
    assert seq == GRID_W * GRID_W
    half = MLA_ROPE // 2
    inv_freq = jnp.power(ROPE_THETA, -jnp.arange(0, half, 2, dtype=F32) / half)
    ang = jnp.arange(GRID_W, dtype=F32)[:, None] * inv_freq[None, :]
    cos, sin = jnp.cos(ang), jnp.sin(ang)
    fill = lambda n, v=0.0: jnp.full((GRID_W, n), v, F32)
    spare = LANES - MLA_QK
    return jnp.stack([
        jnp.concatenate([fill(MLA_NOPE, 1.0), fill(half), cos, cos, fill(spare)], axis=1),
        jnp.concatenate([fill(MLA_NOPE), cos, cos, fill(half), fill(spare)], axis=1),
        jnp.concatenate([fill(MLA_NOPE), fill(half), -sin, sin, fill(spare)], axis=1),
        jnp.concatenate([fill(MLA_NOPE), -sin, sin, fill(half), fill(spare)], axis=1)])


def _rope_partner(tail):
    shape = tail.shape
    t = tail.reshape(shape[:-1] + (2, 2, MLA_ROPE // 4))
    return t[..., ::-1, :].reshape(shape)


def _na_bias_table(rel_bias, rows):
    qcol = jnp.arange(GRID_W)
    kcol = jnp.arange(GRID_W)
    col_start = jnp.clip(qcol - NA_KW // 2, 0, GRID_W - NA_KW)
    col_mask = (kcol[:, None] >= col_start[None, :]) & (kcol[:, None] < col_start[None, :] + NA_KW)
    offs = jnp.arange(2 * NA_KW - 1)
    onehot = (kcol[None, :, None] - qcol[None, None, :] + NA_KW - 1 == offs[:, None, None]).astype(F32)
    blocks = jnp.einsum('hdj,jkq->hdkq', rel_bias.astype(F32) * LOG2E, onehot, precision=lax.Precision.HIGHEST)
    blocks = jnp.where(col_mask[None, None], blocks, NEG_INF)
    neg = jnp.full_like(blocks[:, :1], NEG_INF)
    both = jnp.concatenate([blocks, jnp.concatenate([neg, blocks[:, :-1]], axis=1)], axis=-1)
    d_upper, d_lower = _na_partial_offsets(rows)
    upper_only = jnp.concatenate([blocks[:, d_upper:d_upper + 1], neg], axis=-1)
    lower_only = jnp.concatenate([neg, blocks[:, d_lower - 1:d_lower]], axis=-1)
    return jnp.concatenate([both, upper_only, lower_only, jnp.concatenate([neg, neg], axis=-1)], axis=1)


def _na_partial_offsets(rows):
    first = lambda r: min(max(r - NA_KH // 2, 0), rows - NA_KH)
    upper, lower = set(), set()
    for r0 in range(0, rows, NA_TILE_ROWS):
        w0 = min(max(r0 - NA_KH // 2, 0), rows - NA_WIN_ROWS)
        for j in range(NA_WIN_ROWS):
            for t in range(NA_TILE_ROWS // 2):
                key_row, i = w0 + j, r0 + 2 * t
                in_u = first(i) <= key_row < first(i) + NA_KH
                in_l = first(i + 1) <= key_row < first(i + 1) + NA_KH
                d = key_row - i + NA_KH - 1
                if in_u and in_l:
                    assert 1 <= d <= 2 * NA_KH - 2
                elif in_u:
                    upper.add(d)
                elif in_l:
                    lower.add(d)
    assert len(upper) == 1 and len(lower) == 1, (upper, lower)
    return upper.pop(), lower.pop()


def _prep_l0(w_in, q_lora_norm, kv_lora_norm, w_q_up, w_kv_up, mla_q_norm, mla_k_norm, w_pool, pool_scale,
             w_out, norm_mix, norm_mlp, w_mlp1, w_mlp2):
    lat_end = POOL_WIDTH + Q_LORA + KV_LORA
    w_in_pad = jnp.concatenate([w_in[:, :lat_end], jnp.zeros((D_MODEL, MLA_NOPE), w_in.dtype),
                                w_in[:, lat_end:], _rope_partner(w_in[:, lat_end:])], axis=1)
    pad = LANES - MLA_QK
    hk = MLA_HEADS * LANES
    wq = w_q_up.reshape(Q_LORA, MLA_HEADS, MLA_QK)
    wq = jnp.concatenate([wq, _rope_partner(wq[:, :, MLA_NOPE:])], axis=-1)
    kv = w_kv_up.reshape(KV_LORA, MLA_HEADS, MLA_NOPE + MLA_V)
    wk = jnp.pad(kv[:, :, :MLA_NOPE], ((0, 0), (0, 0), (0, LANES - MLA_NOPE)))
    wv = jnp.pad(kv[:, :, MLA_NOPE:], ((0, 0), (0, 0), (0, LANES - MLA_V)))
    gq = mla_q_norm * (MLA_QK ** -0.5 * LOG2E)
    slab_lane = jnp.arange(2 * LANES)
    e2 = (slab_lane[:, None] // LANES == slab_lane[None, :] // LANES) & (slab_lane[:, None] % LANES < MLA_QK)
    return dict(
        norm_mix=norm_mix.reshape(1, -1), norm_mlp=norm_mlp.reshape(1, -1),
        w_in=w_in_pad.astype(BF16),
        qln=q_lora_norm.reshape(1, -1), kvn=kv_lora_norm.reshape(1, -1),
        wq=wq.reshape(Q_LORA, hk).astype(BF16),
        gq_plain=jnp.pad(gq, (0, pad)).reshape(1, LANES),
        gq_rope=jnp.concatenate([gq, _rope_partner(gq[MLA_NOPE:])]).reshape(1, LANES),
        wk=wk.reshape(KV_LORA, hk).astype(BF16),
        gk_plain=jnp.pad(mla_k_norm, (0, pad)).reshape(1, LANES),
        gk_rope=jnp.concatenate([mla_k_norm, _rope_partner(mla_k_norm[MLA_NOPE:])]).reshape(1, LANES),
        wv=wv.reshape(KV_LORA, hk).astype(BF16),
        e2=e2.astype(BF16),
        w_pool=w_pool.astype(BF16), pool_scale=pool_scale.reshape(1, -1),
        w_out=w_out.astype(BF16), w_mlp1=w_mlp1.astype(BF16), w_mlp2=w_mlp2.astype(BF16))


def kernel(x_prompt, x_sample, cache_l0_mla_ckv, cache_l0_mla_krope, cache_l1_na_k, cache_l1_na_v, c, c_ctx, w_ada_l0, b_ada_l0, norm_mix_l0, norm_mlp_l0, w_mlp1_l0, w_mlp2_l0, w_in_l0, q_lora_norm_l0, kv_lora_norm_l0, w_q_up_l0, w_kv_up_l0, mla_q_norm_l0, mla_k_norm_l0, w_pool_l0, pool_scale_l0, w_out_l0, w_ada_l1, b_ada_l1, norm_mix_l1, norm_mlp_l1, w_mlp1_l1, w_mlp2_l1, w_in_l1, na_q_norm_l1, na_k_norm_l1, rel_bias_l1, w_out_l1):
    batch, seq, _ = x_prompt.shape
    dec_batch, dec_seq, _ = x_sample.shape
    past = cache_l0_mla_ckv.shape[1]
    xp = x_prompt.reshape(batch * seq, D_MODEL)
    xs = x_sample.reshape(dec_batch * dec_seq, D_MODEL)
    tiles_p = (batch * seq) // TOKEN_TILE
    tiles_s = dec_seq // TOKEN_TILE

    cond8 = jnp.concatenate([c_ctx[None, :], c, jnp.zeros((SUBLANES - 1 - dec_batch, D_MODEL), F32)], axis=0)
    mods = _ada(cond8, (w_ada_l0, w_ada_l1), (b_ada_l0, b_ada_l1))
    mods = mods.reshape(DEPTH, SUBLANES, 1, N_ADA * D_MODEL)
    (mod0_p, mod0_s), (mod1_p, mod1_s) = (((mods[l], 0), (mods[l], 1)) for l in range(DEPTH))

    w0 = _prep_l0(w_in_l0, q_lora_norm_l0, kv_lora_norm_l0, w_q_up_l0, w_kv_up_l0, mla_q_norm_l0, mla_k_norm_l0,
                  w_pool_l0, pool_scale_l0, w_out_l0, norm_mix_l0, norm_mlp_l0, w_mlp1_l0, w_mlp2_l0)
    a_p, q_p, k_p, v_p, ckv_p, kr_p = _l0_pre(xp, mod0_p, tiles_p, False, w0, None)
    a_s, q_s, k_s, v_s, _, _ = _l0_pre(xs, mod0_s, tiles_s, True, w0, _rope_tables(dec_seq))
    kr_cache = jnp.pad(cache_l0_mla_krope.reshape(dec_batch * past, MLA_ROPE), ((0, 0), (MLA_NOPE, LANES - MLA_QK)))
    k_c, v_c = _cache_kv(cache_l0_mla_ckv.reshape(dec_batch * past, KV_LORA), kr_cache, w0)
    mla_bound = _normed_len(mla_q_norm_l0 * (MLA_QK ** -0.5 * LOG2E), MLA_QK) * _normed_len(mla_k_norm_l0, MLA_QK)
    attn_p = _ctx_attn(_mla_ctx_kernel, q_p, k_p, v_p, mla_bound, seq, MLA_HEADS * MLA_V, "l0_mla_ctx_attn")
    attn_s = _mla_lat_attn(q_s, k_s, v_s, k_c, v_c, jnp.full((dec_batch, MLA_HEADS), mla_bound, F32), dec_seq, past)
    xp = _post(xp, attn_p, mod0_p, tiles_p, w0, "l0_post_ctx", pool=(a_p, seq))
    xs = _post(xs, attn_s, mod0_s, tiles_s, w0, "l0_post_latent", pool=(a_s, dec_seq))

    w1 = dict(norm_mix=norm_mix_l1.reshape(1, -1), norm_mlp=norm_mlp_l1.reshape(1, -1),
              w_in=w_in_l1.astype(BF16),
              gq=jnp.tile(na_q_norm_l1 * (NA_HEAD_DIM ** -0.5 * LOG2E), 2).reshape(1, LANES),
              gk=jnp.tile(na_k_norm_l1, 2).reshape(1, LANES),
              w_out=w_out_l1.astype(BF16), w_mlp1=w_mlp1_l1.astype(BF16), w_mlp2=w_mlp2_l1.astype(BF16))
    q1_p, k1_p, v1_p, k1_new, v1_new = _l1_pre(xp, mod1_p, tiles_p, True, w1)
    q1_s, k1_s, v1_s = _l1_pre(xs, mod1_s, tiles_s, False, w1)
    q_len, k_len = _normed_len(w1["gq"], NA_HEAD_DIM), _normed_len(w1["gk"], NA_HEAD_DIM)
    na_p = _ctx_attn(_na_ctx_kernel, q1_p, k1_p, v1_p, q_len * k_len, seq, NA_SLABS * LANES, "l1_na_ctx_attn")
    cache_len = jnp.sqrt(jnp.max(jnp.sum(jnp.square(cache_l1_na_k.astype(F32)), axis=-1), axis=1))
    na_bound = (q_len * jnp.maximum(k_len, cache_len)
                + jnp.max(jnp.abs(rel_bias_l1.astype(F32)) * LOG2E, axis=(1, 2))[None, :])
    na_s = _na_lat_attn(q1_s, k1_s, v1_s,
                        cache_l1_na_k.reshape(dec_batch * past, -1), cache_l1_na_v.reshape(dec_batch * past, -1),
                        _na_bias_table(rel_bias_l1, dec_seq // GRID_W), na_bound, dec_seq, past)
    xp = _post(xp, na_p, mod1_p, tiles_p, w1, "l1_post_ctx")
    xs = _post(xs, na_s, mod1_s, tiles_s, w1, "l1_post_latent")

    return (xp.reshape(batch, seq, D_MODEL), xs.reshape(dec_batch, dec_seq, D_MODEL),
            ckv_p.reshape(batch, seq, KV_LORA), kr_p.reshape(batch, seq, MLA_ROPE),
            k1_new.reshape(batch, seq, NA_HEADS, NA_HEAD_DIM), v1_new.reshape(batch, seq, NA_HEADS, NA_HEAD_DIM))
```

```python
import functools

import jax
import jax.numpy as jnp
from jax import lax
from jax.experimental import pallas as pl
from jax.experimental.pallas import tpu as pltpu

F32 = jnp.float32
BF16 = jnp.bfloat16

D_MODEL = 1024
D_FF = 4 * D_MODEL
DEPTH = 2
N_ADA = 6
EPS = 1e-6
NEG_INF = -1e30
ROPE_THETA = 10000.0
GRID_W = 64
POOL_WIDTH = 512
POOL_WINDOWS = (2, 4, 8, 16)
POOL_GROUP = 128
POOL_HALO = 8
MLA_HEADS = 8
MLA_NOPE = 64
MLA_ROPE = 32
MLA_QK = MLA_NOPE + MLA_ROPE
MLA_V = 64
Q_LORA = 256
KV_LORA = 128
NA_HEADS = 16
NA_HEAD_DIM = 64
NA_KH = 8
NA_KW = 16
LOG2E = 1.4426950408889634

LANES = 128
SUBLANES = 8
VMEM_LIMIT = 56 * 1024 * 1024

TOKEN_TILE = 512
MLA_Q_TILE = 512
MLA_STEP_HEADS = 2
MLA_KEY_CHUNK = 2048
MLA_ONLINE_CHUNK = 1024
SCORE_LOOKAHEAD = 4
MAX_SCORE_BOUND = 40.0
FF_CHUNK = 1024
CTX_STEP_SEQS = 2
NA_SLABS = NA_HEADS * NA_HEAD_DIM // LANES
NA_TILE_ROWS = 4
NA_WIN_ROWS = NA_TILE_ROWS + NA_KH - 1


def _params(n_axes):
    return pltpu.CompilerParams(dimension_semantics=("arbitrary",) * n_axes,
                                vmem_limit_bytes=VMEM_LIMIT)


def _const_spec(shape):
    zeros = (0,) * len(shape)
    return pl.BlockSpec(shape, lambda *_: zeros, pipeline_mode=pl.Buffered(1))


def _dot(a, b):
    return jnp.dot(a, b, preferred_element_type=F32)


def _dot_nt(a, b):
    return lax.dot_general(a, b, (((1,), (1,)), ((), ())), preferred_element_type=F32)


def _dot_tn(a, b):
    return lax.dot_general(a, b, (((0,), (0,)), ((), ())), preferred_element_type=F32)


def _rms(x, w, n=None):
    n = x.shape[-1] if n is None else n
    ss = jnp.sum(x * x, axis=-1, keepdims=True)
    return x * lax.rsqrt(ss * (1.0 / n) + EPS) * w


def _modulate(x, nw, shift, scale):
    return _rms(x, nw) * (1.0 + scale) + shift


def _ada_spec(mod, tiles_per_group):
    mods, first_row = mod
    return pl.BlockSpec((None, 1, mods.shape[-1]), lambda i: (first_row + i // tiles_per_group, 0, 0))


def _ada_row(mod_ref, k):
    return mod_ref[:, k * D_MODEL:(k + 1) * D_MODEL]


def _ada_kernel(cond_ref, w0_ref, w1_ref, b_ref, o_ref):
    c = cond_ref[...]
    s = (c * jax.nn.sigmoid(c)).astype(BF16)
    for layer, w_ref in enumerate((w0_ref, w1_ref)):
        @pl.when(pl.program_id(0) == layer)
        def _(w_ref=w_ref):
            o_ref[...] = _dot(s, w_ref[...].astype(BF16)) + b_ref[...]


def _ada(cond8, w_ada, b_ada):
    n_out = w_ada[0].shape[1]
    tn = D_MODEL
    nb = n_out // tn
    return pl.pallas_call(
        _ada_kernel,
        out_shape=jax.ShapeDtypeStruct((DEPTH, SUBLANES, n_out), F32),
        grid=(DEPTH, nb),
        in_specs=[pl.BlockSpec((SUBLANES, D_MODEL), lambda l, j: (0, 0)),
                  pl.BlockSpec((D_MODEL, tn), lambda l, j: (0, jnp.where(l == 0, j, nb - 1))),
                  pl.BlockSpec((D_MODEL, tn), lambda l, j: (0, jnp.where(l == 1, j, 0))),
                  pl.BlockSpec((None, 1, tn), lambda l, j: (l, 0, j))],
        out_specs=pl.BlockSpec((None, SUBLANES, tn), lambda l, j: (l, 0, j)),
        compiler_params=_params(2),
        name="ada",
    )(cond8, w_ada[0], w_ada[1], jnp.stack(b_ada).reshape(DEPTH, 1, n_out))


def _rope(x, c, s):
    return x * c + pltpu.roll(x, LANES - MLA_ROPE, 1) * s


PAIR = 2 * LANES


def _head_rinv(raw, e2_ref):
    ss = _dot((raw * raw).astype(BF16), e2_ref[...])
    return lax.rsqrt(ss * (1.0 / MLA_QK) + EPS)


def _mla_keys_values(ckv, kr_slab, wk_ref, gk, wv_ref, e2_ref, rope_tabs, k_ref, v_ref):
    c16 = ckv.astype(BF16)
    tail = kr_slab * gk
    if rope_tabs is not None:
        tail = _rope(tail, *rope_tabs)
    ones_lane = (lax.broadcasted_iota(jnp.int32, (1, LANES), 1) == MLA_V).astype(F32)
    for p in range(MLA_HEADS // 2):
        cols = pl.ds(p * PAIR, PAIR)
        kn = _dot(c16, wk_ref[:, cols])
        v = _dot(c16, wv_ref[:, cols])
        rinv = _head_rinv(kn + jnp.tile(kr_slab, (1, 2)), e2_ref)
        for i in range(2):
            hs = slice(i * LANES, (i + 1) * LANES)
            k_ref[2 * p + i] = (rinv[:, hs] * (kn[:, hs] * gk + tail)).astype(BF16)
            v_ref[2 * p + i] = (v[:, hs] + ones_lane).astype(BF16)


def _l0_pre_kernel(latent, x_ref, mod_ref, nw_ref, win_ref, qln_ref, kvn_ref, wq_ref, gq_ref,
                   wk_ref, gk_ref, wv_ref, e2_ref, *rest):
    if latent:
        rope_ref, a_ref, q_ref, k_ref, v_ref, ckv_ref, kr_ref = rest
        tile_rows = x_ref.shape[0] // GRID_W
        row0 = (pl.program_id(0) * tile_rows) % rope_ref.shape[1]
        expand = lambda col, row: jnp.concatenate(
            [rope_ref[col] + rope_ref[row, pl.ds(row0 + r, 1), :] for r in range(tile_rows)], axis=0)
        rope_tabs = (expand(0, 1), expand(2, 3))
    else:
        a_ref, q_ref, k_ref, v_ref, ckv_ref, kr_ref = rest
        rope_tabs = None
    h = _modulate(x_ref[...], nw_ref[...], _ada_row(mod_ref, 0), _ada_row(mod_ref, 1)).astype(BF16)
    proj = _dot(h, win_ref[...])
    a_ref[...] = proj[:, :POOL_WIDTH]
    q_lat = proj[:, POOL_WIDTH:POOL_WIDTH + Q_LORA]
    kv_lat = proj[:, POOL_WIDTH + Q_LORA:POOL_WIDTH + Q_LORA + KV_LORA]
    kr_slab = proj[:, POOL_WIDTH + Q_LORA + KV_LORA:]
    ql = _rms(q_lat, qln_ref[...]).astype(BF16)
    gq = gq_ref[...]
    for p in range(MLA_HEADS // 2):
        q = _dot(ql, wq_ref[:, pl.ds(p * PAIR, PAIR)])
        q_rinv = _head_rinv(q, e2_ref)
        for i in range(2):
            hs = slice(i * LANES, (i + 1) * LANES)
            y = q[:, hs] * q_rinv[:, hs] * gq
            if rope_tabs is not None:
                y = _rope(y, *rope_tabs)
            q_ref[2 * p + i] = y.astype(BF16)
    ckv = _rms(kv_lat, kvn_ref[...])
    ckv_ref[...] = ckv
    kr_ref[...] = kr_slab[:, MLA_NOPE:MLA_QK]
    _mla_keys_values(ckv, kr_slab, wk_ref, gk_ref[...], wv_ref, e2_ref, rope_tabs, k_ref, v_ref)


def _slab_spec(slabs, tm):
    return pl.BlockSpec((slabs, tm, LANES), lambda i: (0, i, 0))


def _l0_pre(x, mod, tiles_per_group, latent, w, rope_tabs):
    n = x.shape[0]
    tm = TOKEN_TILE
    row = lambda i: (i, 0)
    hk = MLA_HEADS * LANES
    in_specs = [pl.BlockSpec((tm, D_MODEL), row),
                _ada_spec(mod, tiles_per_group),
                _const_spec((1, D_MODEL)),
                _const_spec((D_MODEL, D_MODEL)),
                _const_spec((1, Q_LORA)),
                _const_spec((1, KV_LORA)),
                _const_spec((Q_LORA, hk)),
                _const_spec((1, LANES)),
                _const_spec((KV_LORA, hk)),
                _const_spec((1, LANES)),
                _const_spec((KV_LORA, hk)),
                _const_spec((2 * LANES, 2 * LANES))]
    gains = "rope" if latent else "plain"
    args = [x, mod[0], w["norm_mix"], w["w_in"], w["qln"], w["kvn"], w["wq"], w["gq_" + gains], w["wk"],
            w["gk_" + gains], w["wv"], w["e2"]]
    if latent:
        assert tm % GRID_W == 0
        in_specs.append(_const_spec(rope_tabs.shape))
        args.append(rope_tabs)
    slab = jax.ShapeDtypeStruct((MLA_HEADS, n, LANES), BF16)
    out_shape = (jax.ShapeDtypeStruct((n, POOL_WIDTH), F32), slab, slab, slab,
                 jax.ShapeDtypeStruct((n, KV_LORA), F32),
                 jax.ShapeDtypeStruct((n, MLA_ROPE), F32))
    out_specs = (pl.BlockSpec((tm, POOL_WIDTH), row),
                 _slab_spec(MLA_HEADS, tm), _slab_spec(MLA_HEADS, tm), _slab_spec(MLA_HEADS, tm),
                 pl.BlockSpec((tm, KV_LORA), row), pl.BlockSpec((tm, MLA_ROPE), row))
    return pl.pallas_call(
        functools.partial(_l0_pre_kernel, latent),
        out_shape=out_shape, grid=(n // tm,), in_specs=in_specs, out_specs=out_specs,
        compiler_params=_params(1), name="l0_pre_latent" if latent else "l0_pre_ctx",
    )(*args)


def _cache_kv_kernel(ckv_ref, kr_ref, wk_ref, gk_ref, wv_ref, e2_ref, k_ref, v_ref):
    _mla_keys_values(ckv_ref[...], kr_ref[...], wk_ref, gk_ref[...], wv_ref, e2_ref, None, k_ref, v_ref)


def _cache_kv(ckv, kr_slab, w):
    n = ckv.shape[0]
    tm = TOKEN_TILE
    row = lambda i: (i, 0)
    hk = MLA_HEADS * LANES
    slab = jax.ShapeDtypeStruct((MLA_HEADS, n, LANES), BF16)
    return pl.pallas_call(
        _cache_kv_kernel,
        out_shape=(slab, slab),
        grid=(n // tm,),
        in_specs=[pl.BlockSpec((tm, KV_LORA), row), pl.BlockSpec((tm, LANES), row),
                  _const_spec((KV_LORA, hk)), _const_spec((1, LANES)), _const_spec((KV_LORA, hk)),
                  _const_spec((2 * LANES, 2 * LANES))],
        out_specs=(_slab_spec(MLA_HEADS, tm), _slab_spec(MLA_HEADS, tm)),
        compiler_params=_params(1), name="l0_cache_kv",
    )(ckv, kr_slab, w["wk"], w["gk_plain"], w["wv"], w["e2"])


def _pool_mixer(seq_len, tile, prev_ref, u_ref, next_ref, wp_ref, ps_ref, buf_ref):
    tm = u_ref.shape[0]
    piece = min(tm, seq_len)
    n_pieces = tm // piece
    stride = piece + 2 * POOL_HALO
    starts = []
    for j in range(n_pieces):
        pos0 = (tile * tm + j * piece) % seq_len
        prev = prev_ref[...] if j == 0 else u_ref[j * piece - POOL_HALO:j * piece, :]
        nxt = next_ref[...] if j == n_pieces - 1 else u_ref[(j + 1) * piece:(j + 1) * piece + POOL_HALO, :]
        base = j * stride
        buf_ref[base:base + POOL_HALO, :] = jnp.where(pos0 != 0, prev, 0.0)
        buf_ref[base + POOL_HALO:base + POOL_HALO + piece, :] = u_ref[j * piece:(j + 1) * piece, :]
        buf_ref[base + POOL_HALO + piece:base + stride, :] = jnp.where(pos0 + piece != seq_len, nxt, 0.0)
        starts.append(pos0)
    def group(g):
        win = POOL_WINDOWS[g]
        lanes = pl.ds(g * POOL_GROUP, POOL_GROUP)
        pooled = []
        for j in range(n_pieces):
            centre = j * stride + POOL_HALO
            lo = starts[j] + lax.broadcasted_iota(jnp.int32, (piece, 1), 0) - win // 2
            cnt = (jnp.minimum(lo + win, seq_len) - jnp.maximum(lo, 0)).astype(F32)
            acc = buf_ref[pl.ds(centre - win // 2, piece), lanes]
            for d in range(1 - win // 2, win - win // 2):
                acc = acc + buf_ref[pl.ds(centre + d, piece), lanes]
            pooled.append(acc / cnt - buf_ref[pl.ds(centre, piece), lanes])
        pooled = pooled[0] if n_pieces == 1 else jnp.concatenate(pooled, axis=0)
        return (_dot(pooled.astype(BF16), wp_ref[g]) * ps_ref[:, lanes]).astype(BF16)

    return [functools.partial(group, g) for g in range(len(POOL_WINDOWS))]


def _attend(streams, running_max):
    items = [(si, ch) for si, (_, chunks, _) in enumerate(streams) for ch in chunks]

    def score(item):
        si, (k_fn, _, extra_fn) = item
        s = _dot_nt(k_fn(), streams[si][0])
        return s if extra_fn is None else s + extra_fn()

    state = [None] * len(streams)
    pending = [score(item) for item in items[:SCORE_LOOKAHEAD]]
    for i, (si, (_, v_fn, _)) in enumerate(items):
        s = pending.pop(0)
        if i + SCORE_LOOKAHEAD < len(items):
            pending.append(score(items[i + SCORE_LOOKAHEAD]))
        want_den = streams[si][2]
        m_new, alpha = None, 1.0
        if state[si] is not None:
            m, acc, den = state[si]
        if running_max:
            m_new = jnp.max(s, axis=0, keepdims=True)
            if state[si] is not None:
                m_new = jnp.maximum(m, m_new)
                alpha = jnp.exp2(m - m_new)
            s = s - m_new
        p = jnp.exp2(s)
        pv = _dot_tn(v_fn(), p.astype(BF16))
        psum = jnp.sum(p, axis=0, keepdims=True) if want_den else None
        if state[si] is None:
            state[si] = (m_new, pv, psum)
        elif running_max:
            state[si] = (m_new, acc * alpha + pv, den * alpha + psum if want_den else None)
        else:
            state[si] = (None, acc + pv, den + psum if want_den else None)
    return [(acc, den) for _, acc, den in state]


def _ref_chunks(k_ref, v_ref, slab, chunk, extra_fn=None, lo=0, hi=None):
    hi = k_ref.shape[1] if hi is None else hi
    out = []
    for c in range(lo, hi, chunk):
        rows = pl.ds(c, min(chunk, hi - c))
        out.append((lambda rows=rows: k_ref[slab, rows, :], lambda rows=rows: v_ref[slab, rows, :], extra_fn))
    return out


def _mla_finish(o_t):
    return o_t[:MLA_V] / o_t[MLA_V:MLA_V + 1]


def _na_queries(q2):
    low = lax.broadcasted_iota(jnp.int32, (1, LANES), 1) < NA_HEAD_DIM
    zero = jnp.zeros_like(q2)
    return jnp.concatenate([jnp.where(low, q2, zero), jnp.where(low, zero, q2)], axis=0)


def _na_finish(o_t, den):
    nq = o_t.shape[1] // 2
    o = o_t / den
    return jnp.concatenate([o[:NA_HEAD_DIM, :nq], o[NA_HEAD_DIM:, nq:]], axis=0)


def _attend_guarded(bound, streams_fn):
    return lax.cond(bound <= MAX_SCORE_BOUND,
                    lambda: _attend(streams_fn(False), False),
                    lambda: _attend(streams_fn(True), True))


def _mla_ctx_kernel(seq, bound_ref, q_ref, k_ref, v_ref, o_ref):
    spans = [(lo, lo + seq) for lo in range(0, k_ref.shape[1], seq)]
    res = _attend_guarded(bound_ref[0, 0], lambda _: [
        (q_ref[h, lo:hi, :], _ref_chunks(k_ref, v_ref, h, seq, lo=lo, hi=hi), False)
        for lo, hi in spans for h in range(MLA_HEADS)])
    for i, (lo, hi) in enumerate(spans):
        for hp in range(MLA_HEADS // 2):
            o_t = jnp.concatenate([_mla_finish(res[i * MLA_HEADS + h][0]) for h in (2 * hp, 2 * hp + 1)], axis=0)
            o_ref[lo:hi, hp * LANES:(hp + 1) * LANES] = o_t.T.astype(BF16)


def _na_ctx_kernel(seq, bound_ref, q_ref, k_ref, v_ref, o_ref):
    spans = [(lo, lo + seq) for lo in range(0, k_ref.shape[1], seq)]
    res = _attend_guarded(bound_ref[0, 0], lambda _: [
        (_na_queries(q_ref[s, lo:hi, :]), _ref_chunks(k_ref, v_ref, s, seq, lo=lo, hi=hi), True)
        for lo, hi in spans for s in range(NA_SLABS)])
    for i, (lo, hi) in enumerate(spans):
        for s in range(NA_SLABS):
            o_ref[lo:hi, s * LANES:(s + 1) * LANES] = _na_finish(*res[i * NA_SLABS + s]).T.astype(BF16)


def _ctx_attn(body, q, k, v, bound, seq, width, name):
    slabs, n, _ = q.shape
    rows = CTX_STEP_SEQS * seq
    spec = pl.BlockSpec((slabs, rows, LANES), lambda b: (0, b, 0))
    return pl.pallas_call(
        functools.partial(body, seq),
        out_shape=jax.ShapeDtypeStruct((n, width), BF16),
        grid=(n // rows,),
        in_specs=[pl.BlockSpec(memory_space=pltpu.SMEM), spec, spec, spec],
        out_specs=pl.BlockSpec((rows, width), lambda b: (b, 0)),
        compiler_params=_params(1), name=name,
    )(bound.reshape(1, 1), q, k, v)


def _mla_lat_kernel(bound_ref, q_ref, kl_ref, vl_ref, kc_ref, vc_ref, o_ref):
    batch, head0 = pl.program_id(0), pl.program_id(1) * MLA_STEP_HEADS
    heads = range(MLA_STEP_HEADS)
    bounds = [bound_ref[batch, head0 + h] for h in heads]

    def streams(running_max):
        size = MLA_ONLINE_CHUNK if running_max else MLA_KEY_CHUNK
        return [(q_ref[h], _ref_chunks(kl_ref, vl_ref, h, size) + _ref_chunks(kc_ref, vc_ref, h, size), False)
                for h in heads]

    res = _attend_guarded(functools.reduce(jnp.maximum, bounds), streams)
    o_ref[...] = jnp.concatenate([_mla_finish(o) for o, _ in res], axis=0).T.astype(BF16)


def _normed_len(gain, dim):
    return (dim ** 0.5) * jnp.max(jnp.abs(gain.astype(F32)))


def _mla_lat_attn(q, k, v, kc, vc, bound, seq, past):
    n = q.shape[1]
    tq = MLA_Q_TILE
    nq = seq // tq
    hs = MLA_STEP_HEADS
    return pl.pallas_call(
        _mla_lat_kernel,
        out_shape=jax.ShapeDtypeStruct((n, MLA_HEADS * MLA_V), BF16),
        grid=(n // seq, MLA_HEADS // hs, nq),
        in_specs=[pl.BlockSpec(memory_space=pltpu.SMEM),
                  pl.BlockSpec((hs, tq, LANES), lambda b, h, i: (h, b * nq + i, 0)),
                  pl.BlockSpec((hs, seq, LANES), lambda b, h, i: (h, b, 0)),
                  pl.BlockSpec((hs, seq, LANES), lambda b, h, i: (h, b, 0)),
                  pl.BlockSpec((hs, past, LANES), lambda b, h, i: (h, b, 0)),
                  pl.BlockSpec((hs, past, LANES), lambda b, h, i: (h, b, 0))],
        out_specs=pl.BlockSpec((tq, hs * MLA_V), lambda b, h, i: (b * nq + i, h)),
        compiler_params=_params(3), name="l0_mla_latent_attn",
    )(bound, q, k, v, kc, vc)


def _na_first_row(r, rows):
    return jnp.clip(r - NA_KH // 2, 0, rows - NA_KH)


def _na_lat_kernel(rows, bound_ref, q_ref, k_ref, v_ref, kc_ref, vc_ref, t2_ref, o_ref):
    nq = NA_TILE_ROWS * GRID_W
    nk = NA_WIN_ROWS * GRID_W
    n_off = 2 * NA_KH - 1
    r0 = pl.program_id(1) * NA_TILE_ROWS
    w0 = jnp.clip(r0 - NA_KH // 2, 0, rows - NA_WIN_ROWS)

    def block_index(j, t):
        key_row = w0 + j
        i = r0 + 2 * t
        first_u, first_l = _na_first_row(i, rows), _na_first_row(i + 1, rows)
        in_u = (key_row >= first_u) & (key_row < first_u + NA_KH)
        in_l = (key_row >= first_l) & (key_row < first_l + NA_KH)
        d = jnp.clip(key_row - i + NA_KH - 1, 0, n_off - 1)
        return jnp.where(in_u & in_l, d, jnp.where(in_u, n_off, jnp.where(in_l, n_off + 1, n_off + 2)))

    idx = [[block_index(j, t) for t in range(NA_TILE_ROWS // 2)] for j in range(NA_WIN_ROWS)]

    def local_bias(s):
        heads = [jnp.concatenate([jnp.concatenate([t2_ref[2 * s + hh, i] for i in row], axis=1) for row in idx], axis=0)
                 for hh in range(2)]
        return jnp.concatenate(heads, axis=1)

    batch = pl.program_id(0)
    bounds = [bound_ref[batch, h] for h in range(NA_HEADS)]

    def cache_chunk(s):
        lanes = pl.ds(s * LANES, LANES)
        return [(lambda: kc_ref[:, lanes].astype(BF16), lambda: vc_ref[:, lanes].astype(BF16), None)]

    res = _attend_guarded(
        functools.reduce(jnp.maximum, bounds),
        lambda _: [(_na_queries(q_ref[s]),
                    _ref_chunks(k_ref, v_ref, s, nk, functools.partial(local_bias, s)) + cache_chunk(s),
                    True) for s in range(NA_SLABS)])
    for s in range(NA_SLABS):
        o_ref[:, s * LANES:(s + 1) * LANES] = _na_finish(*res[s]).T.astype(BF16)


def _na_lat_attn(q, k, v, kc, vc, t2, bound, seq, past):
    slabs, n, _ = q.shape
    rows = seq // GRID_W
    tiles = rows // NA_TILE_ROWS
    nq = NA_TILE_ROWS * GRID_W
    nk = NA_WIN_ROWS * GRID_W

    def win_map(b, t):
        w0 = jnp.clip(t * NA_TILE_ROWS - NA_KH // 2, 0, rows - NA_WIN_ROWS)
        return (0, pl.multiple_of(b * seq + w0 * GRID_W, GRID_W), 0)

    win_spec = pl.BlockSpec((pl.Element(slabs), pl.Element(nk), pl.Element(LANES)), win_map)
    ctx_spec = pl.BlockSpec((past, slabs * LANES), lambda b, t: (b, 0))
    return pl.pallas_call(
        functools.partial(_na_lat_kernel, rows),
        out_shape=jax.ShapeDtypeStruct((n, slabs * LANES), BF16),
        grid=(n // seq, tiles),
        in_specs=[pl.BlockSpec(memory_space=pltpu.SMEM),
                  pl.BlockSpec((slabs, nq, LANES), lambda b, t: (0, b * tiles + t, 0)),
                  win_spec, win_spec, ctx_spec, ctx_spec,
                  _const_spec(t2.shape)],
        out_specs=pl.BlockSpec((nq, slabs * LANES), lambda b, t: (b * tiles + t, 0)),
        compiler_params=_params(2), name="l1_na_latent_attn",
    )(bound, q, k, v, kc, vc, t2)


def _pair_norm(x2, g2):
    low = lax.broadcasted_iota(jnp.int32, (1, LANES), 1) < NA_HEAD_DIM
    sq = x2 * x2
    ss_all = jnp.sum(sq, axis=-1, keepdims=True)
    ss_low = jnp.sum(jnp.where(low, sq, 0.0), axis=-1, keepdims=True)
    ss = jnp.where(low, ss_low, ss_all - ss_low)
    return x2 * lax.rsqrt(ss * (1.0 / NA_HEAD_DIM) + EPS) * g2


def _l1_pre_kernel(keep_f32, x_ref, mod_ref, nw_ref, win_ref, gq_ref, gk_ref, q_ref, k_ref, v_ref, *f32_refs):
    h = _modulate(x_ref[...], nw_ref[...], _ada_row(mod_ref, 0), _ada_row(mod_ref, 1)).astype(BF16)
    width = NA_SLABS * LANES
    q = _dot(h, win_ref[:, 0:width])
    k = _dot(h, win_ref[:, width:2 * width])
    v = _dot(h, win_ref[:, 2 * width:3 * width])
    gq = gq_ref[...]
    gk = gk_ref[...]
    for s in range(NA_SLABS):
        ls = slice(s * LANES, (s + 1) * LANES)
        q_ref[s] = _pair_norm(q[:, ls], gq).astype(BF16)
        kn = _pair_norm(k[:, ls], gk)
        k_ref[s] = kn.astype(BF16)
        v_ref[s] = v[:, ls].astype(BF16)
        if keep_f32:
            f32_refs[0][:, ls] = kn
    if keep_f32:
        f32_refs[1][...] = v


def _l1_pre(x, mod, tiles_per_group, keep_f32, w):
    n = x.shape[0]
    tm = TOKEN_TILE
    row = lambda i: (i, 0)
    width = NA_SLABS * LANES
    slab = jax.ShapeDtypeStruct((NA_SLABS, n, LANES), BF16)
    out_shape = [slab, slab, slab]
    out_specs = [_slab_spec(NA_SLABS, tm)] * 3
    if keep_f32:
        out_shape += [jax.ShapeDtypeStruct((n, width), F32)] * 2
        out_specs += [pl.BlockSpec((tm, width), row)] * 2
    return pl.pallas_call(
        functools.partial(_l1_pre_kernel, keep_f32),
        out_shape=tuple(out_shape), grid=(n // tm,),
        in_specs=[pl.BlockSpec((tm, D_MODEL), row),
                  _ada_spec(mod, tiles_per_group),
                  _const_spec((1, D_MODEL)),
                  _const_spec((D_MODEL, 3 * width)),
                  _const_spec((1, LANES)),
                  _const_spec((1, LANES))],
        out_specs=tuple(out_specs),
        compiler_params=_params(1), name="l1_pre_ctx" if keep_f32 else "l1_pre_latent",
    )(x, mod[0], w["norm_mix"], w["w_in"], w["gq"], w["gk"])


def _post_kernel(pool_seq, x_ref, attn_ref, *rest):
    if pool_seq is None:
        mod_ref, nw_ref, wo_ref, w1_ref, w2_ref, o_ref = rest
        mixed = _dot(attn_ref[...], wo_ref[...])
    else:
        (first_refs, next_refs, wp_ref, ps_ref, mod_ref, nw_ref, wo_ref, w1_ref, w2_ref, o_ref,
         buf_ref, pooled_ref) = rest[0:3], rest[3:6], *rest[6:]
        step = pl.program_id(0)

        @pl.when(step == 0)
        def _():
            for g, group in enumerate(_pool_mixer(pool_seq, 0, *first_refs, wp_ref, ps_ref, buf_ref)):
                pooled_ref[:, g * POOL_GROUP:(g + 1) * POOL_GROUP] = group()

        mixed = _dot(jnp.concatenate([pooled_ref[...], attn_ref[...]], axis=-1), wo_ref[...])
        next_groups = _pool_mixer(pool_seq, jnp.minimum(step + 1, pl.num_programs(0) - 1), *next_refs,
                                  wp_ref, ps_ref, buf_ref)
    x1 = x_ref[...] + _ada_row(mod_ref, 2) * mixed
    h = _modulate(x1, nw_ref[...], _ada_row(mod_ref, 3), _ada_row(mod_ref, 4)).astype(BF16)
    acc = jnp.zeros(x1.shape, F32)
    n_chunks = D_FF // FF_CHUNK
    for c in range(n_chunks):
        cs = pl.ds(c * FF_CHUNK, FF_CHUNK)
        u = jnp.square(jnp.maximum(_dot(h, w1_ref[:, cs]), 0.0)).astype(BF16)
        acc = acc + _dot(u, w2_ref[cs, :])
        if pool_seq is not None:
            for g in range(c * len(next_groups) // n_chunks, (c + 1) * len(next_groups) // n_chunks):
                pooled_ref[:, g * POOL_GROUP:(g + 1) * POOL_GROUP] = next_groups[g]()
    o_ref[...] = x1 + _ada_row(mod_ref, 5) * acc


def _post(x, attn, mod, tiles_per_group, w, name, pool=None):
    n = x.shape[0]
    tm = TOKEN_TILE
    row = lambda i: (i, 0)
    in_specs = [pl.BlockSpec((tm, D_MODEL), row), pl.BlockSpec((tm, attn.shape[1]), row)]
    args = [x, attn]
    scratch = []
    pool_seq = None
    if pool is not None:
        a_in, pool_seq = pool
        hb = tm // POOL_HALO
        last = n // POOL_HALO - 1
        pieces = tm // min(tm, pool_seq)
        last_tile = n // tm - 1

        def tile_specs(tile_of):
            return [pl.BlockSpec((POOL_HALO, POOL_WIDTH), lambda i: (jnp.maximum(tile_of(i) * hb - 1, 0), 0)),
                    pl.BlockSpec((tm, POOL_WIDTH), lambda i: (tile_of(i), 0)),
                    pl.BlockSpec((POOL_HALO, POOL_WIDTH), lambda i: (jnp.minimum((tile_of(i) + 1) * hb, last), 0))]

        in_specs += tile_specs(lambda i: 0) + tile_specs(lambda i: jnp.minimum(i + 1, last_tile))
        in_specs += [_const_spec((len(POOL_WINDOWS), POOL_GROUP, POOL_GROUP)), _const_spec((1, POOL_WIDTH))]
        args += [a_in] * 6 + [w["w_pool"], w["pool_scale"]]
        scratch = [pltpu.VMEM((tm + 2 * POOL_HALO * pieces, POOL_WIDTH), F32), pltpu.VMEM((tm, POOL_WIDTH), BF16)]
    in_specs += [_ada_spec(mod, tiles_per_group),
                 _const_spec((1, D_MODEL)),
                 _const_spec((D_MODEL, D_MODEL)),
                 _const_spec((D_MODEL, D_FF)),
                 _const_spec((D_FF, D_MODEL))]
    args += [mod[0], w["norm_mlp"], w["w_out"], w["w_mlp1"], w["w_mlp2"]]
    return pl.pallas_call(
        functools.partial(_post_kernel, pool_seq),
        out_shape=jax.ShapeDtypeStruct((n, D_MODEL), F32), grid=(n // tm,),
        in_specs=in_specs, out_specs=pl.BlockSpec((tm, D_MODEL), row),
        scratch_shapes=scratch, compiler_params=_params(1), name=name,
    )(*args)


def _rope_tables(seq):
    assert seq == GRID_W * GRID_W
    half = MLA_ROPE // 2
    inv_freq = jnp.power(ROPE_THETA, -jnp.arange(0, half, 2, dtype=F32) / half)
    ang = jnp.arange(GRID_W, dtype=F32)[:, None] * inv_freq[None, :]
    cos, sin = jnp.cos(ang), jnp.sin(ang)
    fill = lambda n, v=0.0: jnp.full((GRID_W, n), v, F32)
    spare = LANES - MLA_QK
    return jnp.stack([
        jnp.concatenate([fill(MLA_NOPE, 1.0), fill(half), cos, cos, fill(spare)], axis=1),
        jnp.concatenate([fill(MLA_NOPE), cos, cos, fill(half), fill(spare)], axis=1),
        jnp.concatenate([fill(MLA_NOPE), fill(half), -sin, sin, fill(spare)], axis=1),
        jnp.concatenate([fill(MLA_NOPE), -sin, sin, fill(half), fill(spare)], axis=1)])


def _rope_partner(tail):
    shape = tail.shape
    t = tail.reshape(shape[:-1] + (2, 2, MLA_ROPE // 4))
    return t[..., ::-1, :].reshape(shape)


def _na_bias_table(rel_bias, rows):
    qcol = jnp.arange(GRID_W)
    kcol = jnp.arange(GRID_W)
    col_start = jnp.clip(qcol - NA_KW // 2, 0, GRID_W - NA_KW)
    col_mask = (kcol[:, None] >= col_start[None, :]) & (kcol[:, None] < col_start[None, :] + NA_KW)
    offs = jnp.arange(2 * NA_KW - 1)
    onehot = (kcol[None, :, None] - qcol[None, None, :] + NA_KW - 1 == offs[:, None, None]).astype(F32)
    blocks = jnp.einsum('hdj,jkq->hdkq', rel_bias.astype(F32) * LOG2E, onehot, precision=lax.Precision.HIGHEST)
    blocks = jnp.where(col_mask[None, None], blocks, NEG_INF)
    neg = jnp.full_like(blocks[:, :1], NEG_INF)
    both = jnp.concatenate([blocks, jnp.concatenate([neg, blocks[:, :-1]], axis=1)], axis=-1)
    d_upper, d_lower = _na_partial_offsets(rows)
    upper_only = jnp.concatenate([blocks[:, d_upper:d_upper + 1], neg], axis=-1)
    lower_only = jnp.concatenate([neg, blocks[:, d_lower - 1:d_lower]], axis=-1)
    return jnp.concatenate([both, upper_only, lower_only, jnp.concatenate([neg, neg], axis=-1)], axis=1)


def _na_partial_offsets(rows):
    first = lambda r: min(max(r - NA_KH // 2, 0), rows - NA_KH)
    upper, lower = set(), set()
    for r0 in range(0, rows, NA_TILE_ROWS):
        w0 = min(max(r0 - NA_KH // 2, 0), rows - NA_WIN_ROWS)
        for j in range(NA_WIN_ROWS):
            for t in range(NA_TILE_ROWS // 2):
                key_row, i = w0 + j, r0 + 2 * t
                in_u = first(i) <= key_row < first(i) + NA_KH
                in_l = first(i + 1) <= key_row < first(i + 1) + NA_KH
                d = key_row - i + NA_KH - 1
                if in_u and in_l:
                    assert 1 <= d <= 2 * NA_KH - 2
                elif in_u:
                    upper.add(d)
                elif in_l:
                    lower.add(d)
    assert len(upper) == 1 and len(lower) == 1, (upper, lower)
    return upper.pop(), lower.pop()


def _prep_l0(w_in, q_lora_norm, kv_lora_norm, w_q_up, w_kv_up, mla_q_norm, mla_k_norm, w_pool, pool_scale,
             w_out, norm_mix, norm_mlp, w_mlp1, w_mlp2):
    lat_end = POOL_WIDTH + Q_LORA + KV_LORA
    w_in_pad = jnp.concatenate([w_in[:, :lat_end], jnp.zeros((D_MODEL, MLA_NOPE), w_in.dtype),
                                w_in[:, lat_end:], _rope_partner(w_in[:, lat_end:])], axis=1)
    pad = LANES - MLA_QK
    hk = MLA_HEADS * LANES
    wq = w_q_up.reshape(Q_LORA, MLA_HEADS, MLA_QK)
    wq = jnp.concatenate([wq, _rope_partner(wq[:, :, MLA_NOPE:])], axis=-1)
    kv = w_kv_up.reshape(KV_LORA, MLA_HEADS, MLA_NOPE + MLA_V)
    wk = jnp.pad(kv[:, :, :MLA_NOPE], ((0, 0), (0, 0), (0, LANES - MLA_NOPE)))
    wv = jnp.pad(kv[:, :, MLA_NOPE:], ((0, 0), (0, 0), (0, LANES - MLA_V)))
    gq = mla_q_norm * (MLA_QK ** -0.5 * LOG2E)
    slab_lane = jnp.arange(2 * LANES)
    e2 = (slab_lane[:, None] // LANES == slab_lane[None, :] // LANES) & (slab_lane[:, None] % LANES < MLA_QK)
    return dict(
        norm_mix=norm_mix.reshape(1, -1), norm_mlp=norm_mlp.reshape(1, -1),
        w_in=w_in_pad.astype(BF16),
        qln=q_lora_norm.reshape(1, -1), kvn=kv_lora_norm.reshape(1, -1),
        wq=wq.reshape(Q_LORA, hk).astype(BF16),
        gq_plain=jnp.pad(gq, (0, pad)).reshape(1, LANES),
        gq_rope=jnp.concatenate([gq, _rope_partner(gq[MLA_NOPE:])]).reshape(1, LANES),
        wk=wk.reshape(KV_LORA, hk).astype(BF16),
        gk_plain=jnp.pad(mla_k_norm, (0, pad)).reshape(1, LANES),
        gk_rope=jnp.concatenate([mla_k_norm, _rope_partner(mla_k_norm[MLA_NOPE:])]).reshape(1, LANES),
        wv=wv.reshape(KV_LORA, hk).astype(BF16),
        e2=e2.astype(BF16),
        w_pool=w_pool.astype(BF16), pool_scale=pool_scale.reshape(1, -1),
        w_out=w_out.astype(BF16), w_mlp1=w_mlp1.astype(BF16), w_mlp2=w_mlp2.astype(BF16))


def kernel(x_prompt, x_sample, cache_l0_mla_ckv, cache_l0_mla_krope, cache_l1_na_k, cache_l1_na_v, c, c_ctx, w_ada_l0, b_ada_l0, norm_mix_l0, norm_mlp_l0, w_mlp1_l0, w_mlp2_l0, w_in_l0, q_lora_norm_l0, kv_lora_norm_l0, w_q_up_l0, w_kv_up_l0, mla_q_norm_l0, mla_k_norm_l0, w_pool_l0, pool_scale_l0, w_out_l0, w_ada_l1, b_ada_l1, norm_mix_l1, norm_mlp_l1, w_mlp1_l1, w_mlp2_l1, w_in_l1, na_q_norm_l1, na_k_norm_l1, rel_bias_l1, w_out_l1):
    batch, seq, _ = x_prompt.shape
    dec_batch, dec_seq, _ = x_sample.shape
    past = cache_l0_mla_ckv.shape[1]
    xp = x_prompt.reshape(batch * seq, D_MODEL)
    xs = x_sample.reshape(dec_batch * dec_seq, D_MODEL)
    tiles_p = (batch * seq) // TOKEN_TILE
    tiles_s = dec_seq // TOKEN_TILE

    cond8 = jnp.concatenate([c_ctx[None, :], c, jnp.zeros((SUBLANES - 1 - dec_batch, D_MODEL), F32)], axis=0)
    mods = _ada(cond8, (w_ada_l0, w_ada_l1), (b_ada_l0, b_ada_l1))
    mods = mods.reshape(DEPTH, SUBLANES, 1, N_ADA * D_MODEL)
    (mod0_p, mod0_s), (mod1_p, mod1_s) = (((mods[l], 0), (mods[l], 1)) for l in range(DEPTH))

    w0 = _prep_l0(w_in_l0, q_lora_norm_l0, kv_lora_norm_l0, w_q_up_l0, w_kv_up_l0, mla_q_norm_l0, mla_k_norm_l0,
                  w_pool_l0, pool_scale_l0, w_out_l0, norm_mix_l0, norm_mlp_l0, w_mlp1_l0, w_mlp2_l0)
    a_p, q_p, k_p, v_p, ckv_p, kr_p = _l0_pre(xp, mod0_p, tiles_p, False, w0, None)
    a_s, q_s, k_s, v_s, _, _ = _l0_pre(xs, mod0_s, tiles_s, True, w0, _rope_tables(dec_seq))
    kr_cache = jnp.pad(cache_l0_mla_krope.reshape(dec_batch * past, MLA_ROPE), ((0, 0), (MLA_NOPE, LANES - MLA_QK)))
    k_c, v_c = _cache_kv(cache_l0_mla_ckv.reshape(dec_batch * past, KV_LORA), kr_cache, w0)
    mla_bound = _normed_len(mla_q_norm_l0 * (MLA_QK ** -0.5 * LOG2E), MLA_QK) * _normed_len(mla_k_norm_l0, MLA_QK)
    attn_p = _ctx_attn(_mla_ctx_kernel, q_p, k_p, v_p, mla_bound, seq, MLA_HEADS * MLA_V, "l0_mla_ctx_attn")
    attn_s = _mla_lat_attn(q_s, k_s, v_s, k_c, v_c, jnp.full((dec_batch, MLA_HEADS), mla_bound, F32), dec_seq, past)
    xp = _post(xp, attn_p, mod0_p, tiles_p, w0, "l0_post_ctx", pool=(a_p, seq))
    xs = _post(xs, attn_s, mod0_s, tiles_s, w0, "l0_post_latent", pool=(a_s, dec_seq))

    w1 = dict(norm_mix=norm_mix_l1.reshape(1, -1), norm_mlp=norm_mlp_l1.reshape(1, -1),
              w_in=w_in_l1.astype(BF16),
              gq=jnp.tile(na_q_norm_l1 * (NA_HEAD_DIM ** -0.5 * LOG2E), 2).reshape(1, LANES),
              gk=jnp.tile(na_k_norm_l1, 2).reshape(1, LANES),
              w_out=w_out_l1.astype(BF16), w_mlp1=w_mlp1_l1.astype(BF16), w_mlp2=w_mlp2_l1.astype(BF16))
    q1_p, k1_p, v1_p, k1_new, v1_new = _l1_pre(xp, mod1_p, tiles_p, True, w1)
    q1_s, k1_s, v1_s = _l1_pre(xs, mod1_s, tiles_s, False, w1)
    q_len, k_len = _normed_len(w1["gq"], NA_HEAD_DIM), _normed_len(w1["gk"], NA_HEAD_DIM)
    na_p = _ctx_attn(_na_ctx_kernel, q1_p, k1_p, v1_p, q_len * k_len, seq, NA_SLABS * LANES, "l1_na_ctx_attn")
    cache_len = jnp.sqrt(jnp.max(jnp.sum(jnp.square(cache_l1_na_k.astype(F32)), axis=-1), axis=1))
    na_bound = (q_len * jnp.maximum(k_len, cache_len)
                + jnp.max(jnp.abs(rel_bias_l1.astype(F32)) * LOG2E, axis=(1, 2))[None, :])
    na_s = _na_lat_attn(q1_s, k1_s, v1_s,
                        cache_l1_na_k.reshape(dec_batch * past, -1), cache_l1_na_v.reshape(dec_batch * past, -1),
                        _na_bias_table(rel_bias_l1, dec_seq // GRID_W), na_bound, dec_seq, past)
    xp = _post(xp, na_p, mod1_p, tiles_p, w1, "l1_post_ctx")
    xs = _post(xs, na_s, mod1_s, tiles_s, w1, "l1_post_latent")

    return (xp.reshape(batch, seq, D_MODEL), xs.reshape(dec_batch, dec_seq, D_MODEL),
            ckv_p.reshape(batch, seq, KV_LORA), kr_p.reshape(batch, seq, MLA_ROPE),
            k1_new.reshape(batch, seq, NA_HEADS, NA_HEAD_DIM), v1_new.reshape(batch, seq, NA_HEADS, NA_HEAD_DIM))
```

```python
import functools

import jax
import jax.numpy as jnp
from jax import lax
from jax.experimental import pallas as pl
from jax.experimental.pallas import tpu as pltpu

F32 = jnp.float32
BF16 = jnp.bfloat16

D_MODEL = 1024
D_FF = 4 * D_MODEL
DEPTH = 2
N_ADA = 6
EPS = 1e-6
NEG_INF = -1e30
ROPE_THETA = 10000.0
GRID_W = 64
POOL_WIDTH = 512
POOL_WINDOWS = (2, 4, 8, 16)
POOL_GROUP = 128
POOL_HALO = 8
MLA_HEADS = 8
MLA_NOPE = 64
MLA_ROPE = 32
MLA_QK = MLA_NOPE + MLA_ROPE
MLA_V = 64
Q_LORA = 256
KV_LORA = 128
NA_HEADS = 16
NA_HEAD_DIM = 64
NA_KH = 8
NA_KW = 16
LOG2E = 1.4426950408889634

LANES = 128
SUBLANES = 8
VMEM_LIMIT = 56 * 1024 * 1024

TOKEN_TILE = 512
MLA_Q_TILE = 512
MLA_STEP_HEADS = 2
MLA_KEY_CHUNK = 2048
MLA_ONLINE_CHUNK = 1024
SCORE_LOOKAHEAD = 4
MAX_SCORE_BOUND = 40.0
FF_CHUNK = 1024
CTX_STEP_SEQS = 2
NA_SLABS = NA_HEADS * NA_HEAD_DIM // LANES
NA_TILE_ROWS = 4
NA_WIN_ROWS = NA_TILE_ROWS + NA_KH - 1


def _params(n_axes, fuse_inputs=None):
    return pltpu.CompilerParams(dimension_semantics=("arbitrary",) * n_axes,
                                vmem_limit_bytes=VMEM_LIMIT, allow_input_fusion=fuse_inputs)


def _const_spec(shape):
    zeros = (0,) * len(shape)
    return pl.BlockSpec(shape, lambda *_: zeros, pipeline_mode=pl.Buffered(1))


def _dot(a, b):
    return jnp.dot(a, b, preferred_element_type=F32)


def _dot_nt(a, b):
    return lax.dot_general(a, b, (((1,), (1,)), ((), ())), preferred_element_type=F32)


def _dot_tn(a, b):
    return lax.dot_general(a, b, (((0,), (0,)), ((), ())), preferred_element_type=F32)


def _rms(x, w, n=None):
    n = x.shape[-1] if n is None else n
    ss = jnp.sum(x * x, axis=-1, keepdims=True)
    return x * lax.rsqrt(ss * (1.0 / n) + EPS) * w


def _modulate(x, nw, shift, scale):
    return _rms(x, nw) * (1.0 + scale) + shift


def _ada_spec(mod, tiles_per_group):
    mods, first_row = mod
    return pl.BlockSpec((None, 1, mods.shape[-1]), lambda i: (first_row + i // tiles_per_group, 0, 0))


def _ada_row(mod_ref, k):
    return mod_ref[:, k * D_MODEL:(k + 1) * D_MODEL]


def _ada_kernel(cond_ref, w0_ref, w1_ref, b_ref, o_ref):
    c = cond_ref[...]
    s = (c * jax.nn.sigmoid(c)).astype(BF16)
    for layer, w_ref in enumerate((w0_ref, w1_ref)):
        @pl.when(pl.program_id(0) == layer)
        def _(w_ref=w_ref):
            o_ref[...] = _dot(s, w_ref[...].astype(BF16)) + b_ref[...]


def _ada(cond8, w_ada, b_ada):
    n_out = w_ada[0].shape[1]
    tn = D_MODEL
    nb = n_out // tn
    return pl.pallas_call(
        _ada_kernel,
        out_shape=jax.ShapeDtypeStruct((DEPTH, SUBLANES, n_out), F32),
        grid=(DEPTH, nb),
        in_specs=[pl.BlockSpec((SUBLANES, D_MODEL), lambda l, j: (0, 0)),
                  pl.BlockSpec((D_MODEL, tn), lambda l, j: (0, jnp.where(l == 0, j, nb - 1))),
                  pl.BlockSpec((D_MODEL, tn), lambda l, j: (0, jnp.where(l == 1, j, 0))),
                  pl.BlockSpec((None, 1, tn), lambda l, j: (l, 0, j))],
        out_specs=pl.BlockSpec((None, SUBLANES, tn), lambda l, j: (l, 0, j)),
        compiler_params=_params(2),
        name="ada",
    )(cond8, w_ada[0], w_ada[1], jnp.stack(b_ada).reshape(DEPTH, 1, n_out))


def _rope(x, c, s):
    return x * c + pltpu.roll(x, LANES - MLA_ROPE, 1) * s


PAIR = 2 * LANES


def _head_rinv(raw, e2_ref):
    ss = _dot((raw * raw).astype(BF16), e2_ref[...])
    return lax.rsqrt(ss * (1.0 / MLA_QK) + EPS)


def _mla_keys_values(ckv, kr_slab, wk_ref, gk, wv_ref, e2_ref, rope_tabs, k_ref, v_ref):
    c16 = ckv.astype(BF16)
    tail = kr_slab * gk
    if rope_tabs is not None:
        tail = _rope(tail, *rope_tabs)
    ones_lane = (lax.broadcasted_iota(jnp.int32, (1, LANES), 1) == MLA_V).astype(F32)
    for p in range(MLA_HEADS // 2):
        cols = pl.ds(p * PAIR, PAIR)
        kn = _dot(c16, wk_ref[:, cols])
        v = _dot(c16, wv_ref[:, cols])
        rinv = _head_rinv(kn + jnp.tile(kr_slab, (1, 2)), e2_ref)
        for i in range(2):
            hs = slice(i * LANES, (i + 1) * LANES)
            k_ref[2 * p + i] = (rinv[:, hs] * (kn[:, hs] * gk + tail)).astype(BF16)
            v_ref[2 * p + i] = (v[:, hs] + ones_lane).astype(BF16)


def _l0_pre_kernel(latent, x_ref, mod_ref, nw_ref, win_ref, qln_ref, kvn_ref, wq_ref, gq_ref,
                   wk_ref, gk_ref, wv_ref, e2_ref, *rest):
    if latent:
        rope_ref, a_ref, q_ref, k_ref, v_ref, ckv_ref, kr_ref = rest
        tile_rows = x_ref.shape[0] // GRID_W
        row0 = (pl.program_id(0) * tile_rows) % rope_ref.shape[1]
        expand = lambda col, row: jnp.concatenate(
            [rope_ref[col] + rope_ref[row, pl.ds(row0 + r, 1), :] for r in range(tile_rows)], axis=0)
        rope_tabs = (expand(0, 1), expand(2, 3))
    else:
        a_ref, q_ref, k_ref, v_ref, ckv_ref, kr_ref = rest
        rope_tabs = None
    h = _modulate(x_ref[...], nw_ref[...], _ada_row(mod_ref, 0), _ada_row(mod_ref, 1)).astype(BF16)
    proj = _dot(h, win_ref[...])
    a_ref[...] = proj[:, :POOL_WIDTH]
    q_lat = proj[:, POOL_WIDTH:POOL_WIDTH + Q_LORA]
    kv_lat = proj[:, POOL_WIDTH + Q_LORA:POOL_WIDTH + Q_LORA + KV_LORA]
    kr_slab = proj[:, POOL_WIDTH + Q_LORA + KV_LORA:]
    ql = _rms(q_lat, qln_ref[...]).astype(BF16)
    gq = gq_ref[...]
    for p in range(MLA_HEADS // 2):
        q = _dot(ql, wq_ref[:, pl.ds(p * PAIR, PAIR)])
        q_rinv = _head_rinv(q, e2_ref)
        for i in range(2):
            hs = slice(i * LANES, (i + 1) * LANES)
            y = q[:, hs] * q_rinv[:, hs] * gq
            if rope_tabs is not None:
                y = _rope(y, *rope_tabs)
            q_ref[2 * p + i] = y.astype(BF16)
    ckv = _rms(kv_lat, kvn_ref[...])
    ckv_ref[...] = ckv
    kr_ref[...] = kr_slab[:, MLA_NOPE:MLA_QK]
    _mla_keys_values(ckv, kr_slab, wk_ref, gk_ref[...], wv_ref, e2_ref, rope_tabs, k_ref, v_ref)


def _slab_spec(slabs, tm):
    return pl.BlockSpec((slabs, tm, LANES), lambda i: (0, i, 0))


def _l0_pre(x, mod, tiles_per_group, latent, w, rope_tabs):
    n = x.shape[0]
    tm = TOKEN_TILE
    row = lambda i: (i, 0)
    hk = MLA_HEADS * LANES
    in_specs = [pl.BlockSpec((tm, D_MODEL), row),
                _ada_spec(mod, tiles_per_group),
                _const_spec((1, D_MODEL)),
                _const_spec((D_MODEL, D_MODEL)),
                _const_spec((1, Q_LORA)),
                _const_spec((1, KV_LORA)),
                _const_spec((Q_LORA, hk)),
                _const_spec((1, LANES)),
                _const_spec((KV_LORA, hk)),
                _const_spec((1, LANES)),
                _const_spec((KV_LORA, hk)),
                _const_spec((2 * LANES, 2 * LANES))]
    gains = "rope" if latent else "plain"
    args = [x, mod[0], w["norm_mix"], w["w_in"], w["qln"], w["kvn"], w["wq"], w["gq_" + gains], w["wk"],
            w["gk_" + gains], w["wv"], w["e2"]]
    if latent:
        assert tm % GRID_W == 0
        in_specs.append(_const_spec(rope_tabs.shape))
        args.append(rope_tabs)
    slab = jax.ShapeDtypeStruct((MLA_HEADS, n, LANES), BF16)
    out_shape = (jax.ShapeDtypeStruct((n, POOL_WIDTH), F32), slab, slab, slab,
                 jax.ShapeDtypeStruct((n, KV_LORA), F32),
                 jax.ShapeDtypeStruct((n, MLA_ROPE), F32))
    out_specs = (pl.BlockSpec((tm, POOL_WIDTH), row),
                 _slab_spec(MLA_HEADS, tm), _slab_spec(MLA_HEADS, tm), _slab_spec(MLA_HEADS, tm),
                 pl.BlockSpec((tm, KV_LORA), row), pl.BlockSpec((tm, MLA_ROPE), row))
    return pl.pallas_call(
        functools.partial(_l0_pre_kernel, latent),
        out_shape=out_shape, grid=(n // tm,), in_specs=in_specs, out_specs=out_specs,
        compiler_params=_params(1), name="l0_pre_latent" if latent else "l0_pre_ctx",
    )(*args)


def _cache_kv_kernel(ckv_ref, kr_ref, wk_ref, gk_ref, wv_ref, e2_ref, k_ref, v_ref):
    _mla_keys_values(ckv_ref[...], kr_ref[...], wk_ref, gk_ref[...], wv_ref, e2_ref, None, k_ref, v_ref)


def _cache_kv(ckv, kr_slab, w):
    n = ckv.shape[0]
    tm = TOKEN_TILE
    row = lambda i: (i, 0)
    hk = MLA_HEADS * LANES
    slab = jax.ShapeDtypeStruct((MLA_HEADS, n, LANES), BF16)
    return pl.pallas_call(
        _cache_kv_kernel,
        out_shape=(slab, slab),
        grid=(n // tm,),
        in_specs=[pl.BlockSpec((tm, KV_LORA), row), pl.BlockSpec((tm, LANES), row),
                  _const_spec((KV_LORA, hk)), _const_spec((1, LANES)), _const_spec((KV_LORA, hk)),
                  _const_spec((2 * LANES, 2 * LANES))],
        out_specs=(_slab_spec(MLA_HEADS, tm), _slab_spec(MLA_HEADS, tm)),
        compiler_params=_params(1), name="l0_cache_kv",
    )(ckv, kr_slab, w["wk"], w["gk_plain"], w["wv"], w["e2"])


def _pool_mixer(seq_len, tile, prev_ref, u_ref, next_ref, wp_ref, ps_ref, buf_ref):
    tm = u_ref.shape[0]
    piece = min(tm, seq_len)
    n_pieces = tm // piece
    stride = piece + 2 * POOL_HALO
    starts = []
    for j in range(n_pieces):
        pos0 = (tile * tm + j * piece) % seq_len
        prev = prev_ref[...] if j == 0 else u_ref[j * piece - POOL_HALO:j * piece, :]
        nxt = next_ref[...] if j == n_pieces - 1 else u_ref[(j + 1) * piece:(j + 1) * piece + POOL_HALO, :]
        base = j * stride
        buf_ref[base:base + POOL_HALO, :] = jnp.where(pos0 != 0, prev, 0.0)
        buf_ref[base + POOL_HALO:base + POOL_HALO + piece, :] = u_ref[j * piece:(j + 1) * piece, :]
        buf_ref[base + POOL_HALO + piece:base + stride, :] = jnp.where(pos0 + piece != seq_len, nxt, 0.0)
        starts.append(pos0)
    def group(g):
        win = POOL_WINDOWS[g]
        lanes = pl.ds(g * POOL_GROUP, POOL_GROUP)
        pooled = []
        for j in range(n_pieces):
            centre = j * stride + POOL_HALO
            lo = starts[j] + lax.broadcasted_iota(jnp.int32, (piece, 1), 0) - win // 2
            cnt = (jnp.minimum(lo + win, seq_len) - jnp.maximum(lo, 0)).astype(F32)
            acc = buf_ref[pl.ds(centre - win // 2, piece), lanes]
            for d in range(1 - win // 2, win - win // 2):
                acc = acc + buf_ref[pl.ds(centre + d, piece), lanes]
            pooled.append(acc / cnt - buf_ref[pl.ds(centre, piece), lanes])
        pooled = pooled[0] if n_pieces == 1 else jnp.concatenate(pooled, axis=0)
        return (_dot(pooled.astype(BF16), wp_ref[g]) * ps_ref[:, lanes]).astype(BF16)

    return [functools.partial(group, g) for g in range(len(POOL_WINDOWS))]


def _attend(streams, running_max):
    items = [(si, ch) for si, (_, chunks, _) in enumerate(streams) for ch in chunks]

    def score(item):
        si, (k_fn, _, extra_fn) = item
        s = _dot_nt(k_fn(), streams[si][0])
        return s if extra_fn is None else s + extra_fn()

    state = [None] * len(streams)
    pending = [score(item) for item in items[:SCORE_LOOKAHEAD]]
    for i, (si, (_, v_fn, _)) in enumerate(items):
        s = pending.pop(0)
        if i + SCORE_LOOKAHEAD < len(items):
            pending.append(score(items[i + SCORE_LOOKAHEAD]))
        want_den = streams[si][2]
        m_new, alpha = None, 1.0
        if state[si] is not None:
            m, acc, den = state[si]
        if running_max:
            m_new = jnp.max(s, axis=0, keepdims=True)
            if state[si] is not None:
                m_new = jnp.maximum(m, m_new)
                alpha = jnp.exp2(m - m_new)
            s = s - m_new
        p = jnp.exp2(s)
        pv = _dot_tn(v_fn(), p.astype(BF16))
        psum = jnp.sum(p, axis=0, keepdims=True) if want_den else None
        if state[si] is None:
            state[si] = (m_new, pv, psum)
        elif running_max:
            state[si] = (m_new, acc * alpha + pv, den * alpha + psum if want_den else None)
        else:
            state[si] = (None, acc + pv, den + psum if want_den else None)
    return [(acc, den) for _, acc, den in state]


def _ref_chunks(k_ref, v_ref, slab, chunk, extra_fn=None, lo=0, hi=None):
    hi = k_ref.shape[1] if hi is None else hi
    out = []
    for c in range(lo, hi, chunk):
        rows = pl.ds(c, min(chunk, hi - c))
        out.append((lambda rows=rows: k_ref[slab, rows, :], lambda rows=rows: v_ref[slab, rows, :], extra_fn))
    return out


def _mla_finish(o_t):
    return o_t[:MLA_V] / o_t[MLA_V:MLA_V + 1]


def _na_queries(q2):
    low = lax.broadcasted_iota(jnp.int32, (1, LANES), 1) < NA_HEAD_DIM
    zero = jnp.zeros_like(q2)
    return jnp.concatenate([jnp.where(low, q2, zero), jnp.where(low, zero, q2)], axis=0)


def _na_finish(o_t, den):
    nq = o_t.shape[1] // 2
    o = o_t / den
    return jnp.concatenate([o[:NA_HEAD_DIM, :nq], o[NA_HEAD_DIM:, nq:]], axis=0)


def _attend_guarded(bound, streams_fn):
    return lax.cond(bound <= MAX_SCORE_BOUND,
                    lambda: _attend(streams_fn(False), False),
                    lambda: _attend(streams_fn(True), True))


def _mla_ctx_kernel(seq, bound_ref, q_ref, k_ref, v_ref, o_ref):
    spans = [(lo, lo + seq) for lo in range(0, k_ref.shape[1], seq)]
    res = _attend_guarded(bound_ref[0, 0], lambda _: [
        (q_ref[h, lo:hi, :], _ref_chunks(k_ref, v_ref, h, seq, lo=lo, hi=hi), False)
        for lo, hi in spans for h in range(MLA_HEADS)])
    for i, (lo, hi) in enumerate(spans):
        for hp in range(MLA_HEADS // 2):
            o_t = jnp.concatenate([_mla_finish(res[i * MLA_HEADS + h][0]) for h in (2 * hp, 2 * hp + 1)], axis=0)
            o_ref[lo:hi, hp * LANES:(hp + 1) * LANES] = o_t.T.astype(BF16)


def _na_ctx_kernel(seq, bound_ref, q_ref, k_ref, v_ref, o_ref):
    spans = [(lo, lo + seq) for lo in range(0, k_ref.shape[1], seq)]
    res = _attend_guarded(bound_ref[0, 0], lambda _: [
        (_na_queries(q_ref[s, lo:hi, :]), _ref_chunks(k_ref, v_ref, s, seq, lo=lo, hi=hi), True)
        for lo, hi in spans for s in range(NA_SLABS)])
    for i, (lo, hi) in enumerate(spans):
        for s in range(NA_SLABS):
            o_ref[lo:hi, s * LANES:(s + 1) * LANES] = _na_finish(*res[i * NA_SLABS + s]).T.astype(BF16)


def _ctx_attn(body, q, k, v, bound, seq, width, name):
    slabs, n, _ = q.shape
    rows = CTX_STEP_SEQS * seq
    spec = pl.BlockSpec((slabs, rows, LANES), lambda b: (0, b, 0))
    return pl.pallas_call(
        functools.partial(body, seq),
        out_shape=jax.ShapeDtypeStruct((n, width), BF16),
        grid=(n // rows,),
        in_specs=[pl.BlockSpec(memory_space=pltpu.SMEM), spec, spec, spec],
        out_specs=pl.BlockSpec((rows, width), lambda b: (b, 0)),
        compiler_params=_params(1), name=name,
    )(bound.reshape(1, 1), q, k, v)


def _mla_lat_kernel(bound_ref, q_ref, kl_ref, vl_ref, kc_ref, vc_ref, o_ref):
    batch, head0 = pl.program_id(0), pl.program_id(1) * MLA_STEP_HEADS
    heads = range(MLA_STEP_HEADS)
    bounds = [bound_ref[batch, head0 + h] for h in heads]

    def streams(running_max):
        size = MLA_ONLINE_CHUNK if running_max else MLA_KEY_CHUNK
        return [(q_ref[h], _ref_chunks(kl_ref, vl_ref, h, size) + _ref_chunks(kc_ref, vc_ref, h, size), False)
                for h in heads]

    res = _attend_guarded(functools.reduce(jnp.maximum, bounds), streams)
    o_ref[...] = jnp.concatenate([_mla_finish(o) for o, _ in res], axis=0).T.astype(BF16)


def _normed_len(gain, dim):
    return (dim ** 0.5) * jnp.max(jnp.abs(gain.astype(F32)))


def _mla_lat_attn(q, k, v, kc, vc, bound, seq, past):
    n = q.shape[1]
    tq = MLA_Q_TILE
    nq = seq // tq
    hs = MLA_STEP_HEADS
    return pl.pallas_call(
        _mla_lat_kernel,
        out_shape=jax.ShapeDtypeStruct((n, MLA_HEADS * MLA_V), BF16),
        grid=(n // seq, MLA_HEADS // hs, nq),
        in_specs=[pl.BlockSpec(memory_space=pltpu.SMEM),
                  pl.BlockSpec((hs, tq, LANES), lambda b, h, i: (h, b * nq + i, 0)),
                  pl.BlockSpec((hs, seq, LANES), lambda b, h, i: (h, b, 0)),
                  pl.BlockSpec((hs, seq, LANES), lambda b, h, i: (h, b, 0)),
                  pl.BlockSpec((hs, past, LANES), lambda b, h, i: (h, b, 0)),
                  pl.BlockSpec((hs, past, LANES), lambda b, h, i: (h, b, 0))],
        out_specs=pl.BlockSpec((tq, hs * MLA_V), lambda b, h, i: (b * nq + i, h)),
        compiler_params=_params(3), name="l0_mla_latent_attn",
    )(bound, q, k, v, kc, vc)


def _na_first_row(r, rows):
    return jnp.clip(r - NA_KH // 2, 0, rows - NA_KH)


def _na_lat_kernel(rows, bound_ref, q_ref, k_ref, v_ref, kc_ref, vc_ref, t2_ref, o_ref):
    nq = NA_TILE_ROWS * GRID_W
    nk = NA_WIN_ROWS * GRID_W
    n_off = 2 * NA_KH - 1
    r0 = pl.program_id(1) * NA_TILE_ROWS
    w0 = jnp.clip(r0 - NA_KH // 2, 0, rows - NA_WIN_ROWS)

    def block_index(j, t):
        key_row = w0 + j
        i = r0 + 2 * t
        first_u, first_l = _na_first_row(i, rows), _na_first_row(i + 1, rows)
        in_u = (key_row >= first_u) & (key_row < first_u + NA_KH)
        in_l = (key_row >= first_l) & (key_row < first_l + NA_KH)
        d = jnp.clip(key_row - i + NA_KH - 1, 0, n_off - 1)
        return jnp.where(in_u & in_l, d, jnp.where(in_u, n_off, jnp.where(in_l, n_off + 1, n_off + 2)))

    idx = [[block_index(j, t) for t in range(NA_TILE_ROWS // 2)] for j in range(NA_WIN_ROWS)]

    def local_bias(s):
        heads = [jnp.concatenate([jnp.concatenate([t2_ref[2 * s + hh, i] for i in row], axis=1) for row in idx], axis=0)
                 for hh in range(2)]
        return jnp.concatenate(heads, axis=1)

    batch = pl.program_id(0)
    bounds = [bound_ref[batch, h] for h in range(NA_HEADS)]

    def cache_chunk(s):
        lanes = pl.ds(s * LANES, LANES)
        return [(lambda: kc_ref[:, lanes].astype(BF16), lambda: vc_ref[:, lanes].astype(BF16), None)]

    res = _attend_guarded(
        functools.reduce(jnp.maximum, bounds),
        lambda _: [(_na_queries(q_ref[s]),
                    _ref_chunks(k_ref, v_ref, s, nk, functools.partial(local_bias, s)) + cache_chunk(s),
                    True) for s in range(NA_SLABS)])
    for s in range(NA_SLABS):
        o_ref[:, s * LANES:(s + 1) * LANES] = _na_finish(*res[s]).T.astype(BF16)


def _na_lat_attn(q, k, v, kc, vc, t2, bound, seq, past):
    slabs, n, _ = q.shape
    rows = seq // GRID_W
    tiles = rows // NA_TILE_ROWS
    nq = NA_TILE_ROWS * GRID_W
    nk = NA_WIN_ROWS * GRID_W

    def win_map(b, t):
        w0 = jnp.clip(t * NA_TILE_ROWS - NA_KH // 2, 0, rows - NA_WIN_ROWS)
        return (0, pl.multiple_of(b * seq + w0 * GRID_W, GRID_W), 0)

    win_spec = pl.BlockSpec((pl.Element(slabs), pl.Element(nk), pl.Element(LANES)), win_map)
    ctx_spec = pl.BlockSpec((past, slabs * LANES), lambda b, t: (b, 0))
    return pl.pallas_call(
        functools.partial(_na_lat_kernel, rows),
        out_shape=jax.ShapeDtypeStruct((n, slabs * LANES), BF16),
        grid=(n // seq, tiles),
        in_specs=[pl.BlockSpec(memory_space=pltpu.SMEM),
                  pl.BlockSpec((slabs, nq, LANES), lambda b, t: (0, b * tiles + t, 0)),
                  win_spec, win_spec, ctx_spec, ctx_spec,
                  _const_spec(t2.shape)],
        out_specs=pl.BlockSpec((nq, slabs * LANES), lambda b, t: (b * tiles + t, 0)),
        compiler_params=_params(2), name="l1_na_latent_attn",
    )(bound, q, k, v, kc, vc, t2)


def _pair_norm(x2, g2):
    low = lax.broadcasted_iota(jnp.int32, (1, LANES), 1) < NA_HEAD_DIM
    sq = x2 * x2
    ss_all = jnp.sum(sq, axis=-1, keepdims=True)
    ss_low = jnp.sum(jnp.where(low, sq, 0.0), axis=-1, keepdims=True)
    ss = jnp.where(low, ss_low, ss_all - ss_low)
    return x2 * lax.rsqrt(ss * (1.0 / NA_HEAD_DIM) + EPS) * g2


def _l1_pre_kernel(keep_f32, x_ref, mod_ref, nw_ref, win_ref, gq_ref, gk_ref, q_ref, k_ref, v_ref, *f32_refs):
    h = _modulate(x_ref[...], nw_ref[...], _ada_row(mod_ref, 0), _ada_row(mod_ref, 1)).astype(BF16)
    width = NA_SLABS * LANES
    q = _dot(h, win_ref[:, 0:width])
    k = _dot(h, win_ref[:, width:2 * width])
    v = _dot(h, win_ref[:, 2 * width:3 * width])
    gq = gq_ref[...]
    gk = gk_ref[...]
    for s in range(NA_SLABS):
        ls = slice(s * LANES, (s + 1) * LANES)
        q_ref[s] = _pair_norm(q[:, ls], gq).astype(BF16)
        kn = _pair_norm(k[:, ls], gk)
        k_ref[s] = kn.astype(BF16)
        v_ref[s] = v[:, ls].astype(BF16)
        if keep_f32:
            f32_refs[0][:, ls] = kn
    if keep_f32:
        f32_refs[1][...] = v


def _l1_pre(x, mod, tiles_per_group, keep_f32, w):
    n = x.shape[0]
    tm = TOKEN_TILE
    row = lambda i: (i, 0)
    width = NA_SLABS * LANES
    slab = jax.ShapeDtypeStruct((NA_SLABS, n, LANES), BF16)
    out_shape = [slab, slab, slab]
    out_specs = [_slab_spec(NA_SLABS, tm)] * 3
    if keep_f32:
        out_shape += [jax.ShapeDtypeStruct((n, width), F32)] * 2
        out_specs += [pl.BlockSpec((tm, width), row)] * 2
    return pl.pallas_call(
        functools.partial(_l1_pre_kernel, keep_f32),
        out_shape=tuple(out_shape), grid=(n // tm,),
        in_specs=[pl.BlockSpec((tm, D_MODEL), row),
                  _ada_spec(mod, tiles_per_group),
                  _const_spec((1, D_MODEL)),
                  _const_spec((D_MODEL, 3 * width)),
                  _const_spec((1, LANES)),
                  _const_spec((1, LANES))],
        out_specs=tuple(out_specs),
        compiler_params=_params(1), name="l1_pre_ctx" if keep_f32 else "l1_pre_latent",
    )(x, mod[0], w["norm_mix"], w["w_in"], w["gq"], w["gk"])


def _post_kernel(pool_seq, x_ref, attn_ref, *rest):
    if pool_seq is None:
        mod_ref, nw_ref, wo_ref, w1_ref, w2_ref, o_ref = rest
        mixed = _dot(attn_ref[...], wo_ref[...])
    else:
        (first_refs, next_refs, wp_ref, ps_ref, mod_ref, nw_ref, wo_ref, w1_ref, w2_ref, o_ref,
         buf_ref, pooled_ref) = rest[0:3], rest[3:6], *rest[6:]
        step = pl.program_id(0)

        @pl.when(step == 0)
        def _():
            for g, group in enumerate(_pool_mixer(pool_seq, 0, *first_refs, wp_ref, ps_ref, buf_ref)):
                pooled_ref[:, g * POOL_GROUP:(g + 1) * POOL_GROUP] = group()

        mixed = _dot(jnp.concatenate([pooled_ref[...], attn_ref[...]], axis=-1), wo_ref[...])
        next_groups = _pool_mixer(pool_seq, jnp.minimum(step + 1, pl.num_programs(0) - 1), *next_refs,
                                  wp_ref, ps_ref, buf_ref)
    x1 = x_ref[...] + _ada_row(mod_ref, 2) * mixed
    h = _modulate(x1, nw_ref[...], _ada_row(mod_ref, 3), _ada_row(mod_ref, 4)).astype(BF16)
    acc = jnp.zeros(x1.shape, F32)
    n_chunks = D_FF // FF_CHUNK
    for c in range(n_chunks):
        cs = pl.ds(c * FF_CHUNK, FF_CHUNK)
        u = jnp.square(jnp.maximum(_dot(h, w1_ref[:, cs]), 0.0)).astype(BF16)
        acc = acc + _dot(u, w2_ref[cs, :])
        if pool_seq is not None:
            for g in range(c * len(next_groups) // n_chunks, (c + 1) * len(next_groups) // n_chunks):
                pooled_ref[:, g * POOL_GROUP:(g + 1) * POOL_GROUP] = next_groups[g]()
    o_ref[...] = x1 + _ada_row(mod_ref, 5) * acc


def _post(x, attn, mod, tiles_per_group, w, name, pool=None):
    n = x.shape[0]
    tm = TOKEN_TILE
    row = lambda i: (i, 0)
    in_specs = [pl.BlockSpec((tm, D_MODEL), row), pl.BlockSpec((tm, attn.shape[1]), row)]
    args = [x, attn]
    scratch = []
    pool_seq = None
    if pool is not None:
        a_in, pool_seq = pool
        hb = tm // POOL_HALO
        last = n // POOL_HALO - 1
        pieces = tm // min(tm, pool_seq)
        last_tile = n // tm - 1

        def tile_specs(tile_of):
            return [pl.BlockSpec((POOL_HALO, POOL_WIDTH), lambda i: (jnp.maximum(tile_of(i) * hb - 1, 0), 0)),
                    pl.BlockSpec((tm, POOL_WIDTH), lambda i: (tile_of(i), 0)),
                    pl.BlockSpec((POOL_HALO, POOL_WIDTH), lambda i: (jnp.minimum((tile_of(i) + 1) * hb, last), 0))]

        in_specs += tile_specs(lambda i: 0) + tile_specs(lambda i: jnp.minimum(i + 1, last_tile))
        in_specs += [_const_spec((len(POOL_WINDOWS), POOL_GROUP, POOL_GROUP)), _const_spec((1, POOL_WIDTH))]
        args += [a_in] * 6 + [w["w_pool"], w["pool_scale"]]
        scratch = [pltpu.VMEM((tm + 2 * POOL_HALO * pieces, POOL_WIDTH), F32), pltpu.VMEM((tm, POOL_WIDTH), BF16)]
    in_specs += [_ada_spec(mod, tiles_per_group),
                 _const_spec((1, D_MODEL)),
                 _const_spec((D_MODEL, D_MODEL)),
                 _const_spec((D_MODEL, D_FF)),
                 _const_spec((D_FF, D_MODEL))]
    args += [mod[0], w["norm_mlp"], w["w_out"], w["w_mlp1"], w["w_mlp2"]]
    return pl.pallas_call(
        functools.partial(_post_kernel, pool_seq),
        out_shape=jax.ShapeDtypeStruct((n, D_MODEL), F32), grid=(n // tm,),
        in_specs=in_specs, out_specs=pl.BlockSpec((tm, D_MODEL), row),
        scratch_shapes=scratch, name=name,
        compiler_params=_params(1, fuse_inputs=[False] * (len(args) - 3) + [True] * 3),
    )(*args)


def _rope_tables(seq):
    assert seq == GRID_W * GRID_W
    half = MLA_ROPE // 2
    inv_freq = jnp.power(ROPE_THETA, -jnp.arange(0, half, 2, dtype=F32) / half)
    ang = jnp.arange(GRID_W, dtype=F32)[:, None] * inv_freq[None, :]
    cos, sin = jnp.cos(ang), jnp.sin(ang)
    fill = lambda n, v=0.0: jnp.full((GRID_W, n), v, F32)
    spare = LANES - MLA_QK
    return jnp.stack([
        jnp.concatenate([fill(MLA_NOPE, 1.0), fill(half), cos, cos, fill(spare)], axis=1),
        jnp.concatenate([fill(MLA_NOPE), cos, cos, fill(half), fill(spare)], axis=1),
        jnp.concatenate([fill(MLA_NOPE), fill(half), -sin, sin, fill(spare)], axis=1),
        jnp.concatenate([fill(MLA_NOPE), -sin, sin, fill(half), fill(spare)], axis=1)])


def _rope_partner(tail):
    shape = tail.shape
    t = tail.reshape(shape[:-1] + (2, 2, MLA_ROPE // 4))
    return t[..., ::-1, :].reshape(shape)


def _na_bias_table(rel_bias, rows):
    qcol = jnp.arange(GRID_W)
    kcol = jnp.arange(GRID_W)
    col_start = jnp.clip(qcol - NA_KW // 2, 0, GRID_W - NA_KW)
    col_mask = (kcol[:, None] >= col_start[None, :]) & (kcol[:, None] < col_start[None, :] + NA_KW)
    offs = jnp.arange(2 * NA_KW - 1)
    onehot = (kcol[None, :, None] - qcol[None, None, :] + NA_KW - 1 == offs[:, None, None]).astype(F32)
    blocks = jnp.einsum('hdj,jkq->hdkq', rel_bias.astype(F32) * LOG2E, onehot, precision=lax.Precision.HIGHEST)
    blocks = jnp.where(col_mask[None, None], blocks, NEG_INF)
    neg = jnp.full_like(blocks[:, :1], NEG_INF)
    both = jnp.concatenate([blocks, jnp.concatenate([neg, blocks[:, :-1]], axis=1)], axis=-1)
    d_upper, d_lower = _na_partial_offsets(rows)
    upper_only = jnp.concatenate([blocks[:, d_upper:d_upper + 1], neg], axis=-1)
    lower_only = jnp.concatenate([neg, blocks[:, d_lower - 1:d_lower]], axis=-1)
    return jnp.concatenate([both, upper_only, lower_only, jnp.concatenate([neg, neg], axis=-1)], axis=1)


def _na_partial_offsets(rows):
    first = lambda r: min(max(r - NA_KH // 2, 0), rows - NA_KH)
    upper, lower = set(), set()
    for r0 in range(0, rows, NA_TILE_ROWS):
        w0 = min(max(r0 - NA_KH // 2, 0), rows - NA_WIN_ROWS)
        for j in range(NA_WIN_ROWS):
            for t in range(NA_TILE_ROWS // 2):
                key_row, i = w0 + j, r0 + 2 * t
                in_u = first(i) <= key_row < first(i) + NA_KH
                in_l = first(i + 1) <= key_row < first(i + 1) + NA_KH
                d = key_row - i + NA_KH - 1
                if in_u and in_l:
                    assert 1 <= d <= 2 * NA_KH - 2
                elif in_u:
                    upper.add(d)
                elif in_l:
                    lower.add(d)
    assert len(upper) == 1 and len(lower) == 1, (upper, lower)
    return upper.pop(), lower.pop()


def _prep_l0(w_in, q_lora_norm, kv_lora_norm, w_q_up, w_kv_up, mla_q_norm, mla_k_norm, w_pool, pool_scale,
             w_out, norm_mix, norm_mlp, w_mlp1, w_mlp2):
    lat_end = POOL_WIDTH + Q_LORA + KV_LORA
    w_in_pad = jnp.concatenate([w_in[:, :lat_end], jnp.zeros((D_MODEL, MLA_NOPE), w_in.dtype),
                                w_in[:, lat_end:], _rope_partner(w_in[:, lat_end:])], axis=1)
    pad = LANES - MLA_QK
    hk = MLA_HEADS * LANES
    wq = w_q_up.reshape(Q_LORA, MLA_HEADS, MLA_QK)
    wq = jnp.concatenate([wq, _rope_partner(wq[:, :, MLA_NOPE:])], axis=-1)
    kv = w_kv_up.reshape(KV_LORA, MLA_HEADS, MLA_NOPE + MLA_V)
    wk = jnp.pad(kv[:, :, :MLA_NOPE], ((0, 0), (0, 0), (0, LANES - MLA_NOPE)))
    wv = jnp.pad(kv[:, :, MLA_NOPE:], ((0, 0), (0, 0), (0, LANES - MLA_V)))
    gq = mla_q_norm * (MLA_QK ** -0.5 * LOG2E)
    slab_lane = jnp.arange(2 * LANES)
    e2 = (slab_lane[:, None] // LANES == slab_lane[None, :] // LANES) & (slab_lane[:, None] % LANES < MLA_QK)
    return dict(
        norm_mix=norm_mix.reshape(1, -1), norm_mlp=norm_mlp.reshape(1, -1),
        w_in=w_in_pad.astype(BF16),
        qln=q_lora_norm.reshape(1, -1), kvn=kv_lora_norm.reshape(1, -1),
        wq=wq.reshape(Q_LORA, hk).astype(BF16),
        gq_plain=jnp.pad(gq, (0, pad)).reshape(1, LANES),
        gq_rope=jnp.concatenate([gq, _rope_partner(gq[MLA_NOPE:])]).reshape(1, LANES),
        wk=wk.reshape(KV_LORA, hk).astype(BF16),
        gk_plain=jnp.pad(mla_k_norm, (0, pad)).reshape(1, LANES),
        gk_rope=jnp.concatenate([mla_k_norm, _rope_partner(mla_k_norm[MLA_NOPE:])]).reshape(1, LANES),
        wv=wv.reshape(KV_LORA, hk).astype(BF16),
        e2=e2.astype(BF16),
        w_pool=w_pool.astype(BF16), pool_scale=pool_scale.reshape(1, -1),
        w_out=w_out.astype(BF16), w_mlp1=w_mlp1.astype(BF16), w_mlp2=w_mlp2.astype(BF16))


def kernel(x_prompt, x_sample, cache_l0_mla_ckv, cache_l0_mla_krope, cache_l1_na_k, cache_l1_na_v, c, c_ctx, w_ada_l0, b_ada_l0, norm_mix_l0, norm_mlp_l0, w_mlp1_l0, w_mlp2_l0, w_in_l0, q_lora_norm_l0, kv_lora_norm_l0, w_q_up_l0, w_kv_up_l0, mla_q_norm_l0, mla_k_norm_l0, w_pool_l0, pool_scale_l0, w_out_l0, w_ada_l1, b_ada_l1, norm_mix_l1, norm_mlp_l1, w_mlp1_l1, w_mlp2_l1, w_in_l1, na_q_norm_l1, na_k_norm_l1, rel_bias_l1, w_out_l1):
    batch, seq, _ = x_prompt.shape
    dec_batch, dec_seq, _ = x_sample.shape
    past = cache_l0_mla_ckv.shape[1]
    xp = x_prompt.reshape(batch * seq, D_MODEL)
    xs = x_sample.reshape(dec_batch * dec_seq, D_MODEL)
    tiles_p = (batch * seq) // TOKEN_TILE
    tiles_s = dec_seq // TOKEN_TILE

    cond8 = jnp.concatenate([c_ctx[None, :], c, jnp.zeros((SUBLANES - 1 - dec_batch, D_MODEL), F32)], axis=0)
    mods = _ada(cond8, (w_ada_l0, w_ada_l1), (b_ada_l0, b_ada_l1))
    mods = mods.reshape(DEPTH, SUBLANES, 1, N_ADA * D_MODEL)
    (mod0_p, mod0_s), (mod1_p, mod1_s) = (((mods[l], 0), (mods[l], 1)) for l in range(DEPTH))

    w0 = _prep_l0(w_in_l0, q_lora_norm_l0, kv_lora_norm_l0, w_q_up_l0, w_kv_up_l0, mla_q_norm_l0, mla_k_norm_l0,
                  w_pool_l0, pool_scale_l0, w_out_l0, norm_mix_l0, norm_mlp_l0, w_mlp1_l0, w_mlp2_l0)
    a_p, q_p, k_p, v_p, ckv_p, kr_p = _l0_pre(xp, mod0_p, tiles_p, False, w0, None)
    a_s, q_s, k_s, v_s, _, _ = _l0_pre(xs, mod0_s, tiles_s, True, w0, _rope_tables(dec_seq))
    kr_cache = jnp.pad(cache_l0_mla_krope.reshape(dec_batch * past, MLA_ROPE), ((0, 0), (MLA_NOPE, LANES - MLA_QK)))
    k_c, v_c = _cache_kv(cache_l0_mla_ckv.reshape(dec_batch * past, KV_LORA), kr_cache, w0)
    mla_bound = _normed_len(mla_q_norm_l0 * (MLA_QK ** -0.5 * LOG2E), MLA_QK) * _normed_len(mla_k_norm_l0, MLA_QK)
    attn_p = _ctx_attn(_mla_ctx_kernel, q_p, k_p, v_p, mla_bound, seq, MLA_HEADS * MLA_V, "l0_mla_ctx_attn")
    attn_s = _mla_lat_attn(q_s, k_s, v_s, k_c, v_c, jnp.full((dec_batch, MLA_HEADS), mla_bound, F32), dec_seq, past)
    xp = _post(xp, attn_p, mod0_p, tiles_p, w0, "l0_post_ctx", pool=(a_p, seq))
    xs = _post(xs, attn_s, mod0_s, tiles_s, w0, "l0_post_latent", pool=(a_s, dec_seq))

    w1 = dict(norm_mix=norm_mix_l1.reshape(1, -1), norm_mlp=norm_mlp_l1.reshape(1, -1),
              w_in=w_in_l1.astype(BF16),
              gq=jnp.tile(na_q_norm_l1 * (NA_HEAD_DIM ** -0.5 * LOG2E), 2).reshape(1, LANES),
              gk=jnp.tile(na_k_norm_l1, 2).reshape(1, LANES),
              w_out=w_out_l1.astype(BF16), w_mlp1=w_mlp1_l1.astype(BF16), w_mlp2=w_mlp2_l1.astype(BF16))
    q1_p, k1_p, v1_p, k1_new, v1_new = _l1_pre(xp, mod1_p, tiles_p, True, w1)
    q1_s, k1_s, v1_s = _l1_pre(xs, mod1_s, tiles_s, False, w1)
    q_len, k_len = _normed_len(w1["gq"], NA_HEAD_DIM), _normed_len(w1["gk"], NA_HEAD_DIM)
    na_p = _ctx_attn(_na_ctx_kernel, q1_p, k1_p, v1_p, q_len * k_len, seq, NA_SLABS * LANES, "l1_na_ctx_attn")
    cache_len = jnp.sqrt(jnp.max(jnp.sum(jnp.square(cache_l1_na_k.astype(F32)), axis=-1), axis=1))
    na_bound = (q_len * jnp.maximum(k_len, cache_len)
                + jnp.max(jnp.abs(rel_bias_l1.astype(F32)) * LOG2E, axis=(1, 2))[None, :])
    na_s = _na_lat_attn(q1_s, k1_s, v1_s,
                        cache_l1_na_k.reshape(dec_batch * past, -1), cache_l1_na_v.reshape(dec_batch * past, -1),
                        _na_bias_table(rel_bias_l1, dec_seq // GRID_W), na_bound, dec_seq, past)
    xp = _post(xp, na_p, mod1_p, tiles_p, w1, "l1_post_ctx")
    xs = _post(xs, na_s, mod1_s, tiles_s, w1, "l1_post_latent")

    return (xp.reshape(batch, seq, D_MODEL), xs.reshape(dec_batch, dec_seq, D_MODEL),
            ckv_p.reshape(batch, seq, KV_LORA), kr_p.reshape(batch, seq, MLA_ROPE),
            k1_new.reshape(batch, seq, NA_HEADS, NA_HEAD_DIM), v1_new.reshape(batch, seq, NA_HEADS, NA_HEAD_DIM))
```

```python
import functools

import jax
import jax.numpy as jnp
from jax import lax
from jax.experimental import pallas as pl
from jax.experimental.pallas import tpu as pltpu

F32 = jnp.float32
BF16 = jnp.bfloat16

D_MODEL = 1024
D_FF = 4 * D_MODEL
DEPTH = 2
N_ADA = 6
EPS = 1e-6
NEG_INF = -1e30
ROPE_THETA = 10000.0
GRID_W = 64
POOL_WIDTH = 512
POOL_WINDOWS = (2, 4, 8, 16)
POOL_GROUP = 128
POOL_HALO = 8
MLA_HEADS = 8
MLA_NOPE = 64
MLA_ROPE = 32
MLA_QK = MLA_NOPE + MLA_ROPE
MLA_V = 64
Q_LORA = 256
KV_LORA = 128
NA_HEADS = 16
NA_HEAD_DIM = 64
NA_KH = 8
NA_KW = 16
LOG2E = 1.4426950408889634

LANES = 128
SUBLANES = 8
VMEM_LIMIT = 56 * 1024 * 1024

TOKEN_TILE = 512
MLA_Q_TILE = 512
MLA_STEP_HEADS = 2
MLA_KEY_CHUNK = 2048
MLA_ONLINE_CHUNK = 1024
SCORE_LOOKAHEAD = 4
MAX_SCORE_BOUND = 40.0
FF_CHUNK = 1024
CTX_STEP_SEQS = 2
NA_SLABS = NA_HEADS * NA_HEAD_DIM // LANES
NA_TILE_ROWS = 4
NA_WIN_ROWS = NA_TILE_ROWS + NA_KH - 1


def _params(n_axes):
    return pltpu.CompilerParams(dimension_semantics=("arbitrary",) * n_axes,
                                vmem_limit_bytes=VMEM_LIMIT)


def _const_spec(shape):
    zeros = (0,) * len(shape)
    return pl.BlockSpec(shape, lambda *_: zeros, pipeline_mode=pl.Buffered(1))


def _dot(a, b):
    return jnp.dot(a, b, preferred_element_type=F32)


def _dot_nt(a, b):
    return lax.dot_general(a, b, (((1,), (1,)), ((), ())), preferred_element_type=F32)


def _dot_tn(a, b):
    return lax.dot_general(a, b, (((0,), (0,)), ((), ())), preferred_element_type=F32)


def _rms(x, w, n=None):
    n = x.shape[-1] if n is None else n
    ss = jnp.sum(x * x, axis=-1, keepdims=True)
    return x * lax.rsqrt(ss * (1.0 / n) + EPS) * w


def _modulate(x, nw, shift, scale):
    return _rms(x, nw) * (1.0 + scale) + shift


def _ada_spec(mod, tiles_per_group):
    mods, first_row = mod
    return pl.BlockSpec((None, 1, mods.shape[-1]), lambda i: (first_row + i // tiles_per_group, 0, 0))


def _ada_row(mod_ref, k):
    return mod_ref[:, k * D_MODEL:(k + 1) * D_MODEL]


def _ada_kernel(cond_ref, w0_ref, w1_ref, b_ref, o_ref):
    c = cond_ref[...]
    s = (c * jax.nn.sigmoid(c)).astype(BF16)
    for layer, w_ref in enumerate((w0_ref, w1_ref)):
        @pl.when(pl.program_id(0) == layer)
        def _(w_ref=w_ref):
            o_ref[...] = _dot(s, w_ref[...].astype(BF16)) + b_ref[...]


def _ada(cond8, w_ada, b_ada):
    n_out = w_ada[0].shape[1]
    tn = D_MODEL
    nb = n_out // tn
    return pl.pallas_call(
        _ada_kernel,
        out_shape=jax.ShapeDtypeStruct((DEPTH, SUBLANES, n_out), F32),
        grid=(DEPTH, nb),
        in_specs=[pl.BlockSpec((SUBLANES, D_MODEL), lambda l, j: (0, 0)),
                  pl.BlockSpec((D_MODEL, tn), lambda l, j: (0, jnp.where(l == 0, j, nb - 1))),
                  pl.BlockSpec((D_MODEL, tn), lambda l, j: (0, jnp.where(l == 1, j, 0))),
                  pl.BlockSpec((None, 1, tn), lambda l, j: (l, 0, j))],
        out_specs=pl.BlockSpec((None, SUBLANES, tn), lambda l, j: (l, 0, j)),
        compiler_params=_params(2),
        name="ada",
    )(cond8, w_ada[0], w_ada[1], jnp.stack(b_ada).reshape(DEPTH, 1, n_out))


def _rope(x, c, s):
    return x * c + pltpu.roll(x, LANES - MLA_ROPE, 1) * s


PAIR = 2 * LANES


def _head_rinv(raw, e2_ref):
    ss = _dot((raw * raw).astype(BF16), e2_ref[...])
    return lax.rsqrt(ss * (1.0 / MLA_QK) + EPS)


def _mla_keys_values(ckv, kr_slab, wk_ref, gk, wv_ref, e2_ref, rope_tabs, k_ref, v_ref):
    c16 = ckv.astype(BF16)
    tail = kr_slab * gk
    if rope_tabs is not None:
        tail = _rope(tail, *rope_tabs)
    ones_lane = (lax.broadcasted_iota(jnp.int32, (1, LANES), 1) == MLA_V).astype(F32)
    for p in range(MLA_HEADS // 2):
        cols = pl.ds(p * PAIR, PAIR)
        kn = _dot(c16, wk_ref[:, cols])
        v = _dot(c16, wv_ref[:, cols])
        rinv = _head_rinv(kn + jnp.tile(kr_slab, (1, 2)), e2_ref)
        for i in range(2):
            hs = slice(i * LANES, (i + 1) * LANES)
            k_ref[2 * p + i] = (rinv[:, hs] * (kn[:, hs] * gk + tail)).astype(BF16)
            v_ref[2 * p + i] = (v[:, hs] + ones_lane).astype(BF16)


def _l0_pre_kernel(latent, x_ref, mod_ref, nw_ref, win_ref, qln_ref, kvn_ref, wq_ref, gq_ref,
                   wk_ref, gk_ref, wv_ref, e2_ref, *rest):
    if latent:
        rope_ref, a_ref, q_ref, k_ref, v_ref, ckv_ref, kr_ref = rest
        tile_rows = x_ref.shape[0] // GRID_W
        row0 = (pl.program_id(0) * tile_rows) % rope_ref.shape[1]
        expand = lambda col, row: jnp.concatenate(
            [rope_ref[col] + rope_ref[row, pl.ds(row0 + r, 1), :] for r in range(tile_rows)], axis=0)
        rope_tabs = (expand(0, 1), expand(2, 3))
    else:
        a_ref, q_ref, k_ref, v_ref, ckv_ref, kr_ref = rest
        rope_tabs = None
    h = _modulate(x_ref[...], nw_ref[...], _ada_row(mod_ref, 0), _ada_row(mod_ref, 1)).astype(BF16)
    proj = _dot(h, win_ref[...])
    a_ref[...] = proj[:, :POOL_WIDTH]
    q_lat = proj[:, POOL_WIDTH:POOL_WIDTH + Q_LORA]
    kv_lat = proj[:, POOL_WIDTH + Q_LORA:POOL_WIDTH + Q_LORA + KV_LORA]
    kr_slab = proj[:, POOL_WIDTH + Q_LORA + KV_LORA:]
    ql = _rms(q_lat, qln_ref[...]).astype(BF16)
    gq = gq_ref[...]
    for p in range(MLA_HEADS // 2):
        q = _dot(ql, wq_ref[:, pl.ds(p * PAIR, PAIR)])
        q_rinv = _head_rinv(q, e2_ref)
        for i in range(2):
            hs = slice(i * LANES, (i + 1) * LANES)
            y = q[:, hs] * q_rinv[:, hs] * gq
            if rope_tabs is not None:
                y = _rope(y, *rope_tabs)
            q_ref[2 * p + i] = y.astype(BF16)
    ckv = _rms(kv_lat, kvn_ref[...])
    ckv_ref[...] = ckv
    kr_ref[...] = kr_slab[:, MLA_NOPE:MLA_QK]
    _mla_keys_values(ckv, kr_slab, wk_ref, gk_ref[...], wv_ref, e2_ref, rope_tabs, k_ref, v_ref)


def _slab_spec(slabs, tm):
    return pl.BlockSpec((slabs, tm, LANES), lambda i: (0, i, 0))


def _l0_pre(x, mod, tiles_per_group, latent, w, rope_tabs):
    n = x.shape[0]
    tm = TOKEN_TILE
    row = lambda i: (i, 0)
    hk = MLA_HEADS * LANES
    in_specs = [pl.BlockSpec((tm, D_MODEL), row),
                _ada_spec(mod, tiles_per_group),
                _const_spec((1, D_MODEL)),
                _const_spec((D_MODEL, D_MODEL)),
                _const_spec((1, Q_LORA)),
                _const_spec((1, KV_LORA)),
                _const_spec((Q_LORA, hk)),
                _const_spec((1, LANES)),
                _const_spec((KV_LORA, hk)),
                _const_spec((1, LANES)),
                _const_spec((KV_LORA, hk)),
                _const_spec((2 * LANES, 2 * LANES))]
    gains = "rope" if latent else "plain"
    args = [x, mod[0], w["norm_mix"], w["w_in"], w["qln"], w["kvn"], w["wq"], w["gq_" + gains], w["wk"],
            w["gk_" + gains], w["wv"], w["e2"]]
    if latent:
        assert tm % GRID_W == 0
        in_specs.append(_const_spec(rope_tabs.shape))
        args.append(rope_tabs)
    slab = jax.ShapeDtypeStruct((MLA_HEADS, n, LANES), BF16)
    out_shape = (jax.ShapeDtypeStruct((n, POOL_WIDTH), F32), slab, slab, slab,
                 jax.ShapeDtypeStruct((n, KV_LORA), F32),
                 jax.ShapeDtypeStruct((n, MLA_ROPE), F32))
    out_specs = (pl.BlockSpec((tm, POOL_WIDTH), row),
                 _slab_spec(MLA_HEADS, tm), _slab_spec(MLA_HEADS, tm), _slab_spec(MLA_HEADS, tm),
                 pl.BlockSpec((tm, KV_LORA), row), pl.BlockSpec((tm, MLA_ROPE), row))
    return pl.pallas_call(
        functools.partial(_l0_pre_kernel, latent),
        out_shape=out_shape, grid=(n // tm,), in_specs=in_specs, out_specs=out_specs,
        compiler_params=_params(1), name="l0_pre_latent" if latent else "l0_pre_ctx",
    )(*args)


def _cache_kv_kernel(ckv_ref, kr_ref, wk_ref, gk_ref, wv_ref, e2_ref, k_ref, v_ref):
    _mla_keys_values(ckv_ref[...], kr_ref[...], wk_ref, gk_ref[...], wv_ref, e2_ref, None, k_ref, v_ref)


def _cache_kv(ckv, kr_slab, w):
    n = ckv.shape[0]
    tm = TOKEN_TILE
    row = lambda i: (i, 0)
    hk = MLA_HEADS * LANES
    slab = jax.ShapeDtypeStruct((MLA_HEADS, n, LANES), BF16)
    return pl.pallas_call(
        _cache_kv_kernel,
        out_shape=(slab, slab),
        grid=(n // tm,),
        in_specs=[pl.BlockSpec((tm, KV_LORA), row), pl.BlockSpec((tm, LANES), row),
                  _const_spec((KV_LORA, hk)), _const_spec((1, LANES)), _const_spec((KV_LORA, hk)),
                  _const_spec((2 * LANES, 2 * LANES))],
        out_specs=(_slab_spec(MLA_HEADS, tm), _slab_spec(MLA_HEADS, tm)),
        compiler_params=_params(1), name="l0_cache_kv",
    )(ckv, kr_slab, w["wk"], w["gk_plain"], w["wv"], w["e2"])


def _pool_mixer(seq_len, tile, prev_ref, u_ref, next_ref, wp_ref, ps_ref, buf_ref):
    tm = u_ref.shape[0]
    piece = min(tm, seq_len)
    n_pieces = tm // piece
    stride = piece + 2 * POOL_HALO
    starts = []
    for j in range(n_pieces):
        pos0 = (tile * tm + j * piece) % seq_len
        prev = prev_ref[...] if j == 0 else u_ref[j * piece - POOL_HALO:j * piece, :]
        nxt = next_ref[...] if j == n_pieces - 1 else u_ref[(j + 1) * piece:(j + 1) * piece + POOL_HALO, :]
        base = j * stride
        buf_ref[base:base + POOL_HALO, :] = jnp.where(pos0 != 0, prev, 0.0)
        buf_ref[base + POOL_HALO:base + POOL_HALO + piece, :] = u_ref[j * piece:(j + 1) * piece, :]
        buf_ref[base + POOL_HALO + piece:base + stride, :] = jnp.where(pos0 + piece != seq_len, nxt, 0.0)
        starts.append(pos0)
    def group(g):
        win = POOL_WINDOWS[g]
        lanes = pl.ds(g * POOL_GROUP, POOL_GROUP)
        pooled = []
        for j in range(n_pieces):
            centre = j * stride + POOL_HALO
            lo = starts[j] + lax.broadcasted_iota(jnp.int32, (piece, 1), 0) - win // 2
            cnt = (jnp.minimum(lo + win, seq_len) - jnp.maximum(lo, 0)).astype(F32)
            acc = buf_ref[pl.ds(centre - win // 2, piece), lanes]
            for d in range(1 - win // 2, win - win // 2):
                acc = acc + buf_ref[pl.ds(centre + d, piece), lanes]
            pooled.append(acc / cnt - buf_ref[pl.ds(centre, piece), lanes])
        pooled = pooled[0] if n_pieces == 1 else jnp.concatenate(pooled, axis=0)
        return (_dot(pooled.astype(BF16), wp_ref[g]) * ps_ref[:, lanes]).astype(BF16)

    return [functools.partial(group, g) for g in range(len(POOL_WINDOWS))]


def _attend(streams, running_max):
    items = [(si, ch) for si, (_, chunks, _) in enumerate(streams) for ch in chunks]

    def score(item):
        si, (k_fn, _, extra_fn) = item
        s = _dot_nt(k_fn(), streams[si][0])
        return s if extra_fn is None else s + extra_fn()

    state = [None] * len(streams)
    pending = [score(item) for item in items[:SCORE_LOOKAHEAD]]
    for i, (si, (_, v_fn, _)) in enumerate(items):
        s = pending.pop(0)
        if i + SCORE_LOOKAHEAD < len(items):
            pending.append(score(items[i + SCORE_LOOKAHEAD]))
        want_den = streams[si][2]
        m_new, alpha = None, 1.0
        if state[si] is not None:
            m, acc, den = state[si]
        if running_max:
            m_new = jnp.max(s, axis=0, keepdims=True)
            if state[si] is not None:
                m_new = jnp.maximum(m, m_new)
                alpha = jnp.exp2(m - m_new)
            s = s - m_new
        p = jnp.exp2(s)
        pv = _dot_tn(v_fn(), p.astype(BF16))
        psum = jnp.sum(p, axis=0, keepdims=True) if want_den else None
        if state[si] is None:
            state[si] = (m_new, pv, psum)
        elif running_max:
            state[si] = (m_new, acc * alpha + pv, den * alpha + psum if want_den else None)
        else:
            state[si] = (None, acc + pv, den + psum if want_den else None)
    return [(acc, den) for _, acc, den in state]


def _ref_chunks(k_ref, v_ref, slab, chunk, extra_fn=None, lo=0, hi=None):
    hi = k_ref.shape[1] if hi is None else hi
    out = []
    for c in range(lo, hi, chunk):
        rows = pl.ds(c, min(chunk, hi - c))
        out.append((lambda rows=rows: k_ref[slab, rows, :], lambda rows=rows: v_ref[slab, rows, :], extra_fn))
    return out


def _mla_finish(o_t):
    return o_t[:MLA_V] / o_t[MLA_V:MLA_V + 1]


def _na_queries(q2):
    low = lax.broadcasted_iota(jnp.int32, (1, LANES), 1) < NA_HEAD_DIM
    zero = jnp.zeros_like(q2)
    return jnp.concatenate([jnp.where(low, q2, zero), jnp.where(low, zero, q2)], axis=0)


def _na_finish(o_t, den):
    nq = o_t.shape[1] // 2
    o = o_t / den
    return jnp.concatenate([o[:NA_HEAD_DIM, :nq], o[NA_HEAD_DIM:, nq:]], axis=0)


def _attend_guarded(bound, streams_fn):
    return lax.cond(bound <= MAX_SCORE_BOUND,
                    lambda: _attend(streams_fn(False), False),
                    lambda: _attend(streams_fn(True), True))


def _mla_ctx_kernel(seq, bound_ref, q_ref, k_ref, v_ref, o_ref):
    spans = [(lo, lo + seq) for lo in range(0, k_ref.shape[1], seq)]
    res = _attend_guarded(bound_ref[0, 0], lambda _: [
        (q_ref[h, lo:hi, :], _ref_chunks(k_ref, v_ref, h, seq, lo=lo, hi=hi), False)
        for lo, hi in spans for h in range(MLA_HEADS)])
    for i, (lo, hi) in enumerate(spans):
        for hp in range(MLA_HEADS // 2):
            o_t = jnp.concatenate([_mla_finish(res[i * MLA_HEADS + h][0]) for h in (2 * hp, 2 * hp + 1)], axis=0)
            o_ref[lo:hi, hp * LANES:(hp + 1) * LANES] = o_t.T.astype(BF16)


def _na_ctx_kernel(seq, bound_ref, q_ref, k_ref, v_ref, o_ref):
    spans = [(lo, lo + seq) for lo in range(0, k_ref.shape[1], seq)]
    res = _attend_guarded(bound_ref[0, 0], lambda _: [
        (_na_queries(q_ref[s, lo:hi, :]), _ref_chunks(k_ref, v_ref, s, seq, lo=lo, hi=hi), True)
        for lo, hi in spans for s in range(NA_SLABS)])
    for i, (lo, hi) in enumerate(spans):
        for s in range(NA_SLABS):
            o_ref[lo:hi, s * LANES:(s + 1) * LANES] = _na_finish(*res[i * NA_SLABS + s]).T.astype(BF16)


def _ctx_attn(body, q, k, v, bound, seq, width, name):
    slabs, n, _ = q.shape
    rows = CTX_STEP_SEQS * seq
    spec = pl.BlockSpec((slabs, rows, LANES), lambda b: (0, b, 0))
    return pl.pallas_call(
        functools.partial(body, seq),
        out_shape=jax.ShapeDtypeStruct((n, width), BF16),
        grid=(n // rows,),
        in_specs=[pl.BlockSpec(memory_space=pltpu.SMEM), spec, spec, spec],
        out_specs=pl.BlockSpec((rows, width), lambda b: (b, 0)),
        compiler_params=_params(1), name=name,
    )(bound.reshape(1, 1), q, k, v)


def _mla_lat_kernel(bound_ref, q_ref, kl_ref, vl_ref, kc_ref, vc_ref, o_ref):
    batch, head0 = pl.program_id(0), pl.program_id(1) * MLA_STEP_HEADS
    heads = range(MLA_STEP_HEADS)
    bounds = [bound_ref[batch, head0 + h] for h in heads]

    def streams(running_max):
        size = MLA_ONLINE_CHUNK if running_max else MLA_KEY_CHUNK
        return [(q_ref[h], _ref_chunks(kl_ref, vl_ref, h, size) + _ref_chunks(kc_ref, vc_ref, h, size), False)
                for h in heads]

    res = _attend_guarded(functools.reduce(jnp.maximum, bounds), streams)
    o_ref[...] = jnp.concatenate([_mla_finish(o) for o, _ in res], axis=0).T.astype(BF16)


def _normed_len(gain, dim):
    return (dim ** 0.5) * jnp.max(jnp.abs(gain.astype(F32)))


def _mla_lat_attn(q, k, v, kc, vc, bound, seq, past):
    n = q.shape[1]
    tq = MLA_Q_TILE
    nq = seq // tq
    hs = MLA_STEP_HEADS
    return pl.pallas_call(
        _mla_lat_kernel,
        out_shape=jax.ShapeDtypeStruct((n, MLA_HEADS * MLA_V), BF16),
        grid=(n // seq, MLA_HEADS // hs, nq),
        in_specs=[pl.BlockSpec(memory_space=pltpu.SMEM),
                  pl.BlockSpec((hs, tq, LANES), lambda b, h, i: (h, b * nq + i, 0)),
                  pl.BlockSpec((hs, seq, LANES), lambda b, h, i: (h, b, 0)),
                  pl.BlockSpec((hs, seq, LANES), lambda b, h, i: (h, b, 0)),
                  pl.BlockSpec((hs, past, LANES), lambda b, h, i: (h, b, 0)),
                  pl.BlockSpec((hs, past, LANES), lambda b, h, i: (h, b, 0))],
        out_specs=pl.BlockSpec((tq, hs * MLA_V), lambda b, h, i: (b * nq + i, h)),
        compiler_params=_params(3), name="l0_mla_latent_attn",
    )(bound, q, k, v, kc, vc)


def _na_first_row(r, rows):
    return jnp.clip(r - NA_KH // 2, 0, rows - NA_KH)


def _na_lat_kernel(rows, bound_ref, q_ref, k_ref, v_ref, kc_ref, vc_ref, t2_ref, o_ref):
    nq = NA_TILE_ROWS * GRID_W
    nk = NA_WIN_ROWS * GRID_W
    n_off = 2 * NA_KH - 1
    r0 = pl.program_id(1) * NA_TILE_ROWS
    w0 = jnp.clip(r0 - NA_KH // 2, 0, rows - NA_WIN_ROWS)

    def block_index(j, t):
        key_row = w0 + j
        i = r0 + 2 * t
        first_u, first_l = _na_first_row(i, rows), _na_first_row(i + 1, rows)
        in_u = (key_row >= first_u) & (key_row < first_u + NA_KH)
        in_l = (key_row >= first_l) & (key_row < first_l + NA_KH)
        d = jnp.clip(key_row - i + NA_KH - 1, 0, n_off - 1)
        return jnp.where(in_u & in_l, d, jnp.where(in_u, n_off, jnp.where(in_l, n_off + 1, n_off + 2)))

    idx = [[block_index(j, t) for t in range(NA_TILE_ROWS // 2)] for j in range(NA_WIN_ROWS)]

    def local_bias(s):
        heads = [jnp.concatenate([jnp.concatenate([t2_ref[2 * s + hh, i] for i in row], axis=1) for row in idx], axis=0)
                 for hh in range(2)]
        return jnp.concatenate(heads, axis=1)

    batch = pl.program_id(0)
    bounds = [bound_ref[batch, h] for h in range(NA_HEADS)]

    def cache_chunk(s):
        lanes = pl.ds(s * LANES, LANES)
        return [(lambda: kc_ref[:, lanes].astype(BF16), lambda: vc_ref[:, lanes].astype(BF16), None)]

    res = _attend_guarded(
        functools.reduce(jnp.maximum, bounds),
        lambda _: [(_na_queries(q_ref[s]),
                    _ref_chunks(k_ref, v_ref, s, nk, functools.partial(local_bias, s)) + cache_chunk(s),
                    True) for s in range(NA_SLABS)])
    for s in range(NA_SLABS):
        o_ref[:, s * LANES:(s + 1) * LANES] = _na_finish(*res[s]).T.astype(BF16)


def _na_lat_attn(q, k, v, kc, vc, t2, bound, seq, past):
    slabs, n, _ = q.shape
    rows = seq // GRID_W
    tiles = rows // NA_TILE_ROWS
    nq = NA_TILE_ROWS * GRID_W
    nk = NA_WIN_ROWS * GRID_W

    def win_map(b, t):
        w0 = jnp.clip(t * NA_TILE_ROWS - NA_KH // 2, 0, rows - NA_WIN_ROWS)
        return (0, pl.multiple_of(b * seq + w0 * GRID_W, GRID_W), 0)

    win_spec = pl.BlockSpec((pl.Element(slabs), pl.Element(nk), pl.Element(LANES)), win_map)
    ctx_spec = pl.BlockSpec((past, slabs * LANES), lambda b, t: (b, 0))
    return pl.pallas_call(
        functools.partial(_na_lat_kernel, rows),
        out_shape=jax.ShapeDtypeStruct((n, slabs * LANES), BF16),
        grid=(n // seq, tiles),
        in_specs=[pl.BlockSpec(memory_space=pltpu.SMEM),
                  pl.BlockSpec((slabs, nq, LANES), lambda b, t: (0, b * tiles + t, 0)),
                  win_spec, win_spec, ctx_spec, ctx_spec,
                  _const_spec(t2.shape)],
        out_specs=pl.BlockSpec((nq, slabs * LANES), lambda b, t: (b * tiles + t, 0)),
        compiler_params=_params(2), name="l1_na_latent_attn",
    )(bound, q, k, v, kc, vc, t2)


def _pair_norm(x2, g2):
    low = lax.broadcasted_iota(jnp.int32, (1, LANES), 1) < NA_HEAD_DIM
    sq = x2 * x2
    ss_all = jnp.sum(sq, axis=-1, keepdims=True)
    ss_low = jnp.sum(jnp.where(low, sq, 0.0), axis=-1, keepdims=True)
    ss = jnp.where(low, ss_low, ss_all - ss_low)
    return x2 * lax.rsqrt(ss * (1.0 / NA_HEAD_DIM) + EPS) * g2


def _l1_pre_kernel(keep_f32, x_ref, mod_ref, nw_ref, win_ref, gq_ref, gk_ref, q_ref, k_ref, v_ref, *f32_refs):
    h = _modulate(x_ref[...], nw_ref[...], _ada_row(mod_ref, 0), _ada_row(mod_ref, 1)).astype(BF16)
    width = NA_SLABS * LANES
    q = _dot(h, win_ref[:, 0:width])
    k = _dot(h, win_ref[:, width:2 * width])
    v = _dot(h, win_ref[:, 2 * width:3 * width])
    gq = gq_ref[...]
    gk = gk_ref[...]
    for s in range(NA_SLABS):
        ls = slice(s * LANES, (s + 1) * LANES)
        q_ref[s] = _pair_norm(q[:, ls], gq).astype(BF16)
        kn = _pair_norm(k[:, ls], gk)
        k_ref[s] = kn.astype(BF16)
        v_ref[s] = v[:, ls].astype(BF16)
        if keep_f32:
            f32_refs[0][:, ls] = kn
    if keep_f32:
        f32_refs[1][...] = v


def _l1_pre(x, mod, tiles_per_group, keep_f32, w):
    n = x.shape[0]
    tm = TOKEN_TILE
    row = lambda i: (i, 0)
    width = NA_SLABS * LANES
    slab = jax.ShapeDtypeStruct((NA_SLABS, n, LANES), BF16)
    out_shape = [slab, slab, slab]
    out_specs = [_slab_spec(NA_SLABS, tm)] * 3
    if keep_f32:
        out_shape += [jax.ShapeDtypeStruct((n, width), F32)] * 2
        out_specs += [pl.BlockSpec((tm, width), row)] * 2
    return pl.pallas_call(
        functools.partial(_l1_pre_kernel, keep_f32),
        out_shape=tuple(out_shape), grid=(n // tm,),
        in_specs=[pl.BlockSpec((tm, D_MODEL), row),
                  _ada_spec(mod, tiles_per_group),
                  _const_spec((1, D_MODEL)),
                  _const_spec((D_MODEL, 3 * width)),
                  _const_spec((1, LANES)),
                  _const_spec((1, LANES))],
        out_specs=tuple(out_specs),
        compiler_params=_params(1), name="l1_pre_ctx" if keep_f32 else "l1_pre_latent",
    )(x, mod[0], w["norm_mix"], w["w_in"], w["gq"], w["gk"])


def _cast_weight(src_hbm, dst_ref, stage_ref, sems):
    rows = stage_ref.shape[1]
    n = src_hbm.shape[0] // rows
    copies = [pltpu.make_async_copy(src_hbm.at[pl.ds(c * rows, rows), :], stage_ref.at[c % 2], sems.at[c % 2])
              for c in range(n)]
    copies[0].start()
    for c in range(n):
        if c + 1 < n:
            copies[c + 1].start()
        copies[c].wait()
        dst_ref[pl.ds(c * rows, rows), :] = stage_ref[c % 2].astype(BF16)


def _post_kernel(pool_seq, cast, x_ref, attn_ref, *rest):
    rest = list(rest)
    take = lambda k: [rest.pop(0) for _ in range(k)]
    pool_in = take(8) if pool_seq is not None else None
    mod_ref, nw_ref, wo_ref, w1_ref, w2_ref = take(5)
    (o_ref,) = take(1)
    w16_out = take(3) if cast else None
    pool_scratch = take(2) if pool_seq is not None else None
    step = pl.program_id(0)
    if cast:
        sources = (wo_ref, w1_ref, w2_ref)
        wo_ref, w1_ref, w2_ref, stage_wide, stage_tall, sem_in, sem_out = take(7)
        to_hbm = [pltpu.make_async_copy(v, o, sem_out.at[i])
                  for i, (v, o) in enumerate(zip((wo_ref, w1_ref, w2_ref), w16_out))]

        @pl.when(step == 0)
        def _():
            _cast_weight(sources[1], w1_ref, stage_wide, sem_in)
            _cast_weight(sources[2], w2_ref, stage_tall, sem_in)
            _cast_weight(sources[0], wo_ref, stage_tall, sem_in)
            for copy in to_hbm:
                copy.start()

        @pl.when(step == pl.num_programs(0) - 1)
        def _():
            for copy in to_hbm:
                copy.wait()

    if pool_seq is None:
        mixed = _dot(attn_ref[...], wo_ref[...])
    else:
        first_refs, next_refs, (wp_ref, ps_ref) = pool_in[0:3], pool_in[3:6], pool_in[6:8]
        buf_ref, pooled_ref = pool_scratch

        @pl.when(step == 0)
        def _():
            for g, group in enumerate(_pool_mixer(pool_seq, 0, *first_refs, wp_ref, ps_ref, buf_ref)):
                pooled_ref[:, g * POOL_GROUP:(g + 1) * POOL_GROUP] = group()

        mixed = _dot(jnp.concatenate([pooled_ref[...], attn_ref[...]], axis=-1), wo_ref[...])
        next_groups = _pool_mixer(pool_seq, jnp.minimum(step + 1, pl.num_programs(0) - 1), *next_refs,
                                  wp_ref, ps_ref, buf_ref)
    x1 = x_ref[...] + _ada_row(mod_ref, 2) * mixed
    h = _modulate(x1, nw_ref[...], _ada_row(mod_ref, 3), _ada_row(mod_ref, 4)).astype(BF16)
    acc = jnp.zeros(x1.shape, F32)
    n_chunks = D_FF // FF_CHUNK
    for c in range(n_chunks):
        cs = pl.ds(c * FF_CHUNK, FF_CHUNK)
        u = jnp.square(jnp.maximum(_dot(h, w1_ref[:, cs]), 0.0)).astype(BF16)
        acc = acc + _dot(u, w2_ref[cs, :])
        if pool_seq is not None:
            for g in range(c * len(next_groups) // n_chunks, (c + 1) * len(next_groups) // n_chunks):
                pooled_ref[:, g * POOL_GROUP:(g + 1) * POOL_GROUP] = next_groups[g]()
    o_ref[...] = x1 + _ada_row(mod_ref, 5) * acc


POST_STAGE_BYTES = 2 * 1024 * 1024


def _post(x, attn, mod, tiles_per_group, w, name, pool=None, cast=False):
    n = x.shape[0]
    tm = TOKEN_TILE
    row = lambda i: (i, 0)
    in_specs = [pl.BlockSpec((tm, D_MODEL), row), pl.BlockSpec((tm, attn.shape[1]), row)]
    args = [x, attn]
    scratch = []
    pool_seq = None
    if pool is not None:
        a_in, pool_seq = pool
        hb = tm // POOL_HALO
        last = n // POOL_HALO - 1
        pieces = tm // min(tm, pool_seq)
        last_tile = n // tm - 1

        def tile_specs(tile_of):
            return [pl.BlockSpec((POOL_HALO, POOL_WIDTH), lambda i: (jnp.maximum(tile_of(i) * hb - 1, 0), 0)),
                    pl.BlockSpec((tm, POOL_WIDTH), lambda i: (tile_of(i), 0)),
                    pl.BlockSpec((POOL_HALO, POOL_WIDTH), lambda i: (jnp.minimum((tile_of(i) + 1) * hb, last), 0))]

        in_specs += tile_specs(lambda i: 0) + tile_specs(lambda i: jnp.minimum(i + 1, last_tile))
        in_specs += [_const_spec((len(POOL_WINDOWS), POOL_GROUP, POOL_GROUP)), _const_spec((1, POOL_WIDTH))]
        args += [a_in] * 6 + [w["w_pool"], w["pool_scale"]]
        scratch = [pltpu.VMEM((tm + 2 * POOL_HALO * pieces, POOL_WIDTH), F32), pltpu.VMEM((tm, POOL_WIDTH), BF16)]
    weights = [w["w_out"], w["w_mlp1"], w["w_mlp2"]]
    in_specs += [_ada_spec(mod, tiles_per_group), _const_spec((1, D_MODEL))]
    out_shape = [jax.ShapeDtypeStruct((n, D_MODEL), F32)]
    out_specs = [pl.BlockSpec((tm, D_MODEL), row)]
    if cast:
        in_specs += [pl.BlockSpec(memory_space=pl.ANY)] * 3
        out_shape += [jax.ShapeDtypeStruct(wt.shape, BF16) for wt in weights]
        out_specs += [pl.BlockSpec(memory_space=pl.ANY)] * 3
        stage = lambda cols: pltpu.VMEM((2, POST_STAGE_BYTES // (4 * cols), cols), F32)
        scratch = scratch + [pltpu.VMEM(wt.shape, BF16) for wt in weights]
        scratch += [stage(D_FF), stage(D_MODEL), pltpu.SemaphoreType.DMA((2,)), pltpu.SemaphoreType.DMA((3,))]
    else:
        in_specs += [_const_spec(wt.shape) for wt in weights]
    args += [mod[0], w["norm_mlp"]] + weights
    outs = pl.pallas_call(
        functools.partial(_post_kernel, pool_seq, cast),
        out_shape=tuple(out_shape), grid=(n // tm,),
        in_specs=in_specs, out_specs=tuple(out_specs),
        scratch_shapes=scratch, compiler_params=_params(1), name=name,
    )(*args)
    return (outs[0], dict(w, w_out=outs[1], w_mlp1=outs[2], w_mlp2=outs[3])) if cast else outs[0]


def _rope_tables(seq):
    assert seq == GRID_W * GRID_W
    half = MLA_ROPE // 2
    inv_freq = jnp.power(ROPE_THETA, -jnp.arange(0, half, 2, dtype=F32) / half)
    ang = jnp.arange(GRID_W, dtype=F32)[:, None] * inv_freq[None, :]
    cos, sin = jnp.cos(ang), jnp.sin(ang)
    fill = lambda n, v=0.0: jnp.full((GRID_W, n), v, F32)
    spare = LANES - MLA_QK
    return jnp.stack([
        jnp.concatenate([fill(MLA_NOPE, 1.0), fill(half), cos, cos, fill(spare)], axis=1),
        jnp.concatenate([fill(MLA_NOPE), cos, cos, fill(half), fill(spare)], axis=1),
        jnp.concatenate([fill(MLA_NOPE), fill(half), -sin, sin, fill(spare)], axis=1),
        jnp.concatenate([fill(MLA_NOPE), -sin, sin, fill(half), fill(spare)], axis=1)])


def _rope_partner(tail):
    shape = tail.shape
    t = tail.reshape(shape[:-1] + (2, 2, MLA_ROPE // 4))
    return t[..., ::-1, :].reshape(shape)


def _na_bias_table(rel_bias, rows):
    qcol = jnp.arange(GRID_W)
    kcol = jnp.arange(GRID_W)
    col_start = jnp.clip(qcol - NA_KW // 2, 0, GRID_W - NA_KW)
    col_mask = (kcol[:, None] >= col_start[None, :]) & (kcol[:, None] < col_start[None, :] + NA_KW)
    offs = jnp.arange(2 * NA_KW - 1)
    onehot = (kcol[None, :, None] - qcol[None, None, :] + NA_KW - 1 == offs[:, None, None]).astype(F32)
    blocks = jnp.einsum('hdj,jkq->hdkq', rel_bias.astype(F32) * LOG2E, onehot, precision=lax.Precision.HIGHEST)
    blocks = jnp.where(col_mask[None, None], blocks, NEG_INF)
    neg = jnp.full_like(blocks[:, :1], NEG_INF)
    both = jnp.concatenate([blocks, jnp.concatenate([neg, blocks[:, :-1]], axis=1)], axis=-1)
    d_upper, d_lower = _na_partial_offsets(rows)
    upper_only = jnp.concatenate([blocks[:, d_upper:d_upper + 1], neg], axis=-1)
    lower_only = jnp.concatenate([neg, blocks[:, d_lower - 1:d_lower]], axis=-1)
    return jnp.concatenate([both, upper_only, lower_only, jnp.concatenate([neg, neg], axis=-1)], axis=1)


def _na_partial_offsets(rows):
    first = lambda r: min(max(r - NA_KH // 2, 0), rows - NA_KH)
    upper, lower = set(), set()
    for r0 in range(0, rows, NA_TILE_ROWS):
        w0 = min(max(r0 - NA_KH // 2, 0), rows - NA_WIN_ROWS)
        for j in range(NA_WIN_ROWS):
            for t in range(NA_TILE_ROWS // 2):
                key_row, i = w0 + j, r0 + 2 * t
                in_u = first(i) <= key_row < first(i) + NA_KH
                in_l = first(i + 1) <= key_row < first(i + 1) + NA_KH
                d = key_row - i + NA_KH - 1
                if in_u and in_l:
                    assert 1 <= d <= 2 * NA_KH - 2
                elif in_u:
                    upper.add(d)
                elif in_l:
                    lower.add(d)
    assert len(upper) == 1 and len(lower) == 1, (upper, lower)
    return upper.pop(), lower.pop()


def _prep_l0(w_in, q_lora_norm, kv_lora_norm, w_q_up, w_kv_up, mla_q_norm, mla_k_norm, w_pool, pool_scale,
             w_out, norm_mix, norm_mlp, w_mlp1, w_mlp2):
    lat_end = POOL_WIDTH + Q_LORA + KV_LORA
    w_in_pad = jnp.concatenate([w_in[:, :lat_end], jnp.zeros((D_MODEL, MLA_NOPE), w_in.dtype),
                                w_in[:, lat_end:], _rope_partner(w_in[:, lat_end:])], axis=1)
    pad = LANES - MLA_QK
    hk = MLA_HEADS * LANES
    wq = w_q_up.reshape(Q_LORA, MLA_HEADS, MLA_QK)
    wq = jnp.concatenate([wq, _rope_partner(wq[:, :, MLA_NOPE:])], axis=-1)
    kv = w_kv_up.reshape(KV_LORA, MLA_HEADS, MLA_NOPE + MLA_V)
    wk = jnp.pad(kv[:, :, :MLA_NOPE], ((0, 0), (0, 0), (0, LANES - MLA_NOPE)))
    wv = jnp.pad(kv[:, :, MLA_NOPE:], ((0, 0), (0, 0), (0, LANES - MLA_V)))
    gq = mla_q_norm * (MLA_QK ** -0.5 * LOG2E)
    slab_lane = jnp.arange(2 * LANES)
    e2 = (slab_lane[:, None] // LANES == slab_lane[None, :] // LANES) & (slab_lane[:, None] % LANES < MLA_QK)
    return dict(
        norm_mix=norm_mix.reshape(1, -1), norm_mlp=norm_mlp.reshape(1, -1),
        w_in=w_in_pad.astype(BF16),
        qln=q_lora_norm.reshape(1, -1), kvn=kv_lora_norm.reshape(1, -1),
        wq=wq.reshape(Q_LORA, hk).astype(BF16),
        gq_plain=jnp.pad(gq, (0, pad)).reshape(1, LANES),
        gq_rope=jnp.concatenate([gq, _rope_partner(gq[MLA_NOPE:])]).reshape(1, LANES),
        wk=wk.reshape(KV_LORA, hk).astype(BF16),
        gk_plain=jnp.pad(mla_k_norm, (0, pad)).reshape(1, LANES),
        gk_rope=jnp.concatenate([mla_k_norm, _rope_partner(mla_k_norm[MLA_NOPE:])]).reshape(1, LANES),
        wv=wv.reshape(KV_LORA, hk).astype(BF16),
        e2=e2.astype(BF16),
        w_pool=w_pool.astype(BF16), pool_scale=pool_scale.reshape(1, -1),
        w_out=w_out, w_mlp1=w_mlp1, w_mlp2=w_mlp2)


def kernel(x_prompt, x_sample, cache_l0_mla_ckv, cache_l0_mla_krope, cache_l1_na_k, cache_l1_na_v, c, c_ctx, w_ada_l0, b_ada_l0, norm_mix_l0, norm_mlp_l0, w_mlp1_l0, w_mlp2_l0, w_in_l0, q_lora_norm_l0, kv_lora_norm_l0, w_q_up_l0, w_kv_up_l0, mla_q_norm_l0, mla_k_norm_l0, w_pool_l0, pool_scale_l0, w_out_l0, w_ada_l1, b_ada_l1, norm_mix_l1, norm_mlp_l1, w_mlp1_l1, w_mlp2_l1, w_in_l1, na_q_norm_l1, na_k_norm_l1, rel_bias_l1, w_out_l1):
    batch, seq, _ = x_prompt.shape
    dec_batch, dec_seq, _ = x_sample.shape
    past = cache_l0_mla_ckv.shape[1]
    xp = x_prompt.reshape(batch * seq, D_MODEL)
    xs = x_sample.reshape(dec_batch * dec_seq, D_MODEL)
    tiles_p = (batch * seq) // TOKEN_TILE
    tiles_s = dec_seq // TOKEN_TILE

    cond8 = jnp.concatenate([c_ctx[None, :], c, jnp.zeros((SUBLANES - 1 - dec_batch, D_MODEL), F32)], axis=0)
    mods = _ada(cond8, (w_ada_l0, w_ada_l1), (b_ada_l0, b_ada_l1))
    mods = mods.reshape(DEPTH, SUBLANES, 1, N_ADA * D_MODEL)
    (mod0_p, mod0_s), (mod1_p, mod1_s) = (((mods[l], 0), (mods[l], 1)) for l in range(DEPTH))

    w0 = _prep_l0(w_in_l0, q_lora_norm_l0, kv_lora_norm_l0, w_q_up_l0, w_kv_up_l0, mla_q_norm_l0, mla_k_norm_l0,
                  w_pool_l0, pool_scale_l0, w_out_l0, norm_mix_l0, norm_mlp_l0, w_mlp1_l0, w_mlp2_l0)
    a_p, q_p, k_p, v_p, ckv_p, kr_p = _l0_pre(xp, mod0_p, tiles_p, False, w0, None)
    a_s, q_s, k_s, v_s, _, _ = _l0_pre(xs, mod0_s, tiles_s, True, w0, _rope_tables(dec_seq))
    kr_cache = jnp.pad(cache_l0_mla_krope.reshape(dec_batch * past, MLA_ROPE), ((0, 0), (MLA_NOPE, LANES - MLA_QK)))
    k_c, v_c = _cache_kv(cache_l0_mla_ckv.reshape(dec_batch * past, KV_LORA), kr_cache, w0)
    mla_bound = _normed_len(mla_q_norm_l0 * (MLA_QK ** -0.5 * LOG2E), MLA_QK) * _normed_len(mla_k_norm_l0, MLA_QK)
    attn_p = _ctx_attn(_mla_ctx_kernel, q_p, k_p, v_p, mla_bound, seq, MLA_HEADS * MLA_V, "l0_mla_ctx_attn")
    attn_s = _mla_lat_attn(q_s, k_s, v_s, k_c, v_c, jnp.full((dec_batch, MLA_HEADS), mla_bound, F32), dec_seq, past)
    xp, w0 = _post(xp, attn_p, mod0_p, tiles_p, w0, "l0_post_ctx", pool=(a_p, seq), cast=True)
    xs = _post(xs, attn_s, mod0_s, tiles_s, w0, "l0_post_latent", pool=(a_s, dec_seq))

    w1 = dict(norm_mix=norm_mix_l1.reshape(1, -1), norm_mlp=norm_mlp_l1.reshape(1, -1),
              w_in=w_in_l1.astype(BF16),
              gq=jnp.tile(na_q_norm_l1 * (NA_HEAD_DIM ** -0.5 * LOG2E), 2).reshape(1, LANES),
              gk=jnp.tile(na_k_norm_l1, 2).reshape(1, LANES),
              w_out=w_out_l1, w_mlp1=w_mlp1_l1, w_mlp2=w_mlp2_l1)
    q1_p, k1_p, v1_p, k1_new, v1_new = _l1_pre(xp, mod1_p, tiles_p, True, w1)
    q1_s, k1_s, v1_s = _l1_pre(xs, mod1_s, tiles_s, False, w1)
    q_len, k_len = _normed_len(w1["gq"], NA_HEAD_DIM), _normed_len(w1["gk"], NA_HEAD_DIM)
    na_p = _ctx_attn(_na_ctx_kernel, q1_p, k1_p, v1_p, q_len * k_len, seq, NA_SLABS * LANES, "l1_na_ctx_attn")
    cache_len = jnp.sqrt(jnp.max(jnp.sum(jnp.square(cache_l1_na_k.astype(F32)), axis=-1), axis=1))
    na_bound = (q_len * jnp.maximum(k_len, cache_len)
                + jnp.max(jnp.abs(rel_bias_l1.astype(F32)) * LOG2E, axis=(1, 2))[None, :])
    na_s = _na_lat_attn(q1_s, k1_s, v1_s,
                        cache_l1_na_k.reshape(dec_batch * past, -1), cache_l1_na_v.reshape(dec_batch * past, -1),
                        _na_bias_table(rel_bias_l1, dec_seq // GRID_W), na_bound, dec_seq, past)
    xp, w1 = _post(xp, na_p, mod1_p, tiles_p, w1, "l1_post_ctx", cast=True)
    xs = _post(xs, na_s, mod1_s, tiles_s, w1, "l1_post_latent")

    return (xp.reshape(batch, seq, D_MODEL), xs.reshape(dec_batch, dec_seq, D_MODEL),
            ckv_p.reshape(batch, seq, KV_LORA), kr_p.reshape(batch, seq, MLA_ROPE),
            k1_new.reshape(batch, seq, NA_HEADS, NA_HEAD_DIM), v1_new.reshape(batch, seq, NA_HEADS, NA_HEAD_DIM))
```

```python
import functools

import jax
import jax.numpy as jnp
from jax import lax
from jax.experimental import pallas as pl
from jax.experimental.pallas import tpu as pltpu

F32 = jnp.float32
BF16 = jnp.bfloat16

D_MODEL = 1024
D_FF = 4 * D_MODEL
DEPTH = 2
N_ADA = 6
EPS = 1e-6
NEG_INF = -1e30
ROPE_THETA = 10000.0
GRID_W = 64
POOL_WIDTH = 512
POOL_WINDOWS = (2, 4, 8, 16)
POOL_GROUP = 128
POOL_HALO = 8
MLA_HEADS = 8
MLA_NOPE = 64
MLA_ROPE = 32
MLA_QK = MLA_NOPE + MLA_ROPE
MLA_V = 64
Q_LORA = 256
KV_LORA = 128
NA_HEADS = 16
NA_HEAD_DIM = 64
NA_KH = 8
NA_KW = 16
LOG2E = 1.4426950408889634

LANES = 128
SUBLANES = 8
VMEM_LIMIT = 56 * 1024 * 1024

TOKEN_TILE = 512
MLA_Q_TILE = 512
MLA_STEP_HEADS = 2
MLA_KEY_CHUNK = 2048
MLA_ONLINE_CHUNK = 1024
SCORE_LOOKAHEAD = 4
MAX_SCORE_BOUND = 40.0
FF_CHUNK = 1024
CTX_STEP_SEQS = 2
NA_SLABS = NA_HEADS * NA_HEAD_DIM // LANES
NA_TILE_ROWS = 4
NA_WIN_ROWS = NA_TILE_ROWS + NA_KH - 1


def _params(n_axes):
    return pltpu.CompilerParams(dimension_semantics=("arbitrary",) * n_axes,
                                vmem_limit_bytes=VMEM_LIMIT)


def _const_spec(shape):
    zeros = (0,) * len(shape)
    return pl.BlockSpec(shape, lambda *_: zeros, pipeline_mode=pl.Buffered(1))


def _dot(a, b):
    return jnp.dot(a, b, preferred_element_type=F32)


def _dot_nt(a, b):
    return lax.dot_general(a, b, (((1,), (1,)), ((), ())), preferred_element_type=F32)


def _dot_tn(a, b):
    return lax.dot_general(a, b, (((0,), (0,)), ((), ())), preferred_element_type=F32)


def _rms(x, w, n=None):
    n = x.shape[-1] if n is None else n
    ss = jnp.sum(x * x, axis=-1, keepdims=True)
    return x * lax.rsqrt(ss * (1.0 / n) + EPS) * w


def _modulate(x, nw, shift, scale):
    return _rms(x, nw) * (1.0 + scale) + shift


def _ada_spec(mod, tiles_per_group):
    mods, first_row = mod
    return pl.BlockSpec((None, 1, mods.shape[-1]), lambda i: (first_row + i // tiles_per_group, 0, 0))


def _ada_row(mod_ref, k):
    return mod_ref[:, k * D_MODEL:(k + 1) * D_MODEL]


def _ada_kernel(cond_ref, w0_ref, w1_ref, b_ref, o_ref):
    c = cond_ref[...]
    s = (c * jax.nn.sigmoid(c)).astype(BF16)
    for layer, w_ref in enumerate((w0_ref, w1_ref)):
        @pl.when(pl.program_id(0) == layer)
        def _(w_ref=w_ref):
            o_ref[...] = _dot(s, w_ref[...].astype(BF16)) + b_ref[...]


def _ada(cond8, w_ada, b_ada):
    n_out = w_ada[0].shape[1]
    tn = D_MODEL
    nb = n_out // tn
    return pl.pallas_call(
        _ada_kernel,
        out_shape=jax.ShapeDtypeStruct((DEPTH, SUBLANES, n_out), F32),
        grid=(DEPTH, nb),
        in_specs=[pl.BlockSpec((SUBLANES, D_MODEL), lambda l, j: (0, 0)),
                  pl.BlockSpec((D_MODEL, tn), lambda l, j: (0, jnp.where(l == 0, j, nb - 1))),
                  pl.BlockSpec((D_MODEL, tn), lambda l, j: (0, jnp.where(l == 1, j, 0))),
                  pl.BlockSpec((None, 1, tn), lambda l, j: (l, 0, j))],
        out_specs=pl.BlockSpec((None, SUBLANES, tn), lambda l, j: (l, 0, j)),
        compiler_params=_params(2),
        name="ada",
    )(cond8, w_ada[0], w_ada[1], jnp.stack(b_ada).reshape(DEPTH, 1, n_out))


def _rope(x, c, s):
    return x * c + pltpu.roll(x, LANES - MLA_ROPE, 1) * s


PAIR = 2 * LANES


def _head_rinv(raw, e2_ref):
    ss = _dot((raw * raw).astype(BF16), e2_ref[...])
    return lax.rsqrt(ss * (1.0 / MLA_QK) + EPS)


def _mla_keys_values(ckv, kr_slab, wk_ref, gk, wv_ref, e2_ref, rope_tabs, k_ref, v_ref):
    c16 = ckv.astype(BF16)
    tail = kr_slab * gk
    if rope_tabs is not None:
        tail = _rope(tail, *rope_tabs)
    ones_lane = (lax.broadcasted_iota(jnp.int32, (1, LANES), 1) == MLA_V).astype(F32)
    for p in range(MLA_HEADS // 2):
        cols = pl.ds(p * PAIR, PAIR)
        kn = _dot(c16, wk_ref[:, cols])
        v = _dot(c16, wv_ref[:, cols])
        rinv = _head_rinv(kn + jnp.tile(kr_slab, (1, 2)), e2_ref)
        for i in range(2):
            hs = slice(i * LANES, (i + 1) * LANES)
            k_ref[2 * p + i] = (rinv[:, hs] * (kn[:, hs] * gk + tail)).astype(BF16)
            v_ref[2 * p + i] = (v[:, hs] + ones_lane).astype(BF16)


def _l0_pre_kernel(latent, x_ref, mod_ref, nw_ref, win_ref, qln_ref, kvn_ref, wq_ref, gq_ref,
                   wk_ref, gk_ref, wv_ref, e2_ref, *rest):
    if latent:
        rope_ref, a_ref, q_ref, k_ref, v_ref, ckv_ref, kr_ref = rest
        tile_rows = x_ref.shape[0] // GRID_W
        row0 = (pl.program_id(0) * tile_rows) % rope_ref.shape[1]
        expand = lambda col, row: jnp.concatenate(
            [rope_ref[col] + rope_ref[row, pl.ds(row0 + r, 1), :] for r in range(tile_rows)], axis=0)
        rope_tabs = (expand(0, 1), expand(2, 3))
    else:
        a_ref, q_ref, k_ref, v_ref, ckv_ref, kr_ref = rest
        rope_tabs = None
    h = _modulate(x_ref[...], nw_ref[...], _ada_row(mod_ref, 0), _ada_row(mod_ref, 1)).astype(BF16)
    proj = _dot(h, win_ref[...])
    a_ref[...] = proj[:, :POOL_WIDTH]
    q_lat = proj[:, POOL_WIDTH:POOL_WIDTH + Q_LORA]
    kv_lat = proj[:, POOL_WIDTH + Q_LORA:POOL_WIDTH + Q_LORA + KV_LORA]
    kr_slab = proj[:, POOL_WIDTH + Q_LORA + KV_LORA:]
    ql = _rms(q_lat, qln_ref[...]).astype(BF16)
    gq = gq_ref[...]
    for p in range(MLA_HEADS // 2):
        q = _dot(ql, wq_ref[:, pl.ds(p * PAIR, PAIR)])
        q_rinv = _head_rinv(q, e2_ref)
        for i in range(2):
            hs = slice(i * LANES, (i + 1) * LANES)
            y = q[:, hs] * q_rinv[:, hs] * gq
            if rope_tabs is not None:
                y = _rope(y, *rope_tabs)
            q_ref[2 * p + i] = y.astype(BF16)
    ckv = _rms(kv_lat, kvn_ref[...])
    ckv_ref[...] = ckv
    kr_ref[...] = kr_slab[:, MLA_NOPE:MLA_QK]
    _mla_keys_values(ckv, kr_slab, wk_ref, gk_ref[...], wv_ref, e2_ref, rope_tabs, k_ref, v_ref)


def _slab_spec(slabs, tm):
    return pl.BlockSpec((slabs, tm, LANES), lambda i: (0, i, 0))


def _l0_pre(x, mod, tiles_per_group, latent, w, rope_tabs):
    n = x.shape[0]
    tm = TOKEN_TILE
    row = lambda i: (i, 0)
    hk = MLA_HEADS * LANES
    in_specs = [pl.BlockSpec((tm, D_MODEL), row),
                _ada_spec(mod, tiles_per_group),
                _const_spec((1, D_MODEL)),
                _const_spec((D_MODEL, D_MODEL)),
                _const_spec((1, Q_LORA)),
                _const_spec((1, KV_LORA)),
                _const_spec((Q_LORA, hk)),
                _const_spec((1, LANES)),
                _const_spec((KV_LORA, hk)),
                _const_spec((1, LANES)),
                _const_spec((KV_LORA, hk)),
                _const_spec((2 * LANES, 2 * LANES))]
    gains = "rope" if latent else "plain"
    args = [x, mod[0], w["norm_mix"], w["w_in"], w["qln"], w["kvn"], w["wq"], w["gq_" + gains], w["wk"],
            w["gk_" + gains], w["wv"], w["e2"]]
    if latent:
        assert tm % GRID_W == 0
        in_specs.append(_const_spec(rope_tabs.shape))
        args.append(rope_tabs)
    slab = jax.ShapeDtypeStruct((MLA_HEADS, n, LANES), BF16)
    out_shape = (jax.ShapeDtypeStruct((n, POOL_WIDTH), F32), slab, slab, slab,
                 jax.ShapeDtypeStruct((n, KV_LORA), F32),
                 jax.ShapeDtypeStruct((n, MLA_ROPE), F32))
    out_specs = (pl.BlockSpec((tm, POOL_WIDTH), row),
                 _slab_spec(MLA_HEADS, tm), _slab_spec(MLA_HEADS, tm), _slab_spec(MLA_HEADS, tm),
                 pl.BlockSpec((tm, KV_LORA), row), pl.BlockSpec((tm, MLA_ROPE), row))
    return pl.pallas_call(
        functools.partial(_l0_pre_kernel, latent),
        out_shape=out_shape, grid=(n // tm,), in_specs=in_specs, out_specs=out_specs,
        compiler_params=_params(1), name="l0_pre_latent" if latent else "l0_pre_ctx",
    )(*args)


def _cache_kv_kernel(ckv_ref, kr_ref, wk_ref, gk_ref, wv_ref, e2_ref, k_ref, v_ref):
    _mla_keys_values(ckv_ref[...], kr_ref[...], wk_ref, gk_ref[...], wv_ref, e2_ref, None, k_ref, v_ref)


def _cache_kv(ckv, kr_slab, w):
    n = ckv.shape[0]
    tm = TOKEN_TILE
    row = lambda i: (i, 0)
    hk = MLA_HEADS * LANES
    slab = jax.ShapeDtypeStruct((MLA_HEADS, n, LANES), BF16)
    return pl.pallas_call(
        _cache_kv_kernel,
        out_shape=(slab, slab),
        grid=(n // tm,),
        in_specs=[pl.BlockSpec((tm, KV_LORA), row), pl.BlockSpec((tm, LANES), row),
                  _const_spec((KV_LORA, hk)), _const_spec((1, LANES)), _const_spec((KV_LORA, hk)),
                  _const_spec((2 * LANES, 2 * LANES))],
        out_specs=(_slab_spec(MLA_HEADS, tm), _slab_spec(MLA_HEADS, tm)),
        compiler_params=_params(1), name="l0_cache_kv",
    )(ckv, kr_slab, w["wk"], w["gk_plain"], w["wv"], w["e2"])


def _pool_mixer(seq_len, tile, prev_ref, u_ref, next_ref, wp_ref, ps_ref, buf_ref):
    tm = u_ref.shape[0]
    piece = min(tm, seq_len)
    n_pieces = tm // piece
    stride = piece + 2 * POOL_HALO
    starts = []
    for j in range(n_pieces):
        pos0 = (tile * tm + j * piece) % seq_len
        prev = prev_ref[...] if j == 0 else u_ref[j * piece - POOL_HALO:j * piece, :]
        nxt = next_ref[...] if j == n_pieces - 1 else u_ref[(j + 1) * piece:(j + 1) * piece + POOL_HALO, :]
        base = j * stride
        buf_ref[base:base + POOL_HALO, :] = jnp.where(pos0 != 0, prev, 0.0)
        buf_ref[base + POOL_HALO:base + POOL_HALO + piece, :] = u_ref[j * piece:(j + 1) * piece, :]
        buf_ref[base + POOL_HALO + piece:base + stride, :] = jnp.where(pos0 + piece != seq_len, nxt, 0.0)
        starts.append(pos0)
    def group(g):
        win = POOL_WINDOWS[g]
        lanes = pl.ds(g * POOL_GROUP, POOL_GROUP)
        pooled = []
        for j in range(n_pieces):
            centre = j * stride + POOL_HALO
            lo = starts[j] + lax.broadcasted_iota(jnp.int32, (piece, 1), 0) - win // 2
            cnt = (jnp.minimum(lo + win, seq_len) - jnp.maximum(lo, 0)).astype(F32)
            acc = buf_ref[pl.ds(centre - win // 2, piece), lanes]
            for d in range(1 - win // 2, win - win // 2):
                acc = acc + buf_ref[pl.ds(centre + d, piece), lanes]
            pooled.append(acc / cnt - buf_ref[pl.ds(centre, piece), lanes])
        pooled = pooled[0] if n_pieces == 1 else jnp.concatenate(pooled, axis=0)
        return (_dot(pooled.astype(BF16), wp_ref[g]) * ps_ref[:, lanes]).astype(BF16)

    return [functools.partial(group, g) for g in range(len(POOL_WINDOWS))]


def _attend(streams, running_max):
    items = [(si, ch) for si, (_, chunks, _) in enumerate(streams) for ch in chunks]

    def score(item):
        si, (k_fn, _, extra_fn) = item
        s = _dot_nt(k_fn(), streams[si][0])
        return s if extra_fn is None else s + extra_fn()

    state = [None] * len(streams)
    pending = [score(item) for item in items[:SCORE_LOOKAHEAD]]
    for i, (si, (_, v_fn, _)) in enumerate(items):
        s = pending.pop(0)
        if i + SCORE_LOOKAHEAD < len(items):
            pending.append(score(items[i + SCORE_LOOKAHEAD]))
        want_den = streams[si][2]
        m_new, alpha = None, 1.0
        if state[si] is not None:
            m, acc, den = state[si]
        if running_max:
            m_new = jnp.max(s, axis=0, keepdims=True)
            if state[si] is not None:
                m_new = jnp.maximum(m, m_new)
                alpha = jnp.exp2(m - m_new)
            s = s - m_new
        p = jnp.exp2(s)
        pv = _dot_tn(v_fn(), p.astype(BF16))
        psum = jnp.sum(p, axis=0, keepdims=True) if want_den else None
        if state[si] is None:
            state[si] = (m_new, pv, psum)
        elif running_max:
            state[si] = (m_new, acc * alpha + pv, den * alpha + psum if want_den else None)
        else:
            state[si] = (None, acc + pv, den + psum if want_den else None)
    return [(acc, den) for _, acc, den in state]


def _ref_chunks(k_ref, v_ref, slab, chunk, extra_fn=None, lo=0, hi=None):
    hi = k_ref.shape[1] if hi is None else hi
    out = []
    for c in range(lo, hi, chunk):
        rows = pl.ds(c, min(chunk, hi - c))
        out.append((lambda rows=rows: k_ref[slab, rows, :], lambda rows=rows: v_ref[slab, rows, :], extra_fn))
    return out


def _mla_finish(o_t):
    return o_t[:MLA_V] / o_t[MLA_V:MLA_V + 1]


def _na_queries(q2):
    low = lax.broadcasted_iota(jnp.int32, (1, LANES), 1) < NA_HEAD_DIM
    zero = jnp.zeros_like(q2)
    return jnp.concatenate([jnp.where(low, q2, zero), jnp.where(low, zero, q2)], axis=0)


def _na_finish(o_t, den):
    nq = o_t.shape[1] // 2
    o = o_t / den
    return jnp.concatenate([o[:NA_HEAD_DIM, :nq], o[NA_HEAD_DIM:, nq:]], axis=0)


def _attend_guarded(bound, streams_fn):
    return lax.cond(bound <= MAX_SCORE_BOUND,
                    lambda: _attend(streams_fn(False), False),
                    lambda: _attend(streams_fn(True), True))


def _mla_ctx_kernel(seq, bound_ref, q_ref, k_ref, v_ref, o_ref):
    spans = [(lo, lo + seq) for lo in range(0, k_ref.shape[1], seq)]
    res = _attend_guarded(bound_ref[0, 0], lambda _: [
        (q_ref[h, lo:hi, :], _ref_chunks(k_ref, v_ref, h, seq, lo=lo, hi=hi), False)
        for lo, hi in spans for h in range(MLA_HEADS)])
    for i, (lo, hi) in enumerate(spans):
        for hp in range(MLA_HEADS // 2):
            o_t = jnp.concatenate([_mla_finish(res[i * MLA_HEADS + h][0]) for h in (2 * hp, 2 * hp + 1)], axis=0)
            o_ref[lo:hi, hp * LANES:(hp + 1) * LANES] = o_t.T.astype(BF16)


def _na_ctx_kernel(seq, bound_ref, q_ref, k_ref, v_ref, o_ref):
    spans = [(lo, lo + seq) for lo in range(0, k_ref.shape[1], seq)]
    res = _attend_guarded(bound_ref[0, 0], lambda _: [
        (_na_queries(q_ref[s, lo:hi, :]), _ref_chunks(k_ref, v_ref, s, seq, lo=lo, hi=hi), True)
        for lo, hi in spans for s in range(NA_SLABS)])
    for i, (lo, hi) in enumerate(spans):
        for s in range(NA_SLABS):
            o_ref[lo:hi, s * LANES:(s + 1) * LANES] = _na_finish(*res[i * NA_SLABS + s]).T.astype(BF16)


def _ctx_attn(body, q, k, v, bound, seq, width, name):
    slabs, n, _ = q.shape
    rows = CTX_STEP_SEQS * seq
    spec = pl.BlockSpec((slabs, rows, LANES), lambda b: (0, b, 0))
    return pl.pallas_call(
        functools.partial(body, seq),
        out_shape=jax.ShapeDtypeStruct((n, width), BF16),
        grid=(n // rows,),
        in_specs=[pl.BlockSpec(memory_space=pltpu.SMEM), spec, spec, spec],
        out_specs=pl.BlockSpec((rows, width), lambda b: (b, 0)),
        compiler_params=_params(1), name=name,
    )(bound.reshape(1, 1), q, k, v)


def _mla_lat_kernel(bound_ref, q_ref, kl_ref, vl_ref, kc_ref, vc_ref, o_ref):
    batch, head0 = pl.program_id(0), pl.program_id(1) * MLA_STEP_HEADS
    heads = range(MLA_STEP_HEADS)
    bounds = [bound_ref[batch, head0 + h] for h in heads]

    def streams(running_max):
        size = MLA_ONLINE_CHUNK if running_max else MLA_KEY_CHUNK
        return [(q_ref[h], _ref_chunks(kl_ref, vl_ref, h, size) + _ref_chunks(kc_ref, vc_ref, h, size), False)
                for h in heads]

    res = _attend_guarded(functools.reduce(jnp.maximum, bounds), streams)
    o_ref[...] = jnp.concatenate([_mla_finish(o) for o, _ in res], axis=0).T.astype(BF16)


def _normed_len(gain, dim):
    return (dim ** 0.5) * jnp.max(jnp.abs(gain.astype(F32)))


def _mla_lat_attn(q, k, v, kc, vc, bound, seq, past):
    n = q.shape[1]
    tq = MLA_Q_TILE
    nq = seq // tq
    hs = MLA_STEP_HEADS
    return pl.pallas_call(
        _mla_lat_kernel,
        out_shape=jax.ShapeDtypeStruct((n, MLA_HEADS * MLA_V), BF16),
        grid=(n // seq, MLA_HEADS // hs, nq),
        in_specs=[pl.BlockSpec(memory_space=pltpu.SMEM),
                  pl.BlockSpec((hs, tq, LANES), lambda b, h, i: (h, b * nq + i, 0)),
                  pl.BlockSpec((hs, seq, LANES), lambda b, h, i: (h, b, 0)),
                  pl.BlockSpec((hs, seq, LANES), lambda b, h, i: (h, b, 0)),
                  pl.BlockSpec((hs, past, LANES), lambda b, h, i: (h, b, 0)),
                  pl.BlockSpec((hs, past, LANES), lambda b, h, i: (h, b, 0))],
        out_specs=pl.BlockSpec((tq, hs * MLA_V), lambda b, h, i: (b * nq + i, h)),
        compiler_params=_params(3), name="l0_mla_latent_attn",
    )(bound, q, k, v, kc, vc)


def _na_first_row(r, rows):
    return jnp.clip(r - NA_KH // 2, 0, rows - NA_KH)


def _na_lat_kernel(rows, bound_ref, q_ref, k_ref, v_ref, kc_ref, vc_ref, t2_ref, o_ref):
    nq = NA_TILE_ROWS * GRID_W
    nk = NA_WIN_ROWS * GRID_W
    n_off = 2 * NA_KH - 1
    r0 = pl.program_id(1) * NA_TILE_ROWS
    w0 = jnp.clip(r0 - NA_KH // 2, 0, rows - NA_WIN_ROWS)

    def block_index(j, t):
        key_row = w0 + j
        i = r0 + 2 * t
        first_u, first_l = _na_first_row(i, rows), _na_first_row(i + 1, rows)
        in_u = (key_row >= first_u) & (key_row < first_u + NA_KH)
        in_l = (key_row >= first_l) & (key_row < first_l + NA_KH)
        d = jnp.clip(key_row - i + NA_KH - 1, 0, n_off - 1)
        return jnp.where(in_u & in_l, d, jnp.where(in_u, n_off, jnp.where(in_l, n_off + 1, n_off + 2)))

    idx = [[block_index(j, t) for t in range(NA_TILE_ROWS // 2)] for j in range(NA_WIN_ROWS)]

    def local_bias(s):
        heads = [jnp.concatenate([jnp.concatenate([t2_ref[2 * s + hh, i] for i in row], axis=1) for row in idx], axis=0)
                 for hh in range(2)]
        return jnp.concatenate(heads, axis=1)

    batch = pl.program_id(0)
    bounds = [bound_ref[batch, h] for h in range(NA_HEADS)]

    def cache_chunk(s):
        lanes = pl.ds(s * LANES, LANES)
        return [(lambda: kc_ref[:, lanes].astype(BF16), lambda: vc_ref[:, lanes].astype(BF16), None)]

    res = _attend_guarded(
        functools.reduce(jnp.maximum, bounds),
        lambda _: [(_na_queries(q_ref[s]),
                    _ref_chunks(k_ref, v_ref, s, nk, functools.partial(local_bias, s)) + cache_chunk(s),
                    True) for s in range(NA_SLABS)])
    for s in range(NA_SLABS):
        o_ref[:, s * LANES:(s + 1) * LANES] = _na_finish(*res[s]).T.astype(BF16)


def _na_lat_attn(q, k, v, kc, vc, t2, bound, seq, past):
    slabs, n, _ = q.shape
    rows = seq // GRID_W
    tiles = rows // NA_TILE_ROWS
    nq = NA_TILE_ROWS * GRID_W
    nk = NA_WIN_ROWS * GRID_W

    def win_map(b, t):
        w0 = jnp.clip(t * NA_TILE_ROWS - NA_KH // 2, 0, rows - NA_WIN_ROWS)
        return (0, pl.multiple_of(b * seq + w0 * GRID_W, GRID_W), 0)

    win_spec = pl.BlockSpec((pl.Element(slabs), pl.Element(nk), pl.Element(LANES)), win_map)
    ctx_spec = pl.BlockSpec((past, slabs * LANES), lambda b, t: (b, 0))
    return pl.pallas_call(
        functools.partial(_na_lat_kernel, rows),
        out_shape=jax.ShapeDtypeStruct((n, slabs * LANES), BF16),
        grid=(n // seq, tiles),
        in_specs=[pl.BlockSpec(memory_space=pltpu.SMEM),
                  pl.BlockSpec((slabs, nq, LANES), lambda b, t: (0, b * tiles + t, 0)),
                  win_spec, win_spec, ctx_spec, ctx_spec,
                  _const_spec(t2.shape)],
        out_specs=pl.BlockSpec((nq, slabs * LANES), lambda b, t: (b * tiles + t, 0)),
        compiler_params=_params(2), name="l1_na_latent_attn",
    )(bound, q, k, v, kc, vc, t2)


def _pair_norm(x2, g2):
    low = lax.broadcasted_iota(jnp.int32, (1, LANES), 1) < NA_HEAD_DIM
    sq = x2 * x2
    ss_all = jnp.sum(sq, axis=-1, keepdims=True)
    ss_low = jnp.sum(jnp.where(low, sq, 0.0), axis=-1, keepdims=True)
    ss = jnp.where(low, ss_low, ss_all - ss_low)
    return x2 * lax.rsqrt(ss * (1.0 / NA_HEAD_DIM) + EPS) * g2


def _l1_pre_kernel(keep_f32, x_ref, mod_ref, nw_ref, win_ref, gq_ref, gk_ref, q_ref, k_ref, v_ref, *f32_refs):
    h = _modulate(x_ref[...], nw_ref[...], _ada_row(mod_ref, 0), _ada_row(mod_ref, 1)).astype(BF16)
    width = NA_SLABS * LANES
    q = _dot(h, win_ref[:, 0:width])
    k = _dot(h, win_ref[:, width:2 * width])
    v = _dot(h, win_ref[:, 2 * width:3 * width])
    gq = gq_ref[...]
    gk = gk_ref[...]
    for s in range(NA_SLABS):
        ls = slice(s * LANES, (s + 1) * LANES)
        q_ref[s] = _pair_norm(q[:, ls], gq).astype(BF16)
        kn = _pair_norm(k[:, ls], gk)
        k_ref[s] = kn.astype(BF16)
        v_ref[s] = v[:, ls].astype(BF16)
        if keep_f32:
            f32_refs[0][:, ls] = kn
    if keep_f32:
        f32_refs[1][...] = v


def _l1_pre(x, mod, tiles_per_group, keep_f32, w):
    n = x.shape[0]
    tm = TOKEN_TILE
    row = lambda i: (i, 0)
    width = NA_SLABS * LANES
    slab = jax.ShapeDtypeStruct((NA_SLABS, n, LANES), BF16)
    out_shape = [slab, slab, slab]
    out_specs = [_slab_spec(NA_SLABS, tm)] * 3
    if keep_f32:
        out_shape += [jax.ShapeDtypeStruct((n, width), F32)] * 2
        out_specs += [pl.BlockSpec((tm, width), row)] * 2
    return pl.pallas_call(
        functools.partial(_l1_pre_kernel, keep_f32),
        out_shape=tuple(out_shape), grid=(n // tm,),
        in_specs=[pl.BlockSpec((tm, D_MODEL), row),
                  _ada_spec(mod, tiles_per_group),
                  _const_spec((1, D_MODEL)),
                  _const_spec((D_MODEL, 3 * width)),
                  _const_spec((1, LANES)),
                  _const_spec((1, LANES))],
        out_specs=tuple(out_specs),
        compiler_params=_params(1), name="l1_pre_ctx" if keep_f32 else "l1_pre_latent",
    )(x, mod[0], w["norm_mix"], w["w_in"], w["gq"], w["gk"])


class _WeightCaster:
    def __init__(self, pieces, stage_ref, sems):
        slots = stage_ref.shape[0]
        self.pieces, self.stage, self.slots, self.done = pieces, stage_ref, slots, 0
        self.copies = [pltpu.make_async_copy(src, stage_ref.at[i % slots], sems.at[i % slots])
                       for i, (src, _) in enumerate(pieces)]
        for copy in self.copies[:slots]:
            copy.start()

    def finish(self, count):
        for i in range(self.done, self.done + count):
            self.copies[i].wait()
            self.pieces[i][1][...] = self.stage[i % self.slots].astype(BF16)
            if i + self.slots < len(self.copies):
                self.copies[i + self.slots].start()
        self.done += count


def _weight_pieces(srcs, dsts, rows):
    wo_s, w1_s, w2_s = srcs
    wo_d, w1_d, w2_d = dsts
    pieces = [(wo_s.at[pl.ds(r, rows), :], wo_d.at[pl.ds(r, rows), :]) for r in range(0, D_MODEL, rows)]
    for c in range(0, D_FF, FF_CHUNK):
        cols = pl.ds(c, FF_CHUNK)
        pieces += [(w1_s.at[pl.ds(r, rows), cols], w1_d.at[pl.ds(r, rows), cols]) for r in range(0, D_MODEL, rows)]
        pieces += [(w2_s.at[pl.ds(c + r, rows), :], w2_d.at[pl.ds(c + r, rows), :]) for r in range(0, FF_CHUNK, rows)]
    return pieces


def _post_kernel(pool_seq, cast, x_ref, attn_ref, *rest):
    rest = list(rest)
    take = lambda k: [rest.pop(0) for _ in range(k)]
    pool_in = take(8) if pool_seq is not None else None
    mod_ref, nw_ref, wo_ref, w1_ref, w2_ref = take(5)
    (o_ref,) = take(1)
    w16_out = take(3) if cast else None
    pool_scratch = take(2) if pool_seq is not None else None
    step = pl.program_id(0)
    n_chunks = D_FF // FF_CHUNK
    if cast:
        sources = (wo_ref, w1_ref, w2_ref)
        wo_ref, w1_ref, w2_ref, stage_ref, sem_in, sem_out = take(6)
        to_hbm = [pltpu.make_async_copy(v, o, sem_out.at[i])
                  for i, (v, o) in enumerate(zip((wo_ref, w1_ref, w2_ref), w16_out))]
        rows = stage_ref.shape[1]
        proj_pieces, chunk_pieces = D_MODEL // rows, (D_MODEL + FF_CHUNK) // rows
    if pool_seq is not None:
        first_refs, next_refs, (wp_ref, ps_ref) = pool_in[0:3], pool_in[3:6], pool_in[6:8]
        buf_ref, pooled_ref = pool_scratch

        def first_pooling():
            for g, group in enumerate(_pool_mixer(pool_seq, 0, *first_refs, wp_ref, ps_ref, buf_ref)):
                pooled_ref[:, g * POOL_GROUP:(g + 1) * POOL_GROUP] = group()

    def body(caster):
        if caster is not None:
            caster.finish(proj_pieces)
        if pool_seq is None:
            mixed = _dot(attn_ref[...], wo_ref[...])
        else:
            mixed = _dot(jnp.concatenate([pooled_ref[...], attn_ref[...]], axis=-1), wo_ref[...])
            next_groups = _pool_mixer(pool_seq, jnp.minimum(step + 1, pl.num_programs(0) - 1), *next_refs,
                                      wp_ref, ps_ref, buf_ref)
        x1 = x_ref[...] + _ada_row(mod_ref, 2) * mixed
        h = _modulate(x1, nw_ref[...], _ada_row(mod_ref, 3), _ada_row(mod_ref, 4)).astype(BF16)
        acc = jnp.zeros(x1.shape, F32)
        for c in range(n_chunks):
            if caster is not None:
                caster.finish(chunk_pieces)
            cs = pl.ds(c * FF_CHUNK, FF_CHUNK)
            u = jnp.square(jnp.maximum(_dot(h, w1_ref[:, cs]), 0.0)).astype(BF16)
            acc = acc + _dot(u, w2_ref[cs, :])
            if pool_seq is not None:
                for g in range(c * len(next_groups) // n_chunks, (c + 1) * len(next_groups) // n_chunks):
                    pooled_ref[:, g * POOL_GROUP:(g + 1) * POOL_GROUP] = next_groups[g]()
        o_ref[...] = x1 + _ada_row(mod_ref, 5) * acc

    if not cast:
        if pool_seq is not None:
            pl.when(step == 0)(first_pooling)
        body(None)
        return

    @pl.when(step == 0)
    def _():
        caster = _WeightCaster(_weight_pieces(sources, (wo_ref, w1_ref, w2_ref), rows), stage_ref, sem_in)
        if pool_seq is not None:
            first_pooling()
        body(caster)
        for copy in to_hbm:
            copy.start()

    @pl.when(step > 0)
    def _():
        body(None)

    @pl.when(step == pl.num_programs(0) - 1)
    def _():
        for copy in to_hbm:
            copy.wait()


POST_STAGE_ROWS = 512
POST_STAGE_SLOTS = 4


def _post(x, attn, mod, tiles_per_group, w, name, pool=None, cast=False):
    n = x.shape[0]
    tm = TOKEN_TILE
    row = lambda i: (i, 0)
    in_specs = [pl.BlockSpec((tm, D_MODEL), row), pl.BlockSpec((tm, attn.shape[1]), row)]
    args = [x, attn]
    scratch = []
    pool_seq = None
    if pool is not None:
        a_in, pool_seq = pool
        hb = tm // POOL_HALO
        last = n // POOL_HALO - 1
        pieces = tm // min(tm, pool_seq)
        last_tile = n // tm - 1

        def tile_specs(tile_of):
            return [pl.BlockSpec((POOL_HALO, POOL_WIDTH), lambda i: (jnp.maximum(tile_of(i) * hb - 1, 0), 0)),
                    pl.BlockSpec((tm, POOL_WIDTH), lambda i: (tile_of(i), 0)),
                    pl.BlockSpec((POOL_HALO, POOL_WIDTH), lambda i: (jnp.minimum((tile_of(i) + 1) * hb, last), 0))]

        in_specs += tile_specs(lambda i: 0) + tile_specs(lambda i: jnp.minimum(i + 1, last_tile))
        in_specs += [_const_spec((len(POOL_WINDOWS), POOL_GROUP, POOL_GROUP)), _const_spec((1, POOL_WIDTH))]
        args += [a_in] * 6 + [w["w_pool"], w["pool_scale"]]
        scratch = [pltpu.VMEM((tm + 2 * POOL_HALO * pieces, POOL_WIDTH), F32), pltpu.VMEM((tm, POOL_WIDTH), BF16)]
    weights = [w["w_out"], w["w_mlp1"], w["w_mlp2"]]
    in_specs += [_ada_spec(mod, tiles_per_group), _const_spec((1, D_MODEL))]
    out_shape = [jax.ShapeDtypeStruct((n, D_MODEL), F32)]
    out_specs = [pl.BlockSpec((tm, D_MODEL), row)]
    if cast:
        in_specs += [pl.BlockSpec(memory_space=pl.ANY)] * 3
        out_shape += [jax.ShapeDtypeStruct(wt.shape, BF16) for wt in weights]
        out_specs += [pl.BlockSpec(memory_space=pl.ANY)] * 3
        scratch = scratch + [pltpu.VMEM(wt.shape, BF16) for wt in weights]
        scratch += [pltpu.VMEM((POST_STAGE_SLOTS, POST_STAGE_ROWS, D_MODEL), F32),
                    pltpu.SemaphoreType.DMA((POST_STAGE_SLOTS,)), pltpu.SemaphoreType.DMA((3,))]
    else:
        in_specs += [_const_spec(wt.shape) for wt in weights]
    args += [mod[0], w["norm_mlp"]] + weights
    outs = pl.pallas_call(
        functools.partial(_post_kernel, pool_seq, cast),
        out_shape=tuple(out_shape), grid=(n // tm,),
        in_specs=in_specs, out_specs=tuple(out_specs),
        scratch_shapes=scratch, compiler_params=_params(1), name=name,
    )(*args)
    return (outs[0], dict(w, w_out=outs[1], w_mlp1=outs[2], w_mlp2=outs[3])) if cast else outs[0]


def _rope_tables(seq):
    assert seq == GRID_W * GRID_W
    half = MLA_ROPE // 2
    inv_freq = jnp.power(ROPE_THETA, -jnp.arange(0, half, 2, dtype=F32) / half)
    ang = jnp.arange(GRID_W, dtype=F32)[:, None] * inv_freq[None, :]
    cos, sin = jnp.cos(ang), jnp.sin(ang)
    fill = lambda n, v=0.0: jnp.full((GRID_W, n), v, F32)
    spare = LANES - MLA_QK
    return jnp.stack([
        jnp.concatenate([fill(MLA_NOPE, 1.0), fill(half), cos, cos, fill(spare)], axis=1),
        jnp.concatenate([fill(MLA_NOPE), cos, cos, fill(half), fill(spare)], axis=1),
        jnp.concatenate([fill(MLA_NOPE), fill(half), -sin, sin, fill(spare)], axis=1),
        jnp.concatenate([fill(MLA_NOPE), -sin, sin, fill(half), fill(spare)], axis=1)])


def _rope_partner(tail):
    shape = tail.shape
    t = tail.reshape(shape[:-1] + (2, 2, MLA_ROPE // 4))
    return t[..., ::-1, :].reshape(shape)


def _na_bias_table(rel_bias, rows):
    qcol = jnp.arange(GRID_W)
    kcol = jnp.arange(GRID_W)
    col_start = jnp.clip(qcol - NA_KW // 2, 0, GRID_W - NA_KW)
    col_mask = (kcol[:, None] >= col_start[None, :]) & (kcol[:, None] < col_start[None, :] + NA_KW)
    offs = jnp.arange(2 * NA_KW - 1)
    onehot = (kcol[None, :, None] - qcol[None, None, :] + NA_KW - 1 == offs[:, None, None]).astype(F32)
    blocks = jnp.einsum('hdj,jkq->hdkq', rel_bias.astype(F32) * LOG2E, onehot, precision=lax.Precision.HIGHEST)
    blocks = jnp.where(col_mask[None, None], blocks, NEG_INF)
    neg = jnp.full_like(blocks[:, :1], NEG_INF)
    both = jnp.concatenate([blocks, jnp.concatenate([neg, blocks[:, :-1]], axis=1)], axis=-1)
    d_upper, d_lower = _na_partial_offsets(rows)
    upper_only = jnp.concatenate([blocks[:, d_upper:d_upper + 1], neg], axis=-1)
    lower_only = jnp.concatenate([neg, blocks[:, d_lower - 1:d_lower]], axis=-1)
    return jnp.concatenate([both, upper_only, lower_only, jnp.concatenate([neg, neg], axis=-1)], axis=1)


def _na_partial_offsets(rows):
    first = lambda r: min(max(r - NA_KH // 2, 0), rows - NA_KH)
    upper, lower = set(), set()
    for r0 in range(0, rows, NA_TILE_ROWS):
        w0 = min(max(r0 - NA_KH // 2, 0), rows - NA_WIN_ROWS)
        for j in range(NA_WIN_ROWS):
            for t in range(NA_TILE_ROWS // 2):
                key_row, i = w0 + j, r0 + 2 * t
                in_u = first(i) <= key_row < first(i) + NA_KH
                in_l = first(i + 1) <= key_row < first(i + 1) + NA_KH
                d = key_row - i + NA_KH - 1
                if in_u and in_l:
                    assert 1 <= d <= 2 * NA_KH - 2
                elif in_u:
                    upper.add(d)
                elif in_l:
                    lower.add(d)
    assert len(upper) == 1 and len(lower) == 1, (upper, lower)
    return upper.pop(), lower.pop()


def _prep_l0(w_in, q_lora_norm, kv_lora_norm, w_q_up, w_kv_up, mla_q_norm, mla_k_norm, w_pool, pool_scale,
             w_out, norm_mix, norm_mlp, w_mlp1, w_mlp2):
    lat_end = POOL_WIDTH + Q_LORA + KV_LORA
    w_in_pad = jnp.concatenate([w_in[:, :lat_end], jnp.zeros((D_MODEL, MLA_NOPE), w_in.dtype),
                                w_in[:, lat_end:], _rope_partner(w_in[:, lat_end:])], axis=1)
    pad = LANES - MLA_QK
    hk = MLA_HEADS * LANES
    wq = w_q_up.reshape(Q_LORA, MLA_HEADS, MLA_QK)
    wq = jnp.concatenate([wq, _rope_partner(wq[:, :, MLA_NOPE:])], axis=-1)
    kv = w_kv_up.reshape(KV_LORA, MLA_HEADS, MLA_NOPE + MLA_V)
    wk = jnp.pad(kv[:, :, :MLA_NOPE], ((0, 0), (0, 0), (0, LANES - MLA_NOPE)))
    wv = jnp.pad(kv[:, :, MLA_NOPE:], ((0, 0), (0, 0), (0, LANES - MLA_V)))
    gq = mla_q_norm * (MLA_QK ** -0.5 * LOG2E)
    slab_lane = jnp.arange(2 * LANES)
    e2 = (slab_lane[:, None] // LANES == slab_lane[None, :] // LANES) & (slab_lane[:, None] % LANES < MLA_QK)
    return dict(
        norm_mix=norm_mix.reshape(1, -1), norm_mlp=norm_mlp.reshape(1, -1),
        w_in=w_in_pad.astype(BF16),
        qln=q_lora_norm.reshape(1, -1), kvn=kv_lora_norm.reshape(1, -1),
        wq=wq.reshape(Q_LORA, hk).astype(BF16),
        gq_plain=jnp.pad(gq, (0, pad)).reshape(1, LANES),
        gq_rope=jnp.concatenate([gq, _rope_partner(gq[MLA_NOPE:])]).reshape(1, LANES),
        wk=wk.reshape(KV_LORA, hk).astype(BF16),
        gk_plain=jnp.pad(mla_k_norm, (0, pad)).reshape(1, LANES),
        gk_rope=jnp.concatenate([mla_k_norm, _rope_partner(mla_k_norm[MLA_NOPE:])]).reshape(1, LANES),
        wv=wv.reshape(KV_LORA, hk).astype(BF16),
        e2=e2.astype(BF16),
        w_pool=w_pool.astype(BF16), pool_scale=pool_scale.reshape(1, -1),
        w_out=w_out, w_mlp1=w_mlp1, w_mlp2=w_mlp2)


def kernel(x_prompt, x_sample, cache_l0_mla_ckv, cache_l0_mla_krope, cache_l1_na_k, cache_l1_na_v, c, c_ctx, w_ada_l0, b_ada_l0, norm_mix_l0, norm_mlp_l0, w_mlp1_l0, w_mlp2_l0, w_in_l0, q_lora_norm_l0, kv_lora_norm_l0, w_q_up_l0, w_kv_up_l0, mla_q_norm_l0, mla_k_norm_l0, w_pool_l0, pool_scale_l0, w_out_l0, w_ada_l1, b_ada_l1, norm_mix_l1, norm_mlp_l1, w_mlp1_l1, w_mlp2_l1, w_in_l1, na_q_norm_l1, na_k_norm_l1, rel_bias_l1, w_out_l1):
    batch, seq, _ = x_prompt.shape
    dec_batch, dec_seq, _ = x_sample.shape
    past = cache_l0_mla_ckv.shape[1]
    xp = x_prompt.reshape(batch * seq, D_MODEL)
    xs = x_sample.reshape(dec_batch * dec_seq, D_MODEL)
    tiles_p = (batch * seq) // TOKEN_TILE
    tiles_s = dec_seq // TOKEN_TILE

    cond8 = jnp.concatenate([c_ctx[None, :], c, jnp.zeros((SUBLANES - 1 - dec_batch, D_MODEL), F32)], axis=0)
    mods = _ada(cond8, (w_ada_l0, w_ada_l1), (b_ada_l0, b_ada_l1))
    mods = mods.reshape(DEPTH, SUBLANES, 1, N_ADA * D_MODEL)
    (mod0_p, mod0_s), (mod1_p, mod1_s) = (((mods[l], 0), (mods[l], 1)) for l in range(DEPTH))

    w0 = _prep_l0(w_in_l0, q_lora_norm_l0, kv_lora_norm_l0, w_q_up_l0, w_kv_up_l0, mla_q_norm_l0, mla_k_norm_l0,
                  w_pool_l0, pool_scale_l0, w_out_l0, norm_mix_l0, norm_mlp_l0, w_mlp1_l0, w_mlp2_l0)
    a_p, q_p, k_p, v_p, ckv_p, kr_p = _l0_pre(xp, mod0_p, tiles_p, False, w0, None)
    a_s, q_s, k_s, v_s, _, _ = _l0_pre(xs, mod0_s, tiles_s, True, w0, _rope_tables(dec_seq))
    kr_cache = jnp.pad(cache_l0_mla_krope.reshape(dec_batch * past, MLA_ROPE), ((0, 0), (MLA_NOPE, LANES - MLA_QK)))
    k_c, v_c = _cache_kv(cache_l0_mla_ckv.reshape(dec_batch * past, KV_LORA), kr_cache, w0)
    mla_bound = _normed_len(mla_q_norm_l0 * (MLA_QK ** -0.5 * LOG2E), MLA_QK) * _normed_len(mla_k_norm_l0, MLA_QK)
    attn_p = _ctx_attn(_mla_ctx_kernel, q_p, k_p, v_p, mla_bound, seq, MLA_HEADS * MLA_V, "l0_mla_ctx_attn")
    attn_s = _mla_lat_attn(q_s, k_s, v_s, k_c, v_c, jnp.full((dec_batch, MLA_HEADS), mla_bound, F32), dec_seq, past)
    xp, w0 = _post(xp, attn_p, mod0_p, tiles_p, w0, "l0_post_ctx", pool=(a_p, seq), cast=True)
    xs = _post(xs, attn_s, mod0_s, tiles_s, w0, "l0_post_latent", pool=(a_s, dec_seq))

    w1 = dict(norm_mix=norm_mix_l1.reshape(1, -1), norm_mlp=norm_mlp_l1.reshape(1, -1),
              w_in=w_in_l1.astype(BF16),
              gq=jnp.tile(na_q_norm_l1 * (NA_HEAD_DIM ** -0.5 * LOG2E), 2).reshape(1, LANES),
              gk=jnp.tile(na_k_norm_l1, 2).reshape(1, LANES),
              w_out=w_out_l1, w_mlp1=w_mlp1_l1, w_mlp2=w_mlp2_l1)
    q1_p, k1_p, v1_p, k1_new, v1_new = _l1_pre(xp, mod1_p, tiles_p, True, w1)
    q1_s, k1_s, v1_s = _l1_pre(xs, mod1_s, tiles_s, False, w1)
    q_len, k_len = _normed_len(w1["gq"], NA_HEAD_DIM), _normed_len(w1["gk"], NA_HEAD_DIM)
    na_p = _ctx_attn(_na_ctx_kernel, q1_p, k1_p, v1_p, q_len * k_len, seq, NA_SLABS * LANES, "l1_na_ctx_attn")
    cache_len = jnp.sqrt(jnp.max(jnp.sum(jnp.square(cache_l1_na_k.astype(F32)), axis=-1), axis=1))
    na_bound = (q_len * jnp.maximum(k_len, cache_len)
                + jnp.max(jnp.abs(rel_bias_l1.astype(F32)) * LOG2E, axis=(1, 2))[None, :])
    na_s = _na_lat_attn(q1_s, k1_s, v1_s,
                        cache_l1_na_k.reshape(dec_batch * past, -1), cache_l1_na_v.reshape(dec_batch * past, -1),
                        _na_bias_table(rel_bias_l1, dec_seq // GRID_W), na_bound, dec_seq, past)
    xp, w1 = _post(xp, na_p, mod1_p, tiles_p, w1, "l1_post_ctx", cast=True)
    xs = _post(xs, na_s, mod1_s, tiles_s, w1, "l1_post_latent")

    return (xp.reshape(batch, seq, D_MODEL), xs.reshape(dec_batch, dec_seq, D_MODEL),
            ckv_p.reshape(batch, seq, KV_LORA), kr_p.reshape(batch, seq, MLA_ROPE),
            k1_new.reshape(batch, seq, NA_HEADS, NA_HEAD_DIM), v1_new.reshape(batch, seq, NA_HEADS, NA_HEAD_DIM))
```

```python
import functools

import jax
import jax.numpy as jnp
from jax import lax
from jax.experimental import pallas as pl
from jax.experimental.pallas import tpu as pltpu

F32 = jnp.float32
BF16 = jnp.bfloat16

D_MODEL = 1024
D_FF = 4 * D_MODEL
DEPTH = 2
N_ADA = 6
EPS = 1e-6
NEG_INF = -1e30
ROPE_THETA = 10000.0
GRID_W = 64
POOL_WIDTH = 512
POOL_WINDOWS = (2, 4, 8, 16)
POOL_GROUP = 128
POOL_HALO = 8
MLA_HEADS = 8
MLA_NOPE = 64
MLA_ROPE = 32
MLA_QK = MLA_NOPE + MLA_ROPE
MLA_V = 64
Q_LORA = 256
KV_LORA = 128
NA_HEADS = 16
NA_HEAD_DIM = 64
NA_KH = 8
NA_KW = 16
LOG2E = 1.4426950408889634

LANES = 128
SUBLANES = 8
VMEM_LIMIT = 56 * 1024 * 1024

TOKEN_TILE = 512
MLA_Q_TILE = 512
MLA_STEP_HEADS = 2
MLA_KEY_CHUNK = 2048
MLA_ONLINE_CHUNK = 1024
SCORE_LOOKAHEAD = 4
MAX_SCORE_BOUND = 40.0
FF_CHUNK = 1024
CTX_STEP_SEQS = 2
NA_SLABS = NA_HEADS * NA_HEAD_DIM // LANES
NA_TILE_ROWS = 4
NA_WIN_ROWS = NA_TILE_ROWS + NA_KH - 1


def _params(n_axes):
    return pltpu.CompilerParams(dimension_semantics=("arbitrary",) * n_axes,
                                vmem_limit_bytes=VMEM_LIMIT)


def _const_spec(shape):
    zeros = (0,) * len(shape)
    return pl.BlockSpec(shape, lambda *_: zeros, pipeline_mode=pl.Buffered(1))


def _dot(a, b):
    return jnp.dot(a, b, preferred_element_type=F32)


def _dot_nt(a, b):
    return lax.dot_general(a, b, (((1,), (1,)), ((), ())), preferred_element_type=F32)


def _dot_tn(a, b):
    return lax.dot_general(a, b, (((0,), (0,)), ((), ())), preferred_element_type=F32)


def _rms(x, w, n=None):
    n = x.shape[-1] if n is None else n
    ss = jnp.sum(x * x, axis=-1, keepdims=True)
    return x * lax.rsqrt(ss * (1.0 / n) + EPS) * w


def _modulate(x, nw, shift, scale):
    return _rms(x, nw) * (1.0 + scale) + shift


def _ada_spec(mod, tiles_per_group):
    mods, first_row = mod
    return pl.BlockSpec((None, 1, mods.shape[-1]), lambda i: (first_row + i // tiles_per_group, 0, 0))


def _ada_row(mod_ref, k):
    return mod_ref[:, k * D_MODEL:(k + 1) * D_MODEL]


def _ada_kernel(cond_ref, w0_ref, w1_ref, b_ref, o_ref):
    c = cond_ref[...]
    s = (c * jax.nn.sigmoid(c)).astype(BF16)
    for layer, w_ref in enumerate((w0_ref, w1_ref)):
        @pl.when(pl.program_id(0) == layer)
        def _(w_ref=w_ref):
            o_ref[...] = _dot(s, w_ref[...].astype(BF16)) + b_ref[...]


def _ada(cond8, w_ada, b_ada):
    n_out = w_ada[0].shape[1]
    tn = D_MODEL
    nb = n_out // tn
    return pl.pallas_call(
        _ada_kernel,
        out_shape=jax.ShapeDtypeStruct((DEPTH, SUBLANES, n_out), F32),
        grid=(DEPTH, nb),
        in_specs=[pl.BlockSpec((SUBLANES, D_MODEL), lambda l, j: (0, 0)),
                  pl.BlockSpec((D_MODEL, tn), lambda l, j: (0, jnp.where(l == 0, j, nb - 1))),
                  pl.BlockSpec((D_MODEL, tn), lambda l, j: (0, jnp.where(l == 1, j, 0))),
                  pl.BlockSpec((None, 1, tn), lambda l, j: (l, 0, j))],
        out_specs=pl.BlockSpec((None, SUBLANES, tn), lambda l, j: (l, 0, j)),
        compiler_params=_params(2),
        name="ada",
    )(cond8, w_ada[0], w_ada[1], jnp.stack(b_ada).reshape(DEPTH, 1, n_out))


def _rope(x, c, s):
    return x * c + pltpu.roll(x, LANES - MLA_ROPE, 1) * s


PAIR = 2 * LANES


def _head_rinv(raw, e2_ref):
    ss = _dot((raw * raw).astype(BF16), e2_ref[...])
    return lax.rsqrt(ss * (1.0 / MLA_QK) + EPS)


def _mla_keys_values(ckv, kr_slab, wk_ref, gk, wv_ref, e2_ref, rope_tabs, k_ref, v_ref):
    c16 = ckv.astype(BF16)
    tail = kr_slab * gk
    if rope_tabs is not None:
        tail = _rope(tail, *rope_tabs)
    ones_lane = (lax.broadcasted_iota(jnp.int32, (1, LANES), 1) == MLA_V).astype(F32)
    for p in range(MLA_HEADS // 2):
        cols = pl.ds(p * PAIR, PAIR)
        kn = _dot(c16, wk_ref[:, cols])
        v = _dot(c16, wv_ref[:, cols])
        rinv = _head_rinv(kn + jnp.tile(kr_slab, (1, 2)), e2_ref)
        for i in range(2):
            hs = slice(i * LANES, (i + 1) * LANES)
            k_ref[2 * p + i] = (rinv[:, hs] * (kn[:, hs] * gk + tail)).astype(BF16)
            v_ref[2 * p + i] = (v[:, hs] + ones_lane).astype(BF16)


def _l0_pre_kernel(latent, x_ref, mod_ref, nw_ref, win_ref, qln_ref, kvn_ref, wq_ref, gq_ref,
                   wk_ref, gk_ref, wv_ref, e2_ref, *rest):
    if latent:
        rope_ref, a_ref, q_ref, k_ref, v_ref, ckv_ref, kr_ref = rest
        tile_rows = x_ref.shape[0] // GRID_W
        row0 = (pl.program_id(0) * tile_rows) % rope_ref.shape[1]
        expand = lambda col, row: jnp.concatenate(
            [rope_ref[col] + rope_ref[row, pl.ds(row0 + r, 1), :] for r in range(tile_rows)], axis=0)
        rope_tabs = (expand(0, 1), expand(2, 3))
    else:
        a_ref, q_ref, k_ref, v_ref, ckv_ref, kr_ref = rest
        rope_tabs = None
    h = _modulate(x_ref[...], nw_ref[...], _ada_row(mod_ref, 0), _ada_row(mod_ref, 1)).astype(BF16)
    proj = _dot(h, win_ref[...])
    a_ref[...] = proj[:, :POOL_WIDTH]
    q_lat = proj[:, POOL_WIDTH:POOL_WIDTH + Q_LORA]
    kv_lat = proj[:, POOL_WIDTH + Q_LORA:POOL_WIDTH + Q_LORA + KV_LORA]
    kr_slab = proj[:, POOL_WIDTH + Q_LORA + KV_LORA:]
    ql = _rms(q_lat, qln_ref[...]).astype(BF16)
    gq = gq_ref[...]
    for p in range(MLA_HEADS // 2):
        q = _dot(ql, wq_ref[:, pl.ds(p * PAIR, PAIR)])
        q_rinv = _head_rinv(q, e2_ref)
        for i in range(2):
            hs = slice(i * LANES, (i + 1) * LANES)
            y = q[:, hs] * q_rinv[:, hs] * gq
            if rope_tabs is not None:
                y = _rope(y, *rope_tabs)
            q_ref[2 * p + i] = y.astype(BF16)
    ckv = _rms(kv_lat, kvn_ref[...])
    ckv_ref[...] = ckv
    kr_ref[...] = kr_slab[:, MLA_NOPE:MLA_QK]
    _mla_keys_values(ckv, kr_slab, wk_ref, gk_ref[...], wv_ref, e2_ref, rope_tabs, k_ref, v_ref)


def _slab_spec(slabs, tm):
    return pl.BlockSpec((slabs, tm, LANES), lambda i: (0, i, 0))


def _l0_pre(x, mod, tiles_per_group, latent, w, rope_tabs):
    n = x.shape[0]
    tm = TOKEN_TILE
    row = lambda i: (i, 0)
    hk = MLA_HEADS * LANES
    in_specs = [pl.BlockSpec((tm, D_MODEL), row),
                _ada_spec(mod, tiles_per_group),
                _const_spec((1, D_MODEL)),
                _const_spec((D_MODEL, D_MODEL)),
                _const_spec((1, Q_LORA)),
                _const_spec((1, KV_LORA)),
                _const_spec((Q_LORA, hk)),
                _const_spec((1, LANES)),
                _const_spec((KV_LORA, hk)),
                _const_spec((1, LANES)),
                _const_spec((KV_LORA, hk)),
                _const_spec((2 * LANES, 2 * LANES))]
    gains = "rope" if latent else "plain"
    args = [x, mod[0], w["norm_mix"], w["w_in"], w["qln"], w["kvn"], w["wq"], w["gq_" + gains], w["wk"],
            w["gk_" + gains], w["wv"], w["e2"]]
    if latent:
        assert tm % GRID_W == 0
        in_specs.append(_const_spec(rope_tabs.shape))
        args.append(rope_tabs)
    slab = jax.ShapeDtypeStruct((MLA_HEADS, n, LANES), BF16)
    out_shape = (jax.ShapeDtypeStruct((n, POOL_WIDTH), F32), slab, slab, slab,
                 jax.ShapeDtypeStruct((n, KV_LORA), F32),
                 jax.ShapeDtypeStruct((n, MLA_ROPE), F32))
    out_specs = (pl.BlockSpec((tm, POOL_WIDTH), row),
                 _slab_spec(MLA_HEADS, tm), _slab_spec(MLA_HEADS, tm), _slab_spec(MLA_HEADS, tm),
                 pl.BlockSpec((tm, KV_LORA), row), pl.BlockSpec((tm, MLA_ROPE), row))
    return pl.pallas_call(
        functools.partial(_l0_pre_kernel, latent),
        out_shape=out_shape, grid=(n // tm,), in_specs=in_specs, out_specs=out_specs,
        compiler_params=_params(1), name="l0_pre_latent" if latent else "l0_pre_ctx",
    )(*args)


def _cache_kv_kernel(ckv_ref, kr_ref, wk_ref, gk_ref, wv_ref, e2_ref, k_ref, v_ref):
    _mla_keys_values(ckv_ref[...], kr_ref[...], wk_ref, gk_ref[...], wv_ref, e2_ref, None, k_ref, v_ref)


def _cache_kv(ckv, kr_slab, w):
    n = ckv.shape[0]
    tm = TOKEN_TILE
    row = lambda i: (i, 0)
    hk = MLA_HEADS * LANES
    slab = jax.ShapeDtypeStruct((MLA_HEADS, n, LANES), BF16)
    return pl.pallas_call(
        _cache_kv_kernel,
        out_shape=(slab, slab),
        grid=(n // tm,),
        in_specs=[pl.BlockSpec((tm, KV_LORA), row), pl.BlockSpec((tm, LANES), row),
                  _const_spec((KV_LORA, hk)), _const_spec((1, LANES)), _const_spec((KV_LORA, hk)),
                  _const_spec((2 * LANES, 2 * LANES))],
        out_specs=(_slab_spec(MLA_HEADS, tm), _slab_spec(MLA_HEADS, tm)),
        compiler_params=_params(1), name="l0_cache_kv",
    )(ckv, kr_slab, w["wk"], w["gk_plain"], w["wv"], w["e2"])


def _pool_mixer(seq_len, tile, prev_ref, u_ref, next_ref, wp_ref, ps_ref, buf_ref):
    tm = u_ref.shape[0]
    piece = min(tm, seq_len)
    n_pieces = tm // piece
    stride = piece + 2 * POOL_HALO
    starts = []
    for j in range(n_pieces):
        pos0 = (tile * tm + j * piece) % seq_len
        prev = prev_ref[...] if j == 0 else u_ref[j * piece - POOL_HALO:j * piece, :]
        nxt = next_ref[...] if j == n_pieces - 1 else u_ref[(j + 1) * piece:(j + 1) * piece + POOL_HALO, :]
        base = j * stride
        buf_ref[base:base + POOL_HALO, :] = jnp.where(pos0 != 0, prev, 0.0)
        buf_ref[base + POOL_HALO:base + POOL_HALO + piece, :] = u_ref[j * piece:(j + 1) * piece, :]
        buf_ref[base + POOL_HALO + piece:base + stride, :] = jnp.where(pos0 + piece != seq_len, nxt, 0.0)
        starts.append(pos0)
    def group(g):
        win = POOL_WINDOWS[g]
        lanes = pl.ds(g * POOL_GROUP, POOL_GROUP)
        pooled = []
        for j in range(n_pieces):
            centre = j * stride + POOL_HALO
            lo = starts[j] + lax.broadcasted_iota(jnp.int32, (piece, 1), 0) - win // 2
            cnt = (jnp.minimum(lo + win, seq_len) - jnp.maximum(lo, 0)).astype(F32)
            acc = buf_ref[pl.ds(centre - win // 2, piece), lanes]
            for d in range(1 - win // 2, win - win // 2):
                acc = acc + buf_ref[pl.ds(centre + d, piece), lanes]
            pooled.append(acc / cnt - buf_ref[pl.ds(centre, piece), lanes])
        pooled = pooled[0] if n_pieces == 1 else jnp.concatenate(pooled, axis=0)
        return (_dot(pooled.astype(BF16), wp_ref[g]) * ps_ref[:, lanes]).astype(BF16)

    return [functools.partial(group, g) for g in range(len(POOL_WINDOWS))]


def _attend(streams, running_max):
    items = [(si, ch) for si, (_, chunks, _) in enumerate(streams) for ch in chunks]

    def score(item):
        si, (k_fn, _, extra_fn) = item
        s = _dot_nt(k_fn(), streams[si][0])
        return s if extra_fn is None else s + extra_fn()

    state = [None] * len(streams)
    pending = [score(item) for item in items[:SCORE_LOOKAHEAD]]
    for i, (si, (_, v_fn, _)) in enumerate(items):
        s = pending.pop(0)
        if i + SCORE_LOOKAHEAD < len(items):
            pending.append(score(items[i + SCORE_LOOKAHEAD]))
        want_den = streams[si][2]
        m_new, alpha = None, 1.0
        if state[si] is not None:
            m, acc, den = state[si]
        if running_max:
            m_new = jnp.max(s, axis=0, keepdims=True)
            if state[si] is not None:
                m_new = jnp.maximum(m, m_new)
                alpha = jnp.exp2(m - m_new)
            s = s - m_new
        p = jnp.exp2(s)
        pv = _dot_tn(v_fn(), p.astype(BF16))
        psum = jnp.sum(p, axis=0, keepdims=True) if want_den else None
        if state[si] is None:
            state[si] = (m_new, pv, psum)
        elif running_max:
            state[si] = (m_new, acc * alpha + pv, den * alpha + psum if want_den else None)
        else:
            state[si] = (None, acc + pv, den + psum if want_den else None)
    return [(acc, den) for _, acc, den in state]


def _ref_chunks(k_ref, v_ref, slab, chunk, extra_fn=None, lo=0, hi=None):
    hi = k_ref.shape[1] if hi is None else hi
    out = []
    for c in range(lo, hi, chunk):
        rows = pl.ds(c, min(chunk, hi - c))
        out.append((lambda rows=rows: k_ref[slab, rows, :], lambda rows=rows: v_ref[slab, rows, :], extra_fn))
    return out


def _mla_finish(o_t):
    return o_t[:MLA_V] / o_t[MLA_V:MLA_V + 1]


def _na_queries(q2):
    low = lax.broadcasted_iota(jnp.int32, (1, LANES), 1) < NA_HEAD_DIM
    zero = jnp.zeros_like(q2)
    return jnp.concatenate([jnp.where(low, q2, zero), jnp.where(low, zero, q2)], axis=0)


def _na_finish(o_t, den):
    nq = o_t.shape[1] // 2
    o = o_t / den
    return jnp.concatenate([o[:NA_HEAD_DIM, :nq], o[NA_HEAD_DIM:, nq:]], axis=0)


def _attend_guarded(bound, streams_fn):
    return lax.cond(bound <= MAX_SCORE_BOUND,
                    lambda: _attend(streams_fn(False), False),
                    lambda: _attend(streams_fn(True), True))


def _mla_ctx_kernel(seq, bound_ref, q_ref, k_ref, v_ref, o_ref):
    spans = [(lo, lo + seq) for lo in range(0, k_ref.shape[1], seq)]
    res = _attend_guarded(bound_ref[0, 0], lambda _: [
        (q_ref[h, lo:hi, :], _ref_chunks(k_ref, v_ref, h, seq, lo=lo, hi=hi), False)
        for lo, hi in spans for h in range(MLA_HEADS)])
    for i, (lo, hi) in enumerate(spans):
        for hp in range(MLA_HEADS // 2):
            o_t = jnp.concatenate([_mla_finish(res[i * MLA_HEADS + h][0]) for h in (2 * hp, 2 * hp + 1)], axis=0)
            o_ref[lo:hi, hp * LANES:(hp + 1) * LANES] = o_t.T.astype(BF16)


def _na_ctx_kernel(seq, bound_ref, q_ref, k_ref, v_ref, o_ref):
    spans = [(lo, lo + seq) for lo in range(0, k_ref.shape[1], seq)]
    res = _attend_guarded(bound_ref[0, 0], lambda _: [
        (_na_queries(q_ref[s, lo:hi, :]), _ref_chunks(k_ref, v_ref, s, seq, lo=lo, hi=hi), True)
        for lo, hi in spans for s in range(NA_SLABS)])
    for i, (lo, hi) in enumerate(spans):
        for s in range(NA_SLABS):
            o_ref[lo:hi, s * LANES:(s + 1) * LANES] = _na_finish(*res[i * NA_SLABS + s]).T.astype(BF16)


def _ctx_attn(body, q, k, v, bound, seq, width, name):
    slabs, n, _ = q.shape
    rows = CTX_STEP_SEQS * seq
    spec = pl.BlockSpec((slabs, rows, LANES), lambda b: (0, b, 0))
    return pl.pallas_call(
        functools.partial(body, seq),
        out_shape=jax.ShapeDtypeStruct((n, width), BF16),
        grid=(n // rows,),
        in_specs=[pl.BlockSpec(memory_space=pltpu.SMEM), spec, spec, spec],
        out_specs=pl.BlockSpec((rows, width), lambda b: (b, 0)),
        compiler_params=_params(1), name=name,
    )(bound.reshape(1, 1), q, k, v)


SIDE_PIECE_ELEMS = 512 * 1024


def _side_cast_plan(shapes):
    plan = []
    for wi, (n_rows, n_cols) in enumerate(shapes):
        rows = SIDE_PIECE_ELEMS // n_cols
        plan += [(wi, r0, rows) for r0 in range(0, n_rows, rows)]
    return plan


def _side_cast_scratch(shapes):
    widths = sorted({c for _, c in shapes})
    stages = [pltpu.VMEM((2, SIDE_PIECE_ELEMS // c, c), dt) for dt in (F32, BF16) for c in widths]
    return stages + [pltpu.SemaphoreType.DMA((2,)), pltpu.SemaphoreType.DMA((2,))]


def _side_cast(step, srcs, dsts, scratch):
    shapes = [s.shape for s in srcs]
    widths = sorted({c for _, c in shapes})
    stage_in = dict(zip(widths, scratch[:len(widths)]))
    stage_out = dict(zip(widths, scratch[len(widths):2 * len(widths)]))
    sem_in, sem_out = scratch[2 * len(widths):]
    plan = _side_cast_plan(shapes)

    def refs(k):
        wi, r0, rows = plan[k]
        cols = shapes[wi][1]
        return (srcs[wi].at[pl.ds(r0, rows), :], dsts[wi].at[pl.ds(r0, rows), :],
                stage_in[cols].at[k % 2], stage_out[cols].at[k % 2])

    read = lambda k: pltpu.make_async_copy(refs(k)[0], refs(k)[2], sem_in.at[k % 2])
    write = lambda k: pltpu.make_async_copy(refs(k)[3], refs(k)[1], sem_out.at[k % 2])
    n = len(plan)
    for k in range(n + 2):
        @pl.when(step == k)
        def _(k=k):
            if k == 0:
                read(0).start()
            if k >= 2:
                write(k - 2).wait()
            if k < n:
                read(k).wait()
                if k + 1 < n:
                    read(k + 1).start()
                refs(k)[3][...] = refs(k)[2][...].astype(BF16)
                write(k).start()


def _mla_lat_kernel(bound_ref, q_ref, kl_ref, vl_ref, kc_ref, vc_ref, *rest):
    o_ref = rest[3]
    linear_step = (pl.program_id(0) * pl.num_programs(1) + pl.program_id(1)) * pl.num_programs(2) + pl.program_id(2)
    _side_cast(linear_step, rest[0:3], rest[4:7], rest[7:])
    batch, head0 = pl.program_id(0), pl.program_id(1) * MLA_STEP_HEADS
    heads = range(MLA_STEP_HEADS)
    bounds = [bound_ref[batch, head0 + h] for h in heads]

    def streams(running_max):
        size = MLA_ONLINE_CHUNK if running_max else MLA_KEY_CHUNK
        return [(q_ref[h], _ref_chunks(kl_ref, vl_ref, h, size) + _ref_chunks(kc_ref, vc_ref, h, size), False)
                for h in heads]

    res = _attend_guarded(functools.reduce(jnp.maximum, bounds), streams)
    o_ref[...] = jnp.concatenate([_mla_finish(o) for o, _ in res], axis=0).T.astype(BF16)


def _normed_len(gain, dim):
    return (dim ** 0.5) * jnp.max(jnp.abs(gain.astype(F32)))


def _side_cast_specs(weights, n_steps):
    shapes = [wt.shape for wt in weights]
    assert n_steps >= len(_side_cast_plan(shapes)) + 2
    hbm = pl.BlockSpec(memory_space=pl.ANY)
    return ([hbm] * len(weights), [jax.ShapeDtypeStruct(s, BF16) for s in shapes], [hbm] * len(weights),
            _side_cast_scratch(shapes))


def _mla_lat_attn(q, k, v, kc, vc, bound, seq, past, weights):
    n = q.shape[1]
    tq = MLA_Q_TILE
    nq = seq // tq
    hs = MLA_STEP_HEADS
    grid = (n // seq, MLA_HEADS // hs, nq)
    w_in, w_shapes, w_out, w_scratch = _side_cast_specs(weights, grid[0] * grid[1] * grid[2])
    outs = pl.pallas_call(
        _mla_lat_kernel,
        out_shape=(jax.ShapeDtypeStruct((n, MLA_HEADS * MLA_V), BF16), *w_shapes),
        grid=grid,
        in_specs=[pl.BlockSpec(memory_space=pltpu.SMEM),
                  pl.BlockSpec((hs, tq, LANES), lambda b, h, i: (h, b * nq + i, 0)),
                  pl.BlockSpec((hs, seq, LANES), lambda b, h, i: (h, b, 0)),
                  pl.BlockSpec((hs, seq, LANES), lambda b, h, i: (h, b, 0)),
                  pl.BlockSpec((hs, past, LANES), lambda b, h, i: (h, b, 0)),
                  pl.BlockSpec((hs, past, LANES), lambda b, h, i: (h, b, 0))] + w_in,
        out_specs=(pl.BlockSpec((tq, hs * MLA_V), lambda b, h, i: (b * nq + i, h)), *w_out),
        scratch_shapes=w_scratch,
        compiler_params=_params(3), name="l0_mla_latent_attn",
    )(bound, q, k, v, kc, vc, *weights)
    return outs[0], outs[1:]


def _na_first_row(r, rows):
    return jnp.clip(r - NA_KH // 2, 0, rows - NA_KH)


def _na_lat_kernel(rows, bound_ref, q_ref, k_ref, v_ref, kc_ref, vc_ref, t2_ref, *rest):
    o_ref = rest[3]
    _side_cast(pl.program_id(0) * pl.num_programs(1) + pl.program_id(1), rest[0:3], rest[4:7], rest[7:])
    nq = NA_TILE_ROWS * GRID_W
    nk = NA_WIN_ROWS * GRID_W
    n_off = 2 * NA_KH - 1
    r0 = pl.program_id(1) * NA_TILE_ROWS
    w0 = jnp.clip(r0 - NA_KH // 2, 0, rows - NA_WIN_ROWS)

    def block_index(j, t):
        key_row = w0 + j
        i = r0 + 2 * t
        first_u, first_l = _na_first_row(i, rows), _na_first_row(i + 1, rows)
        in_u = (key_row >= first_u) & (key_row < first_u + NA_KH)
        in_l = (key_row >= first_l) & (key_row < first_l + NA_KH)
        d = jnp.clip(key_row - i + NA_KH - 1, 0, n_off - 1)
        return jnp.where(in_u & in_l, d, jnp.where(in_u, n_off, jnp.where(in_l, n_off + 1, n_off + 2)))

    idx = [[block_index(j, t) for t in range(NA_TILE_ROWS // 2)] for j in range(NA_WIN_ROWS)]

    def local_bias(s):
        heads = [jnp.concatenate([jnp.concatenate([t2_ref[2 * s + hh, i] for i in row], axis=1) for row in idx], axis=0)
                 for hh in range(2)]
        return jnp.concatenate(heads, axis=1)

    batch = pl.program_id(0)
    bounds = [bound_ref[batch, h] for h in range(NA_HEADS)]

    def cache_chunk(s):
        lanes = pl.ds(s * LANES, LANES)
        return [(lambda: kc_ref[:, lanes].astype(BF16), lambda: vc_ref[:, lanes].astype(BF16), None)]

    res = _attend_guarded(
        functools.reduce(jnp.maximum, bounds),
        lambda _: [(_na_queries(q_ref[s]),
                    _ref_chunks(k_ref, v_ref, s, nk, functools.partial(local_bias, s)) + cache_chunk(s),
                    True) for s in range(NA_SLABS)])
    for s in range(NA_SLABS):
        o_ref[:, s * LANES:(s + 1) * LANES] = _na_finish(*res[s]).T.astype(BF16)


def _na_lat_attn(q, k, v, kc, vc, t2, bound, seq, past, weights):
    slabs, n, _ = q.shape
    rows = seq // GRID_W
    tiles = rows // NA_TILE_ROWS
    nq = NA_TILE_ROWS * GRID_W
    nk = NA_WIN_ROWS * GRID_W

    def win_map(b, t):
        w0 = jnp.clip(t * NA_TILE_ROWS - NA_KH // 2, 0, rows - NA_WIN_ROWS)
        return (0, pl.multiple_of(b * seq + w0 * GRID_W, GRID_W), 0)

    win_spec = pl.BlockSpec((pl.Element(slabs), pl.Element(nk), pl.Element(LANES)), win_map)
    ctx_spec = pl.BlockSpec((past, slabs * LANES), lambda b, t: (b, 0))
    w_in, w_shapes, w_out, w_scratch = _side_cast_specs(weights, (n // seq) * tiles)
    outs = pl.pallas_call(
        functools.partial(_na_lat_kernel, rows),
        out_shape=(jax.ShapeDtypeStruct((n, slabs * LANES), BF16), *w_shapes),
        grid=(n // seq, tiles),
        in_specs=[pl.BlockSpec(memory_space=pltpu.SMEM),
                  pl.BlockSpec((slabs, nq, LANES), lambda b, t: (0, b * tiles + t, 0)),
                  win_spec, win_spec, ctx_spec, ctx_spec,
                  _const_spec(t2.shape)] + w_in,
        out_specs=(pl.BlockSpec((nq, slabs * LANES), lambda b, t: (b * tiles + t, 0)), *w_out),
        scratch_shapes=w_scratch,
        compiler_params=_params(2), name="l1_na_latent_attn",
    )(bound, q, k, v, kc, vc, t2, *weights)
    return outs[0], outs[1:]


def _pair_norm(x2, g2):
    low = lax.broadcasted_iota(jnp.int32, (1, LANES), 1) < NA_HEAD_DIM
    sq = x2 * x2
    ss_all = jnp.sum(sq, axis=-1, keepdims=True)
    ss_low = jnp.sum(jnp.where(low, sq, 0.0), axis=-1, keepdims=True)
    ss = jnp.where(low, ss_low, ss_all - ss_low)
    return x2 * lax.rsqrt(ss * (1.0 / NA_HEAD_DIM) + EPS) * g2


def _l1_pre_kernel(keep_f32, x_ref, mod_ref, nw_ref, win_ref, gq_ref, gk_ref, q_ref, k_ref, v_ref, *f32_refs):
    h = _modulate(x_ref[...], nw_ref[...], _ada_row(mod_ref, 0), _ada_row(mod_ref, 1)).astype(BF16)
    width = NA_SLABS * LANES
    q = _dot(h, win_ref[:, 0:width])
    k = _dot(h, win_ref[:, width:2 * width])
    v = _dot(h, win_ref[:, 2 * width:3 * width])
    gq = gq_ref[...]
    gk = gk_ref[...]
    for s in range(NA_SLABS):
        ls = slice(s * LANES, (s + 1) * LANES)
        q_ref[s] = _pair_norm(q[:, ls], gq).astype(BF16)
        kn = _pair_norm(k[:, ls], gk)
        k_ref[s] = kn.astype(BF16)
        v_ref[s] = v[:, ls].astype(BF16)
        if keep_f32:
            f32_refs[0][:, ls] = kn
    if keep_f32:
        f32_refs[1][...] = v


def _l1_pre(x, mod, tiles_per_group, keep_f32, w):
    n = x.shape[0]
    tm = TOKEN_TILE
    row = lambda i: (i, 0)
    width = NA_SLABS * LANES
    slab = jax.ShapeDtypeStruct((NA_SLABS, n, LANES), BF16)
    out_shape = [slab, slab, slab]
    out_specs = [_slab_spec(NA_SLABS, tm)] * 3
    if keep_f32:
        out_shape += [jax.ShapeDtypeStruct((n, width), F32)] * 2
        out_specs += [pl.BlockSpec((tm, width), row)] * 2
    return pl.pallas_call(
        functools.partial(_l1_pre_kernel, keep_f32),
        out_shape=tuple(out_shape), grid=(n // tm,),
        in_specs=[pl.BlockSpec((tm, D_MODEL), row),
                  _ada_spec(mod, tiles_per_group),
                  _const_spec((1, D_MODEL)),
                  _const_spec((D_MODEL, 3 * width)),
                  _const_spec((1, LANES)),
                  _const_spec((1, LANES))],
        out_specs=tuple(out_specs),
        compiler_params=_params(1), name="l1_pre_ctx" if keep_f32 else "l1_pre_latent",
    )(x, mod[0], w["norm_mix"], w["w_in"], w["gq"], w["gk"])


class _WeightCaster:
    def __init__(self, pieces, stage_ref, sems):
        slots = stage_ref.shape[0]
        self.pieces, self.stage, self.slots, self.done = pieces, stage_ref, slots, 0
        self.copies = [pltpu.make_async_copy(src, stage_ref.at[i % slots], sems.at[i % slots])
                       for i, (src, _) in enumerate(pieces)]
        for copy in self.copies[:slots]:
            copy.start()

    def finish(self, count):
        for i in range(self.done, self.done + count):
            self.copies[i].wait()
            self.pieces[i][1][...] = self.stage[i % self.slots].astype(BF16)
            if i + self.slots < len(self.copies):
                self.copies[i + self.slots].start()
        self.done += count


def _weight_pieces(srcs, dsts, rows):
    wo_s, w1_s, w2_s = srcs
    wo_d, w1_d, w2_d = dsts
    pieces = [(wo_s.at[pl.ds(r, rows), :], wo_d.at[pl.ds(r, rows), :]) for r in range(0, D_MODEL, rows)]
    for c in range(0, D_FF, FF_CHUNK):
        cols = pl.ds(c, FF_CHUNK)
        pieces += [(w1_s.at[pl.ds(r, rows), cols], w1_d.at[pl.ds(r, rows), cols]) for r in range(0, D_MODEL, rows)]
        pieces += [(w2_s.at[pl.ds(c + r, rows), :], w2_d.at[pl.ds(c + r, rows), :]) for r in range(0, FF_CHUNK, rows)]
    return pieces


def _post_kernel(pool_seq, cast, x_ref, attn_ref, *rest):
    rest = list(rest)
    take = lambda k: [rest.pop(0) for _ in range(k)]
    pool_in = take(8) if pool_seq is not None else None
    mod_ref, nw_ref, wo_ref, w1_ref, w2_ref = take(5)
    (o_ref,) = take(1)
    w16_out = take(3) if cast else None
    pool_scratch = take(2) if pool_seq is not None else None
    step = pl.program_id(0)
    n_chunks = D_FF // FF_CHUNK
    if cast:
        sources = (wo_ref, w1_ref, w2_ref)
        wo_ref, w1_ref, w2_ref, stage_ref, sem_in, sem_out = take(6)
        to_hbm = [pltpu.make_async_copy(v, o, sem_out.at[i])
                  for i, (v, o) in enumerate(zip((wo_ref, w1_ref, w2_ref), w16_out))]
        rows = stage_ref.shape[1]
        proj_pieces, chunk_pieces = D_MODEL // rows, (D_MODEL + FF_CHUNK) // rows
    if pool_seq is not None:
        first_refs, next_refs, (wp_ref, ps_ref) = pool_in[0:3], pool_in[3:6], pool_in[6:8]
        buf_ref, pooled_ref = pool_scratch

        def first_pooling():
            for g, group in enumerate(_pool_mixer(pool_seq, 0, *first_refs, wp_ref, ps_ref, buf_ref)):
                pooled_ref[:, g * POOL_GROUP:(g + 1) * POOL_GROUP] = group()

    def body(caster):
        if caster is not None:
            caster.finish(proj_pieces)
        if pool_seq is None:
            mixed = _dot(attn_ref[...], wo_ref[...])
        else:
            mixed = _dot(jnp.concatenate([pooled_ref[...], attn_ref[...]], axis=-1), wo_ref[...])
            next_groups = _pool_mixer(pool_seq, jnp.minimum(step + 1, pl.num_programs(0) - 1), *next_refs,
                                      wp_ref, ps_ref, buf_ref)
        x1 = x_ref[...] + _ada_row(mod_ref, 2) * mixed
        h = _modulate(x1, nw_ref[...], _ada_row(mod_ref, 3), _ada_row(mod_ref, 4)).astype(BF16)
        acc = jnp.zeros(x1.shape, F32)
        for c in range(n_chunks):
            if caster is not None:
                caster.finish(chunk_pieces)
            cs = pl.ds(c * FF_CHUNK, FF_CHUNK)
            u = jnp.square(jnp.maximum(_dot(h, w1_ref[:, cs]), 0.0)).astype(BF16)
            acc = acc + _dot(u, w2_ref[cs, :])
            if pool_seq is not None:
                for g in range(c * len(next_groups) // n_chunks, (c + 1) * len(next_groups) // n_chunks):
                    pooled_ref[:, g * POOL_GROUP:(g + 1) * POOL_GROUP] = next_groups[g]()
        o_ref[...] = x1 + _ada_row(mod_ref, 5) * acc

    if not cast:
        if pool_seq is not None:
            pl.when(step == 0)(first_pooling)
        body(None)
        return

    @pl.when(step == 0)
    def _():
        caster = _WeightCaster(_weight_pieces(sources, (wo_ref, w1_ref, w2_ref), rows), stage_ref, sem_in)
        if pool_seq is not None:
            first_pooling()
        body(caster)
        for copy in to_hbm:
            copy.start()

    @pl.when(step > 0)
    def _():
        body(None)

    @pl.when(step == pl.num_programs(0) - 1)
    def _():
        for copy in to_hbm:
            copy.wait()


POST_STAGE_ROWS = 512
POST_STAGE_SLOTS = 4


def _post(x, attn, mod, tiles_per_group, w, name, pool=None, cast=False):
    n = x.shape[0]
    tm = TOKEN_TILE
    row = lambda i: (i, 0)
    in_specs = [pl.BlockSpec((tm, D_MODEL), row), pl.BlockSpec((tm, attn.shape[1]), row)]
    args = [x, attn]
    scratch = []
    pool_seq = None
    if pool is not None:
        a_in, pool_seq = pool
        hb = tm // POOL_HALO
        last = n // POOL_HALO - 1
        pieces = tm // min(tm, pool_seq)
        last_tile = n // tm - 1

        def tile_specs(tile_of):
            return [pl.BlockSpec((POOL_HALO, POOL_WIDTH), lambda i: (jnp.maximum(tile_of(i) * hb - 1, 0), 0)),
                    pl.BlockSpec((tm, POOL_WIDTH), lambda i: (tile_of(i), 0)),
                    pl.BlockSpec((POOL_HALO, POOL_WIDTH), lambda i: (jnp.minimum((tile_of(i) + 1) * hb, last), 0))]

        in_specs += tile_specs(lambda i: 0) + tile_specs(lambda i: jnp.minimum(i + 1, last_tile))
        in_specs += [_const_spec((len(POOL_WINDOWS), POOL_GROUP, POOL_GROUP)), _const_spec((1, POOL_WIDTH))]
        args += [a_in] * 6 + [w["w_pool"], w["pool_scale"]]
        scratch = [pltpu.VMEM((tm + 2 * POOL_HALO * pieces, POOL_WIDTH), F32), pltpu.VMEM((tm, POOL_WIDTH), BF16)]
    weights = [w["w_out"], w["w_mlp1"], w["w_mlp2"]]
    in_specs += [_ada_spec(mod, tiles_per_group), _const_spec((1, D_MODEL))]
    out_shape = [jax.ShapeDtypeStruct((n, D_MODEL), F32)]
    out_specs = [pl.BlockSpec((tm, D_MODEL), row)]
    if cast:
        in_specs += [pl.BlockSpec(memory_space=pl.ANY)] * 3
        out_shape += [jax.ShapeDtypeStruct(wt.shape, BF16) for wt in weights]
        out_specs += [pl.BlockSpec(memory_space=pl.ANY)] * 3
        scratch = scratch + [pltpu.VMEM(wt.shape, BF16) for wt in weights]
        scratch += [pltpu.VMEM((POST_STAGE_SLOTS, POST_STAGE_ROWS, D_MODEL), F32),
                    pltpu.SemaphoreType.DMA((POST_STAGE_SLOTS,)), pltpu.SemaphoreType.DMA((3,))]
    else:
        in_specs += [_const_spec(wt.shape) for wt in weights]
    args += [mod[0], w["norm_mlp"]] + weights
    outs = pl.pallas_call(
        functools.partial(_post_kernel, pool_seq, cast),
        out_shape=tuple(out_shape), grid=(n // tm,),
        in_specs=in_specs, out_specs=tuple(out_specs),
        scratch_shapes=scratch, compiler_params=_params(1), name=name,
    )(*args)
    return (outs[0], dict(w, w_out=outs[1], w_mlp1=outs[2], w_mlp2=outs[3])) if cast else outs[0]


def _rope_tables(seq):
    assert seq == GRID_W * GRID_W
    half = MLA_ROPE // 2
    inv_freq = jnp.power(ROPE_THETA, -jnp.arange(0, half, 2, dtype=F32) / half)
    ang = jnp.arange(GRID_W, dtype=F32)[:, None] * inv_freq[None, :]
    cos, sin = jnp.cos(ang), jnp.sin(ang)
    fill = lambda n, v=0.0: jnp.full((GRID_W, n), v, F32)
    spare = LANES - MLA_QK
    return jnp.stack([
        jnp.concatenate([fill(MLA_NOPE, 1.0), fill(half), cos, cos, fill(spare)], axis=1),
        jnp.concatenate([fill(MLA_NOPE), cos, cos, fill(half), fill(spare)], axis=1),
        jnp.concatenate([fill(MLA_NOPE), fill(half), -sin, sin, fill(spare)], axis=1),
        jnp.concatenate([fill(MLA_NOPE), -sin, sin, fill(half), fill(spare)], axis=1)])


def _rope_partner(tail):
    shape = tail.shape
    t = tail.reshape(shape[:-1] + (2, 2, MLA_ROPE // 4))
    return t[..., ::-1, :].reshape(shape)


def _na_bias_table(rel_bias, rows):
    qcol = jnp.arange(GRID_W)
    kcol = jnp.arange(GRID_W)
    col_start = jnp.clip(qcol - NA_KW // 2, 0, GRID_W - NA_KW)
    col_mask = (kcol[:, None] >= col_start[None, :]) & (kcol[:, None] < col_start[None, :] + NA_KW)
    offs = jnp.arange(2 * NA_KW - 1)
    onehot = (kcol[None, :, None] - qcol[None, None, :] + NA_KW - 1 == offs[:, None, None]).astype(F32)
    blocks = jnp.einsum('hdj,jkq->hdkq', rel_bias.astype(F32) * LOG2E, onehot, precision=lax.Precision.HIGHEST)
    blocks = jnp.where(col_mask[None, None], blocks, NEG_INF)
    neg = jnp.full_like(blocks[:, :1], NEG_INF)
    both = jnp.concatenate([blocks, jnp.concatenate([neg, blocks[:, :-1]], axis=1)], axis=-1)
    d_upper, d_lower = _na_partial_offsets(rows)
    upper_only = jnp.concatenate([blocks[:, d_upper:d_upper + 1], neg], axis=-1)
    lower_only = jnp.concatenate([neg, blocks[:, d_lower - 1:d_lower]], axis=-1)
    return jnp.concatenate([both, upper_only, lower_only, jnp.concatenate([neg, neg], axis=-1)], axis=1)


def _na_partial_offsets(rows):
    first = lambda r: min(max(r - NA_KH // 2, 0), rows - NA_KH)
    upper, lower = set(), set()
    for r0 in range(0, rows, NA_TILE_ROWS):
        w0 = min(max(r0 - NA_KH // 2, 0), rows - NA_WIN_ROWS)
        for j in range(NA_WIN_ROWS):
            for t in range(NA_TILE_ROWS // 2):
                key_row, i = w0 + j, r0 + 2 * t
                in_u = first(i) <= key_row < first(i) + NA_KH
                in_l = first(i + 1) <= key_row < first(i + 1) + NA_KH
                d = key_row - i + NA_KH - 1
                if in_u and in_l:
                    assert 1 <= d <= 2 * NA_KH - 2
                elif in_u:
                    upper.add(d)
                elif in_l:
                    lower.add(d)
    assert len(upper) == 1 and len(lower) == 1, (upper, lower)
    return upper.pop(), lower.pop()


def _prep_l0(w_in, q_lora_norm, kv_lora_norm, w_q_up, w_kv_up, mla_q_norm, mla_k_norm, w_pool, pool_scale,
             w_out, norm_mix, norm_mlp, w_mlp1, w_mlp2):
    lat_end = POOL_WIDTH + Q_LORA + KV_LORA
    w_in_pad = jnp.concatenate([w_in[:, :lat_end], jnp.zeros((D_MODEL, MLA_NOPE), w_in.dtype),
                                w_in[:, lat_end:], _rope_partner(w_in[:, lat_end:])], axis=1)
    pad = LANES - MLA_QK
    hk = MLA_HEADS * LANES
    wq = w_q_up.reshape(Q_LORA, MLA_HEADS, MLA_QK)
    wq = jnp.concatenate([wq, _rope_partner(wq[:, :, MLA_NOPE:])], axis=-1)
    kv = w_kv_up.reshape(KV_LORA, MLA_HEADS, MLA_NOPE + MLA_V)
    wk = jnp.pad(kv[:, :, :MLA_NOPE], ((0, 0), (0, 0), (0, LANES - MLA_NOPE)))
    wv = jnp.pad(kv[:, :, MLA_NOPE:], ((0, 0), (0, 0), (0, LANES - MLA_V)))
    gq = mla_q_norm * (MLA_QK ** -0.5 * LOG2E)
    slab_lane = jnp.arange(2 * LANES)
    e2 = (slab_lane[:, None] // LANES == slab_lane[None, :] // LANES) & (slab_lane[:, None] % LANES < MLA_QK)
    return dict(
        norm_mix=norm_mix.reshape(1, -1), norm_mlp=norm_mlp.reshape(1, -1),
        w_in=w_in_pad.astype(BF16),
        qln=q_lora_norm.reshape(1, -1), kvn=kv_lora_norm.reshape(1, -1),
        wq=wq.reshape(Q_LORA, hk).astype(BF16),
        gq_plain=jnp.pad(gq, (0, pad)).reshape(1, LANES),
        gq_rope=jnp.concatenate([gq, _rope_partner(gq[MLA_NOPE:])]).reshape(1, LANES),
        wk=wk.reshape(KV_LORA, hk).astype(BF16),
        gk_plain=jnp.pad(mla_k_norm, (0, pad)).reshape(1, LANES),
        gk_rope=jnp.concatenate([mla_k_norm, _rope_partner(mla_k_norm[MLA_NOPE:])]).reshape(1, LANES),
        wv=wv.reshape(KV_LORA, hk).astype(BF16),
        e2=e2.astype(BF16),
        w_pool=w_pool.astype(BF16), pool_scale=pool_scale.reshape(1, -1),
        w_out=w_out, w_mlp1=w_mlp1, w_mlp2=w_mlp2)


def kernel(x_prompt, x_sample, cache_l0_mla_ckv, cache_l0_mla_krope, cache_l1_na_k, cache_l1_na_v, c, c_ctx, w_ada_l0, b_ada_l0, norm_mix_l0, norm_mlp_l0, w_mlp1_l0, w_mlp2_l0, w_in_l0, q_lora_norm_l0, kv_lora_norm_l0, w_q_up_l0, w_kv_up_l0, mla_q_norm_l0, mla_k_norm_l0, w_pool_l0, pool_scale_l0, w_out_l0, w_ada_l1, b_ada_l1, norm_mix_l1, norm_mlp_l1, w_mlp1_l1, w_mlp2_l1, w_in_l1, na_q_norm_l1, na_k_norm_l1, rel_bias_l1, w_out_l1):
    batch, seq, _ = x_prompt.shape
    dec_batch, dec_seq, _ = x_sample.shape
    past = cache_l0_mla_ckv.shape[1]
    xp = x_prompt.reshape(batch * seq, D_MODEL)
    xs = x_sample.reshape(dec_batch * dec_seq, D_MODEL)
    tiles_p = (batch * seq) // TOKEN_TILE
    tiles_s = dec_seq // TOKEN_TILE

    cond8 = jnp.concatenate([c_ctx[None, :], c, jnp.zeros((SUBLANES - 1 - dec_batch, D_MODEL), F32)], axis=0)
    mods = _ada(cond8, (w_ada_l0, w_ada_l1), (b_ada_l0, b_ada_l1))
    mods = mods.reshape(DEPTH, SUBLANES, 1, N_ADA * D_MODEL)
    (mod0_p, mod0_s), (mod1_p, mod1_s) = (((mods[l], 0), (mods[l], 1)) for l in range(DEPTH))

    w0 = _prep_l0(w_in_l0, q_lora_norm_l0, kv_lora_norm_l0, w_q_up_l0, w_kv_up_l0, mla_q_norm_l0, mla_k_norm_l0,
                  w_pool_l0, pool_scale_l0, w_out_l0, norm_mix_l0, norm_mlp_l0, w_mlp1_l0, w_mlp2_l0)
    a_p, q_p, k_p, v_p, ckv_p, kr_p = _l0_pre(xp, mod0_p, tiles_p, False, w0, None)
    a_s, q_s, k_s, v_s, _, _ = _l0_pre(xs, mod0_s, tiles_s, True, w0, _rope_tables(dec_seq))
    kr_cache = jnp.pad(cache_l0_mla_krope.reshape(dec_batch * past, MLA_ROPE), ((0, 0), (MLA_NOPE, LANES - MLA_QK)))
    k_c, v_c = _cache_kv(cache_l0_mla_ckv.reshape(dec_batch * past, KV_LORA), kr_cache, w0)
    mla_bound = _normed_len(mla_q_norm_l0 * (MLA_QK ** -0.5 * LOG2E), MLA_QK) * _normed_len(mla_k_norm_l0, MLA_QK)
    attn_p = _ctx_attn(_mla_ctx_kernel, q_p, k_p, v_p, mla_bound, seq, MLA_HEADS * MLA_V, "l0_mla_ctx_attn")
    attn_s, w16 = _mla_lat_attn(q_s, k_s, v_s, k_c, v_c, jnp.full((dec_batch, MLA_HEADS), mla_bound, F32),
                                dec_seq, past, (w0["w_out"], w0["w_mlp1"], w0["w_mlp2"]))
    w0 = dict(w0, w_out=w16[0], w_mlp1=w16[1], w_mlp2=w16[2])
    xp = _post(xp, attn_p, mod0_p, tiles_p, w0, "l0_post_ctx", pool=(a_p, seq))
    xs = _post(xs, attn_s, mod0_s, tiles_s, w0, "l0_post_latent", pool=(a_s, dec_seq))

    w1 = dict(norm_mix=norm_mix_l1.reshape(1, -1), norm_mlp=norm_mlp_l1.reshape(1, -1),
              w_in=w_in_l1.astype(BF16),
              gq=jnp.tile(na_q_norm_l1 * (NA_HEAD_DIM ** -0.5 * LOG2E), 2).reshape(1, LANES),
              gk=jnp.tile(na_k_norm_l1, 2).reshape(1, LANES),
              w_out=w_out_l1, w_mlp1=w_mlp1_l1, w_mlp2=w_mlp2_l1)
    q1_p, k1_p, v1_p, k1_new, v1_new = _l1_pre(xp, mod1_p, tiles_p, True, w1)
    q1_s, k1_s, v1_s = _l1_pre(xs, mod1_s, tiles_s, False, w1)
    q_len, k_len = _normed_len(w1["gq"], NA_HEAD_DIM), _normed_len(w1["gk"], NA_HEAD_DIM)
    na_p = _ctx_attn(_na_ctx_kernel, q1_p, k1_p, v1_p, q_len * k_len, seq, NA_SLABS * LANES, "l1_na_ctx_attn")
    cache_len = jnp.sqrt(jnp.max(jnp.sum(jnp.square(cache_l1_na_k.astype(F32)), axis=-1), axis=1))
    na_bound = (q_len * jnp.maximum(k_len, cache_len)
                + jnp.max(jnp.abs(rel_bias_l1.astype(F32)) * LOG2E, axis=(1, 2))[None, :])
    na_s, w16 = _na_lat_attn(q1_s, k1_s, v1_s,
                             cache_l1_na_k.reshape(dec_batch * past, -1), cache_l1_na_v.reshape(dec_batch * past, -1),
                             _na_bias_table(rel_bias_l1, dec_seq // GRID_W), na_bound, dec_seq, past,
                             (w1["w_out"], w1["w_mlp1"], w1["w_mlp2"]))
    w1 = dict(w1, w_out=w16[0], w_mlp1=w16[1], w_mlp2=w16[2])
    xp = _post(xp, na_p, mod1_p, tiles_p, w1, "l1_post_ctx")
    xs = _post(xs, na_s, mod1_s, tiles_s, w1, "l1_post_latent")

    return (xp.reshape(batch, seq, D_MODEL), xs.reshape(dec_batch, dec_seq, D_MODEL),
            ckv_p.reshape(batch, seq, KV_LORA), kr_p.reshape(batch, seq, MLA_ROPE),
            k1_new.reshape(batch, seq, NA_HEADS, NA_HEAD_DIM), v1_new.reshape(batch, seq, NA_HEADS, NA_HEAD_DIM))
```

```python
import functools

import jax
import jax.numpy as jnp
from jax import lax
from jax.experimental import pallas as pl
from jax.experimental.pallas import tpu as pltpu

F32 = jnp.float32
BF16 = jnp.bfloat16

D_MODEL = 1024
D_FF = 4 * D_MODEL
DEPTH = 2
N_ADA = 6
EPS = 1e-6
NEG_INF = -1e30
ROPE_THETA = 10000.0
GRID_W = 64
POOL_WIDTH = 512
POOL_WINDOWS = (2, 4, 8, 16)
POOL_GROUP = 128
POOL_HALO = 8
MLA_HEADS = 8
MLA_NOPE = 64
MLA_ROPE = 32
MLA_QK = MLA_NOPE + MLA_ROPE
MLA_V = 64
Q_LORA = 256
KV_LORA = 128
NA_HEADS = 16
NA_HEAD_DIM = 64
NA_KH = 8
NA_KW = 16
LOG2E = 1.4426950408889634

LANES = 128
SUBLANES = 8
VMEM_LIMIT = 56 * 1024 * 1024

TOKEN_TILE = 512
MLA_Q_TILE = 512
MLA_STEP_HEADS = 2
MLA_KEY_CHUNK = 2048
MLA_ONLINE_CHUNK = 1024
SCORE_LOOKAHEAD = 4
MAX_SCORE_BOUND = 40.0
FF_CHUNK = 1024
CTX_STEP_SEQS = 2
NA_SLABS = NA_HEADS * NA_HEAD_DIM // LANES
NA_TILE_ROWS = 4
NA_WIN_ROWS = NA_TILE_ROWS + NA_KH - 1


def _params(n_axes):
    return pltpu.CompilerParams(dimension_semantics=("arbitrary",) * n_axes,
                                vmem_limit_bytes=VMEM_LIMIT)


def _const_spec(shape):
    zeros = (0,) * len(shape)
    return pl.BlockSpec(shape, lambda *_: zeros, pipeline_mode=pl.Buffered(1))


def _dot(a, b):
    return jnp.dot(a, b, preferred_element_type=F32)


def _dot_nt(a, b):
    return lax.dot_general(a, b, (((1,), (1,)), ((), ())), preferred_element_type=F32)


def _dot_tn(a, b):
    return lax.dot_general(a, b, (((0,), (0,)), ((), ())), preferred_element_type=F32)


def _rms(x, w, n=None):
    n = x.shape[-1] if n is None else n
    ss = jnp.sum(x * x, axis=-1, keepdims=True)
    return x * lax.rsqrt(ss * (1.0 / n) + EPS) * w


def _modulate(x, nw, shift, scale):
    return _rms(x, nw) * (1.0 + scale) + shift


def _ada_spec(mod, tiles_per_group):
    mods, first_row = mod
    return pl.BlockSpec((None, 1, mods.shape[-1]), lambda i: (first_row + i // tiles_per_group, 0, 0))


def _ada_row(mod_ref, k):
    return mod_ref[:, k * D_MODEL:(k + 1) * D_MODEL]


def _ada_kernel(cond_ref, w0_ref, w1_ref, b_ref, o_ref):
    c = cond_ref[...]
    s = (c * jax.nn.sigmoid(c)).astype(BF16)
    for layer, w_ref in enumerate((w0_ref, w1_ref)):
        @pl.when(pl.program_id(0) == layer)
        def _(w_ref=w_ref):
            o_ref[...] = _dot(s, w_ref[...].astype(BF16)) + b_ref[...]


def _ada(cond8, w_ada, b_ada):
    n_out = w_ada[0].shape[1]
    tn = D_MODEL
    nb = n_out // tn
    return pl.pallas_call(
        _ada_kernel,
        out_shape=jax.ShapeDtypeStruct((DEPTH, SUBLANES, n_out), F32),
        grid=(DEPTH, nb),
        in_specs=[pl.BlockSpec((SUBLANES, D_MODEL), lambda l, j: (0, 0)),
                  pl.BlockSpec((D_MODEL, tn), lambda l, j: (0, jnp.where(l == 0, j, nb - 1))),
                  pl.BlockSpec((D_MODEL, tn), lambda l, j: (0, jnp.where(l == 1, j, 0))),
                  pl.BlockSpec((None, 1, tn), lambda l, j: (l, 0, j))],
        out_specs=pl.BlockSpec((None, SUBLANES, tn), lambda l, j: (l, 0, j)),
        compiler_params=_params(2),
        name="ada",
    )(cond8, w_ada[0], w_ada[1], jnp.stack(b_ada).reshape(DEPTH, 1, n_out))


def _rope(x, c, s):
    return x * c + pltpu.roll(x, LANES - MLA_ROPE, 1) * s


PAIR = 2 * LANES


def _head_rinv(raw, e2_ref):
    ss = _dot((raw * raw).astype(BF16), e2_ref[...])
    return lax.rsqrt(ss * (1.0 / MLA_QK) + EPS)


def _mla_keys_values(ckv, kr_slab, wk_ref, gk, wv_ref, e2_ref, rope_tabs, k_ref, v_ref):
    c16 = ckv.astype(BF16)
    tail = kr_slab * gk
    if rope_tabs is not None:
        tail = _rope(tail, *rope_tabs)
    ones_lane = (lax.broadcasted_iota(jnp.int32, (1, LANES), 1) == MLA_V).astype(F32)
    for p in range(MLA_HEADS // 2):
        cols = pl.ds(p * PAIR, PAIR)
        kn = _dot(c16, wk_ref[:, cols])
        v = _dot(c16, wv_ref[:, cols])
        rinv = _head_rinv(kn + jnp.tile(kr_slab, (1, 2)), e2_ref)
        for i in range(2):
            hs = slice(i * LANES, (i + 1) * LANES)
            k_ref[2 * p + i] = (rinv[:, hs] * (kn[:, hs] * gk + tail)).astype(BF16)
            v_ref[2 * p + i] = (v[:, hs] + ones_lane).astype(BF16)


def _l0_pre_kernel(latent, x_ref, mod_ref, nw_ref, win_ref, qln_ref, kvn_ref, wq_ref, gq_ref,
                   wk_ref, gk_ref, wv_ref, e2_ref, *rest):
    if latent:
        rope_ref, a_ref, q_ref, k_ref, v_ref, ckv_ref, kr_ref = rest
        tile_rows = x_ref.shape[0] // GRID_W
        row0 = (pl.program_id(0) * tile_rows) % rope_ref.shape[1]
        expand = lambda col, row: jnp.concatenate(
            [rope_ref[col] + rope_ref[row, pl.ds(row0 + r, 1), :] for r in range(tile_rows)], axis=0)
        rope_tabs = (expand(0, 1), expand(2, 3))
    else:
        a_ref, q_ref, k_ref, v_ref, ckv_ref, kr_ref = rest
        rope_tabs = None
    h = _modulate(x_ref[...], nw_ref[...], _ada_row(mod_ref, 0), _ada_row(mod_ref, 1)).astype(BF16)
    proj = _dot(h, win_ref[...])
    a_ref[...] = proj[:, :POOL_WIDTH]
    q_lat = proj[:, POOL_WIDTH:POOL_WIDTH + Q_LORA]
    kv_lat = proj[:, POOL_WIDTH + Q_LORA:POOL_WIDTH + Q_LORA + KV_LORA]
    kr_slab = proj[:, POOL_WIDTH + Q_LORA + KV_LORA:]
    ql = _rms(q_lat, qln_ref[...]).astype(BF16)
    gq = gq_ref[...]
    for p in range(MLA_HEADS // 2):
        q = _dot(ql, wq_ref[:, pl.ds(p * PAIR, PAIR)])
        q_rinv = _head_rinv(q, e2_ref)
        for i in range(2):
            hs = slice(i * LANES, (i + 1) * LANES)
            y = q[:, hs] * q_rinv[:, hs] * gq
            if rope_tabs is not None:
                y = _rope(y, *rope_tabs)
            q_ref[2 * p + i] = y.astype(BF16)
    ckv = _rms(kv_lat, kvn_ref[...])
    ckv_ref[...] = ckv
    kr_ref[...] = kr_slab[:, MLA_NOPE:MLA_QK]
    _mla_keys_values(ckv, kr_slab, wk_ref, gk_ref[...], wv_ref, e2_ref, rope_tabs, k_ref, v_ref)


def _slab_spec(slabs, tm):
    return pl.BlockSpec((slabs, tm, LANES), lambda i: (0, i, 0))


def _l0_pre(x, mod, tiles_per_group, latent, w, rope_tabs):
    n = x.shape[0]
    tm = TOKEN_TILE
    row = lambda i: (i, 0)
    hk = MLA_HEADS * LANES
    in_specs = [pl.BlockSpec((tm, D_MODEL), row),
                _ada_spec(mod, tiles_per_group),
                _const_spec((1, D_MODEL)),
                _const_spec((D_MODEL, D_MODEL)),
                _const_spec((1, Q_LORA)),
                _const_spec((1, KV_LORA)),
                _const_spec((Q_LORA, hk)),
                _const_spec((1, LANES)),
                _const_spec((KV_LORA, hk)),
                _const_spec((1, LANES)),
                _const_spec((KV_LORA, hk)),
                _const_spec((2 * LANES, 2 * LANES))]
    gains = "rope" if latent else "plain"
    args = [x, mod[0], w["norm_mix"], w["w_in"], w["qln"], w["kvn"], w["wq"], w["gq_" + gains], w["wk"],
            w["gk_" + gains], w["wv"], w["e2"]]
    if latent:
        assert tm % GRID_W == 0
        in_specs.append(_const_spec(rope_tabs.shape))
        args.append(rope_tabs)
    slab = jax.ShapeDtypeStruct((MLA_HEADS, n, LANES), BF16)
    out_shape = (jax.ShapeDtypeStruct((n, POOL_WIDTH), F32), slab, slab, slab,
                 jax.ShapeDtypeStruct((n, KV_LORA), F32),
                 jax.ShapeDtypeStruct((n, MLA_ROPE), F32))
    out_specs = (pl.BlockSpec((tm, POOL_WIDTH), row),
                 _slab_spec(MLA_HEADS, tm), _slab_spec(MLA_HEADS, tm), _slab_spec(MLA_HEADS, tm),
                 pl.BlockSpec((tm, KV_LORA), row), pl.BlockSpec((tm, MLA_ROPE), row))
    return pl.pallas_call(
        functools.partial(_l0_pre_kernel, latent),
        out_shape=out_shape, grid=(n // tm,), in_specs=in_specs, out_specs=out_specs,
        compiler_params=_params(1), name="l0_pre_latent" if latent else "l0_pre_ctx",
    )(*args)


def _cache_kv_kernel(ckv_ref, kr_ref, wk_ref, gk_ref, wv_ref, e2_ref, k_ref, v_ref):
    _mla_keys_values(ckv_ref[...], kr_ref[...], wk_ref, gk_ref[...], wv_ref, e2_ref, None, k_ref, v_ref)


def _cache_kv(ckv, kr_slab, w):
    n = ckv.shape[0]
    tm = TOKEN_TILE
    row = lambda i: (i, 0)
    hk = MLA_HEADS * LANES
    slab = jax.ShapeDtypeStruct((MLA_HEADS, n, LANES), BF16)
    return pl.pallas_call(
        _cache_kv_kernel,
        out_shape=(slab, slab),
        grid=(n // tm,),
        in_specs=[pl.BlockSpec((tm, KV_LORA), row), pl.BlockSpec((tm, LANES), row),
                  _const_spec((KV_LORA, hk)), _const_spec((1, LANES)), _const_spec((KV_LORA, hk)),
                  _const_spec((2 * LANES, 2 * LANES))],
        out_specs=(_slab_spec(MLA_HEADS, tm), _slab_spec(MLA_HEADS, tm)),
        compiler_params=_params(1), name="l0_cache_kv",
    )(ckv, kr_slab, w["wk"], w["gk_plain"], w["wv"], w["e2"])


def _pool_mixer(seq_len, tile, prev_ref, u_ref, next_ref, wp_ref, ps_ref, buf_ref):
    tm = u_ref.shape[0]
    piece = min(tm, seq_len)
    n_pieces = tm // piece
    stride = piece + 2 * POOL_HALO
    starts = []
    for j in range(n_pieces):
        pos0 = (tile * tm + j * piece) % seq_len
        prev = prev_ref[...] if j == 0 else u_ref[j * piece - POOL_HALO:j * piece, :]
        nxt = next_ref[...] if j == n_pieces - 1 else u_ref[(j + 1) * piece:(j + 1) * piece + POOL_HALO, :]
        base = j * stride
        buf_ref[base:base + POOL_HALO, :] = jnp.where(pos0 != 0, prev, 0.0)
        buf_ref[base + POOL_HALO:base + POOL_HALO + piece, :] = u_ref[j * piece:(j + 1) * piece, :]
        buf_ref[base + POOL_HALO + piece:base + stride, :] = jnp.where(pos0 + piece != seq_len, nxt, 0.0)
        starts.append(pos0)
    def group(g):
        win = POOL_WINDOWS[g]
        lanes = pl.ds(g * POOL_GROUP, POOL_GROUP)
        pooled = []
        for j in range(n_pieces):
            centre = j * stride + POOL_HALO
            lo = starts[j] + lax.broadcasted_iota(jnp.int32, (piece, 1), 0) - win // 2
            cnt = (jnp.minimum(lo + win, seq_len) - jnp.maximum(lo, 0)).astype(F32)
            acc = buf_ref[pl.ds(centre - win // 2, piece), lanes]
            for d in range(1 - win // 2, win - win // 2):
                acc = acc + buf_ref[pl.ds(centre + d, piece), lanes]
            pooled.append(acc / cnt - buf_ref[pl.ds(centre, piece), lanes])
        pooled = pooled[0] if n_pieces == 1 else jnp.concatenate(pooled, axis=0)
        return (_dot(pooled.astype(BF16), wp_ref[g]) * ps_ref[:, lanes]).astype(BF16)

    return [functools.partial(group, g) for g in range(len(POOL_WINDOWS))]


def _attend(streams, running_max):
    items = [(si, ch) for si, (_, chunks, _) in enumerate(streams) for ch in chunks]

    def score(item):
        si, (k_fn, _, extra_fn) = item
        s = _dot_nt(k_fn(), streams[si][0])
        return s if extra_fn is None else s + extra_fn()

    state = [None] * len(streams)
    pending = [score(item) for item in items[:SCORE_LOOKAHEAD]]
    for i, (si, (_, v_fn, _)) in enumerate(items):
        s = pending.pop(0)
        if i + SCORE_LOOKAHEAD < len(items):
            pending.append(score(items[i + SCORE_LOOKAHEAD]))
        want_den = streams[si][2]
        m_new, alpha = None, 1.0
        if state[si] is not None:
            m, acc, den = state[si]
        if running_max:
            m_new = jnp.max(s, axis=0, keepdims=True)
            if state[si] is not None:
                m_new = jnp.maximum(m, m_new)
                alpha = jnp.exp2(m - m_new)
            s = s - m_new
        p = jnp.exp2(s)
        pv = _dot_tn(v_fn(), p.astype(BF16))
        psum = jnp.sum(p, axis=0, keepdims=True) if want_den else None
        if state[si] is None:
            state[si] = (m_new, pv, psum)
        elif running_max:
            state[si] = (m_new, acc * alpha + pv, den * alpha + psum if want_den else None)
        else:
            state[si] = (None, acc + pv, den + psum if want_den else None)
    return [(acc, den) for _, acc, den in state]


def _ref_chunks(k_ref, v_ref, slab, chunk, extra_fn=None, lo=0, hi=None):
    hi = k_ref.shape[1] if hi is None else hi
    out = []
    for c in range(lo, hi, chunk):
        rows = pl.ds(c, min(chunk, hi - c))
        out.append((lambda rows=rows: k_ref[slab, rows, :], lambda rows=rows: v_ref[slab, rows, :], extra_fn))
    return out


def _mla_finish(o_t):
    return o_t[:MLA_V] / o_t[MLA_V:MLA_V + 1]


def _na_queries(q2):
    low = lax.broadcasted_iota(jnp.int32, (1, LANES), 1) < NA_HEAD_DIM
    zero = jnp.zeros_like(q2)
    return jnp.concatenate([jnp.where(low, q2, zero), jnp.where(low, zero, q2)], axis=0)


def _na_finish(o_t, den):
    nq = o_t.shape[1] // 2
    o = o_t / den
    return jnp.concatenate([o[:NA_HEAD_DIM, :nq], o[NA_HEAD_DIM:, nq:]], axis=0)


def _attend_guarded(bound, streams_fn):
    return lax.cond(bound <= MAX_SCORE_BOUND,
                    lambda: _attend(streams_fn(False), False),
                    lambda: _attend(streams_fn(True), True))


def _mla_ctx_kernel(seq, bound_ref, q_ref, k_ref, v_ref, o_ref):
    spans = [(lo, lo + seq) for lo in range(0, k_ref.shape[1], seq)]
    res = _attend_guarded(bound_ref[0, 0], lambda _: [
        (q_ref[h, lo:hi, :], _ref_chunks(k_ref, v_ref, h, seq, lo=lo, hi=hi), False)
        for lo, hi in spans for h in range(MLA_HEADS)])
    for i, (lo, hi) in enumerate(spans):
        for hp in range(MLA_HEADS // 2):
            o_t = jnp.concatenate([_mla_finish(res[i * MLA_HEADS + h][0]) for h in (2 * hp, 2 * hp + 1)], axis=0)
            o_ref[lo:hi, hp * LANES:(hp + 1) * LANES] = o_t.T.astype(BF16)


def _na_ctx_kernel(seq, bound_ref, q_ref, k_ref, v_ref, o_ref):
    spans = [(lo, lo + seq) for lo in range(0, k_ref.shape[1], seq)]
    res = _attend_guarded(bound_ref[0, 0], lambda _: [
        (_na_queries(q_ref[s, lo:hi, :]), _ref_chunks(k_ref, v_ref, s, seq, lo=lo, hi=hi), True)
        for lo, hi in spans for s in range(NA_SLABS)])
    for i, (lo, hi) in enumerate(spans):
        for s in range(NA_SLABS):
            o_ref[lo:hi, s * LANES:(s + 1) * LANES] = _na_finish(*res[i * NA_SLABS + s]).T.astype(BF16)


def _ctx_attn(body, q, k, v, bound, seq, width, name):
    slabs, n, _ = q.shape
    rows = CTX_STEP_SEQS * seq
    spec = pl.BlockSpec((slabs, rows, LANES), lambda b: (0, b, 0))
    return pl.pallas_call(
        functools.partial(body, seq),
        out_shape=jax.ShapeDtypeStruct((n, width), BF16),
        grid=(n // rows,),
        in_specs=[pl.BlockSpec(memory_space=pltpu.SMEM), spec, spec, spec],
        out_specs=pl.BlockSpec((rows, width), lambda b: (b, 0)),
        compiler_params=_params(1), name=name,
    )(bound.reshape(1, 1), q, k, v)


SIDE_PIECE_ELEMS = 512 * 1024


def _side_cast_plan(shapes):
    plan = []
    for wi, (n_rows, n_cols) in enumerate(shapes):
        rows = SIDE_PIECE_ELEMS // n_cols
        plan += [(wi, r0, rows) for r0 in range(0, n_rows, rows)]
    return plan


def _side_cast_scratch(shapes):
    widths = sorted({c for _, c in shapes})
    stages = [pltpu.VMEM((2, SIDE_PIECE_ELEMS // c, c), dt) for dt in (F32, BF16) for c in widths]
    return stages + [pltpu.SemaphoreType.DMA((2,)), pltpu.SemaphoreType.DMA((2,))]


def _side_cast(step, srcs, dsts, scratch):
    shapes = [s.shape for s in srcs]
    widths = sorted({c for _, c in shapes})
    stage_in = dict(zip(widths, scratch[:len(widths)]))
    stage_out = dict(zip(widths, scratch[len(widths):2 * len(widths)]))
    sem_in, sem_out = scratch[2 * len(widths):]
    plan = _side_cast_plan(shapes)

    def refs(k):
        wi, r0, rows = plan[k]
        cols = shapes[wi][1]
        return (srcs[wi].at[pl.ds(r0, rows), :], dsts[wi].at[pl.ds(r0, rows), :],
                stage_in[cols].at[k % 2], stage_out[cols].at[k % 2])

    read = lambda k: pltpu.make_async_copy(refs(k)[0], refs(k)[2], sem_in.at[k % 2])
    write = lambda k: pltpu.make_async_copy(refs(k)[3], refs(k)[1], sem_out.at[k % 2])
    n = len(plan)

    def piece_step(k):
        if k == 0:
            read(0).start()
        if k >= 2:
            write(k - 2).wait()
        if k < n:
            read(k).wait()
            if k + 1 < n:
                read(k + 1).start()
            refs(k)[3][...] = refs(k)[2][...].astype(BF16)
            write(k).start()

    @pl.when(step < n + 2)
    def _():
        for k in range(n + 2):
            pl.when(step == k)(functools.partial(piece_step, k))


def _mla_lat_kernel(bound_ref, q_ref, kl_ref, vl_ref, kc_ref, vc_ref, *rest):
    o_ref = rest[3]
    linear_step = (pl.program_id(0) * pl.num_programs(1) + pl.program_id(1)) * pl.num_programs(2) + pl.program_id(2)
    _side_cast(linear_step, rest[0:3], rest[4:7], rest[7:])
    batch, head0 = pl.program_id(0), pl.program_id(1) * MLA_STEP_HEADS
    heads = range(MLA_STEP_HEADS)
    bounds = [bound_ref[batch, head0 + h] for h in heads]

    def streams(running_max):
        size = MLA_ONLINE_CHUNK if running_max else MLA_KEY_CHUNK
        return [(q_ref[h], _ref_chunks(kl_ref, vl_ref, h, size) + _ref_chunks(kc_ref, vc_ref, h, size), False)
                for h in heads]

    res = _attend_guarded(functools.reduce(jnp.maximum, bounds), streams)
    o_ref[...] = jnp.concatenate([_mla_finish(o) for o, _ in res], axis=0).T.astype(BF16)


def _normed_len(gain, dim):
    return (dim ** 0.5) * jnp.max(jnp.abs(gain.astype(F32)))


def _side_cast_specs(weights, n_steps):
    shapes = [wt.shape for wt in weights]
    assert n_steps >= len(_side_cast_plan(shapes)) + 2
    hbm = pl.BlockSpec(memory_space=pl.ANY)
    return ([hbm] * len(weights), [jax.ShapeDtypeStruct(s, BF16) for s in shapes], [hbm] * len(weights),
            _side_cast_scratch(shapes))


def _mla_lat_attn(q, k, v, kc, vc, bound, seq, past, weights):
    n = q.shape[1]
    tq = MLA_Q_TILE
    nq = seq // tq
    hs = MLA_STEP_HEADS
    grid = (n // seq, MLA_HEADS // hs, nq)
    w_in, w_shapes, w_out, w_scratch = _side_cast_specs(weights, grid[0] * grid[1] * grid[2])
    outs = pl.pallas_call(
        _mla_lat_kernel,
        out_shape=(jax.ShapeDtypeStruct((n, MLA_HEADS * MLA_V), BF16), *w_shapes),
        grid=grid,
        in_specs=[pl.BlockSpec(memory_space=pltpu.SMEM),
                  pl.BlockSpec((hs, tq, LANES), lambda b, h, i: (h, b * nq + i, 0)),
                  pl.BlockSpec((hs, seq, LANES), lambda b, h, i: (h, b, 0)),
                  pl.BlockSpec((hs, seq, LANES), lambda b, h, i: (h, b, 0)),
                  pl.BlockSpec((hs, past, LANES), lambda b, h, i: (h, b, 0)),
                  pl.BlockSpec((hs, past, LANES), lambda b, h, i: (h, b, 0))] + w_in,
        out_specs=(pl.BlockSpec((tq, hs * MLA_V), lambda b, h, i: (b * nq + i, h)), *w_out),
        scratch_shapes=w_scratch,
        compiler_params=_params(3), name="l0_mla_latent_attn",
    )(bound, q, k, v, kc, vc, *weights)
    return outs[0], outs[1:]


def _na_first_row(r, rows):
    return jnp.clip(r - NA_KH // 2, 0, rows - NA_KH)


def _na_lat_kernel(rows, bound_ref, q_ref, k_ref, v_ref, kc_ref, vc_ref, t2_ref, *rest):
    o_ref = rest[3]
    _side_cast(pl.program_id(0) * pl.num_programs(1) + pl.program_id(1), rest[0:3], rest[4:7], rest[7:])
    nq = NA_TILE_ROWS * GRID_W
    nk = NA_WIN_ROWS * GRID_W
    n_off = 2 * NA_KH - 1
    r0 = pl.program_id(1) * NA_TILE_ROWS
    w0 = jnp.clip(r0 - NA_KH // 2, 0, rows - NA_WIN_ROWS)

    def block_index(j, t):
        key_row = w0 + j
        i = r0 + 2 * t
        first_u, first_l = _na_first_row(i, rows), _na_first_row(i + 1, rows)
        in_u = (key_row >= first_u) & (key_row < first_u + NA_KH)
        in_l = (key_row >= first_l) & (key_row < first_l + NA_KH)
        d = jnp.clip(key_row - i + NA_KH - 1, 0, n_off - 1)
        return jnp.where(in_u & in_l, d, jnp.where(in_u, n_off, jnp.where(in_l, n_off + 1, n_off + 2)))

    idx = [[block_index(j, t) for t in range(NA_TILE_ROWS // 2)] for j in range(NA_WIN_ROWS)]

    def local_bias(s):
        heads = [jnp.concatenate([jnp.concatenate([t2_ref[2 * s + hh, i] for i in row], axis=1) for row in idx], axis=0)
                 for hh in range(2)]
        return jnp.concatenate(heads, axis=1)

    batch = pl.program_id(0)
    bounds = [bound_ref[batch, h] for h in range(NA_HEADS)]

    def cache_chunk(s):
        lanes = pl.ds(s * LANES, LANES)
        return [(lambda: kc_ref[:, lanes].astype(BF16), lambda: vc_ref[:, lanes].astype(BF16), None)]

    res = _attend_guarded(
        functools.reduce(jnp.maximum, bounds),
        lambda _: [(_na_queries(q_ref[s]),
                    _ref_chunks(k_ref, v_ref, s, nk, functools.partial(local_bias, s)) + cache_chunk(s),
                    True) for s in range(NA_SLABS)])
    for s in range(NA_SLABS):
        o_ref[:, s * LANES:(s + 1) * LANES] = _na_finish(*res[s]).T.astype(BF16)


def _na_lat_attn(q, k, v, kc, vc, t2, bound, seq, past, weights):
    slabs, n, _ = q.shape
    rows = seq // GRID_W
    tiles = rows // NA_TILE_ROWS
    nq = NA_TILE_ROWS * GRID_W
    nk = NA_WIN_ROWS * GRID_W

    def win_map(b, t):
        w0 = jnp.clip(t * NA_TILE_ROWS - NA_KH // 2, 0, rows - NA_WIN_ROWS)
        return (0, pl.multiple_of(b * seq + w0 * GRID_W, GRID_W), 0)

    win_spec = pl.BlockSpec((pl.Element(slabs), pl.Element(nk), pl.Element(LANES)), win_map)
    ctx_spec = pl.BlockSpec((past, slabs * LANES), lambda b, t: (b, 0))
    w_in, w_shapes, w_out, w_scratch = _side_cast_specs(weights, (n // seq) * tiles)
    outs = pl.pallas_call(
        functools.partial(_na_lat_kernel, rows),
        out_shape=(jax.ShapeDtypeStruct((n, slabs * LANES), BF16), *w_shapes),
        grid=(n // seq, tiles),
        in_specs=[pl.BlockSpec(memory_space=pltpu.SMEM),
                  pl.BlockSpec((slabs, nq, LANES), lambda b, t: (0, b * tiles + t, 0)),
                  win_spec, win_spec, ctx_spec, ctx_spec,
                  _const_spec(t2.shape)] + w_in,
        out_specs=(pl.BlockSpec((nq, slabs * LANES), lambda b, t: (b * tiles + t, 0)), *w_out),
        scratch_shapes=w_scratch,
        compiler_params=_params(2), name="l1_na_latent_attn",
    )(bound, q, k, v, kc, vc, t2, *weights)
    return outs[0], outs[1:]


def _pair_norm(x2, g2):
    low = lax.broadcasted_iota(jnp.int32, (1, LANES), 1) < NA_HEAD_DIM
    sq = x2 * x2
    ss_all = jnp.sum(sq, axis=-1, keepdims=True)
    ss_low = jnp.sum(jnp.where(low, sq, 0.0), axis=-1, keepdims=True)
    ss = jnp.where(low, ss_low, ss_all - ss_low)
    return x2 * lax.rsqrt(ss * (1.0 / NA_HEAD_DIM) + EPS) * g2


def _l1_pre_kernel(keep_f32, x_ref, mod_ref, nw_ref, win_ref, gq_ref, gk_ref, q_ref, k_ref, v_ref, *f32_refs):
    h = _modulate(x_ref[...], nw_ref[...], _ada_row(mod_ref, 0), _ada_row(mod_ref, 1)).astype(BF16)
    width = NA_SLABS * LANES
    q = _dot(h, win_ref[:, 0:width])
    k = _dot(h, win_ref[:, width:2 * width])
    v = _dot(h, win_ref[:, 2 * width:3 * width])
    gq = gq_ref[...]
    gk = gk_ref[...]
    for s in range(NA_SLABS):
        ls = slice(s * LANES, (s + 1) * LANES)
        q_ref[s] = _pair_norm(q[:, ls], gq).astype(BF16)
        kn = _pair_norm(k[:, ls], gk)
        k_ref[s] = kn.astype(BF16)
        v_ref[s] = v[:, ls].astype(BF16)
        if keep_f32:
            f32_refs[0][:, ls] = kn
    if keep_f32:
        f32_refs[1][...] = v


def _l1_pre(x, mod, tiles_per_group, keep_f32, w):
    n = x.shape[0]
    tm = TOKEN_TILE
    row = lambda i: (i, 0)
    width = NA_SLABS * LANES
    slab = jax.ShapeDtypeStruct((NA_SLABS, n, LANES), BF16)
    out_shape = [slab, slab, slab]
    out_specs = [_slab_spec(NA_SLABS, tm)] * 3
    if keep_f32:
        out_shape += [jax.ShapeDtypeStruct((n, width), F32)] * 2
        out_specs += [pl.BlockSpec((tm, width), row)] * 2
    return pl.pallas_call(
        functools.partial(_l1_pre_kernel, keep_f32),
        out_shape=tuple(out_shape), grid=(n // tm,),
        in_specs=[pl.BlockSpec((tm, D_MODEL), row),
                  _ada_spec(mod, tiles_per_group),
                  _const_spec((1, D_MODEL)),
                  _const_spec((D_MODEL, 3 * width)),
                  _const_spec((1, LANES)),
                  _const_spec((1, LANES))],
        out_specs=tuple(out_specs),
        compiler_params=_params(1), name="l1_pre_ctx" if keep_f32 else "l1_pre_latent",
    )(x, mod[0], w["norm_mix"], w["w_in"], w["gq"], w["gk"])


def _post_kernel(pool_seq, x_ref, attn_ref, *rest):
    rest = list(rest)
    take = lambda k: [rest.pop(0) for _ in range(k)]
    pool_in = take(8) if pool_seq is not None else None
    mod_ref, nw_ref, wo_ref, w1_ref, w2_ref, o_ref = take(6)
    pool_scratch = take(2) if pool_seq is not None else None
    step = pl.program_id(0)
    n_chunks = D_FF // FF_CHUNK
    if pool_seq is not None:
        first_refs, next_refs, (wp_ref, ps_ref) = pool_in[0:3], pool_in[3:6], pool_in[6:8]
        buf_ref, pooled_ref = pool_scratch

        @pl.when(step == 0)
        def _():
            for g, group in enumerate(_pool_mixer(pool_seq, 0, *first_refs, wp_ref, ps_ref, buf_ref)):
                pooled_ref[:, g * POOL_GROUP:(g + 1) * POOL_GROUP] = group()

        mixed = _dot(jnp.concatenate([pooled_ref[...], attn_ref[...]], axis=-1), wo_ref[...])
        next_groups = _pool_mixer(pool_seq, jnp.minimum(step + 1, pl.num_programs(0) - 1), *next_refs,
                                  wp_ref, ps_ref, buf_ref)
    else:
        mixed = _dot(attn_ref[...], wo_ref[...])
    x1 = x_ref[...] + _ada_row(mod_ref, 2) * mixed
    h = _modulate(x1, nw_ref[...], _ada_row(mod_ref, 3), _ada_row(mod_ref, 4)).astype(BF16)
    acc = jnp.zeros(x1.shape, F32)
    for c in range(n_chunks):
        cs = pl.ds(c * FF_CHUNK, FF_CHUNK)
        u = jnp.square(jnp.maximum(_dot(h, w1_ref[:, cs]), 0.0)).astype(BF16)
        acc = acc + _dot(u, w2_ref[cs, :])
        if pool_seq is not None:
            for g in range(c * len(next_groups) // n_chunks, (c + 1) * len(next_groups) // n_chunks):
                pooled_ref[:, g * POOL_GROUP:(g + 1) * POOL_GROUP] = next_groups[g]()
    o_ref[...] = x1 + _ada_row(mod_ref, 5) * acc


def _post(x, attn, mod, tiles_per_group, w, name, pool=None):
    n = x.shape[0]
    tm = TOKEN_TILE
    row = lambda i: (i, 0)
    in_specs = [pl.BlockSpec((tm, D_MODEL), row), pl.BlockSpec((tm, attn.shape[1]), row)]
    args = [x, attn]
    scratch = []
    pool_seq = None
    if pool is not None:
        a_in, pool_seq = pool
        hb = tm // POOL_HALO
        last = n // POOL_HALO - 1
        pieces = tm // min(tm, pool_seq)
        last_tile = n // tm - 1

        def tile_specs(tile_of):
            return [pl.BlockSpec((POOL_HALO, POOL_WIDTH), lambda i: (jnp.maximum(tile_of(i) * hb - 1, 0), 0)),
                    pl.BlockSpec((tm, POOL_WIDTH), lambda i: (tile_of(i), 0)),
                    pl.BlockSpec((POOL_HALO, POOL_WIDTH), lambda i: (jnp.minimum((tile_of(i) + 1) * hb, last), 0))]

        in_specs += tile_specs(lambda i: 0) + tile_specs(lambda i: jnp.minimum(i + 1, last_tile))
        in_specs += [_const_spec((len(POOL_WINDOWS), POOL_GROUP, POOL_GROUP)), _const_spec((1, POOL_WIDTH))]
        args += [a_in] * 6 + [w["w_pool"], w["pool_scale"]]
        scratch = [pltpu.VMEM((tm + 2 * POOL_HALO * pieces, POOL_WIDTH), F32), pltpu.VMEM((tm, POOL_WIDTH), BF16)]
    in_specs += [_ada_spec(mod, tiles_per_group),
                 _const_spec((1, D_MODEL)),
                 _const_spec((D_MODEL, D_MODEL)),
                 _const_spec((D_MODEL, D_FF)),
                 _const_spec((D_FF, D_MODEL))]
    args += [mod[0], w["norm_mlp"], w["w_out"], w["w_mlp1"], w["w_mlp2"]]
    return pl.pallas_call(
        functools.partial(_post_kernel, pool_seq),
        out_shape=jax.ShapeDtypeStruct((n, D_MODEL), F32), grid=(n // tm,),
        in_specs=in_specs, out_specs=pl.BlockSpec((tm, D_MODEL), row),
        scratch_shapes=scratch, compiler_params=_params(1), name=name,
    )(*args)


def _rope_tables(seq):
    assert seq == GRID_W * GRID_W
    half = MLA_ROPE // 2
    inv_freq = jnp.power(ROPE_THETA, -jnp.arange(0, half, 2, dtype=F32) / half)
    ang = jnp.arange(GRID_W, dtype=F32)[:, None] * inv_freq[None, :]
    cos, sin = jnp.cos(ang), jnp.sin(ang)
    fill = lambda n, v=0.0: jnp.full((GRID_W, n), v, F32)
    spare = LANES - MLA_QK
    return jnp.stack([
        jnp.concatenate([fill(MLA_NOPE, 1.0), fill(half), cos, cos, fill(spare)], axis=1),
        jnp.concatenate([fill(MLA_NOPE), cos, cos, fill(half), fill(spare)], axis=1),
        jnp.concatenate([fill(MLA_NOPE), fill(half), -sin, sin, fill(spare)], axis=1),
        jnp.concatenate([fill(MLA_NOPE), -sin, sin, fill(half), fill(spare)], axis=1)])


def _rope_partner(tail):
    shape = tail.shape
    t = tail.reshape(shape[:-1] + (2, 2, MLA_ROPE // 4))
    return t[..., ::-1, :].reshape(shape)


def _na_bias_table(rel_bias, rows):
    qcol = jnp.arange(GRID_W)
    kcol = jnp.arange(GRID_W)
    col_start = jnp.clip(qcol - NA_KW // 2, 0, GRID_W - NA_KW)
    col_mask = (kcol[:, None] >= col_start[None, :]) & (kcol[:, None] < col_start[None, :] + NA_KW)
    offs = jnp.arange(2 * NA_KW - 1)
    onehot = (kcol[None, :, None] - qcol[None, None, :] + NA_KW - 1 == offs[:, None, None]).astype(F32)
    blocks = jnp.einsum('hdj,jkq->hdkq', rel_bias.astype(F32) * LOG2E, onehot, precision=lax.Precision.HIGHEST)
    blocks = jnp.where(col_mask[None, None], blocks, NEG_INF)
    neg = jnp.full_like(blocks[:, :1], NEG_INF)
    both = jnp.concatenate([blocks, jnp.concatenate([neg, blocks[:, :-1]], axis=1)], axis=-1)
    d_upper, d_lower = _na_partial_offsets(rows)
    upper_only = jnp.concatenate([blocks[:, d_upper:d_upper + 1], neg], axis=-1)
    lower_only = jnp.concatenate([neg, blocks[:, d_lower - 1:d_lower]], axis=-1)
    return jnp.concatenate([both, upper_only, lower_only, jnp.concatenate([neg, neg], axis=-1)], axis=1)


def _na_partial_offsets(rows):
    first = lambda r: min(max(r - NA_KH // 2, 0), rows - NA_KH)
    upper, lower = set(), set()
    for r0 in range(0, rows, NA_TILE_ROWS):
        w0 = min(max(r0 - NA_KH // 2, 0), rows - NA_WIN_ROWS)
        for j in range(NA_WIN_ROWS):
            for t in range(NA_TILE_ROWS // 2):
                key_row, i = w0 + j, r0 + 2 * t
                in_u = first(i) <= key_row < first(i) + NA_KH
                in_l = first(i + 1) <= key_row < first(i + 1) + NA_KH
                d = key_row - i + NA_KH - 1
                if in_u and in_l:
                    assert 1 <= d <= 2 * NA_KH - 2
                elif in_u:
                    upper.add(d)
                elif in_l:
                    lower.add(d)
    assert len(upper) == 1 and len(lower) == 1, (upper, lower)
    return upper.pop(), lower.pop()


def _prep_l0(w_in, q_lora_norm, kv_lora_norm, w_q_up, w_kv_up, mla_q_norm, mla_k_norm, w_pool, pool_scale,
             w_out, norm_mix, norm_mlp, w_mlp1, w_mlp2):
    lat_end = POOL_WIDTH + Q_LORA + KV_LORA
    w_in_pad = jnp.concatenate([w_in[:, :lat_end], jnp.zeros((D_MODEL, MLA_NOPE), w_in.dtype),
                                w_in[:, lat_end:], _rope_partner(w_in[:, lat_end:])], axis=1)
    pad = LANES - MLA_QK
    hk = MLA_HEADS * LANES
    wq = w_q_up.reshape(Q_LORA, MLA_HEADS, MLA_QK)
    wq = jnp.concatenate([wq, _rope_partner(wq[:, :, MLA_NOPE:])], axis=-1)
    kv = w_kv_up.reshape(KV_LORA, MLA_HEADS, MLA_NOPE + MLA_V)
    wk = jnp.pad(kv[:, :, :MLA_NOPE], ((0, 0), (0, 0), (0, LANES - MLA_NOPE)))
    wv = jnp.pad(kv[:, :, MLA_NOPE:], ((0, 0), (0, 0), (0, LANES - MLA_V)))
    gq = mla_q_norm * (MLA_QK ** -0.5 * LOG2E)
    slab_lane = jnp.arange(2 * LANES)
    e2 = (slab_lane[:, None] // LANES == slab_lane[None, :] // LANES) & (slab_lane[:, None] % LANES < MLA_QK)
    return dict(
        norm_mix=norm_mix.reshape(1, -1), norm_mlp=norm_mlp.reshape(1, -1),
        w_in=w_in_pad.astype(BF16),
        qln=q_lora_norm.reshape(1, -1), kvn=kv_lora_norm.reshape(1, -1),
        wq=wq.reshape(Q_LORA, hk).astype(BF16),
        gq_plain=jnp.pad(gq, (0, pad)).reshape(1, LANES),
        gq_rope=jnp.concatenate([gq, _rope_partner(gq[MLA_NOPE:])]).reshape(1, LANES),
        wk=wk.reshape(KV_LORA, hk).astype(BF16),
        gk_plain=jnp.pad(mla_k_norm, (0, pad)).reshape(1, LANES),
        gk_rope=jnp.concatenate([mla_k_norm, _rope_partner(mla_k_norm[MLA_NOPE:])]).reshape(1, LANES),
        wv=wv.reshape(KV_LORA, hk).astype(BF16),
        e2=e2.astype(BF16),
        w_pool=w_pool.astype(BF16), pool_scale=pool_scale.reshape(1, -1),
        w_out=w_out, w_mlp1=w_mlp1, w_mlp2=w_mlp2)


def kernel(x_prompt, x_sample, cache_l0_mla_ckv, cache_l0_mla_krope, cache_l1_na_k, cache_l1_na_v, c, c_ctx, w_ada_l0, b_ada_l0, norm_mix_l0, norm_mlp_l0, w_mlp1_l0, w_mlp2_l0, w_in_l0, q_lora_norm_l0, kv_lora_norm_l0, w_q_up_l0, w_kv_up_l0, mla_q_norm_l0, mla_k_norm_l0, w_pool_l0, pool_scale_l0, w_out_l0, w_ada_l1, b_ada_l1, norm_mix_l1, norm_mlp_l1, w_mlp1_l1, w_mlp2_l1, w_in_l1, na_q_norm_l1, na_k_norm_l1, rel_bias_l1, w_out_l1):
    batch, seq, _ = x_prompt.shape
    dec_batch, dec_seq, _ = x_sample.shape
    past = cache_l0_mla_ckv.shape[1]
    xp = x_prompt.reshape(batch * seq, D_MODEL)
    xs = x_sample.reshape(dec_batch * dec_seq, D_MODEL)
    tiles_p = (batch * seq) // TOKEN_TILE
    tiles_s = dec_seq // TOKEN_TILE

    cond8 = jnp.concatenate([c_ctx[None, :], c, jnp.zeros((SUBLANES - 1 - dec_batch, D_MODEL), F32)], axis=0)
    mods = _ada(cond8, (w_ada_l0, w_ada_l1), (b_ada_l0, b_ada_l1))
    mods = mods.reshape(DEPTH, SUBLANES, 1, N_ADA * D_MODEL)
    (mod0_p, mod0_s), (mod1_p, mod1_s) = (((mods[l], 0), (mods[l], 1)) for l in range(DEPTH))

    w0 = _prep_l0(w_in_l0, q_lora_norm_l0, kv_lora_norm_l0, w_q_up_l0, w_kv_up_l0, mla_q_norm_l0, mla_k_norm_l0,
                  w_pool_l0, pool_scale_l0, w_out_l0, norm_mix_l0, norm_mlp_l0, w_mlp1_l0, w_mlp2_l0)
    a_p, q_p, k_p, v_p, ckv_p, kr_p = _l0_pre(xp, mod0_p, tiles_p, False, w0, None)
    a_s, q_s, k_s, v_s, _, _ = _l0_pre(xs, mod0_s, tiles_s, True, w0, _rope_tables(dec_seq))
    kr_cache = jnp.pad(cache_l0_mla_krope.reshape(dec_batch * past, MLA_ROPE), ((0, 0), (MLA_NOPE, LANES - MLA_QK)))
    k_c, v_c = _cache_kv(cache_l0_mla_ckv.reshape(dec_batch * past, KV_LORA), kr_cache, w0)
    mla_bound = _normed_len(mla_q_norm_l0 * (MLA_QK ** -0.5 * LOG2E), MLA_QK) * _normed_len(mla_k_norm_l0, MLA_QK)
    attn_p = _ctx_attn(_mla_ctx_kernel, q_p, k_p, v_p, mla_bound, seq, MLA_HEADS * MLA_V, "l0_mla_ctx_attn")
    attn_s, w16 = _mla_lat_attn(q_s, k_s, v_s, k_c, v_c, jnp.full((dec_batch, MLA_HEADS), mla_bound, F32),
                                dec_seq, past, (w0["w_out"], w0["w_mlp1"], w0["w_mlp2"]))
    w0 = dict(w0, w_out=w16[0], w_mlp1=w16[1], w_mlp2=w16[2])
    xp = _post(xp, attn_p, mod0_p, tiles_p, w0, "l0_post_ctx", pool=(a_p, seq))
    xs = _post(xs, attn_s, mod0_s, tiles_s, w0, "l0_post_latent", pool=(a_s, dec_seq))

    w1 = dict(norm_mix=norm_mix_l1.reshape(1, -1), norm_mlp=norm_mlp_l1.reshape(1, -1),
              w_in=w_in_l1.astype(BF16),
              gq=jnp.tile(na_q_norm_l1 * (NA_HEAD_DIM ** -0.5 * LOG2E), 2).reshape(1, LANES),
              gk=jnp.tile(na_k_norm_l1, 2).reshape(1, LANES),
              w_out=w_out_l1, w_mlp1=w_mlp1_l1, w_mlp2=w_mlp2_l1)
    q1_p, k1_p, v1_p, k1_new, v1_new = _l1_pre(xp, mod1_p, tiles_p, True, w1)
    q1_s, k1_s, v1_s = _l1_pre(xs, mod1_s, tiles_s, False, w1)
    q_len, k_len = _normed_len(w1["gq"], NA_HEAD_DIM), _normed_len(w1["gk"], NA_HEAD_DIM)
    na_p = _ctx_attn(_na_ctx_kernel, q1_p, k1_p, v1_p, q_len * k_len, seq, NA_SLABS * LANES, "l1_na_ctx_attn")
    cache_len = jnp.sqrt(jnp.max(jnp.sum(jnp.square(cache_l1_na_k.astype(F32)), axis=-1), axis=1))
    na_bound = (q_len * jnp.maximum(k_len, cache_len)
                + jnp.max(jnp.abs(rel_bias_l1.astype(F32)) * LOG2E, axis=(1, 2))[None, :])
    na_s, w16 = _na_lat_attn(q1_s, k1_s, v1_s,
                             cache_l1_na_k.reshape(dec_batch * past, -1), cache_l1_na_v.reshape(dec_batch * past, -1),
                             _na_bias_table(rel_bias_l1, dec_seq // GRID_W), na_bound, dec_seq, past,
                             (w1["w_out"], w1["w_mlp1"], w1["w_mlp2"]))
    w1 = dict(w1, w_out=w16[0], w_mlp1=w16[1], w_mlp2=w16[2])
    xp = _post(xp, na_p, mod1_p, tiles_p, w1, "l1_post_ctx")
    xs = _post(xs, na_s, mod1_s, tiles_s, w1, "l1_post_latent")

    return (xp.reshape(batch, seq, D_MODEL), xs.reshape(dec_batch, dec_seq, D_MODEL),
            ckv_p.reshape(batch, seq, KV_LORA), kr_p.reshape(batch, seq, MLA_ROPE),
            k1_new.reshape(batch, seq, NA_HEADS, NA_HEAD_DIM), v1_new.reshape(batch, seq, NA_HEADS, NA_HEAD_DIM))
```

```python
import functools

import jax
import jax.numpy as jnp
from jax import lax
from jax.experimental import pallas as pl
from jax.experimental.pallas import tpu as pltpu

F32 = jnp.float32
BF16 = jnp.bfloat16

D_MODEL = 1024
D_FF = 4 * D_MODEL
DEPTH = 2
N_ADA = 6
EPS = 1e-6
NEG_INF = -1e30
ROPE_THETA = 10000.0
GRID_W = 64
POOL_WIDTH = 512
POOL_WINDOWS = (2, 4, 8, 16)
POOL_GROUP = 128
POOL_HALO = 8
MLA_HEADS = 8
MLA_NOPE = 64
MLA_ROPE = 32
MLA_QK = MLA_NOPE + MLA_ROPE
MLA_V = 64
Q_LORA = 256
KV_LORA = 128
NA_HEADS = 16
NA_HEAD_DIM = 64
NA_KH = 8
NA_KW = 16
LOG2E = 1.4426950408889634

LANES = 128
SUBLANES = 8
VMEM_LIMIT = 56 * 1024 * 1024

TOKEN_TILE = 512
MLA_Q_TILE = 512
MLA_STEP_HEADS = 2
MLA_KEY_CHUNK = 2048
MLA_ONLINE_CHUNK = 1024
SCORE_LOOKAHEAD = 4
MAX_SCORE_BOUND = 40.0
FF_CHUNK = 1024
CTX_STEP_SEQS = 2
NA_SLABS = NA_HEADS * NA_HEAD_DIM // LANES
NA_TILE_ROWS = 4
NA_WIN_ROWS = NA_TILE_ROWS + NA_KH - 1


def _params(n_axes):
    return pltpu.CompilerParams(dimension_semantics=("arbitrary",) * n_axes,
                                vmem_limit_bytes=VMEM_LIMIT)


def _const_spec(shape):
    zeros = (0,) * len(shape)
    return pl.BlockSpec(shape, lambda *_: zeros, pipeline_mode=pl.Buffered(1))


def _dot(a, b):
    return jnp.dot(a, b, preferred_element_type=F32)


def _dot_nt(a, b):
    return lax.dot_general(a, b, (((1,), (1,)), ((), ())), preferred_element_type=F32)


def _dot_tn(a, b):
    return lax.dot_general(a, b, (((0,), (0,)), ((), ())), preferred_element_type=F32)


def _rms(x, w, n=None):
    n = x.shape[-1] if n is None else n
    ss = jnp.sum(x * x, axis=-1, keepdims=True)
    return x * lax.rsqrt(ss * (1.0 / n) + EPS) * w


def _modulate(x, nw, shift, scale):
    return _rms(x, nw) * (1.0 + scale) + shift


def _ada_spec(mod, tiles_per_group):
    mods, first_row = mod
    return pl.BlockSpec((None, 1, mods.shape[-1]), lambda i: (first_row + i // tiles_per_group, 0, 0))


def _ada_row(mod_ref, k):
    return mod_ref[:, k * D_MODEL:(k + 1) * D_MODEL]


def _ada_kernel(cond_ref, w0_ref, w1_ref, b_ref, o_ref):
    c = cond_ref[...]
    s = (c * jax.nn.sigmoid(c)).astype(BF16)
    for layer, w_ref in enumerate((w0_ref, w1_ref)):
        @pl.when(pl.program_id(0) == layer)
        def _(w_ref=w_ref):
            o_ref[...] = _dot(s, w_ref[...].astype(BF16)) + b_ref[...]


def _ada(cond8, w_ada, b_ada):
    n_out = w_ada[0].shape[1]
    tn = D_MODEL
    nb = n_out // tn
    return pl.pallas_call(
        _ada_kernel,
        out_shape=jax.ShapeDtypeStruct((DEPTH, SUBLANES, n_out), F32),
        grid=(DEPTH, nb),
        in_specs=[pl.BlockSpec((SUBLANES, D_MODEL), lambda l, j: (0, 0)),
                  pl.BlockSpec((D_MODEL, tn), lambda l, j: (0, jnp.where(l == 0, j, nb - 1))),
                  pl.BlockSpec((D_MODEL, tn), lambda l, j: (0, jnp.where(l == 1, j, 0))),
                  pl.BlockSpec((None, 1, tn), lambda l, j: (l, 0, j))],
        out_specs=pl.BlockSpec((None, SUBLANES, tn), lambda l, j: (l, 0, j)),
        compiler_params=_params(2),
        name="ada",
    )(cond8, w_ada[0], w_ada[1], jnp.stack(b_ada).reshape(DEPTH, 1, n_out))


def _rope(x, c, s):
    return x * c + pltpu.roll(x, LANES - MLA_ROPE, 1) * s


PAIR = 2 * LANES


def _head_rinv(raw, e2_ref):
    ss = _dot((raw * raw).astype(BF16), e2_ref[...])
    return lax.rsqrt(ss * (1.0 / MLA_QK) + EPS)


def _mla_keys_values(ckv, kr_slab, wk_ref, gk, wv_ref, e2_ref, rope_tabs, k_ref, v_ref):
    c16 = ckv.astype(BF16)
    tail = kr_slab * gk
    if rope_tabs is not None:
        tail = _rope(tail, *rope_tabs)
    ones_lane = (lax.broadcasted_iota(jnp.int32, (1, LANES), 1) == MLA_V).astype(F32)
    for p in range(MLA_HEADS // 2):
        cols = pl.ds(p * PAIR, PAIR)
        kn = _dot(c16, wk_ref[:, cols])
        v = _dot(c16, wv_ref[:, cols])
        rinv = _head_rinv(kn + jnp.tile(kr_slab, (1, 2)), e2_ref)
        for i in range(2):
            hs = slice(i * LANES, (i + 1) * LANES)
            k_ref[2 * p + i] = (rinv[:, hs] * (kn[:, hs] * gk + tail)).astype(BF16)
            v_ref[2 * p + i] = (v[:, hs] + ones_lane).astype(BF16)


def _l0_pre_kernel(latent, x_ref, mod_ref, nw_ref, win_ref, qln_ref, kvn_ref, wq_ref, gq_ref,
                   wk_ref, gk_ref, wv_ref, e2_ref, *rest):
    if latent:
        rope_ref, a_ref, q_ref, k_ref, v_ref, ckv_ref, kr_ref = rest
        tile_rows = x_ref.shape[0] // GRID_W
        row0 = (pl.program_id(0) * tile_rows) % rope_ref.shape[1]
        expand = lambda col, row: jnp.concatenate(
            [rope_ref[col] + rope_ref[row, pl.ds(row0 + r, 1), :] for r in range(tile_rows)], axis=0)
        rope_tabs = (expand(0, 1), expand(2, 3))
    else:
        a_ref, q_ref, k_ref, v_ref, ckv_ref, kr_ref = rest
        rope_tabs = None
    h = _modulate(x_ref[...], nw_ref[...], _ada_row(mod_ref, 0), _ada_row(mod_ref, 1)).astype(BF16)
    proj = _dot(h, win_ref[...])
    a_ref[...] = proj[:, :POOL_WIDTH]
    q_lat = proj[:, POOL_WIDTH:POOL_WIDTH + Q_LORA]
    kv_lat = proj[:, POOL_WIDTH + Q_LORA:POOL_WIDTH + Q_LORA + KV_LORA]
    kr_slab = proj[:, POOL_WIDTH + Q_LORA + KV_LORA:]
    ql = _rms(q_lat, qln_ref[...]).astype(BF16)
    gq = gq_ref[...]
    for p in range(MLA_HEADS // 2):
        q = _dot(ql, wq_ref[:, pl.ds(p * PAIR, PAIR)])
        q_rinv = _head_rinv(q, e2_ref)
        for i in range(2):
            hs = slice(i * LANES, (i + 1) * LANES)
            y = q[:, hs] * q_rinv[:, hs] * gq
            if rope_tabs is not None:
                y = _rope(y, *rope_tabs)
            q_ref[2 * p + i] = y.astype(BF16)
    ckv = _rms(kv_lat, kvn_ref[...])
    ckv_ref[...] = ckv
    kr_ref[...] = kr_slab[:, MLA_NOPE:MLA_QK]
    _mla_keys_values(ckv, kr_slab, wk_ref, gk_ref[...], wv_ref, e2_ref, rope_tabs, k_ref, v_ref)


def _slab_spec(slabs, tm):
    return pl.BlockSpec((slabs, tm, LANES), lambda i: (0, i, 0))


def _l0_pre(x, mod, tiles_per_group, latent, w, rope_tabs):
    n = x.shape[0]
    tm = TOKEN_TILE
    row = lambda i: (i, 0)
    hk = MLA_HEADS * LANES
    in_specs = [pl.BlockSpec((tm, D_MODEL), row),
                _ada_spec(mod, tiles_per_group),
                _const_spec((1, D_MODEL)),
                _const_spec((D_MODEL, D_MODEL)),
                _const_spec((1, Q_LORA)),
                _const_spec((1, KV_LORA)),
                _const_spec((Q_LORA, hk)),
                _const_spec((1, LANES)),
                _const_spec((KV_LORA, hk)),
                _const_spec((1, LANES)),
                _const_spec((KV_LORA, hk)),
                _const_spec((2 * LANES, 2 * LANES))]
    gains = "rope" if latent else "plain"
    args = [x, mod[0], w["norm_mix"], w["w_in"], w["qln"], w["kvn"], w["wq"], w["gq_" + gains], w["wk"],
            w["gk_" + gains], w["wv"], w["e2"]]
    if latent:
        assert tm % GRID_W == 0
        in_specs.append(_const_spec(rope_tabs.shape))
        args.append(rope_tabs)
    slab = jax.ShapeDtypeStruct((MLA_HEADS, n, LANES), BF16)
    out_shape = (jax.ShapeDtypeStruct((n, POOL_WIDTH), F32), slab, slab, slab,
                 jax.ShapeDtypeStruct((n, KV_LORA), F32),
                 jax.ShapeDtypeStruct((n, MLA_ROPE), F32))
    out_specs = (pl.BlockSpec((tm, POOL_WIDTH), row),
                 _slab_spec(MLA_HEADS, tm), _slab_spec(MLA_HEADS, tm), _slab_spec(MLA_HEADS, tm),
                 pl.BlockSpec((tm, KV_LORA), row), pl.BlockSpec((tm, MLA_ROPE), row))
    return pl.pallas_call(
        functools.partial(_l0_pre_kernel, latent),
        out_shape=out_shape, grid=(n // tm,), in_specs=in_specs, out_specs=out_specs,
        compiler_params=_params(1), name="l0_pre_latent" if latent else "l0_pre_ctx",
    )(*args)


def _cache_kv_kernel(ckv_ref, kr_ref, wk_ref, gk_ref, wv_ref, e2_ref, k_ref, v_ref):
    _mla_keys_values(ckv_ref[...], kr_ref[...], wk_ref, gk_ref[...], wv_ref, e2_ref, None, k_ref, v_ref)


def _cache_kv(ckv, kr_slab, w):
    n = ckv.shape[0]
    tm = TOKEN_TILE
    row = lambda i: (i, 0)
    hk = MLA_HEADS * LANES
    slab = jax.ShapeDtypeStruct((MLA_HEADS, n, LANES), BF16)
    return pl.pallas_call(
        _cache_kv_kernel,
        out_shape=(slab, slab),
        grid=(n // tm,),
        in_specs=[pl.BlockSpec((tm, KV_LORA), row), pl.BlockSpec((tm, LANES), row),
                  _const_spec((KV_LORA, hk)), _const_spec((1, LANES)), _const_spec((KV_LORA, hk)),
                  _const_spec((2 * LANES, 2 * LANES))],
        out_specs=(_slab_spec(MLA_HEADS, tm), _slab_spec(MLA_HEADS, tm)),
        compiler_params=_params(1), name="l0_cache_kv",
    )(ckv, kr_slab, w["wk"], w["gk_plain"], w["wv"], w["e2"])


def _pool_mixer(seq_len, tile, prev_ref, u_ref, next_ref, wp_ref, ps_ref, buf_ref):
    tm = u_ref.shape[0]
    piece = min(tm, seq_len)
    n_pieces = tm // piece
    stride = piece + 2 * POOL_HALO
    starts = []
    for j in range(n_pieces):
        pos0 = (tile * tm + j * piece) % seq_len
        prev = prev_ref[...] if j == 0 else u_ref[j * piece - POOL_HALO:j * piece, :]
        nxt = next_ref[...] if j == n_pieces - 1 else u_ref[(j + 1) * piece:(j + 1) * piece + POOL_HALO, :]
        base = j * stride
        buf_ref[base:base + POOL_HALO, :] = jnp.where(pos0 != 0, prev, 0.0)
        buf_ref[base + POOL_HALO:base + POOL_HALO + piece, :] = u_ref[j * piece:(j + 1) * piece, :]
        buf_ref[base + POOL_HALO + piece:base + stride, :] = jnp.where(pos0 + piece != seq_len, nxt, 0.0)
        starts.append(pos0)
    def group(g):
        win = POOL_WINDOWS[g]
        lanes = pl.ds(g * POOL_GROUP, POOL_GROUP)
        pooled = []
        for j in range(n_pieces):
            centre = j * stride + POOL_HALO
            lo = starts[j] + lax.broadcasted_iota(jnp.int32, (piece, 1), 0) - win // 2
            cnt = (jnp.minimum(lo + win, seq_len) - jnp.maximum(lo, 0)).astype(F32)
            acc = buf_ref[pl.ds(centre - win // 2, piece), lanes]
            for d in range(1 - win // 2, win - win // 2):
                acc = acc + buf_ref[pl.ds(centre + d, piece), lanes]
            pooled.append(acc / cnt - buf_ref[pl.ds(centre, piece), lanes])
        pooled = pooled[0] if n_pieces == 1 else jnp.concatenate(pooled, axis=0)
        return (_dot(pooled.astype(BF16), wp_ref[g]) * ps_ref[:, lanes]).astype(BF16)

    return [functools.partial(group, g) for g in range(len(POOL_WINDOWS))]


def _attend(streams, running_max):
    items = [(si, ch) for si, (_, chunks, _) in enumerate(streams) for ch in chunks]

    def score(item):
        si, (k_fn, _, extra_fn) = item
        s = _dot_nt(k_fn(), streams[si][0])
        return s if extra_fn is None else s + extra_fn()

    state = [None] * len(streams)
    pending = [score(item) for item in items[:SCORE_LOOKAHEAD]]
    for i, (si, (_, v_fn, _)) in enumerate(items):
        s = pending.pop(0)
        if i + SCORE_LOOKAHEAD < len(items):
            pending.append(score(items[i + SCORE_LOOKAHEAD]))
        want_den = streams[si][2]
        m_new, alpha = None, 1.0
        if state[si] is not None:
            m, acc, den = state[si]
        if running_max:
            m_new = jnp.max(s, axis=0, keepdims=True)
            if state[si] is not None:
                m_new = jnp.maximum(m, m_new)
                alpha = jnp.exp2(m - m_new)
            s = s - m_new
        p = jnp.exp2(s)
        pv = _dot_tn(v_fn(), p.astype(BF16))
        psum = jnp.sum(p, axis=0, keepdims=True) if want_den else None
        if state[si] is None:
            state[si] = (m_new, pv, psum)
        elif running_max:
            state[si] = (m_new, acc * alpha + pv, den * alpha + psum if want_den else None)
        else:
            state[si] = (None, acc + pv, den + psum if want_den else None)
    return [(acc, den) for _, acc, den in state]


def _ref_chunks(k_ref, v_ref, slab, chunk, extra_fn=None, lo=0, hi=None):
    hi = k_ref.shape[1] if hi is None else hi
    out = []
    for c in range(lo, hi, chunk):
        rows = pl.ds(c, min(chunk, hi - c))
        out.append((lambda rows=rows: k_ref[slab, rows, :], lambda rows=rows: v_ref[slab, rows, :], extra_fn))
    return out


def _mla_finish(o_t):
    return o_t[:MLA_V] / o_t[MLA_V:MLA_V + 1]


def _na_queries(q2):
    low = lax.broadcasted_iota(jnp.int32, (1, LANES), 1) < NA_HEAD_DIM
    zero = jnp.zeros_like(q2)
    return jnp.concatenate([jnp.where(low, q2, zero), jnp.where(low, zero, q2)], axis=0)


def _na_finish(o_t, den):
    nq = o_t.shape[1] // 2
    o = o_t / den
    return jnp.concatenate([o[:NA_HEAD_DIM, :nq], o[NA_HEAD_DIM:, nq:]], axis=0)


def _attend_guarded(bound, streams_fn):
    return lax.cond(bound <= MAX_SCORE_BOUND,
                    lambda: _attend(streams_fn(False), False),
                    lambda: _attend(streams_fn(True), True))


def _mla_ctx_kernel(seq, bound_ref, q_ref, k_ref, v_ref, o_ref):
    spans = [(lo, lo + seq) for lo in range(0, k_ref.shape[1], seq)]
    res = _attend_guarded(bound_ref[0, 0], lambda _: [
        (q_ref[h, lo:hi, :], _ref_chunks(k_ref, v_ref, h, seq, lo=lo, hi=hi), False)
        for lo, hi in spans for h in range(MLA_HEADS)])
    for i, (lo, hi) in enumerate(spans):
        for hp in range(MLA_HEADS // 2):
            o_t = jnp.concatenate([_mla_finish(res[i * MLA_HEADS + h][0]) for h in (2 * hp, 2 * hp + 1)], axis=0)
            o_ref[lo:hi, hp * LANES:(hp + 1) * LANES] = o_t.T.astype(BF16)


def _na_ctx_kernel(seq, bound_ref, q_ref, k_ref, v_ref, o_ref):
    spans = [(lo, lo + seq) for lo in range(0, k_ref.shape[1], seq)]
    res = _attend_guarded(bound_ref[0, 0], lambda _: [
        (_na_queries(q_ref[s, lo:hi, :]), _ref_chunks(k_ref, v_ref, s, seq, lo=lo, hi=hi), True)
        for lo, hi in spans for s in range(NA_SLABS)])
    for i, (lo, hi) in enumerate(spans):
        for s in range(NA_SLABS):
            o_ref[lo:hi, s * LANES:(s + 1) * LANES] = _na_finish(*res[i * NA_SLABS + s]).T.astype(BF16)


def _ctx_attn(body, q, k, v, bound, seq, width, name):
    slabs, n, _ = q.shape
    rows = CTX_STEP_SEQS * seq
    spec = pl.BlockSpec((slabs, rows, LANES), lambda b: (0, b, 0))
    return pl.pallas_call(
        functools.partial(body, seq),
        out_shape=jax.ShapeDtypeStruct((n, width), BF16),
        grid=(n // rows,),
        in_specs=[pl.BlockSpec(memory_space=pltpu.SMEM), spec, spec, spec],
        out_specs=pl.BlockSpec((rows, width), lambda b: (b, 0)),
        compiler_params=_params(1), name=name,
    )(bound.reshape(1, 1), q, k, v)


SIDE_PIECE_ELEMS = 512 * 1024


def _side_piece_rows(n_cols):
    return 1 << ((SIDE_PIECE_ELEMS // n_cols).bit_length() - 1)


def _side_cast_plan(shapes):
    plan = []
    for wi, (n_rows, n_cols) in enumerate(shapes):
        rows = _side_piece_rows(n_cols)
        assert n_rows % rows == 0
        plan += [(wi, r0, rows) for r0 in range(0, n_rows, rows)]
    return plan


def _side_cast_scratch(shapes):
    widths = sorted({c for _, c in shapes})
    stages = [pltpu.VMEM((2, _side_piece_rows(c), c), dt) for dt in (F32, BF16) for c in widths]
    return stages + [pltpu.SemaphoreType.DMA((2,)), pltpu.SemaphoreType.DMA((2,))]


def _side_cast(step, srcs, dsts, scratch):
    shapes = [s.shape for s in srcs]
    widths = sorted({c for _, c in shapes})
    stage_in = dict(zip(widths, scratch[:len(widths)]))
    stage_out = dict(zip(widths, scratch[len(widths):2 * len(widths)]))
    sem_in, sem_out = scratch[2 * len(widths):]
    plan = _side_cast_plan(shapes)

    def refs(k):
        wi, r0, rows = plan[k]
        cols = shapes[wi][1]
        return (srcs[wi].at[pl.ds(r0, rows), :], dsts[wi].at[pl.ds(r0, rows), :],
                stage_in[cols].at[k % 2], stage_out[cols].at[k % 2])

    read = lambda k: pltpu.make_async_copy(refs(k)[0], refs(k)[2], sem_in.at[k % 2])
    write = lambda k: pltpu.make_async_copy(refs(k)[3], refs(k)[1], sem_out.at[k % 2])
    n = len(plan)

    def piece_step(k):
        if k == 0:
            read(0).start()
        if k >= 2:
            write(k - 2).wait()
        if k < n:
            read(k).wait()
            if k + 1 < n:
                read(k + 1).start()
            refs(k)[3][...] = refs(k)[2][...].astype(BF16)
            write(k).start()

    @pl.when(step < n + 2)
    def _():
        for k in range(n + 2):
            pl.when(step == k)(functools.partial(piece_step, k))


def _mla_lat_kernel(n_cast, bound_ref, q_ref, kl_ref, vl_ref, kc_ref, vc_ref, *rest):
    o_ref = rest[n_cast]
    linear_step = (pl.program_id(0) * pl.num_programs(1) + pl.program_id(1)) * pl.num_programs(2) + pl.program_id(2)
    _side_cast(linear_step, rest[:n_cast], rest[n_cast + 1:2 * n_cast + 1], rest[2 * n_cast + 1:])
    batch, head0 = pl.program_id(0), pl.program_id(1) * MLA_STEP_HEADS
    heads = range(MLA_STEP_HEADS)
    bounds = [bound_ref[batch, head0 + h] for h in heads]

    def streams(running_max):
        size = MLA_ONLINE_CHUNK if running_max else MLA_KEY_CHUNK
        return [(q_ref[h], _ref_chunks(kl_ref, vl_ref, h, size) + _ref_chunks(kc_ref, vc_ref, h, size), False)
                for h in heads]

    res = _attend_guarded(functools.reduce(jnp.maximum, bounds), streams)
    o_ref[...] = jnp.concatenate([_mla_finish(o) for o, _ in res], axis=0).T.astype(BF16)


def _normed_len(gain, dim):
    return (dim ** 0.5) * jnp.max(jnp.abs(gain.astype(F32)))


def _side_cast_specs(weights, n_steps):
    shapes = [wt.shape for wt in weights]
    assert n_steps >= len(_side_cast_plan(shapes)) + 2
    hbm = pl.BlockSpec(memory_space=pl.ANY)
    return ([hbm] * len(weights), [jax.ShapeDtypeStruct(s, BF16) for s in shapes], [hbm] * len(weights),
            _side_cast_scratch(shapes))


def _mla_lat_attn(q, k, v, kc, vc, bound, seq, past, weights):
    n = q.shape[1]
    tq = MLA_Q_TILE
    nq = seq // tq
    hs = MLA_STEP_HEADS
    grid = (n // seq, MLA_HEADS // hs, nq)
    w_in, w_shapes, w_out, w_scratch = _side_cast_specs(weights, grid[0] * grid[1] * grid[2])
    outs = pl.pallas_call(
        functools.partial(_mla_lat_kernel, len(weights)),
        out_shape=(jax.ShapeDtypeStruct((n, MLA_HEADS * MLA_V), BF16), *w_shapes),
        grid=grid,
        in_specs=[pl.BlockSpec(memory_space=pltpu.SMEM),
                  pl.BlockSpec((hs, tq, LANES), lambda b, h, i: (h, b * nq + i, 0)),
                  pl.BlockSpec((hs, seq, LANES), lambda b, h, i: (h, b, 0)),
                  pl.BlockSpec((hs, seq, LANES), lambda b, h, i: (h, b, 0)),
                  pl.BlockSpec((hs, past, LANES), lambda b, h, i: (h, b, 0)),
                  pl.BlockSpec((hs, past, LANES), lambda b, h, i: (h, b, 0))] + w_in,
        out_specs=(pl.BlockSpec((tq, hs * MLA_V), lambda b, h, i: (b * nq + i, h)), *w_out),
        scratch_shapes=w_scratch,
        compiler_params=_params(3), name="l0_mla_latent_attn",
    )(bound, q, k, v, kc, vc, *weights)
    return outs[0], outs[1:]


def _na_first_row(r, rows):
    return jnp.clip(r - NA_KH // 2, 0, rows - NA_KH)


def _na_lat_kernel(rows, n_cast, bound_ref, q_ref, k_ref, v_ref, kc_ref, vc_ref, t2_ref, *rest):
    o_ref = rest[n_cast]
    _side_cast(pl.program_id(0) * pl.num_programs(1) + pl.program_id(1),
               rest[:n_cast], rest[n_cast + 1:2 * n_cast + 1], rest[2 * n_cast + 1:])
    nq = NA_TILE_ROWS * GRID_W
    nk = NA_WIN_ROWS * GRID_W
    n_off = 2 * NA_KH - 1
    r0 = pl.program_id(1) * NA_TILE_ROWS
    w0 = jnp.clip(r0 - NA_KH // 2, 0, rows - NA_WIN_ROWS)

    def block_index(j, t):
        key_row = w0 + j
        i = r0 + 2 * t
        first_u, first_l = _na_first_row(i, rows), _na_first_row(i + 1, rows)
        in_u = (key_row >= first_u) & (key_row < first_u + NA_KH)
        in_l = (key_row >= first_l) & (key_row < first_l + NA_KH)
        d = jnp.clip(key_row - i + NA_KH - 1, 0, n_off - 1)
        return jnp.where(in_u & in_l, d, jnp.where(in_u, n_off, jnp.where(in_l, n_off + 1, n_off + 2)))

    idx = [[block_index(j, t) for t in range(NA_TILE_ROWS // 2)] for j in range(NA_WIN_ROWS)]

    def local_bias(s):
        heads = [jnp.concatenate([jnp.concatenate([t2_ref[2 * s + hh, i] for i in row], axis=1) for row in idx], axis=0)
                 for hh in range(2)]
        return jnp.concatenate(heads, axis=1)

    batch = pl.program_id(0)
    bounds = [bound_ref[batch, h] for h in range(NA_HEADS)]

    def cache_chunk(s):
        lanes = pl.ds(s * LANES, LANES)
        return [(lambda: kc_ref[:, lanes].astype(BF16), lambda: vc_ref[:, lanes].astype(BF16), None)]

    res = _attend_guarded(
        functools.reduce(jnp.maximum, bounds),
        lambda _: [(_na_queries(q_ref[s]),
                    _ref_chunks(k_ref, v_ref, s, nk, functools.partial(local_bias, s)) + cache_chunk(s),
                    True) for s in range(NA_SLABS)])
    for s in range(NA_SLABS):
        o_ref[:, s * LANES:(s + 1) * LANES] = _na_finish(*res[s]).T.astype(BF16)


def _na_lat_attn(q, k, v, kc, vc, t2, bound, seq, past, weights):
    slabs, n, _ = q.shape
    rows = seq // GRID_W
    tiles = rows // NA_TILE_ROWS
    nq = NA_TILE_ROWS * GRID_W
    nk = NA_WIN_ROWS * GRID_W

    def win_map(b, t):
        w0 = jnp.clip(t * NA_TILE_ROWS - NA_KH // 2, 0, rows - NA_WIN_ROWS)
        return (0, pl.multiple_of(b * seq + w0 * GRID_W, GRID_W), 0)

    win_spec = pl.BlockSpec((pl.Element(slabs), pl.Element(nk), pl.Element(LANES)), win_map)
    ctx_spec = pl.BlockSpec((past, slabs * LANES), lambda b, t: (b, 0))
    w_in, w_shapes, w_out, w_scratch = _side_cast_specs(weights, (n // seq) * tiles)
    outs = pl.pallas_call(
        functools.partial(_na_lat_kernel, rows, len(weights)),
        out_shape=(jax.ShapeDtypeStruct((n, slabs * LANES), BF16), *w_shapes),
        grid=(n // seq, tiles),
        in_specs=[pl.BlockSpec(memory_space=pltpu.SMEM),
                  pl.BlockSpec((slabs, nq, LANES), lambda b, t: (0, b * tiles + t, 0)),
                  win_spec, win_spec, ctx_spec, ctx_spec,
                  _const_spec(t2.shape)] + w_in,
        out_specs=(pl.BlockSpec((nq, slabs * LANES), lambda b, t: (b * tiles + t, 0)), *w_out),
        scratch_shapes=w_scratch,
        compiler_params=_params(2), name="l1_na_latent_attn",
    )(bound, q, k, v, kc, vc, t2, *weights)
    return outs[0], outs[1:]


def _pair_norm(x2, g2):
    low = lax.broadcasted_iota(jnp.int32, (1, LANES), 1) < NA_HEAD_DIM
    sq = x2 * x2
    ss_all = jnp.sum(sq, axis=-1, keepdims=True)
    ss_low = jnp.sum(jnp.where(low, sq, 0.0), axis=-1, keepdims=True)
    ss = jnp.where(low, ss_low, ss_all - ss_low)
    return x2 * lax.rsqrt(ss * (1.0 / NA_HEAD_DIM) + EPS) * g2


def _l1_pre_kernel(keep_f32, x_ref, mod_ref, nw_ref, win_ref, gq_ref, gk_ref, q_ref, k_ref, v_ref, *f32_refs):
    h = _modulate(x_ref[...], nw_ref[...], _ada_row(mod_ref, 0), _ada_row(mod_ref, 1)).astype(BF16)
    width = NA_SLABS * LANES
    q = _dot(h, win_ref[:, 0:width])
    k = _dot(h, win_ref[:, width:2 * width])
    v = _dot(h, win_ref[:, 2 * width:3 * width])
    gq = gq_ref[...]
    gk = gk_ref[...]
    for s in range(NA_SLABS):
        ls = slice(s * LANES, (s + 1) * LANES)
        q_ref[s] = _pair_norm(q[:, ls], gq).astype(BF16)
        kn = _pair_norm(k[:, ls], gk)
        k_ref[s] = kn.astype(BF16)
        v_ref[s] = v[:, ls].astype(BF16)
        if keep_f32:
            f32_refs[0][:, ls] = kn
    if keep_f32:
        f32_refs[1][...] = v


def _l1_pre(x, mod, tiles_per_group, keep_f32, w):
    n = x.shape[0]
    tm = TOKEN_TILE
    row = lambda i: (i, 0)
    width = NA_SLABS * LANES
    slab = jax.ShapeDtypeStruct((NA_SLABS, n, LANES), BF16)
    out_shape = [slab, slab, slab]
    out_specs = [_slab_spec(NA_SLABS, tm)] * 3
    if keep_f32:
        out_shape += [jax.ShapeDtypeStruct((n, width), F32)] * 2
        out_specs += [pl.BlockSpec((tm, width), row)] * 2
    return pl.pallas_call(
        functools.partial(_l1_pre_kernel, keep_f32),
        out_shape=tuple(out_shape), grid=(n // tm,),
        in_specs=[pl.BlockSpec((tm, D_MODEL), row),
                  _ada_spec(mod, tiles_per_group),
                  _const_spec((1, D_MODEL)),
                  _const_spec((D_MODEL, 3 * width)),
                  _const_spec((1, LANES)),
                  _const_spec((1, LANES))],
        out_specs=tuple(out_specs),
        compiler_params=_params(1), name="l1_pre_ctx" if keep_f32 else "l1_pre_latent",
    )(x, mod[0], w["norm_mix"], w["w_in"], w["gq"], w["gk"])


def _post_kernel(pool_seq, x_ref, attn_ref, *rest):
    rest = list(rest)
    take = lambda k: [rest.pop(0) for _ in range(k)]
    pool_in = take(8) if pool_seq is not None else None
    mod_ref, nw_ref, wo_ref, w1_ref, w2_ref, o_ref = take(6)
    pool_scratch = take(2) if pool_seq is not None else None
    step = pl.program_id(0)
    n_chunks = D_FF // FF_CHUNK
    if pool_seq is not None:
        first_refs, next_refs, (wp_ref, ps_ref) = pool_in[0:3], pool_in[3:6], pool_in[6:8]
        buf_ref, pooled_ref = pool_scratch

        @pl.when(step == 0)
        def _():
            for g, group in enumerate(_pool_mixer(pool_seq, 0, *first_refs, wp_ref, ps_ref, buf_ref)):
                pooled_ref[:, g * POOL_GROUP:(g + 1) * POOL_GROUP] = group()

        mixed = _dot(jnp.concatenate([pooled_ref[...], attn_ref[...]], axis=-1), wo_ref[...])
        next_groups = _pool_mixer(pool_seq, jnp.minimum(step + 1, pl.num_programs(0) - 1), *next_refs,
                                  wp_ref, ps_ref, buf_ref)
    else:
        mixed = _dot(attn_ref[...], wo_ref[...])
    x1 = x_ref[...] + _ada_row(mod_ref, 2) * mixed
    h = _modulate(x1, nw_ref[...], _ada_row(mod_ref, 3), _ada_row(mod_ref, 4)).astype(BF16)
    acc = jnp.zeros(x1.shape, F32)
    for c in range(n_chunks):
        cs = pl.ds(c * FF_CHUNK, FF_CHUNK)
        u = jnp.square(jnp.maximum(_dot(h, w1_ref[:, cs]), 0.0)).astype(BF16)
        acc = acc + _dot(u, w2_ref[cs, :])
        if pool_seq is not None:
            for g in range(c * len(next_groups) // n_chunks, (c + 1) * len(next_groups) // n_chunks):
                pooled_ref[:, g * POOL_GROUP:(g + 1) * POOL_GROUP] = next_groups[g]()
    o_ref[...] = x1 + _ada_row(mod_ref, 5) * acc


def _post(x, attn, mod, tiles_per_group, w, name, pool=None):
    n = x.shape[0]
    tm = TOKEN_TILE
    row = lambda i: (i, 0)
    in_specs = [pl.BlockSpec((tm, D_MODEL), row), pl.BlockSpec((tm, attn.shape[1]), row)]
    args = [x, attn]
    scratch = []
    pool_seq = None
    if pool is not None:
        a_in, pool_seq = pool
        hb = tm // POOL_HALO
        last = n // POOL_HALO - 1
        pieces = tm // min(tm, pool_seq)
        last_tile = n // tm - 1

        def tile_specs(tile_of):
            return [pl.BlockSpec((POOL_HALO, POOL_WIDTH), lambda i: (jnp.maximum(tile_of(i) * hb - 1, 0), 0)),
                    pl.BlockSpec((tm, POOL_WIDTH), lambda i: (tile_of(i), 0)),
                    pl.BlockSpec((POOL_HALO, POOL_WIDTH), lambda i: (jnp.minimum((tile_of(i) + 1) * hb, last), 0))]

        in_specs += tile_specs(lambda i: 0) + tile_specs(lambda i: jnp.minimum(i + 1, last_tile))
        in_specs += [_const_spec((len(POOL_WINDOWS), POOL_GROUP, POOL_GROUP)), _const_spec((1, POOL_WIDTH))]
        args += [a_in] * 6 + [w["w_pool"], w["pool_scale"]]
        scratch = [pltpu.VMEM((tm + 2 * POOL_HALO * pieces, POOL_WIDTH), F32), pltpu.VMEM((tm, POOL_WIDTH), BF16)]
    in_specs += [_ada_spec(mod, tiles_per_group),
                 _const_spec((1, D_MODEL)),
                 _const_spec((D_MODEL, D_MODEL)),
                 _const_spec((D_MODEL, D_FF)),
                 _const_spec((D_FF, D_MODEL))]
    args += [mod[0], w["norm_mlp"], w["w_out"], w["w_mlp1"], w["w_mlp2"]]
    return pl.pallas_call(
        functools.partial(_post_kernel, pool_seq),
        out_shape=jax.ShapeDtypeStruct((n, D_MODEL), F32), grid=(n // tm,),
        in_specs=in_specs, out_specs=pl.BlockSpec((tm, D_MODEL), row),
        scratch_shapes=scratch, compiler_params=_params(1), name=name,
    )(*args)


def _rope_tables(seq):
    assert seq == GRID_W * GRID_W
    half = MLA_ROPE // 2
    inv_freq = jnp.power(ROPE_THETA, -jnp.arange(0, half, 2, dtype=F32) / half)
    ang = jnp.arange(GRID_W, dtype=F32)[:, None] * inv_freq[None, :]
    cos, sin = jnp.cos(ang), jnp.sin(ang)
    fill = lambda n, v=0.0: jnp.full((GRID_W, n), v, F32)
    spare = LANES - MLA_QK
    return jnp.stack([
        jnp.concatenate([fill(MLA_NOPE, 1.0), fill(half), cos, cos, fill(spare)], axis=1),
        jnp.concatenate([fill(MLA_NOPE), cos, cos, fill(half), fill(spare)], axis=1),
        jnp.concatenate([fill(MLA_NOPE), fill(half), -sin, sin, fill(spare)], axis=1),
        jnp.concatenate([fill(MLA_NOPE), -sin, sin, fill(half), fill(spare)], axis=1)])


def _rope_partner(tail):
    shape = tail.shape
    t = tail.reshape(shape[:-1] + (2, 2, MLA_ROPE // 4))
    return t[..., ::-1, :].reshape(shape)


def _na_bias_table(rel_bias, rows):
    qcol = jnp.arange(GRID_W)
    kcol = jnp.arange(GRID_W)
    col_start = jnp.clip(qcol - NA_KW // 2, 0, GRID_W - NA_KW)
    col_mask = (kcol[:, None] >= col_start[None, :]) & (kcol[:, None] < col_start[None, :] + NA_KW)
    offs = jnp.arange(2 * NA_KW - 1)
    onehot = (kcol[None, :, None] - qcol[None, None, :] + NA_KW - 1 == offs[:, None, None]).astype(F32)
    blocks = jnp.einsum('hdj,jkq->hdkq', rel_bias.astype(F32) * LOG2E, onehot, precision=lax.Precision.HIGHEST)
    blocks = jnp.where(col_mask[None, None], blocks, NEG_INF)
    neg = jnp.full_like(blocks[:, :1], NEG_INF)
    both = jnp.concatenate([blocks, jnp.concatenate([neg, blocks[:, :-1]], axis=1)], axis=-1)
    d_upper, d_lower = _na_partial_offsets(rows)
    upper_only = jnp.concatenate([blocks[:, d_upper:d_upper + 1], neg], axis=-1)
    lower_only = jnp.concatenate([neg, blocks[:, d_lower - 1:d_lower]], axis=-1)
    return jnp.concatenate([both, upper_only, lower_only, jnp.concatenate([neg, neg], axis=-1)], axis=1)


def _na_partial_offsets(rows):
    first = lambda r: min(max(r - NA_KH // 2, 0), rows - NA_KH)
    upper, lower = set(), set()
    for r0 in range(0, rows, NA_TILE_ROWS):
        w0 = min(max(r0 - NA_KH // 2, 0), rows - NA_WIN_ROWS)
        for j in range(NA_WIN_ROWS):
            for t in range(NA_TILE_ROWS // 2):
                key_row, i = w0 + j, r0 + 2 * t
                in_u = first(i) <= key_row < first(i) + NA_KH
                in_l = first(i + 1) <= key_row < first(i + 1) + NA_KH
                d = key_row - i + NA_KH - 1
                if in_u and in_l:
                    assert 1 <= d <= 2 * NA_KH - 2
                elif in_u:
                    upper.add(d)
                elif in_l:
                    lower.add(d)
    assert len(upper) == 1 and len(lower) == 1, (upper, lower)
    return upper.pop(), lower.pop()


def _prep_l0(w_in, q_lora_norm, kv_lora_norm, w_q_up, w_kv_up, mla_q_norm, mla_k_norm, w_pool, pool_scale,
             w_out, norm_mix, norm_mlp, w_mlp1, w_mlp2):
    lat_end = POOL_WIDTH + Q_LORA + KV_LORA
    w_in_pad = jnp.concatenate([w_in[:, :lat_end], jnp.zeros((D_MODEL, MLA_NOPE), w_in.dtype),
                                w_in[:, lat_end:], _rope_partner(w_in[:, lat_end:])], axis=1)
    pad = LANES - MLA_QK
    hk = MLA_HEADS * LANES
    wq = w_q_up.reshape(Q_LORA, MLA_HEADS, MLA_QK)
    wq = jnp.concatenate([wq, _rope_partner(wq[:, :, MLA_NOPE:])], axis=-1)
    kv = w_kv_up.reshape(KV_LORA, MLA_HEADS, MLA_NOPE + MLA_V)
    wk = jnp.pad(kv[:, :, :MLA_NOPE], ((0, 0), (0, 0), (0, LANES - MLA_NOPE)))
    wv = jnp.pad(kv[:, :, MLA_NOPE:], ((0, 0), (0, 0), (0, LANES - MLA_V)))
    gq = mla_q_norm * (MLA_QK ** -0.5 * LOG2E)
    slab_lane = jnp.arange(2 * LANES)
    e2 = (slab_lane[:, None] // LANES == slab_lane[None, :] // LANES) & (slab_lane[:, None] % LANES < MLA_QK)
    return dict(
        norm_mix=norm_mix.reshape(1, -1), norm_mlp=norm_mlp.reshape(1, -1),
        w_in=w_in_pad.astype(BF16),
        qln=q_lora_norm.reshape(1, -1), kvn=kv_lora_norm.reshape(1, -1),
        wq=wq.reshape(Q_LORA, hk).astype(BF16),
        gq_plain=jnp.pad(gq, (0, pad)).reshape(1, LANES),
        gq_rope=jnp.concatenate([gq, _rope_partner(gq[MLA_NOPE:])]).reshape(1, LANES),
        wk=wk.reshape(KV_LORA, hk).astype(BF16),
        gk_plain=jnp.pad(mla_k_norm, (0, pad)).reshape(1, LANES),
        gk_rope=jnp.concatenate([mla_k_norm, _rope_partner(mla_k_norm[MLA_NOPE:])]).reshape(1, LANES),
        wv=wv.reshape(KV_LORA, hk).astype(BF16),
        e2=e2.astype(BF16),
        w_pool=w_pool.astype(BF16), pool_scale=pool_scale.reshape(1, -1),
        w_out=w_out, w_mlp1=w_mlp1, w_mlp2=w_mlp2)


def kernel(x_prompt, x_sample, cache_l0_mla_ckv, cache_l0_mla_krope, cache_l1_na_k, cache_l1_na_v, c, c_ctx, w_ada_l0, b_ada_l0, norm_mix_l0, norm_mlp_l0, w_mlp1_l0, w_mlp2_l0, w_in_l0, q_lora_norm_l0, kv_lora_norm_l0, w_q_up_l0, w_kv_up_l0, mla_q_norm_l0, mla_k_norm_l0, w_pool_l0, pool_scale_l0, w_out_l0, w_ada_l1, b_ada_l1, norm_mix_l1, norm_mlp_l1, w_mlp1_l1, w_mlp2_l1, w_in_l1, na_q_norm_l1, na_k_norm_l1, rel_bias_l1, w_out_l1):
    batch, seq, _ = x_prompt.shape
    dec_batch, dec_seq, _ = x_sample.shape
    past = cache_l0_mla_ckv.shape[1]
    xp = x_prompt.reshape(batch * seq, D_MODEL)
    xs = x_sample.reshape(dec_batch * dec_seq, D_MODEL)
    tiles_p = (batch * seq) // TOKEN_TILE
    tiles_s = dec_seq // TOKEN_TILE

    cond8 = jnp.concatenate([c_ctx[None, :], c, jnp.zeros((SUBLANES - 1 - dec_batch, D_MODEL), F32)], axis=0)
    mods = _ada(cond8, (w_ada_l0, w_ada_l1), (b_ada_l0, b_ada_l1))
    mods = mods.reshape(DEPTH, SUBLANES, 1, N_ADA * D_MODEL)
    (mod0_p, mod0_s), (mod1_p, mod1_s) = (((mods[l], 0), (mods[l], 1)) for l in range(DEPTH))

    w0 = _prep_l0(w_in_l0, q_lora_norm_l0, kv_lora_norm_l0, w_q_up_l0, w_kv_up_l0, mla_q_norm_l0, mla_k_norm_l0,
                  w_pool_l0, pool_scale_l0, w_out_l0, norm_mix_l0, norm_mlp_l0, w_mlp1_l0, w_mlp2_l0)
    a_p, q_p, k_p, v_p, ckv_p, kr_p = _l0_pre(xp, mod0_p, tiles_p, False, w0, None)
    a_s, q_s, k_s, v_s, _, _ = _l0_pre(xs, mod0_s, tiles_s, True, w0, _rope_tables(dec_seq))
    kr_cache = jnp.pad(cache_l0_mla_krope.reshape(dec_batch * past, MLA_ROPE), ((0, 0), (MLA_NOPE, LANES - MLA_QK)))
    k_c, v_c = _cache_kv(cache_l0_mla_ckv.reshape(dec_batch * past, KV_LORA), kr_cache, w0)
    mla_bound = _normed_len(mla_q_norm_l0 * (MLA_QK ** -0.5 * LOG2E), MLA_QK) * _normed_len(mla_k_norm_l0, MLA_QK)
    attn_p = _ctx_attn(_mla_ctx_kernel, q_p, k_p, v_p, mla_bound, seq, MLA_HEADS * MLA_V, "l0_mla_ctx_attn")
    attn_s, w16 = _mla_lat_attn(q_s, k_s, v_s, k_c, v_c, jnp.full((dec_batch, MLA_HEADS), mla_bound, F32),
                                dec_seq, past, (w0["w_out"], w0["w_mlp1"], w0["w_mlp2"], w_in_l1))
    w0 = dict(w0, w_out=w16[0], w_mlp1=w16[1], w_mlp2=w16[2])
    xp = _post(xp, attn_p, mod0_p, tiles_p, w0, "l0_post_ctx", pool=(a_p, seq))
    xs = _post(xs, attn_s, mod0_s, tiles_s, w0, "l0_post_latent", pool=(a_s, dec_seq))

    w1 = dict(norm_mix=norm_mix_l1.reshape(1, -1), norm_mlp=norm_mlp_l1.reshape(1, -1),
              w_in=w16[3],
              gq=jnp.tile(na_q_norm_l1 * (NA_HEAD_DIM ** -0.5 * LOG2E), 2).reshape(1, LANES),
              gk=jnp.tile(na_k_norm_l1, 2).reshape(1, LANES),
              w_out=w_out_l1, w_mlp1=w_mlp1_l1, w_mlp2=w_mlp2_l1)
    q1_p, k1_p, v1_p, k1_new, v1_new = _l1_pre(xp, mod1_p, tiles_p, True, w1)
    q1_s, k1_s, v1_s = _l1_pre(xs, mod1_s, tiles_s, False, w1)
    q_len, k_len = _normed_len(w1["gq"], NA_HEAD_DIM), _normed_len(w1["gk"], NA_HEAD_DIM)
    na_p = _ctx_attn(_na_ctx_kernel, q1_p, k1_p, v1_p, q_len * k_len, seq, NA_SLABS * LANES, "l1_na_ctx_attn")
    cache_len = jnp.sqrt(jnp.max(jnp.sum(jnp.square(cache_l1_na_k.astype(F32)), axis=-1), axis=1))
    na_bound = (q_len * jnp.maximum(k_len, cache_len)
                + jnp.max(jnp.abs(rel_bias_l1.astype(F32)) * LOG2E, axis=(1, 2))[None, :])
    na_s, w16 = _na_lat_attn(q1_s, k1_s, v1_s,
                             cache_l1_na_k.reshape(dec_batch * past, -1), cache_l1_na_v.reshape(dec_batch * past, -1),
                             _na_bias_table(rel_bias_l1, dec_seq // GRID_W), na_bound, dec_seq, past,
                             (w1["w_out"], w1["w_mlp1"], w1["w_mlp2"]))
    w1 = dict(w1, w_out=w16[0], w_mlp1=w16[1], w_mlp2=w16[2])
    xp = _post(xp, na_p, mod1_p, tiles_p, w1, "l1_post_ctx")
    xs = _post(xs, na_s, mod1_s, tiles_s, w1, "l1_post_latent")

    return (xp.reshape(batch, seq, D_MODEL), xs.reshape(dec_batch, dec_seq, D_MODEL),
            ckv_p.reshape(batch, seq, KV_LORA), kr_p.reshape(batch, seq, MLA_ROPE),
            k1_new.reshape(batch, seq, NA_HEADS, NA_HEAD_DIM), v1_new.reshape(batch, seq, NA_HEADS, NA_HEAD_DIM))
```

```python
import functools

import jax
import jax.numpy as jnp
from jax import lax
from jax.experimental import pallas as pl
from jax.experimental.pallas import tpu as pltpu

F32 = jnp.float32
BF16 = jnp.bfloat16

D_MODEL = 1024
D_FF = 4 * D_MODEL
DEPTH = 2
N_ADA = 6
EPS = 1e-6
NEG_INF = -1e30
ROPE_THETA = 10000.0
GRID_W = 64
POOL_WIDTH = 512
POOL_WINDOWS = (2, 4, 8, 16)
POOL_GROUP = 128
POOL_HALO = 8
MLA_HEADS = 8
MLA_NOPE = 64
MLA_ROPE = 32
MLA_QK = MLA_NOPE + MLA_ROPE
MLA_V = 64
Q_LORA = 256
KV_LORA = 128
NA_HEADS = 16
NA_HEAD_DIM = 64
NA_KH = 8
NA_KW = 16
LOG2E = 1.4426950408889634

LANES = 128
SUBLANES = 8
VMEM_LIMIT = 56 * 1024 * 1024

TOKEN_TILE = 512
MLA_Q_TILE = 512
MLA_STEP_HEADS = 2
MLA_KEY_CHUNK = 2048
MLA_ONLINE_CHUNK = 1024
SCORE_LOOKAHEAD = 4
MAX_SCORE_BOUND = 40.0
FF_CHUNK = 1024
CTX_STEP_SEQS = 2
NA_SLABS = NA_HEADS * NA_HEAD_DIM // LANES
NA_TILE_ROWS = 4
NA_WIN_ROWS = NA_TILE_ROWS + NA_KH - 1


def _params(n_axes):
    return pltpu.CompilerParams(dimension_semantics=("arbitrary",) * n_axes,
                                vmem_limit_bytes=VMEM_LIMIT)


def _const_spec(shape):
    zeros = (0,) * len(shape)
    return pl.BlockSpec(shape, lambda *_: zeros, pipeline_mode=pl.Buffered(1))


def _dot(a, b):
    return jnp.dot(a, b, preferred_element_type=F32)


def _dot_nt(a, b):
    return lax.dot_general(a, b, (((1,), (1,)), ((), ())), preferred_element_type=F32)


def _dot_tn(a, b):
    return lax.dot_general(a, b, (((0,), (0,)), ((), ())), preferred_element_type=F32)


def _rms(x, w, n=None):
    n = x.shape[-1] if n is None else n
    ss = jnp.sum(x * x, axis=-1, keepdims=True)
    return x * lax.rsqrt(ss * (1.0 / n) + EPS) * w


def _modulate(x, nw, shift, scale):
    return _rms(x, nw) * (1.0 + scale) + shift


def _ada_spec(mod, tiles_per_group):
    mods, first_row = mod
    return pl.BlockSpec((None, 1, mods.shape[-1]), lambda i: (first_row + i // tiles_per_group, 0, 0))


def _ada_row(mod_ref, k):
    return mod_ref[:, k * D_MODEL:(k + 1) * D_MODEL]


def _ada_kernel(cond_ref, w0_ref, w1_ref, b_ref, o_ref):
    c = cond_ref[...]
    s = (c * jax.nn.sigmoid(c)).astype(BF16)
    for layer, w_ref in enumerate((w0_ref, w1_ref)):
        @pl.when(pl.program_id(0) == layer)
        def _(w_ref=w_ref):
            o_ref[...] = _dot(s, w_ref[...].astype(BF16)) + b_ref[...]


def _ada(cond8, w_ada, b_ada):
    n_out = w_ada[0].shape[1]
    tn = D_MODEL
    nb = n_out // tn
    return pl.pallas_call(
        _ada_kernel,
        out_shape=jax.ShapeDtypeStruct((DEPTH, SUBLANES, n_out), F32),
        grid=(DEPTH, nb),
        in_specs=[pl.BlockSpec((SUBLANES, D_MODEL), lambda l, j: (0, 0)),
                  pl.BlockSpec((D_MODEL, tn), lambda l, j: (0, jnp.where(l == 0, j, nb - 1))),
                  pl.BlockSpec((D_MODEL, tn), lambda l, j: (0, jnp.where(l == 1, j, 0))),
                  pl.BlockSpec((None, 1, tn), lambda l, j: (l, 0, j))],
        out_specs=pl.BlockSpec((None, SUBLANES, tn), lambda l, j: (l, 0, j)),
        compiler_params=_params(2),
        name="ada",
    )(cond8, w_ada[0], w_ada[1], jnp.stack(b_ada).reshape(DEPTH, 1, n_out))


def _rope(x, c, s):
    return x * c + pltpu.roll(x, LANES - MLA_ROPE, 1) * s


PAIR = 2 * LANES


def _head_rinv(raw, e2_ref):
    ss = _dot((raw * raw).astype(BF16), e2_ref[...])
    return lax.rsqrt(ss * (1.0 / MLA_QK) + EPS)


def _mla_keys_values(ckv, kr_slab, wk_ref, gk, wv_ref, e2_ref, rope_tabs, k_ref, v_ref):
    c16 = ckv.astype(BF16)
    tail = kr_slab * gk
    if rope_tabs is not None:
        tail = _rope(tail, *rope_tabs)
    ones_lane = (lax.broadcasted_iota(jnp.int32, (1, LANES), 1) == MLA_V).astype(F32)
    for p in range(MLA_HEADS // 2):
        cols = pl.ds(p * PAIR, PAIR)
        kn = _dot(c16, wk_ref[:, cols])
        v = _dot(c16, wv_ref[:, cols])
        rinv = _head_rinv(kn + jnp.tile(kr_slab, (1, 2)), e2_ref)
        for i in range(2):
            hs = slice(i * LANES, (i + 1) * LANES)
            k_ref[2 * p + i] = (rinv[:, hs] * (kn[:, hs] * gk + tail)).astype(BF16)
            v_ref[2 * p + i] = (v[:, hs] + ones_lane).astype(BF16)


def _l0_pre_kernel(latent, x_ref, mod_ref, nw_ref, win_ref, qln_ref, kvn_ref, wq_ref, gq_ref,
                   wk_ref, gk_ref, wv_ref, e2_ref, *rest):
    if latent:
        rope_ref, a_ref, q_ref, k_ref, v_ref, ckv_ref, kr_ref = rest
        tile_rows = x_ref.shape[0] // GRID_W
        row0 = (pl.program_id(0) * tile_rows) % rope_ref.shape[1]
        expand = lambda col, row: jnp.concatenate(
            [rope_ref[col] + rope_ref[row, pl.ds(row0 + r, 1), :] for r in range(tile_rows)], axis=0)
        rope_tabs = (expand(0, 1), expand(2, 3))
    else:
        a_ref, q_ref, k_ref, v_ref, ckv_ref, kr_ref = rest
        rope_tabs = None
    h = _modulate(x_ref[...], nw_ref[...], _ada_row(mod_ref, 0), _ada_row(mod_ref, 1)).astype(BF16)
    proj = _dot(h, win_ref[...])
    a_ref[...] = proj[:, :POOL_WIDTH]
    q_lat = proj[:, POOL_WIDTH:POOL_WIDTH + Q_LORA]
    kv_lat = proj[:, POOL_WIDTH + Q_LORA:POOL_WIDTH + Q_LORA + KV_LORA]
    kr_slab = proj[:, POOL_WIDTH + Q_LORA + KV_LORA:]
    ql = _rms(q_lat, qln_ref[...]).astype(BF16)
    gq = gq_ref[...]
    for p in range(MLA_HEADS // 2):
        q = _dot(ql, wq_ref[:, pl.ds(p * PAIR, PAIR)])
        q_rinv = _head_rinv(q, e2_ref)
        for i in range(2):
            hs = slice(i * LANES, (i + 1) * LANES)
            y = q[:, hs] * q_rinv[:, hs] * gq
            if rope_tabs is not None:
                y = _rope(y, *rope_tabs)
            q_ref[2 * p + i] = y.astype(BF16)
    ckv = _rms(kv_lat, kvn_ref[...])
    ckv_ref[...] = ckv
    kr_ref[...] = kr_slab[:, MLA_NOPE:MLA_QK]
    _mla_keys_values(ckv, kr_slab, wk_ref, gk_ref[...], wv_ref, e2_ref, rope_tabs, k_ref, v_ref)


def _slab_spec(slabs, tm):
    return pl.BlockSpec((slabs, tm, LANES), lambda i: (0, i, 0))


def _l0_pre(x, mod, tiles_per_group, latent, w, rope_tabs):
    n = x.shape[0]
    tm = TOKEN_TILE
    row = lambda i: (i, 0)
    hk = MLA_HEADS * LANES
    in_specs = [pl.BlockSpec((tm, D_MODEL), row),
                _ada_spec(mod, tiles_per_group),
                _const_spec((1, D_MODEL)),
                _const_spec((D_MODEL, D_MODEL)),
                _const_spec((1, Q_LORA)),
                _const_spec((1, KV_LORA)),
                _const_spec((Q_LORA, hk)),
                _const_spec((1, LANES)),
                _const_spec((KV_LORA, hk)),
                _const_spec((1, LANES)),
                _const_spec((KV_LORA, hk)),
                _const_spec((2 * LANES, 2 * LANES))]
    gains = "rope" if latent else "plain"
    args = [x, mod[0], w["norm_mix"], w["w_in"], w["qln"], w["kvn"], w["wq"], w["gq_" + gains], w["wk"],
            w["gk_" + gains], w["wv"], w["e2"]]
    if latent:
        assert tm % GRID_W == 0
        in_specs.append(_const_spec(rope_tabs.shape))
        args.append(rope_tabs)
    slab = jax.ShapeDtypeStruct((MLA_HEADS, n, LANES), BF16)
    out_shape = (jax.ShapeDtypeStruct((n, POOL_WIDTH), F32), slab, slab, slab,
                 jax.ShapeDtypeStruct((n, KV_LORA), F32),
                 jax.ShapeDtypeStruct((n, MLA_ROPE), F32))
    out_specs = (pl.BlockSpec((tm, POOL_WIDTH), row),
                 _slab_spec(MLA_HEADS, tm), _slab_spec(MLA_HEADS, tm), _slab_spec(MLA_HEADS, tm),
                 pl.BlockSpec((tm, KV_LORA), row), pl.BlockSpec((tm, MLA_ROPE), row))
    return pl.pallas_call(
        functools.partial(_l0_pre_kernel, latent),
        out_shape=out_shape, grid=(n // tm,), in_specs=in_specs, out_specs=out_specs,
        compiler_params=_params(1), name="l0_pre_latent" if latent else "l0_pre_ctx",
    )(*args)


def _cache_kv_kernel(ckv_ref, kr_ref, wk_ref, gk_ref, wv_ref, e2_ref, k_ref, v_ref):
    _mla_keys_values(ckv_ref[...], kr_ref[...], wk_ref, gk_ref[...], wv_ref, e2_ref, None, k_ref, v_ref)


def _cache_kv(ckv, kr_slab, w):
    n = ckv.shape[0]
    tm = TOKEN_TILE
    row = lambda i: (i, 0)
    hk = MLA_HEADS * LANES
    slab = jax.ShapeDtypeStruct((MLA_HEADS, n, LANES), BF16)
    return pl.pallas_call(
        _cache_kv_kernel,
        out_shape=(slab, slab),
        grid=(n // tm,),
        in_specs=[pl.BlockSpec((tm, KV_LORA), row), pl.BlockSpec((tm, LANES), row),
                  _const_spec((KV_LORA, hk)), _const_spec((1, LANES)), _const_spec((KV_LORA, hk)),
                  _const_spec((2 * LANES, 2 * LANES))],
        out_specs=(_slab_spec(MLA_HEADS, tm), _slab_spec(MLA_HEADS, tm)),
        compiler_params=_params(1), name="l0_cache_kv",
    )(ckv, kr_slab, w["wk"], w["gk_plain"], w["wv"], w["e2"])


def _pool_mixer(seq_len, tile, prev_ref, u_ref, next_ref, wp_ref, ps_ref, buf_ref):
    tm = u_ref.shape[0]
    piece = min(tm, seq_len)
    n_pieces = tm // piece
    stride = piece + 2 * POOL_HALO
    starts = []
    for j in range(n_pieces):
        pos0 = (tile * tm + j * piece) % seq_len
        prev = prev_ref[...] if j == 0 else u_ref[j * piece - POOL_HALO:j * piece, :]
        nxt = next_ref[...] if j == n_pieces - 1 else u_ref[(j + 1) * piece:(j + 1) * piece + POOL_HALO, :]
        base = j * stride
        buf_ref[base:base + POOL_HALO, :] = jnp.where(pos0 != 0, prev, 0.0)
        buf_ref[base + POOL_HALO:base + POOL_HALO + piece, :] = u_ref[j * piece:(j + 1) * piece, :]
        buf_ref[base + POOL_HALO + piece:base + stride, :] = jnp.where(pos0 + piece != seq_len, nxt, 0.0)
        starts.append(pos0)
    def group(g):
        win = POOL_WINDOWS[g]
        lanes = pl.ds(g * POOL_GROUP, POOL_GROUP)
        pooled = []
        for j in range(n_pieces):
            centre = j * stride + POOL_HALO
            lo = starts[j] + lax.broadcasted_iota(jnp.int32, (piece, 1), 0) - win // 2
            cnt = (jnp.minimum(lo + win, seq_len) - jnp.maximum(lo, 0)).astype(F32)
            acc = buf_ref[pl.ds(centre - win // 2, piece), lanes]
            for d in range(1 - win // 2, win - win // 2):
                acc = acc + buf_ref[pl.ds(centre + d, piece), lanes]
            pooled.append(acc / cnt - buf_ref[pl.ds(centre, piece), lanes])
        pooled = pooled[0] if n_pieces == 1 else jnp.concatenate(pooled, axis=0)
        return (_dot(pooled.astype(BF16), wp_ref[g]) * ps_ref[:, lanes]).astype(BF16)

    return [functools.partial(group, g) for g in range(len(POOL_WINDOWS))]


def _attend(streams, running_max):
    items = [(si, ch) for si, (_, chunks, _) in enumerate(streams) for ch in chunks]

    def score(item):
        si, (k_fn, _, extra_fn) = item
        s = _dot_nt(k_fn(), streams[si][0])
        return s if extra_fn is None else s + extra_fn()

    state = [None] * len(streams)
    pending = [score(item) for item in items[:SCORE_LOOKAHEAD]]
    for i, (si, (_, v_fn, _)) in enumerate(items):
        s = pending.pop(0)
        if i + SCORE_LOOKAHEAD < len(items):
            pending.append(score(items[i + SCORE_LOOKAHEAD]))
        want_den = streams[si][2]
        m_new, alpha = None, 1.0
        if state[si] is not None:
            m, acc, den = state[si]
        if running_max:
            m_new = jnp.max(s, axis=0, keepdims=True)
            if state[si] is not None:
                m_new = jnp.maximum(m, m_new)
                alpha = jnp.exp2(m - m_new)
            s = s - m_new
        p = jnp.exp2(s)
        pv = _dot_tn(v_fn(), p.astype(BF16))
        psum = jnp.sum(p, axis=0, keepdims=True) if want_den else None
        if state[si] is None:
            state[si] = (m_new, pv, psum)
        elif running_max:
            state[si] = (m_new, acc * alpha + pv, den * alpha + psum if want_den else None)
        else:
            state[si] = (None, acc + pv, den + psum if want_den else None)
    return [(acc, den) for _, acc, den in state]


def _ref_chunks(k_ref, v_ref, slab, chunk, extra_fn=None, lo=0, hi=None):
    hi = k_ref.shape[1] if hi is None else hi
    out = []
    for c in range(lo, hi, chunk):
        rows = pl.ds(c, min(chunk, hi - c))
        out.append((lambda rows=rows: k_ref[slab, rows, :], lambda rows=rows: v_ref[slab, rows, :], extra_fn))
    return out


def _mla_finish(o_t):
    return o_t[:MLA_V] / o_t[MLA_V:MLA_V + 1]


def _na_queries(q2):
    low = lax.broadcasted_iota(jnp.int32, (1, LANES), 1) < NA_HEAD_DIM
    zero = jnp.zeros_like(q2)
    return jnp.concatenate([jnp.where(low, q2, zero), jnp.where(low, zero, q2)], axis=0)


def _na_finish(o_t, den):
    nq = o_t.shape[1] // 2
    o = o_t / den
    return jnp.concatenate([o[:NA_HEAD_DIM, :nq], o[NA_HEAD_DIM:, nq:]], axis=0)


def _attend_guarded(bound, streams_fn):
    return lax.cond(bound <= MAX_SCORE_BOUND,
                    lambda: _attend(streams_fn(False), False),
                    lambda: _attend(streams_fn(True), True))


def _mla_ctx_kernel(seq, bound_ref, q_ref, k_ref, v_ref, o_ref):
    spans = [(lo, lo + seq) for lo in range(0, k_ref.shape[1], seq)]
    res = _attend_guarded(bound_ref[0, 0], lambda _: [
        (q_ref[h, lo:hi, :], _ref_chunks(k_ref, v_ref, h, seq, lo=lo, hi=hi), False)
        for lo, hi in spans for h in range(MLA_HEADS)])
    for i, (lo, hi) in enumerate(spans):
        for hp in range(MLA_HEADS // 2):
            o_t = jnp.concatenate([_mla_finish(res[i * MLA_HEADS + h][0]) for h in (2 * hp, 2 * hp + 1)], axis=0)
            o_ref[lo:hi, hp * LANES:(hp + 1) * LANES] = o_t.T.astype(BF16)


def _na_ctx_kernel(seq, bound_ref, q_ref, k_ref, v_ref, o_ref):
    spans = [(lo, lo + seq) for lo in range(0, k_ref.shape[1], seq)]
    res = _attend_guarded(bound_ref[0, 0], lambda _: [
        (_na_queries(q_ref[s, lo:hi, :]), _ref_chunks(k_ref, v_ref, s, seq, lo=lo, hi=hi), True)
        for lo, hi in spans for s in range(NA_SLABS)])
    for i, (lo, hi) in enumerate(spans):
        for s in range(NA_SLABS):
            o_ref[lo:hi, s * LANES:(s + 1) * LANES] = _na_finish(*res[i * NA_SLABS + s]).T.astype(BF16)


def _ctx_attn(body, q, k, v, bound, seq, width, name):
    slabs, n, _ = q.shape
    rows = CTX_STEP_SEQS * seq
    spec = pl.BlockSpec((slabs, rows, LANES), lambda b: (0, b, 0))
    return pl.pallas_call(
        functools.partial(body, seq),
        out_shape=jax.ShapeDtypeStruct((n, width), BF16),
        grid=(n // rows,),
        in_specs=[pl.BlockSpec(memory_space=pltpu.SMEM), spec, spec, spec],
        out_specs=pl.BlockSpec((rows, width), lambda b: (b, 0)),
        compiler_params=_params(1), name=name,
    )(bound.reshape(1, 1), q, k, v)


SIDE_PIECE_ELEMS = 512 * 1024


def _side_piece_rows(n_cols):
    return 1 << ((SIDE_PIECE_ELEMS // n_cols).bit_length() - 1)


def _side_cast_plan(shapes):
    plan = []
    for wi, (n_rows, n_cols) in enumerate(shapes):
        rows = _side_piece_rows(n_cols)
        assert n_rows % rows == 0
        plan += [(wi, r0, rows) for r0 in range(0, n_rows, rows)]
    return plan


def _side_cast_scratch(shapes):
    widths = sorted({c for _, c in shapes})
    stages = [pltpu.VMEM((2, _side_piece_rows(c), c), dt) for dt in (F32, BF16) for c in widths]
    return stages + [pltpu.SemaphoreType.DMA((2,)), pltpu.SemaphoreType.DMA((2,))]


def _side_cast(step, srcs, dsts, scratch):
    shapes = [s.shape for s in srcs]
    widths = sorted({c for _, c in shapes})
    stage_in = dict(zip(widths, scratch[:len(widths)]))
    stage_out = dict(zip(widths, scratch[len(widths):2 * len(widths)]))
    sem_in, sem_out = scratch[2 * len(widths):]
    plan = _side_cast_plan(shapes)

    def refs(k):
        wi, r0, rows = plan[k]
        cols = shapes[wi][1]
        return (srcs[wi].at[pl.ds(r0, rows), :], dsts[wi].at[pl.ds(r0, rows), :],
                stage_in[cols].at[k % 2], stage_out[cols].at[k % 2])

    read = lambda k: pltpu.make_async_copy(refs(k)[0], refs(k)[2], sem_in.at[k % 2])
    write = lambda k: pltpu.make_async_copy(refs(k)[3], refs(k)[1], sem_out.at[k % 2])
    n = len(plan)

    def piece_step(k):
        if k == 0:
            read(0).start()
        if k >= 2:
            write(k - 2).wait()
        if k < n:
            read(k).wait()
            if k + 1 < n:
                read(k + 1).start()
            refs(k)[3][...] = refs(k)[2][...].astype(BF16)
            write(k).start()

    @pl.when(step < n + 2)
    def _():
        for k in range(n + 2):
            pl.when(step == k)(functools.partial(piece_step, k))


def _mla_lat_kernel(n_cast, bound_ref, q_ref, kl_ref, vl_ref, kc_ref, vc_ref, *rest):
    o_ref = rest[n_cast]
    linear_step = (pl.program_id(0) * pl.num_programs(1) + pl.program_id(1)) * pl.num_programs(2) + pl.program_id(2)
    _side_cast(linear_step, rest[:n_cast], rest[n_cast + 1:2 * n_cast + 1], rest[2 * n_cast + 1:])
    batch, head0 = pl.program_id(0), pl.program_id(1) * MLA_STEP_HEADS
    heads = range(MLA_STEP_HEADS)
    bounds = [bound_ref[batch, head0 + h] for h in heads]

    def streams(running_max):
        size = MLA_ONLINE_CHUNK if running_max else MLA_KEY_CHUNK
        return [(q_ref[h], _ref_chunks(kl_ref, vl_ref, h, size) + _ref_chunks(kc_ref, vc_ref, h, size), False)
                for h in heads]

    res = _attend_guarded(functools.reduce(jnp.maximum, bounds), streams)
    o_ref[...] = jnp.concatenate([_mla_finish(o) for o, _ in res], axis=0).T.astype(BF16)


def _normed_len(gain, dim):
    return (dim ** 0.5) * jnp.max(jnp.abs(gain.astype(F32)))


def _side_cast_specs(weights, n_steps):
    shapes = [wt.shape for wt in weights]
    assert n_steps >= len(_side_cast_plan(shapes)) + 2
    hbm = pl.BlockSpec(memory_space=pl.ANY)
    return ([hbm] * len(weights), [jax.ShapeDtypeStruct(s, BF16) for s in shapes], [hbm] * len(weights),
            _side_cast_scratch(shapes))


def _mla_lat_attn(q, k, v, kc, vc, bound, seq, past, weights):
    n = q.shape[1]
    tq = MLA_Q_TILE
    nq = seq // tq
    hs = MLA_STEP_HEADS
    grid = (n // seq, MLA_HEADS // hs, nq)
    w_in, w_shapes, w_out, w_scratch = _side_cast_specs(weights, grid[0] * grid[1] * grid[2])
    outs = pl.pallas_call(
        functools.partial(_mla_lat_kernel, len(weights)),
        out_shape=(jax.ShapeDtypeStruct((n, MLA_HEADS * MLA_V), BF16), *w_shapes),
        grid=grid,
        in_specs=[pl.BlockSpec(memory_space=pltpu.SMEM),
                  pl.BlockSpec((hs, tq, LANES), lambda b, h, i: (h, b * nq + i, 0)),
                  pl.BlockSpec((hs, seq, LANES), lambda b, h, i: (h, b, 0)),
                  pl.BlockSpec((hs, seq, LANES), lambda b, h, i: (h, b, 0)),
                  pl.BlockSpec((hs, past, LANES), lambda b, h, i: (h, b, 0)),
                  pl.BlockSpec((hs, past, LANES), lambda b, h, i: (h, b, 0))] + w_in,
        out_specs=(pl.BlockSpec((tq, hs * MLA_V), lambda b, h, i: (b * nq + i, h)), *w_out),
        scratch_shapes=w_scratch,
        compiler_params=_params(3), name="l0_mla_latent_attn",
    )(bound, q, k, v, kc, vc, *weights)
    return outs[0], outs[1:]


def _na_first_row(r, rows):
    return jnp.clip(r - NA_KH // 2, 0, rows - NA_KH)


def _side_relayout(step, n_steps, srcs, dsts, sems):
    jobs = [(a, h) for a in range(len(srcs)) for h in range(NA_HEADS)]
    assert n_steps >= len(jobs)

    def copy(j):
        a, h = jobs[j]
        return pltpu.make_async_copy(srcs[a].at[h], dsts[a].at[:, h, :], sems.at[j])

    @pl.when(step < len(jobs))
    def _():
        for j in range(len(jobs)):
            pl.when(step == j)(lambda j=j: copy(j).start())

    @pl.when(step == n_steps - 1)
    def _():
        for j in range(len(jobs)):
            copy(j).wait()


def _na_lat_kernel(rows, n_cast, n_relay, n_steps, bound_ref, q_ref, k_ref, v_ref, kc_ref, vc_ref, t2_ref, *rest):
    n_in = n_cast + n_relay
    o_ref = rest[n_in]
    outs = rest[n_in + 1:2 * n_in + 1]
    scratch = rest[2 * n_in + 1:]
    step = pl.program_id(0) * pl.num_programs(1) + pl.program_id(1)
    _side_cast(step, rest[:n_cast], outs[:n_cast], scratch[:-1])
    _side_relayout(step, n_steps, rest[n_cast:n_in], outs[n_cast:], scratch[-1])
    nq = NA_TILE_ROWS * GRID_W
    nk = NA_WIN_ROWS * GRID_W
    n_off = 2 * NA_KH - 1
    r0 = pl.program_id(1) * NA_TILE_ROWS
    w0 = jnp.clip(r0 - NA_KH // 2, 0, rows - NA_WIN_ROWS)

    def block_index(j, t):
        key_row = w0 + j
        i = r0 + 2 * t
        first_u, first_l = _na_first_row(i, rows), _na_first_row(i + 1, rows)
        in_u = (key_row >= first_u) & (key_row < first_u + NA_KH)
        in_l = (key_row >= first_l) & (key_row < first_l + NA_KH)
        d = jnp.clip(key_row - i + NA_KH - 1, 0, n_off - 1)
        return jnp.where(in_u & in_l, d, jnp.where(in_u, n_off, jnp.where(in_l, n_off + 1, n_off + 2)))

    idx = [[block_index(j, t) for t in range(NA_TILE_ROWS // 2)] for j in range(NA_WIN_ROWS)]

    def local_bias(s):
        heads = [jnp.concatenate([jnp.concatenate([t2_ref[2 * s + hh, i] for i in row], axis=1) for row in idx], axis=0)
                 for hh in range(2)]
        return jnp.concatenate(heads, axis=1)

    batch = pl.program_id(0)
    bounds = [bound_ref[batch, h] for h in range(NA_HEADS)]

    def cache_chunk(s):
        lanes = pl.ds(s * LANES, LANES)
        return [(lambda: kc_ref[:, lanes].astype(BF16), lambda: vc_ref[:, lanes].astype(BF16), None)]

    res = _attend_guarded(
        functools.reduce(jnp.maximum, bounds),
        lambda _: [(_na_queries(q_ref[s]),
                    _ref_chunks(k_ref, v_ref, s, nk, functools.partial(local_bias, s)) + cache_chunk(s),
                    True) for s in range(NA_SLABS)])
    for s in range(NA_SLABS):
        o_ref[:, s * LANES:(s + 1) * LANES] = _na_finish(*res[s]).T.astype(BF16)


def _na_lat_attn(q, k, v, kc, vc, t2, bound, seq, past, weights, flat_caches):
    slabs, n, _ = q.shape
    rows = seq // GRID_W
    tiles = rows // NA_TILE_ROWS
    nq = NA_TILE_ROWS * GRID_W
    nk = NA_WIN_ROWS * GRID_W

    def win_map(b, t):
        w0 = jnp.clip(t * NA_TILE_ROWS - NA_KH // 2, 0, rows - NA_WIN_ROWS)
        return (0, pl.multiple_of(b * seq + w0 * GRID_W, GRID_W), 0)

    win_spec = pl.BlockSpec((pl.Element(slabs), pl.Element(nk), pl.Element(LANES)), win_map)
    ctx_spec = pl.BlockSpec((past, slabs * LANES), lambda b, t: (b, 0))
    n_steps = (n // seq) * tiles
    w_in, w_shapes, w_out, w_scratch = _side_cast_specs(weights, n_steps)
    hbm = pl.BlockSpec(memory_space=pl.ANY)
    c_shapes = [jax.ShapeDtypeStruct((c.shape[1], NA_HEADS, NA_HEAD_DIM), c.dtype) for c in flat_caches]
    outs = pl.pallas_call(
        functools.partial(_na_lat_kernel, rows, len(weights), len(flat_caches), n_steps),
        out_shape=(jax.ShapeDtypeStruct((n, slabs * LANES), BF16), *w_shapes, *c_shapes),
        grid=(n // seq, tiles),
        in_specs=[pl.BlockSpec(memory_space=pltpu.SMEM),
                  pl.BlockSpec((slabs, nq, LANES), lambda b, t: (0, b * tiles + t, 0)),
                  win_spec, win_spec, ctx_spec, ctx_spec,
                  _const_spec(t2.shape)] + w_in + [hbm] * len(flat_caches),
        out_specs=(pl.BlockSpec((nq, slabs * LANES), lambda b, t: (b * tiles + t, 0)), *w_out,
                   *[hbm] * len(flat_caches)),
        scratch_shapes=w_scratch + [pltpu.SemaphoreType.DMA((len(flat_caches) * NA_HEADS,))],
        compiler_params=_params(2), name="l1_na_latent_attn",
    )(bound, q, k, v, kc, vc, t2, *weights, *flat_caches)
    n_w = len(weights)
    return outs[0], outs[1:1 + n_w], outs[1 + n_w:]


def _pair_norm(x2, g2):
    low = lax.broadcasted_iota(jnp.int32, (1, LANES), 1) < NA_HEAD_DIM
    sq = x2 * x2
    ss_all = jnp.sum(sq, axis=-1, keepdims=True)
    ss_low = jnp.sum(jnp.where(low, sq, 0.0), axis=-1, keepdims=True)
    ss = jnp.where(low, ss_low, ss_all - ss_low)
    return x2 * lax.rsqrt(ss * (1.0 / NA_HEAD_DIM) + EPS) * g2


def _l1_pre_kernel(keep_f32, x_ref, mod_ref, nw_ref, win_ref, gq_ref, gk_ref, q_ref, k_ref, v_ref, *f32_refs):
    h = _modulate(x_ref[...], nw_ref[...], _ada_row(mod_ref, 0), _ada_row(mod_ref, 1)).astype(BF16)
    width = NA_SLABS * LANES
    q = _dot(h, win_ref[:, 0:width])
    k = _dot(h, win_ref[:, width:2 * width])
    v = _dot(h, win_ref[:, 2 * width:3 * width])
    gq = gq_ref[...]
    gk = gk_ref[...]
    for s in range(NA_SLABS):
        ls = slice(s * LANES, (s + 1) * LANES)
        q_ref[s] = _pair_norm(q[:, ls], gq).astype(BF16)
        kn = _pair_norm(k[:, ls], gk)
        k_ref[s] = kn.astype(BF16)
        v_ref[s] = v[:, ls].astype(BF16)
        if keep_f32:
            for hh in range(2):
                hl = slice(hh * NA_HEAD_DIM, (hh + 1) * NA_HEAD_DIM)
                f32_refs[0][2 * s + hh] = kn[:, hl]
                f32_refs[1][2 * s + hh] = v[:, ls][:, hl]


def _l1_pre(x, mod, tiles_per_group, keep_f32, w):
    n = x.shape[0]
    tm = TOKEN_TILE
    row = lambda i: (i, 0)
    width = NA_SLABS * LANES
    slab = jax.ShapeDtypeStruct((NA_SLABS, n, LANES), BF16)
    out_shape = [slab, slab, slab]
    out_specs = [_slab_spec(NA_SLABS, tm)] * 3
    if keep_f32:
        out_shape += [jax.ShapeDtypeStruct((NA_HEADS, n, NA_HEAD_DIM), F32)] * 2
        out_specs += [pl.BlockSpec((NA_HEADS, tm, NA_HEAD_DIM), lambda i: (0, i, 0))] * 2
    return pl.pallas_call(
        functools.partial(_l1_pre_kernel, keep_f32),
        out_shape=tuple(out_shape), grid=(n // tm,),
        in_specs=[pl.BlockSpec((tm, D_MODEL), row),
                  _ada_spec(mod, tiles_per_group),
                  _const_spec((1, D_MODEL)),
                  _const_spec((D_MODEL, 3 * width)),
                  _const_spec((1, LANES)),
                  _const_spec((1, LANES))],
        out_specs=tuple(out_specs),
        compiler_params=_params(1), name="l1_pre_ctx" if keep_f32 else "l1_pre_latent",
    )(x, mod[0], w["norm_mix"], w["w_in"], w["gq"], w["gk"])


def _post_kernel(pool_seq, x_ref, attn_ref, *rest):
    rest = list(rest)
    take = lambda k: [rest.pop(0) for _ in range(k)]
    pool_in = take(8) if pool_seq is not None else None
    mod_ref, nw_ref, wo_ref, w1_ref, w2_ref, o_ref = take(6)
    pool_scratch = take(2) if pool_seq is not None else None
    step = pl.program_id(0)
    n_chunks = D_FF // FF_CHUNK
    if pool_seq is not None:
        first_refs, next_refs, (wp_ref, ps_ref) = pool_in[0:3], pool_in[3:6], pool_in[6:8]
        buf_ref, pooled_ref = pool_scratch

        @pl.when(step == 0)
        def _():
            for g, group in enumerate(_pool_mixer(pool_seq, 0, *first_refs, wp_ref, ps_ref, buf_ref)):
                pooled_ref[:, g * POOL_GROUP:(g + 1) * POOL_GROUP] = group()

        mixed = _dot(jnp.concatenate([pooled_ref[...], attn_ref[...]], axis=-1), wo_ref[...])
        next_groups = _pool_mixer(pool_seq, jnp.minimum(step + 1, pl.num_programs(0) - 1), *next_refs,
                                  wp_ref, ps_ref, buf_ref)
    else:
        mixed = _dot(attn_ref[...], wo_ref[...])
    x1 = x_ref[...] + _ada_row(mod_ref, 2) * mixed
    h = _modulate(x1, nw_ref[...], _ada_row(mod_ref, 3), _ada_row(mod_ref, 4)).astype(BF16)
    acc = jnp.zeros(x1.shape, F32)
    for c in range(n_chunks):
        cs = pl.ds(c * FF_CHUNK, FF_CHUNK)
        u = jnp.square(jnp.maximum(_dot(h, w1_ref[:, cs]), 0.0)).astype(BF16)
        acc = acc + _dot(u, w2_ref[cs, :])
        if pool_seq is not None:
            for g in range(c * len(next_groups) // n_chunks, (c + 1) * len(next_groups) // n_chunks):
                pooled_ref[:, g * POOL_GROUP:(g + 1) * POOL_GROUP] = next_groups[g]()
    o_ref[...] = x1 + _ada_row(mod_ref, 5) * acc


def _post(x, attn, mod, tiles_per_group, w, name, pool=None):
    n = x.shape[0]
    tm = TOKEN_TILE
    row = lambda i: (i, 0)
    in_specs = [pl.BlockSpec((tm, D_MODEL), row), pl.BlockSpec((tm, attn.shape[1]), row)]
    args = [x, attn]
    scratch = []
    pool_seq = None
    if pool is not None:
        a_in, pool_seq = pool
        hb = tm // POOL_HALO
        last = n // POOL_HALO - 1
        pieces = tm // min(tm, pool_seq)
        last_tile = n // tm - 1

        def tile_specs(tile_of):
            return [pl.BlockSpec((POOL_HALO, POOL_WIDTH), lambda i: (jnp.maximum(tile_of(i) * hb - 1, 0), 0)),
                    pl.BlockSpec((tm, POOL_WIDTH), lambda i: (tile_of(i), 0)),
                    pl.BlockSpec((POOL_HALO, POOL_WIDTH), lambda i: (jnp.minimum((tile_of(i) + 1) * hb, last), 0))]

        in_specs += tile_specs(lambda i: 0) + tile_specs(lambda i: jnp.minimum(i + 1, last_tile))
        in_specs += [_const_spec((len(POOL_WINDOWS), POOL_GROUP, POOL_GROUP)), _const_spec((1, POOL_WIDTH))]
        args += [a_in] * 6 + [w["w_pool"], w["pool_scale"]]
        scratch = [pltpu.VMEM((tm + 2 * POOL_HALO * pieces, POOL_WIDTH), F32), pltpu.VMEM((tm, POOL_WIDTH), BF16)]
    in_specs += [_ada_spec(mod, tiles_per_group),
                 _const_spec((1, D_MODEL)),
                 _const_spec((D_MODEL, D_MODEL)),
                 _const_spec((D_MODEL, D_FF)),
                 _const_spec((D_FF, D_MODEL))]
    args += [mod[0], w["norm_mlp"], w["w_out"], w["w_mlp1"], w["w_mlp2"]]
    return pl.pallas_call(
        functools.partial(_post_kernel, pool_seq),
        out_shape=jax.ShapeDtypeStruct((n, D_MODEL), F32), grid=(n // tm,),
        in_specs=in_specs, out_specs=pl.BlockSpec((tm, D_MODEL), row),
        scratch_shapes=scratch, compiler_params=_params(1), name=name,
    )(*args)


def _rope_tables(seq):
    assert seq == GRID_W * GRID_W
    half = MLA_ROPE // 2
    inv_freq = jnp.power(ROPE_THETA, -jnp.arange(0, half, 2, dtype=F32) / half)
    ang = jnp.arange(GRID_W, dtype=F32)[:, None] * inv_freq[None, :]
    cos, sin = jnp.cos(ang), jnp.sin(ang)
    fill = lambda n, v=0.0: jnp.full((GRID_W, n), v, F32)
    spare = LANES - MLA_QK
    return jnp.stack([
        jnp.concatenate([fill(MLA_NOPE, 1.0), fill(half), cos, cos, fill(spare)], axis=1),
        jnp.concatenate([fill(MLA_NOPE), cos, cos, fill(half), fill(spare)], axis=1),
        jnp.concatenate([fill(MLA_NOPE), fill(half), -sin, sin, fill(spare)], axis=1),
        jnp.concatenate([fill(MLA_NOPE), -sin, sin, fill(half), fill(spare)], axis=1)])


def _rope_partner(tail):
    shape = tail.shape
    t = tail.reshape(shape[:-1] + (2, 2, MLA_ROPE // 4))
    return t[..., ::-1, :].reshape(shape)


def _na_bias_table(rel_bias, rows):
    qcol = jnp.arange(GRID_W)
    kcol = jnp.arange(GRID_W)
    col_start = jnp.clip(qcol - NA_KW // 2, 0, GRID_W - NA_KW)
    col_mask = (kcol[:, None] >= col_start[None, :]) & (kcol[:, None] < col_start[None, :] + NA_KW)
    offs = jnp.arange(2 * NA_KW - 1)
    onehot = (kcol[None, :, None] - qcol[None, None, :] + NA_KW - 1 == offs[:, None, None]).astype(F32)
    blocks = jnp.einsum('hdj,jkq->hdkq', rel_bias.astype(F32) * LOG2E, onehot, precision=lax.Precision.HIGHEST)
    blocks = jnp.where(col_mask[None, None], blocks, NEG_INF)
    neg = jnp.full_like(blocks[:, :1], NEG_INF)
    both = jnp.concatenate([blocks, jnp.concatenate([neg, blocks[:, :-1]], axis=1)], axis=-1)
    d_upper, d_lower = _na_partial_offsets(rows)
    upper_only = jnp.concatenate([blocks[:, d_upper:d_upper + 1], neg], axis=-1)
    lower_only = jnp.concatenate([neg, blocks[:, d_lower - 1:d_lower]], axis=-1)
    return jnp.concatenate([both, upper_only, lower_only, jnp.concatenate([neg, neg], axis=-1)], axis=1)


def _na_partial_offsets(rows):
    first = lambda r: min(max(r - NA_KH // 2, 0), rows - NA_KH)
    upper, lower = set(), set()
    for r0 in range(0, rows, NA_TILE_ROWS):
        w0 = min(max(r0 - NA_KH // 2, 0), rows - NA_WIN_ROWS)
        for j in range(NA_WIN_ROWS):
            for t in range(NA_TILE_ROWS // 2):
                key_row, i = w0 + j, r0 + 2 * t
                in_u = first(i) <= key_row < first(i) + NA_KH
                in_l = first(i + 1) <= key_row < first(i + 1) + NA_KH
                d = key_row - i + NA_KH - 1
                if in_u and in_l:
                    assert 1 <= d <= 2 * NA_KH - 2
                elif in_u:
                    upper.add(d)
                elif in_l:
                    lower.add(d)
    assert len(upper) == 1 and len(lower) == 1, (upper, lower)
    return upper.pop(), lower.pop()


def _prep_l0(w_in, q_lora_norm, kv_lora_norm, w_q_up, w_kv_up, mla_q_norm, mla_k_norm, w_pool, pool_scale,
             w_out, norm_mix, norm_mlp, w_mlp1, w_mlp2):
    lat_end = POOL_WIDTH + Q_LORA + KV_LORA
    w_in_pad = jnp.concatenate([w_in[:, :lat_end], jnp.zeros((D_MODEL, MLA_NOPE), w_in.dtype),
                                w_in[:, lat_end:], _rope_partner(w_in[:, lat_end:])], axis=1)
    pad = LANES - MLA_QK
    hk = MLA_HEADS * LANES
    wq = w_q_up.reshape(Q_LORA, MLA_HEADS, MLA_QK)
    wq = jnp.concatenate([wq, _rope_partner(wq[:, :, MLA_NOPE:])], axis=-1)
    kv = w_kv_up.reshape(KV_LORA, MLA_HEADS, MLA_NOPE + MLA_V)
    wk = jnp.pad(kv[:, :, :MLA_NOPE], ((0, 0), (0, 0), (0, LANES - MLA_NOPE)))
    wv = jnp.pad(kv[:, :, MLA_NOPE:], ((0, 0), (0, 0), (0, LANES - MLA_V)))
    gq = mla_q_norm * (MLA_QK ** -0.5 * LOG2E)
    slab_lane = jnp.arange(2 * LANES)
    e2 = (slab_lane[:, None] // LANES == slab_lane[None, :] // LANES) & (slab_lane[:, None] % LANES < MLA_QK)
    return dict(
        norm_mix=norm_mix.reshape(1, -1), norm_mlp=norm_mlp.reshape(1, -1),
        w_in=w_in_pad.astype(BF16),
        qln=q_lora_norm.reshape(1, -1), kvn=kv_lora_norm.reshape(1, -1),
        wq=wq.reshape(Q_LORA, hk).astype(BF16),
        gq_plain=jnp.pad(gq, (0, pad)).reshape(1, LANES),
        gq_rope=jnp.concatenate([gq, _rope_partner(gq[MLA_NOPE:])]).reshape(1, LANES),
        wk=wk.reshape(KV_LORA, hk).astype(BF16),
        gk_plain=jnp.pad(mla_k_norm, (0, pad)).reshape(1, LANES),
        gk_rope=jnp.concatenate([mla_k_norm, _rope_partner(mla_k_norm[MLA_NOPE:])]).reshape(1, LANES),
        wv=wv.reshape(KV_LORA, hk).astype(BF16),
        e2=e2.astype(BF16),
        w_pool=w_pool.astype(BF16), pool_scale=pool_scale.reshape(1, -1),
        w_out=w_out, w_mlp1=w_mlp1, w_mlp2=w_mlp2)


def kernel(x_prompt, x_sample, cache_l0_mla_ckv, cache_l0_mla_krope, cache_l1_na_k, cache_l1_na_v, c, c_ctx, w_ada_l0, b_ada_l0, norm_mix_l0, norm_mlp_l0, w_mlp1_l0, w_mlp2_l0, w_in_l0, q_lora_norm_l0, kv_lora_norm_l0, w_q_up_l0, w_kv_up_l0, mla_q_norm_l0, mla_k_norm_l0, w_pool_l0, pool_scale_l0, w_out_l0, w_ada_l1, b_ada_l1, norm_mix_l1, norm_mlp_l1, w_mlp1_l1, w_mlp2_l1, w_in_l1, na_q_norm_l1, na_k_norm_l1, rel_bias_l1, w_out_l1):
    batch, seq, _ = x_prompt.shape
    dec_batch, dec_seq, _ = x_sample.shape
    past = cache_l0_mla_ckv.shape[1]
    xp = x_prompt.reshape(batch * seq, D_MODEL)
    xs = x_sample.reshape(dec_batch * dec_seq, D_MODEL)
    tiles_p = (batch * seq) // TOKEN_TILE
    tiles_s = dec_seq // TOKEN_TILE

    cond8 = jnp.concatenate([c_ctx[None, :], c, jnp.zeros((SUBLANES - 1 - dec_batch, D_MODEL), F32)], axis=0)
    mods = _ada(cond8, (w_ada_l0, w_ada_l1), (b_ada_l0, b_ada_l1))
    mods = mods.reshape(DEPTH, SUBLANES, 1, N_ADA * D_MODEL)
    (mod0_p, mod0_s), (mod1_p, mod1_s) = (((mods[l], 0), (mods[l], 1)) for l in range(DEPTH))

    w0 = _prep_l0(w_in_l0, q_lora_norm_l0, kv_lora_norm_l0, w_q_up_l0, w_kv_up_l0, mla_q_norm_l0, mla_k_norm_l0,
                  w_pool_l0, pool_scale_l0, w_out_l0, norm_mix_l0, norm_mlp_l0, w_mlp1_l0, w_mlp2_l0)
    a_p, q_p, k_p, v_p, ckv_p, kr_p = _l0_pre(xp, mod0_p, tiles_p, False, w0, None)
    a_s, q_s, k_s, v_s, _, _ = _l0_pre(xs, mod0_s, tiles_s, True, w0, _rope_tables(dec_seq))
    kr_cache = jnp.pad(cache_l0_mla_krope.reshape(dec_batch * past, MLA_ROPE), ((0, 0), (MLA_NOPE, LANES - MLA_QK)))
    k_c, v_c = _cache_kv(cache_l0_mla_ckv.reshape(dec_batch * past, KV_LORA), kr_cache, w0)
    mla_bound = _normed_len(mla_q_norm_l0 * (MLA_QK ** -0.5 * LOG2E), MLA_QK) * _normed_len(mla_k_norm_l0, MLA_QK)
    attn_p = _ctx_attn(_mla_ctx_kernel, q_p, k_p, v_p, mla_bound, seq, MLA_HEADS * MLA_V, "l0_mla_ctx_attn")
    attn_s, w16 = _mla_lat_attn(q_s, k_s, v_s, k_c, v_c, jnp.full((dec_batch, MLA_HEADS), mla_bound, F32),
                                dec_seq, past, (w0["w_out"], w0["w_mlp1"], w0["w_mlp2"], w_in_l1))
    w0 = dict(w0, w_out=w16[0], w_mlp1=w16[1], w_mlp2=w16[2])
    xp = _post(xp, attn_p, mod0_p, tiles_p, w0, "l0_post_ctx", pool=(a_p, seq))
    xs = _post(xs, attn_s, mod0_s, tiles_s, w0, "l0_post_latent", pool=(a_s, dec_seq))

    w1 = dict(norm_mix=norm_mix_l1.reshape(1, -1), norm_mlp=norm_mlp_l1.reshape(1, -1),
              w_in=w16[3],
              gq=jnp.tile(na_q_norm_l1 * (NA_HEAD_DIM ** -0.5 * LOG2E), 2).reshape(1, LANES),
              gk=jnp.tile(na_k_norm_l1, 2).reshape(1, LANES),
              w_out=w_out_l1, w_mlp1=w_mlp1_l1, w_mlp2=w_mlp2_l1)
    q1_p, k1_p, v1_p, k1_new, v1_new = _l1_pre(xp, mod1_p, tiles_p, True, w1)
    q1_s, k1_s, v1_s = _l1_pre(xs, mod1_s, tiles_s, False, w1)
    q_len, k_len = _normed_len(w1["gq"], NA_HEAD_DIM), _normed_len(w1["gk"], NA_HEAD_DIM)
    na_p = _ctx_attn(_na_ctx_kernel, q1_p, k1_p, v1_p, q_len * k_len, seq, NA_SLABS * LANES, "l1_na_ctx_attn")
    cache_len = jnp.sqrt(jnp.max(jnp.sum(jnp.square(cache_l1_na_k.astype(F32)), axis=-1), axis=1))
    na_bound = (q_len * jnp.maximum(k_len, cache_len)
                + jnp.max(jnp.abs(rel_bias_l1.astype(F32)) * LOG2E, axis=(1, 2))[None, :])
    na_s, w16, (k1_new, v1_new) = _na_lat_attn(
        q1_s, k1_s, v1_s, cache_l1_na_k.reshape(dec_batch * past, -1), cache_l1_na_v.reshape(dec_batch * past, -1),
        _na_bias_table(rel_bias_l1, dec_seq // GRID_W), na_bound, dec_seq, past,
        (w1["w_out"], w1["w_mlp1"], w1["w_mlp2"]), (k1_new, v1_new))
    w1 = dict(w1, w_out=w16[0], w_mlp1=w16[1], w_mlp2=w16[2])
    xp = _post(xp, na_p, mod1_p, tiles_p, w1, "l1_post_ctx")
    xs = _post(xs, na_s, mod1_s, tiles_s, w1, "l1_post_latent")

    return (xp.reshape(batch, seq, D_MODEL), xs.reshape(dec_batch, dec_seq, D_MODEL),
            ckv_p.reshape(batch, seq, KV_LORA), kr_p.reshape(batch, seq, MLA_ROPE),
            k1_new.reshape(batch, seq, NA_HEADS, NA_HEAD_DIM), v1_new.reshape(batch, seq, NA_HEADS, NA_HEAD_DIM))
```

```python
import functools

import jax
import jax.numpy as jnp
from jax import lax
from jax.experimental import pallas as pl
from jax.experimental.pallas import tpu as pltpu

F32 = jnp.float32
BF16 = jnp.bfloat16

D_MODEL = 1024
D_FF = 4 * D_MODEL
DEPTH = 2
N_ADA = 6
EPS = 1e-6
NEG_INF = -1e30
ROPE_THETA = 10000.0
GRID_W = 64
POOL_WIDTH = 512
POOL_WINDOWS = (2, 4, 8, 16)
POOL_GROUP = 128
POOL_HALO = 8
MLA_HEADS = 8
MLA_NOPE = 64
MLA_ROPE = 32
MLA_QK = MLA_NOPE + MLA_ROPE
MLA_V = 64
Q_LORA = 256
KV_LORA = 128
NA_HEADS = 16
NA_HEAD_DIM = 64
NA_KH = 8
NA_KW = 16
LOG2E = 1.4426950408889634

LANES = 128
SUBLANES = 8
VMEM_LIMIT = 56 * 1024 * 1024

TOKEN_TILE = 512
MLA_Q_TILE = 512
MLA_STEP_HEADS = 4
MLA_KEY_CHUNK = 2048
MLA_ONLINE_CHUNK = 1024
SCORE_LOOKAHEAD = 4
MAX_SCORE_BOUND = 40.0
FF_CHUNK = 1024
CTX_STEP_SEQS = 2
NA_SLABS = NA_HEADS * NA_HEAD_DIM // LANES
NA_TILE_ROWS = 4
NA_WIN_ROWS = NA_TILE_ROWS + NA_KH - 1


def _params(n_axes):
    return pltpu.CompilerParams(dimension_semantics=("arbitrary",) * n_axes,
                                vmem_limit_bytes=VMEM_LIMIT)


def _const_spec(shape):
    zeros = (0,) * len(shape)
    return pl.BlockSpec(shape, lambda *_: zeros, pipeline_mode=pl.Buffered(1))


def _dot(a, b):
    return jnp.dot(a, b, preferred_element_type=F32)


def _dot_nt(a, b):
    return lax.dot_general(a, b, (((1,), (1,)), ((), ())), preferred_element_type=F32)


def _dot_tn(a, b):
    return lax.dot_general(a, b, (((0,), (0,)), ((), ())), preferred_element_type=F32)


def _rms(x, w, n=None):
    n = x.shape[-1] if n is None else n
    ss = jnp.sum(x * x, axis=-1, keepdims=True)
    return x * lax.rsqrt(ss * (1.0 / n) + EPS) * w


def _modulate(x, nw, shift, scale):
    return _rms(x, nw) * (1.0 + scale) + shift


def _ada_spec(mod, tiles_per_group):
    mods, first_row = mod
    return pl.BlockSpec((None, 1, mods.shape[-1]), lambda i: (first_row + i // tiles_per_group, 0, 0))


def _ada_row(mod_ref, k):
    return mod_ref[:, k * D_MODEL:(k + 1) * D_MODEL]


def _ada_kernel(cond_ref, w0_ref, w1_ref, b_ref, o_ref):
    c = cond_ref[...]
    s = (c * jax.nn.sigmoid(c)).astype(BF16)
    for layer, w_ref in enumerate((w0_ref, w1_ref)):
        @pl.when(pl.program_id(0) == layer)
        def _(w_ref=w_ref):
            o_ref[...] = _dot(s, w_ref[...].astype(BF16)) + b_ref[...]


def _ada(cond8, w_ada, b_ada):
    n_out = w_ada[0].shape[1]
    tn = D_MODEL
    nb = n_out // tn
    return pl.pallas_call(
        _ada_kernel,
        out_shape=jax.ShapeDtypeStruct((DEPTH, SUBLANES, n_out), F32),
        grid=(DEPTH, nb),
        in_specs=[pl.BlockSpec((SUBLANES, D_MODEL), lambda l, j: (0, 0)),
                  pl.BlockSpec((D_MODEL, tn), lambda l, j: (0, jnp.where(l == 0, j, nb - 1))),
                  pl.BlockSpec((D_MODEL, tn), lambda l, j: (0, jnp.where(l == 1, j, 0))),
                  pl.BlockSpec((None, 1, tn), lambda l, j: (l, 0, j))],
        out_specs=pl.BlockSpec((None, SUBLANES, tn), lambda l, j: (l, 0, j)),
        compiler_params=_params(2),
        name="ada",
    )(cond8, w_ada[0], w_ada[1], jnp.stack(b_ada).reshape(DEPTH, 1, n_out))


def _rope(x, c, s):
    return x * c + pltpu.roll(x, LANES - MLA_ROPE, 1) * s


PAIR = 2 * LANES


def _head_rinv(raw, e2_ref):
    ss = _dot((raw * raw).astype(BF16), e2_ref[...])
    return lax.rsqrt(ss * (1.0 / MLA_QK) + EPS)


def _mla_keys_values(ckv, kr_slab, wk_ref, gk, wv_ref, e2_ref, rope_tabs, k_ref, v_ref):
    c16 = ckv.astype(BF16)
    tail = kr_slab * gk
    if rope_tabs is not None:
        tail = _rope(tail, *rope_tabs)
    ones_lane = (lax.broadcasted_iota(jnp.int32, (1, LANES), 1) == MLA_V).astype(F32)
    for p in range(MLA_HEADS // 2):
        cols = pl.ds(p * PAIR, PAIR)
        kn = _dot(c16, wk_ref[:, cols])
        v = _dot(c16, wv_ref[:, cols])
        rinv = _head_rinv(kn + jnp.tile(kr_slab, (1, 2)), e2_ref)
        for i in range(2):
            hs = slice(i * LANES, (i + 1) * LANES)
            k_ref[2 * p + i] = (rinv[:, hs] * (kn[:, hs] * gk + tail)).astype(BF16)
            v_ref[2 * p + i] = (v[:, hs] + ones_lane).astype(BF16)


def _l0_pre_kernel(latent, x_ref, mod_ref, nw_ref, win_ref, qln_ref, kvn_ref, wq_ref, gq_ref,
                   wk_ref, gk_ref, wv_ref, e2_ref, *rest):
    if latent:
        rope_ref, a_ref, q_ref, k_ref, v_ref, ckv_ref, kr_ref = rest
        tile_rows = x_ref.shape[0] // GRID_W
        row0 = (pl.program_id(0) * tile_rows) % rope_ref.shape[1]
        expand = lambda col, row: jnp.concatenate(
            [rope_ref[col] + rope_ref[row, pl.ds(row0 + r, 1), :] for r in range(tile_rows)], axis=0)
        rope_tabs = (expand(0, 1), expand(2, 3))
    else:
        a_ref, q_ref, k_ref, v_ref, ckv_ref, kr_ref = rest
        rope_tabs = None
    h = _modulate(x_ref[...], nw_ref[...], _ada_row(mod_ref, 0), _ada_row(mod_ref, 1)).astype(BF16)
    proj = _dot(h, win_ref[...])
    a_ref[...] = proj[:, :POOL_WIDTH]
    q_lat = proj[:, POOL_WIDTH:POOL_WIDTH + Q_LORA]
    kv_lat = proj[:, POOL_WIDTH + Q_LORA:POOL_WIDTH + Q_LORA + KV_LORA]
    kr_slab = proj[:, POOL_WIDTH + Q_LORA + KV_LORA:]
    ql = _rms(q_lat, qln_ref[...]).astype(BF16)
    gq = gq_ref[...]
    for p in range(MLA_HEADS // 2):
        q = _dot(ql, wq_ref[:, pl.ds(p * PAIR, PAIR)])
        q_rinv = _head_rinv(q, e2_ref)
        for i in range(2):
            hs = slice(i * LANES, (i + 1) * LANES)
            y = q[:, hs] * q_rinv[:, hs] * gq
            if rope_tabs is not None:
                y = _rope(y, *rope_tabs)
            q_ref[2 * p + i] = y.astype(BF16)
    ckv = _rms(kv_lat, kvn_ref[...])
    ckv_ref[...] = ckv
    kr_ref[...] = kr_slab[:, MLA_NOPE:MLA_QK]
    _mla_keys_values(ckv, kr_slab, wk_ref, gk_ref[...], wv_ref, e2_ref, rope_tabs, k_ref, v_ref)


def _slab_spec(slabs, tm):
    return pl.BlockSpec((slabs, tm, LANES), lambda i: (0, i, 0))


def _l0_pre(x, mod, tiles_per_group, latent, w, rope_tabs):
    n = x.shape[0]
    tm = TOKEN_TILE
    row = lambda i: (i, 0)
    hk = MLA_HEADS * LANES
    in_specs = [pl.BlockSpec((tm, D_MODEL), row),
                _ada_spec(mod, tiles_per_group),
                _const_spec((1, D_MODEL)),
                _const_spec((D_MODEL, D_MODEL)),
                _const_spec((1, Q_LORA)),
                _const_spec((1, KV_LORA)),
                _const_spec((Q_LORA, hk)),
                _const_spec((1, LANES)),
                _const_spec((KV_LORA, hk)),
                _const_spec((1, LANES)),
                _const_spec((KV_LORA, hk)),
                _const_spec((2 * LANES, 2 * LANES))]
    gains = "rope" if latent else "plain"
    args = [x, mod[0], w["norm_mix"], w["w_in"], w["qln"], w["kvn"], w["wq"], w["gq_" + gains], w["wk"],
            w["gk_" + gains], w["wv"], w["e2"]]
    if latent:
        assert tm % GRID_W == 0
        in_specs.append(_const_spec(rope_tabs.shape))
        args.append(rope_tabs)
    slab = jax.ShapeDtypeStruct((MLA_HEADS, n, LANES), BF16)
    out_shape = (jax.ShapeDtypeStruct((n, POOL_WIDTH), F32), slab, slab, slab,
                 jax.ShapeDtypeStruct((n, KV_LORA), F32),
                 jax.ShapeDtypeStruct((n, MLA_ROPE), F32))
    out_specs = (pl.BlockSpec((tm, POOL_WIDTH), row),
                 _slab_spec(MLA_HEADS, tm), _slab_spec(MLA_HEADS, tm), _slab_spec(MLA_HEADS, tm),
                 pl.BlockSpec((tm, KV_LORA), row), pl.BlockSpec((tm, MLA_ROPE), row))
    return pl.pallas_call(
        functools.partial(_l0_pre_kernel, latent),
        out_shape=out_shape, grid=(n // tm,), in_specs=in_specs, out_specs=out_specs,
        compiler_params=_params(1), name="l0_pre_latent" if latent else "l0_pre_ctx",
    )(*args)


def _cache_kv_kernel(ckv_ref, kr_ref, wk_ref, gk_ref, wv_ref, e2_ref, k_ref, v_ref):
    _mla_keys_values(ckv_ref[...], kr_ref[...], wk_ref, gk_ref[...], wv_ref, e2_ref, None, k_ref, v_ref)


def _cache_kv(ckv, kr_slab, w):
    n = ckv.shape[0]
    tm = TOKEN_TILE
    row = lambda i: (i, 0)
    hk = MLA_HEADS * LANES
    slab = jax.ShapeDtypeStruct((MLA_HEADS, n, LANES), BF16)
    return pl.pallas_call(
        _cache_kv_kernel,
        out_shape=(slab, slab),
        grid=(n // tm,),
        in_specs=[pl.BlockSpec((tm, KV_LORA), row), pl.BlockSpec((tm, LANES), row),
                  _const_spec((KV_LORA, hk)), _const_spec((1, LANES)), _const_spec((KV_LORA, hk)),
                  _const_spec((2 * LANES, 2 * LANES))],
        out_specs=(_slab_spec(MLA_HEADS, tm), _slab_spec(MLA_HEADS, tm)),
        compiler_params=_params(1), name="l0_cache_kv",
    )(ckv, kr_slab, w["wk"], w["gk_plain"], w["wv"], w["e2"])


def _pool_mixer(seq_len, tile, prev_ref, u_ref, next_ref, wp_ref, ps_ref, buf_ref):
    tm = u_ref.shape[0]
    piece = min(tm, seq_len)
    n_pieces = tm // piece
    stride = piece + 2 * POOL_HALO
    starts = []
    for j in range(n_pieces):
        pos0 = (tile * tm + j * piece) % seq_len
        prev = prev_ref[...] if j == 0 else u_ref[j * piece - POOL_HALO:j * piece, :]
        nxt = next_ref[...] if j == n_pieces - 1 else u_ref[(j + 1) * piece:(j + 1) * piece + POOL_HALO, :]
        base = j * stride
        buf_ref[base:base + POOL_HALO, :] = jnp.where(pos0 != 0, prev, 0.0)
        buf_ref[base + POOL_HALO:base + POOL_HALO + piece, :] = u_ref[j * piece:(j + 1) * piece, :]
        buf_ref[base + POOL_HALO + piece:base + stride, :] = jnp.where(pos0 + piece != seq_len, nxt, 0.0)
        starts.append(pos0)
    def group(g):
        win = POOL_WINDOWS[g]
        lanes = pl.ds(g * POOL_GROUP, POOL_GROUP)
        pooled = []
        for j in range(n_pieces):
            centre = j * stride + POOL_HALO
            lo = starts[j] + lax.broadcasted_iota(jnp.int32, (piece, 1), 0) - win // 2
            cnt = (jnp.minimum(lo + win, seq_len) - jnp.maximum(lo, 0)).astype(F32)
            acc = buf_ref[pl.ds(centre - win // 2, piece), lanes]
            for d in range(1 - win // 2, win - win // 2):
                acc = acc + buf_ref[pl.ds(centre + d, piece), lanes]
            pooled.append(acc / cnt - buf_ref[pl.ds(centre, piece), lanes])
        pooled = pooled[0] if n_pieces == 1 else jnp.concatenate(pooled, axis=0)
        return (_dot(pooled.astype(BF16), wp_ref[g]) * ps_ref[:, lanes]).astype(BF16)

    return [functools.partial(group, g) for g in range(len(POOL_WINDOWS))]


def _attend(streams, running_max):
    items = [(si, ch) for si, (_, chunks, _) in enumerate(streams) for ch in chunks]

    def score(item):
        si, (k_fn, _, extra_fn) = item
        s = _dot_nt(k_fn(), streams[si][0])
        return s if extra_fn is None else s + extra_fn()

    state = [None] * len(streams)
    pending = [score(item) for item in items[:SCORE_LOOKAHEAD]]
    for i, (si, (_, v_fn, _)) in enumerate(items):
        s = pending.pop(0)
        if i + SCORE_LOOKAHEAD < len(items):
            pending.append(score(items[i + SCORE_LOOKAHEAD]))
        want_den = streams[si][2]
        m_new, alpha = None, 1.0
        if state[si] is not None:
            m, acc, den = state[si]
        if running_max:
            m_new = jnp.max(s, axis=0, keepdims=True)
            if state[si] is not None:
                m_new = jnp.maximum(m, m_new)
                alpha = jnp.exp2(m - m_new)
            s = s - m_new
        p = jnp.exp2(s)
        pv = _dot_tn(v_fn(), p.astype(BF16))
        psum = jnp.sum(p, axis=0, keepdims=True) if want_den else None
        if state[si] is None:
            state[si] = (m_new, pv, psum)
        elif running_max:
            state[si] = (m_new, acc * alpha + pv, den * alpha + psum if want_den else None)
        else:
            state[si] = (None, acc + pv, den + psum if want_den else None)
    return [(acc, den) for _, acc, den in state]


def _ref_chunks(k_ref, v_ref, slab, chunk, extra_fn=None, lo=0, hi=None):
    hi = k_ref.shape[1] if hi is None else hi
    out = []
    for c in range(lo, hi, chunk):
        rows = pl.ds(c, min(chunk, hi - c))
        out.append((lambda rows=rows: k_ref[slab, rows, :], lambda rows=rows: v_ref[slab, rows, :], extra_fn))
    return out


def _mla_finish(o_t):
    return o_t[:MLA_V] / o_t[MLA_V:MLA_V + 1]


def _na_queries(q2):
    low = lax.broadcasted_iota(jnp.int32, (1, LANES), 1) < NA_HEAD_DIM
    zero = jnp.zeros_like(q2)
    return jnp.concatenate([jnp.where(low, q2, zero), jnp.where(low, zero, q2)], axis=0)


def _na_finish(o_t, den):
    nq = o_t.shape[1] // 2
    o = o_t / den
    return jnp.concatenate([o[:NA_HEAD_DIM, :nq], o[NA_HEAD_DIM:, nq:]], axis=0)


def _attend_guarded(bound, streams_fn):
    return lax.cond(bound <= MAX_SCORE_BOUND,
                    lambda: _attend(streams_fn(False), False),
                    lambda: _attend(streams_fn(True), True))


def _mla_ctx_kernel(seq, bound_ref, q_ref, k_ref, v_ref, o_ref):
    spans = [(lo, lo + seq) for lo in range(0, k_ref.shape[1], seq)]
    res = _attend_guarded(bound_ref[0, 0], lambda _: [
        (q_ref[h, lo:hi, :], _ref_chunks(k_ref, v_ref, h, seq, lo=lo, hi=hi), False)
        for lo, hi in spans for h in range(MLA_HEADS)])
    for i, (lo, hi) in enumerate(spans):
        for hp in range(MLA_HEADS // 2):
            o_t = jnp.concatenate([_mla_finish(res[i * MLA_HEADS + h][0]) for h in (2 * hp, 2 * hp + 1)], axis=0)
            o_ref[lo:hi, hp * LANES:(hp + 1) * LANES] = o_t.T.astype(BF16)


def _na_ctx_kernel(seq, bound_ref, q_ref, k_ref, v_ref, o_ref):
    spans = [(lo, lo + seq) for lo in range(0, k_ref.shape[1], seq)]
    res = _attend_guarded(bound_ref[0, 0], lambda _: [
        (_na_queries(q_ref[s, lo:hi, :]), _ref_chunks(k_ref, v_ref, s, seq, lo=lo, hi=hi), True)
        for lo, hi in spans for s in range(NA_SLABS)])
    for i, (lo, hi) in enumerate(spans):
        for s in range(NA_SLABS):
            o_ref[lo:hi, s * LANES:(s + 1) * LANES] = _na_finish(*res[i * NA_SLABS + s]).T.astype(BF16)


def _ctx_attn(body, q, k, v, bound, seq, width, name):
    slabs, n, _ = q.shape
    rows = CTX_STEP_SEQS * seq
    spec = pl.BlockSpec((slabs, rows, LANES), lambda b: (0, b, 0))
    return pl.pallas_call(
        functools.partial(body, seq),
        out_shape=jax.ShapeDtypeStruct((n, width), BF16),
        grid=(n // rows,),
        in_specs=[pl.BlockSpec(memory_space=pltpu.SMEM), spec, spec, spec],
        out_specs=pl.BlockSpec((rows, width), lambda b: (b, 0)),
        compiler_params=_params(1), name=name,
    )(bound.reshape(1, 1), q, k, v)


SIDE_PIECE_ELEMS = 512 * 1024


def _side_piece_rows(n_cols):
    return 1 << ((SIDE_PIECE_ELEMS // n_cols).bit_length() - 1)


def _side_cast_plan(shapes):
    plan = []
    for wi, (n_rows, n_cols) in enumerate(shapes):
        rows = _side_piece_rows(n_cols)
        assert n_rows % rows == 0
        plan += [(wi, r0, rows) for r0 in range(0, n_rows, rows)]
    return plan


def _side_cast_scratch(shapes):
    widths = sorted({c for _, c in shapes})
    stages = [pltpu.VMEM((2, _side_piece_rows(c), c), dt) for dt in (F32, BF16) for c in widths]
    return stages + [pltpu.SemaphoreType.DMA((2,)), pltpu.SemaphoreType.DMA((2,))]


def _side_cast(step, srcs, dsts, scratch):
    shapes = [s.shape for s in srcs]
    widths = sorted({c for _, c in shapes})
    stage_in = dict(zip(widths, scratch[:len(widths)]))
    stage_out = dict(zip(widths, scratch[len(widths):2 * len(widths)]))
    sem_in, sem_out = scratch[2 * len(widths):]
    plan = _side_cast_plan(shapes)

    def refs(k):
        wi, r0, rows = plan[k]
        cols = shapes[wi][1]
        return (srcs[wi].at[pl.ds(r0, rows), :], dsts[wi].at[pl.ds(r0, rows), :],
                stage_in[cols].at[k % 2], stage_out[cols].at[k % 2])

    read = lambda k: pltpu.make_async_copy(refs(k)[0], refs(k)[2], sem_in.at[k % 2])
    write = lambda k: pltpu.make_async_copy(refs(k)[3], refs(k)[1], sem_out.at[k % 2])
    n = len(plan)

    def piece_step(k):
        if k == 0:
            read(0).start()
        if k >= 2:
            write(k - 2).wait()
        if k < n:
            read(k).wait()
            if k + 1 < n:
                read(k + 1).start()
            refs(k)[3][...] = refs(k)[2][...].astype(BF16)
            write(k).start()

    @pl.when(step < n + 2)
    def _():
        for k in range(n + 2):
            pl.when(step == k)(functools.partial(piece_step, k))


def _mla_lat_kernel(n_cast, bound_ref, q_ref, kl_ref, vl_ref, kc_ref, vc_ref, *rest):
    o_ref = rest[n_cast]
    linear_step = (pl.program_id(0) * pl.num_programs(1) + pl.program_id(1)) * pl.num_programs(2) + pl.program_id(2)
    _side_cast(linear_step, rest[:n_cast], rest[n_cast + 1:2 * n_cast + 1], rest[2 * n_cast + 1:])
    batch, head0 = pl.program_id(0), pl.program_id(1) * MLA_STEP_HEADS
    heads = range(MLA_STEP_HEADS)
    bounds = [bound_ref[batch, head0 + h] for h in heads]

    def streams(running_max):
        size = MLA_ONLINE_CHUNK if running_max else MLA_KEY_CHUNK
        return [(q_ref[h], _ref_chunks(kl_ref, vl_ref, h, size) + _ref_chunks(kc_ref, vc_ref, h, size), False)
                for h in heads]

    res = _attend_guarded(functools.reduce(jnp.maximum, bounds), streams)
    o_ref[...] = jnp.concatenate([_mla_finish(o) for o, _ in res], axis=0).T.astype(BF16)


def _normed_len(gain, dim):
    return (dim ** 0.5) * jnp.max(jnp.abs(gain.astype(F32)))


def _side_cast_specs(weights, n_steps):
    shapes = [wt.shape for wt in weights]
    assert n_steps >= len(_side_cast_plan(shapes)) + 2
    hbm = pl.BlockSpec(memory_space=pl.ANY)
    return ([hbm] * len(weights), [jax.ShapeDtypeStruct(s, BF16) for s in shapes], [hbm] * len(weights),
            _side_cast_scratch(shapes))


def _mla_lat_attn(q, k, v, kc, vc, bound, seq, past, weights):
    n = q.shape[1]
    tq = MLA_Q_TILE
    nq = seq // tq
    hs = MLA_STEP_HEADS
    grid = (n // seq, MLA_HEADS // hs, nq)
    w_in, w_shapes, w_out, w_scratch = _side_cast_specs(weights, grid[0] * grid[1] * grid[2])
    outs = pl.pallas_call(
        functools.partial(_mla_lat_kernel, len(weights)),
        out_shape=(jax.ShapeDtypeStruct((n, MLA_HEADS * MLA_V), BF16), *w_shapes),
        grid=grid,
        in_specs=[pl.BlockSpec(memory_space=pltpu.SMEM),
                  pl.BlockSpec((hs, tq, LANES), lambda b, h, i: (h, b * nq + i, 0)),
                  pl.BlockSpec((hs, seq, LANES), lambda b, h, i: (h, b, 0)),
                  pl.BlockSpec((hs, seq, LANES), lambda b, h, i: (h, b, 0)),
                  pl.BlockSpec((hs, past, LANES), lambda b, h, i: (h, b, 0)),
                  pl.BlockSpec((hs, past, LANES), lambda b, h, i: (h, b, 0))] + w_in,
        out_specs=(pl.BlockSpec((tq, hs * MLA_V), lambda b, h, i: (b * nq + i, h)), *w_out),
        scratch_shapes=w_scratch,
        compiler_params=_params(3), name="l0_mla_latent_attn",
    )(bound, q, k, v, kc, vc, *weights)
    return outs[0], outs[1:]


def _na_first_row(r, rows):
    return jnp.clip(r - NA_KH // 2, 0, rows - NA_KH)


def _na_lat_kernel(rows, n_cast, bound_ref, q_ref, k_ref, v_ref, kc_ref, vc_ref, t2_ref, *rest):
    o_ref = rest[n_cast]
    _side_cast(pl.program_id(0) * pl.num_programs(1) + pl.program_id(1),
               rest[:n_cast], rest[n_cast + 1:2 * n_cast + 1], rest[2 * n_cast + 1:])
    nq = NA_TILE_ROWS * GRID_W
    nk = NA_WIN_ROWS * GRID_W
    n_off = 2 * NA_KH - 1
    r0 = pl.program_id(1) * NA_TILE_ROWS
    w0 = jnp.clip(r0 - NA_KH // 2, 0, rows - NA_WIN_ROWS)

    def block_index(j, t):
        key_row = w0 + j
        i = r0 + 2 * t
        first_u, first_l = _na_first_row(i, rows), _na_first_row(i + 1, rows)
        in_u = (key_row >= first_u) & (key_row < first_u + NA_KH)
        in_l = (key_row >= first_l) & (key_row < first_l + NA_KH)
        d = jnp.clip(key_row - i + NA_KH - 1, 0, n_off - 1)
        return jnp.where(in_u & in_l, d, jnp.where(in_u, n_off, jnp.where(in_l, n_off + 1, n_off + 2)))

    idx = [[block_index(j, t) for t in range(NA_TILE_ROWS // 2)] for j in range(NA_WIN_ROWS)]

    def local_bias(s):
        heads = [jnp.concatenate([jnp.concatenate([t2_ref[2 * s + hh, i] for i in row], axis=1) for row in idx], axis=0)
                 for hh in range(2)]
        return jnp.concatenate(heads, axis=1)

    batch = pl.program_id(0)
    bounds = [bound_ref[batch, h] for h in range(NA_HEADS)]

    def cache_chunk(s):
        lanes = pl.ds(s * LANES, LANES)
        return [(lambda: kc_ref[:, lanes].astype(BF16), lambda: vc_ref[:, lanes].astype(BF16), None)]

    res = _attend_guarded(
        functools.reduce(jnp.maximum, bounds),
        lambda _: [(_na_queries(q_ref[s]),
                    _ref_chunks(k_ref, v_ref, s, nk, functools.partial(local_bias, s)) + cache_chunk(s),
                    True) for s in range(NA_SLABS)])
    for s in range(NA_SLABS):
        o_ref[:, s * LANES:(s + 1) * LANES] = _na_finish(*res[s]).T.astype(BF16)


def _na_lat_attn(q, k, v, kc, vc, t2, bound, seq, past, weights):
    slabs, n, _ = q.shape
    rows = seq // GRID_W
    tiles = rows // NA_TILE_ROWS
    nq = NA_TILE_ROWS * GRID_W
    nk = NA_WIN_ROWS * GRID_W

    def win_map(b, t):
        w0 = jnp.clip(t * NA_TILE_ROWS - NA_KH // 2, 0, rows - NA_WIN_ROWS)
        return (0, pl.multiple_of(b * seq + w0 * GRID_W, GRID_W), 0)

    win_spec = pl.BlockSpec((pl.Element(slabs), pl.Element(nk), pl.Element(LANES)), win_map)
    ctx_spec = pl.BlockSpec((past, slabs * LANES), lambda b, t: (b, 0))
    w_in, w_shapes, w_out, w_scratch = _side_cast_specs(weights, (n // seq) * tiles)
    outs = pl.pallas_call(
        functools.partial(_na_lat_kernel, rows, len(weights)),
        out_shape=(jax.ShapeDtypeStruct((n, slabs * LANES), BF16), *w_shapes),
        grid=(n // seq, tiles),
        in_specs=[pl.BlockSpec(memory_space=pltpu.SMEM),
                  pl.BlockSpec((slabs, nq, LANES), lambda b, t: (0, b * tiles + t, 0)),
                  win_spec, win_spec, ctx_spec, ctx_spec,
                  _const_spec(t2.shape)] + w_in,
        out_specs=(pl.BlockSpec((nq, slabs * LANES), lambda b, t: (b * tiles + t, 0)), *w_out),
        scratch_shapes=w_scratch,
        compiler_params=_params(2), name="l1_na_latent_attn",
    )(bound, q, k, v, kc, vc, t2, *weights)
    return outs[0], outs[1:]


def _pair_norm(x2, g2):
    low = lax.broadcasted_iota(jnp.int32, (1, LANES), 1) < NA_HEAD_DIM
    sq = x2 * x2
    ss_all = jnp.sum(sq, axis=-1, keepdims=True)
    ss_low = jnp.sum(jnp.where(low, sq, 0.0), axis=-1, keepdims=True)
    ss = jnp.where(low, ss_low, ss_all - ss_low)
    return x2 * lax.rsqrt(ss * (1.0 / NA_HEAD_DIM) + EPS) * g2


def _l1_pre_kernel(keep_f32, x_ref, mod_ref, nw_ref, win_ref, gq_ref, gk_ref, q_ref, k_ref, v_ref, *f32_refs):
    h = _modulate(x_ref[...], nw_ref[...], _ada_row(mod_ref, 0), _ada_row(mod_ref, 1)).astype(BF16)
    width = NA_SLABS * LANES
    q = _dot(h, win_ref[:, 0:width])
    k = _dot(h, win_ref[:, width:2 * width])
    v = _dot(h, win_ref[:, 2 * width:3 * width])
    gq = gq_ref[...]
    gk = gk_ref[...]
    for s in range(NA_SLABS):
        ls = slice(s * LANES, (s + 1) * LANES)
        q_ref[s] = _pair_norm(q[:, ls], gq).astype(BF16)
        kn = _pair_norm(k[:, ls], gk)
        k_ref[s] = kn.astype(BF16)
        v_ref[s] = v[:, ls].astype(BF16)
        if keep_f32:
            f32_refs[0][:, ls] = kn
    if keep_f32:
        f32_refs[1][...] = v


def _l1_pre(x, mod, tiles_per_group, keep_f32, w):
    n = x.shape[0]
    tm = TOKEN_TILE
    row = lambda i: (i, 0)
    width = NA_SLABS * LANES
    slab = jax.ShapeDtypeStruct((NA_SLABS, n, LANES), BF16)
    out_shape = [slab, slab, slab]
    out_specs = [_slab_spec(NA_SLABS, tm)] * 3
    if keep_f32:
        out_shape += [jax.ShapeDtypeStruct((n, width), F32)] * 2
        out_specs += [pl.BlockSpec((tm, width), row)] * 2
    return pl.pallas_call(
        functools.partial(_l1_pre_kernel, keep_f32),
        out_shape=tuple(out_shape), grid=(n // tm,),
        in_specs=[pl.BlockSpec((tm, D_MODEL), row),
                  _ada_spec(mod, tiles_per_group),
                  _const_spec((1, D_MODEL)),
                  _const_spec((D_MODEL, 3 * width)),
                  _const_spec((1, LANES)),
                  _const_spec((1, LANES))],
        out_specs=tuple(out_specs),
        compiler_params=_params(1), name="l1_pre_ctx" if keep_f32 else "l1_pre_latent",
    )(x, mod[0], w["norm_mix"], w["w_in"], w["gq"], w["gk"])


def _post_kernel(pool_seq, x_ref, attn_ref, *rest):
    rest = list(rest)
    take = lambda k: [rest.pop(0) for _ in range(k)]
    pool_in = take(8) if pool_seq is not None else None
    mod_ref, nw_ref, wo_ref, w1_ref, w2_ref, o_ref = take(6)
    pool_scratch = take(2) if pool_seq is not None else None
    step = pl.program_id(0)
    n_chunks = D_FF // FF_CHUNK
    if pool_seq is not None:
        first_refs, next_refs, (wp_ref, ps_ref) = pool_in[0:3], pool_in[3:6], pool_in[6:8]
        buf_ref, pooled_ref = pool_scratch

        @pl.when(step == 0)
        def _():
            for g, group in enumerate(_pool_mixer(pool_seq, 0, *first_refs, wp_ref, ps_ref, buf_ref)):
                pooled_ref[:, g * POOL_GROUP:(g + 1) * POOL_GROUP] = group()

        mixed = _dot(jnp.concatenate([pooled_ref[...], attn_ref[...]], axis=-1), wo_ref[...])
        next_groups = _pool_mixer(pool_seq, jnp.minimum(step + 1, pl.num_programs(0) - 1), *next_refs,
                                  wp_ref, ps_ref, buf_ref)
    else:
        mixed = _dot(attn_ref[...], wo_ref[...])
    x1 = x_ref[...] + _ada_row(mod_ref, 2) * mixed
    h = _modulate(x1, nw_ref[...], _ada_row(mod_ref, 3), _ada_row(mod_ref, 4)).astype(BF16)
    acc = jnp.zeros(x1.shape, F32)
    for c in range(n_chunks):
        cs = pl.ds(c * FF_CHUNK, FF_CHUNK)
        u = jnp.square(jnp.maximum(_dot(h, w1_ref[:, cs]), 0.0)).astype(BF16)
        acc = acc + _dot(u, w2_ref[cs, :])
        if pool_seq is not None:
            for g in range(c * len(next_groups) // n_chunks, (c + 1) * len(next_groups) // n_chunks):
                pooled_ref[:, g * POOL_GROUP:(g + 1) * POOL_GROUP] = next_groups[g]()
    o_ref[...] = x1 + _ada_row(mod_ref, 5) * acc


def _post(x, attn, mod, tiles_per_group, w, name, pool=None):
    n = x.shape[0]
    tm = TOKEN_TILE
    row = lambda i: (i, 0)
    in_specs = [pl.BlockSpec((tm, D_MODEL), row), pl.BlockSpec((tm, attn.shape[1]), row)]
    args = [x, attn]
    scratch = []
    pool_seq = None
    if pool is not None:
        a_in, pool_seq = pool
        hb = tm // POOL_HALO
        last = n // POOL_HALO - 1
        pieces = tm // min(tm, pool_seq)
        last_tile = n // tm - 1

        def tile_specs(tile_of):
            return [pl.BlockSpec((POOL_HALO, POOL_WIDTH), lambda i: (jnp.maximum(tile_of(i) * hb - 1, 0), 0)),
                    pl.BlockSpec((tm, POOL_WIDTH), lambda i: (tile_of(i), 0)),
                    pl.BlockSpec((POOL_HALO, POOL_WIDTH), lambda i: (jnp.minimum((tile_of(i) + 1) * hb, last), 0))]

        in_specs += tile_specs(lambda i: 0) + tile_specs(lambda i: jnp.minimum(i + 1, last_tile))
        in_specs += [_const_spec((len(POOL_WINDOWS), POOL_GROUP, POOL_GROUP)), _const_spec((1, POOL_WIDTH))]
        args += [a_in] * 6 + [w["w_pool"], w["pool_scale"]]
        scratch = [pltpu.VMEM((tm + 2 * POOL_HALO * pieces, POOL_WIDTH), F32), pltpu.VMEM((tm, POOL_WIDTH), BF16)]
    in_specs += [_ada_spec(mod, tiles_per_group),
                 _const_spec((1, D_MODEL)),
                 _const_spec((D_MODEL, D_MODEL)),
                 _const_spec((D_MODEL, D_FF)),
                 _const_spec((D_FF, D_MODEL))]
    args += [mod[0], w["norm_mlp"], w["w_out"], w["w_mlp1"], w["w_mlp2"]]
    return pl.pallas_call(
        functools.partial(_post_kernel, pool_seq),
        out_shape=jax.ShapeDtypeStruct((n, D_MODEL), F32), grid=(n // tm,),
        in_specs=in_specs, out_specs=pl.BlockSpec((tm, D_MODEL), row),
        scratch_shapes=scratch, compiler_params=_params(1), name=name,
    )(*args)


def _rope_tables(seq):
    assert seq == GRID_W * GRID_W
    half = MLA_ROPE // 2
    inv_freq = jnp.power(ROPE_THETA, -jnp.arange(0, half, 2, dtype=F32) / half)
    ang = jnp.arange(GRID_W, dtype=F32)[:, None] * inv_freq[None, :]
    cos, sin = jnp.cos(ang), jnp.sin(ang)
    fill = lambda n, v=0.0: jnp.full((GRID_W, n), v, F32)
    spare = LANES - MLA_QK
    return jnp.stack([
        jnp.concatenate([fill(MLA_NOPE, 1.0), fill(half), cos, cos, fill(spare)], axis=1),
        jnp.concatenate([fill(MLA_NOPE), cos, cos, fill(half), fill(spare)], axis=1),
        jnp.concatenate([fill(MLA_NOPE), fill(half), -sin, sin, fill(spare)], axis=1),
        jnp.concatenate([fill(MLA_NOPE), -sin, sin, fill(half), fill(spare)], axis=1)])


def _rope_partner(tail):
    shape = tail.shape
    t = tail.reshape(shape[:-1] + (2, 2, MLA_ROPE // 4))
    return t[..., ::-1, :].reshape(shape)


def _na_bias_table(rel_bias, rows):
    qcol = jnp.arange(GRID_W)
    kcol = jnp.arange(GRID_W)
    col_start = jnp.clip(qcol - NA_KW // 2, 0, GRID_W - NA_KW)
    col_mask = (kcol[:, None] >= col_start[None, :]) & (kcol[:, None] < col_start[None, :] + NA_KW)
    offs = jnp.arange(2 * NA_KW - 1)
    onehot = (kcol[None, :, None] - qcol[None, None, :] + NA_KW - 1 == offs[:, None, None]).astype(F32)
    blocks = jnp.einsum('hdj,jkq->hdkq', rel_bias.astype(F32) * LOG2E, onehot, precision=lax.Precision.HIGHEST)
    blocks = jnp.where(col_mask[None, None], blocks, NEG_INF)
    neg = jnp.full_like(blocks[:, :1], NEG_INF)
    both = jnp.concatenate([blocks, jnp.concatenate([neg, blocks[:, :-1]], axis=1)], axis=-1)
    d_upper, d_lower = _na_partial_offsets(rows)
    upper_only = jnp.concatenate([blocks[:, d_upper:d_upper + 1], neg], axis=-1)
    lower_only = jnp.concatenate([neg, blocks[:, d_lower - 1:d_lower]], axis=-1)
    return jnp.concatenate([both, upper_only, lower_only, jnp.concatenate([neg, neg], axis=-1)], axis=1)


def _na_partial_offsets(rows):
    first = lambda r: min(max(r - NA_KH // 2, 0), rows - NA_KH)
    upper, lower = set(), set()
    for r0 in range(0, rows, NA_TILE_ROWS):
        w0 = min(max(r0 - NA_KH // 2, 0), rows - NA_WIN_ROWS)
        for j in range(NA_WIN_ROWS):
            for t in range(NA_TILE_ROWS // 2):
                key_row, i = w0 + j, r0 + 2 * t
                in_u = first(i) <= key_row < first(i) + NA_KH
                in_l = first(i + 1) <= key_row < first(i + 1) + NA_KH
                d = key_row - i + NA_KH - 1
                if in_u and in_l:
                    assert 1 <= d <= 2 * NA_KH - 2
                elif in_u:
                    upper.add(d)
                elif in_l:
                    lower.add(d)
    assert len(upper) == 1 and len(lower) == 1, (upper, lower)
    return upper.pop(), lower.pop()


def _prep_l0(w_in, q_lora_norm, kv_lora_norm, w_q_up, w_kv_up, mla_q_norm, mla_k_norm, w_pool, pool_scale,
             w_out, norm_mix, norm_mlp, w_mlp1, w_mlp2):
    lat_end = POOL_WIDTH + Q_LORA + KV_LORA
    w_in_pad = jnp.concatenate([w_in[:, :lat_end], jnp.zeros((D_MODEL, MLA_NOPE), w_in.dtype),
                                w_in[:, lat_end:], _rope_partner(w_in[:, lat_end:])], axis=1)
    pad = LANES - MLA_QK
    hk = MLA_HEADS * LANES
    wq = w_q_up.reshape(Q_LORA, MLA_HEADS, MLA_QK)
    wq = jnp.concatenate([wq, _rope_partner(wq[:, :, MLA_NOPE:])], axis=-1)
    kv = w_kv_up.reshape(KV_LORA, MLA_HEADS, MLA_NOPE + MLA_V)
    wk = jnp.pad(kv[:, :, :MLA_NOPE], ((0, 0), (0, 0), (0, LANES - MLA_NOPE)))
    wv = jnp.pad(kv[:, :, MLA_NOPE:], ((0, 0), (0, 0), (0, LANES - MLA_V)))
    gq = mla_q_norm * (MLA_QK ** -0.5 * LOG2E)
    slab_lane = jnp.arange(2 * LANES)
    e2 = (slab_lane[:, None] // LANES == slab_lane[None, :] // LANES) & (slab_lane[:, None] % LANES < MLA_QK)
    return dict(
        norm_mix=norm_mix.reshape(1, -1), norm_mlp=norm_mlp.reshape(1, -1),
        w_in=w_in_pad.astype(BF16),
        qln=q_lora_norm.reshape(1, -1), kvn=kv_lora_norm.reshape(1, -1),
        wq=wq.reshape(Q_LORA, hk).astype(BF16),
        gq_plain=jnp.pad(gq, (0, pad)).reshape(1, LANES),
        gq_rope=jnp.concatenate([gq, _rope_partner(gq[MLA_NOPE:])]).reshape(1, LANES),
        wk=wk.reshape(KV_LORA, hk).astype(BF16),
        gk_plain=jnp.pad(mla_k_norm, (0, pad)).reshape(1, LANES),
        gk_rope=jnp.concatenate([mla_k_norm, _rope_partner(mla_k_norm[MLA_NOPE:])]).reshape(1, LANES),
        wv=wv.reshape(KV_LORA, hk).astype(BF16),
        e2=e2.astype(BF16),
        w_pool=w_pool.astype(BF16), pool_scale=pool_scale.reshape(1, -1),
        w_out=w_out, w_mlp1=w_mlp1, w_mlp2=w_mlp2)


def kernel(x_prompt, x_sample, cache_l0_mla_ckv, cache_l0_mla_krope, cache_l1_na_k, cache_l1_na_v, c, c_ctx, w_ada_l0, b_ada_l0, norm_mix_l0, norm_mlp_l0, w_mlp1_l0, w_mlp2_l0, w_in_l0, q_lora_norm_l0, kv_lora_norm_l0, w_q_up_l0, w_kv_up_l0, mla_q_norm_l0, mla_k_norm_l0, w_pool_l0, pool_scale_l0, w_out_l0, w_ada_l1, b_ada_l1, norm_mix_l1, norm_mlp_l1, w_mlp1_l1, w_mlp2_l1, w_in_l1, na_q_norm_l1, na_k_norm_l1, rel_bias_l1, w_out_l1):
    batch, seq, _ = x_prompt.shape
    dec_batch, dec_seq, _ = x_sample.shape
    past = cache_l0_mla_ckv.shape[1]
    xp = x_prompt.reshape(batch * seq, D_MODEL)
    xs = x_sample.reshape(dec_batch * dec_seq, D_MODEL)
    tiles_p = (batch * seq) // TOKEN_TILE
    tiles_s = dec_seq // TOKEN_TILE

    cond8 = jnp.concatenate([c_ctx[None, :], c, jnp.zeros((SUBLANES - 1 - dec_batch, D_MODEL), F32)], axis=0)
    mods = _ada(cond8, (w_ada_l0, w_ada_l1), (b_ada_l0, b_ada_l1))
    mods = mods.reshape(DEPTH, SUBLANES, 1, N_ADA * D_MODEL)
    (mod0_p, mod0_s), (mod1_p, mod1_s) = (((mods[l], 0), (mods[l], 1)) for l in range(DEPTH))

    w0 = _prep_l0(w_in_l0, q_lora_norm_l0, kv_lora_norm_l0, w_q_up_l0, w_kv_up_l0, mla_q_norm_l0, mla_k_norm_l0,
                  w_pool_l0, pool_scale_l0, w_out_l0, norm_mix_l0, norm_mlp_l0, w_mlp1_l0, w_mlp2_l0)
    a_p, q_p, k_p, v_p, ckv_p, kr_p = _l0_pre(xp, mod0_p, tiles_p, False, w0, None)
    a_s, q_s, k_s, v_s, _, _ = _l0_pre(xs, mod0_s, tiles_s, True, w0, _rope_tables(dec_seq))
    kr_cache = jnp.pad(cache_l0_mla_krope.reshape(dec_batch * past, MLA_ROPE), ((0, 0), (MLA_NOPE, LANES - MLA_QK)))
    k_c, v_c = _cache_kv(cache_l0_mla_ckv.reshape(dec_batch * past, KV_LORA), kr_cache, w0)
    mla_bound = _normed_len(mla_q_norm_l0 * (MLA_QK ** -0.5 * LOG2E), MLA_QK) * _normed_len(mla_k_norm_l0, MLA_QK)
    attn_p = _ctx_attn(_mla_ctx_kernel, q_p, k_p, v_p, mla_bound, seq, MLA_HEADS * MLA_V, "l0_mla_ctx_attn")
    attn_s, w16 = _mla_lat_attn(q_s, k_s, v_s, k_c, v_c, jnp.full((dec_batch, MLA_HEADS), mla_bound, F32),
                                dec_seq, past, (w0["w_out"], w0["w_mlp1"], w0["w_mlp2"], w_in_l1))
    w0 = dict(w0, w_out=w16[0], w_mlp1=w16[1], w_mlp2=w16[2])
    xp = _post(xp, attn_p, mod0_p, tiles_p, w0, "l0_post_ctx", pool=(a_p, seq))
    xs = _post(xs, attn_s, mod0_s, tiles_s, w0, "l0_post_latent", pool=(a_s, dec_seq))

    w1 = dict(norm_mix=norm_mix_l1.reshape(1, -1), norm_mlp=norm_mlp_l1.reshape(1, -1),
              w_in=w16[3],
              gq=jnp.tile(na_q_norm_l1 * (NA_HEAD_DIM ** -0.5 * LOG2E), 2).reshape(1, LANES),
              gk=jnp.tile(na_k_norm_l1, 2).reshape(1, LANES),
              w_out=w_out_l1, w_mlp1=w_mlp1_l1, w_mlp2=w_mlp2_l1)
    q1_p, k1_p, v1_p, k1_new, v1_new = _l1_pre(xp, mod1_p, tiles_p, True, w1)
    q1_s, k1_s, v1_s = _l1_pre(xs, mod1_s, tiles_s, False, w1)
    q_len, k_len = _normed_len(w1["gq"], NA_HEAD_DIM), _normed_len(w1["gk"], NA_HEAD_DIM)
    na_p = _ctx_attn(_na_ctx_kernel, q1_p, k1_p, v1_p, q_len * k_len, seq, NA_SLABS * LANES, "l1_na_ctx_attn")
    cache_len = jnp.sqrt(jnp.max(jnp.sum(jnp.square(cache_l1_na_k.astype(F32)), axis=-1), axis=1))
    na_bound = (q_len * jnp.maximum(k_len, cache_len)
                + jnp.max(jnp.abs(rel_bias_l1.astype(F32)) * LOG2E, axis=(1, 2))[None, :])
    na_s, w16 = _na_lat_attn(q1_s, k1_s, v1_s,
                             cache_l1_na_k.reshape(dec_batch * past, -1), cache_l1_na_v.reshape(dec_batch * past, -1),
                             _na_bias_table(rel_bias_l1, dec_seq // GRID_W), na_bound, dec_seq, past,
                             (w1["w_out"], w1["w_mlp1"], w1["w_mlp2"]))
    w1 = dict(w1, w_out=w16[0], w_mlp1=w16[1], w_mlp2=w16[2])
    xp = _post(xp, na_p, mod1_p, tiles_p, w1, "l1_post_ctx")
    xs = _post(xs, na_s, mod1_s, tiles_s, w1, "l1_post_latent")

    return (xp.reshape(batch, seq, D_MODEL), xs.reshape(dec_batch, dec_seq, D_MODEL),
            ckv_p.reshape(batch, seq, KV_LORA), kr_p.reshape(batch, seq, MLA_ROPE),
            k1_new.reshape(batch, seq, NA_HEADS, NA_HEAD_DIM), v1_new.reshape(batch, seq, NA_HEADS, NA_HEAD_DIM))
```

```python
import functools

import jax
import jax.numpy as jnp
from jax import lax
from jax.experimental import pallas as pl
from jax.experimental.pallas import tpu as pltpu

F32 = jnp.float32
BF16 = jnp.bfloat16

D_MODEL = 1024
D_FF = 4 * D_MODEL
DEPTH = 2
N_ADA = 6
EPS = 1e-6
NEG_INF = -1e30
ROPE_THETA = 10000.0
GRID_W = 64
POOL_WIDTH = 512
POOL_WINDOWS = (2, 4, 8, 16)
POOL_GROUP = 128
POOL_HALO = 8
MLA_HEADS = 8
MLA_NOPE = 64
MLA_ROPE = 32
MLA_QK = MLA_NOPE + MLA_ROPE
MLA_V = 64
Q_LORA = 256
KV_LORA = 128
NA_HEADS = 16
NA_HEAD_DIM = 64
NA_KH = 8
NA_KW = 16
LOG2E = 1.4426950408889634

LANES = 128
SUBLANES = 8
VMEM_LIMIT = 56 * 1024 * 1024

TOKEN_TILE = 512
MLA_Q_TILE = 512
MLA_STEP_HEADS = 2
MLA_KEY_CHUNK = 2048
MLA_ONLINE_CHUNK = 1024
SCORE_LOOKAHEAD = 4
MAX_SCORE_BOUND = 40.0
FF_CHUNK = 1024
CTX_STEP_SEQS = 2
NA_SLABS = NA_HEADS * NA_HEAD_DIM // LANES
NA_TILE_ROWS = 4
NA_WIN_ROWS = NA_TILE_ROWS + NA_KH - 1


def _params(n_axes):
    return pltpu.CompilerParams(dimension_semantics=("arbitrary",) * n_axes,
                                vmem_limit_bytes=VMEM_LIMIT)


def _const_spec(shape):
    zeros = (0,) * len(shape)
    return pl.BlockSpec(shape, lambda *_: zeros, pipeline_mode=pl.Buffered(1))


def _dot(a, b):
    return jnp.dot(a, b, preferred_element_type=F32)


def _dot_nt(a, b):
    return lax.dot_general(a, b, (((1,), (1,)), ((), ())), preferred_element_type=F32)


def _dot_tn(a, b):
    return lax.dot_general(a, b, (((0,), (0,)), ((), ())), preferred_element_type=F32)


def _rms(x, w, n=None):
    n = x.shape[-1] if n is None else n
    ss = jnp.sum(x * x, axis=-1, keepdims=True)
    return x * lax.rsqrt(ss * (1.0 / n) + EPS) * w


def _modulate(x, nw, shift, scale):
    return _rms(x, nw) * (1.0 + scale) + shift


def _ada_spec(mod, tiles_per_group):
    mods, first_row = mod
    return pl.BlockSpec((None, 1, mods.shape[-1]), lambda i: (first_row + i // tiles_per_group, 0, 0))


def _ada_row(mod_ref, k):
    return mod_ref[:, k * D_MODEL:(k + 1) * D_MODEL]


def _ada_kernel(cond_ref, w0_ref, w1_ref, b_ref, o_ref):
    c = cond_ref[...]
    s = (c * jax.nn.sigmoid(c)).astype(BF16)
    for layer, w_ref in enumerate((w0_ref, w1_ref)):
        @pl.when(pl.program_id(0) == layer)
        def _(w_ref=w_ref):
            o_ref[...] = _dot(s, w_ref[...].astype(BF16)) + b_ref[...]


def _ada(cond8, w_ada, b_ada):
    n_out = w_ada[0].shape[1]
    tn = 2 * D_MODEL
    nb = n_out // tn
    return pl.pallas_call(
        _ada_kernel,
        out_shape=jax.ShapeDtypeStruct((DEPTH, SUBLANES, n_out), F32),
        grid=(DEPTH, nb),
        in_specs=[pl.BlockSpec((SUBLANES, D_MODEL), lambda l, j: (0, 0)),
                  pl.BlockSpec((D_MODEL, tn), lambda l, j: (0, jnp.where(l == 0, j, nb - 1))),
                  pl.BlockSpec((D_MODEL, tn), lambda l, j: (0, jnp.where(l == 1, j, 0))),
                  pl.BlockSpec((None, 1, tn), lambda l, j: (l, 0, j))],
        out_specs=pl.BlockSpec((None, SUBLANES, tn), lambda l, j: (l, 0, j)),
        compiler_params=_params(2),
        name="ada",
    )(cond8, w_ada[0], w_ada[1], jnp.stack(b_ada).reshape(DEPTH, 1, n_out))


def _rope(x, c, s):
    return x * c + pltpu.roll(x, LANES - MLA_ROPE, 1) * s


PAIR = 2 * LANES


def _head_rinv(raw, e2_ref):
    ss = _dot((raw * raw).astype(BF16), e2_ref[...])
    return lax.rsqrt(ss * (1.0 / MLA_QK) + EPS)


def _mla_keys_values(ckv, kr_slab, wk_ref, gk, wv_ref, e2_ref, rope_tabs, k_ref, v_ref):
    c16 = ckv.astype(BF16)
    tail = kr_slab * gk
    if rope_tabs is not None:
        tail = _rope(tail, *rope_tabs)
    ones_lane = (lax.broadcasted_iota(jnp.int32, (1, LANES), 1) == MLA_V).astype(F32)
    for p in range(MLA_HEADS // 2):
        cols = pl.ds(p * PAIR, PAIR)
        kn = _dot(c16, wk_ref[:, cols])
        v = _dot(c16, wv_ref[:, cols])
        rinv = _head_rinv(kn + jnp.tile(kr_slab, (1, 2)), e2_ref)
        for i in range(2):
            hs = slice(i * LANES, (i + 1) * LANES)
            k_ref[2 * p + i] = (rinv[:, hs] * (kn[:, hs] * gk + tail)).astype(BF16)
            v_ref[2 * p + i] = (v[:, hs] + ones_lane).astype(BF16)


def _l0_pre_kernel(latent, x_ref, mod_ref, nw_ref, win_ref, qln_ref, kvn_ref, wq_ref, gq_ref,
                   wk_ref, gk_ref, wv_ref, e2_ref, *rest):
    if latent:
        rope_ref, a_ref, q_ref, k_ref, v_ref, ckv_ref, kr_ref = rest
        tile_rows = x_ref.shape[0] // GRID_W
        row0 = (pl.program_id(0) * tile_rows) % rope_ref.shape[1]
        expand = lambda col, row: jnp.concatenate(
            [rope_ref[col] + rope_ref[row, pl.ds(row0 + r, 1), :] for r in range(tile_rows)], axis=0)
        rope_tabs = (expand(0, 1), expand(2, 3))
    else:
        a_ref, q_ref, k_ref, v_ref, ckv_ref, kr_ref = rest
        rope_tabs = None
    h = _modulate(x_ref[...], nw_ref[...], _ada_row(mod_ref, 0), _ada_row(mod_ref, 1)).astype(BF16)
    proj = _dot(h, win_ref[...])
    a_ref[...] = proj[:, :POOL_WIDTH]
    q_lat = proj[:, POOL_WIDTH:POOL_WIDTH + Q_LORA]
    kv_lat = proj[:, POOL_WIDTH + Q_LORA:POOL_WIDTH + Q_LORA + KV_LORA]
    kr_slab = proj[:, POOL_WIDTH + Q_LORA + KV_LORA:]
    ql = _rms(q_lat, qln_ref[...]).astype(BF16)
    gq = gq_ref[...]
    for p in range(MLA_HEADS // 2):
        q = _dot(ql, wq_ref[:, pl.ds(p * PAIR, PAIR)])
        q_rinv = _head_rinv(q, e2_ref)
        for i in range(2):
            hs = slice(i * LANES, (i + 1) * LANES)
            y = q[:, hs] * q_rinv[:, hs] * gq
            if rope_tabs is not None:
                y = _rope(y, *rope_tabs)
            q_ref[2 * p + i] = y.astype(BF16)
    ckv = _rms(kv_lat, kvn_ref[...])
    ckv_ref[...] = ckv
    kr_ref[...] = kr_slab[:, MLA_NOPE:MLA_QK]
    _mla_keys_values(ckv, kr_slab, wk_ref, gk_ref[...], wv_ref, e2_ref, rope_tabs, k_ref, v_ref)


def _slab_spec(slabs, tm):
    return pl.BlockSpec((slabs, tm, LANES), lambda i: (0, i, 0))


def _l0_pre(x, mod, tiles_per_group, latent, w, rope_tabs):
    n = x.shape[0]
    tm = TOKEN_TILE
    row = lambda i: (i, 0)
    hk = MLA_HEADS * LANES
    in_specs = [pl.BlockSpec((tm, D_MODEL), row),
                _ada_spec(mod, tiles_per_group),
                _const_spec((1, D_MODEL)),
                _const_spec((D_MODEL, D_MODEL)),
                _const_spec((1, Q_LORA)),
                _const_spec((1, KV_LORA)),
                _const_spec((Q_LORA, hk)),
                _const_spec((1, LANES)),
                _const_spec((KV_LORA, hk)),
                _const_spec((1, LANES)),
                _const_spec((KV_LORA, hk)),
                _const_spec((2 * LANES, 2 * LANES))]
    gains = "rope" if latent else "plain"
    args = [x, mod[0], w["norm_mix"], w["w_in"], w["qln"], w["kvn"], w["wq"], w["gq_" + gains], w["wk"],
            w["gk_" + gains], w["wv"], w["e2"]]
    if latent:
        assert tm % GRID_W == 0
        in_specs.append(_const_spec(rope_tabs.shape))
        args.append(rope_tabs)
    slab = jax.ShapeDtypeStruct((MLA_HEADS, n, LANES), BF16)
    out_shape = (jax.ShapeDtypeStruct((n, POOL_WIDTH), F32), slab, slab, slab,
                 jax.ShapeDtypeStruct((n, KV_LORA), F32),
                 jax.ShapeDtypeStruct((n, MLA_ROPE), F32))
    out_specs = (pl.BlockSpec((tm, POOL_WIDTH), row),
                 _slab_spec(MLA_HEADS, tm), _slab_spec(MLA_HEADS, tm), _slab_spec(MLA_HEADS, tm),
                 pl.BlockSpec((tm, KV_LORA), row), pl.BlockSpec((tm, MLA_ROPE), row))
    return pl.pallas_call(
        functools.partial(_l0_pre_kernel, latent),
        out_shape=out_shape, grid=(n // tm,), in_specs=in_specs, out_specs=out_specs,
        compiler_params=_params(1), name="l0_pre_latent" if latent else "l0_pre_ctx",
    )(*args)


def _cache_kv_kernel(ckv_ref, kr_ref, wk_ref, gk_ref, wv_ref, e2_ref, k_ref, v_ref):
    _mla_keys_values(ckv_ref[...], kr_ref[...], wk_ref, gk_ref[...], wv_ref, e2_ref, None, k_ref, v_ref)


def _cache_kv(ckv, kr_slab, w):
    n = ckv.shape[0]
    tm = TOKEN_TILE
    row = lambda i: (i, 0)
    hk = MLA_HEADS * LANES
    slab = jax.ShapeDtypeStruct((MLA_HEADS, n, LANES), BF16)
    return pl.pallas_call(
        _cache_kv_kernel,
        out_shape=(slab, slab),
        grid=(n // tm,),
        in_specs=[pl.BlockSpec((tm, KV_LORA), row), pl.BlockSpec((tm, LANES), row),
                  _const_spec((KV_LORA, hk)), _const_spec((1, LANES)), _const_spec((KV_LORA, hk)),
                  _const_spec((2 * LANES, 2 * LANES))],
        out_specs=(_slab_spec(MLA_HEADS, tm), _slab_spec(MLA_HEADS, tm)),
        compiler_params=_params(1), name="l0_cache_kv",
    )(ckv, kr_slab, w["wk"], w["gk_plain"], w["wv"], w["e2"])


def _pool_mixer(seq_len, tile, prev_ref, u_ref, next_ref, wp_ref, ps_ref, buf_ref):
    tm = u_ref.shape[0]
    piece = min(tm, seq_len)
    n_pieces = tm // piece
    stride = piece + 2 * POOL_HALO
    starts = []
    for j in range(n_pieces):
        pos0 = (tile * tm + j * piece) % seq_len
        prev = prev_ref[...] if j == 0 else u_ref[j * piece - POOL_HALO:j * piece, :]
        nxt = next_ref[...] if j == n_pieces - 1 else u_ref[(j + 1) * piece:(j + 1) * piece + POOL_HALO, :]
        base = j * stride
        buf_ref[base:base + POOL_HALO, :] = jnp.where(pos0 != 0, prev, 0.0)
        buf_ref[base + POOL_HALO:base + POOL_HALO + piece, :] = u_ref[j * piece:(j + 1) * piece, :]
        buf_ref[base + POOL_HALO + piece:base + stride, :] = jnp.where(pos0 + piece != seq_len, nxt, 0.0)
        starts.append(pos0)
    def group(g):
        win = POOL_WINDOWS[g]
        lanes = pl.ds(g * POOL_GROUP, POOL_GROUP)
        pooled = []
        for j in range(n_pieces):
            centre = j * stride + POOL_HALO
            lo = starts[j] + lax.broadcasted_iota(jnp.int32, (piece, 1), 0) - win // 2
            cnt = (jnp.minimum(lo + win, seq_len) - jnp.maximum(lo, 0)).astype(F32)
            acc = buf_ref[pl.ds(centre - win // 2, piece), lanes]
            for d in range(1 - win // 2, win - win // 2):
                acc = acc + buf_ref[pl.ds(centre + d, piece), lanes]
            pooled.append(acc / cnt - buf_ref[pl.ds(centre, piece), lanes])
        pooled = pooled[0] if n_pieces == 1 else jnp.concatenate(pooled, axis=0)
        return (_dot(pooled.astype(BF16), wp_ref[g]) * ps_ref[:, lanes]).astype(BF16)

    return [functools.partial(group, g) for g in range(len(POOL_WINDOWS))]


def _attend(streams, running_max):
    items = [(si, ch) for si, (_, chunks, _) in enumerate(streams) for ch in chunks]

    def score(item):
        si, (k_fn, _, extra_fn) = item
        s = _dot_nt(k_fn(), streams[si][0])
        return s if extra_fn is None else s + extra_fn()

    state = [None] * len(streams)
    pending = [score(item) for item in items[:SCORE_LOOKAHEAD]]
    for i, (si, (_, v_fn, _)) in enumerate(items):
        s = pending.pop(0)
        if i + SCORE_LOOKAHEAD < len(items):
            pending.append(score(items[i + SCORE_LOOKAHEAD]))
        want_den = streams[si][2]
        m_new, alpha = None, 1.0
        if state[si] is not None:
            m, acc, den = state[si]
        if running_max:
            m_new = jnp.max(s, axis=0, keepdims=True)
            if state[si] is not None:
                m_new = jnp.maximum(m, m_new)
                alpha = jnp.exp2(m - m_new)
            s = s - m_new
        p = jnp.exp2(s)
        pv = _dot_tn(v_fn(), p.astype(BF16))
        psum = jnp.sum(p, axis=0, keepdims=True) if want_den else None
        if state[si] is None:
            state[si] = (m_new, pv, psum)
        elif running_max:
            state[si] = (m_new, acc * alpha + pv, den * alpha + psum if want_den else None)
        else:
            state[si] = (None, acc + pv, den + psum if want_den else None)
    return [(acc, den) for _, acc, den in state]


def _ref_chunks(k_ref, v_ref, slab, chunk, extra_fn=None, lo=0, hi=None):
    hi = k_ref.shape[1] if hi is None else hi
    out = []
    for c in range(lo, hi, chunk):
        rows = pl.ds(c, min(chunk, hi - c))
        out.append((lambda rows=rows: k_ref[slab, rows, :], lambda rows=rows: v_ref[slab, rows, :], extra_fn))
    return out


def _mla_finish(o_t):
    return o_t[:MLA_V] / o_t[MLA_V:MLA_V + 1]


def _na_queries(q2):
    low = lax.broadcasted_iota(jnp.int32, (1, LANES), 1) < NA_HEAD_DIM
    zero = jnp.zeros_like(q2)
    return jnp.concatenate([jnp.where(low, q2, zero), jnp.where(low, zero, q2)], axis=0)


def _na_finish(o_t, den):
    nq = o_t.shape[1] // 2
    o = o_t / den
    return jnp.concatenate([o[:NA_HEAD_DIM, :nq], o[NA_HEAD_DIM:, nq:]], axis=0)


def _attend_guarded(bound, streams_fn):
    return lax.cond(bound <= MAX_SCORE_BOUND,
                    lambda: _attend(streams_fn(False), False),
                    lambda: _attend(streams_fn(True), True))


def _mla_ctx_kernel(seq, bound_ref, q_ref, k_ref, v_ref, o_ref):
    spans = [(lo, lo + seq) for lo in range(0, k_ref.shape[1], seq)]
    res = _attend_guarded(bound_ref[0, 0], lambda _: [
        (q_ref[h, lo:hi, :], _ref_chunks(k_ref, v_ref, h, seq, lo=lo, hi=hi), False)
        for lo, hi in spans for h in range(MLA_HEADS)])
    for i, (lo, hi) in enumerate(spans):
        for hp in range(MLA_HEADS // 2):
            o_t = jnp.concatenate([_mla_finish(res[i * MLA_HEADS + h][0]) for h in (2 * hp, 2 * hp + 1)], axis=0)
            o_ref[lo:hi, hp * LANES:(hp + 1) * LANES] = o_t.T.astype(BF16)


def _na_ctx_kernel(seq, bound_ref, q_ref, k_ref, v_ref, o_ref):
    spans = [(lo, lo + seq) for lo in range(0, k_ref.shape[1], seq)]
    res = _attend_guarded(bound_ref[0, 0], lambda _: [
        (_na_queries(q_ref[s, lo:hi, :]), _ref_chunks(k_ref, v_ref, s, seq, lo=lo, hi=hi), True)
        for lo, hi in spans for s in range(NA_SLABS)])
    for i, (lo, hi) in enumerate(spans):
        for s in range(NA_SLABS):
            o_ref[lo:hi, s * LANES:(s + 1) * LANES] = _na_finish(*res[i * NA_SLABS + s]).T.astype(BF16)


def _ctx_attn(body, q, k, v, bound, seq, width, name):
    slabs, n, _ = q.shape
    rows = CTX_STEP_SEQS * seq
    spec = pl.BlockSpec((slabs, rows, LANES), lambda b: (0, b, 0))
    return pl.pallas_call(
        functools.partial(body, seq),
        out_shape=jax.ShapeDtypeStruct((n, width), BF16),
        grid=(n // rows,),
        in_specs=[pl.BlockSpec(memory_space=pltpu.SMEM), spec, spec, spec],
        out_specs=pl.BlockSpec((rows, width), lambda b: (b, 0)),
        compiler_params=_params(1), name=name,
    )(bound.reshape(1, 1), q, k, v)


SIDE_PIECE_ELEMS = 512 * 1024


def _side_piece_rows(n_cols):
    return 1 << ((SIDE_PIECE_ELEMS // n_cols).bit_length() - 1)


def _side_cast_plan(shapes):
    plan = []
    for wi, (n_rows, n_cols) in enumerate(shapes):
        rows = _side_piece_rows(n_cols)
        assert n_rows % rows == 0
        plan += [(wi, r0, rows) for r0 in range(0, n_rows, rows)]
    return plan


def _side_cast_scratch(shapes):
    widths = sorted({c for _, c in shapes})
    stages = [pltpu.VMEM((2, _side_piece_rows(c), c), dt) for dt in (F32, BF16) for c in widths]
    return stages + [pltpu.SemaphoreType.DMA((2,)), pltpu.SemaphoreType.DMA((2,))]


def _side_cast(step, srcs, dsts, scratch):
    shapes = [s.shape for s in srcs]
    widths = sorted({c for _, c in shapes})
    stage_in = dict(zip(widths, scratch[:len(widths)]))
    stage_out = dict(zip(widths, scratch[len(widths):2 * len(widths)]))
    sem_in, sem_out = scratch[2 * len(widths):]
    plan = _side_cast_plan(shapes)

    def refs(k):
        wi, r0, rows = plan[k]
        cols = shapes[wi][1]
        return (srcs[wi].at[pl.ds(r0, rows), :], dsts[wi].at[pl.ds(r0, rows), :],
                stage_in[cols].at[k % 2], stage_out[cols].at[k % 2])

    read = lambda k: pltpu.make_async_copy(refs(k)[0], refs(k)[2], sem_in.at[k % 2])
    write = lambda k: pltpu.make_async_copy(refs(k)[3], refs(k)[1], sem_out.at[k % 2])
    n = len(plan)

    def piece_step(k):
        if k == 0:
            read(0).start()
        if k >= 2:
            write(k - 2).wait()
        if k < n:
            read(k).wait()
            if k + 1 < n:
                read(k + 1).start()
            refs(k)[3][...] = refs(k)[2][...].astype(BF16)
            write(k).start()

    @pl.when(step < n + 2)
    def _():
        for k in range(n + 2):
            pl.when(step == k)(functools.partial(piece_step, k))


def _mla_lat_kernel(n_cast, bound_ref, q_ref, kl_ref, vl_ref, kc_ref, vc_ref, *rest):
    o_ref = rest[n_cast]
    linear_step = (pl.program_id(0) * pl.num_programs(1) + pl.program_id(1)) * pl.num_programs(2) + pl.program_id(2)
    _side_cast(linear_step, rest[:n_cast], rest[n_cast + 1:2 * n_cast + 1], rest[2 * n_cast + 1:])
    batch, head0 = pl.program_id(0), pl.program_id(1) * MLA_STEP_HEADS
    heads = range(MLA_STEP_HEADS)
    bounds = [bound_ref[batch, head0 + h] for h in heads]

    def streams(running_max):
        size = MLA_ONLINE_CHUNK if running_max else MLA_KEY_CHUNK
        return [(q_ref[h], _ref_chunks(kl_ref, vl_ref, h, size) + _ref_chunks(kc_ref, vc_ref, h, size), False)
                for h in heads]

    res = _attend_guarded(functools.reduce(jnp.maximum, bounds), streams)
    o_ref[...] = jnp.concatenate([_mla_finish(o) for o, _ in res], axis=0).T.astype(BF16)


def _normed_len(gain, dim):
    return (dim ** 0.5) * jnp.max(jnp.abs(gain.astype(F32)))


def _side_cast_specs(weights, n_steps):
    shapes = [wt.shape for wt in weights]
    assert n_steps >= len(_side_cast_plan(shapes)) + 2
    hbm = pl.BlockSpec(memory_space=pl.ANY)
    return ([hbm] * len(weights), [jax.ShapeDtypeStruct(s, BF16) for s in shapes], [hbm] * len(weights),
            _side_cast_scratch(shapes))


def _mla_lat_attn(q, k, v, kc, vc, bound, seq, past, weights):
    n = q.shape[1]
    tq = MLA_Q_TILE
    nq = seq // tq
    hs = MLA_STEP_HEADS
    grid = (n // seq, MLA_HEADS // hs, nq)
    w_in, w_shapes, w_out, w_scratch = _side_cast_specs(weights, grid[0] * grid[1] * grid[2])
    outs = pl.pallas_call(
        functools.partial(_mla_lat_kernel, len(weights)),
        out_shape=(jax.ShapeDtypeStruct((n, MLA_HEADS * MLA_V), BF16), *w_shapes),
        grid=grid,
        in_specs=[pl.BlockSpec(memory_space=pltpu.SMEM),
                  pl.BlockSpec((hs, tq, LANES), lambda b, h, i: (h, b * nq + i, 0)),
                  pl.BlockSpec((hs, seq, LANES), lambda b, h, i: (h, b, 0)),
                  pl.BlockSpec((hs, seq, LANES), lambda b, h, i: (h, b, 0)),
                  pl.BlockSpec((hs, past, LANES), lambda b, h, i: (h, b, 0)),
                  pl.BlockSpec((hs, past, LANES), lambda b, h, i: (h, b, 0))] + w_in,
        out_specs=(pl.BlockSpec((tq, hs * MLA_V), lambda b, h, i: (b * nq + i, h)), *w_out),
        scratch_shapes=w_scratch,
        compiler_params=_params(3), name="l0_mla_latent_attn",
    )(bound, q, k, v, kc, vc, *weights)
    return outs[0], outs[1:]


def _na_first_row(r, rows):
    return jnp.clip(r - NA_KH // 2, 0, rows - NA_KH)


def _na_lat_kernel(rows, n_cast, bound_ref, q_ref, k_ref, v_ref, kc_ref, vc_ref, t2_ref, *rest):
    o_ref = rest[n_cast]
    _side_cast(pl.program_id(0) * pl.num_programs(1) + pl.program_id(1),
               rest[:n_cast], rest[n_cast + 1:2 * n_cast + 1], rest[2 * n_cast + 1:])
    nq = NA_TILE_ROWS * GRID_W
    nk = NA_WIN_ROWS * GRID_W
    n_off = 2 * NA_KH - 1
    r0 = pl.program_id(1) * NA_TILE_ROWS
    w0 = jnp.clip(r0 - NA_KH // 2, 0, rows - NA_WIN_ROWS)

    def block_index(j, t):
        key_row = w0 + j
        i = r0 + 2 * t
        first_u, first_l = _na_first_row(i, rows), _na_first_row(i + 1, rows)
        in_u = (key_row >= first_u) & (key_row < first_u + NA_KH)
        in_l = (key_row >= first_l) & (key_row < first_l + NA_KH)
        d = jnp.clip(key_row - i + NA_KH - 1, 0, n_off - 1)
        return jnp.where(in_u & in_l, d, jnp.where(in_u, n_off, jnp.where(in_l, n_off + 1, n_off + 2)))

    idx = [[block_index(j, t) for t in range(NA_TILE_ROWS // 2)] for j in range(NA_WIN_ROWS)]

    def local_bias(s):
        heads = [jnp.concatenate([jnp.concatenate([t2_ref[2 * s + hh, i] for i in row], axis=1) for row in idx], axis=0)
                 for hh in range(2)]
        return jnp.concatenate(heads, axis=1)

    batch = pl.program_id(0)
    bounds = [bound_ref[batch, h] for h in range(NA_HEADS)]

    def cache_chunk(s):
        lanes = pl.ds(s * LANES, LANES)
        return [(lambda: kc_ref[:, lanes].astype(BF16), lambda: vc_ref[:, lanes].astype(BF16), None)]

    res = _attend_guarded(
        functools.reduce(jnp.maximum, bounds),
        lambda _: [(_na_queries(q_ref[s]),
                    _ref_chunks(k_ref, v_ref, s, nk, functools.partial(local_bias, s)) + cache_chunk(s),
                    True) for s in range(NA_SLABS)])
    for s in range(NA_SLABS):
        o_ref[:, s * LANES:(s + 1) * LANES] = _na_finish(*res[s]).T.astype(BF16)


def _na_lat_attn(q, k, v, kc, vc, t2, bound, seq, past, weights):
    slabs, n, _ = q.shape
    rows = seq // GRID_W
    tiles = rows // NA_TILE_ROWS
    nq = NA_TILE_ROWS * GRID_W
    nk = NA_WIN_ROWS * GRID_W

    def win_map(b, t):
        w0 = jnp.clip(t * NA_TILE_ROWS - NA_KH // 2, 0, rows - NA_WIN_ROWS)
        return (0, pl.multiple_of(b * seq + w0 * GRID_W, GRID_W), 0)

    win_spec = pl.BlockSpec((pl.Element(slabs), pl.Element(nk), pl.Element(LANES)), win_map)
    ctx_spec = pl.BlockSpec((past, slabs * LANES), lambda b, t: (b, 0))
    w_in, w_shapes, w_out, w_scratch = _side_cast_specs(weights, (n // seq) * tiles)
    outs = pl.pallas_call(
        functools.partial(_na_lat_kernel, rows, len(weights)),
        out_shape=(jax.ShapeDtypeStruct((n, slabs * LANES), BF16), *w_shapes),
        grid=(n // seq, tiles),
        in_specs=[pl.BlockSpec(memory_space=pltpu.SMEM),
                  pl.BlockSpec((slabs, nq, LANES), lambda b, t: (0, b * tiles + t, 0)),
                  win_spec, win_spec, ctx_spec, ctx_spec,
                  _const_spec(t2.shape)] + w_in,
        out_specs=(pl.BlockSpec((nq, slabs * LANES), lambda b, t: (b * tiles + t, 0)), *w_out),
        scratch_shapes=w_scratch,
        compiler_params=_params(2), name="l1_na_latent_attn",
    )(bound, q, k, v, kc, vc, t2, *weights)
    return outs[0], outs[1:]


def _pair_norm(x2, g2):
    low = lax.broadcasted_iota(jnp.int32, (1, LANES), 1) < NA_HEAD_DIM
    sq = x2 * x2
    ss_all = jnp.sum(sq, axis=-1, keepdims=True)
    ss_low = jnp.sum(jnp.where(low, sq, 0.0), axis=-1, keepdims=True)
    ss = jnp.where(low, ss_low, ss_all - ss_low)
    return x2 * lax.rsqrt(ss * (1.0 / NA_HEAD_DIM) + EPS) * g2


def _l1_pre_kernel(keep_f32, x_ref, mod_ref, nw_ref, win_ref, gq_ref, gk_ref, q_ref, k_ref, v_ref, *f32_refs):
    h = _modulate(x_ref[...], nw_ref[...], _ada_row(mod_ref, 0), _ada_row(mod_ref, 1)).astype(BF16)
    width = NA_SLABS * LANES
    q = _dot(h, win_ref[:, 0:width])
    k = _dot(h, win_ref[:, width:2 * width])
    v = _dot(h, win_ref[:, 2 * width:3 * width])
    gq = gq_ref[...]
    gk = gk_ref[...]
    for s in range(NA_SLABS):
        ls = slice(s * LANES, (s + 1) * LANES)
        q_ref[s] = _pair_norm(q[:, ls], gq).astype(BF16)
        kn = _pair_norm(k[:, ls], gk)
        k_ref[s] = kn.astype(BF16)
        v_ref[s] = v[:, ls].astype(BF16)
        if keep_f32:
            f32_refs[0][:, ls] = kn
    if keep_f32:
        f32_refs[1][...] = v


def _l1_pre(x, mod, tiles_per_group, keep_f32, w):
    n = x.shape[0]
    tm = TOKEN_TILE
    row = lambda i: (i, 0)
    width = NA_SLABS * LANES
    slab = jax.ShapeDtypeStruct((NA_SLABS, n, LANES), BF16)
    out_shape = [slab, slab, slab]
    out_specs = [_slab_spec(NA_SLABS, tm)] * 3
    if keep_f32:
        out_shape += [jax.ShapeDtypeStruct((n, width), F32)] * 2
        out_specs += [pl.BlockSpec((tm, width), row)] * 2
    return pl.pallas_call(
        functools.partial(_l1_pre_kernel, keep_f32),
        out_shape=tuple(out_shape), grid=(n // tm,),
        in_specs=[pl.BlockSpec((tm, D_MODEL), row),
                  _ada_spec(mod, tiles_per_group),
                  _const_spec((1, D_MODEL)),
                  _const_spec((D_MODEL, 3 * width)),
                  _const_spec((1, LANES)),
                  _const_spec((1, LANES))],
        out_specs=tuple(out_specs),
        compiler_params=_params(1), name="l1_pre_ctx" if keep_f32 else "l1_pre_latent",
    )(x, mod[0], w["norm_mix"], w["w_in"], w["gq"], w["gk"])


def _post_kernel(pool_seq, x_ref, attn_ref, *rest):
    rest = list(rest)
    take = lambda k: [rest.pop(0) for _ in range(k)]
    pool_in = take(8) if pool_seq is not None else None
    mod_ref, nw_ref, wo_ref, w1_ref, w2_ref, o_ref = take(6)
    pool_scratch = take(2) if pool_seq is not None else None
    step = pl.program_id(0)
    n_chunks = D_FF // FF_CHUNK
    if pool_seq is not None:
        first_refs, next_refs, (wp_ref, ps_ref) = pool_in[0:3], pool_in[3:6], pool_in[6:8]
        buf_ref, pooled_ref = pool_scratch

        @pl.when(step == 0)
        def _():
            for g, group in enumerate(_pool_mixer(pool_seq, 0, *first_refs, wp_ref, ps_ref, buf_ref)):
                pooled_ref[:, g * POOL_GROUP:(g + 1) * POOL_GROUP] = group()

        mixed = _dot(jnp.concatenate([pooled_ref[...], attn_ref[...]], axis=-1), wo_ref[...])
        next_groups = _pool_mixer(pool_seq, jnp.minimum(step + 1, pl.num_programs(0) - 1), *next_refs,
                                  wp_ref, ps_ref, buf_ref)
    else:
        mixed = _dot(attn_ref[...], wo_ref[...])
    x1 = x_ref[...] + _ada_row(mod_ref, 2) * mixed
    h = _modulate(x1, nw_ref[...], _ada_row(mod_ref, 3), _ada_row(mod_ref, 4)).astype(BF16)
    acc = jnp.zeros(x1.shape, F32)
    for c in range(n_chunks):
        cs = pl.ds(c * FF_CHUNK, FF_CHUNK)
        u = jnp.square(jnp.maximum(_dot(h, w1_ref[:, cs]), 0.0)).astype(BF16)
        acc = acc + _dot(u, w2_ref[cs, :])
        if pool_seq is not None:
            for g in range(c * len(next_groups) // n_chunks, (c + 1) * len(next_groups) // n_chunks):
                pooled_ref[:, g * POOL_GROUP:(g + 1) * POOL_GROUP] = next_groups[g]()
    o_ref[...] = x1 + _ada_row(mod_ref, 5) * acc


def _post(x, attn, mod, tiles_per_group, w, name, pool=None):
    n = x.shape[0]
    tm = TOKEN_TILE
    row = lambda i: (i, 0)
    in_specs = [pl.BlockSpec((tm, D_MODEL), row), pl.BlockSpec((tm, attn.shape[1]), row)]
    args = [x, attn]
    scratch = []
    pool_seq = None
    if pool is not None:
        a_in, pool_seq = pool
        hb = tm // POOL_HALO
        last = n // POOL_HALO - 1
        pieces = tm // min(tm, pool_seq)
        last_tile = n // tm - 1

        def tile_specs(tile_of):
            return [pl.BlockSpec((POOL_HALO, POOL_WIDTH), lambda i: (jnp.maximum(tile_of(i) * hb - 1, 0), 0)),
                    pl.BlockSpec((tm, POOL_WIDTH), lambda i: (tile_of(i), 0)),
                    pl.BlockSpec((POOL_HALO, POOL_WIDTH), lambda i: (jnp.minimum((tile_of(i) + 1) * hb, last), 0))]

        in_specs += tile_specs(lambda i: 0) + tile_specs(lambda i: jnp.minimum(i + 1, last_tile))
        in_specs += [_const_spec((len(POOL_WINDOWS), POOL_GROUP, POOL_GROUP)), _const_spec((1, POOL_WIDTH))]
        args += [a_in] * 6 + [w["w_pool"], w["pool_scale"]]
        scratch = [pltpu.VMEM((tm + 2 * POOL_HALO * pieces, POOL_WIDTH), F32), pltpu.VMEM((tm, POOL_WIDTH), BF16)]
    in_specs += [_ada_spec(mod, tiles_per_group),
                 _const_spec((1, D_MODEL)),
                 _const_spec((D_MODEL, D_MODEL)),
                 _const_spec((D_MODEL, D_FF)),
                 _const_spec((D_FF, D_MODEL))]
    args += [mod[0], w["norm_mlp"], w["w_out"], w["w_mlp1"], w["w_mlp2"]]
    return pl.pallas_call(
        functools.partial(_post_kernel, pool_seq),
        out_shape=jax.ShapeDtypeStruct((n, D_MODEL), F32), grid=(n // tm,),
        in_specs=in_specs, out_specs=pl.BlockSpec((tm, D_MODEL), row),
        scratch_shapes=scratch, compiler_params=_params(1), name=name,
    )(*args)


def _rope_tables(seq):
    assert seq == GRID_W * GRID_W
    half = MLA_ROPE // 2
    inv_freq = jnp.power(ROPE_THETA, -jnp.arange(0, half, 2, dtype=F32) / half)
    ang = jnp.arange(GRID_W, dtype=F32)[:, None] * inv_freq[None, :]
    cos, sin = jnp.cos(ang), jnp.sin(ang)
    fill = lambda n, v=0.0: jnp.full((GRID_W, n), v, F32)
    spare = LANES - MLA_QK
    return jnp.stack([
        jnp.concatenate([fill(MLA_NOPE, 1.0), fill(half), cos, cos, fill(spare)], axis=1),
        jnp.concatenate([fill(MLA_NOPE), cos, cos, fill(half), fill(spare)], axis=1),
        jnp.concatenate([fill(MLA_NOPE), fill(half), -sin, sin, fill(spare)], axis=1),
        jnp.concatenate([fill(MLA_NOPE), -sin, sin, fill(half), fill(spare)], axis=1)])


def _rope_partner(tail):
    shape = tail.shape
    t = tail.reshape(shape[:-1] + (2, 2, MLA_ROPE // 4))
    return t[..., ::-1, :].reshape(shape)


def _na_bias_table(rel_bias, rows):
    qcol = jnp.arange(GRID_W)
    kcol = jnp.arange(GRID_W)
    col_start = jnp.clip(qcol - NA_KW // 2, 0, GRID_W - NA_KW)
    col_mask = (kcol[:, None] >= col_start[None, :]) & (kcol[:, None] < col_start[None, :] + NA_KW)
    offs = jnp.arange(2 * NA_KW - 1)
    onehot = (kcol[None, :, None] - qcol[None, None, :] + NA_KW - 1 == offs[:, None, None]).astype(F32)
    blocks = jnp.einsum('hdj,jkq->hdkq', rel_bias.astype(F32) * LOG2E, onehot, precision=lax.Precision.HIGHEST)
    blocks = jnp.where(col_mask[None, None], blocks, NEG_INF)
    neg = jnp.full_like(blocks[:, :1], NEG_INF)
    both = jnp.concatenate([blocks, jnp.concatenate([neg, blocks[:, :-1]], axis=1)], axis=-1)
    d_upper, d_lower = _na_partial_offsets(rows)
    upper_only = jnp.concatenate([blocks[:, d_upper:d_upper + 1], neg], axis=-1)
    lower_only = jnp.concatenate([neg, blocks[:, d_lower - 1:d_lower]], axis=-1)
    return jnp.concatenate([both, upper_only, lower_only, jnp.concatenate([neg, neg], axis=-1)], axis=1)


def _na_partial_offsets(rows):
    first = lambda r: min(max(r - NA_KH // 2, 0), rows - NA_KH)
    upper, lower = set(), set()
    for r0 in range(0, rows, NA_TILE_ROWS):
        w0 = min(max(r0 - NA_KH // 2, 0), rows - NA_WIN_ROWS)
        for j in range(NA_WIN_ROWS):
            for t in range(NA_TILE_ROWS // 2):
                key_row, i = w0 + j, r0 + 2 * t
                in_u = first(i) <= key_row < first(i) + NA_KH
                in_l = first(i + 1) <= key_row < first(i + 1) + NA_KH
                d = key_row - i + NA_KH - 1
                if in_u and in_l:
                    assert 1 <= d <= 2 * NA_KH - 2
                elif in_u:
                    upper.add(d)
                elif in_l:
                    lower.add(d)
    assert len(upper) == 1 and len(lower) == 1, (upper, lower)
    return upper.pop(), lower.pop()


def _prep_l0(w_in, q_lora_norm, kv_lora_norm, w_q_up, w_kv_up, mla_q_norm, mla_k_norm, w_pool, pool_scale,
             w_out, norm_mix, norm_mlp, w_mlp1, w_mlp2):
    lat_end = POOL_WIDTH + Q_LORA + KV_LORA
    w_in_pad = jnp.concatenate([w_in[:, :lat_end], jnp.zeros((D_MODEL, MLA_NOPE), w_in.dtype),
                                w_in[:, lat_end:], _rope_partner(w_in[:, lat_end:])], axis=1)
    pad = LANES - MLA_QK
    hk = MLA_HEADS * LANES
    wq = w_q_up.reshape(Q_LORA, MLA_HEADS, MLA_QK)
    wq = jnp.concatenate([wq, _rope_partner(wq[:, :, MLA_NOPE:])], axis=-1)
    kv = w_kv_up.reshape(KV_LORA, MLA_HEADS, MLA_NOPE + MLA_V)
    wk = jnp.pad(kv[:, :, :MLA_NOPE], ((0, 0), (0, 0), (0, LANES - MLA_NOPE)))
    wv = jnp.pad(kv[:, :, MLA_NOPE:], ((0, 0), (0, 0), (0, LANES - MLA_V)))
    gq = mla_q_norm * (MLA_QK ** -0.5 * LOG2E)
    slab_lane = jnp.arange(2 * LANES)
    e2 = (slab_lane[:, None] // LANES == slab_lane[None, :] // LANES) & (slab_lane[:, None] % LANES < MLA_QK)
    return dict(
        norm_mix=norm_mix.reshape(1, -1), norm_mlp=norm_mlp.reshape(1, -1),
        w_in=w_in_pad.astype(BF16),
        qln=q_lora_norm.reshape(1, -1), kvn=kv_lora_norm.reshape(1, -1),
        wq=wq.reshape(Q_LORA, hk).astype(BF16),
        gq_plain=jnp.pad(gq, (0, pad)).reshape(1, LANES),
        gq_rope=jnp.concatenate([gq, _rope_partner(gq[MLA_NOPE:])]).reshape(1, LANES),
        wk=wk.reshape(KV_LORA, hk).astype(BF16),
        gk_plain=jnp.pad(mla_k_norm, (0, pad)).reshape(1, LANES),
        gk_rope=jnp.concatenate([mla_k_norm, _rope_partner(mla_k_norm[MLA_NOPE:])]).reshape(1, LANES),
        wv=wv.reshape(KV_LORA, hk).astype(BF16),
        e2=e2.astype(BF16),
        w_pool=w_pool.astype(BF16), pool_scale=pool_scale.reshape(1, -1),
        w_out=w_out, w_mlp1=w_mlp1, w_mlp2=w_mlp2)


def kernel(x_prompt, x_sample, cache_l0_mla_ckv, cache_l0_mla_krope, cache_l1_na_k, cache_l1_na_v, c, c_ctx, w_ada_l0, b_ada_l0, norm_mix_l0, norm_mlp_l0, w_mlp1_l0, w_mlp2_l0, w_in_l0, q_lora_norm_l0, kv_lora_norm_l0, w_q_up_l0, w_kv_up_l0, mla_q_norm_l0, mla_k_norm_l0, w_pool_l0, pool_scale_l0, w_out_l0, w_ada_l1, b_ada_l1, norm_mix_l1, norm_mlp_l1, w_mlp1_l1, w_mlp2_l1, w_in_l1, na_q_norm_l1, na_k_norm_l1, rel_bias_l1, w_out_l1):
    batch, seq, _ = x_prompt.shape
    dec_batch, dec_seq, _ = x_sample.shape
    past = cache_l0_mla_ckv.shape[1]
    xp = x_prompt.reshape(batch * seq, D_MODEL)
    xs = x_sample.reshape(dec_batch * dec_seq, D_MODEL)
    tiles_p = (batch * seq) // TOKEN_TILE
    tiles_s = dec_seq // TOKEN_TILE

    cond8 = jnp.concatenate([c_ctx[None, :], c, jnp.zeros((SUBLANES - 1 - dec_batch, D_MODEL), F32)], axis=0)
    mods = _ada(cond8, (w_ada_l0, w_ada_l1), (b_ada_l0, b_ada_l1))
    mods = mods.reshape(DEPTH, SUBLANES, 1, N_ADA * D_MODEL)
    (mod0_p, mod0_s), (mod1_p, mod1_s) = (((mods[l], 0), (mods[l], 1)) for l in range(DEPTH))

    w0 = _prep_l0(w_in_l0, q_lora_norm_l0, kv_lora_norm_l0, w_q_up_l0, w_kv_up_l0, mla_q_norm_l0, mla_k_norm_l0,
                  w_pool_l0, pool_scale_l0, w_out_l0, norm_mix_l0, norm_mlp_l0, w_mlp1_l0, w_mlp2_l0)
    a_p, q_p, k_p, v_p, ckv_p, kr_p = _l0_pre(xp, mod0_p, tiles_p, False, w0, None)
    a_s, q_s, k_s, v_s, _, _ = _l0_pre(xs, mod0_s, tiles_s, True, w0, _rope_tables(dec_seq))
    kr_cache = jnp.pad(cache_l0_mla_krope.reshape(dec_batch * past, MLA_ROPE), ((0, 0), (MLA_NOPE, LANES - MLA_QK)))
    k_c, v_c = _cache_kv(cache_l0_mla_ckv.reshape(dec_batch * past, KV_LORA), kr_cache, w0)
    mla_bound = _normed_len(mla_q_norm_l0 * (MLA_QK ** -0.5 * LOG2E), MLA_QK) * _normed_len(mla_k_norm_l0, MLA_QK)
    attn_p = _ctx_attn(_mla_ctx_kernel, q_p, k_p, v_p, mla_bound, seq, MLA_HEADS * MLA_V, "l0_mla_ctx_attn")
    attn_s, w16 = _mla_lat_attn(q_s, k_s, v_s, k_c, v_c, jnp.full((dec_batch, MLA_HEADS), mla_bound, F32),
                                dec_seq, past, (w0["w_out"], w0["w_mlp1"], w0["w_mlp2"], w_in_l1))
    w0 = dict(w0, w_out=w16[0], w_mlp1=w16[1], w_mlp2=w16[2])
    xp = _post(xp, attn_p, mod0_p, tiles_p, w0, "l0_post_ctx", pool=(a_p, seq))
    xs = _post(xs, attn_s, mod0_s, tiles_s, w0, "l0_post_latent", pool=(a_s, dec_seq))

    w1 = dict(norm_mix=norm_mix_l1.reshape(1, -1), norm_mlp=norm_mlp_l1.reshape(1, -1),
              w_in=w16[3],
              gq=jnp.tile(na_q_norm_l1 * (NA_HEAD_DIM ** -0.5 * LOG2E), 2).reshape(1, LANES),
              gk=jnp.tile(na_k_norm_l1, 2).reshape(1, LANES),
              w_out=w_out_l1, w_mlp1=w_mlp1_l1, w_mlp2=w_mlp2_l1)
    q1_p, k1_p, v1_p, k1_new, v1_new = _l1_pre(xp, mod1_p, tiles_p, True, w1)
    q1_s, k1_s, v1_s = _l1_pre(xs, mod1_s, tiles_s, False, w1)
    q_len, k_len = _normed_len(w1["gq"], NA_HEAD_DIM), _normed_len(w1["gk"], NA_HEAD_DIM)
    na_p = _ctx_attn(_na_ctx_kernel, q1_p, k1_p, v1_p, q_len * k_len, seq, NA_SLABS * LANES, "l1_na_ctx_attn")
    cache_len = jnp.sqrt(jnp.max(jnp.sum(jnp.square(cache_l1_na_k.astype(F32)), axis=-1), axis=1))
    na_bound = (q_len * jnp.maximum(k_len, cache_len)
                + jnp.max(jnp.abs(rel_bias_l1.astype(F32)) * LOG2E, axis=(1, 2))[None, :])
    na_s, w16 = _na_lat_attn(q1_s, k1_s, v1_s,
                             cache_l1_na_k.reshape(dec_batch * past, -1), cache_l1_na_v.reshape(dec_batch * past, -1),
                             _na_bias_table(rel_bias_l1, dec_seq // GRID_W), na_bound, dec_seq, past,
                             (w1["w_out"], w1["w_mlp1"], w1["w_mlp2"]))
    w1 = dict(w1, w_out=w16[0], w_mlp1=w16[1], w_mlp2=w16[2])
    xp = _post(xp, na_p, mod1_p, tiles_p, w1, "l1_post_ctx")
    xs = _post(xs, na_s, mod1_s, tiles_s, w1, "l1_post_latent")

    return (xp.reshape(batch, seq, D_MODEL), xs.reshape(dec_batch, dec_seq, D_MODEL),
            ckv_p.reshape(batch, seq, KV_LORA), kr_p.reshape(batch, seq, MLA_ROPE),
            k1_new.reshape(batch, seq, NA_HEADS, NA_HEAD_DIM), v1_new.reshape(batch, seq, NA_HEADS, NA_HEAD_DIM))
```

```python
import functools

import jax
import jax.numpy as jnp
from jax import lax
from jax.experimental import pallas as pl
from jax.experimental.pallas import tpu as pltpu

F32 = jnp.float32
BF16 = jnp.bfloat16

D_MODEL = 1024
D_FF = 4 * D_MODEL
DEPTH = 2
N_ADA = 6
EPS = 1e-6
NEG_INF = -1e30
ROPE_THETA = 10000.0
GRID_W = 64
POOL_WIDTH = 512
POOL_WINDOWS = (2, 4, 8, 16)
POOL_GROUP = 128
POOL_HALO = 8
MLA_HEADS = 8
MLA_NOPE = 64
MLA_ROPE = 32
MLA_QK = MLA_NOPE + MLA_ROPE
MLA_V = 64
Q_LORA = 256
KV_LORA = 128
NA_HEADS = 16
NA_HEAD_DIM = 64
NA_KH = 8
NA_KW = 16
LOG2E = 1.4426950408889634

LANES = 128
SUBLANES = 8
VMEM_LIMIT = 56 * 1024 * 1024

TOKEN_TILE = 512
MLA_Q_TILE = 512
MLA_STEP_HEADS = 2
MLA_KEY_CHUNK = 2048
MLA_ONLINE_CHUNK = 1024
SCORE_LOOKAHEAD = 4
MAX_SCORE_BOUND = 40.0
FF_CHUNK = 1024
CTX_STEP_SEQS = 2
NA_SLABS = NA_HEADS * NA_HEAD_DIM // LANES
NA_TILE_ROWS = 4
NA_WIN_ROWS = NA_TILE_ROWS + NA_KH - 1


def _params(n_axes):
    return pltpu.CompilerParams(dimension_semantics=("arbitrary",) * n_axes,
                                vmem_limit_bytes=VMEM_LIMIT)


def _const_spec(shape):
    zeros = (0,) * len(shape)
    return pl.BlockSpec(shape, lambda *_: zeros, pipeline_mode=pl.Buffered(1))


def _dot(a, b):
    return jnp.dot(a, b, preferred_element_type=F32)


def _dot_nt(a, b):
    return lax.dot_general(a, b, (((1,), (1,)), ((), ())), preferred_element_type=F32)


def _dot_tn(a, b):
    return lax.dot_general(a, b, (((0,), (0,)), ((), ())), preferred_element_type=F32)


def _rms(x, w, n=None):
    n = x.shape[-1] if n is None else n
    ss = jnp.sum(x * x, axis=-1, keepdims=True)
    return x * lax.rsqrt(ss * (1.0 / n) + EPS) * w


def _modulate(x, nw, shift, scale):
    return _rms(x, nw) * (1.0 + scale) + shift


def _ada_spec(mod, tiles_per_group):
    mods, first_row = mod
    return pl.BlockSpec((None, 1, mods.shape[-1]), lambda i: (first_row + i // tiles_per_group, 0, 0))


def _ada_row(mod_ref, k):
    return mod_ref[:, k * D_MODEL:(k + 1) * D_MODEL]


def _ada_kernel(cond_ref, w0_ref, w1_ref, b_ref, o_ref):
    c = cond_ref[...]
    s = (c * jax.nn.sigmoid(c)).astype(BF16)
    for layer, w_ref in enumerate((w0_ref, w1_ref)):
        @pl.when(pl.program_id(0) == layer)
        def _(w_ref=w_ref):
            o_ref[...] = _dot(s, w_ref[...].astype(BF16)) + b_ref[...]


def _ada(cond8, w_ada, b_ada):
    n_out = w_ada[0].shape[1]
    tn = D_MODEL
    nb = n_out // tn
    return pl.pallas_call(
        _ada_kernel,
        out_shape=jax.ShapeDtypeStruct((DEPTH, SUBLANES, n_out), F32),
        grid=(DEPTH, nb),
        in_specs=[pl.BlockSpec((SUBLANES, D_MODEL), lambda l, j: (0, 0)),
                  pl.BlockSpec((D_MODEL, tn), lambda l, j: (0, jnp.where(l == 0, j, nb - 1))),
                  pl.BlockSpec((D_MODEL, tn), lambda l, j: (0, jnp.where(l == 1, j, 0))),
                  pl.BlockSpec((None, 1, tn), lambda l, j: (l, 0, j))],
        out_specs=pl.BlockSpec((None, SUBLANES, tn), lambda l, j: (l, 0, j)),
        compiler_params=_params(2),
        name="ada",
    )(cond8, w_ada[0], w_ada[1], jnp.stack(b_ada).reshape(DEPTH, 1, n_out))


def _rope(x, c, s):
    return x * c + pltpu.roll(x, LANES - MLA_ROPE, 1) * s


PAIR = 2 * LANES


def _head_rinv(raw, e2_ref):
    ss = _dot((raw * raw).astype(BF16), e2_ref[...])
    return lax.rsqrt(ss * (1.0 / MLA_QK) + EPS)


def _mla_keys_values(ckv, kr_slab, wk_ref, gk, wv_ref, e2_ref, rope_tabs, k_ref, v_ref):
    c16 = ckv.astype(BF16)
    tail = kr_slab * gk
    if rope_tabs is not None:
        tail = _rope(tail, *rope_tabs)
    ones_lane = (lax.broadcasted_iota(jnp.int32, (1, LANES), 1) == MLA_V).astype(F32)
    for p in range(MLA_HEADS // 2):
        cols = pl.ds(p * PAIR, PAIR)
        kn = _dot(c16, wk_ref[:, cols])
        v = _dot(c16, wv_ref[:, cols])
        rinv = _head_rinv(kn + jnp.tile(kr_slab, (1, 2)), e2_ref)
        for i in range(2):
            hs = slice(i * LANES, (i + 1) * LANES)
            k_ref[2 * p + i] = (rinv[:, hs] * (kn[:, hs] * gk + tail)).astype(BF16)
            v_ref[2 * p + i] = (v[:, hs] + ones_lane).astype(BF16)


def _l0_pre_kernel(latent, x_ref, mod_ref, nw_ref, win_ref, qln_ref, kvn_ref, wq_ref, gq_ref,
                   wk_ref, gk_ref, wv_ref, e2_ref, *rest):
    if latent:
        rope_ref, a_ref, q_ref, k_ref, v_ref, ckv_ref, kr_ref = rest
        tile_rows = x_ref.shape[0] // GRID_W
        row0 = (pl.program_id(0) * tile_rows) % rope_ref.shape[1]
        expand = lambda col, row: jnp.concatenate(
            [rope_ref[col] + rope_ref[row, pl.ds(row0 + r, 1), :] for r in range(tile_rows)], axis=0)
        rope_tabs = (expand(0, 1), expand(2, 3))
    else:
        a_ref, q_ref, k_ref, v_ref, ckv_ref, kr_ref = rest
        rope_tabs = None
    h = _modulate(x_ref[...], nw_ref[...], _ada_row(mod_ref, 0), _ada_row(mod_ref, 1)).astype(BF16)
    proj = _dot(h, win_ref[...])
    a_ref[...] = proj[:, :POOL_WIDTH]
    q_lat = proj[:, POOL_WIDTH:POOL_WIDTH + Q_LORA]
    kv_lat = proj[:, POOL_WIDTH + Q_LORA:POOL_WIDTH + Q_LORA + KV_LORA]
    kr_slab = proj[:, POOL_WIDTH + Q_LORA + KV_LORA:]
    ql = _rms(q_lat, qln_ref[...]).astype(BF16)
    gq = gq_ref[...]
    for p in range(MLA_HEADS // 2):
        q = _dot(ql, wq_ref[:, pl.ds(p * PAIR, PAIR)])
        q_rinv = _head_rinv(q, e2_ref)
        for i in range(2):
            hs = slice(i * LANES, (i + 1) * LANES)
            y = q[:, hs] * q_rinv[:, hs] * gq
            if rope_tabs is not None:
                y = _rope(y, *rope_tabs)
            q_ref[2 * p + i] = y.astype(BF16)
    ckv = _rms(kv_lat, kvn_ref[...])
    ckv_ref[...] = ckv
    kr_ref[...] = kr_slab[:, MLA_NOPE:MLA_QK]
    _mla_keys_values(ckv, kr_slab, wk_ref, gk_ref[...], wv_ref, e2_ref, rope_tabs, k_ref, v_ref)


def _slab_spec(slabs, tm):
    return pl.BlockSpec((slabs, tm, LANES), lambda i: (0, i, 0))


def _l0_pre(x, mod, tiles_per_group, latent, w, rope_tabs):
    n = x.shape[0]
    tm = TOKEN_TILE
    row = lambda i: (i, 0)
    hk = MLA_HEADS * LANES
    in_specs = [pl.BlockSpec((tm, D_MODEL), row),
                _ada_spec(mod, tiles_per_group),
                _const_spec((1, D_MODEL)),
                _const_spec((D_MODEL, D_MODEL)),
                _const_spec((1, Q_LORA)),
                _const_spec((1, KV_LORA)),
                _const_spec((Q_LORA, hk)),
                _const_spec((1, LANES)),
                _const_spec((KV_LORA, hk)),
                _const_spec((1, LANES)),
                _const_spec((KV_LORA, hk)),
                _const_spec((2 * LANES, 2 * LANES))]
    gains = "rope" if latent else "plain"
    args = [x, mod[0], w["norm_mix"], w["w_in"], w["qln"], w["kvn"], w["wq"], w["gq_" + gains], w["wk"],
            w["gk_" + gains], w["wv"], w["e2"]]
    if latent:
        assert tm % GRID_W == 0
        in_specs.append(_const_spec(rope_tabs.shape))
        args.append(rope_tabs)
    slab = jax.ShapeDtypeStruct((MLA_HEADS, n, LANES), BF16)
    out_shape = (jax.ShapeDtypeStruct((n, POOL_WIDTH), F32), slab, slab, slab,
                 jax.ShapeDtypeStruct((n, KV_LORA), F32),
                 jax.ShapeDtypeStruct((n, MLA_ROPE), F32))
    out_specs = (pl.BlockSpec((tm, POOL_WIDTH), row),
                 _slab_spec(MLA_HEADS, tm), _slab_spec(MLA_HEADS, tm), _slab_spec(MLA_HEADS, tm),
                 pl.BlockSpec((tm, KV_LORA), row), pl.BlockSpec((tm, MLA_ROPE), row))
    return pl.pallas_call(
        functools.partial(_l0_pre_kernel, latent),
        out_shape=out_shape, grid=(n // tm,), in_specs=in_specs, out_specs=out_specs,
        compiler_params=_params(1), name="l0_pre_latent" if latent else "l0_pre_ctx",
    )(*args)


def _cache_kv_kernel(ckv_ref, kr_ref, wk_ref, gk_ref, wv_ref, e2_ref, k_ref, v_ref):
    _mla_keys_values(ckv_ref[...], kr_ref[...], wk_ref, gk_ref[...], wv_ref, e2_ref, None, k_ref, v_ref)


def _cache_kv(ckv, kr_slab, w):
    n = ckv.shape[0]
    tm = TOKEN_TILE
    row = lambda i: (i, 0)
    hk = MLA_HEADS * LANES
    slab = jax.ShapeDtypeStruct((MLA_HEADS, n, LANES), BF16)
    return pl.pallas_call(
        _cache_kv_kernel,
        out_shape=(slab, slab),
        grid=(n // tm,),
        in_specs=[pl.BlockSpec((tm, KV_LORA), row), pl.BlockSpec((tm, LANES), row),
                  _const_spec((KV_LORA, hk)), _const_spec((1, LANES)), _const_spec((KV_LORA, hk)),
                  _const_spec((2 * LANES, 2 * LANES))],
        out_specs=(_slab_spec(MLA_HEADS, tm), _slab_spec(MLA_HEADS, tm)),
        compiler_params=_params(1), name="l0_cache_kv",
    )(ckv, kr_slab, w["wk"], w["gk_plain"], w["wv"], w["e2"])


def _pool_mixer(seq_len, tile, prev_ref, u_ref, next_ref, wp_ref, ps_ref, buf_ref):
    tm = u_ref.shape[0]
    piece = min(tm, seq_len)
    n_pieces = tm // piece
    stride = piece + 2 * POOL_HALO
    starts = []
    for j in range(n_pieces):
        pos0 = (tile * tm + j * piece) % seq_len
        prev = prev_ref[...] if j == 0 else u_ref[j * piece - POOL_HALO:j * piece, :]
        nxt = next_ref[...] if j == n_pieces - 1 else u_ref[(j + 1) * piece:(j + 1) * piece + POOL_HALO, :]
        base = j * stride
        buf_ref[base:base + POOL_HALO, :] = jnp.where(pos0 != 0, prev, 0.0)
        buf_ref[base + POOL_HALO:base + POOL_HALO + piece, :] = u_ref[j * piece:(j + 1) * piece, :]
        buf_ref[base + POOL_HALO + piece:base + stride, :] = jnp.where(pos0 + piece != seq_len, nxt, 0.0)
        starts.append(pos0)
    def group(g):
        win = POOL_WINDOWS[g]
        lanes = pl.ds(g * POOL_GROUP, POOL_GROUP)
        pooled = []
        for j in range(n_pieces):
            centre = j * stride + POOL_HALO
            lo = starts[j] + lax.broadcasted_iota(jnp.int32, (piece, 1), 0) - win // 2
            cnt = (jnp.minimum(lo + win, seq_len) - jnp.maximum(lo, 0)).astype(F32)
            acc = buf_ref[pl.ds(centre - win // 2, piece), lanes]
            for d in range(1 - win // 2, win - win // 2):
                acc = acc + buf_ref[pl.ds(centre + d, piece), lanes]
            pooled.append(acc / cnt - buf_ref[pl.ds(centre, piece), lanes])
        pooled = pooled[0] if n_pieces == 1 else jnp.concatenate(pooled, axis=0)
        return (_dot(pooled.astype(BF16), wp_ref[g]) * ps_ref[:, lanes]).astype(BF16)

    return [functools.partial(group, g) for g in range(len(POOL_WINDOWS))]


def _attend(streams, running_max):
    items = [(si, ch) for si, (_, chunks, _) in enumerate(streams) for ch in chunks]

    def score(item):
        si, (k_fn, _, extra_fn) = item
        s = _dot_nt(k_fn(), streams[si][0])
        return s if extra_fn is None else s + extra_fn()

    state = [None] * len(streams)
    pending = [score(item) for item in items[:SCORE_LOOKAHEAD]]
    for i, (si, (_, v_fn, _)) in enumerate(items):
        s = pending.pop(0)
        if i + SCORE_LOOKAHEAD < len(items):
            pending.append(score(items[i + SCORE_LOOKAHEAD]))
        want_den = streams[si][2]
        m_new, alpha = None, 1.0
        if state[si] is not None:
            m, acc, den = state[si]
        if running_max:
            m_new = jnp.max(s, axis=0, keepdims=True)
            if state[si] is not None:
                m_new = jnp.maximum(m, m_new)
                alpha = jnp.exp2(m - m_new)
            s = s - m_new
        p = jnp.exp2(s)
        pv = _dot_tn(v_fn(), p.astype(BF16))
        psum = jnp.sum(p, axis=0, keepdims=True) if want_den else None
        if state[si] is None:
            state[si] = (m_new, pv, psum)
        elif running_max:
            state[si] = (m_new, acc * alpha + pv, den * alpha + psum if want_den else None)
        else:
            state[si] = (None, acc + pv, den + psum if want_den else None)
    return [(acc, den) for _, acc, den in state]


def _ref_chunks(k_ref, v_ref, slab, chunk, extra_fn=None, lo=0, hi=None):
    hi = k_ref.shape[1] if hi is None else hi
    out = []
    for c in range(lo, hi, chunk):
        rows = pl.ds(c, min(chunk, hi - c))
        out.append((lambda rows=rows: k_ref[slab, rows, :], lambda rows=rows: v_ref[slab, rows, :], extra_fn))
    return out


def _mla_finish(o_t):
    return o_t[:MLA_V] / o_t[MLA_V:MLA_V + 1]


def _na_queries(q2):
    low = lax.broadcasted_iota(jnp.int32, (1, LANES), 1) < NA_HEAD_DIM
    zero = jnp.zeros_like(q2)
    return jnp.concatenate([jnp.where(low, q2, zero), jnp.where(low, zero, q2)], axis=0)


def _na_finish(o_t, den):
    nq = o_t.shape[1] // 2
    o = o_t / den
    return jnp.concatenate([o[:NA_HEAD_DIM, :nq], o[NA_HEAD_DIM:, nq:]], axis=0)


def _attend_guarded(bound, streams_fn):
    return lax.cond(bound <= MAX_SCORE_BOUND,
                    lambda: _attend(streams_fn(False), False),
                    lambda: _attend(streams_fn(True), True))


def _mla_ctx_kernel(seq, bound_ref, q_ref, k_ref, v_ref, o_ref):
    spans = [(lo, lo + seq) for lo in range(0, k_ref.shape[1], seq)]
    res = _attend_guarded(bound_ref[0, 0], lambda _: [
        (q_ref[h, lo:hi, :], _ref_chunks(k_ref, v_ref, h, seq, lo=lo, hi=hi), False)
        for lo, hi in spans for h in range(MLA_HEADS)])
    for i, (lo, hi) in enumerate(spans):
        for hp in range(MLA_HEADS // 2):
            o_t = jnp.concatenate([_mla_finish(res[i * MLA_HEADS + h][0]) for h in (2 * hp, 2 * hp + 1)], axis=0)
            o_ref[lo:hi, hp * LANES:(hp + 1) * LANES] = o_t.T.astype(BF16)


def _na_ctx_kernel(seq, bound_ref, q_ref, k_ref, v_ref, o_ref):
    spans = [(lo, lo + seq) for lo in range(0, k_ref.shape[1], seq)]
    res = _attend_guarded(bound_ref[0, 0], lambda _: [
        (_na_queries(q_ref[s, lo:hi, :]), _ref_chunks(k_ref, v_ref, s, seq, lo=lo, hi=hi), True)
        for lo, hi in spans for s in range(NA_SLABS)])
    for i, (lo, hi) in enumerate(spans):
        for s in range(NA_SLABS):
            o_ref[lo:hi, s * LANES:(s + 1) * LANES] = _na_finish(*res[i * NA_SLABS + s]).T.astype(BF16)


def _ctx_attn(body, q, k, v, bound, seq, width, name):
    slabs, n, _ = q.shape
    rows = CTX_STEP_SEQS * seq
    spec = pl.BlockSpec((slabs, rows, LANES), lambda b: (0, b, 0))
    return pl.pallas_call(
        functools.partial(body, seq),
        out_shape=jax.ShapeDtypeStruct((n, width), BF16),
        grid=(n // rows,),
        in_specs=[pl.BlockSpec(memory_space=pltpu.SMEM), spec, spec, spec],
        out_specs=pl.BlockSpec((rows, width), lambda b: (b, 0)),
        compiler_params=_params(1), name=name,
    )(bound.reshape(1, 1), q, k, v)


SIDE_PIECE_ELEMS = 512 * 1024


def _side_piece_rows(n_cols):
    return 1 << ((SIDE_PIECE_ELEMS // n_cols).bit_length() - 1)


def _side_cast_plan(shapes):
    plan = []
    for wi, (n_rows, n_cols) in enumerate(shapes):
        rows = _side_piece_rows(n_cols)
        assert n_rows % rows == 0
        plan += [(wi, r0, rows) for r0 in range(0, n_rows, rows)]
    return plan


def _side_cast_scratch(shapes):
    widths = sorted({c for _, c in shapes})
    stages = [pltpu.VMEM((2, _side_piece_rows(c), c), dt) for dt in (F32, BF16) for c in widths]
    return stages + [pltpu.SemaphoreType.DMA((2,)), pltpu.SemaphoreType.DMA((2,))]


def _side_cast(step, srcs, dsts, scratch):
    shapes = [s.shape for s in srcs]
    widths = sorted({c for _, c in shapes})
    stage_in = dict(zip(widths, scratch[:len(widths)]))
    stage_out = dict(zip(widths, scratch[len(widths):2 * len(widths)]))
    sem_in, sem_out = scratch[2 * len(widths):]
    plan = _side_cast_plan(shapes)

    def refs(k):
        wi, r0, rows = plan[k]
        cols = shapes[wi][1]
        return (srcs[wi].at[pl.ds(r0, rows), :], dsts[wi].at[pl.ds(r0, rows), :],
                stage_in[cols].at[k % 2], stage_out[cols].at[k % 2])

    read = lambda k: pltpu.make_async_copy(refs(k)[0], refs(k)[2], sem_in.at[k % 2])
    write = lambda k: pltpu.make_async_copy(refs(k)[3], refs(k)[1], sem_out.at[k % 2])
    n = len(plan)

    def piece_step(k):
        if k == 0:
            read(0).start()
        if k >= 2:
            write(k - 2).wait()
        if k < n:
            read(k).wait()
            if k + 1 < n:
                read(k + 1).start()
            refs(k)[3][...] = refs(k)[2][...].astype(BF16)
            write(k).start()

    @pl.when(step < n + 2)
    def _():
        for k in range(n + 2):
            pl.when(step == k)(functools.partial(piece_step, k))


def _mla_lat_kernel(n_cast, bound_ref, q_ref, kl_ref, vl_ref, kc_ref, vc_ref, *rest):
    o_ref = rest[n_cast]
    linear_step = (pl.program_id(0) * pl.num_programs(1) + pl.program_id(1)) * pl.num_programs(2) + pl.program_id(2)
    _side_cast(linear_step, rest[:n_cast], rest[n_cast + 1:2 * n_cast + 1], rest[2 * n_cast + 1:])
    batch, head0 = pl.program_id(0), pl.program_id(1) * MLA_STEP_HEADS
    heads = range(MLA_STEP_HEADS)
    bounds = [bound_ref[batch, head0 + h] for h in heads]

    def streams(running_max):
        size = MLA_ONLINE_CHUNK if running_max else MLA_KEY_CHUNK
        return [(q_ref[h], _ref_chunks(kl_ref, vl_ref, h, size) + _ref_chunks(kc_ref, vc_ref, h, size), False)
                for h in heads]

    res = _attend_guarded(functools.reduce(jnp.maximum, bounds), streams)
    o_ref[...] = jnp.concatenate([_mla_finish(o) for o, _ in res], axis=0).T.astype(BF16)


def _normed_len(gain, dim):
    return (dim ** 0.5) * jnp.max(jnp.abs(gain.astype(F32)))


def _side_cast_specs(weights, n_steps):
    shapes = [wt.shape for wt in weights]
    assert n_steps >= len(_side_cast_plan(shapes)) + 2
    hbm = pl.BlockSpec(memory_space=pl.ANY)
    return ([hbm] * len(weights), [jax.ShapeDtypeStruct(s, BF16) for s in shapes], [hbm] * len(weights),
            _side_cast_scratch(shapes))


def _mla_lat_attn(q, k, v, kc, vc, bound, seq, past, weights):
    n = q.shape[1]
    tq = MLA_Q_TILE
    nq = seq // tq
    hs = MLA_STEP_HEADS
    grid = (n // seq, MLA_HEADS // hs, nq)
    w_in, w_shapes, w_out, w_scratch = _side_cast_specs(weights, grid[0] * grid[1] * grid[2])
    outs = pl.pallas_call(
        functools.partial(_mla_lat_kernel, len(weights)),
        out_shape=(jax.ShapeDtypeStruct((n, MLA_HEADS * MLA_V), BF16), *w_shapes),
        grid=grid,
        in_specs=[pl.BlockSpec(memory_space=pltpu.SMEM),
                  pl.BlockSpec((hs, tq, LANES), lambda b, h, i: (h, b * nq + i, 0)),
                  pl.BlockSpec((hs, seq, LANES), lambda b, h, i: (h, b, 0)),
                  pl.BlockSpec((hs, seq, LANES), lambda b, h, i: (h, b, 0)),
                  pl.BlockSpec((hs, past, LANES), lambda b, h, i: (h, b, 0)),
                  pl.BlockSpec((hs, past, LANES), lambda b, h, i: (h, b, 0))] + w_in,
        out_specs=(pl.BlockSpec((tq, hs * MLA_V), lambda b, h, i: (b * nq + i, h)), *w_out),
        scratch_shapes=w_scratch,
        compiler_params=_params(3), name="l0_mla_latent_attn",
    )(bound, q, k, v, kc, vc, *weights)
    return outs[0], outs[1:]


def _na_first_row(r, rows):
    return jnp.clip(r - NA_KH // 2, 0, rows - NA_KH)


def _na_lat_kernel(rows, n_cast, bound_ref, q_ref, k_ref, v_ref, kc_ref, vc_ref, t2_ref, *rest):
    o_ref = rest[n_cast]
    _side_cast(pl.program_id(0) * pl.num_programs(1) + pl.program_id(1),
               rest[:n_cast], rest[n_cast + 1:2 * n_cast + 1], rest[2 * n_cast + 1:])
    nq = NA_TILE_ROWS * GRID_W
    nk = NA_WIN_ROWS * GRID_W
    n_off = 2 * NA_KH - 1
    r0 = pl.program_id(1) * NA_TILE_ROWS
    w0 = jnp.clip(r0 - NA_KH // 2, 0, rows - NA_WIN_ROWS)

    def block_index(j, t):
        key_row = w0 + j
        i = r0 + 2 * t
        first_u, first_l = _na_first_row(i, rows), _na_first_row(i + 1, rows)
        in_u = (key_row >= first_u) & (key_row < first_u + NA_KH)
        in_l = (key_row >= first_l) & (key_row < first_l + NA_KH)
        d = jnp.clip(key_row - i + NA_KH - 1, 0, n_off - 1)
        return jnp.where(in_u & in_l, d, jnp.where(in_u, n_off, jnp.where(in_l, n_off + 1, n_off + 2)))

    idx = [[block_index(j, t) for t in range(NA_TILE_ROWS // 2)] for j in range(NA_WIN_ROWS)]

    def local_bias(s):
        heads = [jnp.concatenate([jnp.concatenate([t2_ref[2 * s + hh, i] for i in row], axis=1) for row in idx], axis=0)
                 for hh in range(2)]
        return jnp.concatenate(heads, axis=1)

    batch = pl.program_id(0)
    bounds = [bound_ref[batch, h] for h in range(NA_HEADS)]

    def cache_chunk(s):
        lanes = pl.ds(s * LANES, LANES)
        return [(lambda: kc_ref[:, lanes].astype(BF16), lambda: vc_ref[:, lanes].astype(BF16), None)]

    res = _attend_guarded(
        functools.reduce(jnp.maximum, bounds),
        lambda _: [(_na_queries(q_ref[s]),
                    _ref_chunks(k_ref, v_ref, s, nk, functools.partial(local_bias, s)) + cache_chunk(s),
                    True) for s in range(NA_SLABS)])
    for s in range(NA_SLABS):
        o_ref[:, s * LANES:(s + 1) * LANES] = _na_finish(*res[s]).T.astype(BF16)


def _na_lat_attn(q, k, v, kc, vc, t2, bound, seq, past, weights):
    slabs, n, _ = q.shape
    rows = seq // GRID_W
    tiles = rows // NA_TILE_ROWS
    nq = NA_TILE_ROWS * GRID_W
    nk = NA_WIN_ROWS * GRID_W

    def win_map(b, t):
        w0 = jnp.clip(t * NA_TILE_ROWS - NA_KH // 2, 0, rows - NA_WIN_ROWS)
        return (0, pl.multiple_of(b * seq + w0 * GRID_W, GRID_W), 0)

    win_spec = pl.BlockSpec((pl.Element(slabs), pl.Element(nk), pl.Element(LANES)), win_map)
    ctx_spec = pl.BlockSpec((past, slabs * LANES), lambda b, t: (b, 0))
    w_in, w_shapes, w_out, w_scratch = _side_cast_specs(weights, (n // seq) * tiles)
    outs = pl.pallas_call(
        functools.partial(_na_lat_kernel, rows, len(weights)),
        out_shape=(jax.ShapeDtypeStruct((n, slabs * LANES), BF16), *w_shapes),
        grid=(n // seq, tiles),
        in_specs=[pl.BlockSpec(memory_space=pltpu.SMEM),
                  pl.BlockSpec((slabs, nq, LANES), lambda b, t: (0, b * tiles + t, 0)),
                  win_spec, win_spec, ctx_spec, ctx_spec,
                  _const_spec(t2.shape)] + w_in,
        out_specs=(pl.BlockSpec((nq, slabs * LANES), lambda b, t: (b * tiles + t, 0)), *w_out),
        scratch_shapes=w_scratch,
        compiler_params=_params(2), name="l1_na_latent_attn",
    )(bound, q, k, v, kc, vc, t2, *weights)
    return outs[0], outs[1:]


def _pair_norm(x2, g2):
    low = lax.broadcasted_iota(jnp.int32, (1, LANES), 1) < NA_HEAD_DIM
    sq = x2 * x2
    ss_all = jnp.sum(sq, axis=-1, keepdims=True)
    ss_low = jnp.sum(jnp.where(low, sq, 0.0), axis=-1, keepdims=True)
    ss = jnp.where(low, ss_low, ss_all - ss_low)
    return x2 * lax.rsqrt(ss * (1.0 / NA_HEAD_DIM) + EPS) * g2


def _l1_pre_kernel(keep_f32, x_ref, mod_ref, nw_ref, win_ref, gq_ref, gk_ref, q_ref, k_ref, v_ref, *f32_refs):
    h = _modulate(x_ref[...], nw_ref[...], _ada_row(mod_ref, 0), _ada_row(mod_ref, 1)).astype(BF16)
    width = NA_SLABS * LANES
    q = _dot(h, win_ref[:, 0:width])
    k = _dot(h, win_ref[:, width:2 * width])
    v = _dot(h, win_ref[:, 2 * width:3 * width])
    gq = gq_ref[...]
    gk = gk_ref[...]
    for s in range(NA_SLABS):
        ls = slice(s * LANES, (s + 1) * LANES)
        q_ref[s] = _pair_norm(q[:, ls], gq).astype(BF16)
        kn = _pair_norm(k[:, ls], gk)
        k_ref[s] = kn.astype(BF16)
        v_ref[s] = v[:, ls].astype(BF16)
        if keep_f32:
            f32_refs[0][:, ls] = kn
    if keep_f32:
        f32_refs[1][...] = v


def _l1_pre(x, mod, tiles_per_group, keep_f32, w):
    n = x.shape[0]
    tm = TOKEN_TILE
    row = lambda i: (i, 0)
    width = NA_SLABS * LANES
    slab = jax.ShapeDtypeStruct((NA_SLABS, n, LANES), BF16)
    out_shape = [slab, slab, slab]
    out_specs = [_slab_spec(NA_SLABS, tm)] * 3
    if keep_f32:
        out_shape += [jax.ShapeDtypeStruct((n, width), F32)] * 2
        out_specs += [pl.BlockSpec((tm, width), row)] * 2
    return pl.pallas_call(
        functools.partial(_l1_pre_kernel, keep_f32),
        out_shape=tuple(out_shape), grid=(n // tm,),
        in_specs=[pl.BlockSpec((tm, D_MODEL), row),
                  _ada_spec(mod, tiles_per_group),
                  _const_spec((1, D_MODEL)),
                  _const_spec((D_MODEL, 3 * width)),
                  _const_spec((1, LANES)),
                  _const_spec((1, LANES))],
        out_specs=tuple(out_specs),
        compiler_params=_params(1), name="l1_pre_ctx" if keep_f32 else "l1_pre_latent",
    )(x, mod[0], w["norm_mix"], w["w_in"], w["gq"], w["gk"])


def _post_kernel(pool_seq, x_ref, attn_ref, *rest):
    rest = list(rest)
    take = lambda k: [rest.pop(0) for _ in range(k)]
    pool_in = take(8) if pool_seq is not None else None
    mod_ref, nw_ref, wo_ref, w1_ref, w2_ref, o_ref = take(6)
    pool_scratch = take(2) if pool_seq is not None else None
    step = pl.program_id(0)
    n_chunks = D_FF // FF_CHUNK
    if pool_seq is not None:
        first_refs, next_refs, (wp_ref, ps_ref) = pool_in[0:3], pool_in[3:6], pool_in[6:8]
        buf_ref, pooled_ref = pool_scratch

        @pl.when(step == 0)
        def _():
            for g, group in enumerate(_pool_mixer(pool_seq, 0, *first_refs, wp_ref, ps_ref, buf_ref)):
                pooled_ref[:, g * POOL_GROUP:(g + 1) * POOL_GROUP] = group()

        mixed = _dot(jnp.concatenate([pooled_ref[...], attn_ref[...]], axis=-1), wo_ref[...])
        next_groups = _pool_mixer(pool_seq, jnp.minimum(step + 1, pl.num_programs(0) - 1), *next_refs,
                                  wp_ref, ps_ref, buf_ref)
    else:
        mixed = _dot(attn_ref[...], wo_ref[...])
    x1 = x_ref[...] + _ada_row(mod_ref, 2) * mixed
    h = _modulate(x1, nw_ref[...], _ada_row(mod_ref, 3), _ada_row(mod_ref, 4)).astype(BF16)
    acc = jnp.zeros(x1.shape, F32)
    for c in range(n_chunks):
        cs = pl.ds(c * FF_CHUNK, FF_CHUNK)
        u = jnp.square(jnp.maximum(_dot(h, w1_ref[:, cs]), 0.0)).astype(BF16)
        acc = acc + _dot(u, w2_ref[cs, :])
        if pool_seq is not None:
            for g in range(c * len(next_groups) // n_chunks, (c + 1) * len(next_groups) // n_chunks):
                pooled_ref[:, g * POOL_GROUP:(g + 1) * POOL_GROUP] = next_groups[g]()
    o_ref[...] = x1 + _ada_row(mod_ref, 5) * acc


def _post(x, attn, mod, tiles_per_group, w, name, pool=None):
    n = x.shape[0]
    tm = TOKEN_TILE if pool is not None else 2 * TOKEN_TILE
    tiles_per_group = tiles_per_group * TOKEN_TILE // tm
    row = lambda i: (i, 0)
    in_specs = [pl.BlockSpec((tm, D_MODEL), row), pl.BlockSpec((tm, attn.shape[1]), row)]
    args = [x, attn]
    scratch = []
    pool_seq = None
    if pool is not None:
        a_in, pool_seq = pool
        hb = tm // POOL_HALO
        last = n // POOL_HALO - 1
        pieces = tm // min(tm, pool_seq)
        last_tile = n // tm - 1

        def tile_specs(tile_of):
            return [pl.BlockSpec((POOL_HALO, POOL_WIDTH), lambda i: (jnp.maximum(tile_of(i) * hb - 1, 0), 0)),
                    pl.BlockSpec((tm, POOL_WIDTH), lambda i: (tile_of(i), 0)),
                    pl.BlockSpec((POOL_HALO, POOL_WIDTH), lambda i: (jnp.minimum((tile_of(i) + 1) * hb, last), 0))]

        in_specs += tile_specs(lambda i: 0) + tile_specs(lambda i: jnp.minimum(i + 1, last_tile))
        in_specs += [_const_spec((len(POOL_WINDOWS), POOL_GROUP, POOL_GROUP)), _const_spec((1, POOL_WIDTH))]
        args += [a_in] * 6 + [w["w_pool"], w["pool_scale"]]
        scratch = [pltpu.VMEM((tm + 2 * POOL_HALO * pieces, POOL_WIDTH), F32), pltpu.VMEM((tm, POOL_WIDTH), BF16)]
    in_specs += [_ada_spec(mod, tiles_per_group),
                 _const_spec((1, D_MODEL)),
                 _const_spec((D_MODEL, D_MODEL)),
                 _const_spec((D_MODEL, D_FF)),
                 _const_spec((D_FF, D_MODEL))]
    args += [mod[0], w["norm_mlp"], w["w_out"], w["w_mlp1"], w["w_mlp2"]]
    return pl.pallas_call(
        functools.partial(_post_kernel, pool_seq),
        out_shape=jax.ShapeDtypeStruct((n, D_MODEL), F32), grid=(n // tm,),
        in_specs=in_specs, out_specs=pl.BlockSpec((tm, D_MODEL), row),
        scratch_shapes=scratch, compiler_params=_params(1), name=name,
    )(*args)


def _rope_tables(seq):
    assert seq == GRID_W * GRID_W
    half = MLA_ROPE // 2
    inv_freq = jnp.power(ROPE_THETA, -jnp.arange(0, half, 2, dtype=F32) / half)
    ang = jnp.arange(GRID_W, dtype=F32)[:, None] * inv_freq[None, :]
    cos, sin = jnp.cos(ang), jnp.sin(ang)
    fill = lambda n, v=0.0: jnp.full((GRID_W, n), v, F32)
    spare = LANES - MLA_QK
    return jnp.stack([
        jnp.concatenate([fill(MLA_NOPE, 1.0), fill(half), cos, cos, fill(spare)], axis=1),
        jnp.concatenate([fill(MLA_NOPE), cos, cos, fill(half), fill(spare)], axis=1),
        jnp.concatenate([fill(MLA_NOPE), fill(half), -sin, sin, fill(spare)], axis=1),
        jnp.concatenate([fill(MLA_NOPE), -sin, sin, fill(half), fill(spare)], axis=1)])


def _rope_partner(tail):
    shape = tail.shape
    t = tail.reshape(shape[:-1] + (2, 2, MLA_ROPE // 4))
    return t[..., ::-1, :].reshape(shape)


def _na_bias_table(rel_bias, rows):
    qcol = jnp.arange(GRID_W)
    kcol = jnp.arange(GRID_W)
    col_start = jnp.clip(qcol - NA_KW // 2, 0, GRID_W - NA_KW)
    col_mask = (kcol[:, None] >= col_start[None, :]) & (kcol[:, None] < col_start[None, :] + NA_KW)
    offs = jnp.arange(2 * NA_KW - 1)
    onehot = (kcol[None, :, None] - qcol[None, None, :] + NA_KW - 1 == offs[:, None, None]).astype(F32)
    blocks = jnp.einsum('hdj,jkq->hdkq', rel_bias.astype(F32) * LOG2E, onehot, precision=lax.Precision.HIGHEST)
    blocks = jnp.where(col_mask[None, None], blocks, NEG_INF)
    neg = jnp.full_like(blocks[:, :1], NEG_INF)
    both = jnp.concatenate([blocks, jnp.concatenate([neg, blocks[:, :-1]], axis=1)], axis=-1)
    d_upper, d_lower = _na_partial_offsets(rows)
    upper_only = jnp.concatenate([blocks[:, d_upper:d_upper + 1], neg], axis=-1)
    lower_only = jnp.concatenate([neg, blocks[:, d_lower - 1:d_lower]], axis=-1)
    return jnp.concatenate([both, upper_only, lower_only, jnp.concatenate([neg, neg], axis=-1)], axis=1)


def _na_partial_offsets(rows):
    first = lambda r: min(max(r - NA_KH // 2, 0), rows - NA_KH)
    upper, lower = set(), set()
    for r0 in range(0, rows, NA_TILE_ROWS):
        w0 = min(max(r0 - NA_KH // 2, 0), rows - NA_WIN_ROWS)
        for j in range(NA_WIN_ROWS):
            for t in range(NA_TILE_ROWS // 2):
                key_row, i = w0 + j, r0 + 2 * t
                in_u = first(i) <= key_row < first(i) + NA_KH
                in_l = first(i + 1) <= key_row < first(i + 1) + NA_KH
                d = key_row - i + NA_KH - 1
                if in_u and in_l:
                    assert 1 <= d <= 2 * NA_KH - 2
                elif in_u:
                    upper.add(d)
                elif in_l:
                    lower.add(d)
    assert len(upper) == 1 and len(lower) == 1, (upper, lower)
    return upper.pop(), lower.pop()


def _prep_l0(w_in, q_lora_norm, kv_lora_norm, w_q_up, w_kv_up, mla_q_norm, mla_k_norm, w_pool, pool_scale,
             w_out, norm_mix, norm_mlp, w_mlp1, w_mlp2):
    lat_end = POOL_WIDTH + Q_LORA + KV_LORA
    w_in_pad = jnp.concatenate([w_in[:, :lat_end], jnp.zeros((D_MODEL, MLA_NOPE), w_in.dtype),
                                w_in[:, lat_end:], _rope_partner(w_in[:, lat_end:])], axis=1)
    pad = LANES - MLA_QK
    hk = MLA_HEADS * LANES
    wq = w_q_up.reshape(Q_LORA, MLA_HEADS, MLA_QK)
    wq = jnp.concatenate([wq, _rope_partner(wq[:, :, MLA_NOPE:])], axis=-1)
    kv = w_kv_up.reshape(KV_LORA, MLA_HEADS, MLA_NOPE + MLA_V)
    wk = jnp.pad(kv[:, :, :MLA_NOPE], ((0, 0), (0, 0), (0, LANES - MLA_NOPE)))
    wv = jnp.pad(kv[:, :, MLA_NOPE:], ((0, 0), (0, 0), (0, LANES - MLA_V)))
    gq = mla_q_norm * (MLA_QK ** -0.5 * LOG2E)
    slab_lane = jnp.arange(2 * LANES)
    e2 = (slab_lane[:, None] // LANES == slab_lane[None, :] // LANES) & (slab_lane[:, None] % LANES < MLA_QK)
    return dict(
        norm_mix=norm_mix.reshape(1, -1), norm_mlp=norm_mlp.reshape(1, -1),
        w_in=w_in_pad.astype(BF16),
        qln=q_lora_norm.reshape(1, -1), kvn=kv_lora_norm.reshape(1, -1),
        wq=wq.reshape(Q_LORA, hk).astype(BF16),
        gq_plain=jnp.pad(gq, (0, pad)).reshape(1, LANES),
        gq_rope=jnp.concatenate([gq, _rope_partner(gq[MLA_NOPE:])]).reshape(1, LANES),
        wk=wk.reshape(KV_LORA, hk).astype(BF16),
        gk_plain=jnp.pad(mla_k_norm, (0, pad)).reshape(1, LANES),
        gk_rope=jnp.concatenate([mla_k_norm, _rope_partner(mla_k_norm[MLA_NOPE:])]).reshape(1, LANES),
        wv=wv.reshape(KV_LORA, hk).astype(BF16),
        e2=e2.astype(BF16),
        w_pool=w_pool.astype(BF16), pool_scale=pool_scale.reshape(1, -1),
        w_out=w_out, w_mlp1=w_mlp1, w_mlp2=w_mlp2)


def kernel(x_prompt, x_sample, cache_l0_mla_ckv, cache_l0_mla_krope, cache_l1_na_k, cache_l1_na_v, c, c_ctx, w_ada_l0, b_ada_l0, norm_mix_l0, norm_mlp_l0, w_mlp1_l0, w_mlp2_l0, w_in_l0, q_lora_norm_l0, kv_lora_norm_l0, w_q_up_l0, w_kv_up_l0, mla_q_norm_l0, mla_k_norm_l0, w_pool_l0, pool_scale_l0, w_out_l0, w_ada_l1, b_ada_l1, norm_mix_l1, norm_mlp_l1, w_mlp1_l1, w_mlp2_l1, w_in_l1, na_q_norm_l1, na_k_norm_l1, rel_bias_l1, w_out_l1):
    batch, seq, _ = x_prompt.shape
    dec_batch, dec_seq, _ = x_sample.shape
    past = cache_l0_mla_ckv.shape[1]
    xp = x_prompt.reshape(batch * seq, D_MODEL)
    xs = x_sample.reshape(dec_batch * dec_seq, D_MODEL)
    tiles_p = (batch * seq) // TOKEN_TILE
    tiles_s = dec_seq // TOKEN_TILE

    cond8 = jnp.concatenate([c_ctx[None, :], c, jnp.zeros((SUBLANES - 1 - dec_batch, D_MODEL), F32)], axis=0)
    mods = _ada(cond8, (w_ada_l0, w_ada_l1), (b_ada_l0, b_ada_l1))
    mods = mods.reshape(DEPTH, SUBLANES, 1, N_ADA * D_MODEL)
    (mod0_p, mod0_s), (mod1_p, mod1_s) = (((mods[l], 0), (mods[l], 1)) for l in range(DEPTH))

    w0 = _prep_l0(w_in_l0, q_lora_norm_l0, kv_lora_norm_l0, w_q_up_l0, w_kv_up_l0, mla_q_norm_l0, mla_k_norm_l0,
                  w_pool_l0, pool_scale_l0, w_out_l0, norm_mix_l0, norm_mlp_l0, w_mlp1_l0, w_mlp2_l0)
    a_p, q_p, k_p, v_p, ckv_p, kr_p = _l0_pre(xp, mod0_p, tiles_p, False, w0, None)
    a_s, q_s, k_s, v_s, _, _ = _l0_pre(xs, mod0_s, tiles_s, True, w0, _rope_tables(dec_seq))
    kr_cache = jnp.pad(cache_l0_mla_krope.reshape(dec_batch * past, MLA_ROPE), ((0, 0), (MLA_NOPE, LANES - MLA_QK)))
    k_c, v_c = _cache_kv(cache_l0_mla_ckv.reshape(dec_batch * past, KV_LORA), kr_cache, w0)
    mla_bound = _normed_len(mla_q_norm_l0 * (MLA_QK ** -0.5 * LOG2E), MLA_QK) * _normed_len(mla_k_norm_l0, MLA_QK)
    attn_p = _ctx_attn(_mla_ctx_kernel, q_p, k_p, v_p, mla_bound, seq, MLA_HEADS * MLA_V, "l0_mla_ctx_attn")
    attn_s, w16 = _mla_lat_attn(q_s, k_s, v_s, k_c, v_c, jnp.full((dec_batch, MLA_HEADS), mla_bound, F32),
                                dec_seq, past, (w0["w_out"], w0["w_mlp1"], w0["w_mlp2"], w_in_l1))
    w0 = dict(w0, w_out=w16[0], w_mlp1=w16[1], w_mlp2=w16[2])
    xp = _post(xp, attn_p, mod0_p, tiles_p, w0, "l0_post_ctx", pool=(a_p, seq))
    xs = _post(xs, attn_s, mod0_s, tiles_s, w0, "l0_post_latent", pool=(a_s, dec_seq))

    w1 = dict(norm_mix=norm_mix_l1.reshape(1, -1), norm_mlp=norm_mlp_l1.reshape(1, -1),
              w_in=w16[3],
              gq=jnp.tile(na_q_norm_l1 * (NA_HEAD_DIM ** -0.5 * LOG2E), 2).reshape(1, LANES),
              gk=jnp.tile(na_k_norm_l1, 2).reshape(1, LANES),
              w_out=w_out_l1, w_mlp1=w_mlp1_l1, w_mlp2=w_mlp2_l1)
    q1_p, k1_p, v1_p, k1_new, v1_new = _l1_pre(xp, mod1_p, tiles_p, True, w1)
    q1_s, k1_s, v1_s = _l1_pre(xs, mod1_s, tiles_s, False, w1)
    q_len, k_len = _normed_len(w1["gq"], NA_HEAD_DIM), _normed_len(w1["gk"], NA_HEAD_DIM)
    na_p = _ctx_attn(_na_ctx_kernel, q1_p, k1_p, v1_p, q_len * k_len, seq, NA_SLABS * LANES, "l1_na_ctx_attn")
    cache_len = jnp.sqrt(jnp.max(jnp.sum(jnp.square(cache_l1_na_k.astype(F32)), axis=-1), axis=1))
    na_bound = (q_len * jnp.maximum(k_len, cache_len)
                + jnp.max(jnp.abs(rel_bias_l1.astype(F32)) * LOG2E, axis=(1, 2))[None, :])
    na_s, w16 = _na_lat_attn(q1_s, k1_s, v1_s,
                             cache_l1_na_k.reshape(dec_batch * past, -1), cache_l1_na_v.reshape(dec_batch * past, -1),
                             _na_bias_table(rel_bias_l1, dec_seq // GRID_W), na_bound, dec_seq, past,
                             (w1["w_out"], w1["w_mlp1"], w1["w_mlp2"]))
    w1 = dict(w1, w_out=w16[0], w_mlp1=w16[1], w_mlp2=w16[2])
    xp = _post(xp, na_p, mod1_p, tiles_p, w1, "l1_post_ctx")
    xs = _post(xs, na_s, mod1_s, tiles_s, w1, "l1_post_latent")

    return (xp.reshape(batch, seq, D_MODEL), xs.reshape(dec_batch, dec_seq, D_MODEL),
            ckv_p.reshape(batch, seq, KV_LORA), kr_p.reshape(batch, seq, MLA_ROPE),
            k1_new.reshape(batch, seq, NA_HEADS, NA_HEAD_DIM), v1_new.reshape(batch, seq, NA_HEADS, NA_HEAD_DIM))
```

```python
import functools

import jax
import jax.numpy as jnp
from jax import lax
from jax.experimental import pallas as pl
from jax.experimental.pallas import tpu as pltpu

F32 = jnp.float32
BF16 = jnp.bfloat16

D_MODEL = 1024
D_FF = 4 * D_MODEL
DEPTH = 2
N_ADA = 6
EPS = 1e-6
NEG_INF = -1e30
ROPE_THETA = 10000.0
GRID_W = 64
POOL_WIDTH = 512
POOL_WINDOWS = (2, 4, 8, 16)
POOL_GROUP = 128
POOL_HALO = 8
MLA_HEADS = 8
MLA_NOPE = 64
MLA_ROPE = 32
MLA_QK = MLA_NOPE + MLA_ROPE
MLA_V = 64
Q_LORA = 256
KV_LORA = 128
NA_HEADS = 16
NA_HEAD_DIM = 64
NA_KH = 8
NA_KW = 16
LOG2E = 1.4426950408889634

LANES = 128
SUBLANES = 8
VMEM_LIMIT = 56 * 1024 * 1024

TOKEN_TILE = 512
MLA_Q_TILE = 512
MLA_STEP_HEADS = 2
MLA_KEY_CHUNK = 2048
MLA_ONLINE_CHUNK = 1024
SCORE_LOOKAHEAD = 4
MAX_SCORE_BOUND = 40.0
FF_CHUNK = 1024
CTX_STEP_SEQS = 2
NA_SLABS = NA_HEADS * NA_HEAD_DIM // LANES
NA_TILE_ROWS = 4
NA_WIN_ROWS = NA_TILE_ROWS + NA_KH - 1


def _params(n_axes):
    return pltpu.CompilerParams(dimension_semantics=("arbitrary",) * n_axes,
                                vmem_limit_bytes=VMEM_LIMIT)


def _const_spec(shape):
    zeros = (0,) * len(shape)
    return pl.BlockSpec(shape, lambda *_: zeros, pipeline_mode=pl.Buffered(1))


def _dot(a, b):
    return jnp.dot(a, b, preferred_element_type=F32)


def _dot_nt(a, b):
    return lax.dot_general(a, b, (((1,), (1,)), ((), ())), preferred_element_type=F32)


def _dot_tn(a, b):
    return lax.dot_general(a, b, (((0,), (0,)), ((), ())), preferred_element_type=F32)


def _rms(x, w, n=None):
    n = x.shape[-1] if n is None else n
    ss = jnp.sum(x * x, axis=-1, keepdims=True)
    return x * lax.rsqrt(ss * (1.0 / n) + EPS) * w


def _modulate(x, nw, shift, scale):
    return _rms(x, nw) * (1.0 + scale) + shift


def _ada_spec(mod, tiles_per_group):
    mods, first_row = mod
    return pl.BlockSpec((None, 1, mods.shape[-1]), lambda i: (first_row + i // tiles_per_group, 0, 0))


def _ada_row(mod_ref, k):
    return mod_ref[:, k * D_MODEL:(k + 1) * D_MODEL]


def _ada_kernel(cond_ref, w0_ref, w1_ref, b_ref, o_ref):
    c = cond_ref[...]
    s = (c * jax.nn.sigmoid(c)).astype(BF16)
    for layer, w_ref in enumerate((w0_ref, w1_ref)):
        @pl.when(pl.program_id(0) == layer)
        def _(w_ref=w_ref):
            o_ref[...] = _dot(s, w_ref[...].astype(BF16)) + b_ref[...]


def _ada(cond8, w_ada, b_ada):
    n_out = w_ada[0].shape[1]
    tn = D_MODEL
    nb = n_out // tn
    return pl.pallas_call(
        _ada_kernel,
        out_shape=jax.ShapeDtypeStruct((DEPTH, SUBLANES, n_out), F32),
        grid=(DEPTH, nb),
        in_specs=[pl.BlockSpec((SUBLANES, D_MODEL), lambda l, j: (0, 0)),
                  pl.BlockSpec((D_MODEL, tn), lambda l, j: (0, jnp.where(l == 0, j, nb - 1))),
                  pl.BlockSpec((D_MODEL, tn), lambda l, j: (0, jnp.where(l == 1, j, 0))),
                  pl.BlockSpec((None, 1, tn), lambda l, j: (l, 0, j))],
        out_specs=pl.BlockSpec((None, SUBLANES, tn), lambda l, j: (l, 0, j)),
        compiler_params=_params(2),
        name="ada",
    )(cond8, w_ada[0], w_ada[1], jnp.stack(b_ada).reshape(DEPTH, 1, n_out))


def _rope(x, c, s):
    return x * c + pltpu.roll(x, LANES - MLA_ROPE, 1) * s


PAIR = 2 * LANES


def _head_rinv(raw, e2_ref):
    ss = _dot((raw * raw).astype(BF16), e2_ref[...])
    return lax.rsqrt(ss * (1.0 / MLA_QK) + EPS)


def _mla_keys_values(ckv, kr_slab, wk_ref, gk, wv_ref, e2_ref, rope_tabs, k_ref, v_ref):
    c16 = ckv.astype(BF16)
    tail = kr_slab * gk
    if rope_tabs is not None:
        tail = _rope(tail, *rope_tabs)
    ones_lane = (lax.broadcasted_iota(jnp.int32, (1, LANES), 1) == MLA_V).astype(F32)
    for p in range(MLA_HEADS // 2):
        cols = pl.ds(p * PAIR, PAIR)
        kn = _dot(c16, wk_ref[:, cols])
        v = _dot(c16, wv_ref[:, cols])
        rinv = _head_rinv(kn + jnp.tile(kr_slab, (1, 2)), e2_ref)
        for i in range(2):
            hs = slice(i * LANES, (i + 1) * LANES)
            k_ref[2 * p + i] = (rinv[:, hs] * (kn[:, hs] * gk + tail)).astype(BF16)
            v_ref[2 * p + i] = (v[:, hs] + ones_lane).astype(BF16)


def _l0_pre_kernel(latent, x_ref, mod_ref, nw_ref, win_ref, qln_ref, kvn_ref, wq_ref, gq_ref,
                   wk_ref, gk_ref, wv_ref, e2_ref, *rest):
    if latent:
        rope_ref, a_ref, q_ref, k_ref, v_ref, ckv_ref, kr_ref = rest
        tile_rows = x_ref.shape[0] // GRID_W
        row0 = (pl.program_id(0) * tile_rows) % rope_ref.shape[1]
        expand = lambda col, row: jnp.concatenate(
            [rope_ref[col] + rope_ref[row, pl.ds(row0 + r, 1), :] for r in range(tile_rows)], axis=0)
        rope_tabs = (expand(0, 1), expand(2, 3))
    else:
        a_ref, q_ref, k_ref, v_ref, ckv_ref, kr_ref = rest
        rope_tabs = None
    h = _modulate(x_ref[...], nw_ref[...], _ada_row(mod_ref, 0), _ada_row(mod_ref, 1)).astype(BF16)
    proj = _dot(h, win_ref[...])
    a_ref[...] = proj[:, :POOL_WIDTH]
    q_lat = proj[:, POOL_WIDTH:POOL_WIDTH + Q_LORA]
    kv_lat = proj[:, POOL_WIDTH + Q_LORA:POOL_WIDTH + Q_LORA + KV_LORA]
    kr_slab = proj[:, POOL_WIDTH + Q_LORA + KV_LORA:]
    ql = _rms(q_lat, qln_ref[...]).astype(BF16)
    gq = gq_ref[...]
    for p in range(MLA_HEADS // 2):
        q = _dot(ql, wq_ref[:, pl.ds(p * PAIR, PAIR)])
        q_rinv = _head_rinv(q, e2_ref)
        for i in range(2):
            hs = slice(i * LANES, (i + 1) * LANES)
            y = q[:, hs] * q_rinv[:, hs] * gq
            if rope_tabs is not None:
                y = _rope(y, *rope_tabs)
            q_ref[2 * p + i] = y.astype(BF16)
    ckv = _rms(kv_lat, kvn_ref[...])
    ckv_ref[...] = ckv
    kr_ref[...] = kr_slab[:, MLA_NOPE:MLA_QK]
    _mla_keys_values(ckv, kr_slab, wk_ref, gk_ref[...], wv_ref, e2_ref, rope_tabs, k_ref, v_ref)


def _slab_spec(slabs, tm):
    return pl.BlockSpec((slabs, tm, LANES), lambda i: (0, i, 0))


def _l0_pre(x, mod, tiles_per_group, latent, w, rope_tabs):
    n = x.shape[0]
    tm = TOKEN_TILE
    row = lambda i: (i, 0)
    hk = MLA_HEADS * LANES
    in_specs = [pl.BlockSpec((tm, D_MODEL), row),
                _ada_spec(mod, tiles_per_group),
                _const_spec((1, D_MODEL)),
                _const_spec((D_MODEL, D_MODEL)),
                _const_spec((1, Q_LORA)),
                _const_spec((1, KV_LORA)),
                _const_spec((Q_LORA, hk)),
                _const_spec((1, LANES)),
                _const_spec((KV_LORA, hk)),
                _const_spec((1, LANES)),
                _const_spec((KV_LORA, hk)),
                _const_spec((2 * LANES, 2 * LANES))]
    gains = "rope" if latent else "plain"
    args = [x, mod[0], w["norm_mix"], w["w_in"], w["qln"], w["kvn"], w["wq"], w["gq_" + gains], w["wk"],
            w["gk_" + gains], w["wv"], w["e2"]]
    if latent:
        assert tm % GRID_W == 0
        in_specs.append(_const_spec(rope_tabs.shape))
        args.append(rope_tabs)
    slab = jax.ShapeDtypeStruct((MLA_HEADS, n, LANES), BF16)
    out_shape = (jax.ShapeDtypeStruct((n, POOL_WIDTH), F32), slab, slab, slab,
                 jax.ShapeDtypeStruct((n, KV_LORA), F32),
                 jax.ShapeDtypeStruct((n, MLA_ROPE), F32))
    out_specs = (pl.BlockSpec((tm, POOL_WIDTH), row),
                 _slab_spec(MLA_HEADS, tm), _slab_spec(MLA_HEADS, tm), _slab_spec(MLA_HEADS, tm),
                 pl.BlockSpec((tm, KV_LORA), row), pl.BlockSpec((tm, MLA_ROPE), row))
    return pl.pallas_call(
        functools.partial(_l0_pre_kernel, latent),
        out_shape=out_shape, grid=(n // tm,), in_specs=in_specs, out_specs=out_specs,
        compiler_params=_params(1), name="l0_pre_latent" if latent else "l0_pre_ctx",
    )(*args)


def _cache_kv_kernel(ckv_ref, kr_ref, wk_ref, gk_ref, wv_ref, e2_ref, k_ref, v_ref):
    _mla_keys_values(ckv_ref[...], kr_ref[...], wk_ref, gk_ref[...], wv_ref, e2_ref, None, k_ref, v_ref)


def _cache_kv(ckv, kr_slab, w):
    n = ckv.shape[0]
    tm = TOKEN_TILE
    row = lambda i: (i, 0)
    hk = MLA_HEADS * LANES
    slab = jax.ShapeDtypeStruct((MLA_HEADS, n, LANES), BF16)
    return pl.pallas_call(
        _cache_kv_kernel,
        out_shape=(slab, slab),
        grid=(n // tm,),
        in_specs=[pl.BlockSpec((tm, KV_LORA), row), pl.BlockSpec((tm, LANES), row),
                  _const_spec((KV_LORA, hk)), _const_spec((1, LANES)), _const_spec((KV_LORA, hk)),
                  _const_spec((2 * LANES, 2 * LANES))],
        out_specs=(_slab_spec(MLA_HEADS, tm), _slab_spec(MLA_HEADS, tm)),
        compiler_params=_params(1), name="l0_cache_kv",
    )(ckv, kr_slab, w["wk"], w["gk_plain"], w["wv"], w["e2"])


def _pool_mixer(seq_len, tile, prev_ref, u_ref, next_ref, wp_ref, ps_ref, buf_ref):
    tm = u_ref.shape[0]
    piece = min(tm, seq_len)
    n_pieces = tm // piece
    stride = piece + 2 * POOL_HALO
    starts = []
    for j in range(n_pieces):
        pos0 = (tile * tm + j * piece) % seq_len
        prev = prev_ref[...] if j == 0 else u_ref[j * piece - POOL_HALO:j * piece, :]
        nxt = next_ref[...] if j == n_pieces - 1 else u_ref[(j + 1) * piece:(j + 1) * piece + POOL_HALO, :]
        base = j * stride
        buf_ref[base:base + POOL_HALO, :] = jnp.where(pos0 != 0, prev, 0.0)
        buf_ref[base + POOL_HALO:base + POOL_HALO + piece, :] = u_ref[j * piece:(j + 1) * piece, :]
        buf_ref[base + POOL_HALO + piece:base + stride, :] = jnp.where(pos0 + piece != seq_len, nxt, 0.0)
        starts.append(pos0)
    def group(g):
        win = POOL_WINDOWS[g]
        lanes = pl.ds(g * POOL_GROUP, POOL_GROUP)
        pooled = []
        for j in range(n_pieces):
            centre = j * stride + POOL_HALO
            lo = starts[j] + lax.broadcasted_iota(jnp.int32, (piece, 1), 0) - win // 2
            cnt = (jnp.minimum(lo + win, seq_len) - jnp.maximum(lo, 0)).astype(F32)
            acc = buf_ref[pl.ds(centre - win // 2, piece), lanes]
            for d in range(1 - win // 2, win - win // 2):
                acc = acc + buf_ref[pl.ds(centre + d, piece), lanes]
            pooled.append(acc / cnt - buf_ref[pl.ds(centre, piece), lanes])
        pooled = pooled[0] if n_pieces == 1 else jnp.concatenate(pooled, axis=0)
        return (_dot(pooled.astype(BF16), wp_ref[g]) * ps_ref[:, lanes]).astype(BF16)

    return [functools.partial(group, g) for g in range(len(POOL_WINDOWS))]


def _attend(streams, running_max):
    items = [(si, ch) for si, (_, chunks, _) in enumerate(streams) for ch in chunks]

    def score(item):
        si, (k_fn, _, extra_fn) = item
        s = _dot_nt(k_fn(), streams[si][0])
        return s if extra_fn is None else s + extra_fn()

    state = [None] * len(streams)
    pending = [score(item) for item in items[:SCORE_LOOKAHEAD]]
    for i, (si, (_, v_fn, _)) in enumerate(items):
        s = pending.pop(0)
        if i + SCORE_LOOKAHEAD < len(items):
            pending.append(score(items[i + SCORE_LOOKAHEAD]))
        want_den = streams[si][2]
        m_new, alpha = None, 1.0
        if state[si] is not None:
            m, acc, den = state[si]
        if running_max:
            m_new = jnp.max(s, axis=0, keepdims=True)
            if state[si] is not None:
                m_new = jnp.maximum(m, m_new)
                alpha = jnp.exp2(m - m_new)
            s = s - m_new
        p = jnp.exp2(s)
        pv = _dot_tn(v_fn(), p.astype(BF16))
        psum = jnp.sum(p, axis=0, keepdims=True) if want_den else None
        if state[si] is None:
            state[si] = (m_new, pv, psum)
        elif running_max:
            state[si] = (m_new, acc * alpha + pv, den * alpha + psum if want_den else None)
        else:
            state[si] = (None, acc + pv, den + psum if want_den else None)
    return [(acc, den) for _, acc, den in state]


def _ref_chunks(k_ref, v_ref, slab, chunk, extra_fn=None, lo=0, hi=None):
    hi = k_ref.shape[1] if hi is None else hi
    out = []
    for c in range(lo, hi, chunk):
        rows = pl.ds(c, min(chunk, hi - c))
        out.append((lambda rows=rows: k_ref[slab, rows, :], lambda rows=rows: v_ref[slab, rows, :], extra_fn))
    return out


def _mla_finish(o_t):
    return o_t[:MLA_V] / o_t[MLA_V:MLA_V + 1]


def _na_queries(q2):
    low = lax.broadcasted_iota(jnp.int32, (1, LANES), 1) < NA_HEAD_DIM
    zero = jnp.zeros_like(q2)
    return jnp.concatenate([jnp.where(low, q2, zero), jnp.where(low, zero, q2)], axis=0)


def _na_finish(o_t, den):
    nq = o_t.shape[1] // 2
    o = o_t / den
    return jnp.concatenate([o[:NA_HEAD_DIM, :nq], o[NA_HEAD_DIM:, nq:]], axis=0)


def _attend_guarded(bound, streams_fn):
    return lax.cond(bound <= MAX_SCORE_BOUND,
                    lambda: _attend(streams_fn(False), False),
                    lambda: _attend(streams_fn(True), True))


def _mla_ctx_kernel(seq, bound_ref, q_ref, k_ref, v_ref, o_ref):
    spans = [(lo, lo + seq) for lo in range(0, k_ref.shape[1], seq)]
    res = _attend_guarded(bound_ref[0, 0], lambda _: [
        (q_ref[h, lo:hi, :], _ref_chunks(k_ref, v_ref, h, seq, lo=lo, hi=hi), False)
        for lo, hi in spans for h in range(MLA_HEADS)])
    for i, (lo, hi) in enumerate(spans):
        for hp in range(MLA_HEADS // 2):
            o_t = jnp.concatenate([_mla_finish(res[i * MLA_HEADS + h][0]) for h in (2 * hp, 2 * hp + 1)], axis=0)
            o_ref[lo:hi, hp * LANES:(hp + 1) * LANES] = o_t.T.astype(BF16)


def _na_ctx_kernel(seq, bound_ref, q_ref, k_ref, v_ref, o_ref):
    spans = [(lo, lo + seq) for lo in range(0, k_ref.shape[1], seq)]
    res = _attend_guarded(bound_ref[0, 0], lambda _: [
        (_na_queries(q_ref[s, lo:hi, :]), _ref_chunks(k_ref, v_ref, s, seq, lo=lo, hi=hi), True)
        for lo, hi in spans for s in range(NA_SLABS)])
    for i, (lo, hi) in enumerate(spans):
        for s in range(NA_SLABS):
            o_ref[lo:hi, s * LANES:(s + 1) * LANES] = _na_finish(*res[i * NA_SLABS + s]).T.astype(BF16)


def _ctx_attn(body, q, k, v, bound, seq, width, name):
    slabs, n, _ = q.shape
    rows = CTX_STEP_SEQS * seq
    spec = pl.BlockSpec((slabs, rows, LANES), lambda b: (0, b, 0))
    return pl.pallas_call(
        functools.partial(body, seq),
        out_shape=jax.ShapeDtypeStruct((n, width), BF16),
        grid=(n // rows,),
        in_specs=[pl.BlockSpec(memory_space=pltpu.SMEM), spec, spec, spec],
        out_specs=pl.BlockSpec((rows, width), lambda b: (b, 0)),
        compiler_params=_params(1), name=name,
    )(bound.reshape(1, 1), q, k, v)


SIDE_PIECE_ELEMS = 512 * 1024


def _side_piece_rows(n_cols):
    return 1 << ((SIDE_PIECE_ELEMS // n_cols).bit_length() - 1)


def _side_cast_plan(shapes):
    plan = []
    for wi, (n_rows, n_cols) in enumerate(shapes):
        rows = _side_piece_rows(n_cols)
        assert n_rows % rows == 0
        plan += [(wi, r0, rows) for r0 in range(0, n_rows, rows)]
    return plan


def _side_cast_scratch(shapes):
    widths = sorted({c for _, c in shapes})
    stages = [pltpu.VMEM((2, _side_piece_rows(c), c), dt) for dt in (F32, BF16) for c in widths]
    return stages + [pltpu.SemaphoreType.DMA((2,)), pltpu.SemaphoreType.DMA((2,))]


def _side_cast(step, srcs, dsts, scratch):
    shapes = [s.shape for s in srcs]
    widths = sorted({c for _, c in shapes})
    stage_in = dict(zip(widths, scratch[:len(widths)]))
    stage_out = dict(zip(widths, scratch[len(widths):2 * len(widths)]))
    sem_in, sem_out = scratch[2 * len(widths):]
    plan = _side_cast_plan(shapes)

    def refs(k):
        wi, r0, rows = plan[k]
        cols = shapes[wi][1]
        return (srcs[wi].at[pl.ds(r0, rows), :], dsts[wi].at[pl.ds(r0, rows), :],
                stage_in[cols].at[k % 2], stage_out[cols].at[k % 2])

    read = lambda k: pltpu.make_async_copy(refs(k)[0], refs(k)[2], sem_in.at[k % 2])
    write = lambda k: pltpu.make_async_copy(refs(k)[3], refs(k)[1], sem_out.at[k % 2])
    n = len(plan)

    def piece_step(k):
        if k == 0:
            read(0).start()
        if k >= 2:
            write(k - 2).wait()
        if k < n:
            read(k).wait()
            if k + 1 < n:
                read(k + 1).start()
            refs(k)[3][...] = refs(k)[2][...].astype(BF16)
            write(k).start()

    @pl.when(step < n + 2)
    def _():
        for k in range(n + 2):
            pl.when(step == k)(functools.partial(piece_step, k))


def _mla_lat_kernel(n_cast, bound_ref, q_ref, kl_ref, vl_ref, kc_ref, vc_ref, *rest):
    o_ref = rest[n_cast]
    linear_step = (pl.program_id(0) * pl.num_programs(1) + pl.program_id(1)) * pl.num_programs(2) + pl.program_id(2)
    _side_cast(linear_step, rest[:n_cast], rest[n_cast + 1:2 * n_cast + 1], rest[2 * n_cast + 1:])
    batch, head0 = pl.program_id(0), pl.program_id(1) * MLA_STEP_HEADS
    heads = range(MLA_STEP_HEADS)
    bounds = [bound_ref[batch, head0 + h] for h in heads]

    def streams(running_max):
        size = MLA_ONLINE_CHUNK if running_max else MLA_KEY_CHUNK
        return [(q_ref[h], _ref_chunks(kl_ref, vl_ref, h, size) + _ref_chunks(kc_ref, vc_ref, h, size), False)
                for h in heads]

    res = _attend_guarded(functools.reduce(jnp.maximum, bounds), streams)
    o_ref[...] = jnp.concatenate([_mla_finish(o) for o, _ in res], axis=0).T.astype(BF16)


def _normed_len(gain, dim):
    return (dim ** 0.5) * jnp.max(jnp.abs(gain.astype(F32)))


def _side_cast_specs(weights, n_steps):
    shapes = [wt.shape for wt in weights]
    assert n_steps >= len(_side_cast_plan(shapes)) + 2
    hbm = pl.BlockSpec(memory_space=pl.ANY)
    return ([hbm] * len(weights), [jax.ShapeDtypeStruct(s, BF16) for s in shapes], [hbm] * len(weights),
            _side_cast_scratch(shapes))


def _mla_lat_attn(q, k, v, kc, vc, bound, seq, past, weights):
    n = q.shape[1]
    tq = MLA_Q_TILE
    nq = seq // tq
    hs = MLA_STEP_HEADS
    grid = (n // seq, MLA_HEADS // hs, nq)
    w_in, w_shapes, w_out, w_scratch = _side_cast_specs(weights, grid[0] * grid[1] * grid[2])
    outs = pl.pallas_call(
        functools.partial(_mla_lat_kernel, len(weights)),
        out_shape=(jax.ShapeDtypeStruct((n, MLA_HEADS * MLA_V), BF16), *w_shapes),
        grid=grid,
        in_specs=[pl.BlockSpec(memory_space=pltpu.SMEM),
                  pl.BlockSpec((hs, tq, LANES), lambda b, h, i: (h, b * nq + i, 0)),
                  pl.BlockSpec((hs, seq, LANES), lambda b, h, i: (h, b, 0)),
                  pl.BlockSpec((hs, seq, LANES), lambda b, h, i: (h, b, 0)),
                  pl.BlockSpec((hs, past, LANES), lambda b, h, i: (h, b, 0)),
                  pl.BlockSpec((hs, past, LANES), lambda b, h, i: (h, b, 0))] + w_in,
        out_specs=(pl.BlockSpec((tq, hs * MLA_V), lambda b, h, i: (b * nq + i, h)), *w_out),
        scratch_shapes=w_scratch,
        compiler_params=_params(3), name="l0_mla_latent_attn",
    )(bound, q, k, v, kc, vc, *weights)
    return outs[0], outs[1:]


def _na_first_row(r, rows):
    return jnp.clip(r - NA_KH // 2, 0, rows - NA_KH)


def _na_lat_kernel(rows, n_cast, bound_ref, q_ref, k_ref, v_ref, kc_ref, vc_ref, t2_ref, *rest):
    o_ref = rest[n_cast]
    _side_cast(pl.program_id(0) * pl.num_programs(1) + pl.program_id(1),
               rest[:n_cast], rest[n_cast + 1:2 * n_cast + 1], rest[2 * n_cast + 1:])
    nq = NA_TILE_ROWS * GRID_W
    nk = NA_WIN_ROWS * GRID_W
    n_off = 2 * NA_KH - 1
    r0 = pl.program_id(1) * NA_TILE_ROWS
    w0 = jnp.clip(r0 - NA_KH // 2, 0, rows - NA_WIN_ROWS)

    def block_index(j, t):
        key_row = w0 + j
        i = r0 + 2 * t
        first_u, first_l = _na_first_row(i, rows), _na_first_row(i + 1, rows)
        in_u = (key_row >= first_u) & (key_row < first_u + NA_KH)
        in_l = (key_row >= first_l) & (key_row < first_l + NA_KH)
        d = jnp.clip(key_row - i + NA_KH - 1, 0, n_off - 1)
        return jnp.where(in_u & in_l, d, jnp.where(in_u, n_off, jnp.where(in_l, n_off + 1, n_off + 2)))

    idx = [[block_index(j, t) for t in range(NA_TILE_ROWS // 2)] for j in range(NA_WIN_ROWS)]

    def local_bias(s):
        heads = [jnp.concatenate([jnp.concatenate([t2_ref[2 * s + hh, i] for i in row], axis=1) for row in idx], axis=0)
                 for hh in range(2)]
        return jnp.concatenate(heads, axis=1)

    batch = pl.program_id(0)
    bounds = [bound_ref[batch, h] for h in range(NA_HEADS)]

    def cache_chunk(s):
        lanes = pl.ds(s * LANES, LANES)
        return [(lambda: kc_ref[:, lanes].astype(BF16), lambda: vc_ref[:, lanes].astype(BF16), None)]

    res = _attend_guarded(
        functools.reduce(jnp.maximum, bounds),
        lambda _: [(_na_queries(q_ref[s]),
                    _ref_chunks(k_ref, v_ref, s, nk, functools.partial(local_bias, s)) + cache_chunk(s),
                    True) for s in range(NA_SLABS)])
    for s in range(NA_SLABS):
        o_ref[:, s * LANES:(s + 1) * LANES] = _na_finish(*res[s]).T.astype(BF16)


def _na_lat_attn(q, k, v, kc, vc, t2, bound, seq, past, weights):
    slabs, n, _ = q.shape
    rows = seq // GRID_W
    tiles = rows // NA_TILE_ROWS
    nq = NA_TILE_ROWS * GRID_W
    nk = NA_WIN_ROWS * GRID_W

    def win_map(b, t):
        w0 = jnp.clip(t * NA_TILE_ROWS - NA_KH // 2, 0, rows - NA_WIN_ROWS)
        return (0, pl.multiple_of(b * seq + w0 * GRID_W, GRID_W), 0)

    win_spec = pl.BlockSpec((pl.Element(slabs), pl.Element(nk), pl.Element(LANES)), win_map)
    ctx_spec = pl.BlockSpec((past, slabs * LANES), lambda b, t: (b, 0))
    w_in, w_shapes, w_out, w_scratch = _side_cast_specs(weights, (n // seq) * tiles)
    outs = pl.pallas_call(
        functools.partial(_na_lat_kernel, rows, len(weights)),
        out_shape=(jax.ShapeDtypeStruct((n, slabs * LANES), BF16), *w_shapes),
        grid=(n // seq, tiles),
        in_specs=[pl.BlockSpec(memory_space=pltpu.SMEM),
                  pl.BlockSpec((slabs, nq, LANES), lambda b, t: (0, b * tiles + t, 0)),
                  win_spec, win_spec, ctx_spec, ctx_spec,
                  _const_spec(t2.shape)] + w_in,
        out_specs=(pl.BlockSpec((nq, slabs * LANES), lambda b, t: (b * tiles + t, 0)), *w_out),
        scratch_shapes=w_scratch,
        compiler_params=_params(2), name="l1_na_latent_attn",
    )(bound, q, k, v, kc, vc, t2, *weights)
    return outs[0], outs[1:]


def _pair_norm(x2, g2):
    low = lax.broadcasted_iota(jnp.int32, (1, LANES), 1) < NA_HEAD_DIM
    sq = x2 * x2
    ss_all = jnp.sum(sq, axis=-1, keepdims=True)
    ss_low = jnp.sum(jnp.where(low, sq, 0.0), axis=-1, keepdims=True)
    ss = jnp.where(low, ss_low, ss_all - ss_low)
    return x2 * lax.rsqrt(ss * (1.0 / NA_HEAD_DIM) + EPS) * g2


def _l1_pre_kernel(keep_f32, x_ref, mod_ref, nw_ref, win_ref, gq_ref, gk_ref, q_ref, k_ref, v_ref, *f32_refs):
    h = _modulate(x_ref[...], nw_ref[...], _ada_row(mod_ref, 0), _ada_row(mod_ref, 1)).astype(BF16)
    width = NA_SLABS * LANES
    q = _dot(h, win_ref[:, 0:width])
    k = _dot(h, win_ref[:, width:2 * width])
    v = _dot(h, win_ref[:, 2 * width:3 * width])
    gq = gq_ref[...]
    gk = gk_ref[...]
    for s in range(NA_SLABS):
        ls = slice(s * LANES, (s + 1) * LANES)
        q_ref[s] = _pair_norm(q[:, ls], gq).astype(BF16)
        kn = _pair_norm(k[:, ls], gk)
        k_ref[s] = kn.astype(BF16)
        v_ref[s] = v[:, ls].astype(BF16)
        if keep_f32:
            f32_refs[0][:, ls] = kn
    if keep_f32:
        f32_refs[1][...] = v


def _l1_pre(x, mod, tiles_per_group, keep_f32, w):
    n = x.shape[0]
    tm = TOKEN_TILE
    row = lambda i: (i, 0)
    width = NA_SLABS * LANES
    slab = jax.ShapeDtypeStruct((NA_SLABS, n, LANES), BF16)
    out_shape = [slab, slab, slab]
    out_specs = [_slab_spec(NA_SLABS, tm)] * 3
    if keep_f32:
        out_shape += [jax.ShapeDtypeStruct((n, width), F32)] * 2
        out_specs += [pl.BlockSpec((tm, width), row)] * 2
    return pl.pallas_call(
        functools.partial(_l1_pre_kernel, keep_f32),
        out_shape=tuple(out_shape), grid=(n // tm,),
        in_specs=[pl.BlockSpec((tm, D_MODEL), row),
                  _ada_spec(mod, tiles_per_group),
                  _const_spec((1, D_MODEL)),
                  _const_spec((D_MODEL, 3 * width)),
                  _const_spec((1, LANES)),
                  _const_spec((1, LANES))],
        out_specs=tuple(out_specs),
        compiler_params=_params(1), name="l1_pre_ctx" if keep_f32 else "l1_pre_latent",
    )(x, mod[0], w["norm_mix"], w["w_in"], w["gq"], w["gk"])


def _post_kernel(pool_seq, x_ref, attn_ref, *rest):
    rest = list(rest)
    take = lambda k: [rest.pop(0) for _ in range(k)]
    pool_in = take(8) if pool_seq is not None else None
    mod_ref, nw_ref, wo_ref, w1_ref, w2_ref, o_ref = take(6)
    pool_scratch = take(2) if pool_seq is not None else None
    step = pl.program_id(0)
    n_chunks = D_FF // FF_CHUNK
    if pool_seq is not None:
        first_refs, next_refs, (wp_ref, ps_ref) = pool_in[0:3], pool_in[3:6], pool_in[6:8]
        buf_ref, pooled_ref = pool_scratch

        @pl.when(step == 0)
        def _():
            for g, group in enumerate(_pool_mixer(pool_seq, 0, *first_refs, wp_ref, ps_ref, buf_ref)):
                pooled_ref[:, g * POOL_GROUP:(g + 1) * POOL_GROUP] = group()

        mixed = _dot(jnp.concatenate([pooled_ref[...], attn_ref[...]], axis=-1), wo_ref[...])
        next_groups = _pool_mixer(pool_seq, jnp.minimum(step + 1, pl.num_programs(0) - 1), *next_refs,
                                  wp_ref, ps_ref, buf_ref)
    else:
        mixed = _dot(attn_ref[...], wo_ref[...])
    x1 = x_ref[...] + _ada_row(mod_ref, 2) * mixed
    h = _modulate(x1, nw_ref[...], _ada_row(mod_ref, 3), _ada_row(mod_ref, 4)).astype(BF16)
    acc = jnp.zeros(x1.shape, F32)
    for c in range(n_chunks):
        cs = pl.ds(c * FF_CHUNK, FF_CHUNK)
        u = jnp.square(jnp.maximum(_dot(h, w1_ref[:, cs]), 0.0)).astype(BF16)
        acc = acc + _dot(u, w2_ref[cs, :])
        if pool_seq is not None:
            for g in range(c * len(next_groups) // n_chunks, (c + 1) * len(next_groups) // n_chunks):
                pooled_ref[:, g * POOL_GROUP:(g + 1) * POOL_GROUP] = next_groups[g]()
    o_ref[...] = x1 + _ada_row(mod_ref, 5) * acc


def _post(x, attn, mod, tiles_per_group, w, name, pool=None):
    n = x.shape[0]
    tm = TOKEN_TILE
    row = lambda i: (i, 0)
    in_specs = [pl.BlockSpec((tm, D_MODEL), row), pl.BlockSpec((tm, attn.shape[1]), row)]
    args = [x, attn]
    scratch = []
    pool_seq = None
    if pool is not None:
        a_in, pool_seq = pool
        hb = tm // POOL_HALO
        last = n // POOL_HALO - 1
        pieces = tm // min(tm, pool_seq)
        last_tile = n // tm - 1

        def tile_specs(tile_of):
            return [pl.BlockSpec((POOL_HALO, POOL_WIDTH), lambda i: (jnp.maximum(tile_of(i) * hb - 1, 0), 0)),
                    pl.BlockSpec((tm, POOL_WIDTH), lambda i: (tile_of(i), 0)),
                    pl.BlockSpec((POOL_HALO, POOL_WIDTH), lambda i: (jnp.minimum((tile_of(i) + 1) * hb, last), 0))]

        in_specs += tile_specs(lambda i: 0) + tile_specs(lambda i: jnp.minimum(i + 1, last_tile))
        in_specs += [_const_spec((len(POOL_WINDOWS), POOL_GROUP, POOL_GROUP)), _const_spec((1, POOL_WIDTH))]
        args += [a_in] * 6 + [w["w_pool"], w["pool_scale"]]
        scratch = [pltpu.VMEM((tm + 2 * POOL_HALO * pieces, POOL_WIDTH), F32), pltpu.VMEM((tm, POOL_WIDTH), BF16)]
    in_specs += [_ada_spec(mod, tiles_per_group),
                 _const_spec((1, D_MODEL)),
                 _const_spec((D_MODEL, D_MODEL)),
                 _const_spec((D_MODEL, D_FF)),
                 _const_spec((D_FF, D_MODEL))]
    args += [mod[0], w["norm_mlp"], w["w_out"], w["w_mlp1"], w["w_mlp2"]]
    return pl.pallas_call(
        functools.partial(_post_kernel, pool_seq),
        out_shape=jax.ShapeDtypeStruct((n, D_MODEL), F32), grid=(n // tm,),
        in_specs=in_specs, out_specs=pl.BlockSpec((tm, D_MODEL), row),
        scratch_shapes=scratch, compiler_params=_params(1), name=name,
    )(*args)


def _rope_tables(seq):
    assert seq == GRID_W * GRID_W
    half = MLA_ROPE // 2
    inv_freq = jnp.power(ROPE_THETA, -jnp.arange(0, half, 2, dtype=F32) / half)
    ang = jnp.arange(GRID_W, dtype=F32)[:, None] * inv_freq[None, :]
    cos, sin = jnp.cos(ang), jnp.sin(ang)
    fill = lambda n, v=0.0: jnp.full((GRID_W, n), v, F32)
    spare = LANES - MLA_QK
    return jnp.stack([
        jnp.concatenate([fill(MLA_NOPE, 1.0), fill(half), cos, cos, fill(spare)], axis=1),
        jnp.concatenate([fill(MLA_NOPE), cos, cos, fill(half), fill(spare)], axis=1),
        jnp.concatenate([fill(MLA_NOPE), fill(half), -sin, sin, fill(spare)], axis=1),
        jnp.concatenate([fill(MLA_NOPE), -sin, sin, fill(half), fill(spare)], axis=1)])


def _rope_partner(tail):
    shape = tail.shape
    t = tail.reshape(shape[:-1] + (2, 2, MLA_ROPE // 4))
    return t[..., ::-1, :].reshape(shape)


def _na_bias_table(rel_bias, rows):
    qcol = jnp.arange(GRID_W)
    kcol = jnp.arange(GRID_W)
    col_start = jnp.clip(qcol - NA_KW // 2, 0, GRID_W - NA_KW)
    col_mask = (kcol[:, None] >= col_start[None, :]) & (kcol[:, None] < col_start[None, :] + NA_KW)
    offs = jnp.arange(2 * NA_KW - 1)
    onehot = (kcol[None, :, None] - qcol[None, None, :] + NA_KW - 1 == offs[:, None, None]).astype(F32)
    blocks = jnp.einsum('hdj,jkq->hdkq', rel_bias.astype(F32) * LOG2E, onehot, precision=lax.Precision.HIGHEST)
    blocks = jnp.where(col_mask[None, None], blocks, NEG_INF)
    neg = jnp.full_like(blocks[:, :1], NEG_INF)
    both = jnp.concatenate([blocks, jnp.concatenate([neg, blocks[:, :-1]], axis=1)], axis=-1)
    d_upper, d_lower = _na_partial_offsets(rows)
    upper_only = jnp.concatenate([blocks[:, d_upper:d_upper + 1], neg], axis=-1)
    lower_only = jnp.concatenate([neg, blocks[:, d_lower - 1:d_lower]], axis=-1)
    return jnp.concatenate([both, upper_only, lower_only, jnp.concatenate([neg, neg], axis=-1)], axis=1)


def _na_partial_offsets(rows):
    first = lambda r: min(max(r - NA_KH // 2, 0), rows - NA_KH)
    upper, lower = set(), set()
    for r0 in range(0, rows, NA_TILE_ROWS):
        w0 = min(max(r0 - NA_KH // 2, 0), rows - NA_WIN_ROWS)
        for j in range(NA_WIN_ROWS):
            for t in range(NA_TILE_ROWS // 2):
                key_row, i = w0 + j, r0 + 2 * t
                in_u = first(i) <= key_row < first(i) + NA_KH
                in_l = first(i + 1) <= key_row < first(i + 1) + NA_KH
                d = key_row - i + NA_KH - 1
                if in_u and in_l:
                    assert 1 <= d <= 2 * NA_KH - 2
                elif in_u:
                    upper.add(d)
                elif in_l:
                    lower.add(d)
    assert len(upper) == 1 and len(lower) == 1, (upper, lower)
    return upper.pop(), lower.pop()


def _prep_l0(w_in, q_lora_norm, kv_lora_norm, w_q_up, w_kv_up, mla_q_norm, mla_k_norm, w_pool, pool_scale,
             w_out, norm_mix, norm_mlp, w_mlp1, w_mlp2):
    lat_end = POOL_WIDTH + Q_LORA + KV_LORA
    w_in_pad = jnp.concatenate([w_in[:, :lat_end], jnp.zeros((D_MODEL, MLA_NOPE), w_in.dtype),
                                w_in[:, lat_end:], _rope_partner(w_in[:, lat_end:])], axis=1)
    pad = LANES - MLA_QK
    hk = MLA_HEADS * LANES
    wq = w_q_up.reshape(Q_LORA, MLA_HEADS, MLA_QK)
    wq = jnp.concatenate([wq, _rope_partner(wq[:, :, MLA_NOPE:])], axis=-1)
    kv = w_kv_up.reshape(KV_LORA, MLA_HEADS, MLA_NOPE + MLA_V)
    wk = jnp.pad(kv[:, :, :MLA_NOPE], ((0, 0), (0, 0), (0, LANES - MLA_NOPE)))
    wv = jnp.pad(kv[:, :, MLA_NOPE:], ((0, 0), (0, 0), (0, LANES - MLA_V)))
    gq = mla_q_norm * (MLA_QK ** -0.5 * LOG2E)
    slab_lane = jnp.arange(2 * LANES)
    e2 = (slab_lane[:, None] // LANES == slab_lane[None, :] // LANES) & (slab_lane[:, None] % LANES < MLA_QK)
    return dict(
        norm_mix=norm_mix.reshape(1, -1), norm_mlp=norm_mlp.reshape(1, -1),
        w_in=w_in_pad.astype(BF16),
        qln=q_lora_norm.reshape(1, -1), kvn=kv_lora_norm.reshape(1, -1),
        wq=wq.reshape(Q_LORA, hk).astype(BF16),
        gq_plain=jnp.pad(gq, (0, pad)).reshape(1, LANES),
        gq_rope=jnp.concatenate([gq, _rope_partner(gq[MLA_NOPE:])]).reshape(1, LANES),
        wk=wk.reshape(KV_LORA, hk).astype(BF16),
        gk_plain=jnp.pad(mla_k_norm, (0, pad)).reshape(1, LANES),
        gk_rope=jnp.concatenate([mla_k_norm, _rope_partner(mla_k_norm[MLA_NOPE:])]).reshape(1, LANES),
        wv=wv.reshape(KV_LORA, hk).astype(BF16),
        e2=e2.astype(BF16),
        w_pool=w_pool.astype(BF16), pool_scale=pool_scale.reshape(1, -1),
        w_out=w_out, w_mlp1=w_mlp1, w_mlp2=w_mlp2)


def kernel(x_prompt, x_sample, cache_l0_mla_ckv, cache_l0_mla_krope, cache_l1_na_k, cache_l1_na_v, c, c_ctx, w_ada_l0, b_ada_l0, norm_mix_l0, norm_mlp_l0, w_mlp1_l0, w_mlp2_l0, w_in_l0, q_lora_norm_l0, kv_lora_norm_l0, w_q_up_l0, w_kv_up_l0, mla_q_norm_l0, mla_k_norm_l0, w_pool_l0, pool_scale_l0, w_out_l0, w_ada_l1, b_ada_l1, norm_mix_l1, norm_mlp_l1, w_mlp1_l1, w_mlp2_l1, w_in_l1, na_q_norm_l1, na_k_norm_l1, rel_bias_l1, w_out_l1):
    batch, seq, _ = x_prompt.shape
    dec_batch, dec_seq, _ = x_sample.shape
    past = cache_l0_mla_ckv.shape[1]
    xp = x_prompt.reshape(batch * seq, D_MODEL)
    xs = x_sample.reshape(dec_batch * dec_seq, D_MODEL)
    tiles_p = (batch * seq) // TOKEN_TILE
    tiles_s = dec_seq // TOKEN_TILE

    cond8 = jnp.concatenate([c_ctx[None, :], c, jnp.zeros((SUBLANES - 1 - dec_batch, D_MODEL), F32)], axis=0)
    mods = _ada(cond8, (w_ada_l0, w_ada_l1), (b_ada_l0, b_ada_l1))
    mods = mods.reshape(DEPTH, SUBLANES, 1, N_ADA * D_MODEL)
    (mod0_p, mod0_s), (mod1_p, mod1_s) = (((mods[l], 0), (mods[l], 1)) for l in range(DEPTH))

    w0 = _prep_l0(w_in_l0, q_lora_norm_l0, kv_lora_norm_l0, w_q_up_l0, w_kv_up_l0, mla_q_norm_l0, mla_k_norm_l0,
                  w_pool_l0, pool_scale_l0, w_out_l0, norm_mix_l0, norm_mlp_l0, w_mlp1_l0, w_mlp2_l0)
    a_p, q_p, k_p, v_p, ckv_p, kr_p = _l0_pre(xp, mod0_p, tiles_p, False, w0, None)
    a_s, q_s, k_s, v_s, _, _ = _l0_pre(xs, mod0_s, tiles_s, True, w0, _rope_tables(dec_seq))
    kr_cache = jnp.pad(cache_l0_mla_krope.reshape(dec_batch * past, MLA_ROPE), ((0, 0), (MLA_NOPE, LANES - MLA_QK)))
    k_c, v_c = _cache_kv(cache_l0_mla_ckv.reshape(dec_batch * past, KV_LORA), kr_cache, w0)
    mla_bound = _normed_len(mla_q_norm_l0 * (MLA_QK ** -0.5 * LOG2E), MLA_QK) * _normed_len(mla_k_norm_l0, MLA_QK)
    attn_p = _ctx_attn(_mla_ctx_kernel, q_p, k_p, v_p, mla_bound, seq, MLA_HEADS * MLA_V, "l0_mla_ctx_attn")
    attn_s, w16 = _mla_lat_attn(q_s, k_s, v_s, k_c, v_c, jnp.full((dec_batch, MLA_HEADS), mla_bound, F32),
                                dec_seq, past, (w0["w_out"], w0["w_mlp1"], w0["w_mlp2"], w_in_l1))
    w0 = dict(w0, w_out=w16[0], w_mlp1=w16[1], w_mlp2=w16[2])
    xp = _post(xp, attn_p, mod0_p, tiles_p, w0, "l0_post_ctx", pool=(a_p, seq))
    xs = _post(xs, attn_s, mod0_s, tiles_s, w0, "l0_post_latent", pool=(a_s, dec_seq))

    w1 = dict(norm_mix=norm_mix_l1.reshape(1, -1), norm_mlp=norm_mlp_l1.reshape(1, -1),
              w_in=w16[3],
              gq=jnp.tile(na_q_norm_l1 * (NA_HEAD_DIM ** -0.5 * LOG2E), 2).reshape(1, LANES),
              gk=jnp.tile(na_k_norm_l1, 2).reshape(1, LANES),
              w_out=w_out_l1, w_mlp1=w_mlp1_l1, w_mlp2=w_mlp2_l1)
    q1_p, k1_p, v1_p, k1_new, v1_new = _l1_pre(xp, mod1_p, tiles_p, True, w1)
    q1_s, k1_s, v1_s = _l1_pre(xs, mod1_s, tiles_s, False, w1)
    q_len, k_len = _normed_len(w1["gq"], NA_HEAD_DIM), _normed_len(w1["gk"], NA_HEAD_DIM)
    na_p = _ctx_attn(_na_ctx_kernel, q1_p, k1_p, v1_p, q_len * k_len, seq, NA_SLABS * LANES, "l1_na_ctx_attn")
    cache_len = jnp.sqrt(jnp.max(jnp.sum(jnp.square(cache_l1_na_k.astype(F32)), axis=-1), axis=1))
    na_bound = (q_len * jnp.maximum(k_len, cache_len)
                + jnp.max(jnp.abs(rel_bias_l1.astype(F32)) * LOG2E, axis=(1, 2))[None, :])
    na_s, w16 = _na_lat_attn(q1_s, k1_s, v1_s,
                             cache_l1_na_k.reshape(dec_batch * past, -1), cache_l1_na_v.reshape(dec_batch * past, -1),
                             _na_bias_table(rel_bias_l1, dec_seq // GRID_W), na_bound, dec_seq, past,
                             (w1["w_out"], w1["w_mlp1"], w1["w_mlp2"]))
    w1 = dict(w1, w_out=w16[0], w_mlp1=w16[1], w_mlp2=w16[2])
    xp = _post(xp, na_p, mod1_p, tiles_p, w1, "l1_post_ctx")
    xs = _post(xs, na_s, mod1_s, tiles_s, w1, "l1_post_latent")

    return (xp.reshape(batch, seq, D_MODEL), xs.reshape(dec_batch, dec_seq, D_MODEL),
            ckv_p.reshape(batch, seq, KV_LORA), kr_p.reshape(batch, seq, MLA_ROPE),
            k1_new.reshape(batch, seq, NA_HEADS, NA_HEAD_DIM), v1_new.reshape(batch, seq, NA_HEADS, NA_HEAD_DIM))
```

```python
import functools

import jax
import jax.numpy as jnp
from jax import lax
from jax.experimental import pallas as pl
from jax.experimental.pallas import tpu as pltpu

F32 = jnp.float32
BF16 = jnp.bfloat16

D_MODEL = 1024
D_FF = 4 * D_MODEL
DEPTH = 2
N_ADA = 6
EPS = 1e-6
NEG_INF = -1e30
ROPE_THETA = 10000.0
GRID_W = 64
POOL_WIDTH = 512
POOL_WINDOWS = (2, 4, 8, 16)
POOL_GROUP = 128
POOL_HALO = 8
MLA_HEADS = 8
MLA_NOPE = 64
MLA_ROPE = 32
MLA_QK = MLA_NOPE + MLA_ROPE
MLA_V = 64
Q_LORA = 256
KV_LORA = 128
NA_HEADS = 16
NA_HEAD_DIM = 64
NA_KH = 8
NA_KW = 16
LOG2E = 1.4426950408889634

LANES = 128
SUBLANES = 8
VMEM_LIMIT = 56 * 1024 * 1024

TOKEN_TILE = 512
MLA_Q_TILE = 512
MLA_STEP_HEADS = 2
MLA_KEY_CHUNK = 2048
MLA_ONLINE_CHUNK = 1024
SCORE_LOOKAHEAD = 4
MAX_SCORE_BOUND = 40.0
FF_CHUNK = 1024
CTX_STEP_SEQS = 2
NA_SLABS = NA_HEADS * NA_HEAD_DIM // LANES
NA_TILE_ROWS = 4
NA_WIN_ROWS = NA_TILE_ROWS + NA_KH - 1


def _params(n_axes):
    return pltpu.CompilerParams(dimension_semantics=("arbitrary",) * n_axes,
                                vmem_limit_bytes=VMEM_LIMIT)


def _const_spec(shape):
    zeros = (0,) * len(shape)
    return pl.BlockSpec(shape, lambda *_: zeros, pipeline_mode=pl.Buffered(1))


def _dot(a, b):
    return jnp.dot(a, b, preferred_element_type=F32)


def _dot_nt(a, b):
    return lax.dot_general(a, b, (((1,), (1,)), ((), ())), preferred_element_type=F32)


def _dot_tn(a, b):
    return lax.dot_general(a, b, (((0,), (0,)), ((), ())), preferred_element_type=F32)


def _rms(x, w, n=None):
    n = x.shape[-1] if n is None else n
    ss = jnp.sum(x * x, axis=-1, keepdims=True)
    return x * lax.rsqrt(ss * (1.0 / n) + EPS) * w


def _modulate(x, nw, shift, scale):
    return _rms(x, nw) * (1.0 + scale) + shift


def _ada_spec(mod, tiles_per_group):
    mods, first_row = mod
    return pl.BlockSpec((None, 1, mods.shape[-1]), lambda i: (first_row + i // tiles_per_group, 0, 0))


def _ada_row(mod_ref, k):
    return mod_ref[:, k * D_MODEL:(k + 1) * D_MODEL]


def _ada_kernel(cond_ref, w0_ref, w1_ref, b_ref, o_ref):
    c = cond_ref[...]
    s = (c * jax.nn.sigmoid(c)).astype(BF16)
    for layer, w_ref in enumerate((w0_ref, w1_ref)):
        @pl.when(pl.program_id(0) == layer)
        def _(w_ref=w_ref):
            o_ref[...] = _dot(s, w_ref[...].astype(BF16)) + b_ref[...]


def _ada(cond8, w_ada, b_ada):
    n_out = w_ada[0].shape[1]
    tn = D_MODEL
    nb = n_out // tn
    return pl.pallas_call(
        _ada_kernel,
        out_shape=jax.ShapeDtypeStruct((DEPTH, SUBLANES, n_out), F32),
        grid=(DEPTH, nb),
        in_specs=[pl.BlockSpec((SUBLANES, D_MODEL), lambda l, j: (0, 0)),
                  pl.BlockSpec((D_MODEL, tn), lambda l, j: (0, jnp.where(l == 0, j, nb - 1))),
                  pl.BlockSpec((D_MODEL, tn), lambda l, j: (0, jnp.where(l == 1, j, 0))),
                  pl.BlockSpec((None, 1, tn), lambda l, j: (l, 0, j))],
        out_specs=pl.BlockSpec((None, SUBLANES, tn), lambda l, j: (l, 0, j)),
        compiler_params=_params(2),
        name="ada",
    )(cond8, w_ada[0], w_ada[1], jnp.stack(b_ada).reshape(DEPTH, 1, n_out))


def _rope(x, c, s):
    return x * c + pltpu.roll(x, LANES - MLA_ROPE, 1) * s


PAIR = 2 * LANES


def _head_rinv(raw, e2_ref):
    ss = _dot((raw * raw).astype(BF16), e2_ref[...])
    return lax.rsqrt(ss * (1.0 / MLA_QK) + EPS)


def _mla_keys_values(ckv, kr_slab, wk_ref, gk, wv_ref, e2_ref, rope_tabs, k_ref, v_ref):
    c16 = ckv.astype(BF16)
    tail = kr_slab * gk
    if rope_tabs is not None:
        tail = _rope(tail, *rope_tabs)
    ones_lane = (lax.broadcasted_iota(jnp.int32, (1, LANES), 1) == MLA_V).astype(F32)
    for p in range(MLA_HEADS // 2):
        cols = pl.ds(p * PAIR, PAIR)
        kn = _dot(c16, wk_ref[:, cols])
        v = _dot(c16, wv_ref[:, cols])
        rinv = _head_rinv(kn + jnp.tile(kr_slab, (1, 2)), e2_ref)
        for i in range(2):
            hs = slice(i * LANES, (i + 1) * LANES)
            k_ref[2 * p + i] = (rinv[:, hs] * (kn[:, hs] * gk + tail)).astype(BF16)
            v_ref[2 * p + i] = (v[:, hs] + ones_lane).astype(BF16)


def _l0_pre_kernel(latent, x_ref, mod_ref, nw_ref, win_ref, qln_ref, kvn_ref, wq_ref, gq_ref,
                   wk_ref, gk_ref, wv_ref, e2_ref, *rest):
    if latent:
        rope_ref, a_ref, q_ref, k_ref, v_ref, ckv_ref, kr_ref = rest
        tile_rows = x_ref.shape[0] // GRID_W
        row0 = (pl.program_id(0) * tile_rows) % rope_ref.shape[1]
        expand = lambda col, row: jnp.concatenate(
            [rope_ref[col] + rope_ref[row, pl.ds(row0 + r, 1), :] for r in range(tile_rows)], axis=0)
        rope_tabs = (expand(0, 1), expand(2, 3))
    else:
        a_ref, q_ref, k_ref, v_ref, ckv_ref, kr_ref = rest
        rope_tabs = None
    h = _modulate(x_ref[...], nw_ref[...], _ada_row(mod_ref, 0), _ada_row(mod_ref, 1)).astype(BF16)
    proj = _dot(h, win_ref[...])
    a_ref[...] = proj[:, :POOL_WIDTH]
    q_lat = proj[:, POOL_WIDTH:POOL_WIDTH + Q_LORA]
    kv_lat = proj[:, POOL_WIDTH + Q_LORA:POOL_WIDTH + Q_LORA + KV_LORA]
    kr_slab = proj[:, POOL_WIDTH + Q_LORA + KV_LORA:]
    ql = _rms(q_lat, qln_ref[...]).astype(BF16)
    gq = gq_ref[...]
    for p in range(MLA_HEADS // 2):
        q = _dot(ql, wq_ref[:, pl.ds(p * PAIR, PAIR)])
        q_rinv = _head_rinv(q, e2_ref)
        for i in range(2):
            hs = slice(i * LANES, (i + 1) * LANES)
            y = q[:, hs] * q_rinv[:, hs] * gq
            if rope_tabs is not None:
                y = _rope(y, *rope_tabs)
            q_ref[2 * p + i] = y.astype(BF16)
    ckv = _rms(kv_lat, kvn_ref[...])
    ckv_ref[...] = ckv
    kr_ref[...] = kr_slab[:, MLA_NOPE:MLA_QK]
    _mla_keys_values(ckv, kr_slab, wk_ref, gk_ref[...], wv_ref, e2_ref, rope_tabs, k_ref, v_ref)


def _slab_spec(slabs, tm):
    return pl.BlockSpec((slabs, tm, LANES), lambda i: (0, i, 0))


def _l0_pre(x, mod, tiles_per_group, latent, w, rope_tabs):
    n = x.shape[0]
    tm = TOKEN_TILE
    row = lambda i: (i, 0)
    hk = MLA_HEADS * LANES
    in_specs = [pl.BlockSpec((tm, D_MODEL), row),
                _ada_spec(mod, tiles_per_group),
                _const_spec((1, D_MODEL)),
                _const_spec((D_MODEL, D_MODEL)),
                _const_spec((1, Q_LORA)),
                _const_spec((1, KV_LORA)),
                _const_spec((Q_LORA, hk)),
                _const_spec((1, LANES)),
                _const_spec((KV_LORA, hk)),
                _const_spec((1, LANES)),
                _const_spec((KV_LORA, hk)),
                _const_spec((2 * LANES, 2 * LANES))]
    gains = "rope" if latent else "plain"
    args = [x, mod[0], w["norm_mix"], w["w_in"], w["qln"], w["kvn"], w["wq"], w["gq_" + gains], w["wk"],
            w["gk_" + gains], w["wv"], w["e2"]]
    if latent:
        assert tm % GRID_W == 0
        in_specs.append(_const_spec(rope_tabs.shape))
        args.append(rope_tabs)
    slab = jax.ShapeDtypeStruct((MLA_HEADS, n, LANES), BF16)
    out_shape = (jax.ShapeDtypeStruct((n, POOL_WIDTH), F32), slab, slab, slab,
                 jax.ShapeDtypeStruct((n, KV_LORA), F32),
                 jax.ShapeDtypeStruct((n, MLA_ROPE), F32))
    out_specs = (pl.BlockSpec((tm, POOL_WIDTH), row),
                 _slab_spec(MLA_HEADS, tm), _slab_spec(MLA_HEADS, tm), _slab_spec(MLA_HEADS, tm),
                 pl.BlockSpec((tm, KV_LORA), row), pl.BlockSpec((tm, MLA_ROPE), row))
    return pl.pallas_call(
        functools.partial(_l0_pre_kernel, latent),
        out_shape=out_shape, grid=(n // tm,), in_specs=in_specs, out_specs=out_specs,
        compiler_params=_params(1), name="l0_pre_latent" if latent else "l0_pre_ctx",
    )(*args)


def _cache_kv_kernel(ckv_ref, kr_ref, wk_ref, gk_ref, wv_ref, e2_ref, k_ref, v_ref):
    _mla_keys_values(ckv_ref[...], kr_ref[...], wk_ref, gk_ref[...], wv_ref, e2_ref, None, k_ref, v_ref)


def _cache_kv(ckv, kr_slab, w):
    n = ckv.shape[0]
    tm = TOKEN_TILE
    row = lambda i: (i, 0)
    hk = MLA_HEADS * LANES
    slab = jax.ShapeDtypeStruct((MLA_HEADS, n, LANES), BF16)
    return pl.pallas_call(
        _cache_kv_kernel,
        out_shape=(slab, slab),
        grid=(n // tm,),
        in_specs=[pl.BlockSpec((tm, KV_LORA), row), pl.BlockSpec((tm, LANES), row),
                  _const_spec((KV_LORA, hk)), _const_spec((1, LANES)), _const_spec((KV_LORA, hk)),
                  _const_spec((2 * LANES, 2 * LANES))],
        out_specs=(_slab_spec(MLA_HEADS, tm), _slab_spec(MLA_HEADS, tm)),
        compiler_params=_params(1), name="l0_cache_kv",
    )(ckv, kr_slab, w["wk"], w["gk_plain"], w["wv"], w["e2"])


def _pool_mixer(seq_len, tile, prev_ref, u_ref, next_ref, wp_ref, ps_ref, buf_ref):
    tm = u_ref.shape[0]
    piece = min(tm, seq_len)
    n_pieces = tm // piece
    stride = piece + 2 * POOL_HALO
    starts = []
    for j in range(n_pieces):
        pos0 = (tile * tm + j * piece) % seq_len
        prev = prev_ref[...] if j == 0 else u_ref[j * piece - POOL_HALO:j * piece, :]
        nxt = next_ref[...] if j == n_pieces - 1 else u_ref[(j + 1) * piece:(j + 1) * piece + POOL_HALO, :]
        base = j * stride
        buf_ref[base:base + POOL_HALO, :] = jnp.where(pos0 != 0, prev, 0.0)
        buf_ref[base + POOL_HALO:base + POOL_HALO + piece, :] = u_ref[j * piece:(j + 1) * piece, :]
        buf_ref[base + POOL_HALO + piece:base + stride, :] = jnp.where(pos0 + piece != seq_len, nxt, 0.0)
        starts.append(pos0)
    def group(g):
        win = POOL_WINDOWS[g]
        lanes = pl.ds(g * POOL_GROUP, POOL_GROUP)
        pooled = []
        for j in range(n_pieces):
            centre = j * stride + POOL_HALO
            lo = starts[j] + lax.broadcasted_iota(jnp.int32, (piece, 1), 0) - win // 2
            cnt = (jnp.minimum(lo + win, seq_len) - jnp.maximum(lo, 0)).astype(F32)
            acc = buf_ref[pl.ds(centre - win // 2, piece), lanes]
            for d in range(1 - win // 2, win - win // 2):
                acc = acc + buf_ref[pl.ds(centre + d, piece), lanes]
            pooled.append(acc / cnt - buf_ref[pl.ds(centre, piece), lanes])
        pooled = pooled[0] if n_pieces == 1 else jnp.concatenate(pooled, axis=0)
        return (_dot(pooled.astype(BF16), wp_ref[g]) * ps_ref[:, lanes]).astype(BF16)

    return [functools.partial(group, g) for g in range(len(POOL_WINDOWS))]


def _attend(streams, running_max):
    items = [(si, ch) for si, (_, chunks, _) in enumerate(streams) for ch in chunks]

    def score(item):
        si, (k_fn, _, extra_fn) = item
        s = _dot_nt(k_fn(), streams[si][0])
        return s if extra_fn is None else s + extra_fn()

    state = [None] * len(streams)
    pending = [score(item) for item in items[:SCORE_LOOKAHEAD]]
    for i, (si, (_, v_fn, _)) in enumerate(items):
        s = pending.pop(0)
        if i + SCORE_LOOKAHEAD < len(items):
            pending.append(score(items[i + SCORE_LOOKAHEAD]))
        want_den = streams[si][2]
        m_new, alpha = None, 1.0
        if state[si] is not None:
            m, acc, den = state[si]
        if running_max:
            m_new = jnp.max(s, axis=0, keepdims=True)
            if state[si] is not None:
                m_new = jnp.maximum(m, m_new)
                alpha = jnp.exp2(m - m_new)
            s = s - m_new
        p = jnp.exp2(s)
        pv = _dot_tn(v_fn(), p.astype(BF16))
        psum = jnp.sum(p, axis=0, keepdims=True) if want_den else None
        if state[si] is None:
            state[si] = (m_new, pv, psum)
        elif running_max:
            state[si] = (m_new, acc * alpha + pv, den * alpha + psum if want_den else None)
        else:
            state[si] = (None, acc + pv, den + psum if want_den else None)
    return [(acc, den) for _, acc, den in state]


def _ref_chunks(k_ref, v_ref, slab, chunk, extra_fn=None, lo=0, hi=None):
    hi = k_ref.shape[1] if hi is None else hi
    out = []
    for c in range(lo, hi, chunk):
        rows = pl.ds(c, min(chunk, hi - c))
        out.append((lambda rows=rows: k_ref[slab, rows, :], lambda rows=rows: v_ref[slab, rows, :], extra_fn))
    return out


def _mla_finish(o_t):
    return o_t[:MLA_V] / o_t[MLA_V:MLA_V + 1]


def _na_queries(q2):
    low = lax.broadcasted_iota(jnp.int32, (1, LANES), 1) < NA_HEAD_DIM
    zero = jnp.zeros_like(q2)
    return jnp.concatenate([jnp.where(low, q2, zero), jnp.where(low, zero, q2)], axis=0)


def _na_finish(o_t, den):
    nq = o_t.shape[1] // 2
    o = o_t / den
    return jnp.concatenate([o[:NA_HEAD_DIM, :nq], o[NA_HEAD_DIM:, nq:]], axis=0)


def _attend_guarded(bound, streams_fn):
    return lax.cond(bound <= MAX_SCORE_BOUND,
                    lambda: _attend(streams_fn(False), False),
                    lambda: _attend(streams_fn(True), True))


def _mla_ctx_kernel(seq, bound_ref, q_ref, k_ref, v_ref, o_ref):
    spans = [(lo, lo + seq) for lo in range(0, k_ref.shape[1], seq)]
    res = _attend_guarded(bound_ref[0, 0], lambda _: [
        (q_ref[h, lo:hi, :], _ref_chunks(k_ref, v_ref, h, seq, lo=lo, hi=hi), False)
        for lo, hi in spans for h in range(MLA_HEADS)])
    for i, (lo, hi) in enumerate(spans):
        for hp in range(MLA_HEADS // 2):
            o_t = jnp.concatenate([_mla_finish(res[i * MLA_HEADS + h][0]) for h in (2 * hp, 2 * hp + 1)], axis=0)
            o_ref[lo:hi, hp * LANES:(hp + 1) * LANES] = o_t.T.astype(BF16)


def _na_ctx_kernel(seq, bound_ref, q_ref, k_ref, v_ref, o_ref):
    spans = [(lo, lo + seq) for lo in range(0, k_ref.shape[1], seq)]
    res = _attend_guarded(bound_ref[0, 0], lambda _: [
        (_na_queries(q_ref[s, lo:hi, :]), _ref_chunks(k_ref, v_ref, s, seq, lo=lo, hi=hi), True)
        for lo, hi in spans for s in range(NA_SLABS)])
    for i, (lo, hi) in enumerate(spans):
        for s in range(NA_SLABS):
            o_ref[lo:hi, s * LANES:(s + 1) * LANES] = _na_finish(*res[i * NA_SLABS + s]).T.astype(BF16)


def _ctx_attn(body, q, k, v, bound, seq, width, name):
    slabs, n, _ = q.shape
    rows = CTX_STEP_SEQS * seq
    spec = pl.BlockSpec((slabs, rows, LANES), lambda b: (0, b, 0))
    return pl.pallas_call(
        functools.partial(body, seq),
        out_shape=jax.ShapeDtypeStruct((n, width), BF16),
        grid=(n // rows,),
        in_specs=[pl.BlockSpec(memory_space=pltpu.SMEM), spec, spec, spec],
        out_specs=pl.BlockSpec((rows, width), lambda b: (b, 0)),
        compiler_params=_params(1), name=name,
    )(bound.reshape(1, 1), q, k, v)


SIDE_PIECE_ELEMS = 512 * 1024


def _side_piece_rows(n_cols):
    return 1 << ((SIDE_PIECE_ELEMS // n_cols).bit_length() - 1)


def _side_cast_plan(shapes):
    plan = []
    for wi, (n_rows, n_cols) in enumerate(shapes):
        rows = _side_piece_rows(n_cols)
        assert n_rows % rows == 0
        plan += [(wi, r0, rows) for r0 in range(0, n_rows, rows)]
    return plan


def _side_cast_scratch(shapes):
    widths = sorted({c for _, c in shapes})
    stages = [pltpu.VMEM((2, _side_piece_rows(c), c), dt) for dt in (F32, BF16) for c in widths]
    return stages + [pltpu.SemaphoreType.DMA((2,)), pltpu.SemaphoreType.DMA((2,))]


def _side_cast(step, srcs, dsts, scratch):
    shapes = [s.shape for s in srcs]
    widths = sorted({c for _, c in shapes})
    stage_in = dict(zip(widths, scratch[:len(widths)]))
    stage_out = dict(zip(widths, scratch[len(widths):2 * len(widths)]))
    sem_in, sem_out = scratch[2 * len(widths):]
    plan = _side_cast_plan(shapes)

    def refs(k):
        wi, r0, rows = plan[k]
        cols = shapes[wi][1]
        return (srcs[wi].at[pl.ds(r0, rows), :], dsts[wi].at[pl.ds(r0, rows), :],
                stage_in[cols].at[k % 2], stage_out[cols].at[k % 2])

    read = lambda k: pltpu.make_async_copy(refs(k)[0], refs(k)[2], sem_in.at[k % 2])
    write = lambda k: pltpu.make_async_copy(refs(k)[3], refs(k)[1], sem_out.at[k % 2])
    n = len(plan)

    def piece_step(k):
        if k == 0:
            read(0).start()
        if k >= 2:
            write(k - 2).wait()
        if k < n:
            read(k).wait()
            if k + 1 < n:
                read(k + 1).start()
            refs(k)[3][...] = refs(k)[2][...].astype(BF16)
            write(k).start()

    @pl.when(step < n + 2)
    def _():
        for k in range(n + 2):
            pl.when(step == k)(functools.partial(piece_step, k))


def _mla_lat_kernel(n_cast, bound_ref, q_ref, kl_ref, vl_ref, kc_ref, vc_ref, *rest):
    o_ref = rest[n_cast]
    linear_step = (pl.program_id(0) * pl.num_programs(1) + pl.program_id(1)) * pl.num_programs(2) + pl.program_id(2)
    _side_cast(linear_step, rest[:n_cast], rest[n_cast + 1:2 * n_cast + 1], rest[2 * n_cast + 1:])
    batch, head0 = pl.program_id(0), pl.program_id(1) * MLA_STEP_HEADS
    heads = range(MLA_STEP_HEADS)
    bounds = [bound_ref[batch, head0 + h] for h in heads]

    def streams(running_max):
        size = MLA_ONLINE_CHUNK if running_max else MLA_KEY_CHUNK
        return [(q_ref[h], _ref_chunks(kl_ref, vl_ref, h, size) + _ref_chunks(kc_ref, vc_ref, h, size), False)
                for h in heads]

    res = _attend_guarded(functools.reduce(jnp.maximum, bounds), streams)
    o_ref[...] = jnp.concatenate([_mla_finish(o) for o, _ in res], axis=0).T.astype(BF16)


def _normed_len(gain, dim):
    return (dim ** 0.5) * jnp.max(jnp.abs(gain.astype(F32)))


def _side_cast_specs(weights, n_steps):
    shapes = [wt.shape for wt in weights]
    assert n_steps >= len(_side_cast_plan(shapes)) + 2
    hbm = pl.BlockSpec(memory_space=pl.ANY)
    return ([hbm] * len(weights), [jax.ShapeDtypeStruct(s, BF16) for s in shapes], [hbm] * len(weights),
            _side_cast_scratch(shapes))


def _mla_lat_attn(q, k, v, kc, vc, bound, seq, past, weights):
    n = q.shape[1]
    tq = MLA_Q_TILE
    nq = seq // tq
    hs = MLA_STEP_HEADS
    grid = (n // seq, MLA_HEADS // hs, nq)
    w_in, w_shapes, w_out, w_scratch = _side_cast_specs(weights, grid[0] * grid[1] * grid[2])
    outs = pl.pallas_call(
        functools.partial(_mla_lat_kernel, len(weights)),
        out_shape=(jax.ShapeDtypeStruct((n, MLA_HEADS * MLA_V), BF16), *w_shapes),
        grid=grid,
        in_specs=[pl.BlockSpec(memory_space=pltpu.SMEM),
                  pl.BlockSpec((hs, tq, LANES), lambda b, h, i: (h, b * nq + i, 0)),
                  pl.BlockSpec((hs, seq, LANES), lambda b, h, i: (h, b, 0)),
                  pl.BlockSpec((hs, seq, LANES), lambda b, h, i: (h, b, 0)),
                  pl.BlockSpec((hs, past, LANES), lambda b, h, i: (h, b, 0)),
                  pl.BlockSpec((hs, past, LANES), lambda b, h, i: (h, b, 0))] + w_in,
        out_specs=(pl.BlockSpec((tq, hs * MLA_V), lambda b, h, i: (b * nq + i, h)), *w_out),
        scratch_shapes=w_scratch,
        compiler_params=_params(3), name="l0_mla_latent_attn",
    )(bound, q, k, v, kc, vc, *weights)
    return outs[0], outs[1:]


def _na_first_row(r, rows):
    return jnp.clip(r - NA_KH // 2, 0, rows - NA_KH)


def _na_lat_kernel(rows, n_cast, bound_ref, q_ref, k_ref, v_ref, kc_ref, vc_ref, t2_ref, *rest):
    o_ref = rest[n_cast]
    _side_cast(pl.program_id(0) * pl.num_programs(1) + pl.program_id(1),
               rest[:n_cast], rest[n_cast + 1:2 * n_cast + 1], rest[2 * n_cast + 1:])
    nq = NA_TILE_ROWS * GRID_W
    nk = NA_WIN_ROWS * GRID_W
    n_off = 2 * NA_KH - 1
    r0 = pl.program_id(1) * NA_TILE_ROWS
    w0 = jnp.clip(r0 - NA_KH // 2, 0, rows - NA_WIN_ROWS)

    def block_index(j, t):
        key_row = w0 + j
        i = r0 + 2 * t
        first_u, first_l = _na_first_row(i, rows), _na_first_row(i + 1, rows)
        in_u = (key_row >= first_u) & (key_row < first_u + NA_KH)
        in_l = (key_row >= first_l) & (key_row < first_l + NA_KH)
        d = jnp.clip(key_row - i + NA_KH - 1, 0, n_off - 1)
        return jnp.where(in_u & in_l, d, jnp.where(in_u, n_off, jnp.where(in_l, n_off + 1, n_off + 2)))

    idx = [[block_index(j, t) for t in range(NA_TILE_ROWS // 2)] for j in range(NA_WIN_ROWS)]

    def local_bias(s):
        heads = [jnp.concatenate([jnp.concatenate([t2_ref[2 * s + hh, i] for i in row], axis=1) for row in idx], axis=0)
                 for hh in range(2)]
        return jnp.concatenate(heads, axis=1)

    batch = pl.program_id(0)
    bounds = [bound_ref[batch, h] for h in range(NA_HEADS)]

    def cache_chunk(s):
        lanes = pl.ds(s * LANES, LANES)
        return [(lambda: kc_ref[:, lanes].astype(BF16), lambda: vc_ref[:, lanes].astype(BF16), None)]

    window = pl.ds(pl.multiple_of(w0 * GRID_W, GRID_W), nk)
    res = _attend_guarded(
        functools.reduce(jnp.maximum, bounds),
        lambda _: [(_na_queries(q_ref[s]),
                    [(lambda s=s: k_ref[s, window, :], lambda s=s: v_ref[s, window, :],
                      functools.partial(local_bias, s))] + cache_chunk(s),
                    True) for s in range(NA_SLABS)])
    for s in range(NA_SLABS):
        o_ref[:, s * LANES:(s + 1) * LANES] = _na_finish(*res[s]).T.astype(BF16)


def _na_lat_attn(q, k, v, kc, vc, t2, bound, seq, past, weights):
    slabs, n, _ = q.shape
    rows = seq // GRID_W
    tiles = rows // NA_TILE_ROWS
    nq = NA_TILE_ROWS * GRID_W
    nk = NA_WIN_ROWS * GRID_W

    win_spec = pl.BlockSpec((slabs, seq, LANES), lambda b, t: (0, b, 0), pipeline_mode=pl.Buffered(1))
    ctx_spec = pl.BlockSpec((past, slabs * LANES), lambda b, t: (b, 0))
    w_in, w_shapes, w_out, w_scratch = _side_cast_specs(weights, (n // seq) * tiles)
    outs = pl.pallas_call(
        functools.partial(_na_lat_kernel, rows, len(weights)),
        out_shape=(jax.ShapeDtypeStruct((n, slabs * LANES), BF16), *w_shapes),
        grid=(n // seq, tiles),
        in_specs=[pl.BlockSpec(memory_space=pltpu.SMEM),
                  pl.BlockSpec((slabs, nq, LANES), lambda b, t: (0, b * tiles + t, 0)),
                  win_spec, win_spec, ctx_spec, ctx_spec,
                  _const_spec(t2.shape)] + w_in,
        out_specs=(pl.BlockSpec((nq, slabs * LANES), lambda b, t: (b * tiles + t, 0)), *w_out),
        scratch_shapes=w_scratch,
        compiler_params=_params(2), name="l1_na_latent_attn",
    )(bound, q, k, v, kc, vc, t2, *weights)
    return outs[0], outs[1:]


def _pair_norm(x2, g2):
    low = lax.broadcasted_iota(jnp.int32, (1, LANES), 1) < NA_HEAD_DIM
    sq = x2 * x2
    ss_all = jnp.sum(sq, axis=-1, keepdims=True)
    ss_low = jnp.sum(jnp.where(low, sq, 0.0), axis=-1, keepdims=True)
    ss = jnp.where(low, ss_low, ss_all - ss_low)
    return x2 * lax.rsqrt(ss * (1.0 / NA_HEAD_DIM) + EPS) * g2


def _l1_pre_kernel(keep_f32, x_ref, mod_ref, nw_ref, win_ref, gq_ref, gk_ref, q_ref, k_ref, v_ref, *f32_refs):
    h = _modulate(x_ref[...], nw_ref[...], _ada_row(mod_ref, 0), _ada_row(mod_ref, 1)).astype(BF16)
    width = NA_SLABS * LANES
    q = _dot(h, win_ref[:, 0:width])
    k = _dot(h, win_ref[:, width:2 * width])
    v = _dot(h, win_ref[:, 2 * width:3 * width])
    gq = gq_ref[...]
    gk = gk_ref[...]
    for s in range(NA_SLABS):
        ls = slice(s * LANES, (s + 1) * LANES)
        q_ref[s] = _pair_norm(q[:, ls], gq).astype(BF16)
        kn = _pair_norm(k[:, ls], gk)
        k_ref[s] = kn.astype(BF16)
        v_ref[s] = v[:, ls].astype(BF16)
        if keep_f32:
            f32_refs[0][:, ls] = kn
    if keep_f32:
        f32_refs[1][...] = v


def _l1_pre(x, mod, tiles_per_group, keep_f32, w):
    n = x.shape[0]
    tm = TOKEN_TILE
    row = lambda i: (i, 0)
    width = NA_SLABS * LANES
    slab = jax.ShapeDtypeStruct((NA_SLABS, n, LANES), BF16)
    out_shape = [slab, slab, slab]
    out_specs = [_slab_spec(NA_SLABS, tm)] * 3
    if keep_f32:
        out_shape += [jax.ShapeDtypeStruct((n, width), F32)] * 2
        out_specs += [pl.BlockSpec((tm, width), row)] * 2
    return pl.pallas_call(
        functools.partial(_l1_pre_kernel, keep_f32),
        out_shape=tuple(out_shape), grid=(n // tm,),
        in_specs=[pl.BlockSpec((tm, D_MODEL), row),
                  _ada_spec(mod, tiles_per_group),
                  _const_spec((1, D_MODEL)),
                  _const_spec((D_MODEL, 3 * width)),
                  _const_spec((1, LANES)),
                  _const_spec((1, LANES))],
        out_specs=tuple(out_specs),
        compiler_params=_params(1), name="l1_pre_ctx" if keep_f32 else "l1_pre_latent",
    )(x, mod[0], w["norm_mix"], w["w_in"], w["gq"], w["gk"])


def _post_kernel(pool_seq, x_ref, attn_ref, *rest):
    rest = list(rest)
    take = lambda k: [rest.pop(0) for _ in range(k)]
    pool_in = take(8) if pool_seq is not None else None
    mod_ref, nw_ref, wo_ref, w1_ref, w2_ref, o_ref = take(6)
    pool_scratch = take(2) if pool_seq is not None else None
    step = pl.program_id(0)
    n_chunks = D_FF // FF_CHUNK
    if pool_seq is not None:
        first_refs, next_refs, (wp_ref, ps_ref) = pool_in[0:3], pool_in[3:6], pool_in[6:8]
        buf_ref, pooled_ref = pool_scratch

        @pl.when(step == 0)
        def _():
            for g, group in enumerate(_pool_mixer(pool_seq, 0, *first_refs, wp_ref, ps_ref, buf_ref)):
                pooled_ref[:, g * POOL_GROUP:(g + 1) * POOL_GROUP] = group()

        mixed = _dot(jnp.concatenate([pooled_ref[...], attn_ref[...]], axis=-1), wo_ref[...])
        next_groups = _pool_mixer(pool_seq, jnp.minimum(step + 1, pl.num_programs(0) - 1), *next_refs,
                                  wp_ref, ps_ref, buf_ref)
    else:
        mixed = _dot(attn_ref[...], wo_ref[...])
    x1 = x_ref[...] + _ada_row(mod_ref, 2) * mixed
    h = _modulate(x1, nw_ref[...], _ada_row(mod_ref, 3), _ada_row(mod_ref, 4)).astype(BF16)
    acc = jnp.zeros(x1.shape, F32)
    for c in range(n_chunks):
        cs = pl.ds(c * FF_CHUNK, FF_CHUNK)
        u = jnp.square(jnp.maximum(_dot(h, w1_ref[:, cs]), 0.0)).astype(BF16)
        acc = acc + _dot(u, w2_ref[cs, :])
        if pool_seq is not None:
            for g in range(c * len(next_groups) // n_chunks, (c + 1) * len(next_groups) // n_chunks):
                pooled_ref[:, g * POOL_GROUP:(g + 1) * POOL_GROUP] = next_groups[g]()
    o_ref[...] = x1 + _ada_row(mod_ref, 5) * acc


def _post(x, attn, mod, tiles_per_group, w, name, pool=None):
    n = x.shape[0]
    tm = TOKEN_TILE
    row = lambda i: (i, 0)
    in_specs = [pl.BlockSpec((tm, D_MODEL), row), pl.BlockSpec((tm, attn.shape[1]), row)]
    args = [x, attn]
    scratch = []
    pool_seq = None
    if pool is not None:
        a_in, pool_seq = pool
        hb = tm // POOL_HALO
        last = n // POOL_HALO - 1
        pieces = tm // min(tm, pool_seq)
        last_tile = n // tm - 1

        def tile_specs(tile_of):
            return [pl.BlockSpec((POOL_HALO, POOL_WIDTH), lambda i: (jnp.maximum(tile_of(i) * hb - 1, 0), 0)),
                    pl.BlockSpec((tm, POOL_WIDTH), lambda i: (tile_of(i), 0)),
                    pl.BlockSpec((POOL_HALO, POOL_WIDTH), lambda i: (jnp.minimum((tile_of(i) + 1) * hb, last), 0))]

        in_specs += tile_specs(lambda i: 0) + tile_specs(lambda i: jnp.minimum(i + 1, last_tile))
        in_specs += [_const_spec((len(POOL_WINDOWS), POOL_GROUP, POOL_GROUP)), _const_spec((1, POOL_WIDTH))]
        args += [a_in] * 6 + [w["w_pool"], w["pool_scale"]]
        scratch = [pltpu.VMEM((tm + 2 * POOL_HALO * pieces, POOL_WIDTH), F32), pltpu.VMEM((tm, POOL_WIDTH), BF16)]
    in_specs += [_ada_spec(mod, tiles_per_group),
                 _const_spec((1, D_MODEL)),
                 _const_spec((D_MODEL, D_MODEL)),
                 _const_spec((D_MODEL, D_FF)),
                 _const_spec((D_FF, D_MODEL))]
    args += [mod[0], w["norm_mlp"], w["w_out"], w["w_mlp1"], w["w_mlp2"]]
    return pl.pallas_call(
        functools.partial(_post_kernel, pool_seq),
        out_shape=jax.ShapeDtypeStruct((n, D_MODEL), F32), grid=(n // tm,),
        in_specs=in_specs, out_specs=pl.BlockSpec((tm, D_MODEL), row),
        scratch_shapes=scratch, compiler_params=_params(1), name=name,
    )(*args)


def _rope_tables(seq):
    assert seq == GRID_W * GRID_W
    half = MLA_ROPE // 2
    inv_freq = jnp.power(ROPE_THETA, -jnp.arange(0, half, 2, dtype=F32) / half)
    ang = jnp.arange(GRID_W, dtype=F32)[:, None] * inv_freq[None, :]
    cos, sin = jnp.cos(ang), jnp.sin(ang)
    fill = lambda n, v=0.0: jnp.full((GRID_W, n), v, F32)
    spare = LANES - MLA_QK
    return jnp.stack([
        jnp.concatenate([fill(MLA_NOPE, 1.0), fill(half), cos, cos, fill(spare)], axis=1),
        jnp.concatenate([fill(MLA_NOPE), cos, cos, fill(half), fill(spare)], axis=1),
        jnp.concatenate([fill(MLA_NOPE), fill(half), -sin, sin, fill(spare)], axis=1),
        jnp.concatenate([fill(MLA_NOPE), -sin, sin, fill(half), fill(spare)], axis=1)])


def _rope_partner(tail):
    shape = tail.shape
    t = tail.reshape(shape[:-1] + (2, 2, MLA_ROPE // 4))
    return t[..., ::-1, :].reshape(shape)


def _na_bias_table(rel_bias, rows):
    qcol = jnp.arange(GRID_W)
    kcol = jnp.arange(GRID_W)
    col_start = jnp.clip(qcol - NA_KW // 2, 0, GRID_W - NA_KW)
    col_mask = (kcol[:, None] >= col_start[None, :]) & (kcol[:, None] < col_start[None, :] + NA_KW)
    offs = jnp.arange(2 * NA_KW - 1)
    onehot = (kcol[None, :, None] - qcol[None, None, :] + NA_KW - 1 == offs[:, None, None]).astype(F32)
    blocks = jnp.einsum('hdj,jkq->hdkq', rel_bias.astype(F32) * LOG2E, onehot, precision=lax.Precision.HIGHEST)
    blocks = jnp.where(col_mask[None, None], blocks, NEG_INF)
    neg = jnp.full_like(blocks[:, :1], NEG_INF)
    both = jnp.concatenate([blocks, jnp.concatenate([neg, blocks[:, :-1]], axis=1)], axis=-1)
    d_upper, d_lower = _na_partial_offsets(rows)
    upper_only = jnp.concatenate([blocks[:, d_upper:d_upper + 1], neg], axis=-1)
    lower_only = jnp.concatenate([neg, blocks[:, d_lower - 1:d_lower]], axis=-1)
    return jnp.concatenate([both, upper_only, lower_only, jnp.concatenate([neg, neg], axis=-1)], axis=1)


def _na_partial_offsets(rows):
    first = lambda r: min(max(r - NA_KH // 2, 0), rows - NA_KH)
    upper, lower = set(), set()
    for r0 in range(0, rows, NA_TILE_ROWS):
        w0 = min(max(r0 - NA_KH // 2, 0), rows - NA_WIN_ROWS)
        for j in range(NA_WIN_ROWS):
            for t in range(NA_TILE_ROWS // 2):
                key_row, i = w0 + j, r0 + 2 * t
                in_u = first(i) <= key_row < first(i) + NA_KH
                in_l = first(i + 1) <= key_row < first(i + 1) + NA_KH
                d = key_row - i + NA_KH - 1
                if in_u and in_l:
                    assert 1 <= d <= 2 * NA_KH - 2
                elif in_u:
                    upper.add(d)
                elif in_l:
                    lower.add(d)
    assert len(upper) == 1 and len(lower) == 1, (upper, lower)
    return upper.pop(), lower.pop()


def _prep_l0(w_in, q_lora_norm, kv_lora_norm, w_q_up, w_kv_up, mla_q_norm, mla_k_norm, w_pool, pool_scale,
             w_out, norm_mix, norm_mlp, w_mlp1, w_mlp2):
    lat_end = POOL_WIDTH + Q_LORA + KV_LORA
    w_in_pad = jnp.concatenate([w_in[:, :lat_end], jnp.zeros((D_MODEL, MLA_NOPE), w_in.dtype),
                                w_in[:, lat_end:], _rope_partner(w_in[:, lat_end:])], axis=1)
    pad = LANES - MLA_QK
    hk = MLA_HEADS * LANES
    wq = w_q_up.reshape(Q_LORA, MLA_HEADS, MLA_QK)
    wq = jnp.concatenate([wq, _rope_partner(wq[:, :, MLA_NOPE:])], axis=-1)
    kv = w_kv_up.reshape(KV_LORA, MLA_HEADS, MLA_NOPE + MLA_V)
    wk = jnp.pad(kv[:, :, :MLA_NOPE], ((0, 0), (0, 0), (0, LANES - MLA_NOPE)))
    wv = jnp.pad(kv[:, :, MLA_NOPE:], ((0, 0), (0, 0), (0, LANES - MLA_V)))
    gq = mla_q_norm * (MLA_QK ** -0.5 * LOG2E)
    slab_lane = jnp.arange(2 * LANES)
    e2 = (slab_lane[:, None] // LANES == slab_lane[None, :] // LANES) & (slab_lane[:, None] % LANES < MLA_QK)
    return dict(
        norm_mix=norm_mix.reshape(1, -1), norm_mlp=norm_mlp.reshape(1, -1),
        w_in=w_in_pad.astype(BF16),
        qln=q_lora_norm.reshape(1, -1), kvn=kv_lora_norm.reshape(1, -1),
        wq=wq.reshape(Q_LORA, hk).astype(BF16),
        gq_plain=jnp.pad(gq, (0, pad)).reshape(1, LANES),
        gq_rope=jnp.concatenate([gq, _rope_partner(gq[MLA_NOPE:])]).reshape(1, LANES),
        wk=wk.reshape(KV_LORA, hk).astype(BF16),
        gk_plain=jnp.pad(mla_k_norm, (0, pad)).reshape(1, LANES),
        gk_rope=jnp.concatenate([mla_k_norm, _rope_partner(mla_k_norm[MLA_NOPE:])]).reshape(1, LANES),
        wv=wv.reshape(KV_LORA, hk).astype(BF16),
        e2=e2.astype(BF16),
        w_pool=w_pool.astype(BF16), pool_scale=pool_scale.reshape(1, -1),
        w_out=w_out, w_mlp1=w_mlp1, w_mlp2=w_mlp2)


def kernel(x_prompt, x_sample, cache_l0_mla_ckv, cache_l0_mla_krope, cache_l1_na_k, cache_l1_na_v, c, c_ctx, w_ada_l0, b_ada_l0, norm_mix_l0, norm_mlp_l0, w_mlp1_l0, w_mlp2_l0, w_in_l0, q_lora_norm_l0, kv_lora_norm_l0, w_q_up_l0, w_kv_up_l0, mla_q_norm_l0, mla_k_norm_l0, w_pool_l0, pool_scale_l0, w_out_l0, w_ada_l1, b_ada_l1, norm_mix_l1, norm_mlp_l1, w_mlp1_l1, w_mlp2_l1, w_in_l1, na_q_norm_l1, na_k_norm_l1, rel_bias_l1, w_out_l1):
    batch, seq, _ = x_prompt.shape
    dec_batch, dec_seq, _ = x_sample.shape
    past = cache_l0_mla_ckv.shape[1]
    xp = x_prompt.reshape(batch * seq, D_MODEL)
    xs = x_sample.reshape(dec_batch * dec_seq, D_MODEL)
    tiles_p = (batch * seq) // TOKEN_TILE
    tiles_s = dec_seq // TOKEN_TILE

    cond8 = jnp.concatenate([c_ctx[None, :], c, jnp.zeros((SUBLANES - 1 - dec_batch, D_MODEL), F32)], axis=0)
    mods = _ada(cond8, (w_ada_l0, w_ada_l1), (b_ada_l0, b_ada_l1))
    mods = mods.reshape(DEPTH, SUBLANES, 1, N_ADA * D_MODEL)
    (mod0_p, mod0_s), (mod1_p, mod1_s) = (((mods[l], 0), (mods[l], 1)) for l in range(DEPTH))

    w0 = _prep_l0(w_in_l0, q_lora_norm_l0, kv_lora_norm_l0, w_q_up_l0, w_kv_up_l0, mla_q_norm_l0, mla_k_norm_l0,
                  w_pool_l0, pool_scale_l0, w_out_l0, norm_mix_l0, norm_mlp_l0, w_mlp1_l0, w_mlp2_l0)
    a_p, q_p, k_p, v_p, ckv_p, kr_p = _l0_pre(xp, mod0_p, tiles_p, False, w0, None)
    a_s, q_s, k_s, v_s, _, _ = _l0_pre(xs, mod0_s, tiles_s, True, w0, _rope_tables(dec_seq))
    kr_cache = jnp.pad(cache_l0_mla_krope.reshape(dec_batch * past, MLA_ROPE), ((0, 0), (MLA_NOPE, LANES - MLA_QK)))
    k_c, v_c = _cache_kv(cache_l0_mla_ckv.reshape(dec_batch * past, KV_LORA), kr_cache, w0)
    mla_bound = _normed_len(mla_q_norm_l0 * (MLA_QK ** -0.5 * LOG2E), MLA_QK) * _normed_len(mla_k_norm_l0, MLA_QK)
    attn_p = _ctx_attn(_mla_ctx_kernel, q_p, k_p, v_p, mla_bound, seq, MLA_HEADS * MLA_V, "l0_mla_ctx_attn")
    attn_s, w16 = _mla_lat_attn(q_s, k_s, v_s, k_c, v_c, jnp.full((dec_batch, MLA_HEADS), mla_bound, F32),
                                dec_seq, past, (w0["w_out"], w0["w_mlp1"], w0["w_mlp2"], w_in_l1))
    w0 = dict(w0, w_out=w16[0], w_mlp1=w16[1], w_mlp2=w16[2])
    xp = _post(xp, attn_p, mod0_p, tiles_p, w0, "l0_post_ctx", pool=(a_p, seq))
    xs = _post(xs, attn_s, mod0_s, tiles_s, w0, "l0_post_latent", pool=(a_s, dec_seq))

    w1 = dict(norm_mix=norm_mix_l1.reshape(1, -1), norm_mlp=norm_mlp_l1.reshape(1, -1),
              w_in=w16[3],
              gq=jnp.tile(na_q_norm_l1 * (NA_HEAD_DIM ** -0.5 * LOG2E), 2).reshape(1, LANES),
              gk=jnp.tile(na_k_norm_l1, 2).reshape(1, LANES),
              w_out=w_out_l1, w_mlp1=w_mlp1_l1, w_mlp2=w_mlp2_l1)
    q1_p, k1_p, v1_p, k1_new, v1_new = _l1_pre(xp, mod1_p, tiles_p, True, w1)
    q1_s, k1_s, v1_s = _l1_pre(xs, mod1_s, tiles_s, False, w1)
    q_len, k_len = _normed_len(w1["gq"], NA_HEAD_DIM), _normed_len(w1["gk"], NA_HEAD_DIM)
    na_p = _ctx_attn(_na_ctx_kernel, q1_p, k1_p, v1_p, q_len * k_len, seq, NA_SLABS * LANES, "l1_na_ctx_attn")
    cache_len = jnp.sqrt(jnp.max(jnp.sum(jnp.square(cache_l1_na_k.astype(F32)), axis=-1), axis=1))
    na_bound = (q_len * jnp.maximum(k_len, cache_len)
                + jnp.max(jnp.abs(rel_bias_l1.astype(F32)) * LOG2E, axis=(1, 2))[None, :])
    na_s, w16 = _na_lat_attn(q1_s, k1_s, v1_s,
                             cache_l1_na_k.reshape(dec_batch * past, -1), cache_l1_na_v.reshape(dec_batch * past, -1),
                             _na_bias_table(rel_bias_l1, dec_seq // GRID_W), na_bound, dec_seq, past,
                             (w1["w_out"], w1["w_mlp1"], w1["w_mlp2"]))
    w1 = dict(w1, w_out=w16[0], w_mlp1=w16[1], w_mlp2=w16[2])
    xp = _post(xp, na_p, mod1_p, tiles_p, w1, "l1_post_ctx")
    xs = _post(xs, na_s, mod1_s, tiles_s, w1, "l1_post_latent")

    return (xp.reshape(batch, seq, D_MODEL), xs.reshape(dec_batch, dec_seq, D_MODEL),
            ckv_p.reshape(batch, seq, KV_LORA), kr_p.reshape(batch, seq, MLA_ROPE),
            k1_new.reshape(batch, seq, NA_HEADS, NA_HEAD_DIM), v1_new.reshape(batch, seq, NA_HEADS, NA_HEAD_DIM))
```
